```python
import math
import jax, jax.numpy as jnp
from jax import lax
import numpy as np

D_MODEL = 1024
BATCH = 32
SEQ = 2048
DEPTH = 1

N_MEM = 256
HEAD_DIM = 64
N_DIL_HEADS = 8
N_FOX_HEADS = 8
DIL_WIDTH = N_DIL_HEADS * HEAD_DIM
FOX_WIDTH = N_FOX_HEADS * HEAD_DIM
MIX_WIDTH = DIL_WIDTH + FOX_WIDTH
IN_WIDTH = 3 * DIL_WIDTH + 3 * FOX_WIDTH + N_FOX_HEADS
DIL_CONFIGS = ((128, 1), (512, 4), (2048, 16))
BLOCK = 128
N_XATTN_HEADS = 4
XATTN_HEAD_DIM = D_MODEL // N_XATTN_HEADS
D_FF = 4 * D_MODEL
EPS = 1e-6
NEG = -1e30

kernel_name = "hybrid_dilated_fox_block"


def _rmsnorm(x, g):
    x32 = x.astype(jnp.float32)
    y = x32 * lax.rsqrt(jnp.mean(x32 * x32, axis=-1, keepdims=True) + EPS) * g.astype(jnp.float32)
    return y.astype(x.dtype)


def _alibi_slopes(n):
    return 2.0 ** (-(jnp.arange(1, n + 1, dtype=jnp.float32) * (8.0 / n)))


def _dilated_branch(q, k, v, slopes, window, dilation):
    B, S, H, Dh = q.shape
    span = dilation * BLOCK
    s_pad = -(-S // span) * span
    pad = ((0, 0), (0, s_pad - S), (0, 0), (0, 0))
    q, k, v = (jnp.pad(a, pad) for a in (q, k, v))
    L = s_pad // dilation
    nb = L // BLOCK

    def to_res(a):
        return a.reshape(B, L, dilation, H, Dh).transpose(0, 2, 1, 3, 4).reshape(B, dilation, nb, BLOCK, H, Dh)

    def with_prev(a):
        prev = jnp.pad(a, ((0, 0), (0, 0), (1, 0), (0, 0), (0, 0), (0, 0)))[:, :, :-1]
        return jnp.concatenate([prev, a], axis=3)

    qr = to_res(q)
    kb = with_prev(to_res(k))
    vb = with_prev(to_res(v))

    steps = window // dilation
    qi = jnp.arange(BLOCK)[:, None]
    kj = jnp.arange(2 * BLOCK)[None, :]
    delta = qi + BLOCK - kj
    blk = jnp.arange(nb)[:, None, None]
    valid = (delta >= 0) & (delta <= steps) & ((blk > 0) | (kj >= BLOCK))
    dist = (delta * dilation).astype(jnp.float32)
    bias = jnp.where(valid[:, None], -slopes[None, :, None, None] * dist[None, None], NEG)

    scale = 1.0 / math.sqrt(Dh)
    s = jnp.einsum('brnqhd,brnkhd->brnhqk', qr, kb).astype(jnp.float32) * scale + bias[None, None]
    lse = jax.nn.logsumexp(s, axis=-1)
    p = jnp.exp(s - lse[..., None])
    o = jnp.einsum('brnhqk,brnkhd->brnqhd', p, vb.astype(jnp.float32))

    o = o.reshape(B, dilation, L, H, Dh).transpose(0, 2, 1, 3, 4).reshape(B, s_pad, H, Dh)[:, :S]
    lse = lse.transpose(0, 1, 2, 4, 3).reshape(B, dilation, L, H).transpose(0, 2, 1, 3).reshape(B, s_pad, H)[:, :S]
    return o, lse


def _dilated_attention(q, k, v):
    slopes = _alibi_slopes(q.shape[2])
    outs, lses = [], []
    for window, dilation in DIL_CONFIGS:
        o, lse = _dilated_branch(q, k, v, slopes, window, dilation)
        outs.append(o)
        lses.append(lse)
    w = jax.nn.softmax(jnp.stack(lses, axis=0), axis=0)
    o = jnp.sum(w[..., None] * jnp.stack(outs, axis=0), axis=0)
    return o.astype(q.dtype)


def _forgetting_attention(q, k, v, log_f):
    B, S, H, Dh = q.shape
    nb = S // BLOCK
    c = jnp.cumsum(log_f, axis=1).transpose(0, 2, 1)
    qb = q.reshape(B, nb, BLOCK, H, Dh).transpose(1, 0, 2, 3, 4)
    cq = c.reshape(B, H, nb, BLOCK).transpose(2, 0, 1, 3)
    kpos = jnp.arange(S)
    scale = 1.0 / math.sqrt(Dh)

    def one_block(args):
        qblk, cblk, n = args
        s = jnp.einsum('bqhd,bkhd->bhqk', qblk, k).astype(jnp.float32) * scale
        s = s + (cblk[..., :, None] - c[:, :, None, :])
        qpos = n * BLOCK + jnp.arange(BLOCK)
        s = jnp.where((kpos[None, :] <= qpos[:, None])[None, None], s, NEG)
        p = jax.nn.softmax(s, axis=-1)
        return jnp.einsum('bhqk,bkhd->bqhd', p.astype(v.dtype), v)

    o = lax.map(one_block, (qb, cq, jnp.arange(nb)))
    return o.transpose(1, 0, 2, 3, 4).reshape(B, S, H, Dh)


def _hybrid_mixer(h, w_in, b_forget, w_out):
    B, S, _ = h.shape
    z = h @ w_in
    o0 = 3 * DIL_WIDTH
    o1 = o0 + 3 * FOX_WIDTH
    qa, ka, va = jnp.split(z[..., :o0], 3, axis=-1)
    qf, kf, vf = jnp.split(z[..., o0:o1], 3, axis=-1)
    gate = z[..., o1:]
    shp_a = (B, S, N_DIL_HEADS, HEAD_DIM)
    shp_f = (B, S, N_FOX_HEADS, HEAD_DIM)
    ya = _dilated_attention(qa.reshape(shp_a), ka.reshape(shp_a), va.reshape(shp_a))
    log_f = jax.nn.log_sigmoid(gate.astype(jnp.float32) + b_forget.astype(jnp.float32))
    yf = _forgetting_attention(qf.reshape(shp_f), kf.reshape(shp_f), vf.reshape(shp_f), log_f)
    y = jnp.concatenate([ya.reshape(B, S, DIL_WIDTH), yf.reshape(B, S, FOX_WIDTH).astype(ya.dtype)], axis=-1)
    return y @ w_out


def _cross_attention(h, m, w_xq, w_xk, w_xv, w_xo):
    B, S, _ = h.shape
    M = m.shape[1]
    q = (h @ w_xq).reshape(B, S, N_XATTN_HEADS, XATTN_HEAD_DIM)
    k = (m @ w_xk).reshape(B, M, N_XATTN_HEADS, XATTN_HEAD_DIM)
    v = (m @ w_xv).reshape(B, M, N_XATTN_HEADS, XATTN_HEAD_DIM)
    s = jnp.einsum('bqhd,bkhd->bhqk', q, k).astype(jnp.float32) / math.sqrt(XATTN_HEAD_DIM)
    p = jax.nn.softmax(s, axis=-1)
    o = jnp.einsum('bhqk,bkhd->bqhd', p.astype(v.dtype), v).reshape(B, S, N_XATTN_HEADS * XATTN_HEAD_DIM)
    return o @ w_xo


def _sq_relu_mlp(h, w_up, w_down):
    a = jax.nn.relu(h @ w_up)
    return (a * a) @ w_down


def _fwd_setup_inputs(seed: int = 0) -> dict:
    key = jax.random.key(seed)
    ks = jax.random.split(key, 16)
    f32 = jnp.float32

    def w(k, shape):
        return jax.random.normal(k, shape, f32) * shape[0] ** -0.5

    def gain(k):
        return 1.0 + 0.05 * jax.random.normal(k, (D_MODEL,), f32)

    return {
        "x": jax.random.normal(ks[0], (BATCH, SEQ, D_MODEL), f32),
        "mem": jax.random.normal(ks[1], (BATCH, N_MEM, D_MODEL), f32),
        "g_mix": gain(ks[2]),
        "w_in": w(ks[3], (D_MODEL, IN_WIDTH)),
        "b_forget": 0.1 * jax.random.normal(ks[4], (N_FOX_HEADS,), f32),
        "w_out": w(ks[5], (MIX_WIDTH, D_MODEL)),
        "g_xattn": gain(ks[6]),
        "g_mem": gain(ks[7]),
        "w_xq": w(ks[8], (D_MODEL, N_XATTN_HEADS * XATTN_HEAD_DIM)),
        "w_xk": w(ks[9], (D_MODEL, N_XATTN_HEADS * XATTN_HEAD_DIM)),
        "w_xv": w(ks[10], (D_MODEL, N_XATTN_HEADS * XATTN_HEAD_DIM)),
        "w_xo": w(ks[11], (N_XATTN_HEADS * XATTN_HEAD_DIM, D_MODEL)),
        "g_mlp": gain(ks[12]),
        "w_up": w(ks[13], (D_MODEL, D_FF)),
        "w_down": w(ks[14], (D_FF, D_MODEL)),
        "g_final": gain(ks[15]),
    }


def _fwd_reference(x, mem, g_mix, w_in, b_forget, w_out, g_xattn, g_mem, w_xq, w_xk, w_xv, w_xo,
              g_mlp, w_up, w_down, g_final):
    m = _rmsnorm(mem, g_mem)
    for _ in range(DEPTH):
        x = x + _hybrid_mixer(_rmsnorm(x, g_mix), w_in, b_forget, w_out)
        x = x + _cross_attention(_rmsnorm(x, g_xattn), m, w_xq, w_xk, w_xv, w_xo)
        x = x + _sq_relu_mlp(_rmsnorm(x, g_mlp), w_up, w_down)
    return _rmsnorm(x, g_final)


import jax as _jax
import jax.numpy as _jnp

TWIN_FORMAT = 'train_step'
FWD_PARAMS = ['x', 'mem', 'g_mix', 'w_in', 'b_forget', 'w_out', 'g_xattn', 'g_mem', 'w_xq', 'w_xk', 'w_xv', 'w_xo', 'g_mlp', 'w_up', 'w_down', 'g_final']
TWIN_WEIGHTS = ['g_mix', 'w_in', 'b_forget', 'w_out', 'g_xattn', 'g_mem', 'w_xq', 'w_xk', 'w_xv', 'w_xo', 'g_mlp', 'w_up', 'w_down', 'g_final']
TWIN_DIFF_INPUT = 'x'
TWIN_INPUTS = ['x', 'mem', 'g_mix', 'w_in', 'b_forget', 'w_out', 'g_xattn', 'g_mem', 'w_xq', 'w_xk', 'w_xv', 'w_xo', 'g_mlp', 'w_up', 'w_down', 'g_final', 'loss_target', 'm_g_mix', 'm_w_in', 'm_b_forget', 'm_w_out', 'm_g_xattn', 'm_g_mem', 'm_w_xq', 'm_w_xk', 'm_w_xv', 'm_w_xo', 'm_g_mlp', 'm_w_up', 'm_w_down', 'm_g_final', 'v_g_mix', 'v_w_in', 'v_b_forget', 'v_w_out', 'v_g_xattn', 'v_g_mem', 'v_w_xq', 'v_w_xk', 'v_w_xv', 'v_w_xo', 'v_g_mlp', 'v_w_up', 'v_w_down', 'v_g_final']
TWIN_OUTPUTS = ['loss', 'grad_x', 'grad_g_mix', 'grad_w_in', 'grad_b_forget', 'grad_w_out', 'grad_g_xattn', 'grad_g_mem', 'grad_w_xq', 'grad_w_xk', 'grad_w_xv', 'grad_w_xo', 'grad_g_mlp', 'grad_w_up', 'grad_w_down', 'grad_g_final', 'delta_g_mix', 'delta_w_in', 'delta_b_forget', 'delta_w_out', 'delta_g_xattn', 'delta_g_mem', 'delta_w_xq', 'delta_w_xk', 'delta_w_xv', 'delta_w_xo', 'delta_g_mlp', 'delta_w_up', 'delta_w_down', 'delta_g_final', 'new_m_g_mix', 'new_m_w_in', 'new_m_b_forget', 'new_m_w_out', 'new_m_g_xattn', 'new_m_g_mem', 'new_m_w_xq', 'new_m_w_xk', 'new_m_w_xv', 'new_m_w_xo', 'new_m_g_mlp', 'new_m_w_up', 'new_m_w_down', 'new_m_g_final', 'new_v_g_mix', 'new_v_w_in', 'new_v_b_forget', 'new_v_w_out', 'new_v_g_xattn', 'new_v_g_mem', 'new_v_w_xq', 'new_v_w_xk', 'new_v_w_xv', 'new_v_w_xo', 'new_v_g_mlp', 'new_v_w_up', 'new_v_w_down', 'new_v_g_final']
TWIN_LEAF_KINDS = {'loss': 'loss', 'grad_x': 'grad_x', 'grad_g_mix': 'grad_w', 'grad_w_in': 'grad_w', 'grad_b_forget': 'grad_w', 'grad_w_out': 'grad_w', 'grad_g_xattn': 'grad_w', 'grad_g_mem': 'grad_w', 'grad_w_xq': 'grad_w', 'grad_w_xk': 'grad_w', 'grad_w_xv': 'grad_w', 'grad_w_xo': 'grad_w', 'grad_g_mlp': 'grad_w', 'grad_w_up': 'grad_w', 'grad_w_down': 'grad_w', 'grad_g_final': 'grad_w', 'delta_g_mix': 'delta_w', 'delta_w_in': 'delta_w', 'delta_b_forget': 'delta_w', 'delta_w_out': 'delta_w', 'delta_g_xattn': 'delta_w', 'delta_g_mem': 'delta_w', 'delta_w_xq': 'delta_w', 'delta_w_xk': 'delta_w', 'delta_w_xv': 'delta_w', 'delta_w_xo': 'delta_w', 'delta_g_mlp': 'delta_w', 'delta_w_up': 'delta_w', 'delta_w_down': 'delta_w', 'delta_g_final': 'delta_w', 'new_m_g_mix': 'new_m', 'new_m_w_in': 'new_m', 'new_m_b_forget': 'new_m', 'new_m_w_out': 'new_m', 'new_m_g_xattn': 'new_m', 'new_m_g_mem': 'new_m', 'new_m_w_xq': 'new_m', 'new_m_w_xk': 'new_m', 'new_m_w_xv': 'new_m', 'new_m_w_xo': 'new_m', 'new_m_g_mlp': 'new_m', 'new_m_w_up': 'new_m', 'new_m_w_down': 'new_m', 'new_m_g_final': 'new_m', 'new_v_g_mix': 'new_v', 'new_v_w_in': 'new_v', 'new_v_b_forget': 'new_v', 'new_v_w_out': 'new_v', 'new_v_g_xattn': 'new_v', 'new_v_g_mem': 'new_v', 'new_v_w_xq': 'new_v', 'new_v_w_xk': 'new_v', 'new_v_w_xv': 'new_v', 'new_v_w_xo': 'new_v', 'new_v_g_mlp': 'new_v', 'new_v_w_up': 'new_v', 'new_v_w_down': 'new_v', 'new_v_g_final': 'new_v'}


def _forward(args):
    return _fwd_reference(*[args[k] for k in FWD_PARAMS])


def _output_shape():
    out = _jax.eval_shape(lambda: _forward(_fwd_setup_inputs(0)))
    return out.shape, out.dtype

N_MICROBATCH = 1
ADAM_LR = 0.001
ADAM_B1 = 0.9
ADAM_B2 = 0.999
ADAM_EPS = 1e-08
ADAM_WD = 0.01
ADAM_STEP = 10
PER_EXAMPLE_BATCH_AXIS = {'x': 0, 'mem': 0, 'loss_target': 0}
SHARED_INPUTS = []
_WEIGHT_DTYPES = {'g_mix': _jnp.float32, 'w_in': _jnp.float32, 'b_forget': _jnp.float32, 'w_out': _jnp.float32, 'g_xattn': _jnp.float32, 'g_mem': _jnp.float32, 'w_xq': _jnp.float32, 'w_xk': _jnp.float32, 'w_xv': _jnp.float32, 'w_xo': _jnp.float32, 'g_mlp': _jnp.float32, 'w_up': _jnp.float32, 'w_down': _jnp.float32, 'g_final': _jnp.float32}
MOMENT_SCALE = {'g_mix': 1.696494e-01, 'w_in': 9.913452e-02, 'b_forget': 8.738445e-01, 'w_out': 1.317166e-01, 'g_xattn': 2.653664e-02, 'g_mem': 4.019712e-02, 'w_xq': 2.509948e-02, 'w_xk': 2.514435e-02, 'w_xv': 3.031717e-02, 'w_xo': 2.990252e-02, 'g_mlp': 2.317749e-01, 'w_up': 1.089154e-01, 'w_down': 3.280071e-01, 'g_final': 6.431468e+01}


def _to_microbatches(a, axis):
    t = _jnp.moveaxis(a, axis, 0)
    t = t.reshape((N_MICROBATCH, t.shape[0] // N_MICROBATCH) + t.shape[1:])
    return _jnp.moveaxis(t, 1, axis + 1)


def setup_inputs(seed: int = 0) -> dict:
    inp = _fwd_setup_inputs(seed)
    key = _jax.random.fold_in(_jax.random.key(seed), 7919)
    shape, _ = _output_shape()
    out = dict(inp)
    out["loss_target"] = _jax.random.normal(_jax.random.fold_in(key, 0), shape, _jnp.float32)
    for i, name in enumerate(TWIN_WEIGHTS):
        w = inp[name].astype(_jnp.float32)
        if MOMENT_SCALE is None:
            s = _jnp.sqrt(_jnp.mean(_jnp.square(w)) + 1e-30)
        else:
            s = MOMENT_SCALE[name]
        km, kv = _jax.random.split(_jax.random.fold_in(key, i + 1))
        out[name] = w
        out["m_" + name] = s * _jax.random.normal(km, w.shape, _jnp.float32)
        out["v_" + name] = (s * s) * _jax.random.uniform(kv, w.shape, _jnp.float32, 0.5, 1.5)
    if N_MICROBATCH > 1:
        for name, axis in PER_EXAMPLE_BATCH_AXIS.items():
            out[name] = _to_microbatches(out[name], axis)
    return {'x': out['x'], 'mem': out['mem'], 'g_mix': out['g_mix'], 'w_in': out['w_in'], 'b_forget': out['b_forget'], 'w_out': out['w_out'], 'g_xattn': out['g_xattn'], 'g_mem': out['g_mem'], 'w_xq': out['w_xq'], 'w_xk': out['w_xk'], 'w_xv': out['w_xv'], 'w_xo': out['w_xo'], 'g_mlp': out['g_mlp'], 'w_up': out['w_up'], 'w_down': out['w_down'], 'g_final': out['g_final'], 'loss_target': out['loss_target'], 'm_g_mix': out['m_g_mix'], 'm_w_in': out['m_w_in'], 'm_b_forget': out['m_b_forget'], 'm_w_out': out['m_w_out'], 'm_g_xattn': out['m_g_xattn'], 'm_g_mem': out['m_g_mem'], 'm_w_xq': out['m_w_xq'], 'm_w_xk': out['m_w_xk'], 'm_w_xv': out['m_w_xv'], 'm_w_xo': out['m_w_xo'], 'm_g_mlp': out['m_g_mlp'], 'm_w_up': out['m_w_up'], 'm_w_down': out['m_w_down'], 'm_g_final': out['m_g_final'], 'v_g_mix': out['v_g_mix'], 'v_w_in': out['v_w_in'], 'v_b_forget': out['v_b_forget'], 'v_w_out': out['v_w_out'], 'v_g_xattn': out['v_g_xattn'], 'v_g_mem': out['v_g_mem'], 'v_w_xq': out['v_w_xq'], 'v_w_xk': out['v_w_xk'], 'v_w_xv': out['v_w_xv'], 'v_w_xo': out['v_w_xo'], 'v_g_mlp': out['v_g_mlp'], 'v_w_up': out['v_w_up'], 'v_w_down': out['v_w_down'], 'v_g_final': out['v_g_final']}


def _loss(weights, diff, rest, loss_target):
    with _jax.named_scope("forward"):
        args = {**rest, TWIN_DIFF_INPUT: diff, **{k: w.astype(_WEIGHT_DTYPES[k]) for k, w in weights.items()}}
        y = _forward(args)
    with _jax.named_scope("loss_head"):
        err = _jnp.square(y.astype(_jnp.float32) - loss_target)
        return 0.5 * _jnp.sum(_jnp.mean(err, axis=-1)) if err.ndim else 0.5 * err


def _adamw(w, g, m, v):
    m = ADAM_B1 * m + (1.0 - ADAM_B1) * g
    v = ADAM_B2 * v + (1.0 - ADAM_B2) * _jnp.square(g)
    m_hat = m / (1.0 - ADAM_B1 ** ADAM_STEP)
    v_hat = v / (1.0 - ADAM_B2 ** ADAM_STEP)
    delta = -ADAM_LR * (m_hat / (_jnp.sqrt(v_hat) + ADAM_EPS) + ADAM_WD * w)
    return delta, m, v


def reference(x, mem, g_mix, w_in, b_forget, w_out, g_xattn, g_mem, w_xq, w_xk, w_xv, w_xo, g_mlp, w_up, w_down, g_final, loss_target, m_g_mix, m_w_in, m_b_forget, m_w_out, m_g_xattn, m_g_mem, m_w_xq, m_w_xk, m_w_xv, m_w_xo, m_g_mlp, m_w_up, m_w_down, m_g_final, v_g_mix, v_w_in, v_b_forget, v_w_out, v_g_xattn, v_g_mem, v_w_xq, v_w_xk, v_w_xv, v_w_xo, v_g_mlp, v_w_up, v_w_down, v_g_final):
    given = dict(x=x, mem=mem, g_mix=g_mix, w_in=w_in, b_forget=b_forget, w_out=w_out, g_xattn=g_xattn, g_mem=g_mem, w_xq=w_xq, w_xk=w_xk, w_xv=w_xv, w_xo=w_xo, g_mlp=g_mlp, w_up=w_up, w_down=w_down, g_final=g_final, loss_target=loss_target, m_g_mix=m_g_mix, m_w_in=m_w_in, m_b_forget=m_b_forget, m_w_out=m_w_out, m_g_xattn=m_g_xattn, m_g_mem=m_g_mem, m_w_xq=m_w_xq, m_w_xk=m_w_xk, m_w_xv=m_w_xv, m_w_xo=m_w_xo, m_g_mlp=m_g_mlp, m_w_up=m_w_up, m_w_down=m_w_down, m_g_final=m_g_final, v_g_mix=v_g_mix, v_w_in=v_w_in, v_b_forget=v_b_forget, v_w_out=v_w_out, v_g_xattn=v_g_xattn, v_g_mem=v_g_mem, v_w_xq=v_w_xq, v_w_xk=v_w_xk, v_w_xv=v_w_xv, v_w_xo=v_w_xo, v_g_mlp=v_g_mlp, v_w_up=v_w_up, v_w_down=v_w_down, v_g_final=v_g_final)
    weights = {n: given[n] for n in TWIN_WEIGHTS}
    shared = {n: given[n] for n in SHARED_INPUTS}
    per_example = {n: given[n] for n in ['x', 'mem']}
    grad_fn = _jax.value_and_grad(_loss, argnums=(0, 1))

    def one_microbatch(ex, loss_target):
        ex = dict(ex)
        diff = ex.pop(TWIN_DIFF_INPUT)
        return grad_fn(weights, diff, {**shared, **ex}, loss_target)

    if N_MICROBATCH == 1:
        loss, (grad_w, grad_x) = one_microbatch(per_example, given["loss_target"])
    else:
        def body(carry, xs):
            loss_sum, grad_sum = carry
            l_k, (gw_k, gx_k) = one_microbatch(xs[0], xs[1])
            with _jax.named_scope("update"):
                return (loss_sum + l_k, _jax.tree.map(_jnp.add, grad_sum, gw_k)), gx_k

        init = (_jnp.zeros((), _jnp.float32), _jax.tree.map(_jnp.zeros_like, weights))
        (loss, grad_w), grad_x = _jax.lax.scan(body, init, (per_example, given["loss_target"]))
    with _jax.named_scope("update"):
        delta_w, new_m, new_v = {}, {}, {}
        for n in TWIN_WEIGHTS:
            delta_w[n], new_m[n], new_v[n] = _adamw(weights[n], grad_w[n], given["m_" + n], given["v_" + n])
    return (loss, grad_x, *[grad_w[n] for n in TWIN_WEIGHTS], *[delta_w[n] for n in TWIN_WEIGHTS],
            *[new_m[n] for n in TWIN_WEIGHTS], *[new_v[n] for n in TWIN_WEIGHTS])
```

```python
import jax
import jax.numpy as jnp
from jax import lax
from jax.experimental import pallas as pl
from jax.experimental.pallas import tpu as pltpu

F32, BF16 = jnp.float32, jnp.bfloat16
SDS = jax.ShapeDtypeStruct

D_MODEL = 1024
HEAD_DIM = 64
WIDTH = 512
QKV_W = 6 * WIDTH
IN_W = QKV_W + 8
IN_PAD = QKV_W + 128
BLOCK = 128
DIL_CONFIGS = ((128, 1), (512, 4), (2048, 16))
N_XH, XHD = 4, 256
D_FF = 4096
EPS = 1e-6
NEG = -1e30
N_DEV = 8
AXES = ("x", "y", "c")

ADAM_LR, ADAM_B1, ADAM_B2, ADAM_EPS, ADAM_WD, ADAM_STEP = 0.001, 0.9, 0.999, 1e-08, 0.01, 10

VMEM_CAP_V7X = 64 * 1024 * 1024
VMEM_LIMIT = VMEM_CAP_V7X * 7 // 8

NT = (((1,), (1,)), ((), ()))
TN = (((0,), (0,)), ((), ()))


def _cp(**kw):
    return pltpu.CompilerParams(vmem_limit_bytes=VMEM_LIMIT, **kw)


def _dot(a, b, dims=None):
    if dims is None:
        return jnp.dot(a, b, preferred_element_type=F32)
    return lax.dot_general(a, b, dims, preferred_element_type=F32)


def _rstd(xv):
    return lax.rsqrt(jnp.mean(xv * xv, axis=-1, keepdims=True) + EPS)


def _rms_bwd(dh, xv, g):
    r = _rstd(xv)
    xhat = xv * r
    dxhat = dh * g
    dx = r * (dxhat - xhat * jnp.mean(dxhat * xhat, axis=-1, keepdims=True))
    return dx, jnp.sum(dh * xhat, axis=0, keepdims=True)


def _rms_matmul(x, g, w, *, tm, tn, out_dtype, relu=False, name):
    T, D = x.shape
    N = w.shape[1]

    def body(x_ref, g_ref, w_ref, h_ref, o_ref, h_s):
        @pl.when(pl.program_id(1) == 0)
        def _():
            xv = x_ref[...]
            h = (xv * _rstd(xv) * g_ref[...]).astype(BF16)
            h_s[...] = h
            h_ref[...] = h

        acc = _dot(h_s[...], w_ref[...])
        if relu:
            acc = jnp.maximum(acc, 0.0)
        o_ref[...] = acc.astype(out_dtype)

    return pl.pallas_call(
        body, grid=(T // tm, N // tn),
        in_specs=[pl.BlockSpec((tm, D), lambda i, j: (i, 0)), pl.BlockSpec((1, D), lambda i, j: (0, 0)),
                  pl.BlockSpec((D, tn), lambda i, j: (0, j))],
        out_specs=[pl.BlockSpec((tm, D), lambda i, j: (i, 0)), pl.BlockSpec((tm, tn), lambda i, j: (i, j))],
        out_shape=[SDS((T, D), BF16), SDS((T, N), out_dtype)],
        scratch_shapes=[pltpu.VMEM((tm, D), BF16)], compiler_params=_cp(), name=name,
    )(x, g.reshape(1, D), w)


def _matmul_nn(a, w, *, res=None, square=False, tm, tn, tk, out_dtype, name):
    T, K = a.shape
    N = w.shape[1]
    nk = K // tk

    def body(*refs):
        a_ref, w_ref = refs[0], refs[1]
        res_ref = refs[2] if res is not None else None
        o_ref, acc = refs[-2], refs[-1]
        k = pl.program_id(2)

        @pl.when(k == 0)
        def _():
            acc[...] = jnp.zeros_like(acc)

        av = a_ref[...]
        if square:
            af = av.astype(F32)
            av = (af * af).astype(BF16)
        acc[...] += _dot(av, w_ref[...])

        @pl.when(k == nk - 1)
        def _():
            r = acc[...]
            if res_ref is not None:
                r = res_ref[...] + r
            o_ref[...] = r.astype(out_dtype)

    in_specs = [pl.BlockSpec((tm, tk), lambda i, j, k: (i, k)), pl.BlockSpec((tk, tn), lambda i, j, k: (k, j))]
    args = [a, w]
    if res is not None:
        in_specs.append(pl.BlockSpec((tm, tn), lambda i, j, k: (i, j)))
        args.append(res)
    return pl.pallas_call(
        body, grid=(T // tm, N // tn, nk), in_specs=in_specs,
        out_specs=pl.BlockSpec((tm, tn), lambda i, j, k: (i, j)), out_shape=SDS((T, N), out_dtype),
        scratch_shapes=[pltpu.VMEM((tm, tn), F32)], compiler_params=_cp(), name=name,
    )(*args)


def _matmul_nt(g, w, *, mul2a=None, tm, tn, name):
    T, K = g.shape
    N = w.shape[0]

    def body(*refs):
        g_ref, w_ref = refs[0], refs[1]
        o_ref = refs[-1]
        acc = _dot(g_ref[...].astype(BF16), w_ref[...], NT)
        if mul2a is not None:
            acc = acc * (2.0 * refs[2][...].astype(F32))
        o_ref[...] = acc.astype(BF16)

    in_specs = [pl.BlockSpec((tm, K), lambda i, j: (i, 0)), pl.BlockSpec((tn, K), lambda i, j: (j, 0))]
    args = [g, w]
    if mul2a is not None:
        in_specs.append(pl.BlockSpec((tm, tn), lambda i, j: (i, j)))
        args.append(mul2a)
    return pl.pallas_call(
        body, grid=(T // tm, N // tn), in_specs=in_specs,
        out_specs=pl.BlockSpec((tm, tn), lambda i, j: (i, j)), out_shape=SDS((T, N), BF16),
        compiler_params=_cp(), name=name,
    )(*args)


def _matmul_nt_rms(g, w, x, gain, dres, *, tm, tk, name):
    T, K = g.shape
    D = w.shape[0]
    nk = K // tk
    nt = T // tm

    def body(*refs):
        g_ref, w_ref, x_ref, gain_ref = refs[:4]
        dres_ref = refs[4] if dres is not None else None
        dx_ref, dg_ref, acc = refs[-3], refs[-2], refs[-1]
        i, k = pl.program_id(0), pl.program_id(1)

        @pl.when(k == 0)
        def _():
            acc[...] = jnp.zeros_like(acc)

        acc[...] += _dot(g_ref[...].astype(BF16), w_ref[...], NT)

        @pl.when(k == nk - 1)
        def _():
            dx, dg = _rms_bwd(acc[...], x_ref[...], gain_ref[...])
            if dres_ref is not None:
                dx = dres_ref[...] + dx
            dx_ref[...] = dx

            @pl.when(i == 0)
            def _():
                dg_ref[...] = dg

            @pl.when(i > 0)
            def _():
                dg_ref[...] += dg

    in_specs = [pl.BlockSpec((tm, tk), lambda i, k: (i, k)), pl.BlockSpec((D, tk), lambda i, k: (0, k)),
                pl.BlockSpec((tm, D), lambda i, k: (i, 0)), pl.BlockSpec((1, D), lambda i, k: (0, 0))]
    args = [g, w, x, gain.reshape(1, D)]
    if dres is not None:
        in_specs.append(pl.BlockSpec((tm, D), lambda i, k: (i, 0)))
        args.append(dres)
    return pl.pallas_call(
        body, grid=(nt, nk), in_specs=in_specs,
        out_specs=[pl.BlockSpec((tm, D), lambda i, k: (i, 0)), pl.BlockSpec((1, D), lambda i, k: (0, 0))],
        out_shape=[SDS((T, D), F32), SDS((1, D), F32)],
        scratch_shapes=[pltpu.VMEM((tm, D), F32)], compiler_params=_cp(), name=name,
    )(*args)


def _matmul_tn(a, g, *, square=False, bk, bn, tt, out_dtype, name):
    T, K = a.shape
    N = g.shape[1]
    nt = T // tt

    def body(a_ref, g_ref, o_ref, acc):
        t = pl.program_id(2)

        @pl.when(t == 0)
        def _():
            acc[...] = jnp.zeros_like(acc)

        av = a_ref[...]
        if square:
            af = av.astype(F32)
            av = (af * af).astype(BF16)
        acc[...] += _dot(av, g_ref[...].astype(BF16), TN)

        @pl.when(t == nt - 1)
        def _():
            o_ref[...] = acc[...].astype(out_dtype)

    return pl.pallas_call(
        body, grid=(K // bk, N // bn, nt),
        in_specs=[pl.BlockSpec((tt, bk), lambda i, j, t: (t, i)), pl.BlockSpec((tt, bn), lambda i, j, t: (t, j))],
        out_specs=pl.BlockSpec((bk, bn), lambda i, j, t: (i, j)), out_shape=SDS((K, N), out_dtype),
        scratch_shapes=[pltpu.VMEM((bk, bn), F32)], compiler_params=_cp(), name=name,
    )(a, g)


def _loss_head(x3, g_final, target, *, tm, name):
    T, D = x3.shape

    def body(x_ref, g_ref, t_ref, dx_ref, dg_ref, loss_ref):
        i = pl.program_id(0)
        xv, g = x_ref[...], g_ref[...]
        r = _rstd(xv)
        xhat = xv * r
        diff = xhat * g - t_ref[...]
        part = 0.5 * jnp.sum(jnp.mean(diff * diff, axis=-1, keepdims=True), axis=0, keepdims=True)
        dy = diff * (1.0 / D)
        dxhat = dy * g
        dx_ref[...] = r * (dxhat - xhat * jnp.mean(dxhat * xhat, axis=-1, keepdims=True))
        dg = jnp.sum(dy * xhat, axis=0, keepdims=True)
        lp = jnp.broadcast_to(part, loss_ref.shape)

        @pl.when(i == 0)
        def _():
            dg_ref[...] = dg
            loss_ref[...] = lp

        @pl.when(i > 0)
        def _():
            dg_ref[...] += dg
            loss_ref[...] += lp

    return pl.pallas_call(
        body, grid=(T // tm,),
        in_specs=[pl.BlockSpec((tm, D), lambda i: (i, 0)), pl.BlockSpec((1, D), lambda i: (0, 0)),
                  pl.BlockSpec((tm, D), lambda i: (i, 0))],
        out_specs=[pl.BlockSpec((tm, D), lambda i: (i, 0)), pl.BlockSpec((1, D), lambda i: (0, 0)),
                   pl.BlockSpec((8, 128), lambda i: (0, 0))],
        out_shape=[SDS((T, D), F32), SDS((1, D), F32), SDS((8, 128), F32)],
        compiler_params=_cp(), name=name,
    )(x3, g_final.reshape(1, D), target)


def _head_lanes(shape, width):
    return lax.broadcasted_iota(jnp.int32, shape, len(shape) - 1) // width


def _gate_fwd(gate, b_pad, *, B, S, name):
    def body(g_ref, b_ref, c_ref, cc_ref):
        xv = g_ref[...] + b_ref[...]
        lf = jnp.minimum(xv, 0.0) - jnp.log(1.0 + jnp.exp(-jnp.abs(xv)))
        lane = lax.broadcasted_iota(jnp.int32, lf.shape, 1)
        row = lax.broadcasted_iota(jnp.int32, lf.shape, 0)
        c = jnp.where(lane < 8, lf, 0.0)
        sh = 1
        while sh < S:
            c = c + jnp.where(row >= sh, pltpu.roll(c, sh, 0), 0.0)
            sh *= 2
        c_ref[...] = c
        grp = _head_lanes((S, WIDTH), HEAD_DIM)
        cc = jnp.zeros((S, WIDTH), F32)
        for h in range(8):
            cc = jnp.where(grp == h, c[:, h:h + 1], cc)
        cc_ref[...] = cc

    return pl.pallas_call(
        body, grid=(B,),
        in_specs=[pl.BlockSpec((S, 128), lambda b: (b, 0)), pl.BlockSpec((1, 128), lambda b: (0, 0))],
        out_specs=[pl.BlockSpec((S, 128), lambda b: (b, 0)), pl.BlockSpec((S, WIDTH), lambda b: (b, 0))],
        out_shape=[SDS((B * S, 128), F32), SDS((B * S, WIDTH), F32)],
        compiler_params=_cp(), name=name,
    )(gate, b_pad)


def _gate_bwd(dcc, gate, b_pad, *, B, S, name):
    def body(dcc_ref, g_ref, b_ref, dg_ref, db_ref):
        bi = pl.program_id(0)
        dccv = dcc_ref[...]
        lane = lax.broadcasted_iota(jnp.int32, (S, 128), 1)
        row = lax.broadcasted_iota(jnp.int32, (S, 128), 0)
        dc = jnp.zeros((S, 128), F32)
        for h in range(8):
            dc = jnp.where(lane == h, dccv[:, HEAD_DIM * h:HEAD_DIM * h + 1], dc)
        sh = 1
        while sh < S:
            dc = dc + jnp.where(row < S - sh, pltpu.roll(dc, S - sh, 0), 0.0)
            sh *= 2
        xv = g_ref[...] + b_ref[...]
        dgate = jnp.where(lane < 8, dc / (1.0 + jnp.exp(xv)), 0.0)
        dg_ref[...] = dgate.astype(BF16)
        db = jnp.sum(dgate, axis=0, keepdims=True)

        @pl.when(bi == 0)
        def _():
            db_ref[...] = db

        @pl.when(bi > 0)
        def _():
            db_ref[...] += db

    return pl.pallas_call(
        body, grid=(B,),
        in_specs=[pl.BlockSpec((S, WIDTH), lambda b: (b, 0)), pl.BlockSpec((S, 128), lambda b: (b, 0)),
                  pl.BlockSpec((1, 128), lambda b: (0, 0))],
        out_specs=[pl.BlockSpec((S, 128), lambda b: (b, 0)), pl.BlockSpec((1, 128), lambda b: (0, 0))],
        out_shape=[SDS((B * S, 128), BF16), SDS((1, 128), F32)],
        compiler_params=_cp(), name=name,
    )(dcc, gate, b_pad)


_SMEM_SPEC = pl.BlockSpec(memory_space=pltpu.SMEM)


def _alibi_slopes():
    return 2.0 ** (-(jnp.arange(1, 9, dtype=F32) * (8.0 / 8)))


def _pair_masks():
    lane = lax.broadcasted_iota(jnp.int32, (1, 128), 1)
    first = lane < HEAD_DIM
    return (first.astype(BF16), (~first).astype(BF16)), first


def _dil_bias(dilation):
    qi = lax.broadcasted_iota(jnp.int32, (BLOCK, 2 * BLOCK), 0)
    kj = lax.broadcasted_iota(jnp.int32, (BLOCK, 2 * BLOCK), 1)
    delta2 = qi + BLOCK - kj
    valid2 = (delta2 >= 0) & (delta2 <= BLOCK)
    q1 = lax.broadcasted_iota(jnp.int32, (BLOCK, BLOCK), 0)
    k1 = lax.broadcasted_iota(jnp.int32, (BLOCK, BLOCK), 1)
    delta1 = q1 - k1
    valid1 = delta1 >= 0
    return ((delta1 * dilation).astype(F32), valid1), ((delta2 * dilation).astype(F32), valid2)


def _dil_views(B, S, dilation):
    L = S // dilation
    return L, L // BLOCK


def _dil_fwd(z, *, B, S, dilation, name):
    L, nb = _dil_views(B, S, dilation)
    zv = z.reshape(B * L, dilation * QKV_W)

    def body(slope_ref, q_ref, k_ref, v_ref, o_ref, l_ref):
        (m_first, m_second), first = _pair_masks()
        (dist1, valid1), (dist2, valid2) = _dil_bias(dilation)
        p = pl.program_id(2)
        bias1 = [jnp.where(valid1, -slope_ref[2 * p + e] * dist1, NEG) for e in (0, 1)]
        bias2 = [jnp.where(valid2, -slope_ref[2 * p + e] * dist2, NEG) for e in (0, 1)]

        def block(qs, ks, klen, bias):
            q2 = q_ref[pl.ds(qs, BLOCK), :]
            kk = k_ref[pl.ds(ks, klen), :]
            vv = v_ref[pl.ds(ks, klen), :]
            outs, lses = [], []
            for e, hm in enumerate((m_first, m_second)):
                s = _dot(q2 * hm, kk, NT) * 0.125 + bias[e]
                m = jnp.max(s, axis=1, keepdims=True)
                pe = jnp.exp(s - m)
                l = jnp.sum(pe, axis=1, keepdims=True)
                outs.append(_dot(pe.astype(BF16), vv) * (1.0 / l))
                lses.append(m + jnp.log(l))
            o_ref[pl.ds(qs, BLOCK), :] = jnp.where(first, outs[0], outs[1]).astype(BF16)
            l_ref[pl.ds(qs, BLOCK), :] = jnp.where(first, lses[0], lses[1])

        block(0, 0, BLOCK, bias1)
        if nb > 1:
            def step(n, carry):
                block(pl.multiple_of(n * BLOCK, BLOCK), pl.multiple_of((n - 1) * BLOCK, BLOCK), 2 * BLOCK, bias2)
                return carry
            lax.fori_loop(1, nb, step, 0)

    spec = lambda off: pl.BlockSpec((L, 128), lambda b, r, p: (b, 24 * r + 4 * off + p))
    ospec = pl.BlockSpec((L, 128), lambda b, r, p: (b, 4 * r + p))
    o, l = pl.pallas_call(
        body, grid=(B, dilation, 4), in_specs=[_SMEM_SPEC, spec(0), spec(1), spec(2)], out_specs=[ospec, ospec],
        out_shape=[SDS((B * L, dilation * WIDTH), BF16), SDS((B * L, dilation * WIDTH), F32)],
        compiler_params=_cp(), name=name,
    )(_alibi_slopes(), zv, zv, zv)
    return o.reshape(B * S, WIDTH), l.reshape(B * S, WIDTH)


def _dil_combine(os_, ls_, *, tm, name):
    T = os_[0].shape[0]

    def body(o1, o2, o3, l1, l2, l3, y_ref, lse_ref):
        a, b, c = l1[...], l2[...], l3[...]
        m = jnp.maximum(jnp.maximum(a, b), c)
        lse = m + jnp.log(jnp.exp(a - m) + jnp.exp(b - m) + jnp.exp(c - m))
        y = (jnp.exp(a - lse) * o1[...].astype(F32) + jnp.exp(b - lse) * o2[...].astype(F32)
             + jnp.exp(c - lse) * o3[...].astype(F32))
        y_ref[...] = y.astype(BF16)
        lse_ref[...] = lse

    spec = pl.BlockSpec((tm, WIDTH), lambda i: (i, 0))
    return pl.pallas_call(
        body, grid=(T // tm,), in_specs=[spec] * 6, out_specs=[spec, spec],
        out_shape=[SDS((T, WIDTH), BF16), SDS((T, WIDTH), F32)], compiler_params=_cp(), name=name,
    )(*os_, *ls_)


def _dil_bwd(z, dy, ya, lse, *, B, S, dilation, name):
    L, nb = _dil_views(B, S, dilation)
    zv = z.reshape(B * L, dilation * QKV_W)
    dyv = dy.reshape(B * L, dilation * 2 * WIDTH)
    yav = ya.reshape(B * L, dilation * WIDTH)
    lsev = lse.reshape(B * L, dilation * WIDTH)

    def body(slope_ref, q_ref, k_ref, v_ref, do_ref, o_ref, lse_ref, dq_ref, dk_ref, dv_ref, dk_s, dv_s):
        (m_first, m_second), first = _pair_masks()
        (dist1, valid1), (dist2, valid2) = _dil_bias(dilation)
        p = pl.program_id(2)
        bias1 = [jnp.where(valid1, -slope_ref[2 * p + e] * dist1, NEG) for e in (0, 1)]
        bias2 = [jnp.where(valid2, -slope_ref[2 * p + e] * dist2, NEG) for e in (0, 1)]
        dk_s[...] = jnp.zeros_like(dk_s)
        dv_s[...] = jnp.zeros_like(dv_s)

        def block(qs, ks, klen, bias):
            q2 = q_ref[pl.ds(qs, BLOCK), :]
            kk = k_ref[pl.ds(ks, klen), :]
            vv = v_ref[pl.ds(ks, klen), :]
            do2 = do_ref[pl.ds(qs, BLOCK), :]
            lse2 = lse_ref[pl.ds(qs, BLOCK), :]
            prod = do2.astype(F32) * o_ref[pl.ds(qs, BLOCK), :].astype(F32)
            dq_pair = None
            dk_c = jnp.zeros((klen, 128), F32)
            dv_c = jnp.zeros((klen, 128), F32)
            for e, hm in enumerate((m_first, m_second)):
                sel = first if e == 0 else ~first
                qm, dom = q2 * hm, do2 * hm
                rowdot = jnp.sum(jnp.where(sel, prod, 0.0), axis=1, keepdims=True)
                s = _dot(qm, kk, NT) * 0.125 + bias[e]
                pe = jnp.exp(s - lse2[:, HEAD_DIM * e:HEAD_DIM * e + 1])
                ds = (pe * (_dot(dom, vv, NT) - rowdot)).astype(BF16)
                dqe = _dot(ds, kk)
                dq_pair = dqe if e == 0 else jnp.where(first, dq_pair, dqe)
                dk_c = dk_c + _dot(ds, qm, TN)
                dv_c = dv_c + _dot(pe.astype(BF16), dom, TN)
            dq_ref[pl.ds(qs, BLOCK), :] = (dq_pair * 0.125).astype(BF16)
            dk_s[pl.ds(ks, klen), :] += dk_c * 0.125
            dv_s[pl.ds(ks, klen), :] += dv_c

        block(0, 0, BLOCK, bias1)
        if nb > 1:
            def step(n, carry):
                block(pl.multiple_of(n * BLOCK, BLOCK), pl.multiple_of((n - 1) * BLOCK, BLOCK), 2 * BLOCK, bias2)
                return carry
            lax.fori_loop(1, nb, step, 0)

        dk_ref[...] = dk_s[...].astype(BF16)
        dv_ref[...] = dv_s[...].astype(BF16)

    spec = lambda off: pl.BlockSpec((L, 128), lambda b, r, p: (b, 24 * r + 4 * off + p))
    ospec = pl.BlockSpec((L, 128), lambda b, r, p: (b, 4 * r + p))
    outs = pl.pallas_call(
        body, grid=(B, dilation, 4),
        in_specs=[_SMEM_SPEC, spec(0), spec(1), spec(2), pl.BlockSpec((L, 128), lambda b, r, p: (b, 8 * r + p)), ospec, ospec],
        out_specs=[ospec] * 3, out_shape=[SDS((B * L, dilation * WIDTH), BF16)] * 3,
        scratch_shapes=[pltpu.VMEM((L, 128), F32)] * 2, compiler_params=_cp(), name=name,
    )(_alibi_slopes(), zv, zv, zv, dyv, yav, lsev)
    return [o.reshape(B * S, WIDTH) for o in outs]


FOX_TQ = 256


def _fox_fwd(z, cc, cr, *, B, S, name):
    def body(q_ref, k_ref, v_ref, cc_ref, cr_ref, o_ref):
        (m_first, m_second), first = _pair_masks()
        for qi in range(S // FOX_TQ):
            r0, kend = qi * FOX_TQ, (qi + 1) * FOX_TQ
            q2 = q_ref[r0:kend, :]
            kk, vv = k_ref[0:kend, :], v_ref[0:kend, :]
            row = lax.broadcasted_iota(jnp.int32, (FOX_TQ, kend), 0) + r0
            col = lax.broadcasted_iota(jnp.int32, (FOX_TQ, kend), 1)
            causal = col <= row
            outs = []
            for e, hm in enumerate((m_first, m_second)):
                s = _dot(q2 * hm, kk, NT) * 0.125
                s = s + (cc_ref[r0:kend, :][:, HEAD_DIM * e:HEAD_DIM * e + 1] - cr_ref[e:e + 1, 0:kend])
                s = jnp.where(causal, s, NEG)
                m = jnp.max(s, axis=1, keepdims=True)
                pe = jnp.exp(s - m)
                l = jnp.sum(pe, axis=1, keepdims=True)
                outs.append(_dot(pe.astype(BF16), vv) * (1.0 / l))
            o_ref[r0:kend, :] = jnp.where(first, outs[0], outs[1]).astype(BF16)

    spec = lambda off: pl.BlockSpec((S, 128), lambda b, p: (b, 4 * off + p))
    return pl.pallas_call(
        body, grid=(B, 4),
        in_specs=[spec(3), spec(4), spec(5), pl.BlockSpec((S, 128), lambda b, p: (b, p)),
                  pl.BlockSpec((None, 8, S), lambda b, p: (4 * b + p, 0, 0))],
        out_specs=pl.BlockSpec((S, 128), lambda b, p: (b, p)), out_shape=SDS((B * S, WIDTH), BF16),
        compiler_params=_cp(), name=name,
    )(z, z, z, cc, cr)


def _fox_bwd(z, dy, cc, cr, *, B, S, name):
    def body(q_ref, k_ref, v_ref, do_ref, cc_ref, cr_ref, dq_ref, dk_ref, dv_ref, dc_ref, dk_s, dv_s, dc_s):
        (m_first, m_second), first = _pair_masks()
        dk_s[...] = jnp.zeros_like(dk_s)
        dv_s[...] = jnp.zeros_like(dv_s)
        dc_s[...] = jnp.zeros_like(dc_s)
        for qi in range(S // FOX_TQ):
            r0, kend = qi * FOX_TQ, (qi + 1) * FOX_TQ
            q2, do2 = q_ref[r0:kend, :], do_ref[r0:kend, :]
            kk, vv = k_ref[0:kend, :], v_ref[0:kend, :]
            krow = lax.broadcasted_iota(jnp.int32, (kend, FOX_TQ), 0)
            qcol = lax.broadcasted_iota(jnp.int32, (kend, FOX_TQ), 1) + r0
            causal = krow <= qcol
            dq_t = jnp.zeros((FOX_TQ, 128), F32)
            for e, hm in enumerate((m_first, m_second)):
                sel = first if e == 0 else ~first
                km = kk * hm
                st = _dot(km, q2, NT) * 0.125
                st = st + (cr_ref[e:e + 1, r0:kend] - cc_ref[0:kend, :][:, HEAD_DIM * e:HEAD_DIM * e + 1])
                st = jnp.where(causal, st, NEG)
                pt = jnp.exp(st - jnp.max(st, axis=0, keepdims=True))
                pt = pt * (1.0 / jnp.sum(pt, axis=0, keepdims=True))
                dpt = _dot(vv * hm, do2, NT)
                dst = pt * (dpt - jnp.sum(pt * dpt, axis=0, keepdims=True))
                dsb = dst.astype(BF16)
                dv_s[0:kend, :] += _dot(pt.astype(BF16), do2 * hm)
                dk_s[0:kend, :] += _dot(dsb, q2 * hm) * 0.125
                dq_t = dq_t + _dot(dsb, km, TN)
                dc_s[0:kend, :] += jnp.where(sel, -jnp.sum(dst, axis=1, keepdims=True), 0.0)
            dq_ref[r0:kend, :] = (dq_t * 0.125).astype(BF16)
        dk_ref[...] = dk_s[...].astype(BF16)
        dv_ref[...] = dv_s[...].astype(BF16)
        dc_ref[...] = dc_s[...]

    spec = lambda off: pl.BlockSpec((S, 128), lambda b, p: (b, 4 * off + p))
    pspec = pl.BlockSpec((S, 128), lambda b, p: (b, p))
    return pl.pallas_call(
        body, grid=(B, 4),
        in_specs=[spec(3), spec(4), spec(5), pl.BlockSpec((S, 128), lambda b, p: (b, 4 + p)), pspec,
                  pl.BlockSpec((None, 8, S), lambda b, p: (4 * b + p, 0, 0))],
        out_specs=[pspec] * 4,
        out_shape=[SDS((B * S, WIDTH), BF16)] * 3 + [SDS((B * S, WIDTH), F32)],
        scratch_shapes=[pltpu.VMEM((S, 128), F32)] * 3, compiler_params=_cp(), name=name,
    )(z, z, z, dy, cc, cr)


def _xattn_fwd(q, kv, *, B, S, M, tq, name):
    D = D_MODEL

    def body(q_ref, kv_ref, o_ref):
        for h in range(N_XH):
            cs = slice(XHD * h, XHD * (h + 1))
            s = _dot(q_ref[:, cs], kv_ref[:, cs], NT) * (1.0 / 16.0)
            pe = jnp.exp(s - jnp.max(s, axis=1, keepdims=True))
            l = jnp.sum(pe, axis=1, keepdims=True)
            o_ref[:, cs] = (_dot(pe.astype(BF16), kv_ref[:, D + XHD * h:D + XHD * (h + 1)]) * (1.0 / l)).astype(BF16)

    nq = S // tq
    return pl.pallas_call(
        body, grid=(B, nq),
        in_specs=[pl.BlockSpec((tq, D), lambda b, t: (b * nq + t, 0)), pl.BlockSpec((M, 2 * D), lambda b, t: (b, 0))],
        out_specs=pl.BlockSpec((tq, D), lambda b, t: (b * nq + t, 0)), out_shape=SDS((B * S, D), BF16),
        compiler_params=_cp(), name=name,
    )(q, kv)


def _xattn_bwd(q, kv, do, *, B, S, M, tq, name):
    D = D_MODEL

    def body(q_ref, kv_ref, do_ref, dq_ref, dkv_ref):
        t = pl.program_id(1)

        @pl.when(t == 0)
        def _():
            dkv_ref[...] = jnp.zeros_like(dkv_ref)

        for h in range(N_XH):
            cs = slice(XHD * h, XHD * (h + 1))
            vs = slice(D + XHD * h, D + XHD * (h + 1))
            qh, kh, vh, doh = q_ref[:, cs], kv_ref[:, cs], kv_ref[:, vs], do_ref[:, cs]
            s = _dot(qh, kh, NT) * (1.0 / 16.0)
            pe = jnp.exp(s - jnp.max(s, axis=1, keepdims=True))
            pe = pe * (1.0 / jnp.sum(pe, axis=1, keepdims=True))
            dp = _dot(doh, vh, NT)
            ds = (pe * (dp - jnp.sum(pe * dp, axis=1, keepdims=True))).astype(BF16)
            dq_ref[:, cs] = (_dot(ds, kh) * (1.0 / 16.0)).astype(BF16)
            dkv_ref[:, cs] += _dot(ds, qh, TN) * (1.0 / 16.0)
            dkv_ref[:, vs] += _dot(pe.astype(BF16), doh, TN)

    nq = S // tq
    qspec = pl.BlockSpec((tq, D), lambda b, t: (b * nq + t, 0))
    kvspec = pl.BlockSpec((M, 2 * D), lambda b, t: (b, 0))
    return pl.pallas_call(
        body, grid=(B, nq), in_specs=[qspec, kvspec, qspec], out_specs=[qspec, kvspec],
        out_shape=[SDS((B * S, D), BF16), SDS((B * M, 2 * D), F32)], compiler_params=_cp(), name=name,
    )(q, kv, do)


def _assemble_dz(dil_parts, fox_parts, dgate, *, tm, name):
    T = dgate.shape[0]

    def body(*refs):
        o_ref = refs[-1]
        for j in range(3):
            acc = refs[j][...].astype(F32) + refs[3 + j][...].astype(F32) + refs[6 + j][...].astype(F32)
            o_ref[:, WIDTH * j:WIDTH * (j + 1)] = acc.astype(BF16)
        for j in range(3):
            o_ref[:, WIDTH * (3 + j):WIDTH * (4 + j)] = refs[9 + j][...]
        o_ref[:, QKV_W:IN_PAD] = refs[12][...]

    wspec = pl.BlockSpec((tm, WIDTH), lambda i: (i, 0))
    return pl.pallas_call(
        body, grid=(T // tm,), in_specs=[wspec] * 12 + [pl.BlockSpec((tm, 128), lambda i: (i, 0))],
        out_specs=pl.BlockSpec((tm, IN_PAD), lambda i: (i, 0)), out_shape=SDS((T, IN_PAD), BF16),
        compiler_params=_cp(), name=name,
    )(*[a for br in dil_parts for a in br], *fox_parts, dgate)


def _adamw(parts, w, m, v, *, tr, name):
    R, C = w.shape

    def body(p_ref, w_ref, m_ref, v_ref, g_ref, d_ref, nm_ref, nv_ref):
        g = p_ref[0].astype(F32)
        for d in range(1, N_DEV):
            g = g + p_ref[d].astype(F32)
        m2 = ADAM_B1 * m_ref[...] + (1.0 - ADAM_B1) * g
        v2 = ADAM_B2 * v_ref[...] + (1.0 - ADAM_B2) * (g * g)
        m_hat = m2 / (1.0 - ADAM_B1 ** ADAM_STEP)
        v_hat = v2 / (1.0 - ADAM_B2 ** ADAM_STEP)
        g_ref[...] = g
        d_ref[...] = -ADAM_LR * (m_hat / (jnp.sqrt(v_hat) + ADAM_EPS) + ADAM_WD * w_ref[...])
        nm_ref[...] = m2
        nv_ref[...] = v2

    spec = pl.BlockSpec((tr, C), lambda i: (i, 0))
    return pl.pallas_call(
        body, grid=(R // tr,), in_specs=[pl.BlockSpec((N_DEV, tr, C), lambda i: (0, i, 0)), spec, spec, spec],
        out_specs=[spec] * 4, out_shape=[SDS((R, C), F32)] * 4, compiler_params=_cp(), name=name,
    )(parts, w, m, v)


def _exchange(arrays, modes, *, name):
    n = len(arrays)
    out_shape = [SDS((N_DEV,) + a.shape if md == "gather" else a.shape, a.dtype) for a, md in zip(arrays, modes)]

    def body(*refs):
        ins, outs = refs[:n], refs[n:2 * n]
        send_sems, recv_sems, local_sems = refs[2 * n:]
        x, y, c = (lax.axis_index(a) for a in AXES)
        me = 4 * x + 2 * y + c
        copies = []
        for i, md in enumerate(modes):
            src = ins[i] if md == "gather" else ins[i].at[me]
            cp = pltpu.make_async_copy(src, outs[i].at[me], local_sems.at[i])
            cp.start()
            copies.append(cp)
            for k in range(1, N_DEV):
                px = 1 - x if k & 4 else x
                py = 1 - y if k & 2 else y
                pc = 1 - c if k & 1 else c
                src = ins[i] if md == "gather" else ins[i].at[4 * px + 2 * py + pc]
                cp = pltpu.make_async_remote_copy(
                    src_ref=src, dst_ref=outs[i].at[me], send_sem=send_sems.at[i, k - 1], recv_sem=recv_sems.at[i, k - 1],
                    device_id=(px, py, pc), device_id_type=pl.DeviceIdType.MESH)
                cp.start()
                copies.append(cp)
        for cp in copies:
            cp.wait()

    anyspec = pl.BlockSpec(memory_space=pl.ANY)
    return pl.pallas_call(
        body, in_specs=[anyspec] * n, out_specs=[anyspec] * n, out_shape=out_shape,
        scratch_shapes=[pltpu.SemaphoreType.DMA((n, N_DEV - 1)), pltpu.SemaphoreType.DMA((n, N_DEV - 1)),
                        pltpu.SemaphoreType.DMA((n,))],
        name=name,
    )(*arrays)


def _local_step(x, mem, g_mix, b_forget, g_xattn, g_mem, g_mlp, g_final, target,
                w_in_pad, w_out, w_xq, w_kv, w_xo, w_up, w_down):
    B, S, D = x.shape
    M = mem.shape[1]
    T = B * S
    x0 = x.reshape(T, D)
    mem2 = mem.reshape(B * M, D)
    tgt = target.reshape(T, D)
    b_pad = jnp.pad(b_forget, (0, 120)).reshape(1, 128)

    h1, z = _rms_matmul(x0, g_mix, w_in_pad[:, :QKV_W], tm=1024, tn=768, out_dtype=BF16, name="f_in")
    gate = _matmul_nn(h1, w_in_pad[:, QKV_W:], tm=1024, tn=128, tk=D, out_dtype=F32, name="f_gate")
    c, cc = _gate_fwd(gate, b_pad, B=B, S=S, name="f_gatecum")
    cr = jnp.pad(c[:, :8].reshape(B, S, 4, 2).transpose(0, 2, 3, 1), ((0, 0), (0, 0), (0, 6), (0, 0))).reshape(B * 4, 8, S)
    branches = [_dil_fwd(z, B=B, S=S, dilation=d, name=f"f_dil{d}") for _, d in DIL_CONFIGS]
    ya, lse = _dil_combine([o for o, _ in branches], [l for _, l in branches], tm=1024, name="f_dilmix")
    yf = _fox_fwd(z, cc, cr, B=B, S=S, name="f_fox")
    ymix = jnp.concatenate([ya, yf], axis=1)
    x1 = _matmul_nn(ymix, w_out, res=x0, tm=1024, tn=D, tk=D, out_dtype=F32, name="f_out")
    h2, q = _rms_matmul(x1, g_xattn, w_xq, tm=1024, tn=D, out_dtype=BF16, name="f_xq")
    mn, kv = _rms_matmul(mem2, g_mem, w_kv, tm=B * M, tn=D, out_dtype=BF16, name="f_xkv")
    xo = _xattn_fwd(q, kv, B=B, S=S, M=M, tq=512, name="f_xattn")
    x2 = _matmul_nn(xo, w_xo, res=x1, tm=1024, tn=D, tk=D, out_dtype=F32, name="f_xo")
    h3, act = _rms_matmul(x2, g_mlp, w_up, tm=1024, tn=1024, out_dtype=BF16, relu=True, name="f_up")
    x3 = _matmul_nn(act, w_down, res=x2, square=True, tm=1024, tn=D, tk=1024, out_dtype=F32, name="f_down")
    dx3, dg_final, loss = _loss_head(x3, g_final, tgt, tm=512, name="f_loss")

    du = _matmul_nt(dx3, w_down, mul2a=act, tm=1024, tn=1024, name="b_dact")
    dw_down = _matmul_tn(act, dx3, square=True, bk=1024, bn=D, tt=512, out_dtype=BF16, name="b_wdown")
    dw_up = _matmul_tn(h3, du, bk=D, bn=1024, tt=512, out_dtype=BF16, name="b_wup")
    dx2, dg_mlp = _matmul_nt_rms(du, w_up, x2, g_mlp, dx3, tm=512, tk=1024, name="b_dh3")
    dxo = _matmul_nt(dx2, w_xo, tm=1024, tn=D, name="b_dxo")
    dw_xo = _matmul_tn(xo, dx2, bk=D, bn=D, tt=512, out_dtype=BF16, name="b_wxo")
    dq, dkv = _xattn_bwd(q, kv, dxo, B=B, S=S, M=M, tq=512, name="b_xattn")
    dw_xq = _matmul_tn(h2, dq, bk=D, bn=D, tt=512, out_dtype=BF16, name="b_wxq")
    dx1, dg_xattn = _matmul_nt_rms(dq, w_xq, x1, g_xattn, dx2, tm=512, tk=D, name="b_dh2")
    dw_kv = _matmul_tn(mn, dkv, bk=D, bn=D, tt=min(512, B * M), out_dtype=BF16, name="b_wkv")
    _, dg_mem = _matmul_nt_rms(dkv, w_kv, mem2, g_mem, None, tm=min(512, B * M), tk=D, name="b_dmem")
    dy = _matmul_nt(dx1, w_out, tm=1024, tn=D, name="b_dy")
    dw_out = _matmul_tn(ymix, dx1, bk=D, bn=D, tt=512, out_dtype=BF16, name="b_wout")
    dqf, dkf, dvf, dcc = _fox_bwd(z, dy, cc, cr, B=B, S=S, name="b_fox")
    dgate, db = _gate_bwd(dcc, gate, b_pad, B=B, S=S, name="b_gate")
    dil_parts = [_dil_bwd(z, dy, ya, lse, B=B, S=S, dilation=d, name=f"b_dil{d}") for _, d in DIL_CONFIGS]
    dz = _assemble_dz(dil_parts, [dqf, dkf, dvf], dgate, tm=256, name="b_dz")
    dw_in = _matmul_tn(h1, dz, bk=D, bn=640, tt=512, out_dtype=BF16, name="b_win")
    gx, dg_mix = _matmul_nt_rms(dz, w_in_pad, x0, g_mix, dx1, tm=512, tk=640, name="b_dh1")

    small = dict(g_mix=dg_mix, b_forget=db, g_xattn=dg_xattn, g_mem=dg_mem, g_mlp=dg_mlp, g_final=dg_final)
    big = dict(w_in=dw_in, w_out=dw_out, w_xq=dw_xq, w_kv=dw_kv, w_xo=dw_xo, w_up=dw_up, w_down=dw_down)
    return gx.reshape(B, S, D), big, small, loss


SMALL_ROWS = ("g_mix", "b_forget", "g_xattn", "g_mem", "g_mlp", "g_final")


def _pack_rows(rows):
    D = D_MODEL
    rows = [jnp.pad(r.reshape(-1), (0, D - r.size)) for r in rows]
    rows += [jnp.zeros((D,), F32)] * (8 - len(rows))
    return jnp.stack(rows)


def kernel(x, mem, g_mix, w_in, b_forget, w_out, g_xattn, g_mem, w_xq, w_xk, w_xv, w_xo, g_mlp, w_up, w_down, g_final, loss_target, m_g_mix, m_w_in, m_b_forget, m_w_out, m_g_xattn, m_g_mem, m_w_xq, m_w_xk, m_w_xv, m_w_xo, m_g_mlp, m_w_up, m_w_down, m_g_final, v_g_mix, v_w_in, v_b_forget, v_w_out, v_g_xattn, v_g_mem, v_w_xq, v_w_xk, v_w_xv, v_w_xo, v_g_mlp, v_w_up, v_w_down, v_g_final):
    D = D_MODEL
    W = dict(w_in=w_in, w_out=w_out, w_xq=w_xq, w_xk=w_xk, w_xv=w_xv, w_xo=w_xo, w_up=w_up, w_down=w_down)
    Mo = dict(w_in=m_w_in, w_out=m_w_out, w_xq=m_w_xq, w_xk=m_w_xk, w_xv=m_w_xv, w_xo=m_w_xo, w_up=m_w_up, w_down=m_w_down)
    Vo = dict(w_in=v_w_in, w_out=v_w_out, w_xq=v_w_xq, w_xk=v_w_xk, w_xv=v_w_xv, w_xo=v_w_xo, w_up=v_w_up, w_down=v_w_down)
    names = list(W)

    gathered = dict(zip(names, _exchange([W[n].astype(BF16) for n in names], ["gather"] * len(names), name="gather_weights")))
    cols = lambda g: g.transpose(1, 0, 2).reshape(g.shape[1], -1)
    rows = lambda g: g.reshape(-1, g.shape[2])
    w_in_pad = jnp.pad(cols(gathered["w_in"]), ((0, 0), (0, IN_PAD - IN_W)))
    w_kv = jnp.concatenate([rows(gathered["w_xk"]), rows(gathered["w_xv"])], axis=1)

    gx, big, small, loss = _local_step(
        x, mem, g_mix, b_forget, g_xattn, g_mem, g_mlp, g_final, loss_target,
        w_in_pad, rows(gathered["w_out"]), rows(gathered["w_xq"]), w_kv, rows(gathered["w_xo"]),
        cols(gathered["w_up"]), rows(gathered["w_down"]))

    by_cols = lambda g, n: g[:, :n * N_DEV].reshape(g.shape[0], N_DEV, n).transpose(1, 0, 2)
    by_rows = lambda g: g.reshape(N_DEV, g.shape[0] // N_DEV, g.shape[1])
    parts = dict(w_in=by_cols(big["w_in"], IN_W // N_DEV), w_out=by_rows(big["w_out"]), w_xq=by_rows(big["w_xq"]),
                 w_xk=by_rows(big["w_kv"][:, :D]), w_xv=by_rows(big["w_kv"][:, D:]), w_xo=by_rows(big["w_xo"]),
                 w_up=by_cols(big["w_up"], D_FF // N_DEV), w_down=by_rows(big["w_down"]))
    packed = _pack_rows([small[n] for n in SMALL_ROWS] + [loss[0, :1]])
    received = _exchange([parts[n] for n in names] + [packed], ["scatter"] * len(names) + ["gather"], name="exchange_grads")
    received, packed_all = dict(zip(names, received[:-1])), received[-1]

    res = {n: _adamw(received[n], W[n], Mo[n], Vo[n], tr=128, name=f"adamw_{n}") for n in names}
    small_w = dict(g_mix=g_mix, b_forget=b_forget, g_xattn=g_xattn, g_mem=g_mem, g_mlp=g_mlp, g_final=g_final)
    small_m = dict(g_mix=m_g_mix, b_forget=m_b_forget, g_xattn=m_g_xattn, g_mem=m_g_mem, g_mlp=m_g_mlp, g_final=m_g_final)
    small_v = dict(g_mix=v_g_mix, b_forget=v_b_forget, g_xattn=v_g_xattn, g_mem=v_g_mem, g_mlp=v_g_mlp, g_final=v_g_final)
    sres = _adamw(packed_all, _pack_rows([small_w[n] for n in SMALL_ROWS]), _pack_rows([small_m[n] for n in SMALL_ROWS]),
                  _pack_rows([small_v[n] for n in SMALL_ROWS]), tr=8, name="adamw_small")
    for i, n in enumerate(SMALL_ROWS):
        res[n] = [r[i, :small_w[n].size] for r in sres]
    loss_total = sres[0][6, 0]

    order = ["g_mix", "w_in", "b_forget", "w_out", "g_xattn", "g_mem", "w_xq", "w_xk", "w_xv", "w_xo", "g_mlp", "w_up", "w_down", "g_final"]
    return (loss_total, gx, *[res[n][0] for n in order], *[res[n][1] for n in order],
            *[res[n][2] for n in order], *[res[n][3] for n in order])
```

```python
import jax
import jax.numpy as jnp
from jax import lax
from jax.experimental import pallas as pl
from jax.experimental.pallas import tpu as pltpu

F32, BF16 = jnp.float32, jnp.bfloat16
SDS = jax.ShapeDtypeStruct

D_MODEL = 1024
HEAD_DIM = 64
WIDTH = 512
QKV_W = 6 * WIDTH
IN_W = QKV_W + 8
IN_PAD = QKV_W + 128
BLOCK = 128
DIL_CONFIGS = ((128, 1), (512, 4), (2048, 16))
N_XH, XHD = 4, 256
D_FF = 4096
EPS = 1e-6
NEG = -1e30
N_DEV = 8
AXES = ("x", "y", "c")

ADAM_LR, ADAM_B1, ADAM_B2, ADAM_EPS, ADAM_WD, ADAM_STEP = 0.001, 0.9, 0.999, 1e-08, 0.01, 10

VMEM_CAP_V7X = 64 * 1024 * 1024
VMEM_LIMIT = VMEM_CAP_V7X * 7 // 8

NT = (((1,), (1,)), ((), ()))
TN = (((0,), (0,)), ((), ()))


def _cp(**kw):
    return pltpu.CompilerParams(vmem_limit_bytes=VMEM_LIMIT, **kw)


def _dot(a, b, dims=None):
    if dims is None:
        return jnp.dot(a, b, preferred_element_type=F32)
    return lax.dot_general(a, b, dims, preferred_element_type=F32)


def _rstd(xv):
    return lax.rsqrt(jnp.mean(xv * xv, axis=-1, keepdims=True) + EPS)


def _rms_bwd(dh, xv, g):
    r = _rstd(xv)
    xhat = xv * r
    dxhat = dh * g
    dx = r * (dxhat - xhat * jnp.mean(dxhat * xhat, axis=-1, keepdims=True))
    return dx, jnp.sum(dh * xhat, axis=0, keepdims=True)


def _rms_matmul(x, g, w, *, tm, tn, out_dtype, relu=False, name):
    T, D = x.shape
    N = w.shape[1]

    def body(x_ref, g_ref, w_ref, h_ref, o_ref, h_s):
        @pl.when(pl.program_id(1) == 0)
        def _():
            xv = x_ref[...]
            h = (xv * _rstd(xv) * g_ref[...]).astype(BF16)
            h_s[...] = h
            h_ref[...] = h

        acc = _dot(h_s[...], w_ref[...])
        if relu:
            acc = jnp.maximum(acc, 0.0)
        o_ref[...] = acc.astype(out_dtype)

    return pl.pallas_call(
        body, grid=(T // tm, N // tn),
        in_specs=[pl.BlockSpec((tm, D), lambda i, j: (i, 0)), pl.BlockSpec((1, D), lambda i, j: (0, 0)),
                  pl.BlockSpec((D, tn), lambda i, j: (0, j))],
        out_specs=[pl.BlockSpec((tm, D), lambda i, j: (i, 0)), pl.BlockSpec((tm, tn), lambda i, j: (i, j))],
        out_shape=[SDS((T, D), BF16), SDS((T, N), out_dtype)],
        scratch_shapes=[pltpu.VMEM((tm, D), BF16)], compiler_params=_cp(), name=name,
    )(x, g.reshape(1, D), w)


def _matmul_nn(a, w, *, res=None, square=False, tm, tn, tk, out_dtype, name):
    T, K = a.shape
    N = w.shape[1]
    nk = K // tk

    def body(*refs):
        a_ref, w_ref = refs[0], refs[1]
        res_ref = refs[2] if res is not None else None
        o_ref, acc = refs[-2], refs[-1]
        k = pl.program_id(2)

        @pl.when(k == 0)
        def _():
            acc[...] = jnp.zeros_like(acc)

        av = a_ref[...]
        if square:
            af = av.astype(F32)
            av = (af * af).astype(BF16)
        acc[...] += _dot(av, w_ref[...])

        @pl.when(k == nk - 1)
        def _():
            r = acc[...]
            if res_ref is not None:
                r = res_ref[...] + r
            o_ref[...] = r.astype(out_dtype)

    in_specs = [pl.BlockSpec((tm, tk), lambda i, j, k: (i, k)), pl.BlockSpec((tk, tn), lambda i, j, k: (k, j))]
    args = [a, w]
    if res is not None:
        in_specs.append(pl.BlockSpec((tm, tn), lambda i, j, k: (i, j)))
        args.append(res)
    return pl.pallas_call(
        body, grid=(T // tm, N // tn, nk), in_specs=in_specs,
        out_specs=pl.BlockSpec((tm, tn), lambda i, j, k: (i, j)), out_shape=SDS((T, N), out_dtype),
        scratch_shapes=[pltpu.VMEM((tm, tn), F32)], compiler_params=_cp(), name=name,
    )(*args)


def _matmul_nt(g, w, *, mul2a=None, tm, tn, name):
    T, K = g.shape
    N = w.shape[0]

    def body(*refs):
        g_ref, w_ref = refs[0], refs[1]
        o_ref = refs[-1]
        acc = _dot(g_ref[...].astype(BF16), w_ref[...], NT)
        if mul2a is not None:
            acc = acc * (2.0 * refs[2][...].astype(F32))
        o_ref[...] = acc.astype(BF16)

    in_specs = [pl.BlockSpec((tm, K), lambda i, j: (i, 0)), pl.BlockSpec((tn, K), lambda i, j: (j, 0))]
    args = [g, w]
    if mul2a is not None:
        in_specs.append(pl.BlockSpec((tm, tn), lambda i, j: (i, j)))
        args.append(mul2a)
    return pl.pallas_call(
        body, grid=(T // tm, N // tn), in_specs=in_specs,
        out_specs=pl.BlockSpec((tm, tn), lambda i, j: (i, j)), out_shape=SDS((T, N), BF16),
        compiler_params=_cp(), name=name,
    )(*args)


def _matmul_nt_rms(g, w, x, gain, dres, *, tm, tk, name):
    T, K = g.shape
    D = w.shape[0]
    nk = K // tk
    nt = T // tm

    def body(*refs):
        g_ref, w_ref, x_ref, gain_ref = refs[:4]
        dres_ref = refs[4] if dres is not None else None
        dx_ref, dg_ref, acc = refs[-3], refs[-2], refs[-1]
        i, k = pl.program_id(0), pl.program_id(1)

        @pl.when(k == 0)
        def _():
            acc[...] = jnp.zeros_like(acc)

        acc[...] += _dot(g_ref[...].astype(BF16), w_ref[...], NT)

        @pl.when(k == nk - 1)
        def _():
            dx, dg = _rms_bwd(acc[...], x_ref[...], gain_ref[...])
            if dres_ref is not None:
                dx = dres_ref[...] + dx
            dx_ref[...] = dx

            @pl.when(i == 0)
            def _():
                dg_ref[...] = dg

            @pl.when(i > 0)
            def _():
                dg_ref[...] += dg

    in_specs = [pl.BlockSpec((tm, tk), lambda i, k: (i, k)), pl.BlockSpec((D, tk), lambda i, k: (0, k)),
                pl.BlockSpec((tm, D), lambda i, k: (i, 0)), pl.BlockSpec((1, D), lambda i, k: (0, 0))]
    args = [g, w, x, gain.reshape(1, D)]
    if dres is not None:
        in_specs.append(pl.BlockSpec((tm, D), lambda i, k: (i, 0)))
        args.append(dres)
    return pl.pallas_call(
        body, grid=(nt, nk), in_specs=in_specs,
        out_specs=[pl.BlockSpec((tm, D), lambda i, k: (i, 0)), pl.BlockSpec((1, D), lambda i, k: (0, 0))],
        out_shape=[SDS((T, D), F32), SDS((1, D), F32)],
        scratch_shapes=[pltpu.VMEM((tm, D), F32)], compiler_params=_cp(), name=name,
    )(*args)


def _matmul_tn(a, g, *, square=False, bk, bn, tt, out_dtype, name):
    T, K = a.shape
    N = g.shape[1]
    nt = T // tt

    def body(a_ref, g_ref, o_ref, acc):
        t = pl.program_id(2)

        @pl.when(t == 0)
        def _():
            acc[...] = jnp.zeros_like(acc)

        av = a_ref[...]
        if square:
            af = av.astype(F32)
            av = (af * af).astype(BF16)
        acc[...] += _dot(av, g_ref[...].astype(BF16), TN)

        @pl.when(t == nt - 1)
        def _():
            o_ref[...] = acc[...].astype(out_dtype)

    return pl.pallas_call(
        body, grid=(K // bk, N // bn, nt),
        in_specs=[pl.BlockSpec((tt, bk), lambda i, j, t: (t, i)), pl.BlockSpec((tt, bn), lambda i, j, t: (t, j))],
        out_specs=pl.BlockSpec((bk, bn), lambda i, j, t: (i, j)), out_shape=SDS((K, N), out_dtype),
        scratch_shapes=[pltpu.VMEM((bk, bn), F32)], compiler_params=_cp(), name=name,
    )(a, g)


def _loss_head(x3, g_final, target, *, tm, name):
    T, D = x3.shape

    def body(x_ref, g_ref, t_ref, dx_ref, dg_ref, loss_ref):
        i = pl.program_id(0)
        xv, g = x_ref[...], g_ref[...]
        r = _rstd(xv)
        xhat = xv * r
        diff = xhat * g - t_ref[...]
        part = 0.5 * jnp.sum(jnp.mean(diff * diff, axis=-1, keepdims=True), axis=0, keepdims=True)
        dy = diff * (1.0 / D)
        dxhat = dy * g
        dx_ref[...] = r * (dxhat - xhat * jnp.mean(dxhat * xhat, axis=-1, keepdims=True))
        dg = jnp.sum(dy * xhat, axis=0, keepdims=True)
        lp = jnp.broadcast_to(part, loss_ref.shape)

        @pl.when(i == 0)
        def _():
            dg_ref[...] = dg
            loss_ref[...] = lp

        @pl.when(i > 0)
        def _():
            dg_ref[...] += dg
            loss_ref[...] += lp

    return pl.pallas_call(
        body, grid=(T // tm,),
        in_specs=[pl.BlockSpec((tm, D), lambda i: (i, 0)), pl.BlockSpec((1, D), lambda i: (0, 0)),
                  pl.BlockSpec((tm, D), lambda i: (i, 0))],
        out_specs=[pl.BlockSpec((tm, D), lambda i: (i, 0)), pl.BlockSpec((1, D), lambda i: (0, 0)),
                   pl.BlockSpec((8, 128), lambda i: (0, 0))],
        out_shape=[SDS((T, D), F32), SDS((1, D), F32), SDS((8, 128), F32)],
        compiler_params=_cp(), name=name,
    )(x3, g_final.reshape(1, D), target)


def _head_lanes(shape, width):
    return lax.broadcasted_iota(jnp.int32, shape, len(shape) - 1) // width


def _gate_fwd(gate, b_pad, *, B, S, name):
    def body(g_ref, b_ref, c_ref, cc_ref):
        xv = g_ref[...] + b_ref[...]
        lf = jnp.minimum(xv, 0.0) - jnp.log(1.0 + jnp.exp(-jnp.abs(xv)))
        lane = lax.broadcasted_iota(jnp.int32, lf.shape, 1)
        row = lax.broadcasted_iota(jnp.int32, lf.shape, 0)
        c = jnp.where(lane < 8, lf, 0.0)
        sh = 1
        while sh < S:
            c = c + jnp.where(row >= sh, pltpu.roll(c, sh, 0), 0.0)
            sh *= 2
        c_ref[...] = c
        grp = _head_lanes((S, WIDTH), HEAD_DIM)
        cc = jnp.zeros((S, WIDTH), F32)
        for h in range(8):
            cc = jnp.where(grp == h, c[:, h:h + 1], cc)
        cc_ref[...] = cc

    return pl.pallas_call(
        body, grid=(B,),
        in_specs=[pl.BlockSpec((S, 128), lambda b: (b, 0)), pl.BlockSpec((1, 128), lambda b: (0, 0))],
        out_specs=[pl.BlockSpec((S, 128), lambda b: (b, 0)), pl.BlockSpec((S, WIDTH), lambda b: (b, 0))],
        out_shape=[SDS((B * S, 128), F32), SDS((B * S, WIDTH), F32)],
        compiler_params=_cp(), name=name,
    )(gate, b_pad)


def _gate_bwd(dcc, gate, b_pad, *, B, S, name):
    def body(dcc_ref, g_ref, b_ref, dg_ref, db_ref):
        bi = pl.program_id(0)
        dccv = dcc_ref[...]
        lane = lax.broadcasted_iota(jnp.int32, (S, 128), 1)
        row = lax.broadcasted_iota(jnp.int32, (S, 128), 0)
        dc = jnp.zeros((S, 128), F32)
        for h in range(8):
            dc = jnp.where(lane == h, dccv[:, HEAD_DIM * h:HEAD_DIM * h + 1], dc)
        sh = 1
        while sh < S:
            dc = dc + jnp.where(row < S - sh, pltpu.roll(dc, S - sh, 0), 0.0)
            sh *= 2
        xv = g_ref[...] + b_ref[...]
        dgate = jnp.where(lane < 8, dc / (1.0 + jnp.exp(xv)), 0.0)
        dg_ref[...] = dgate.astype(BF16)
        db = jnp.sum(dgate, axis=0, keepdims=True)

        @pl.when(bi == 0)
        def _():
            db_ref[...] = db

        @pl.when(bi > 0)
        def _():
            db_ref[...] += db

    return pl.pallas_call(
        body, grid=(B,),
        in_specs=[pl.BlockSpec((S, WIDTH), lambda b: (b, 0)), pl.BlockSpec((S, 128), lambda b: (b, 0)),
                  pl.BlockSpec((1, 128), lambda b: (0, 0))],
        out_specs=[pl.BlockSpec((S, 128), lambda b: (b, 0)), pl.BlockSpec((1, 128), lambda b: (0, 0))],
        out_shape=[SDS((B * S, 128), BF16), SDS((1, 128), F32)],
        compiler_params=_cp(), name=name,
    )(dcc, gate, b_pad)


_SMEM_SPEC = pl.BlockSpec(memory_space=pltpu.SMEM)


def _alibi_slopes():
    return 2.0 ** (-(jnp.arange(1, 9, dtype=F32) * (8.0 / 8)))


def _pair_masks():
    lane = lax.broadcasted_iota(jnp.int32, (1, 128), 1)
    first = lane < HEAD_DIM
    return (first.astype(BF16), (~first).astype(BF16)), first


def _dil_bias(dilation):
    qi = lax.broadcasted_iota(jnp.int32, (BLOCK, 2 * BLOCK), 0)
    kj = lax.broadcasted_iota(jnp.int32, (BLOCK, 2 * BLOCK), 1)
    delta2 = qi + BLOCK - kj
    valid2 = (delta2 >= 0) & (delta2 <= BLOCK)
    q1 = lax.broadcasted_iota(jnp.int32, (BLOCK, BLOCK), 0)
    k1 = lax.broadcasted_iota(jnp.int32, (BLOCK, BLOCK), 1)
    delta1 = q1 - k1
    valid1 = delta1 >= 0
    return ((delta1 * dilation).astype(F32), valid1), ((delta2 * dilation).astype(F32), valid2)


DIL_UNROLL = 2


def _dil_units(S, dilation, unit):
    nb = S // dilation // BLOCK
    span = dilation * BLOCK

    def first_blocks(r, carry):
        unit(r, r, BLOCK, True)
        return carry

    lax.fori_loop(0, dilation, first_blocks, 0, unroll=min(dilation, DIL_UNROLL))
    if nb > 1:
        def later_blocks(i, carry):
            r, n = i // (nb - 1), i % (nb - 1) + 1
            unit(r + n * span, r + (n - 1) * span, 2 * BLOCK, False)
            return carry

        lax.fori_loop(0, dilation * (nb - 1), later_blocks, 0, unroll=DIL_UNROLL)


def _dilated_fwd(z, *, B, S, name):
    def body(slope_ref, q_ref, k_ref, v_ref, y_ref, lse_ref, qf, kf, vf, acc_o, acc_l):
        (m_first, m_second), first = _pair_masks()
        p = pl.program_id(1)
        qf[...] = q_ref[...].astype(F32)
        kf[...] = k_ref[...].astype(F32)
        vf[...] = v_ref[...].astype(F32)
        acc_o[...] = jnp.zeros_like(acc_o)
        acc_l[...] = jnp.full_like(acc_l, NEG)

        for _, dilation in DIL_CONFIGS:
            (dist1, valid1), (dist2, valid2) = _dil_bias(dilation)
            bias1 = [jnp.where(valid1, -slope_ref[2 * p + e] * dist1, NEG) for e in (0, 1)]
            bias2 = [jnp.where(valid2, -slope_ref[2 * p + e] * dist2, NEG) for e in (0, 1)]

            def unit(qs, ks, klen, first_block, dilation=dilation, bias1=bias1, bias2=bias2):
                bias = bias1 if first_block else bias2
                rows = pl.ds(qs, BLOCK, stride=dilation)
                krows = pl.ds(ks, klen, stride=dilation)
                q2 = qf[rows, :].astype(BF16)
                kk = kf[krows, :].astype(BF16)
                vv = vf[krows, :].astype(BF16)
                outs, lses = [], []
                for e, hm in enumerate((m_first, m_second)):
                    s = _dot(q2 * hm, kk, NT) * 0.125 + bias[e]
                    m = jnp.max(s, axis=1, keepdims=True)
                    pe = jnp.exp(s - m)
                    l = jnp.sum(pe, axis=1, keepdims=True)
                    outs.append(_dot(pe.astype(BF16), vv) * (1.0 / l))
                    lses.append(m + jnp.log(l))
                o_new = jnp.where(first, outs[0], outs[1])
                l_new = jnp.where(first, lses[0], lses[1])
                l_old = acc_l[rows, :]
                m2 = jnp.maximum(l_old, l_new)
                w_old, w_new = jnp.exp(l_old - m2), jnp.exp(l_new - m2)
                tot = w_old + w_new
                acc_o[rows, :] = (w_old * acc_o[rows, :] + w_new * o_new) * (1.0 / tot)
                acc_l[rows, :] = m2 + jnp.log(tot)

            _dil_units(S, dilation, unit)

        y_ref[...] = acc_o[...].astype(BF16)
        lse_ref[...] = acc_l[...]

    spec = lambda off: pl.BlockSpec((S, 128), lambda b, p: (b, 4 * off + p))
    ospec = pl.BlockSpec((S, 128), lambda b, p: (b, p))
    return pl.pallas_call(
        body, grid=(B, 4), in_specs=[_SMEM_SPEC, spec(0), spec(1), spec(2)], out_specs=[ospec, ospec],
        out_shape=[SDS((B * S, WIDTH), BF16), SDS((B * S, WIDTH), F32)],
        scratch_shapes=[pltpu.VMEM((S, 128), F32)] * 5, compiler_params=_cp(), name=name,
    )(_alibi_slopes(), z, z, z)


def _dilated_bwd(z, dy, ya, lse, *, B, S, name):
    def body(slope_ref, q_ref, k_ref, v_ref, do_ref, o_ref, lse_ref, dq_ref, dk_ref, dv_ref,
             qf, kf, vf, dof, ef, dqa, dka, dva):
        (m_first, m_second), first = _pair_masks()
        p = pl.program_id(1)
        qf[...] = q_ref[...].astype(F32)
        kf[...] = k_ref[...].astype(F32)
        vf[...] = v_ref[...].astype(F32)
        dov = do_ref[...].astype(F32)
        dof[...] = dov
        prod = dov * o_ref[...].astype(F32)
        ef[...] = jnp.where(first, jnp.sum(jnp.where(first, prod, 0.0), axis=1, keepdims=True),
                            jnp.sum(jnp.where(first, 0.0, prod), axis=1, keepdims=True))
        dqa[...] = jnp.zeros_like(dqa)
        dka[...] = jnp.zeros_like(dka)
        dva[...] = jnp.zeros_like(dva)

        for _, dilation in DIL_CONFIGS:
            (dist1, valid1), (dist2, valid2) = _dil_bias(dilation)
            bias1 = [jnp.where(valid1, -slope_ref[2 * p + e] * dist1, NEG) for e in (0, 1)]
            bias2 = [jnp.where(valid2, -slope_ref[2 * p + e] * dist2, NEG) for e in (0, 1)]

            def unit(qs, ks, klen, first_block, dilation=dilation, bias1=bias1, bias2=bias2):
                bias = bias1 if first_block else bias2
                rows = pl.ds(qs, BLOCK, stride=dilation)
                krows = pl.ds(ks, klen, stride=dilation)
                q2 = qf[rows, :].astype(BF16)
                do2 = dof[rows, :].astype(BF16)
                kk = kf[krows, :].astype(BF16)
                vv = vf[krows, :].astype(BF16)
                lse2, e2 = lse_ref[rows, :], ef[rows, :]
                dq_pair = None
                dk_c = jnp.zeros((klen, 128), F32)
                dv_c = jnp.zeros((klen, 128), F32)
                for e, hm in enumerate((m_first, m_second)):
                    lane0 = slice(HEAD_DIM * e, HEAD_DIM * e + 1)
                    qm, dom = q2 * hm, do2 * hm
                    s = _dot(qm, kk, NT) * 0.125 + bias[e]
                    pe = jnp.exp(s - lse2[:, lane0])
                    ds = (pe * (_dot(dom, vv, NT) - e2[:, lane0])).astype(BF16)
                    dqe = _dot(ds, kk)
                    dq_pair = dqe if e == 0 else jnp.where(first, dq_pair, dqe)
                    dk_c = dk_c + _dot(ds, qm, TN)
                    dv_c = dv_c + _dot(pe.astype(BF16), dom, TN)
                dqa[rows, :] += dq_pair * 0.125
                dka[krows, :] += dk_c * 0.125
                dva[krows, :] += dv_c

            _dil_units(S, dilation, unit)

        dq_ref[...] = dqa[...].astype(BF16)
        dk_ref[...] = dka[...].astype(BF16)
        dv_ref[...] = dva[...].astype(BF16)

    spec = lambda off: pl.BlockSpec((S, 128), lambda b, p: (b, 4 * off + p))
    ospec = pl.BlockSpec((S, 128), lambda b, p: (b, p))
    return pl.pallas_call(
        body, grid=(B, 4), in_specs=[_SMEM_SPEC, spec(0), spec(1), spec(2), ospec, ospec, ospec],
        out_specs=[ospec] * 3, out_shape=[SDS((B * S, WIDTH), BF16)] * 3,
        scratch_shapes=[pltpu.VMEM((S, 128), F32)] * 8, compiler_params=_cp(), name=name,
    )(_alibi_slopes(), z, z, z, dy, ya, lse)


FOX_TQ = 256


def _fox_fwd(z, cc, cr, *, B, S, name):
    def body(q_ref, k_ref, v_ref, cc_ref, cr_ref, o_ref):
        (m_first, m_second), first = _pair_masks()
        for qi in range(S // FOX_TQ):
            r0, kend = qi * FOX_TQ, (qi + 1) * FOX_TQ
            q2 = q_ref[r0:kend, :]
            kk, vv = k_ref[0:kend, :], v_ref[0:kend, :]
            row = lax.broadcasted_iota(jnp.int32, (FOX_TQ, kend), 0) + r0
            col = lax.broadcasted_iota(jnp.int32, (FOX_TQ, kend), 1)
            causal = col <= row
            outs = []
            for e, hm in enumerate((m_first, m_second)):
                s = _dot(q2 * hm, kk, NT) * 0.125
                s = s + (cc_ref[r0:kend, :][:, HEAD_DIM * e:HEAD_DIM * e + 1] - cr_ref[e:e + 1, 0:kend])
                s = jnp.where(causal, s, NEG)
                m = jnp.max(s, axis=1, keepdims=True)
                pe = jnp.exp(s - m)
                l = jnp.sum(pe, axis=1, keepdims=True)
                outs.append(_dot(pe.astype(BF16), vv) * (1.0 / l))
            o_ref[r0:kend, :] = jnp.where(first, outs[0], outs[1]).astype(BF16)

    spec = lambda off: pl.BlockSpec((S, 128), lambda b, p: (b, 4 * off + p))
    return pl.pallas_call(
        body, grid=(B, 4),
        in_specs=[spec(3), spec(4), spec(5), pl.BlockSpec((S, 128), lambda b, p: (b, p)),
                  pl.BlockSpec((None, 8, S), lambda b, p: (4 * b + p, 0, 0))],
        out_specs=pl.BlockSpec((S, 128), lambda b, p: (b, p)), out_shape=SDS((B * S, WIDTH), BF16),
        compiler_params=_cp(), name=name,
    )(z, z, z, cc, cr)


def _fox_bwd(z, dy, cc, cr, *, B, S, name):
    def body(q_ref, k_ref, v_ref, do_ref, cc_ref, cr_ref, dq_ref, dk_ref, dv_ref, dc_ref, dk_s, dv_s, dc_s):
        (m_first, m_second), first = _pair_masks()
        dk_s[...] = jnp.zeros_like(dk_s)
        dv_s[...] = jnp.zeros_like(dv_s)
        dc_s[...] = jnp.zeros_like(dc_s)
        for qi in range(S // FOX_TQ):
            r0, kend = qi * FOX_TQ, (qi + 1) * FOX_TQ
            q2, do2 = q_ref[r0:kend, :], do_ref[r0:kend, :]
            kk, vv = k_ref[0:kend, :], v_ref[0:kend, :]
            krow = lax.broadcasted_iota(jnp.int32, (kend, FOX_TQ), 0)
            qcol = lax.broadcasted_iota(jnp.int32, (kend, FOX_TQ), 1) + r0
            causal = krow <= qcol
            dq_t = jnp.zeros((FOX_TQ, 128), F32)
            for e, hm in enumerate((m_first, m_second)):
                sel = first if e == 0 else ~first
                km = kk * hm
                st = _dot(km, q2, NT) * 0.125
                st = st + (cr_ref[e:e + 1, r0:kend] - cc_ref[0:kend, :][:, HEAD_DIM * e:HEAD_DIM * e + 1])
                st = jnp.where(causal, st, NEG)
                pt = jnp.exp(st - jnp.max(st, axis=0, keepdims=True))
                pt = pt * (1.0 / jnp.sum(pt, axis=0, keepdims=True))
                dpt = _dot(vv * hm, do2, NT)
                dst = pt * (dpt - jnp.sum(pt * dpt, axis=0, keepdims=True))
                dsb = dst.astype(BF16)
                dv_s[0:kend, :] += _dot(pt.astype(BF16), do2 * hm)
                dk_s[0:kend, :] += _dot(dsb, q2 * hm) * 0.125
                dq_t = dq_t + _dot(dsb, km, TN)
                dc_s[0:kend, :] += jnp.where(sel, -jnp.sum(dst, axis=1, keepdims=True), 0.0)
            dq_ref[r0:kend, :] = (dq_t * 0.125).astype(BF16)
        dk_ref[...] = dk_s[...].astype(BF16)
        dv_ref[...] = dv_s[...].astype(BF16)
        dc_ref[...] = dc_s[...]

    spec = lambda off: pl.BlockSpec((S, 128), lambda b, p: (b, 4 * off + p))
    pspec = pl.BlockSpec((S, 128), lambda b, p: (b, p))
    return pl.pallas_call(
        body, grid=(B, 4),
        in_specs=[spec(3), spec(4), spec(5), pl.BlockSpec((S, 128), lambda b, p: (b, 4 + p)), pspec,
                  pl.BlockSpec((None, 8, S), lambda b, p: (4 * b + p, 0, 0))],
        out_specs=[pspec] * 4,
        out_shape=[SDS((B * S, WIDTH), BF16)] * 3 + [SDS((B * S, WIDTH), F32)],
        scratch_shapes=[pltpu.VMEM((S, 128), F32)] * 3, compiler_params=_cp(), name=name,
    )(z, z, z, dy, cc, cr)


def _xattn_fwd(q, kv, *, B, S, M, tq, name):
    D = D_MODEL

    def body(q_ref, kv_ref, o_ref):
        for h in range(N_XH):
            cs = slice(XHD * h, XHD * (h + 1))
            s = _dot(q_ref[:, cs], kv_ref[:, cs], NT) * (1.0 / 16.0)
            pe = jnp.exp(s - jnp.max(s, axis=1, keepdims=True))
            l = jnp.sum(pe, axis=1, keepdims=True)
            o_ref[:, cs] = (_dot(pe.astype(BF16), kv_ref[:, D + XHD * h:D + XHD * (h + 1)]) * (1.0 / l)).astype(BF16)

    nq = S // tq
    return pl.pallas_call(
        body, grid=(B, nq),
        in_specs=[pl.BlockSpec((tq, D), lambda b, t: (b * nq + t, 0)), pl.BlockSpec((M, 2 * D), lambda b, t: (b, 0))],
        out_specs=pl.BlockSpec((tq, D), lambda b, t: (b * nq + t, 0)), out_shape=SDS((B * S, D), BF16),
        compiler_params=_cp(), name=name,
    )(q, kv)


def _xattn_bwd(q, kv, do, *, B, S, M, tq, name):
    D = D_MODEL

    def body(q_ref, kv_ref, do_ref, dq_ref, dkv_ref):
        t = pl.program_id(1)

        @pl.when(t == 0)
        def _():
            dkv_ref[...] = jnp.zeros_like(dkv_ref)

        for h in range(N_XH):
            cs = slice(XHD * h, XHD * (h + 1))
            vs = slice(D + XHD * h, D + XHD * (h + 1))
            qh, kh, vh, doh = q_ref[:, cs], kv_ref[:, cs], kv_ref[:, vs], do_ref[:, cs]
            s = _dot(qh, kh, NT) * (1.0 / 16.0)
            pe = jnp.exp(s - jnp.max(s, axis=1, keepdims=True))
            pe = pe * (1.0 / jnp.sum(pe, axis=1, keepdims=True))
            dp = _dot(doh, vh, NT)
            ds = (pe * (dp - jnp.sum(pe * dp, axis=1, keepdims=True))).astype(BF16)
            dq_ref[:, cs] = (_dot(ds, kh) * (1.0 / 16.0)).astype(BF16)
            dkv_ref[:, cs] += _dot(ds, qh, TN) * (1.0 / 16.0)
            dkv_ref[:, vs] += _dot(pe.astype(BF16), doh, TN)

    nq = S // tq
    qspec = pl.BlockSpec((tq, D), lambda b, t: (b * nq + t, 0))
    kvspec = pl.BlockSpec((M, 2 * D), lambda b, t: (b, 0))
    return pl.pallas_call(
        body, grid=(B, nq), in_specs=[qspec, kvspec, qspec], out_specs=[qspec, kvspec],
        out_shape=[SDS((B * S, D), BF16), SDS((B * M, 2 * D), F32)], compiler_params=_cp(), name=name,
    )(q, kv, do)


def _assemble_dz(parts, dgate, *, tm, name):
    T = dgate.shape[0]

    def body(*refs):
        o_ref = refs[-1]
        for j in range(6):
            o_ref[:, WIDTH * j:WIDTH * (j + 1)] = refs[j][...]
        o_ref[:, QKV_W:IN_PAD] = refs[6][...]

    wspec = pl.BlockSpec((tm, WIDTH), lambda i: (i, 0))
    return pl.pallas_call(
        body, grid=(T // tm,), in_specs=[wspec] * 6 + [pl.BlockSpec((tm, 128), lambda i: (i, 0))],
        out_specs=pl.BlockSpec((tm, IN_PAD), lambda i: (i, 0)), out_shape=SDS((T, IN_PAD), BF16),
        compiler_params=_cp(), name=name,
    )(*parts, dgate)


def _adamw(parts, w, m, v, *, tr, name):
    R, C = w.shape

    def body(p_ref, w_ref, m_ref, v_ref, g_ref, d_ref, nm_ref, nv_ref):
        g = p_ref[0].astype(F32)
        for d in range(1, N_DEV):
            g = g + p_ref[d].astype(F32)
        m2 = ADAM_B1 * m_ref[...] + (1.0 - ADAM_B1) * g
        v2 = ADAM_B2 * v_ref[...] + (1.0 - ADAM_B2) * (g * g)
        m_hat = m2 / (1.0 - ADAM_B1 ** ADAM_STEP)
        v_hat = v2 / (1.0 - ADAM_B2 ** ADAM_STEP)
        g_ref[...] = g
        d_ref[...] = -ADAM_LR * (m_hat / (jnp.sqrt(v_hat) + ADAM_EPS) + ADAM_WD * w_ref[...])
        nm_ref[...] = m2
        nv_ref[...] = v2

    spec = pl.BlockSpec((tr, C), lambda i: (i, 0))
    return pl.pallas_call(
        body, grid=(R // tr,), in_specs=[pl.BlockSpec((N_DEV, tr, C), lambda i: (0, i, 0)), spec, spec, spec],
        out_specs=[spec] * 4, out_shape=[SDS((R, C), F32)] * 4, compiler_params=_cp(), name=name,
    )(parts, w, m, v)


def _exchange(arrays, modes, *, name):
    n = len(arrays)
    out_shape = [SDS((N_DEV,) + a.shape if md == "gather" else a.shape, a.dtype) for a, md in zip(arrays, modes)]

    def body(*refs):
        ins, outs = refs[:n], refs[n:2 * n]
        send_sems, recv_sems, local_sems = refs[2 * n:]
        x, y, c = (lax.axis_index(a) for a in AXES)
        me = 4 * x + 2 * y + c
        copies = []
        for i, md in enumerate(modes):
            src = ins[i] if md == "gather" else ins[i].at[me]
            cp = pltpu.make_async_copy(src, outs[i].at[me], local_sems.at[i])
            cp.start()
            copies.append(cp)
            for k in range(1, N_DEV):
                px = 1 - x if k & 4 else x
                py = 1 - y if k & 2 else y
                pc = 1 - c if k & 1 else c
                src = ins[i] if md == "gather" else ins[i].at[4 * px + 2 * py + pc]
                cp = pltpu.make_async_remote_copy(
                    src_ref=src, dst_ref=outs[i].at[me], send_sem=send_sems.at[i, k - 1], recv_sem=recv_sems.at[i, k - 1],
                    device_id=(px, py, pc), device_id_type=pl.DeviceIdType.MESH)
                cp.start()
                copies.append(cp)
        for cp in copies:
            cp.wait()

    anyspec = pl.BlockSpec(memory_space=pl.ANY)
    return pl.pallas_call(
        body, in_specs=[anyspec] * n, out_specs=[anyspec] * n, out_shape=out_shape,
        scratch_shapes=[pltpu.SemaphoreType.DMA((n, N_DEV - 1)), pltpu.SemaphoreType.DMA((n, N_DEV - 1)),
                        pltpu.SemaphoreType.DMA((n,))],
        name=name,
    )(*arrays)


def _local_step(x, mem, g_mix, b_forget, g_xattn, g_mem, g_mlp, g_final, target,
                w_in_pad, w_out, w_xq, w_kv, w_xo, w_up, w_down):
    B, S, D = x.shape
    M = mem.shape[1]
    T = B * S
    x0 = x.reshape(T, D)
    mem2 = mem.reshape(B * M, D)
    tgt = target.reshape(T, D)
    b_pad = jnp.pad(b_forget, (0, 120)).reshape(1, 128)

    h1, z = _rms_matmul(x0, g_mix, w_in_pad[:, :QKV_W], tm=1024, tn=768, out_dtype=BF16, name="f_in")
    gate = _matmul_nn(h1, w_in_pad[:, QKV_W:], tm=1024, tn=128, tk=D, out_dtype=F32, name="f_gate")
    c, cc = _gate_fwd(gate, b_pad, B=B, S=S, name="f_gatecum")
    cr = jnp.pad(c[:, :8].reshape(B, S, 4, 2).transpose(0, 2, 3, 1), ((0, 0), (0, 0), (0, 6), (0, 0))).reshape(B * 4, 8, S)
    ya, lse = _dilated_fwd(z, B=B, S=S, name="f_dil")
    yf = _fox_fwd(z, cc, cr, B=B, S=S, name="f_fox")
    ymix = jnp.concatenate([ya, yf], axis=1)
    x1 = _matmul_nn(ymix, w_out, res=x0, tm=1024, tn=D, tk=D, out_dtype=F32, name="f_out")
    h2, q = _rms_matmul(x1, g_xattn, w_xq, tm=1024, tn=D, out_dtype=BF16, name="f_xq")
    mn, kv = _rms_matmul(mem2, g_mem, w_kv, tm=B * M, tn=D, out_dtype=BF16, name="f_xkv")
    xo = _xattn_fwd(q, kv, B=B, S=S, M=M, tq=512, name="f_xattn")
    x2 = _matmul_nn(xo, w_xo, res=x1, tm=1024, tn=D, tk=D, out_dtype=F32, name="f_xo")
    h3, act = _rms_matmul(x2, g_mlp, w_up, tm=1024, tn=1024, out_dtype=BF16, relu=True, name="f_up")
    x3 = _matmul_nn(act, w_down, res=x2, square=True, tm=1024, tn=D, tk=1024, out_dtype=F32, name="f_down")
    dx3, dg_final, loss = _loss_head(x3, g_final, tgt, tm=512, name="f_loss")

    du = _matmul_nt(dx3, w_down, mul2a=act, tm=1024, tn=1024, name="b_dact")
    dw_down = _matmul_tn(act, dx3, square=True, bk=1024, bn=D, tt=512, out_dtype=BF16, name="b_wdown")
    dw_up = _matmul_tn(h3, du, bk=D, bn=1024, tt=512, out_dtype=BF16, name="b_wup")
    dx2, dg_mlp = _matmul_nt_rms(du, w_up, x2, g_mlp, dx3, tm=512, tk=1024, name="b_dh3")
    dxo = _matmul_nt(dx2, w_xo, tm=1024, tn=D, name="b_dxo")
    dw_xo = _matmul_tn(xo, dx2, bk=D, bn=D, tt=512, out_dtype=BF16, name="b_wxo")
    dq, dkv = _xattn_bwd(q, kv, dxo, B=B, S=S, M=M, tq=512, name="b_xattn")
    dw_xq = _matmul_tn(h2, dq, bk=D, bn=D, tt=512, out_dtype=BF16, name="b_wxq")
    dx1, dg_xattn = _matmul_nt_rms(dq, w_xq, x1, g_xattn, dx2, tm=512, tk=D, name="b_dh2")
    dw_kv = _matmul_tn(mn, dkv, bk=D, bn=D, tt=min(512, B * M), out_dtype=BF16, name="b_wkv")
    _, dg_mem = _matmul_nt_rms(dkv, w_kv, mem2, g_mem, None, tm=min(512, B * M), tk=D, name="b_dmem")
    dy = _matmul_nt(dx1, w_out, tm=1024, tn=D, name="b_dy")
    dw_out = _matmul_tn(ymix, dx1, bk=D, bn=D, tt=512, out_dtype=BF16, name="b_wout")
    dqf, dkf, dvf, dcc = _fox_bwd(z, dy, cc, cr, B=B, S=S, name="b_fox")
    dgate, db = _gate_bwd(dcc, gate, b_pad, B=B, S=S, name="b_gate")
    dqa, dka, dva = _dilated_bwd(z, dy, ya, lse, B=B, S=S, name="b_dil")
    dz = _assemble_dz([dqa, dka, dva, dqf, dkf, dvf], dgate, tm=512, name="b_dz")
    dw_in = _matmul_tn(h1, dz, bk=D, bn=640, tt=512, out_dtype=BF16, name="b_win")
    gx, dg_mix = _matmul_nt_rms(dz, w_in_pad, x0, g_mix, dx1, tm=512, tk=640, name="b_dh1")

    small = dict(g_mix=dg_mix, b_forget=db, g_xattn=dg_xattn, g_mem=dg_mem, g_mlp=dg_mlp, g_final=dg_final)
    big = dict(w_in=dw_in, w_out=dw_out, w_xq=dw_xq, w_kv=dw_kv, w_xo=dw_xo, w_up=dw_up, w_down=dw_down)
    return gx.reshape(B, S, D), big, small, loss


SMALL_ROWS = ("g_mix", "b_forget", "g_xattn", "g_mem", "g_mlp", "g_final")


def _pack_rows(rows):
    D = D_MODEL
    rows = [jnp.pad(r.reshape(-1), (0, D - r.size)) for r in rows]
    rows += [jnp.zeros((D,), F32)] * (8 - len(rows))
    return jnp.stack(rows)


def kernel(x, mem, g_mix, w_in, b_forget, w_out, g_xattn, g_mem, w_xq, w_xk, w_xv, w_xo, g_mlp, w_up, w_down, g_final, loss_target, m_g_mix, m_w_in, m_b_forget, m_w_out, m_g_xattn, m_g_mem, m_w_xq, m_w_xk, m_w_xv, m_w_xo, m_g_mlp, m_w_up, m_w_down, m_g_final, v_g_mix, v_w_in, v_b_forget, v_w_out, v_g_xattn, v_g_mem, v_w_xq, v_w_xk, v_w_xv, v_w_xo, v_g_mlp, v_w_up, v_w_down, v_g_final):
    D = D_MODEL
    W = dict(w_in=w_in, w_out=w_out, w_xq=w_xq, w_xk=w_xk, w_xv=w_xv, w_xo=w_xo, w_up=w_up, w_down=w_down)
    Mo = dict(w_in=m_w_in, w_out=m_w_out, w_xq=m_w_xq, w_xk=m_w_xk, w_xv=m_w_xv, w_xo=m_w_xo, w_up=m_w_up, w_down=m_w_down)
    Vo = dict(w_in=v_w_in, w_out=v_w_out, w_xq=v_w_xq, w_xk=v_w_xk, w_xv=v_w_xv, w_xo=v_w_xo, w_up=v_w_up, w_down=v_w_down)
    names = list(W)

    gathered = dict(zip(names, _exchange([W[n].astype(BF16) for n in names], ["gather"] * len(names), name="gather_weights")))
    cols = lambda g: g.transpose(1, 0, 2).reshape(g.shape[1], -1)
    rows = lambda g: g.reshape(-1, g.shape[2])
    w_in_pad = jnp.pad(cols(gathered["w_in"]), ((0, 0), (0, IN_PAD - IN_W)))
    w_kv = jnp.concatenate([rows(gathered["w_xk"]), rows(gathered["w_xv"])], axis=1)

    gx, big, small, loss = _local_step(
        x, mem, g_mix, b_forget, g_xattn, g_mem, g_mlp, g_final, loss_target,
        w_in_pad, rows(gathered["w_out"]), rows(gathered["w_xq"]), w_kv, rows(gathered["w_xo"]),
        cols(gathered["w_up"]), rows(gathered["w_down"]))

    by_cols = lambda g, n: g[:, :n * N_DEV].reshape(g.shape[0], N_DEV, n).transpose(1, 0, 2)
    by_rows = lambda g: g.reshape(N_DEV, g.shape[0] // N_DEV, g.shape[1])
    parts = dict(w_in=by_cols(big["w_in"], IN_W // N_DEV), w_out=by_rows(big["w_out"]), w_xq=by_rows(big["w_xq"]),
                 w_xk=by_rows(big["w_kv"][:, :D]), w_xv=by_rows(big["w_kv"][:, D:]), w_xo=by_rows(big["w_xo"]),
                 w_up=by_cols(big["w_up"], D_FF // N_DEV), w_down=by_rows(big["w_down"]))
    packed = _pack_rows([small[n] for n in SMALL_ROWS] + [loss[0, :1]])
    received = _exchange([parts[n] for n in names] + [packed], ["scatter"] * len(names) + ["gather"], name="exchange_grads")
    received, packed_all = dict(zip(names, received[:-1])), received[-1]

    res = {n: _adamw(received[n], W[n], Mo[n], Vo[n], tr=128, name=f"adamw_{n}") for n in names}
    small_w = dict(g_mix=g_mix, b_forget=b_forget, g_xattn=g_xattn, g_mem=g_mem, g_mlp=g_mlp, g_final=g_final)
    small_m = dict(g_mix=m_g_mix, b_forget=m_b_forget, g_xattn=m_g_xattn, g_mem=m_g_mem, g_mlp=m_g_mlp, g_final=m_g_final)
    small_v = dict(g_mix=v_g_mix, b_forget=v_b_forget, g_xattn=v_g_xattn, g_mem=v_g_mem, g_mlp=v_g_mlp, g_final=v_g_final)
    sres = _adamw(packed_all, _pack_rows([small_w[n] for n in SMALL_ROWS]), _pack_rows([small_m[n] for n in SMALL_ROWS]),
                  _pack_rows([small_v[n] for n in SMALL_ROWS]), tr=8, name="adamw_small")
    for i, n in enumerate(SMALL_ROWS):
        res[n] = [r[i, :small_w[n].size] for r in sres]
    loss_total = sres[0][6, 0]

    order = ["g_mix", "w_in", "b_forget", "w_out", "g_xattn", "g_mem", "w_xq", "w_xk", "w_xv", "w_xo", "g_mlp", "w_up", "w_down", "g_final"]
    return (loss_total, gx, *[res[n][0] for n in order], *[res[n][1] for n in order],
            *[res[n][2] for n in order], *[res[n][3] for n in order])
```

```python
import jax
import jax.numpy as jnp
from jax import lax
from jax.experimental import pallas as pl
from jax.experimental.pallas import tpu as pltpu

F32, BF16 = jnp.float32, jnp.bfloat16
SDS = jax.ShapeDtypeStruct

D_MODEL = 1024
HEAD_DIM = 64
WIDTH = 512
QKV_W = 6 * WIDTH
IN_W = QKV_W + 8
IN_PAD = QKV_W + 128
BLOCK = 128
DIL_CONFIGS = ((128, 1), (512, 4), (2048, 16))
N_XH, XHD = 4, 256
D_FF = 4096
EPS = 1e-6
NEG = -1e30
N_DEV = 8
AXES = ("x", "y", "c")

ADAM_LR, ADAM_B1, ADAM_B2, ADAM_EPS, ADAM_WD, ADAM_STEP = 0.001, 0.9, 0.999, 1e-08, 0.01, 10

VMEM_CAP_V7X = 64 * 1024 * 1024
VMEM_LIMIT = VMEM_CAP_V7X * 7 // 8

NT = (((1,), (1,)), ((), ()))
TN = (((0,), (0,)), ((), ()))


def _cp(**kw):
    return pltpu.CompilerParams(vmem_limit_bytes=VMEM_LIMIT, **kw)


def _dot(a, b, dims=None):
    if dims is None:
        return jnp.dot(a, b, preferred_element_type=F32)
    return lax.dot_general(a, b, dims, preferred_element_type=F32)


def _rstd(xv):
    return lax.rsqrt(jnp.mean(xv * xv, axis=-1, keepdims=True) + EPS)


def _rms_bwd(dh, xv, g):
    r = _rstd(xv)
    xhat = xv * r
    dxhat = dh * g
    dx = r * (dxhat - xhat * jnp.mean(dxhat * xhat, axis=-1, keepdims=True))
    return dx, jnp.sum(dh * xhat, axis=0, keepdims=True)


def _rms_matmul(x, g, w, *, tm, tn, out_dtype, relu=False, name):
    T, D = x.shape
    N = w.shape[1]

    def body(x_ref, g_ref, w_ref, h_ref, o_ref, h_s):
        @pl.when(pl.program_id(1) == 0)
        def _():
            xv = x_ref[...]
            h = (xv * _rstd(xv) * g_ref[...]).astype(BF16)
            h_s[...] = h
            h_ref[...] = h

        acc = _dot(h_s[...], w_ref[...])
        if relu:
            acc = jnp.maximum(acc, 0.0)
        o_ref[...] = acc.astype(out_dtype)

    return pl.pallas_call(
        body, grid=(T // tm, N // tn),
        in_specs=[pl.BlockSpec((tm, D), lambda i, j: (i, 0)), pl.BlockSpec((1, D), lambda i, j: (0, 0)),
                  pl.BlockSpec((D, tn), lambda i, j: (0, j))],
        out_specs=[pl.BlockSpec((tm, D), lambda i, j: (i, 0)), pl.BlockSpec((tm, tn), lambda i, j: (i, j))],
        out_shape=[SDS((T, D), BF16), SDS((T, N), out_dtype)],
        scratch_shapes=[pltpu.VMEM((tm, D), BF16)], compiler_params=_cp(), name=name,
    )(x, g.reshape(1, D), w)


def _matmul_nn(a, w, *, res=None, square=False, tm, tn, tk, out_dtype, name):
    T, K = a.shape
    N = w.shape[1]
    nk = K // tk

    def body(*refs):
        a_ref, w_ref = refs[0], refs[1]
        res_ref = refs[2] if res is not None else None
        o_ref, acc = refs[-2], refs[-1]
        k = pl.program_id(2)

        @pl.when(k == 0)
        def _():
            acc[...] = jnp.zeros_like(acc)

        av = a_ref[...]
        if square:
            af = av.astype(F32)
            av = (af * af).astype(BF16)
        acc[...] += _dot(av, w_ref[...])

        @pl.when(k == nk - 1)
        def _():
            r = acc[...]
            if res_ref is not None:
                r = res_ref[...] + r
            o_ref[...] = r.astype(out_dtype)

    in_specs = [pl.BlockSpec((tm, tk), lambda i, j, k: (i, k)), pl.BlockSpec((tk, tn), lambda i, j, k: (k, j))]
    args = [a, w]
    if res is not None:
        in_specs.append(pl.BlockSpec((tm, tn), lambda i, j, k: (i, j)))
        args.append(res)
    return pl.pallas_call(
        body, grid=(T // tm, N // tn, nk), in_specs=in_specs,
        out_specs=pl.BlockSpec((tm, tn), lambda i, j, k: (i, j)), out_shape=SDS((T, N), out_dtype),
        scratch_shapes=[pltpu.VMEM((tm, tn), F32)], compiler_params=_cp(), name=name,
    )(*args)


def _matmul_nt(g, w, *, mul2a=None, tm, tn, name):
    T, K = g.shape
    N = w.shape[0]

    def body(*refs):
        g_ref, w_ref = refs[0], refs[1]
        o_ref = refs[-1]
        acc = _dot(g_ref[...].astype(BF16), w_ref[...], NT)
        if mul2a is not None:
            acc = acc * (2.0 * refs[2][...].astype(F32))
        o_ref[...] = acc.astype(BF16)

    in_specs = [pl.BlockSpec((tm, K), lambda i, j: (i, 0)), pl.BlockSpec((tn, K), lambda i, j: (j, 0))]
    args = [g, w]
    if mul2a is not None:
        in_specs.append(pl.BlockSpec((tm, tn), lambda i, j: (i, j)))
        args.append(mul2a)
    return pl.pallas_call(
        body, grid=(T // tm, N // tn), in_specs=in_specs,
        out_specs=pl.BlockSpec((tm, tn), lambda i, j: (i, j)), out_shape=SDS((T, N), BF16),
        compiler_params=_cp(), name=name,
    )(*args)


def _matmul_nt_rms(g, w, x, gain, dres, *, tm, tk, name):
    T, K = g.shape
    D = w.shape[0]
    nk = K // tk
    nt = T // tm

    def body(*refs):
        g_ref, w_ref, x_ref, gain_ref = refs[:4]
        dres_ref = refs[4] if dres is not None else None
        dx_ref, dg_ref, acc = refs[-3], refs[-2], refs[-1]
        i, k = pl.program_id(0), pl.program_id(1)

        @pl.when(k == 0)
        def _():
            acc[...] = jnp.zeros_like(acc)

        acc[...] += _dot(g_ref[...].astype(BF16), w_ref[...], NT)

        @pl.when(k == nk - 1)
        def _():
            dx, dg = _rms_bwd(acc[...], x_ref[...], gain_ref[...])
            if dres_ref is not None:
                dx = dres_ref[...] + dx
            dx_ref[...] = dx

            @pl.when(i == 0)
            def _():
                dg_ref[...] = dg

            @pl.when(i > 0)
            def _():
                dg_ref[...] += dg

    in_specs = [pl.BlockSpec((tm, tk), lambda i, k: (i, k)), pl.BlockSpec((D, tk), lambda i, k: (0, k)),
                pl.BlockSpec((tm, D), lambda i, k: (i, 0)), pl.BlockSpec((1, D), lambda i, k: (0, 0))]
    args = [g, w, x, gain.reshape(1, D)]
    if dres is not None:
        in_specs.append(pl.BlockSpec((tm, D), lambda i, k: (i, 0)))
        args.append(dres)
    return pl.pallas_call(
        body, grid=(nt, nk), in_specs=in_specs,
        out_specs=[pl.BlockSpec((tm, D), lambda i, k: (i, 0)), pl.BlockSpec((1, D), lambda i, k: (0, 0))],
        out_shape=[SDS((T, D), F32), SDS((1, D), F32)],
        scratch_shapes=[pltpu.VMEM((tm, D), F32)], compiler_params=_cp(), name=name,
    )(*args)


def _matmul_tn(a, g, *, square=False, bk, bn, tt, out_dtype, name):
    T, K = a.shape
    N = g.shape[1]
    nt = T // tt

    def body(a_ref, g_ref, o_ref, acc):
        t = pl.program_id(2)

        @pl.when(t == 0)
        def _():
            acc[...] = jnp.zeros_like(acc)

        av = a_ref[...]
        if square:
            af = av.astype(F32)
            av = (af * af).astype(BF16)
        acc[...] += _dot(av, g_ref[...].astype(BF16), TN)

        @pl.when(t == nt - 1)
        def _():
            o_ref[...] = acc[...].astype(out_dtype)

    return pl.pallas_call(
        body, grid=(K // bk, N // bn, nt),
        in_specs=[pl.BlockSpec((tt, bk), lambda i, j, t: (t, i)), pl.BlockSpec((tt, bn), lambda i, j, t: (t, j))],
        out_specs=pl.BlockSpec((bk, bn), lambda i, j, t: (i, j)), out_shape=SDS((K, N), out_dtype),
        scratch_shapes=[pltpu.VMEM((bk, bn), F32)], compiler_params=_cp(), name=name,
    )(a, g)


def _loss_head(x3, g_final, target, *, tm, name):
    T, D = x3.shape

    def body(x_ref, g_ref, t_ref, dx_ref, dg_ref, loss_ref):
        i = pl.program_id(0)
        xv, g = x_ref[...], g_ref[...]
        r = _rstd(xv)
        xhat = xv * r
        diff = xhat * g - t_ref[...]
        part = 0.5 * jnp.sum(jnp.mean(diff * diff, axis=-1, keepdims=True), axis=0, keepdims=True)
        dy = diff * (1.0 / D)
        dxhat = dy * g
        dx_ref[...] = r * (dxhat - xhat * jnp.mean(dxhat * xhat, axis=-1, keepdims=True))
        dg = jnp.sum(dy * xhat, axis=0, keepdims=True)
        lp = jnp.broadcast_to(part, loss_ref.shape)

        @pl.when(i == 0)
        def _():
            dg_ref[...] = dg
            loss_ref[...] = lp

        @pl.when(i > 0)
        def _():
            dg_ref[...] += dg
            loss_ref[...] += lp

    return pl.pallas_call(
        body, grid=(T // tm,),
        in_specs=[pl.BlockSpec((tm, D), lambda i: (i, 0)), pl.BlockSpec((1, D), lambda i: (0, 0)),
                  pl.BlockSpec((tm, D), lambda i: (i, 0))],
        out_specs=[pl.BlockSpec((tm, D), lambda i: (i, 0)), pl.BlockSpec((1, D), lambda i: (0, 0)),
                   pl.BlockSpec((8, 128), lambda i: (0, 0))],
        out_shape=[SDS((T, D), F32), SDS((1, D), F32), SDS((8, 128), F32)],
        compiler_params=_cp(), name=name,
    )(x3, g_final.reshape(1, D), target)


def _head_lanes(shape, width):
    return lax.broadcasted_iota(jnp.int32, shape, len(shape) - 1) // width


def _gate_fwd(gate, b_pad, *, B, S, name):
    def body(g_ref, b_ref, c_ref, cc_ref):
        xv = g_ref[...] + b_ref[...]
        lf = jnp.minimum(xv, 0.0) - jnp.log(1.0 + jnp.exp(-jnp.abs(xv)))
        lane = lax.broadcasted_iota(jnp.int32, lf.shape, 1)
        row = lax.broadcasted_iota(jnp.int32, lf.shape, 0)
        c = jnp.where(lane < 8, lf, 0.0)
        sh = 1
        while sh < S:
            c = c + jnp.where(row >= sh, pltpu.roll(c, sh, 0), 0.0)
            sh *= 2
        c_ref[...] = c
        grp = _head_lanes((S, WIDTH), HEAD_DIM)
        cc = jnp.zeros((S, WIDTH), F32)
        for h in range(8):
            cc = jnp.where(grp == h, c[:, h:h + 1], cc)
        cc_ref[...] = cc

    return pl.pallas_call(
        body, grid=(B,),
        in_specs=[pl.BlockSpec((S, 128), lambda b: (b, 0)), pl.BlockSpec((1, 128), lambda b: (0, 0))],
        out_specs=[pl.BlockSpec((S, 128), lambda b: (b, 0)), pl.BlockSpec((S, WIDTH), lambda b: (b, 0))],
        out_shape=[SDS((B * S, 128), F32), SDS((B * S, WIDTH), F32)],
        compiler_params=_cp(), name=name,
    )(gate, b_pad)


def _gate_bwd(dcc, gate, b_pad, *, B, S, name):
    def body(dcc_ref, g_ref, b_ref, dg_ref, db_ref):
        bi = pl.program_id(0)
        dccv = dcc_ref[...]
        lane = lax.broadcasted_iota(jnp.int32, (S, 128), 1)
        row = lax.broadcasted_iota(jnp.int32, (S, 128), 0)
        dc = jnp.zeros((S, 128), F32)
        for h in range(8):
            dc = jnp.where(lane == h, dccv[:, HEAD_DIM * h:HEAD_DIM * h + 1], dc)
        sh = 1
        while sh < S:
            dc = dc + jnp.where(row < S - sh, pltpu.roll(dc, S - sh, 0), 0.0)
            sh *= 2
        xv = g_ref[...] + b_ref[...]
        dgate = jnp.where(lane < 8, dc / (1.0 + jnp.exp(xv)), 0.0)
        dg_ref[...] = dgate.astype(BF16)
        db = jnp.sum(dgate, axis=0, keepdims=True)

        @pl.when(bi == 0)
        def _():
            db_ref[...] = db

        @pl.when(bi > 0)
        def _():
            db_ref[...] += db

    return pl.pallas_call(
        body, grid=(B,),
        in_specs=[pl.BlockSpec((S, WIDTH), lambda b: (b, 0)), pl.BlockSpec((S, 128), lambda b: (b, 0)),
                  pl.BlockSpec((1, 128), lambda b: (0, 0))],
        out_specs=[pl.BlockSpec((S, 128), lambda b: (b, 0)), pl.BlockSpec((1, 128), lambda b: (0, 0))],
        out_shape=[SDS((B * S, 128), BF16), SDS((1, 128), F32)],
        compiler_params=_cp(), name=name,
    )(dcc, gate, b_pad)


_SMEM_SPEC = pl.BlockSpec(memory_space=pltpu.SMEM)


def _alibi_slopes():
    return 2.0 ** (-(jnp.arange(1, 9, dtype=F32) * (8.0 / 8)))


def _pair_masks():
    lane = lax.broadcasted_iota(jnp.int32, (1, 128), 1)
    first = lane < HEAD_DIM
    return (first.astype(BF16), (~first).astype(BF16)), first


BNT =(((2,), (2,)), ((0,), (0,)))
BNN = (((2,), (1,)), ((0,), (0,)))
BTN = (((1,), (1,)), ((0,), (0,)))


def _band_bias(slope, dilation):
    qi = lax.broadcasted_iota(jnp.int32, (BLOCK, BLOCK), 0)
    kj = lax.broadcasted_iota(jnp.int32, (BLOCK, BLOCK), 1)
    cur = jnp.where(kj <= qi, (-slope * dilation) * (qi - kj).astype(F32), NEG)
    prev = jnp.where(kj >= qi, (-slope * dilation) * (qi + BLOCK - kj).astype(F32), NEG)
    return cur, prev


def _to_residue_major(dst, src_f32, dilation, nb, lead=0):
    L = nb * BLOCK
    for r in range(dilation):
        rows = src_f32[pl.ds(r, L, stride=dilation), :] if dilation > 1 else src_f32[...]
        dst[lead + r * nb:lead + (r + 1) * nb] = rows.reshape(nb, BLOCK, 128).astype(dst.dtype)


def _dil_attn_fwd(z, *, B, S, name):
    NB = S // BLOCK

    def body(slope_ref, q_ref, k_ref, v_ref, y_ref, lse_ref, qf, kf, vf, qd, kd, vd, od, ld, acc_o, acc_l):
        (m_first, m_second), first = _pair_masks()
        p = pl.program_id(1)
        qf[...] = q_ref[...].astype(F32)
        kf[...] = k_ref[...].astype(F32)
        vf[...] = v_ref[...].astype(F32)
        kd[0] = jnp.zeros((BLOCK, 128), BF16)
        vd[0] = jnp.zeros((BLOCK, 128), BF16)
        blk = lax.broadcasted_iota(jnp.int32, (NB, 1, 1), 0)

        for idx, (_, dilation) in enumerate(DIL_CONFIGS):
            nb = NB // dilation
            _to_residue_major(qd, qf, dilation, nb)
            _to_residue_major(kd, kf, dilation, nb, lead=1)
            _to_residue_major(vd, vf, dilation, nb, lead=1)
            q4, kc, vc = qd[...], kd[1:NB + 1], vd[1:NB + 1]
            outs, lses = [], []
            for e, hm in enumerate((m_first, m_second)):
                bias_cur, bias_prev = _band_bias(slope_ref[2 * p + e], dilation)
                qm = q4 * hm
                sc = _dot(qm, kc, BNT) * 0.125 + bias_cur
                m = jnp.max(sc, axis=2, keepdims=True)
                if nb > 1:
                    sp = _dot(qm, kd[0:NB], BNT) * 0.125 + jnp.where(blk % nb == 0, NEG, bias_prev)
                    m = jnp.maximum(m, jnp.max(sp, axis=2, keepdims=True))
                pc = jnp.exp(sc - m)
                l = jnp.sum(pc, axis=2, keepdims=True)
                o = _dot(pc.astype(BF16), vc, BNN)
                if nb > 1:
                    pp = jnp.exp(sp - m)
                    l = l + jnp.sum(pp, axis=2, keepdims=True)
                    o = o + _dot(pp.astype(BF16), vd[0:NB], BNN)
                outs.append(o * (1.0 / l))
                lses.append(m + jnp.log(l))
            od[...] = jnp.where(first, outs[0], outs[1])
            ld[...] = jnp.where(first, lses[0], lses[1])

            L = nb * BLOCK
            for r in range(dilation):
                rows = pl.ds(r, L, stride=dilation) if dilation > 1 else slice(None)
                o_new = od[r * nb:(r + 1) * nb].reshape(L, 128)
                l_new = ld[r * nb:(r + 1) * nb].reshape(L, 128)
                if idx == 0:
                    acc_o[rows, :] = o_new
                    acc_l[rows, :] = l_new
                else:
                    l_old = acc_l[rows, :]
                    m2 = jnp.maximum(l_old, l_new)
                    w_old, w_new = jnp.exp(l_old - m2), jnp.exp(l_new - m2)
                    tot = w_old + w_new
                    acc_o[rows, :] = (w_old * acc_o[rows, :] + w_new * o_new) * (1.0 / tot)
                    acc_l[rows, :] = m2 + jnp.log(tot)

        y_ref[...] = acc_o[...].astype(BF16)
        lse_ref[...] = acc_l[...]

    spec = lambda off: pl.BlockSpec((S, 128), lambda b, p: (b, 4 * off + p))
    ospec = pl.BlockSpec((S, 128), lambda b, p: (b, p))
    blocks = lambda n, dt: pltpu.VMEM((n, BLOCK, 128), dt)
    return pl.pallas_call(
        body, grid=(B, 4), in_specs=[_SMEM_SPEC, spec(0), spec(1), spec(2)], out_specs=[ospec, ospec],
        out_shape=[SDS((B * S, WIDTH), BF16), SDS((B * S, WIDTH), F32)],
        scratch_shapes=[pltpu.VMEM((S, 128), F32)] * 3 + [blocks(NB, BF16), blocks(NB + 1, BF16), blocks(NB + 1, BF16),
                                                         blocks(NB, F32), blocks(NB, F32)] + [pltpu.VMEM((S, 128), F32)] * 2,
        compiler_params=_cp(), name=name,
    )(_alibi_slopes(), z, z, z)


def _dil_attn_bwd(z, dy, ya, lse, *, B, S, name):
    NB = S // BLOCK

    def body(slope_ref, q_ref, k_ref, v_ref, do_ref, o_ref, lse_ref, dq_ref, dk_ref, dv_ref,
             qf, kf, vf, dof, ef, qd, dod, kd, vd, lsd, ed, dkd, dvd, dqa, dka, dva):
        (m_first, m_second), first = _pair_masks()
        p = pl.program_id(1)
        qf[...] = q_ref[...].astype(F32)
        kf[...] = k_ref[...].astype(F32)
        vf[...] = v_ref[...].astype(F32)
        dov = do_ref[...].astype(F32)
        dof[...] = dov
        prod = dov * o_ref[...].astype(F32)
        ef[...] = jnp.where(first, jnp.sum(jnp.where(first, prod, 0.0), axis=1, keepdims=True),
                            jnp.sum(jnp.where(first, 0.0, prod), axis=1, keepdims=True))
        kd[0] = jnp.zeros((BLOCK, 128), BF16)
        vd[0] = jnp.zeros((BLOCK, 128), BF16)
        blk = lax.broadcasted_iota(jnp.int32, (NB, 1, 1), 0)

        for idx, (_, dilation) in enumerate(DIL_CONFIGS):
            nb = NB // dilation
            _to_residue_major(qd, qf, dilation, nb)
            _to_residue_major(dod, dof, dilation, nb)
            _to_residue_major(kd, kf, dilation, nb, lead=1)
            _to_residue_major(vd, vf, dilation, nb, lead=1)
            _to_residue_major(lsd, lse_ref, dilation, nb)
            _to_residue_major(ed, ef, dilation, nb)
            q4, do4, kc, vc = qd[...], dod[...], kd[1:NB + 1], vd[1:NB + 1]
            dq4 = None
            dkc = dvc = dkp = dvp = None
            for e, hm in enumerate((m_first, m_second)):
                lane0 = slice(HEAD_DIM * e, HEAD_DIM * e + 1)
                bias_cur, bias_prev = _band_bias(slope_ref[2 * p + e], dilation)
                qm, dom = q4 * hm, do4 * hm
                lse_e, e_e = lsd[...][:, :, lane0], ed[...][:, :, lane0]
                pc = jnp.exp(_dot(qm, kc, BNT) * 0.125 + bias_cur - lse_e)
                dsc = (pc * (_dot(dom, vc, BNT) - e_e)).astype(BF16)
                pcb = pc.astype(BF16)
                dqe = _dot(dsc, kc, BNN)
                dkc = _dot(dsc, qm, BTN) if e == 0 else dkc + _dot(dsc, qm, BTN)
                dvc = _dot(pcb, dom, BTN) if e == 0 else dvc + _dot(pcb, dom, BTN)
                if nb > 1:
                    kp, vp = kd[0:NB], vd[0:NB]
                    pp = jnp.exp(_dot(qm, kp, BNT) * 0.125 + jnp.where(blk % nb == 0, NEG, bias_prev) - lse_e)
                    dsp = (pp * (_dot(dom, vp, BNT) - e_e)).astype(BF16)
                    ppb = pp.astype(BF16)
                    dqe = dqe + _dot(dsp, kp, BNN)
                    dkp = _dot(dsp, qm, BTN) if e == 0 else dkp + _dot(dsp, qm, BTN)
                    dvp = _dot(ppb, dom, BTN) if e == 0 else dvp + _dot(ppb, dom, BTN)
                dq4 = dqe if e == 0 else jnp.where(first, dq4, dqe)

            dkd[1:NB + 1] = dkc
            dvd[1:NB + 1] = dvc
            if nb > 1:
                dkd[1:NB] += dkp[1:NB]
                dvd[1:NB] += dvp[1:NB]
            L = nb * BLOCK
            for r in range(dilation):
                rows = pl.ds(r, L, stride=dilation) if dilation > 1 else slice(None)
                dq_r = dq4[r * nb:(r + 1) * nb].reshape(L, 128) * 0.125
                dk_r = dkd[1 + r * nb:1 + (r + 1) * nb].reshape(L, 128) * 0.125
                dv_r = dvd[1 + r * nb:1 + (r + 1) * nb].reshape(L, 128)
                if idx == 0:
                    dqa[rows, :], dka[rows, :], dva[rows, :] = dq_r, dk_r, dv_r
                else:
                    dqa[rows, :] += dq_r
                    dka[rows, :] += dk_r
                    dva[rows, :] += dv_r

        dq_ref[...] = dqa[...].astype(BF16)
        dk_ref[...] = dka[...].astype(BF16)
        dv_ref[...] = dva[...].astype(BF16)

    spec = lambda off: pl.BlockSpec((S, 128), lambda b, p: (b, 4 * off + p))
    ospec = pl.BlockSpec((S, 128), lambda b, p: (b, p))
    blocks = lambda n, dt: pltpu.VMEM((n, BLOCK, 128), dt)
    return pl.pallas_call(
        body, grid=(B, 4), in_specs=[_SMEM_SPEC, spec(0), spec(1), spec(2), ospec, ospec, ospec],
        out_specs=[ospec] * 3, out_shape=[SDS((B * S, WIDTH), BF16)] * 3,
        scratch_shapes=[pltpu.VMEM((S, 128), F32)] * 5
        + [blocks(NB, BF16), blocks(NB, BF16), blocks(NB + 1, BF16), blocks(NB + 1, BF16), blocks(NB, F32), blocks(NB, F32),
           blocks(NB + 1, F32), blocks(NB + 1, F32)] + [pltpu.VMEM((S, 128), F32)] * 3,
        compiler_params=_cp(), name=name,
    )(_alibi_slopes(), z, z, z, dy, ya, lse)


FOX_TQ = 256


def _fox_fwd(z, cc, cr, *, B, S, name):
    def body(q_ref, k_ref, v_ref, cc_ref, cr_ref, o_ref):
        (m_first, m_second), first = _pair_masks()
        for qi in range(S // FOX_TQ):
            r0, kend = qi * FOX_TQ, (qi + 1) * FOX_TQ
            q2 = q_ref[r0:kend, :]
            kk, vv = k_ref[0:kend, :], v_ref[0:kend, :]
            row = lax.broadcasted_iota(jnp.int32, (FOX_TQ, kend), 0) + r0
            col = lax.broadcasted_iota(jnp.int32, (FOX_TQ, kend), 1)
            causal = col <= row
            outs = []
            for e, hm in enumerate((m_first, m_second)):
                s = _dot(q2 * hm, kk, NT) * 0.125
                s = s + (cc_ref[r0:kend, :][:, HEAD_DIM * e:HEAD_DIM * e + 1] - cr_ref[e:e + 1, 0:kend])
                s = jnp.where(causal, s, NEG)
                m = jnp.max(s, axis=1, keepdims=True)
                pe = jnp.exp(s - m)
                l = jnp.sum(pe, axis=1, keepdims=True)
                outs.append(_dot(pe.astype(BF16), vv) * (1.0 / l))
            o_ref[r0:kend, :] = jnp.where(first, outs[0], outs[1]).astype(BF16)

    spec = lambda off: pl.BlockSpec((S, 128), lambda b, p: (b, 4 * off + p))
    return pl.pallas_call(
        body, grid=(B, 4),
        in_specs=[spec(3), spec(4), spec(5), pl.BlockSpec((S, 128), lambda b, p: (b, p)),
                  pl.BlockSpec((None, 8, S), lambda b, p: (4 * b + p, 0, 0))],
        out_specs=pl.BlockSpec((S, 128), lambda b, p: (b, p)), out_shape=SDS((B * S, WIDTH), BF16),
        compiler_params=_cp(), name=name,
    )(z, z, z, cc, cr)


def _fox_bwd(z, dy, cc, cr, *, B, S, name):
    def body(q_ref, k_ref, v_ref, do_ref, cc_ref, cr_ref, dq_ref, dk_ref, dv_ref, dc_ref, dk_s, dv_s, dc_s):
        (m_first, m_second), first = _pair_masks()
        dk_s[...] = jnp.zeros_like(dk_s)
        dv_s[...] = jnp.zeros_like(dv_s)
        dc_s[...] = jnp.zeros_like(dc_s)
        for qi in range(S // FOX_TQ):
            r0, kend = qi * FOX_TQ, (qi + 1) * FOX_TQ
            q2, do2 = q_ref[r0:kend, :], do_ref[r0:kend, :]
            kk, vv = k_ref[0:kend, :], v_ref[0:kend, :]
            krow = lax.broadcasted_iota(jnp.int32, (kend, FOX_TQ), 0)
            qcol = lax.broadcasted_iota(jnp.int32, (kend, FOX_TQ), 1) + r0
            causal = krow <= qcol
            dq_t = jnp.zeros((FOX_TQ, 128), F32)
            for e, hm in enumerate((m_first, m_second)):
                sel = first if e == 0 else ~first
                km = kk * hm
                st = _dot(km, q2, NT) * 0.125
                st = st + (cr_ref[e:e + 1, r0:kend] - cc_ref[0:kend, :][:, HEAD_DIM * e:HEAD_DIM * e + 1])
                st = jnp.where(causal, st, NEG)
                pt = jnp.exp(st - jnp.max(st, axis=0, keepdims=True))
                pt = pt * (1.0 / jnp.sum(pt, axis=0, keepdims=True))
                dpt = _dot(vv * hm, do2, NT)
                dst = pt * (dpt - jnp.sum(pt * dpt, axis=0, keepdims=True))
                dsb = dst.astype(BF16)
                dv_s[0:kend, :] += _dot(pt.astype(BF16), do2 * hm)
                dk_s[0:kend, :] += _dot(dsb, q2 * hm) * 0.125
                dq_t = dq_t + _dot(dsb, km, TN)
                dc_s[0:kend, :] += jnp.where(sel, -jnp.sum(dst, axis=1, keepdims=True), 0.0)
            dq_ref[r0:kend, :] = (dq_t * 0.125).astype(BF16)
        dk_ref[...] = dk_s[...].astype(BF16)
        dv_ref[...] = dv_s[...].astype(BF16)
        dc_ref[...] = dc_s[...]

    spec = lambda off: pl.BlockSpec((S, 128), lambda b, p: (b, 4 * off + p))
    pspec = pl.BlockSpec((S, 128), lambda b, p: (b, p))
    return pl.pallas_call(
        body, grid=(B, 4),
        in_specs=[spec(3), spec(4), spec(5), pl.BlockSpec((S, 128), lambda b, p: (b, 4 + p)), pspec,
                  pl.BlockSpec((None, 8, S), lambda b, p: (4 * b + p, 0, 0))],
        out_specs=[pspec] * 4,
        out_shape=[SDS((B * S, WIDTH), BF16)] * 3 + [SDS((B * S, WIDTH), F32)],
        scratch_shapes=[pltpu.VMEM((S, 128), F32)] * 3, compiler_params=_cp(), name=name,
    )(z, z, z, dy, cc, cr)


def _xattn_fwd(q, kv, *, B, S, M, tq, name):
    D = D_MODEL

    def body(q_ref, kv_ref, o_ref):
        for h in range(N_XH):
            cs = slice(XHD * h, XHD * (h + 1))
            s = _dot(q_ref[:, cs], kv_ref[:, cs], NT) * (1.0 / 16.0)
            pe = jnp.exp(s - jnp.max(s, axis=1, keepdims=True))
            l = jnp.sum(pe, axis=1, keepdims=True)
            o_ref[:, cs] = (_dot(pe.astype(BF16), kv_ref[:, D + XHD * h:D + XHD * (h + 1)]) * (1.0 / l)).astype(BF16)

    nq = S // tq
    return pl.pallas_call(
        body, grid=(B, nq),
        in_specs=[pl.BlockSpec((tq, D), lambda b, t: (b * nq + t, 0)), pl.BlockSpec((M, 2 * D), lambda b, t: (b, 0))],
        out_specs=pl.BlockSpec((tq, D), lambda b, t: (b * nq + t, 0)), out_shape=SDS((B * S, D), BF16),
        compiler_params=_cp(), name=name,
    )(q, kv)


def _xattn_bwd(q, kv, do, *, B, S, M, tq, name):
    D = D_MODEL

    def body(q_ref, kv_ref, do_ref, dq_ref, dkv_ref):
        t = pl.program_id(1)

        @pl.when(t == 0)
        def _():
            dkv_ref[...] = jnp.zeros_like(dkv_ref)

        for h in range(N_XH):
            cs = slice(XHD * h, XHD * (h + 1))
            vs = slice(D + XHD * h, D + XHD * (h + 1))
            qh, kh, vh, doh = q_ref[:, cs], kv_ref[:, cs], kv_ref[:, vs], do_ref[:, cs]
            s = _dot(qh, kh, NT) * (1.0 / 16.0)
            pe = jnp.exp(s - jnp.max(s, axis=1, keepdims=True))
            pe = pe * (1.0 / jnp.sum(pe, axis=1, keepdims=True))
            dp = _dot(doh, vh, NT)
            ds = (pe * (dp - jnp.sum(pe * dp, axis=1, keepdims=True))).astype(BF16)
            dq_ref[:, cs] = (_dot(ds, kh) * (1.0 / 16.0)).astype(BF16)
            dkv_ref[:, cs] += _dot(ds, qh, TN) * (1.0 / 16.0)
            dkv_ref[:, vs] += _dot(pe.astype(BF16), doh, TN)

    nq = S // tq
    qspec = pl.BlockSpec((tq, D), lambda b, t: (b * nq + t, 0))
    kvspec = pl.BlockSpec((M, 2 * D), lambda b, t: (b, 0))
    return pl.pallas_call(
        body, grid=(B, nq), in_specs=[qspec, kvspec, qspec], out_specs=[qspec, kvspec],
        out_shape=[SDS((B * S, D), BF16), SDS((B * M, 2 * D), F32)], compiler_params=_cp(), name=name,
    )(q, kv, do)


def _assemble_dz(parts, dgate, *, tm, name):
    T = dgate.shape[0]

    def body(*refs):
        o_ref = refs[-1]
        for j in range(6):
            o_ref[:, WIDTH * j:WIDTH * (j + 1)] = refs[j][...]
        o_ref[:, QKV_W:IN_PAD] = refs[6][...]

    wspec = pl.BlockSpec((tm, WIDTH), lambda i: (i, 0))
    return pl.pallas_call(
        body, grid=(T // tm,), in_specs=[wspec] * 6 + [pl.BlockSpec((tm, 128), lambda i: (i, 0))],
        out_specs=pl.BlockSpec((tm, IN_PAD), lambda i: (i, 0)), out_shape=SDS((T, IN_PAD), BF16),
        compiler_params=_cp(), name=name,
    )(*parts, dgate)


def _adamw(parts, w, m, v, *, tr, name):
    R, C = w.shape

    def body(p_ref, w_ref, m_ref, v_ref, g_ref, d_ref, nm_ref, nv_ref):
        g = p_ref[0].astype(F32)
        for d in range(1, N_DEV):
            g = g + p_ref[d].astype(F32)
        m2 = ADAM_B1 * m_ref[...] + (1.0 - ADAM_B1) * g
        v2 = ADAM_B2 * v_ref[...] + (1.0 - ADAM_B2) * (g * g)
        m_hat = m2 / (1.0 - ADAM_B1 ** ADAM_STEP)
        v_hat = v2 / (1.0 - ADAM_B2 ** ADAM_STEP)
        g_ref[...] = g
        d_ref[...] = -ADAM_LR * (m_hat / (jnp.sqrt(v_hat) + ADAM_EPS) + ADAM_WD * w_ref[...])
        nm_ref[...] = m2
        nv_ref[...] = v2

    spec = pl.BlockSpec((tr, C), lambda i: (i, 0))
    return pl.pallas_call(
        body, grid=(R // tr,), in_specs=[pl.BlockSpec((N_DEV, tr, C), lambda i: (0, i, 0)), spec, spec, spec],
        out_specs=[spec] * 4, out_shape=[SDS((R, C), F32)] * 4, compiler_params=_cp(), name=name,
    )(parts, w, m, v)


def _exchange(arrays, modes, *, name):
    n = len(arrays)
    out_shape = [SDS((N_DEV,) + a.shape if md == "gather" else a.shape, a.dtype) for a, md in zip(arrays, modes)]

    def body(*refs):
        ins, outs = refs[:n], refs[n:2 * n]
        send_sems, recv_sems, local_sems = refs[2 * n:]
        x, y, c = (lax.axis_index(a) for a in AXES)
        me = 4 * x + 2 * y + c
        copies = []
        for i, md in enumerate(modes):
            src = ins[i] if md == "gather" else ins[i].at[me]
            cp = pltpu.make_async_copy(src, outs[i].at[me], local_sems.at[i])
            cp.start()
            copies.append(cp)
            for k in range(1, N_DEV):
                px = 1 - x if k & 4 else x
                py = 1 - y if k & 2 else y
                pc = 1 - c if k & 1 else c
                src = ins[i] if md == "gather" else ins[i].at[4 * px + 2 * py + pc]
                cp = pltpu.make_async_remote_copy(
                    src_ref=src, dst_ref=outs[i].at[me], send_sem=send_sems.at[i, k - 1], recv_sem=recv_sems.at[i, k - 1],
                    device_id=(px, py, pc), device_id_type=pl.DeviceIdType.MESH)
                cp.start()
                copies.append(cp)
        for cp in copies:
            cp.wait()

    anyspec = pl.BlockSpec(memory_space=pl.ANY)
    return pl.pallas_call(
        body, in_specs=[anyspec] * n, out_specs=[anyspec] * n, out_shape=out_shape,
        scratch_shapes=[pltpu.SemaphoreType.DMA((n, N_DEV - 1)), pltpu.SemaphoreType.DMA((n, N_DEV - 1)),
                        pltpu.SemaphoreType.DMA((n,))],
        name=name,
    )(*arrays)


def _local_step(x, mem, g_mix, b_forget, g_xattn, g_mem, g_mlp, g_final, target,
                w_in_pad, w_out, w_xq, w_kv, w_xo, w_up, w_down):
    B, S, D = x.shape
    M = mem.shape[1]
    T = B * S
    x0 = x.reshape(T, D)
    mem2 = mem.reshape(B * M, D)
    tgt = target.reshape(T, D)
    b_pad = jnp.pad(b_forget, (0, 120)).reshape(1, 128)

    h1, z = _rms_matmul(x0, g_mix, w_in_pad[:, :QKV_W], tm=1024, tn=768, out_dtype=BF16, name="f_in")
    gate = _matmul_nn(h1, w_in_pad[:, QKV_W:], tm=1024, tn=128, tk=D, out_dtype=F32, name="f_gate")
    c, cc = _gate_fwd(gate, b_pad, B=B, S=S, name="f_gatecum")
    cr = jnp.pad(c[:, :8].reshape(B, S, 4, 2).transpose(0, 2, 3, 1), ((0, 0), (0, 0), (0, 6), (0, 0))).reshape(B * 4, 8, S)
    ya, lse = _dil_attn_fwd(z, B=B, S=S, name="f_dil")
    yf = _fox_fwd(z, cc, cr, B=B, S=S, name="f_fox")
    ymix = jnp.concatenate([ya, yf], axis=1)
    x1 = _matmul_nn(ymix, w_out, res=x0, tm=1024, tn=D, tk=D, out_dtype=F32, name="f_out")
    h2, q = _rms_matmul(x1, g_xattn, w_xq, tm=1024, tn=D, out_dtype=BF16, name="f_xq")
    mn, kv = _rms_matmul(mem2, g_mem, w_kv, tm=B * M, tn=D, out_dtype=BF16, name="f_xkv")
    xo = _xattn_fwd(q, kv, B=B, S=S, M=M, tq=512, name="f_xattn")
    x2 = _matmul_nn(xo, w_xo, res=x1, tm=1024, tn=D, tk=D, out_dtype=F32, name="f_xo")
    h3, act = _rms_matmul(x2, g_mlp, w_up, tm=1024, tn=1024, out_dtype=BF16, relu=True, name="f_up")
    x3 = _matmul_nn(act, w_down, res=x2, square=True, tm=1024, tn=D, tk=1024, out_dtype=F32, name="f_down")
    dx3, dg_final, loss = _loss_head(x3, g_final, tgt, tm=512, name="f_loss")

    du = _matmul_nt(dx3, w_down, mul2a=act, tm=1024, tn=1024, name="b_dact")
    dw_down = _matmul_tn(act, dx3, square=True, bk=1024, bn=D, tt=512, out_dtype=BF16, name="b_wdown")
    dw_up = _matmul_tn(h3, du, bk=D, bn=1024, tt=512, out_dtype=BF16, name="b_wup")
    dx2, dg_mlp = _matmul_nt_rms(du, w_up, x2, g_mlp, dx3, tm=512, tk=1024, name="b_dh3")
    dxo = _matmul_nt(dx2, w_xo, tm=1024, tn=D, name="b_dxo")
    dw_xo = _matmul_tn(xo, dx2, bk=D, bn=D, tt=512, out_dtype=BF16, name="b_wxo")
    dq, dkv = _xattn_bwd(q, kv, dxo, B=B, S=S, M=M, tq=512, name="b_xattn")
    dw_xq = _matmul_tn(h2, dq, bk=D, bn=D, tt=512, out_dtype=BF16, name="b_wxq")
    dx1, dg_xattn = _matmul_nt_rms(dq, w_xq, x1, g_xattn, dx2, tm=512, tk=D, name="b_dh2")
    dw_kv = _matmul_tn(mn, dkv, bk=D, bn=D, tt=min(512, B * M), out_dtype=BF16, name="b_wkv")
    _, dg_mem = _matmul_nt_rms(dkv, w_kv, mem2, g_mem, None, tm=min(512, B * M), tk=D, name="b_dmem")
    dy = _matmul_nt(dx1, w_out, tm=1024, tn=D, name="b_dy")
    dw_out = _matmul_tn(ymix, dx1, bk=D, bn=D, tt=512, out_dtype=BF16, name="b_wout")
    dqf, dkf, dvf, dcc = _fox_bwd(z, dy, cc, cr, B=B, S=S, name="b_fox")
    dgate, db = _gate_bwd(dcc, gate, b_pad, B=B, S=S, name="b_gate")
    dqa, dka, dva = _dil_attn_bwd(z, dy, ya, lse, B=B, S=S, name="b_dil")
    dz = _assemble_dz([dqa, dka, dva, dqf, dkf, dvf], dgate, tm=512, name="b_dz")
    dw_in = _matmul_tn(h1, dz, bk=D, bn=640, tt=512, out_dtype=BF16, name="b_win")
    gx, dg_mix = _matmul_nt_rms(dz, w_in_pad, x0, g_mix, dx1, tm=512, tk=640, name="b_dh1")

    small = dict(g_mix=dg_mix, b_forget=db, g_xattn=dg_xattn, g_mem=dg_mem, g_mlp=dg_mlp, g_final=dg_final)
    big = dict(w_in=dw_in, w_out=dw_out, w_xq=dw_xq, w_kv=dw_kv, w_xo=dw_xo, w_up=dw_up, w_down=dw_down)
    return gx.reshape(B, S, D), big, small, loss


SMALL_ROWS = ("g_mix", "b_forget", "g_xattn", "g_mem", "g_mlp", "g_final")


def _pack_rows(rows):
    D = D_MODEL
    rows = [jnp.pad(r.reshape(-1), (0, D - r.size)) for r in rows]
    rows += [jnp.zeros((D,), F32)] * (8 - len(rows))
    return jnp.stack(rows)


def kernel(x, mem, g_mix, w_in, b_forget, w_out, g_xattn, g_mem, w_xq, w_xk, w_xv, w_xo, g_mlp, w_up, w_down, g_final, loss_target, m_g_mix, m_w_in, m_b_forget, m_w_out, m_g_xattn, m_g_mem, m_w_xq, m_w_xk, m_w_xv, m_w_xo, m_g_mlp, m_w_up, m_w_down, m_g_final, v_g_mix, v_w_in, v_b_forget, v_w_out, v_g_xattn, v_g_mem, v_w_xq, v_w_xk, v_w_xv, v_w_xo, v_g_mlp, v_w_up, v_w_down, v_g_final):
    D = D_MODEL
    W = dict(w_in=w_in, w_out=w_out, w_xq=w_xq, w_xk=w_xk, w_xv=w_xv, w_xo=w_xo, w_up=w_up, w_down=w_down)
    Mo = dict(w_in=m_w_in, w_out=m_w_out, w_xq=m_w_xq, w_xk=m_w_xk, w_xv=m_w_xv, w_xo=m_w_xo, w_up=m_w_up, w_down=m_w_down)
    Vo = dict(w_in=v_w_in, w_out=v_w_out, w_xq=v_w_xq, w_xk=v_w_xk, w_xv=v_w_xv, w_xo=v_w_xo, w_up=v_w_up, w_down=v_w_down)
    names = list(W)

    gathered = dict(zip(names, _exchange([W[n].astype(BF16) for n in names], ["gather"] * len(names), name="gather_weights")))
    cols = lambda g: g.transpose(1, 0, 2).reshape(g.shape[1], -1)
    rows = lambda g: g.reshape(-1, g.shape[2])
    w_in_pad = jnp.pad(cols(gathered["w_in"]), ((0, 0), (0, IN_PAD - IN_W)))
    w_kv = jnp.concatenate([rows(gathered["w_xk"]), rows(gathered["w_xv"])], axis=1)

    gx, big, small, loss = _local_step(
        x, mem, g_mix, b_forget, g_xattn, g_mem, g_mlp, g_final, loss_target,
        w_in_pad, rows(gathered["w_out"]), rows(gathered["w_xq"]), w_kv, rows(gathered["w_xo"]),
        cols(gathered["w_up"]), rows(gathered["w_down"]))

    by_cols = lambda g, n: g[:, :n * N_DEV].reshape(g.shape[0], N_DEV, n).transpose(1, 0, 2)
    by_rows = lambda g: g.reshape(N_DEV, g.shape[0] // N_DEV, g.shape[1])
    parts = dict(w_in=by_cols(big["w_in"], IN_W // N_DEV), w_out=by_rows(big["w_out"]), w_xq=by_rows(big["w_xq"]),
                 w_xk=by_rows(big["w_kv"][:, :D]), w_xv=by_rows(big["w_kv"][:, D:]), w_xo=by_rows(big["w_xo"]),
                 w_up=by_cols(big["w_up"], D_FF // N_DEV), w_down=by_rows(big["w_down"]))
    packed = _pack_rows([small[n] for n in SMALL_ROWS] + [loss[0, :1]])
    received = _exchange([parts[n] for n in names] + [packed], ["scatter"] * len(names) + ["gather"], name="exchange_grads")
    received, packed_all = dict(zip(names, received[:-1])), received[-1]

    res = {n: _adamw(received[n], W[n], Mo[n], Vo[n], tr=128, name=f"adamw_{n}") for n in names}
    small_w = dict(g_mix=g_mix, b_forget=b_forget, g_xattn=g_xattn, g_mem=g_mem, g_mlp=g_mlp, g_final=g_final)
    small_m = dict(g_mix=m_g_mix, b_forget=m_b_forget, g_xattn=m_g_xattn, g_mem=m_g_mem, g_mlp=m_g_mlp, g_final=m_g_final)
    small_v = dict(g_mix=v_g_mix, b_forget=v_b_forget, g_xattn=v_g_xattn, g_mem=v_g_mem, g_mlp=v_g_mlp, g_final=v_g_final)
    sres = _adamw(packed_all, _pack_rows([small_w[n] for n in SMALL_ROWS]), _pack_rows([small_m[n] for n in SMALL_ROWS]),
                  _pack_rows([small_v[n] for n in SMALL_ROWS]), tr=8, name="adamw_small")
    for i, n in enumerate(SMALL_ROWS):
        res[n] = [r[i, :small_w[n].size] for r in sres]
    loss_total = sres[0][6, 0]

    order = ["g_mix", "w_in", "b_forget", "w_out", "g_xattn", "g_mem", "w_xq", "w_xk", "w_xv", "w_xo", "g_mlp", "w_up", "w_down", "g_final"]
    return (loss_total, gx, *[res[n][0] for n in order], *[res[n][1] for n in order],
            *[res[n][2] for n in order], *[res[n][3] for n in order])
```

```python
import jax
import jax.numpy as jnp
from jax import lax
from jax.experimental import pallas as pl
from jax.experimental.pallas import tpu as pltpu

F32, BF16 = jnp.float32, jnp.bfloat16
SDS = jax.ShapeDtypeStruct

D_MODEL = 1024
HEAD_DIM = 64
WIDTH = 512
QKV_W = 6 * WIDTH
IN_W = QKV_W + 8
IN_PAD = QKV_W + 128
BLOCK = 128
DIL_CONFIGS = ((128, 1), (512, 4), (2048, 16))
N_XH, XHD = 4, 256
D_FF = 4096
EPS = 1e-6
NEG = -1e30
N_DEV = 8
AXES = ("x", "y", "c")

ADAM_LR, ADAM_B1, ADAM_B2, ADAM_EPS, ADAM_WD, ADAM_STEP = 0.001, 0.9, 0.999, 1e-08, 0.01, 10

VMEM_CAP_V7X = 64 * 1024 * 1024
VMEM_LIMIT = VMEM_CAP_V7X * 7 // 8

NT = (((1,), (1,)), ((), ()))
TN = (((0,), (0,)), ((), ()))


def _cp(**kw):
    return pltpu.CompilerParams(vmem_limit_bytes=VMEM_LIMIT, **kw)


def _dot(a, b, dims=None):
    if dims is None:
        return jnp.dot(a, b, preferred_element_type=F32)
    return lax.dot_general(a, b, dims, preferred_element_type=F32)


def _rstd(xv):
    return lax.rsqrt(jnp.mean(xv * xv, axis=-1, keepdims=True) + EPS)


def _rms_bwd(dh, xv, g):
    r = _rstd(xv)
    xhat = xv * r
    dxhat = dh * g
    dx = r * (dxhat - xhat * jnp.mean(dxhat * xhat, axis=-1, keepdims=True))
    return dx, jnp.sum(dh * xhat, axis=0, keepdims=True)


def _rms_matmul(x, g, w, *, tm, tn, out_dtype, relu=False, name):
    T, D = x.shape
    N = w.shape[1]

    def body(x_ref, g_ref, w_ref, h_ref, o_ref, h_s):
        @pl.when(pl.program_id(1) == 0)
        def _():
            xv = x_ref[...]
            h = (xv * _rstd(xv) * g_ref[...]).astype(BF16)
            h_s[...] = h
            h_ref[...] = h

        acc = _dot(h_s[...], w_ref[...])
        if relu:
            acc = jnp.maximum(acc, 0.0)
        o_ref[...] = acc.astype(out_dtype)

    return pl.pallas_call(
        body, grid=(T // tm, N // tn),
        in_specs=[pl.BlockSpec((tm, D), lambda i, j: (i, 0)), pl.BlockSpec((1, D), lambda i, j: (0, 0)),
                  pl.BlockSpec((D, tn), lambda i, j: (0, j))],
        out_specs=[pl.BlockSpec((tm, D), lambda i, j: (i, 0)), pl.BlockSpec((tm, tn), lambda i, j: (i, j))],
        out_shape=[SDS((T, D), BF16), SDS((T, N), out_dtype)],
        scratch_shapes=[pltpu.VMEM((tm, D), BF16)], compiler_params=_cp(), name=name,
    )(x, g.reshape(1, D), w)


def _matmul_nn(a, w, *, res=None, square=False, tm, tn, tk, out_dtype, name):
    T, K = a.shape
    N = w.shape[1]
    nk = K // tk

    def body(*refs):
        a_ref, w_ref = refs[0], refs[1]
        res_ref = refs[2] if res is not None else None
        o_ref, acc = refs[-2], refs[-1]
        k = pl.program_id(2)

        @pl.when(k == 0)
        def _():
            acc[...] = jnp.zeros_like(acc)

        av = a_ref[...]
        if square:
            af = av.astype(F32)
            av = (af * af).astype(BF16)
        acc[...] += _dot(av, w_ref[...])

        @pl.when(k == nk - 1)
        def _():
            r = acc[...]
            if res_ref is not None:
                r = res_ref[...] + r
            o_ref[...] = r.astype(out_dtype)

    in_specs = [pl.BlockSpec((tm, tk), lambda i, j, k: (i, k)), pl.BlockSpec((tk, tn), lambda i, j, k: (k, j))]
    args = [a, w]
    if res is not None:
        in_specs.append(pl.BlockSpec((tm, tn), lambda i, j, k: (i, j)))
        args.append(res)
    return pl.pallas_call(
        body, grid=(T // tm, N // tn, nk), in_specs=in_specs,
        out_specs=pl.BlockSpec((tm, tn), lambda i, j, k: (i, j)), out_shape=SDS((T, N), out_dtype),
        scratch_shapes=[pltpu.VMEM((tm, tn), F32)], compiler_params=_cp(), name=name,
    )(*args)


def _matmul_nt(g, w, *, mul2a=None, tm, tn, name):
    T, K = g.shape
    N = w.shape[0]

    def body(*refs):
        g_ref, w_ref = refs[0], refs[1]
        o_ref = refs[-1]
        acc = _dot(g_ref[...].astype(BF16), w_ref[...], NT)
        if mul2a is not None:
            acc = acc * (2.0 * refs[2][...].astype(F32))
        o_ref[...] = acc.astype(BF16)

    in_specs = [pl.BlockSpec((tm, K), lambda i, j: (i, 0)), pl.BlockSpec((tn, K), lambda i, j: (j, 0))]
    args = [g, w]
    if mul2a is not None:
        in_specs.append(pl.BlockSpec((tm, tn), lambda i, j: (i, j)))
        args.append(mul2a)
    return pl.pallas_call(
        body, grid=(T // tm, N // tn), in_specs=in_specs,
        out_specs=pl.BlockSpec((tm, tn), lambda i, j: (i, j)), out_shape=SDS((T, N), BF16),
        compiler_params=_cp(), name=name,
    )(*args)


def _matmul_nt_rms(g, w, x, gain, dres, *, tm, tk, name):
    T, K = g.shape
    D = w.shape[0]
    nk = K // tk
    nt = T // tm

    def body(*refs):
        g_ref, w_ref, x_ref, gain_ref = refs[:4]
        dres_ref = refs[4] if dres is not None else None
        dx_ref, dg_ref, acc = refs[-3], refs[-2], refs[-1]
        i, k = pl.program_id(0), pl.program_id(1)

        @pl.when(k == 0)
        def _():
            acc[...] = jnp.zeros_like(acc)

        acc[...] += _dot(g_ref[...].astype(BF16), w_ref[...], NT)

        @pl.when(k == nk - 1)
        def _():
            dx, dg = _rms_bwd(acc[...], x_ref[...], gain_ref[...])
            if dres_ref is not None:
                dx = dres_ref[...] + dx
            dx_ref[...] = dx

            @pl.when(i == 0)
            def _():
                dg_ref[...] = dg

            @pl.when(i > 0)
            def _():
                dg_ref[...] += dg

    in_specs = [pl.BlockSpec((tm, tk), lambda i, k: (i, k)), pl.BlockSpec((D, tk), lambda i, k: (0, k)),
                pl.BlockSpec((tm, D), lambda i, k: (i, 0)), pl.BlockSpec((1, D), lambda i, k: (0, 0))]
    args = [g, w, x, gain.reshape(1, D)]
    if dres is not None:
        in_specs.append(pl.BlockSpec((tm, D), lambda i, k: (i, 0)))
        args.append(dres)
    return pl.pallas_call(
        body, grid=(nt, nk), in_specs=in_specs,
        out_specs=[pl.BlockSpec((tm, D), lambda i, k: (i, 0)), pl.BlockSpec((1, D), lambda i, k: (0, 0))],
        out_shape=[SDS((T, D), F32), SDS((1, D), F32)],
        scratch_shapes=[pltpu.VMEM((tm, D), F32)], compiler_params=_cp(), name=name,
    )(*args)


def _matmul_tn(a, g, *, square=False, bk, bn, tt, out_dtype, name):
    T, K = a.shape
    N = g.shape[1]
    nt = T // tt

    def body(a_ref, g_ref, o_ref, acc):
        t = pl.program_id(2)

        @pl.when(t == 0)
        def _():
            acc[...] = jnp.zeros_like(acc)

        av = a_ref[...]
        if square:
            af = av.astype(F32)
            av = (af * af).astype(BF16)
        acc[...] += _dot(av, g_ref[...].astype(BF16), TN)

        @pl.when(t == nt - 1)
        def _():
            o_ref[...] = acc[...].astype(out_dtype)

    return pl.pallas_call(
        body, grid=(K // bk, N // bn, nt),
        in_specs=[pl.BlockSpec((tt, bk), lambda i, j, t: (t, i)), pl.BlockSpec((tt, bn), lambda i, j, t: (t, j))],
        out_specs=pl.BlockSpec((bk, bn), lambda i, j, t: (i, j)), out_shape=SDS((K, N), out_dtype),
        scratch_shapes=[pltpu.VMEM((bk, bn), F32)], compiler_params=_cp(), name=name,
    )(a, g)


def _loss_head(x3, g_final, target, *, tm, name):
    T, D = x3.shape

    def body(x_ref, g_ref, t_ref, dx_ref, dg_ref, loss_ref):
        i = pl.program_id(0)
        xv, g = x_ref[...], g_ref[...]
        r = _rstd(xv)
        xhat = xv * r
        diff = xhat * g - t_ref[...]
        part = 0.5 * jnp.sum(jnp.mean(diff * diff, axis=-1, keepdims=True), axis=0, keepdims=True)
        dy = diff * (1.0 / D)
        dxhat = dy * g
        dx_ref[...] = r * (dxhat - xhat * jnp.mean(dxhat * xhat, axis=-1, keepdims=True))
        dg = jnp.sum(dy * xhat, axis=0, keepdims=True)
        lp = jnp.broadcast_to(part, loss_ref.shape)

        @pl.when(i == 0)
        def _():
            dg_ref[...] = dg
            loss_ref[...] = lp

        @pl.when(i > 0)
        def _():
            dg_ref[...] += dg
            loss_ref[...] += lp

    return pl.pallas_call(
        body, grid=(T // tm,),
        in_specs=[pl.BlockSpec((tm, D), lambda i: (i, 0)), pl.BlockSpec((1, D), lambda i: (0, 0)),
                  pl.BlockSpec((tm, D), lambda i: (i, 0))],
        out_specs=[pl.BlockSpec((tm, D), lambda i: (i, 0)), pl.BlockSpec((1, D), lambda i: (0, 0)),
                   pl.BlockSpec((8, 128), lambda i: (0, 0))],
        out_shape=[SDS((T, D), F32), SDS((1, D), F32), SDS((8, 128), F32)],
        compiler_params=_cp(), name=name,
    )(x3, g_final.reshape(1, D), target)


def _head_lanes(shape, width):
    return lax.broadcasted_iota(jnp.int32, shape, len(shape) - 1) // width


def _gate_fwd(gate, b_pad, *, B, S, name):
    def body(g_ref, b_ref, c_ref, cc_ref):
        xv = g_ref[...] + b_ref[...]
        lf = jnp.minimum(xv, 0.0) - jnp.log(1.0 + jnp.exp(-jnp.abs(xv)))
        lane = lax.broadcasted_iota(jnp.int32, lf.shape, 1)
        row = lax.broadcasted_iota(jnp.int32, lf.shape, 0)
        c = jnp.where(lane < 8, lf, 0.0)
        sh = 1
        while sh < S:
            c = c + jnp.where(row >= sh, pltpu.roll(c, sh, 0), 0.0)
            sh *= 2
        c_ref[...] = c
        grp = _head_lanes((S, WIDTH), HEAD_DIM)
        cc = jnp.zeros((S, WIDTH), F32)
        for h in range(8):
            cc = jnp.where(grp == h, c[:, h:h + 1], cc)
        cc_ref[...] = cc

    return pl.pallas_call(
        body, grid=(B,),
        in_specs=[pl.BlockSpec((S, 128), lambda b: (b, 0)), pl.BlockSpec((1, 128), lambda b: (0, 0))],
        out_specs=[pl.BlockSpec((S, 128), lambda b: (b, 0)), pl.BlockSpec((S, WIDTH), lambda b: (b, 0))],
        out_shape=[SDS((B * S, 128), F32), SDS((B * S, WIDTH), F32)],
        compiler_params=_cp(), name=name,
    )(gate, b_pad)


def _gate_bwd(dcc, gate, b_pad, *, B, S, name):
    def body(dcc_ref, g_ref, b_ref, dg_ref, db_ref):
        bi = pl.program_id(0)
        dccv = dcc_ref[...]
        lane = lax.broadcasted_iota(jnp.int32, (S, 128), 1)
        row = lax.broadcasted_iota(jnp.int32, (S, 128), 0)
        dc = jnp.zeros((S, 128), F32)
        for h in range(8):
            dc = jnp.where(lane == h, dccv[:, HEAD_DIM * h:HEAD_DIM * h + 1], dc)
        sh = 1
        while sh < S:
            dc = dc + jnp.where(row < S - sh, pltpu.roll(dc, S - sh, 0), 0.0)
            sh *= 2
        xv = g_ref[...] + b_ref[...]
        dgate = jnp.where(lane < 8, dc / (1.0 + jnp.exp(xv)), 0.0)
        dg_ref[...] = dgate.astype(BF16)
        db = jnp.sum(dgate, axis=0, keepdims=True)

        @pl.when(bi == 0)
        def _():
            db_ref[...] = db

        @pl.when(bi > 0)
        def _():
            db_ref[...] += db

    return pl.pallas_call(
        body, grid=(B,),
        in_specs=[pl.BlockSpec((S, WIDTH), lambda b: (b, 0)), pl.BlockSpec((S, 128), lambda b: (b, 0)),
                  pl.BlockSpec((1, 128), lambda b: (0, 0))],
        out_specs=[pl.BlockSpec((S, 128), lambda b: (b, 0)), pl.BlockSpec((1, 128), lambda b: (0, 0))],
        out_shape=[SDS((B * S, 128), BF16), SDS((1, 128), F32)],
        compiler_params=_cp(), name=name,
    )(dcc, gate, b_pad)


_SMEM_SPEC = pl.BlockSpec(memory_space=pltpu.SMEM)


def _alibi_slopes():
    return 2.0 ** (-(jnp.arange(1, 9, dtype=F32) * (8.0 / 8)))


def _pair_masks():
    lane = lax.broadcasted_iota(jnp.int32, (1, 128), 1)
    first = lane < HEAD_DIM
    return (first.astype(BF16), (~first).astype(BF16)), first


BNT =(((2,), (2,)), ((0,), (0,)))
BNN = (((2,), (1,)), ((0,), (0,)))
BTN = (((1,), (1,)), ((0,), (0,)))


def _band_bias(slope, dilation):
    qi = lax.broadcasted_iota(jnp.int32, (BLOCK, BLOCK), 0)
    kj = lax.broadcasted_iota(jnp.int32, (BLOCK, BLOCK), 1)
    cur = jnp.where(kj <= qi, (-slope * dilation) * (qi - kj).astype(F32), NEG)
    prev = jnp.where(kj >= qi, (-slope * dilation) * (qi + BLOCK - kj).astype(F32), NEG)
    return cur, prev


def _to_residue_major(dst, src_f32, dilation, nb, lead=0):
    L = nb * BLOCK
    for r in range(dilation):
        rows = src_f32[pl.ds(r, L, stride=dilation), :] if dilation > 1 else src_f32[...]
        dst[lead + r * nb:lead + (r + 1) * nb] = rows.reshape(nb, BLOCK, 128).astype(dst.dtype)


def _dil_attn_fwd(z, *, B, S, name):
    NB = S // BLOCK

    def body(slope_ref, q_ref, k_ref, v_ref, y_ref, lse_ref, qf, kf, vf, qd, kd, vd, od, ld, acc_o, acc_l):
        (m_first, m_second), first = _pair_masks()
        p = pl.program_id(1)
        qf[...] = q_ref[...].astype(F32)
        kf[...] = k_ref[...].astype(F32)
        vf[...] = v_ref[...].astype(F32)
        kd[0] = jnp.zeros((BLOCK, 128), BF16)
        vd[0] = jnp.zeros((BLOCK, 128), BF16)
        blk = lax.broadcasted_iota(jnp.int32, (NB, 1, 1), 0)

        for idx, (_, dilation) in enumerate(DIL_CONFIGS):
            nb = NB // dilation
            _to_residue_major(qd, qf, dilation, nb)
            _to_residue_major(kd, kf, dilation, nb, lead=1)
            _to_residue_major(vd, vf, dilation, nb, lead=1)
            q4, kc, vc = qd[...], kd[1:NB + 1], vd[1:NB + 1]
            outs, lses = [], []
            for e, hm in enumerate((m_first, m_second)):
                bias_cur, bias_prev = _band_bias(slope_ref[2 * p + e], dilation)
                qm = q4 * hm
                sc = _dot(qm, kc, BNT) * 0.125 + bias_cur
                m = jnp.max(sc, axis=2, keepdims=True)
                if nb > 1:
                    sp = _dot(qm, kd[0:NB], BNT) * 0.125 + jnp.where(blk % nb == 0, NEG, bias_prev)
                    m = jnp.maximum(m, jnp.max(sp, axis=2, keepdims=True))
                pc = jnp.exp(sc - m)
                l = jnp.sum(pc, axis=2, keepdims=True)
                o = _dot(pc.astype(BF16), vc, BNN)
                if nb > 1:
                    pp = jnp.exp(sp - m)
                    l = l + jnp.sum(pp, axis=2, keepdims=True)
                    o = o + _dot(pp.astype(BF16), vd[0:NB], BNN)
                outs.append(o * (1.0 / l))
                lses.append(m + jnp.log(l))
            od[...] = jnp.where(first, outs[0], outs[1])
            ld[...] = jnp.where(first, lses[0], lses[1])

            L = nb * BLOCK
            for r in range(dilation):
                rows = pl.ds(r, L, stride=dilation) if dilation > 1 else slice(None)
                o_new = od[r * nb:(r + 1) * nb].reshape(L, 128)
                l_new = ld[r * nb:(r + 1) * nb].reshape(L, 128)
                if idx == 0:
                    acc_o[rows, :] = o_new
                    acc_l[rows, :] = l_new
                else:
                    l_old = acc_l[rows, :]
                    m2 = jnp.maximum(l_old, l_new)
                    w_old, w_new = jnp.exp(l_old - m2), jnp.exp(l_new - m2)
                    tot = w_old + w_new
                    acc_o[rows, :] = (w_old * acc_o[rows, :] + w_new * o_new) * (1.0 / tot)
                    acc_l[rows, :] = m2 + jnp.log(tot)

        y_ref[...] = acc_o[...].astype(BF16)
        lse_ref[...] = acc_l[...]

    spec = lambda off: pl.BlockSpec((S, 128), lambda b, p: (b, 4 * off + p))
    ospec = pl.BlockSpec((S, 128), lambda b, p: (b, p))
    blocks = lambda n, dt: pltpu.VMEM((n, BLOCK, 128), dt)
    return pl.pallas_call(
        body, grid=(B, 4), in_specs=[_SMEM_SPEC, spec(0), spec(1), spec(2)], out_specs=[ospec, ospec],
        out_shape=[SDS((B * S, WIDTH), BF16), SDS((B * S, WIDTH), F32)],
        scratch_shapes=[pltpu.VMEM((S, 128), F32)] * 3 + [blocks(NB, BF16), blocks(NB + 1, BF16), blocks(NB + 1, BF16),
                                                         blocks(NB, F32), blocks(NB, F32)] + [pltpu.VMEM((S, 128), F32)] * 2,
        compiler_params=_cp(), name=name,
    )(_alibi_slopes(), z, z, z)


def _dil_attn_bwd(z, dy, ya, lse, *, B, S, name):
    NB = S // BLOCK

    def body(slope_ref, q_ref, k_ref, v_ref, do_ref, o_ref, lse_ref, dq_ref, dk_ref, dv_ref,
             qf, kf, vf, dof, ef, qd, dod, kd, vd, lsd, ed, dkd, dvd, dqa, dka, dva):
        (m_first, m_second), first = _pair_masks()
        p = pl.program_id(1)
        qf[...] = q_ref[...].astype(F32)
        kf[...] = k_ref[...].astype(F32)
        vf[...] = v_ref[...].astype(F32)
        dov = do_ref[...].astype(F32)
        dof[...] = dov
        prod = dov * o_ref[...].astype(F32)
        ef[...] = jnp.where(first, jnp.sum(jnp.where(first, prod, 0.0), axis=1, keepdims=True),
                            jnp.sum(jnp.where(first, 0.0, prod), axis=1, keepdims=True))
        kd[0] = jnp.zeros((BLOCK, 128), BF16)
        vd[0] = jnp.zeros((BLOCK, 128), BF16)
        blk = lax.broadcasted_iota(jnp.int32, (NB, 1, 1), 0)

        for idx, (_, dilation) in enumerate(DIL_CONFIGS):
            nb = NB // dilation
            _to_residue_major(qd, qf, dilation, nb)
            _to_residue_major(dod, dof, dilation, nb)
            _to_residue_major(kd, kf, dilation, nb, lead=1)
            _to_residue_major(vd, vf, dilation, nb, lead=1)
            _to_residue_major(lsd, lse_ref, dilation, nb)
            _to_residue_major(ed, ef, dilation, nb)
            q4, do4, kc, vc = qd[...], dod[...], kd[1:NB + 1], vd[1:NB + 1]
            dq4 = None
            dkc = dvc = dkp = dvp = None
            for e, hm in enumerate((m_first, m_second)):
                lane0 = slice(HEAD_DIM * e, HEAD_DIM * e + 1)
                bias_cur, bias_prev = _band_bias(slope_ref[2 * p + e], dilation)
                qm, dom = q4 * hm, do4 * hm
                lse_e, e_e = lsd[...][:, :, lane0], ed[...][:, :, lane0]
                pc = jnp.exp(_dot(qm, kc, BNT) * 0.125 + bias_cur - lse_e)
                dsc = (pc * (_dot(dom, vc, BNT) - e_e)).astype(BF16)
                pcb = pc.astype(BF16)
                dqe = _dot(dsc, kc, BNN)
                dkc = _dot(dsc, qm, BTN) if e == 0 else dkc + _dot(dsc, qm, BTN)
                dvc = _dot(pcb, dom, BTN) if e == 0 else dvc + _dot(pcb, dom, BTN)
                if nb > 1:
                    kp, vp = kd[0:NB], vd[0:NB]
                    pp = jnp.exp(_dot(qm, kp, BNT) * 0.125 + jnp.where(blk % nb == 0, NEG, bias_prev) - lse_e)
                    dsp = (pp * (_dot(dom, vp, BNT) - e_e)).astype(BF16)
                    ppb = pp.astype(BF16)
                    dqe = dqe + _dot(dsp, kp, BNN)
                    dkp = _dot(dsp, qm, BTN) if e == 0 else dkp + _dot(dsp, qm, BTN)
                    dvp = _dot(ppb, dom, BTN) if e == 0 else dvp + _dot(ppb, dom, BTN)
                dq4 = dqe if e == 0 else jnp.where(first, dq4, dqe)

            dkd[1:NB + 1] = dkc
            dvd[1:NB + 1] = dvc
            if nb > 1:
                dkd[1:NB] += dkp[1:NB]
                dvd[1:NB] += dvp[1:NB]
            L = nb * BLOCK
            for r in range(dilation):
                rows = pl.ds(r, L, stride=dilation) if dilation > 1 else slice(None)
                dq_r = dq4[r * nb:(r + 1) * nb].reshape(L, 128) * 0.125
                dk_r = dkd[1 + r * nb:1 + (r + 1) * nb].reshape(L, 128) * 0.125
                dv_r = dvd[1 + r * nb:1 + (r + 1) * nb].reshape(L, 128)
                if idx == 0:
                    dqa[rows, :], dka[rows, :], dva[rows, :] = dq_r, dk_r, dv_r
                else:
                    dqa[rows, :] += dq_r
                    dka[rows, :] += dk_r
                    dva[rows, :] += dv_r

        dq_ref[...] = dqa[...].astype(BF16)
        dk_ref[...] = dka[...].astype(BF16)
        dv_ref[...] = dva[...].astype(BF16)

    spec = lambda off: pl.BlockSpec((S, 128), lambda b, p: (b, 4 * off + p))
    ospec = pl.BlockSpec((S, 128), lambda b, p: (b, p))
    blocks = lambda n, dt: pltpu.VMEM((n, BLOCK, 128), dt)
    return pl.pallas_call(
        body, grid=(B, 4), in_specs=[_SMEM_SPEC, spec(0), spec(1), spec(2), ospec, ospec, ospec],
        out_specs=[ospec] * 3, out_shape=[SDS((B * S, WIDTH), BF16)] * 3,
        scratch_shapes=[pltpu.VMEM((S, 128), F32)] * 5
        + [blocks(NB, BF16), blocks(NB, BF16), blocks(NB + 1, BF16), blocks(NB + 1, BF16), blocks(NB, F32), blocks(NB, F32),
           blocks(NB + 1, F32), blocks(NB + 1, F32)] + [pltpu.VMEM((S, 128), F32)] * 3,
        compiler_params=_cp(), name=name,
    )(_alibi_slopes(), z, z, z, dy, ya, lse)


FOX_TQ = 256


def _fox_fwd(z, cc, cr, *, B, S, name):
    def body(q_ref, k_ref, v_ref, cc_ref, cr_ref, o_ref):
        (m_first, m_second), first = _pair_masks()
        for qi in range(S // FOX_TQ):
            r0, kend = qi * FOX_TQ, (qi + 1) * FOX_TQ
            q2 = q_ref[r0:kend, :]
            kk, vv = k_ref[0:kend, :], v_ref[0:kend, :]
            row = lax.broadcasted_iota(jnp.int32, (FOX_TQ, kend), 0) + r0
            col = lax.broadcasted_iota(jnp.int32, (FOX_TQ, kend), 1)
            causal = col <= row
            outs = []
            for e, hm in enumerate((m_first, m_second)):
                s = _dot(q2 * hm, kk, NT) * 0.125
                s = s + (cc_ref[r0:kend, :][:, HEAD_DIM * e:HEAD_DIM * e + 1] - cr_ref[e:e + 1, 0:kend])
                s = jnp.where(causal, s, NEG)
                m = jnp.max(s, axis=1, keepdims=True)
                pe = jnp.exp(s - m)
                l = jnp.sum(pe, axis=1, keepdims=True)
                outs.append(_dot(pe.astype(BF16), vv) * (1.0 / l))
            o_ref[r0:kend, :] = jnp.where(first, outs[0], outs[1]).astype(BF16)

    spec = lambda off: pl.BlockSpec((S, 128), lambda b, p: (b, 4 * off + p))
    return pl.pallas_call(
        body, grid=(B, 4),
        in_specs=[spec(3), spec(4), spec(5), pl.BlockSpec((S, 128), lambda b, p: (b, p)),
                  pl.BlockSpec((None, 8, S), lambda b, p: (4 * b + p, 0, 0))],
        out_specs=pl.BlockSpec((S, 128), lambda b, p: (b, p)), out_shape=SDS((B * S, WIDTH), BF16),
        compiler_params=_cp(), name=name,
    )(z, z, z, cc, cr)


def _fox_bwd(z, dy, cc, cr, *, B, S, name):
    def body(q_ref, k_ref, v_ref, do_ref, cc_ref, cr_ref, dq_ref, dk_ref, dv_ref, dc_ref, dk_s, dv_s, dc_s):
        (m_first, m_second), first = _pair_masks()
        dk_s[...] = jnp.zeros_like(dk_s)
        dv_s[...] = jnp.zeros_like(dv_s)
        dc_s[...] = jnp.zeros_like(dc_s)
        for qi in range(S // FOX_TQ):
            r0, kend = qi * FOX_TQ, (qi + 1) * FOX_TQ
            q2, do2 = q_ref[r0:kend, :], do_ref[r0:kend, :]
            kk, vv = k_ref[0:kend, :], v_ref[0:kend, :]
            krow = lax.broadcasted_iota(jnp.int32, (kend, FOX_TQ), 0)
            qcol = lax.broadcasted_iota(jnp.int32, (kend, FOX_TQ), 1) + r0
            causal = krow <= qcol
            dq_t = jnp.zeros((FOX_TQ, 128), F32)
            for e, hm in enumerate((m_first, m_second)):
                sel = first if e == 0 else ~first
                km = kk * hm
                st = _dot(km, q2, NT) * 0.125
                st = st + (cr_ref[e:e + 1, r0:kend] - cc_ref[0:kend, :][:, HEAD_DIM * e:HEAD_DIM * e + 1])
                st = jnp.where(causal, st, NEG)
                pt = jnp.exp(st - jnp.max(st, axis=0, keepdims=True))
                pt = pt * (1.0 / jnp.sum(pt, axis=0, keepdims=True))
                dpt = _dot(vv * hm, do2, NT)
                dst = pt * (dpt - jnp.sum(pt * dpt, axis=0, keepdims=True))
                dsb = dst.astype(BF16)
                dv_s[0:kend, :] += _dot(pt.astype(BF16), do2 * hm)
                dk_s[0:kend, :] += _dot(dsb, q2 * hm) * 0.125
                dq_t = dq_t + _dot(dsb, km, TN)
                dc_s[0:kend, :] += jnp.where(sel, -jnp.sum(dst, axis=1, keepdims=True), 0.0)
            dq_ref[r0:kend, :] = (dq_t * 0.125).astype(BF16)
        dk_ref[...] = dk_s[...].astype(BF16)
        dv_ref[...] = dv_s[...].astype(BF16)
        dc_ref[...] = dc_s[...]

    spec = lambda off: pl.BlockSpec((S, 128), lambda b, p: (b, 4 * off + p))
    pspec = pl.BlockSpec((S, 128), lambda b, p: (b, p))
    return pl.pallas_call(
        body, grid=(B, 4),
        in_specs=[spec(3), spec(4), spec(5), pl.BlockSpec((S, 128), lambda b, p: (b, 4 + p)), pspec,
                  pl.BlockSpec((None, 8, S), lambda b, p: (4 * b + p, 0, 0))],
        out_specs=[pspec] * 4,
        out_shape=[SDS((B * S, WIDTH), BF16)] * 3 + [SDS((B * S, WIDTH), F32)],
        scratch_shapes=[pltpu.VMEM((S, 128), F32)] * 3, compiler_params=_cp(), name=name,
    )(z, z, z, dy, cc, cr)


def _xattn_fwd(q, kv, *, B, S, M, tq, name):
    D = D_MODEL

    def body(q_ref, kv_ref, o_ref):
        for h in range(N_XH):
            cs = slice(XHD * h, XHD * (h + 1))
            s = _dot(q_ref[:, cs], kv_ref[:, cs], NT) * (1.0 / 16.0)
            pe = jnp.exp(s - jnp.max(s, axis=1, keepdims=True))
            l = jnp.sum(pe, axis=1, keepdims=True)
            o_ref[:, cs] = (_dot(pe.astype(BF16), kv_ref[:, D + XHD * h:D + XHD * (h + 1)]) * (1.0 / l)).astype(BF16)

    nq = S // tq
    return pl.pallas_call(
        body, grid=(B, nq),
        in_specs=[pl.BlockSpec((tq, D), lambda b, t: (b * nq + t, 0)), pl.BlockSpec((M, 2 * D), lambda b, t: (b, 0))],
        out_specs=pl.BlockSpec((tq, D), lambda b, t: (b * nq + t, 0)), out_shape=SDS((B * S, D), BF16),
        compiler_params=_cp(), name=name,
    )(q, kv)


def _xattn_bwd(q, kv, do, *, B, S, M, tq, name):
    D = D_MODEL

    def body(q_ref, kv_ref, do_ref, dq_ref, dkv_ref):
        t = pl.program_id(1)

        @pl.when(t == 0)
        def _():
            dkv_ref[...] = jnp.zeros_like(dkv_ref)

        for h in range(N_XH):
            cs = slice(XHD * h, XHD * (h + 1))
            vs = slice(D + XHD * h, D + XHD * (h + 1))
            qh, kh, vh, doh = q_ref[:, cs], kv_ref[:, cs], kv_ref[:, vs], do_ref[:, cs]
            s = _dot(qh, kh, NT) * (1.0 / 16.0)
            pe = jnp.exp(s - jnp.max(s, axis=1, keepdims=True))
            pe = pe * (1.0 / jnp.sum(pe, axis=1, keepdims=True))
            dp = _dot(doh, vh, NT)
            ds = (pe * (dp - jnp.sum(pe * dp, axis=1, keepdims=True))).astype(BF16)
            dq_ref[:, cs] = (_dot(ds, kh) * (1.0 / 16.0)).astype(BF16)
            dkv_ref[:, cs] += _dot(ds, qh, TN) * (1.0 / 16.0)
            dkv_ref[:, vs] += _dot(pe.astype(BF16), doh, TN)

    nq = S // tq
    qspec = pl.BlockSpec((tq, D), lambda b, t: (b * nq + t, 0))
    kvspec = pl.BlockSpec((M, 2 * D), lambda b, t: (b, 0))
    return pl.pallas_call(
        body, grid=(B, nq), in_specs=[qspec, kvspec, qspec], out_specs=[qspec, kvspec],
        out_shape=[SDS((B * S, D), BF16), SDS((B * M, 2 * D), F32)], compiler_params=_cp(), name=name,
    )(q, kv, do)


def _assemble_dz(parts, dgate, *, tm, name):
    T = dgate.shape[0]

    def body(*refs):
        o_ref = refs[-1]
        for j in range(6):
            o_ref[:, WIDTH * j:WIDTH * (j + 1)] = refs[j][...]
        o_ref[:, QKV_W:IN_PAD] = refs[6][...]

    wspec = pl.BlockSpec((tm, WIDTH), lambda i: (i, 0))
    return pl.pallas_call(
        body, grid=(T // tm,), in_specs=[wspec] * 6 + [pl.BlockSpec((tm, 128), lambda i: (i, 0))],
        out_specs=pl.BlockSpec((tm, IN_PAD), lambda i: (i, 0)), out_shape=SDS((T, IN_PAD), BF16),
        compiler_params=_cp(), name=name,
    )(*parts, dgate)


def _adamw(parts, w, m, v, *, tr, name):
    R, C = w.shape

    def body(p_ref, w_ref, m_ref, v_ref, g_ref, d_ref, nm_ref, nv_ref):
        g = p_ref[0].astype(F32)
        for d in range(1, N_DEV):
            g = g + p_ref[d].astype(F32)
        m2 = ADAM_B1 * m_ref[...] + (1.0 - ADAM_B1) * g
        v2 = ADAM_B2 * v_ref[...] + (1.0 - ADAM_B2) * (g * g)
        m_hat = m2 / (1.0 - ADAM_B1 ** ADAM_STEP)
        v_hat = v2 / (1.0 - ADAM_B2 ** ADAM_STEP)
        g_ref[...] = g
        d_ref[...] = -ADAM_LR * (m_hat / (jnp.sqrt(v_hat) + ADAM_EPS) + ADAM_WD * w_ref[...])
        nm_ref[...] = m2
        nv_ref[...] = v2

    spec = pl.BlockSpec((tr, C), lambda i: (i, 0))
    return pl.pallas_call(
        body, grid=(R // tr,), in_specs=[pl.BlockSpec((N_DEV, tr, C), lambda i: (0, i, 0)), spec, spec, spec],
        out_specs=[spec] * 4, out_shape=[SDS((R, C), F32)] * 4, compiler_params=_cp(), name=name,
    )(parts, w, m, v)


def _peer(k, x, y, c):
    return (1 - x if k & 4 else x, 1 - y if k & 2 else y, 1 - c if k & 1 else c)


_HBM_SPEC = pl.BlockSpec(memory_space=pltpu.HBM)
_SEM_SPEC = pl.BlockSpec(memory_space=pltpu.SEMAPHORE)
_SPLIT_EFFECT = pltpu.SideEffectType.DATAFLOW_SIDE_EFFECTING


def _split_copies(srcs, lands, send_sems, recv_sems, modes):
    x, y, c = (lax.axis_index(a) for a in AXES)
    me = 4 * x + 2 * y + c
    copies = []
    for i, md in enumerate(modes):
        for k in range(1, N_DEV):
            px, py, pc = _peer(k, x, y, c)
            src = srcs[i] if md == "gather" else srcs[i].at[4 * px + 2 * py + pc]
            j = i * (N_DEV - 1) + k - 1
            copies.append(pltpu.make_async_remote_copy(
                src_ref=src, dst_ref=lands[i].at[me], send_sem=send_sems.at[j], recv_sem=recv_sems.at[j],
                device_id=(px, py, pc), device_id_type=pl.DeviceIdType.MESH))
    return copies


def _exchange_start(arrays, modes, *, name):
    n = len(arrays)
    hbm = lambda a: pltpu.with_memory_space_constraint(a, pltpu.HBM)
    srcs = [hbm(a) for a in arrays]
    lands = [hbm(jnp.broadcast_to(a[None], (N_DEV,) + a.shape)) if md == "gather" else hbm(a) for a, md in zip(arrays, modes)]

    def body(*refs):
        for cp in _split_copies(refs[:n], refs[n:2 * n], refs[2 * n], refs[2 * n + 1], modes):
            cp.start()
        token = refs[-1]
        token[...] = jnp.zeros_like(token)

    sems = pltpu.SemaphoreType.DMA((n * (N_DEV - 1),))
    outs = pl.pallas_call(
        body, name=name, in_specs=[_HBM_SPEC] * (2 * n),
        out_shape=(sems, sems, *[pltpu.HBM(a.shape, a.dtype) for a in srcs + lands], SDS((8, 128), F32)),
        out_specs=(_SEM_SPEC, _SEM_SPEC, *[_HBM_SPEC] * (2 * n), pl.BlockSpec(memory_space=pltpu.VMEM)),
        input_output_aliases={i: 2 + i for i in range(2 * n)},
        compiler_params=pltpu.CompilerParams(has_side_effects=_SPLIT_EFFECT),
    )(*srcs, *lands)
    return (outs[0], outs[1], outs[2:2 + n], outs[2 + n:2 + 2 * n], modes), outs[-1]


def _exchange_wait(handle, after, *, name):
    send_sems, recv_sems, srcs, lands, modes = handle
    n = len(srcs)

    def body(*refs):
        for cp in _split_copies(refs[:n], refs[n:2 * n], refs[2 * n], refs[2 * n + 1], modes):
            cp.wait_send()
            cp.wait_recv()

    outs = pl.pallas_call(
        body, name=name, in_specs=[_HBM_SPEC] * (2 * n) + [_SEM_SPEC, _SEM_SPEC, pl.BlockSpec(memory_space=pl.ANY)],
        out_shape=tuple(pltpu.HBM(a.shape, a.dtype) for a in list(srcs) + list(lands)), out_specs=tuple([_HBM_SPEC] * (2 * n)),
        input_output_aliases={i: i for i in range(2 * n)},
        compiler_params=pltpu.CompilerParams(has_side_effects=_SPLIT_EFFECT),
    )(*srcs, *lands, send_sems, recv_sems, after)
    return list(outs[n:])


def _exchange(arrays, modes, *, name):
    n = len(arrays)
    out_shape = [SDS((N_DEV,) + a.shape if md == "gather" else a.shape, a.dtype) for a, md in zip(arrays, modes)]

    def body(*refs):
        ins, outs = refs[:n], refs[n:2 * n]
        send_sems, recv_sems, local_sems = refs[2 * n:]
        x, y, c = (lax.axis_index(a) for a in AXES)
        me = 4 * x + 2 * y + c
        copies = []
        for i, md in enumerate(modes):
            src = ins[i] if md == "gather" else ins[i].at[me]
            cp = pltpu.make_async_copy(src, outs[i].at[me], local_sems.at[i])
            cp.start()
            copies.append(cp)
            for k in range(1, N_DEV):
                px, py, pc = _peer(k, x, y, c)
                src = ins[i] if md == "gather" else ins[i].at[4 * px + 2 * py + pc]
                cp = pltpu.make_async_remote_copy(
                    src_ref=src, dst_ref=outs[i].at[me], send_sem=send_sems.at[i, k - 1], recv_sem=recv_sems.at[i, k - 1],
                    device_id=(px, py, pc), device_id_type=pl.DeviceIdType.MESH)
                cp.start()
                copies.append(cp)
        for cp in copies:
            cp.wait()

    anyspec = pl.BlockSpec(memory_space=pl.ANY)
    return pl.pallas_call(
        body, in_specs=[anyspec] * n, out_specs=[anyspec] * n, out_shape=out_shape,
        scratch_shapes=[pltpu.SemaphoreType.DMA((n, N_DEV - 1)), pltpu.SemaphoreType.DMA((n, N_DEV - 1)),
                        pltpu.SemaphoreType.DMA((n,))],
        name=name,
    )(*arrays)


def _local_step(x, mem, g_mix, b_forget, g_xattn, g_mem, g_mlp, g_final, target, get_w_in, get_rest, send):
    B, S, D = x.shape
    M = mem.shape[1]
    T = B * S
    x0 = x.reshape(T, D)
    mem2 = mem.reshape(B * M, D)
    tgt = target.reshape(T, D)
    b_pad = jnp.pad(b_forget, (0, 120)).reshape(1, 128)
    after = lambda a, tok: a if tok is None else a + tok[0, 0]

    w_in_pad = get_w_in()
    h1, z = _rms_matmul(x0, g_mix, w_in_pad[:, :QKV_W], tm=1024, tn=768, out_dtype=BF16, name="f_in")
    gate = _matmul_nn(h1, w_in_pad[:, QKV_W:], tm=1024, tn=128, tk=D, out_dtype=F32, name="f_gate")
    c, cc = _gate_fwd(gate, b_pad, B=B, S=S, name="f_gatecum")
    cr = jnp.pad(c[:, :8].reshape(B, S, 4, 2).transpose(0, 2, 3, 1), ((0, 0), (0, 0), (0, 6), (0, 0))).reshape(B * 4, 8, S)
    ya, lse = _dil_attn_fwd(z, B=B, S=S, name="f_dil")
    yf = _fox_fwd(z, cc, cr, B=B, S=S, name="f_fox")
    ymix = jnp.concatenate([ya, yf], axis=1)
    w = get_rest(ymix)
    x1 = _matmul_nn(ymix, w["w_out"], res=x0, tm=1024, tn=D, tk=D, out_dtype=F32, name="f_out")
    h2, q = _rms_matmul(x1, g_xattn, w["w_xq"], tm=1024, tn=D, out_dtype=BF16, name="f_xq")
    mn, kv = _rms_matmul(mem2, g_mem, w["w_kv"], tm=B * M, tn=D, out_dtype=BF16, name="f_xkv")
    xo = _xattn_fwd(q, kv, B=B, S=S, M=M, tq=512, name="f_xattn")
    x2 = _matmul_nn(xo, w["w_xo"], res=x1, tm=1024, tn=D, tk=D, out_dtype=F32, name="f_xo")
    h3, act = _rms_matmul(x2, g_mlp, w["w_up"], tm=1024, tn=1024, out_dtype=BF16, relu=True, name="f_up")
    x3 = _matmul_nn(act, w["w_down"], res=x2, square=True, tm=1024, tn=D, tk=1024, out_dtype=F32, name="f_down")
    dx3, dg_final, loss = _loss_head(x3, g_final, tgt, tm=512, name="f_loss")

    du = _matmul_nt(dx3, w["w_down"], mul2a=act, tm=1024, tn=1024, name="b_dact")
    dw_down = _matmul_tn(act, dx3, square=True, bk=1024, bn=D, tt=512, out_dtype=BF16, name="b_wdown")
    dw_up = _matmul_tn(h3, du, bk=D, bn=1024, tt=512, out_dtype=BF16, name="b_wup")
    tok = send(dict(w_down=dw_down, w_up=dw_up))
    dx2, dg_mlp = _matmul_nt_rms(du, w["w_up"], x2, after(g_mlp, tok), dx3, tm=512, tk=1024, name="b_dh3")
    dxo = _matmul_nt(dx2, w["w_xo"], tm=1024, tn=D, name="b_dxo")
    dw_xo = _matmul_tn(xo, dx2, bk=D, bn=D, tt=512, out_dtype=BF16, name="b_wxo")
    dq, dkv = _xattn_bwd(q, kv, dxo, B=B, S=S, M=M, tq=512, name="b_xattn")
    dw_xq = _matmul_tn(h2, dq, bk=D, bn=D, tt=512, out_dtype=BF16, name="b_wxq")
    dx1, dg_xattn = _matmul_nt_rms(dq, w["w_xq"], x1, g_xattn, dx2, tm=512, tk=D, name="b_dh2")
    dw_kv = _matmul_tn(mn, dkv, bk=D, bn=D, tt=min(512, B * M), out_dtype=BF16, name="b_wkv")
    _, dg_mem = _matmul_nt_rms(dkv, w["w_kv"], mem2, g_mem, None, tm=min(512, B * M), tk=D, name="b_dmem")
    dy = _matmul_nt(dx1, w["w_out"], tm=1024, tn=D, name="b_dy")
    dw_out = _matmul_tn(ymix, dx1, bk=D, bn=D, tt=512, out_dtype=BF16, name="b_wout")
    tok = send(dict(w_xo=dw_xo, w_xq=dw_xq, w_xk=dw_kv[:, :D], w_xv=dw_kv[:, D:], w_out=dw_out))
    dqf, dkf, dvf, dcc = _fox_bwd(z, dy, cc, after(cr, tok), B=B, S=S, name="b_fox")
    dgate, db = _gate_bwd(dcc, gate, b_pad, B=B, S=S, name="b_gate")
    dqa, dka, dva = _dil_attn_bwd(z, dy, ya, lse, B=B, S=S, name="b_dil")
    dz = _assemble_dz([dqa, dka, dva, dqf, dkf, dvf], dgate, tm=512, name="b_dz")
    dw_in = _matmul_tn(h1, dz, bk=D, bn=640, tt=512, out_dtype=BF16, name="b_win")
    tok = send(dict(w_in=dw_in))
    gx, dg_mix = _matmul_nt_rms(dz, w_in_pad, x0, after(g_mix, tok), dx1, tm=512, tk=640, name="b_dh1")

    small = dict(g_mix=dg_mix, b_forget=db, g_xattn=dg_xattn, g_mem=dg_mem, g_mlp=dg_mlp, g_final=dg_final)
    return gx.reshape(B, S, D), small, loss


SMALL_ROWS = ("g_mix", "b_forget", "g_xattn", "g_mem", "g_mlp", "g_final")
COL_SHARDED = ("w_in", "w_up")


def _pack_rows(rows):
    D = D_MODEL
    rows = [jnp.pad(r.reshape(-1), (0, D - r.size)) for r in rows]
    rows += [jnp.zeros((D,), F32)] * (8 - len(rows))
    return jnp.stack(rows)


def _full(name, g):
    if name in COL_SHARDED:
        return g.transpose(1, 0, 2).reshape(g.shape[1], -1)
    return g.reshape(-1, g.shape[2])


def _blocks(name, g, shard_shape):
    if name in COL_SHARDED:
        n = shard_shape[1]
        return g[:, :n * N_DEV].reshape(g.shape[0], N_DEV, n).transpose(1, 0, 2)
    return g.reshape((N_DEV,) + shard_shape)


def kernel(x, mem, g_mix, w_in, b_forget, w_out, g_xattn, g_mem, w_xq, w_xk, w_xv, w_xo, g_mlp, w_up, w_down, g_final, loss_target, m_g_mix, m_w_in, m_b_forget, m_w_out, m_g_xattn, m_g_mem, m_w_xq, m_w_xk, m_w_xv, m_w_xo, m_g_mlp, m_w_up, m_w_down, m_g_final, v_g_mix, v_w_in, v_b_forget, v_w_out, v_g_xattn, v_g_mem, v_w_xq, v_w_xk, v_w_xv, v_w_xo, v_g_mlp, v_w_up, v_w_down, v_g_final):
    W = dict(w_in=w_in, w_out=w_out, w_xq=w_xq, w_xk=w_xk, w_xv=w_xv, w_xo=w_xo, w_up=w_up, w_down=w_down)
    Mo = dict(w_in=m_w_in, w_out=m_w_out, w_xq=m_w_xq, w_xk=m_w_xk, w_xv=m_w_xv, w_xo=m_w_xo, w_up=m_w_up, w_down=m_w_down)
    Vo = dict(w_in=v_w_in, w_out=v_w_out, w_xq=v_w_xq, w_xk=v_w_xk, w_xv=v_w_xv, w_xo=v_w_xo, w_up=v_w_up, w_down=v_w_down)
    later = [n for n in W if n != "w_in"]

    first_handle, first_token = _exchange_start([w_in.astype(BF16)], ["gather"], name="gather_in_start")
    rest_handle, rest_token = _exchange_start([W[n].astype(BF16) + first_token[0, 0].astype(BF16) for n in later],
                                              ["gather"] * len(later), name="gather_rest_start")

    def get_w_in():
        (g,) = _exchange_wait(first_handle, rest_token, name="gather_in_wait")
        return jnp.pad(_full("w_in", g), ((0, 0), (0, IN_PAD - IN_W)))

    def get_rest(after):
        full = {n: _full(n, g) for n, g in zip(later, _exchange_wait(rest_handle, after, name="gather_rest_wait"))}
        full["w_kv"] = jnp.concatenate([full.pop("w_xk"), full.pop("w_xv")], axis=1)
        return full

    sent = []

    def send(grads):
        names = list(grads)
        handle, token = _exchange_start([_blocks(n, grads[n], W[n].shape) for n in names], ["scatter"] * len(names),
                                        name=f"scatter{len(sent)}_start")
        sent.append((names, handle))
        return token

    gx, small, loss = _local_step(x, mem, g_mix, b_forget, g_xattn, g_mem, g_mlp, g_final, loss_target, get_w_in, get_rest, send)

    received = {}
    for i, (names, handle) in enumerate(sent):
        received.update(zip(names, _exchange_wait(handle, gx, name=f"scatter{i}_wait")))
    packed = _pack_rows([small[n] for n in SMALL_ROWS] + [loss[0, :1]])
    (packed_all,) = _exchange([packed], ["gather"], name="gather_small")

    res = {n: _adamw(received[n], W[n], Mo[n], Vo[n], tr=128, name=f"adamw_{n}") for n in W}
    small_w = dict(g_mix=g_mix, b_forget=b_forget, g_xattn=g_xattn, g_mem=g_mem, g_mlp=g_mlp, g_final=g_final)
    small_m = dict(g_mix=m_g_mix, b_forget=m_b_forget, g_xattn=m_g_xattn, g_mem=m_g_mem, g_mlp=m_g_mlp, g_final=m_g_final)
    small_v = dict(g_mix=v_g_mix, b_forget=v_b_forget, g_xattn=v_g_xattn, g_mem=v_g_mem, g_mlp=v_g_mlp, g_final=v_g_final)
    sres = _adamw(packed_all, _pack_rows([small_w[n] for n in SMALL_ROWS]), _pack_rows([small_m[n] for n in SMALL_ROWS]),
                  _pack_rows([small_v[n] for n in SMALL_ROWS]), tr=8, name="adamw_small")
    for i, n in enumerate(SMALL_ROWS):
        res[n] = [r[i, :small_w[n].size] for r in sres]
    loss_total = sres[0][6, 0]

    order = ["g_mix", "w_in", "b_forget", "w_out", "g_xattn", "g_mem", "w_xq", "w_xk", "w_xv", "w_xo", "g_mlp", "w_up", "w_down", "g_final"]
    return (loss_total, gx, *[res[n][0] for n in order], *[res[n][1] for n in order],
            *[res[n][2] for n in order], *[res[n][3] for n in order])
```

```python
import jax
import jax.numpy as jnp
from jax import lax
from jax.experimental import pallas as pl
from jax.experimental.pallas import tpu as pltpu

F32, BF16 = jnp.float32, jnp.bfloat16
SDS = jax.ShapeDtypeStruct

D_MODEL = 1024
HEAD_DIM = 64
WIDTH = 512
QKV_W = 6 * WIDTH
IN_W = QKV_W + 8
IN_PAD = QKV_W + 128
BLOCK = 128
DIL_CONFIGS = ((128, 1), (512, 4), (2048, 16))
N_XH, XHD = 4, 256
D_FF = 4096
EPS = 1e-6
NEG = -1e30
N_DEV = 8
AXES = ("x", "y", "c")

ADAM_LR, ADAM_B1, ADAM_B2, ADAM_EPS, ADAM_WD, ADAM_STEP = 0.001, 0.9, 0.999, 1e-08, 0.01, 10

VMEM_CAP_V7X = 64 * 1024 * 1024
VMEM_LIMIT = VMEM_CAP_V7X * 7 // 8

ROWS = 512
ACC_ROWS = 2048

NT = (((1,), (1,)), ((), ()))
TN = (((0,), (0,)), ((), ()))


def _cp(**kw):
    return pltpu.CompilerParams(vmem_limit_bytes=VMEM_LIMIT, **kw)


def _dot(a, b, dims=None):
    if dims is None:
        return jnp.dot(a, b, preferred_element_type=F32)
    return lax.dot_general(a, b, dims, preferred_element_type=F32)


def _rstd(xv):
    return lax.rsqrt(jnp.mean(xv * xv, axis=-1, keepdims=True) + EPS)


def _rms_bwd(dh, xv, g):
    r = _rstd(xv)
    xhat = xv * r
    dxhat = dh * g
    dx = r * (dxhat - xhat * jnp.mean(dxhat * xhat, axis=-1, keepdims=True))
    return dx, jnp.sum(dh * xhat, axis=0, keepdims=True)


def _rms_matmul(x, g, w, *, tm, tn, out_dtype, relu=False, name):
    T, D = x.shape
    N = w.shape[1]

    def body(x_ref, g_ref, w_ref, h_ref, o_ref, h_s):
        @pl.when(pl.program_id(1) == 0)
        def _():
            xv = x_ref[...]
            h = (xv * _rstd(xv) * g_ref[...]).astype(BF16)
            h_s[...] = h
            h_ref[...] = h

        acc = _dot(h_s[...], w_ref[...])
        if relu:
            acc = jnp.maximum(acc, 0.0)
        o_ref[...] = acc.astype(out_dtype)

    return pl.pallas_call(
        body, grid=(T // tm, N // tn),
        in_specs=[pl.BlockSpec((tm, D), lambda i, j: (i, 0)), pl.BlockSpec((1, D), lambda i, j: (0, 0)),
                  pl.BlockSpec((D, tn), lambda i, j: (0, j))],
        out_specs=[pl.BlockSpec((tm, D), lambda i, j: (i, 0)), pl.BlockSpec((tm, tn), lambda i, j: (i, j))],
        out_shape=[SDS((T, D), BF16), SDS((T, N), out_dtype)],
        scratch_shapes=[pltpu.VMEM((tm, D), BF16)], compiler_params=_cp(), name=name,
    )(x, g.reshape(1, D), w)


def _matmul_nn(a, w, *, res=None, square=False, tm, tn, tk, out_dtype, name):
    T, K = a.shape
    N = w.shape[1]
    nk = K // tk

    def body(*refs):
        a_ref, w_ref = refs[0], refs[1]
        res_ref = refs[2] if res is not None else None
        o_ref = refs[3] if res is not None else refs[2]
        k = pl.program_id(2)
        av = a_ref[...]
        if square:
            af = av.astype(F32)
            av = (af * af).astype(BF16)
        part = _dot(av, w_ref[...])

        def finish(r):
            if res_ref is not None:
                r = res_ref[...] + r
            o_ref[...] = r.astype(out_dtype)

        if nk == 1:
            finish(part)
        else:
            acc = refs[-1]

            @pl.when(k == 0)
            def _():
                acc[...] = part

            @pl.when(k > 0)
            def _():
                acc[...] += part

            @pl.when(k == nk - 1)
            def _():
                finish(acc[...])

    in_specs = [pl.BlockSpec((tm, tk), lambda i, j, k: (i, k)), pl.BlockSpec((tk, tn), lambda i, j, k: (k, j))]
    args = [a, w]
    if res is not None:
        in_specs.append(pl.BlockSpec((tm, tn), lambda i, j, k: (i, j)))
        args.append(res)
    return pl.pallas_call(
        body, grid=(T // tm, N // tn, nk), in_specs=in_specs,
        out_specs=pl.BlockSpec((tm, tn), lambda i, j, k: (i, j)), out_shape=SDS((T, N), out_dtype),
        scratch_shapes=[pltpu.VMEM((tm, tn), F32)] if nk > 1 else [], compiler_params=_cp(), name=name,
    )(*args)


def _matmul_nt(g, w, *, mul2a=None, tm, tn, name):
    T, K = g.shape
    N = w.shape[0]

    def body(*refs):
        g_ref, w_ref = refs[0], refs[1]
        o_ref = refs[-1]
        acc = _dot(g_ref[...].astype(BF16), w_ref[...], NT)
        if mul2a is not None:
            acc = acc * (2.0 * refs[2][...].astype(F32))
        o_ref[...] = acc.astype(BF16)

    in_specs = [pl.BlockSpec((tm, K), lambda i, j: (i, 0)), pl.BlockSpec((tn, K), lambda i, j: (j, 0))]
    args = [g, w]
    if mul2a is not None:
        in_specs.append(pl.BlockSpec((tm, tn), lambda i, j: (i, j)))
        args.append(mul2a)
    return pl.pallas_call(
        body, grid=(T // tm, N // tn), in_specs=in_specs,
        out_specs=pl.BlockSpec((tm, tn), lambda i, j: (i, j)), out_shape=SDS((T, N), BF16),
        compiler_params=_cp(), name=name,
    )(*args)


def _matmul_nt_rms(g, w, x, gain, dres, *, tm, tk, name):
    T, K = g.shape
    D = w.shape[0]
    nk = K // tk
    nt = T // tm

    def body(*refs):
        g_ref, w_ref, x_ref, gain_ref = refs[:4]
        dres_ref = refs[4] if dres is not None else None
        n_in = 5 if dres is not None else 4
        dx_ref, dg_ref = refs[n_in], refs[n_in + 1]
        i, k = pl.program_id(0), pl.program_id(1)
        part = _dot(g_ref[...].astype(BF16), w_ref[...], NT)

        def finish(dh):
            dx, dg = _rms_bwd(dh, x_ref[...], gain_ref[...])
            if dres_ref is not None:
                dx = dres_ref[...] + dx
            dx_ref[...] = dx

            @pl.when(i == 0)
            def _():
                dg_ref[...] = dg

            @pl.when(i > 0)
            def _():
                dg_ref[...] += dg

        if nk == 1:
            finish(part)
        else:
            acc = refs[-1]

            @pl.when(k == 0)
            def _():
                acc[...] = part

            @pl.when(k > 0)
            def _():
                acc[...] += part

            @pl.when(k == nk - 1)
            def _():
                finish(acc[...])

    in_specs = [pl.BlockSpec((tm, tk), lambda i, k: (i, k)), pl.BlockSpec((D, tk), lambda i, k: (0, k)),
                pl.BlockSpec((tm, D), lambda i, k: (i, 0)), pl.BlockSpec((1, D), lambda i, k: (0, 0))]
    args = [g, w, x, gain.reshape(1, D)]
    if dres is not None:
        in_specs.append(pl.BlockSpec((tm, D), lambda i, k: (i, 0)))
        args.append(dres)
    return pl.pallas_call(
        body, grid=(nt, nk), in_specs=in_specs,
        out_specs=[pl.BlockSpec((tm, D), lambda i, k: (i, 0)), pl.BlockSpec((1, D), lambda i, k: (0, 0))],
        out_shape=[SDS((T, D), F32), SDS((1, D), F32)],
        scratch_shapes=[pltpu.VMEM((tm, D), F32)] if nk > 1 else [], compiler_params=_cp(), name=name,
    )(*args)


def _matmul_tn(a, g, *, square=False, bk, bn, tt, out_dtype, name):
    T, K = a.shape
    N = g.shape[1]
    nt = T // tt

    def body(a_ref, g_ref, o_ref, acc):
        t = pl.program_id(2)
        av = a_ref[...]
        if square:
            af = av.astype(F32)
            av = (af * af).astype(BF16)
        part = _dot(av, g_ref[...].astype(BF16), TN)
        if nt == 1:
            o_ref[...] = part.astype(out_dtype)
        else:
            @pl.when(t == 0)
            def _():
                acc[...] = part

            @pl.when((t > 0) & (t < nt - 1))
            def _():
                acc[...] += part

            @pl.when(t == nt - 1)
            def _():
                o_ref[...] = (acc[...] + part).astype(out_dtype)

    return pl.pallas_call(
        body, grid=(K // bk, N // bn, nt),
        in_specs=[pl.BlockSpec((tt, bk), lambda i, j, t: (t, i)), pl.BlockSpec((tt, bn), lambda i, j, t: (t, j))],
        out_specs=pl.BlockSpec((bk, bn), lambda i, j, t: (i, j)), out_shape=SDS((K, N), out_dtype),
        scratch_shapes=[pltpu.VMEM((bk, bn), F32)], compiler_params=_cp(), name=name,
    )(a, g)


def _loss_head(x3, g_final, target, *, tm, name):
    T, D = x3.shape

    def body(x_ref, g_ref, t_ref, dx_ref, dg_ref, loss_ref):
        i = pl.program_id(0)
        xv, g = x_ref[...], g_ref[...]
        r = _rstd(xv)
        xhat = xv * r
        diff = xhat * g - t_ref[...]
        part = 0.5 * jnp.sum(jnp.mean(diff * diff, axis=-1, keepdims=True), axis=0, keepdims=True)
        dy = diff * (1.0 / D)
        dxhat = dy * g
        dx_ref[...] = r * (dxhat - xhat * jnp.mean(dxhat * xhat, axis=-1, keepdims=True))
        dg = jnp.sum(dy * xhat, axis=0, keepdims=True)
        lp = jnp.broadcast_to(part, loss_ref.shape)

        @pl.when(i == 0)
        def _():
            dg_ref[...] = dg
            loss_ref[...] = lp

        @pl.when(i > 0)
        def _():
            dg_ref[...] += dg
            loss_ref[...] += lp

    return pl.pallas_call(
        body, grid=(T // tm,),
        in_specs=[pl.BlockSpec((tm, D), lambda i: (i, 0)), pl.BlockSpec((1, D), lambda i: (0, 0)),
                  pl.BlockSpec((tm, D), lambda i: (i, 0))],
        out_specs=[pl.BlockSpec((tm, D), lambda i: (i, 0)), pl.BlockSpec((1, D), lambda i: (0, 0)),
                   pl.BlockSpec((8, 128), lambda i: (0, 0))],
        out_shape=[SDS((T, D), F32), SDS((1, D), F32), SDS((8, 128), F32)],
        compiler_params=_cp(), name=name,
    )(x3, g_final.reshape(1, D), target)


def _head_lanes(shape, width):
    return lax.broadcasted_iota(jnp.int32, shape, len(shape) - 1) // width


def _gate_fwd(gate, b_pad, *, B, S, name):
    def body(g_ref, b_ref, c_ref, cc_ref):
        xv = g_ref[...] + b_ref[...]
        lf = jnp.minimum(xv, 0.0) - jnp.log(1.0 + jnp.exp(-jnp.abs(xv)))
        lane = lax.broadcasted_iota(jnp.int32, lf.shape, 1)
        row = lax.broadcasted_iota(jnp.int32, lf.shape, 0)
        c = jnp.where(lane < 8, lf, 0.0)
        sh = 1
        while sh < S:
            c = c + jnp.where(row >= sh, pltpu.roll(c, sh, 0), 0.0)
            sh *= 2
        c_ref[...] = c
        grp = _head_lanes((S, WIDTH), HEAD_DIM)
        cc = jnp.zeros((S, WIDTH), F32)
        for h in range(8):
            cc = jnp.where(grp == h, c[:, h:h + 1], cc)
        cc_ref[...] = cc

    return pl.pallas_call(
        body, grid=(B,),
        in_specs=[pl.BlockSpec((S, 128), lambda b: (b, 0)), pl.BlockSpec((1, 128), lambda b: (0, 0))],
        out_specs=[pl.BlockSpec((S, 128), lambda b: (b, 0)), pl.BlockSpec((S, WIDTH), lambda b: (b, 0))],
        out_shape=[SDS((B * S, 128), F32), SDS((B * S, WIDTH), F32)],
        compiler_params=_cp(), name=name,
    )(gate, b_pad)


def _gate_bwd(dcc, gate, b_pad, *, B, S, name):
    def body(dcc_ref, g_ref, b_ref, dg_ref, db_ref):
        bi = pl.program_id(0)
        dccv = dcc_ref[...]
        lane = lax.broadcasted_iota(jnp.int32, (S, 128), 1)
        row = lax.broadcasted_iota(jnp.int32, (S, 128), 0)
        dc = jnp.zeros((S, 128), F32)
        for h in range(8):
            dc = jnp.where(lane == h, dccv[:, HEAD_DIM * h:HEAD_DIM * h + 1], dc)
        sh = 1
        while sh < S:
            dc = dc + jnp.where(row < S - sh, pltpu.roll(dc, S - sh, 0), 0.0)
            sh *= 2
        xv = g_ref[...] + b_ref[...]
        dgate = jnp.where(lane < 8, dc / (1.0 + jnp.exp(xv)), 0.0)
        dg_ref[...] = dgate.astype(BF16)
        db = jnp.sum(dgate, axis=0, keepdims=True)

        @pl.when(bi == 0)
        def _():
            db_ref[...] = db

        @pl.when(bi > 0)
        def _():
            db_ref[...] += db

    return pl.pallas_call(
        body, grid=(B,),
        in_specs=[pl.BlockSpec((S, WIDTH), lambda b: (b, 0)), pl.BlockSpec((S, 128), lambda b: (b, 0)),
                  pl.BlockSpec((1, 128), lambda b: (0, 0))],
        out_specs=[pl.BlockSpec((S, 128), lambda b: (b, 0)), pl.BlockSpec((1, 128), lambda b: (0, 0))],
        out_shape=[SDS((B * S, 128), BF16), SDS((1, 128), F32)],
        compiler_params=_cp(), name=name,
    )(dcc, gate, b_pad)


_SMEM_SPEC = pl.BlockSpec(memory_space=pltpu.SMEM)


def _alibi_slopes():
    return 2.0 ** (-(jnp.arange(1, 9, dtype=F32) * (8.0 / 8)))


def _pair_masks():
    lane = lax.broadcasted_iota(jnp.int32, (1, 128), 1)
    first = lane < HEAD_DIM
    return (first.astype(BF16), (~first).astype(BF16)), first


BNT =(((2,), (2,)), ((0,), (0,)))
BNN = (((2,), (1,)), ((0,), (0,)))
BTN = (((1,), (1,)), ((0,), (0,)))


def _band_bias(slope, dilation):
    qi = lax.broadcasted_iota(jnp.int32, (BLOCK, BLOCK), 0)
    kj = lax.broadcasted_iota(jnp.int32, (BLOCK, BLOCK), 1)
    cur = jnp.where(kj <= qi, (-slope * dilation) * (qi - kj).astype(F32), NEG)
    prev = jnp.where(kj >= qi, (-slope * dilation) * (qi + BLOCK - kj).astype(F32), NEG)
    return cur, prev


def _to_residue_major(dst, src_f32, dilation, nb, lead=0):
    L = nb * BLOCK
    for r in range(dilation):
        rows = src_f32[pl.ds(r, L, stride=dilation), :] if dilation > 1 else src_f32[...]
        dst[lead + r * nb:lead + (r + 1) * nb] = rows.reshape(nb, BLOCK, 128).astype(dst.dtype)


def _dil_attn_fwd(z, *, B, S, name):
    NB = S // BLOCK

    def body(slope_ref, q_ref, k_ref, v_ref, y_ref, lse_ref, qf, kf, vf, qd, kd, vd, od, ld, acc_o, acc_l):
        (m_first, m_second), first = _pair_masks()
        p = pl.program_id(1)
        qf[...] = q_ref[...].astype(F32)
        kf[...] = k_ref[...].astype(F32)
        vf[...] = v_ref[...].astype(F32)
        kd[0] = jnp.zeros((BLOCK, 128), BF16)
        vd[0] = jnp.zeros((BLOCK, 128), BF16)
        blk = lax.broadcasted_iota(jnp.int32, (NB, 1, 1), 0)

        for idx, (_, dilation) in enumerate(DIL_CONFIGS):
            nb = NB // dilation
            _to_residue_major(qd, qf, dilation, nb)
            _to_residue_major(kd, kf, dilation, nb, lead=1)
            _to_residue_major(vd, vf, dilation, nb, lead=1)
            q4, kc, vc = qd[...], kd[1:NB + 1], vd[1:NB + 1]
            outs, lses = [], []
            for e, hm in enumerate((m_first, m_second)):
                bias_cur, bias_prev = _band_bias(slope_ref[2 * p + e], dilation)
                qm = q4 * hm
                sc = _dot(qm, kc, BNT) * 0.125 + bias_cur
                m = jnp.max(sc, axis=2, keepdims=True)
                if nb > 1:
                    sp = _dot(qm, kd[0:NB], BNT) * 0.125 + jnp.where(blk % nb == 0, NEG, bias_prev)
                    m = jnp.maximum(m, jnp.max(sp, axis=2, keepdims=True))
                pc = jnp.exp(sc - m)
                l = jnp.sum(pc, axis=2, keepdims=True)
                o = _dot(pc.astype(BF16), vc, BNN)
                if nb > 1:
                    pp = jnp.exp(sp - m)
                    l = l + jnp.sum(pp, axis=2, keepdims=True)
                    o = o + _dot(pp.astype(BF16), vd[0:NB], BNN)
                outs.append(o * (1.0 / l))
                lses.append(m + jnp.log(l))
            od[...] = jnp.where(first, outs[0], outs[1])
            ld[...] = jnp.where(first, lses[0], lses[1])

            L = nb * BLOCK
            for r in range(dilation):
                rows = pl.ds(r, L, stride=dilation) if dilation > 1 else slice(None)
                o_new = od[r * nb:(r + 1) * nb].reshape(L, 128)
                l_new = ld[r * nb:(r + 1) * nb].reshape(L, 128)
                if idx == 0:
                    acc_o[rows, :] = o_new
                    acc_l[rows, :] = l_new
                else:
                    l_old = acc_l[rows, :]
                    m2 = jnp.maximum(l_old, l_new)
                    w_old, w_new = jnp.exp(l_old - m2), jnp.exp(l_new - m2)
                    tot = w_old + w_new
                    acc_o[rows, :] = (w_old * acc_o[rows, :] + w_new * o_new) * (1.0 / tot)
                    acc_l[rows, :] = m2 + jnp.log(tot)

        y_ref[...] = acc_o[...].astype(BF16)
        lse_ref[...] = acc_l[...]

    spec = lambda off: pl.BlockSpec((S, 128), lambda b, p: (b, 4 * off + p))
    ospec = pl.BlockSpec((S, 128), lambda b, p: (b, p))
    blocks = lambda n, dt: pltpu.VMEM((n, BLOCK, 128), dt)
    return pl.pallas_call(
        body, grid=(B, 4), in_specs=[_SMEM_SPEC, spec(0), spec(1), spec(2)], out_specs=[ospec, ospec],
        out_shape=[SDS((B * S, WIDTH), BF16), SDS((B * S, WIDTH), F32)],
        scratch_shapes=[pltpu.VMEM((S, 128), F32)] * 3 + [blocks(NB, BF16), blocks(NB + 1, BF16), blocks(NB + 1, BF16),
                                                         blocks(NB, F32), blocks(NB, F32)] + [pltpu.VMEM((S, 128), F32)] * 2,
        compiler_params=_cp(), name=name,
    )(_alibi_slopes(), z, z, z)


def _dil_attn_bwd(z, dy, ya, lse, *, B, S, name):
    NB = S // BLOCK

    def body(slope_ref, q_ref, k_ref, v_ref, do_ref, o_ref, lse_ref, dq_ref, dk_ref, dv_ref,
             qf, kf, vf, dof, ef, qd, dod, kd, vd, lsd, ed, dkd, dvd, dqa, dka, dva):
        (m_first, m_second), first = _pair_masks()
        p = pl.program_id(1)
        qf[...] = q_ref[...].astype(F32)
        kf[...] = k_ref[...].astype(F32)
        vf[...] = v_ref[...].astype(F32)
        dov = do_ref[...].astype(F32)
        dof[...] = dov
        prod = dov * o_ref[...].astype(F32)
        ef[...] = jnp.where(first, jnp.sum(jnp.where(first, prod, 0.0), axis=1, keepdims=True),
                            jnp.sum(jnp.where(first, 0.0, prod), axis=1, keepdims=True))
        kd[0] = jnp.zeros((BLOCK, 128), BF16)
        vd[0] = jnp.zeros((BLOCK, 128), BF16)
        blk = lax.broadcasted_iota(jnp.int32, (NB, 1, 1), 0)

        for idx, (_, dilation) in enumerate(DIL_CONFIGS):
            nb = NB // dilation
            _to_residue_major(qd, qf, dilation, nb)
            _to_residue_major(dod, dof, dilation, nb)
            _to_residue_major(kd, kf, dilation, nb, lead=1)
            _to_residue_major(vd, vf, dilation, nb, lead=1)
            _to_residue_major(lsd, lse_ref, dilation, nb)
            _to_residue_major(ed, ef, dilation, nb)
            q4, do4, kc, vc = qd[...], dod[...], kd[1:NB + 1], vd[1:NB + 1]
            dq4 = None
            dkc = dvc = dkp = dvp = None
            for e, hm in enumerate((m_first, m_second)):
                lane0 = slice(HEAD_DIM * e, HEAD_DIM * e + 1)
                bias_cur, bias_prev = _band_bias(slope_ref[2 * p + e], dilation)
                qm, dom = q4 * hm, do4 * hm
                lse_e, e_e = lsd[...][:, :, lane0], ed[...][:, :, lane0]
                pc = jnp.exp(_dot(qm, kc, BNT) * 0.125 + bias_cur - lse_e)
                dsc = (pc * (_dot(dom, vc, BNT) - e_e)).astype(BF16)
                pcb = pc.astype(BF16)
                dqe = _dot(dsc, kc, BNN)
                dkc = _dot(dsc, qm, BTN) if e == 0 else dkc + _dot(dsc, qm, BTN)
                dvc = _dot(pcb, dom, BTN) if e == 0 else dvc + _dot(pcb, dom, BTN)
                if nb > 1:
                    kp, vp = kd[0:NB], vd[0:NB]
                    pp = jnp.exp(_dot(qm, kp, BNT) * 0.125 + jnp.where(blk % nb == 0, NEG, bias_prev) - lse_e)
                    dsp = (pp * (_dot(dom, vp, BNT) - e_e)).astype(BF16)
                    ppb = pp.astype(BF16)
                    dqe = dqe + _dot(dsp, kp, BNN)
                    dkp = _dot(dsp, qm, BTN) if e == 0 else dkp + _dot(dsp, qm, BTN)
                    dvp = _dot(ppb, dom, BTN) if e == 0 else dvp + _dot(ppb, dom, BTN)
                dq4 = dqe if e == 0 else jnp.where(first, dq4, dqe)

            dkd[1:NB + 1] = dkc
            dvd[1:NB + 1] = dvc
            if nb > 1:
                dkd[1:NB] += dkp[1:NB]
                dvd[1:NB] += dvp[1:NB]
            L = nb * BLOCK
            for r in range(dilation):
                rows = pl.ds(r, L, stride=dilation) if dilation > 1 else slice(None)
                dq_r = dq4[r * nb:(r + 1) * nb].reshape(L, 128) * 0.125
                dk_r = dkd[1 + r * nb:1 + (r + 1) * nb].reshape(L, 128) * 0.125
                dv_r = dvd[1 + r * nb:1 + (r + 1) * nb].reshape(L, 128)
                if idx == 0:
                    dqa[rows, :], dka[rows, :], dva[rows, :] = dq_r, dk_r, dv_r
                else:
                    dqa[rows, :] += dq_r
                    dka[rows, :] += dk_r
                    dva[rows, :] += dv_r

        dq_ref[...] = dqa[...].astype(BF16)
        dk_ref[...] = dka[...].astype(BF16)
        dv_ref[...] = dva[...].astype(BF16)

    spec = lambda off: pl.BlockSpec((S, 128), lambda b, p: (b, 4 * off + p))
    ospec = pl.BlockSpec((S, 128), lambda b, p: (b, p))
    blocks = lambda n, dt: pltpu.VMEM((n, BLOCK, 128), dt)
    return pl.pallas_call(
        body, grid=(B, 4), in_specs=[_SMEM_SPEC, spec(0), spec(1), spec(2), ospec, ospec, ospec],
        out_specs=[ospec] * 3, out_shape=[SDS((B * S, WIDTH), BF16)] * 3,
        scratch_shapes=[pltpu.VMEM((S, 128), F32)] * 5
        + [blocks(NB, BF16), blocks(NB, BF16), blocks(NB + 1, BF16), blocks(NB + 1, BF16), blocks(NB, F32), blocks(NB, F32),
           blocks(NB + 1, F32), blocks(NB + 1, F32)] + [pltpu.VMEM((S, 128), F32)] * 3,
        compiler_params=_cp(), name=name,
    )(_alibi_slopes(), z, z, z, dy, ya, lse)


FOX_TQ = 256


def _fox_fwd(z, cc, cr, *, B, S, name):
    def body(q_ref, k_ref, v_ref, cc_ref, cr_ref, o_ref):
        (m_first, m_second), first = _pair_masks()
        for qi in range(S // FOX_TQ):
            r0, kend = qi * FOX_TQ, (qi + 1) * FOX_TQ
            q2 = q_ref[r0:kend, :]
            kk, vv = k_ref[0:kend, :], v_ref[0:kend, :]
            row = lax.broadcasted_iota(jnp.int32, (FOX_TQ, kend), 0) + r0
            col = lax.broadcasted_iota(jnp.int32, (FOX_TQ, kend), 1)
            causal = col <= row
            outs = []
            for e, hm in enumerate((m_first, m_second)):
                s = _dot(q2 * hm, kk, NT) * 0.125
                s = s + (cc_ref[r0:kend, :][:, HEAD_DIM * e:HEAD_DIM * e + 1] - cr_ref[e:e + 1, 0:kend])
                s = jnp.where(causal, s, NEG)
                m = jnp.max(s, axis=1, keepdims=True)
                pe = jnp.exp(s - m)
                l = jnp.sum(pe, axis=1, keepdims=True)
                outs.append(_dot(pe.astype(BF16), vv) * (1.0 / l))
            o_ref[r0:kend, :] = jnp.where(first, outs[0], outs[1]).astype(BF16)

    spec = lambda off: pl.BlockSpec((S, 128), lambda b, p: (b, 4 * off + p))
    return pl.pallas_call(
        body, grid=(B, 4),
        in_specs=[spec(3), spec(4), spec(5), pl.BlockSpec((S, 128), lambda b, p: (b, p)),
                  pl.BlockSpec((None, 8, S), lambda b, p: (4 * b + p, 0, 0))],
        out_specs=pl.BlockSpec((S, 128), lambda b, p: (b, p)), out_shape=SDS((B * S, WIDTH), BF16),
        compiler_params=_cp(), name=name,
    )(z, z, z, cc, cr)


def _fox_bwd(z, dy, cc, cr, *, B, S, name):
    def body(q_ref, k_ref, v_ref, do_ref, cc_ref, cr_ref, dq_ref, dk_ref, dv_ref, dc_ref, dk_s, dv_s, dc_s):
        (m_first, m_second), first = _pair_masks()
        dk_s[...] = jnp.zeros_like(dk_s)
        dv_s[...] = jnp.zeros_like(dv_s)
        dc_s[...] = jnp.zeros_like(dc_s)
        for qi in range(S // FOX_TQ):
            r0, kend = qi * FOX_TQ, (qi + 1) * FOX_TQ
            q2, do2 = q_ref[r0:kend, :], do_ref[r0:kend, :]
            kk, vv = k_ref[0:kend, :], v_ref[0:kend, :]
            krow = lax.broadcasted_iota(jnp.int32, (kend, FOX_TQ), 0)
            qcol = lax.broadcasted_iota(jnp.int32, (kend, FOX_TQ), 1) + r0
            causal = krow <= qcol
            dq_t = jnp.zeros((FOX_TQ, 128), F32)
            for e, hm in enumerate((m_first, m_second)):
                sel = first if e == 0 else ~first
                km = kk * hm
                st = _dot(km, q2, NT) * 0.125
                st = st + (cr_ref[e:e + 1, r0:kend] - cc_ref[0:kend, :][:, HEAD_DIM * e:HEAD_DIM * e + 1])
                st = jnp.where(causal, st, NEG)
                pt = jnp.exp(st - jnp.max(st, axis=0, keepdims=True))
                pt = pt * (1.0 / jnp.sum(pt, axis=0, keepdims=True))
                dpt = _dot(vv * hm, do2, NT)
                dst = pt * (dpt - jnp.sum(pt * dpt, axis=0, keepdims=True))
                dsb = dst.astype(BF16)
                dv_s[0:kend, :] += _dot(pt.astype(BF16), do2 * hm)
                dk_s[0:kend, :] += _dot(dsb, q2 * hm) * 0.125
                dq_t = dq_t + _dot(dsb, km, TN)
                dc_s[0:kend, :] += jnp.where(sel, -jnp.sum(dst, axis=1, keepdims=True), 0.0)
            dq_ref[r0:kend, :] = (dq_t * 0.125).astype(BF16)
        dk_ref[...] = dk_s[...].astype(BF16)
        dv_ref[...] = dv_s[...].astype(BF16)
        dc_ref[...] = dc_s[...]

    spec = lambda off: pl.BlockSpec((S, 128), lambda b, p: (b, 4 * off + p))
    pspec = pl.BlockSpec((S, 128), lambda b, p: (b, p))
    return pl.pallas_call(
        body, grid=(B, 4),
        in_specs=[spec(3), spec(4), spec(5), pl.BlockSpec((S, 128), lambda b, p: (b, 4 + p)), pspec,
                  pl.BlockSpec((None, 8, S), lambda b, p: (4 * b + p, 0, 0))],
        out_specs=[pspec] * 4,
        out_shape=[SDS((B * S, WIDTH), BF16)] * 3 + [SDS((B * S, WIDTH), F32)],
        scratch_shapes=[pltpu.VMEM((S, 128), F32)] * 3, compiler_params=_cp(), name=name,
    )(z, z, z, dy, cc, cr)


def _xattn_fwd(q, kv, *, B, S, M, tq, name):
    D = D_MODEL

    def body(q_ref, kv_ref, o_ref):
        for h in range(N_XH):
            cs = slice(XHD * h, XHD * (h + 1))
            s = _dot(q_ref[:, cs], kv_ref[:, cs], NT) * (1.0 / 16.0)
            pe = jnp.exp(s - jnp.max(s, axis=1, keepdims=True))
            l = jnp.sum(pe, axis=1, keepdims=True)
            o_ref[:, cs] = (_dot(pe.astype(BF16), kv_ref[:, D + XHD * h:D + XHD * (h + 1)]) * (1.0 / l)).astype(BF16)

    nq = S // tq
    return pl.pallas_call(
        body, grid=(B, nq),
        in_specs=[pl.BlockSpec((tq, D), lambda b, t: (b * nq + t, 0)), pl.BlockSpec((M, 2 * D), lambda b, t: (b, 0))],
        out_specs=pl.BlockSpec((tq, D), lambda b, t: (b * nq + t, 0)), out_shape=SDS((B * S, D), BF16),
        compiler_params=_cp(), name=name,
    )(q, kv)


def _xattn_bwd(q, kv, do, *, B, S, M, tq, name):
    D = D_MODEL

    def body(q_ref, kv_ref, do_ref, dq_ref, dkv_ref):
        t = pl.program_id(1)

        @pl.when(t == 0)
        def _():
            dkv_ref[...] = jnp.zeros_like(dkv_ref)

        for h in range(N_XH):
            cs = slice(XHD * h, XHD * (h + 1))
            vs = slice(D + XHD * h, D + XHD * (h + 1))
            qh, kh, vh, doh = q_ref[:, cs], kv_ref[:, cs], kv_ref[:, vs], do_ref[:, cs]
            s = _dot(qh, kh, NT) * (1.0 / 16.0)
            pe = jnp.exp(s - jnp.max(s, axis=1, keepdims=True))
            pe = pe * (1.0 / jnp.sum(pe, axis=1, keepdims=True))
            dp = _dot(doh, vh, NT)
            ds = (pe * (dp - jnp.sum(pe * dp, axis=1, keepdims=True))).astype(BF16)
            dq_ref[:, cs] = (_dot(ds, kh) * (1.0 / 16.0)).astype(BF16)
            dkv_ref[:, cs] += _dot(ds, qh, TN) * (1.0 / 16.0)
            dkv_ref[:, vs] += _dot(pe.astype(BF16), doh, TN)

    nq = S // tq
    qspec = pl.BlockSpec((tq, D), lambda b, t: (b * nq + t, 0))
    kvspec = pl.BlockSpec((M, 2 * D), lambda b, t: (b, 0))
    return pl.pallas_call(
        body, grid=(B, nq), in_specs=[qspec, kvspec, qspec], out_specs=[qspec, kvspec],
        out_shape=[SDS((B * S, D), BF16), SDS((B * M, 2 * D), F32)], compiler_params=_cp(), name=name,
    )(q, kv, do)


def _assemble_dz(parts, dgate, *, tm, name):
    T = dgate.shape[0]

    def body(*refs):
        o_ref = refs[-1]
        for j in range(6):
            o_ref[:, WIDTH * j:WIDTH * (j + 1)] = refs[j][...]
        o_ref[:, QKV_W:IN_PAD] = refs[6][...]

    wspec = pl.BlockSpec((tm, WIDTH), lambda i: (i, 0))
    return pl.pallas_call(
        body, grid=(T // tm,), in_specs=[wspec] * 6 + [pl.BlockSpec((tm, 128), lambda i: (i, 0))],
        out_specs=pl.BlockSpec((tm, IN_PAD), lambda i: (i, 0)), out_shape=SDS((T, IN_PAD), BF16),
        compiler_params=_cp(), name=name,
    )(*parts, dgate)


def _adamw(parts, w, m, v, *, tr, name):
    R, C = w.shape

    def body(p_ref, w_ref, m_ref, v_ref, g_ref, d_ref, nm_ref, nv_ref):
        g = p_ref[0].astype(F32)
        for d in range(1, N_DEV):
            g = g + p_ref[d].astype(F32)
        m2 = ADAM_B1 * m_ref[...] + (1.0 - ADAM_B1) * g
        v2 = ADAM_B2 * v_ref[...] + (1.0 - ADAM_B2) * (g * g)
        m_hat = m2 / (1.0 - ADAM_B1 ** ADAM_STEP)
        v_hat = v2 / (1.0 - ADAM_B2 ** ADAM_STEP)
        g_ref[...] = g
        d_ref[...] = -ADAM_LR * (m_hat / (jnp.sqrt(v_hat) + ADAM_EPS) + ADAM_WD * w_ref[...])
        nm_ref[...] = m2
        nv_ref[...] = v2

    spec = pl.BlockSpec((tr, C), lambda i: (i, 0))
    return pl.pallas_call(
        body, grid=(R // tr,), in_specs=[pl.BlockSpec((N_DEV, tr, C), lambda i: (0, i, 0)), spec, spec, spec],
        out_specs=[spec] * 4, out_shape=[SDS((R, C), F32)] * 4, compiler_params=_cp(), name=name,
    )(parts, w, m, v)


def _peer(k, x, y, c):
    return (1 - x if k & 4 else x, 1 - y if k & 2 else y, 1 - c if k & 1 else c)


_HBM_SPEC = pl.BlockSpec(memory_space=pltpu.HBM)
_SEM_SPEC = pl.BlockSpec(memory_space=pltpu.SEMAPHORE)
_SPLIT_EFFECT = pltpu.SideEffectType.DATAFLOW_SIDE_EFFECTING


def _split_copies(srcs, lands, send_sems, recv_sems, modes):
    x, y, c = (lax.axis_index(a) for a in AXES)
    me = 4 * x + 2 * y + c
    copies = []
    for i, md in enumerate(modes):
        for k in range(1, N_DEV):
            px, py, pc = _peer(k, x, y, c)
            src = srcs[i] if md == "gather" else srcs[i].at[4 * px + 2 * py + pc]
            j = i * (N_DEV - 1) + k - 1
            copies.append(pltpu.make_async_remote_copy(
                src_ref=src, dst_ref=lands[i].at[me], send_sem=send_sems.at[j], recv_sem=recv_sems.at[j],
                device_id=(px, py, pc), device_id_type=pl.DeviceIdType.MESH))
    return copies


def _exchange_start(arrays, modes, *, name):
    n = len(arrays)
    hbm = lambda a: pltpu.with_memory_space_constraint(a, pltpu.HBM)
    srcs = [hbm(a) for a in arrays]
    lands = [hbm(jnp.broadcast_to(a[None], (N_DEV,) + a.shape)) if md == "gather" else hbm(a) for a, md in zip(arrays, modes)]

    def body(*refs):
        for cp in _split_copies(refs[:n], refs[n:2 * n], refs[2 * n], refs[2 * n + 1], modes):
            cp.start()
        token = refs[-1]
        token[...] = jnp.zeros_like(token)

    sems = pltpu.SemaphoreType.DMA((n * (N_DEV - 1),))
    outs = pl.pallas_call(
        body, name=name, in_specs=[_HBM_SPEC] * (2 * n),
        out_shape=(sems, sems, *[pltpu.HBM(a.shape, a.dtype) for a in srcs + lands], SDS((8, 128), F32)),
        out_specs=(_SEM_SPEC, _SEM_SPEC, *[_HBM_SPEC] * (2 * n), pl.BlockSpec(memory_space=pltpu.VMEM)),
        input_output_aliases={i: 2 + i for i in range(2 * n)},
        compiler_params=pltpu.CompilerParams(has_side_effects=_SPLIT_EFFECT),
    )(*srcs, *lands)
    return (outs[0], outs[1], outs[2:2 + n], outs[2 + n:2 + 2 * n], modes), outs[-1]


def _exchange_wait(handle, after, *, name):
    send_sems, recv_sems, srcs, lands, modes = handle
    n = len(srcs)

    def body(*refs):
        for cp in _split_copies(refs[:n], refs[n:2 * n], refs[2 * n], refs[2 * n + 1], modes):
            cp.wait_send()
            cp.wait_recv()

    outs = pl.pallas_call(
        body, name=name, in_specs=[_HBM_SPEC] * (2 * n) + [_SEM_SPEC, _SEM_SPEC, pl.BlockSpec(memory_space=pl.ANY)],
        out_shape=tuple(pltpu.HBM(a.shape, a.dtype) for a in list(srcs) + list(lands)), out_specs=tuple([_HBM_SPEC] * (2 * n)),
        input_output_aliases={i: i for i in range(2 * n)},
        compiler_params=pltpu.CompilerParams(has_side_effects=_SPLIT_EFFECT),
    )(*srcs, *lands, send_sems, recv_sems, after)
    return list(outs[n:])


def _exchange(arrays, modes, *, name):
    n = len(arrays)
    out_shape = [SDS((N_DEV,) + a.shape if md == "gather" else a.shape, a.dtype) for a, md in zip(arrays, modes)]

    def body(*refs):
        ins, outs = refs[:n], refs[n:2 * n]
        send_sems, recv_sems, local_sems = refs[2 * n:]
        x, y, c = (lax.axis_index(a) for a in AXES)
        me = 4 * x + 2 * y + c
        copies = []
        for i, md in enumerate(modes):
            src = ins[i] if md == "gather" else ins[i].at[me]
            cp = pltpu.make_async_copy(src, outs[i].at[me], local_sems.at[i])
            cp.start()
            copies.append(cp)
            for k in range(1, N_DEV):
                px, py, pc = _peer(k, x, y, c)
                src = ins[i] if md == "gather" else ins[i].at[4 * px + 2 * py + pc]
                cp = pltpu.make_async_remote_copy(
                    src_ref=src, dst_ref=outs[i].at[me], send_sem=send_sems.at[i, k - 1], recv_sem=recv_sems.at[i, k - 1],
                    device_id=(px, py, pc), device_id_type=pl.DeviceIdType.MESH)
                cp.start()
                copies.append(cp)
        for cp in copies:
            cp.wait()

    anyspec = pl.BlockSpec(memory_space=pl.ANY)
    return pl.pallas_call(
        body, in_specs=[anyspec] * n, out_specs=[anyspec] * n, out_shape=out_shape,
        scratch_shapes=[pltpu.SemaphoreType.DMA((n, N_DEV - 1)), pltpu.SemaphoreType.DMA((n, N_DEV - 1)),
                        pltpu.SemaphoreType.DMA((n,))],
        name=name,
    )(*arrays)


def _local_step(x, mem, g_mix, b_forget, g_xattn, g_mem, g_mlp, g_final, target, get_w_in, get_rest, send):
    B, S, D = x.shape
    M = mem.shape[1]
    T = B * S
    x0 = x.reshape(T, D)
    mem2 = mem.reshape(B * M, D)
    tgt = target.reshape(T, D)
    b_pad = jnp.pad(b_forget, (0, 120)).reshape(1, 128)
    after = lambda a, tok: a if tok is None else a + tok[0, 0]

    w_in_pad = get_w_in()
    h1, z = _rms_matmul(x0, g_mix, w_in_pad[:, :QKV_W], tm=ROWS, tn=QKV_W, out_dtype=BF16, name="f_in")
    gate = _matmul_nn(h1, w_in_pad[:, QKV_W:], tm=1024, tn=128, tk=D, out_dtype=F32, name="f_gate")
    c, cc = _gate_fwd(gate, b_pad, B=B, S=S, name="f_gatecum")
    cr = jnp.pad(c[:, :8].reshape(B, S, 4, 2).transpose(0, 2, 3, 1), ((0, 0), (0, 0), (0, 6), (0, 0))).reshape(B * 4, 8, S)
    ya, lse = _dil_attn_fwd(z, B=B, S=S, name="f_dil")
    yf = _fox_fwd(z, cc, cr, B=B, S=S, name="f_fox")
    ymix = jnp.concatenate([ya, yf], axis=1)
    w = get_rest(ymix)
    x1 = _matmul_nn(ymix, w["w_out"], res=x0, tm=ROWS, tn=D, tk=D, out_dtype=F32, name="f_out")
    h2, q = _rms_matmul(x1, g_xattn, w["w_xq"], tm=ROWS, tn=D, out_dtype=BF16, name="f_xq")
    mn, kv = _rms_matmul(mem2, g_mem, w["w_kv"], tm=B * M, tn=D, out_dtype=BF16, name="f_xkv")
    xo = _xattn_fwd(q, kv, B=B, S=S, M=M, tq=512, name="f_xattn")
    x2 = _matmul_nn(xo, w["w_xo"], res=x1, tm=ROWS, tn=D, tk=D, out_dtype=F32, name="f_xo")
    h3, act = _rms_matmul(x2, g_mlp, w["w_up"], tm=ROWS, tn=D_FF, out_dtype=BF16, relu=True, name="f_up")
    x3 = _matmul_nn(act, w["w_down"], res=x2, square=True, tm=ROWS, tn=D, tk=D_FF, out_dtype=F32, name="f_down")
    dx3, dg_final, loss = _loss_head(x3, g_final, tgt, tm=512, name="f_loss")

    du = _matmul_nt(dx3, w["w_down"], mul2a=act, tm=ROWS, tn=D_FF, name="b_dact")
    dw_down = _matmul_tn(act, dx3, square=True, bk=1024, bn=D, tt=ACC_ROWS, out_dtype=BF16, name="b_wdown")
    dw_up = _matmul_tn(h3, du, bk=D, bn=1024, tt=ACC_ROWS, out_dtype=BF16, name="b_wup")
    tok = send(dict(w_down=dw_down, w_up=dw_up))
    dx2, dg_mlp = _matmul_nt_rms(du, w["w_up"], x2, after(g_mlp, tok), dx3, tm=ROWS, tk=D_FF, name="b_dh3")
    dxo = _matmul_nt(dx2, w["w_xo"], tm=1024, tn=D, name="b_dxo")
    dw_xo = _matmul_tn(xo, dx2, bk=D, bn=D, tt=ACC_ROWS, out_dtype=BF16, name="b_wxo")
    dq, dkv = _xattn_bwd(q, kv, dxo, B=B, S=S, M=M, tq=512, name="b_xattn")
    dw_xq = _matmul_tn(h2, dq, bk=D, bn=D, tt=ACC_ROWS, out_dtype=BF16, name="b_wxq")
    dx1, dg_xattn = _matmul_nt_rms(dq, w["w_xq"], x1, g_xattn, dx2, tm=ROWS, tk=D, name="b_dh2")
    dw_kv = _matmul_tn(mn, dkv, bk=D, bn=D, tt=B * M, out_dtype=BF16, name="b_wkv")
    _, dg_mem = _matmul_nt_rms(dkv, w["w_kv"], mem2, g_mem, None, tm=min(ROWS, B * M), tk=2 * D, name="b_dmem")
    dy = _matmul_nt(dx1, w["w_out"], tm=1024, tn=D, name="b_dy")
    dw_out = _matmul_tn(ymix, dx1, bk=D, bn=D, tt=ACC_ROWS, out_dtype=BF16, name="b_wout")
    tok = send(dict(w_xo=dw_xo, w_xq=dw_xq, w_xk=dw_kv[:, :D], w_xv=dw_kv[:, D:], w_out=dw_out))
    dqf, dkf, dvf, dcc = _fox_bwd(z, dy, cc, after(cr, tok), B=B, S=S, name="b_fox")
    dgate, db = _gate_bwd(dcc, gate, b_pad, B=B, S=S, name="b_gate")
    dqa, dka, dva = _dil_attn_bwd(z, dy, ya, lse, B=B, S=S, name="b_dil")
    dz = _assemble_dz([dqa, dka, dva, dqf, dkf, dvf], dgate, tm=512, name="b_dz")
    dw_in = _matmul_tn(h1, dz, bk=D, bn=640, tt=ACC_ROWS, out_dtype=BF16, name="b_win")
    tok = send(dict(w_in=dw_in))
    gx, dg_mix = _matmul_nt_rms(dz, w_in_pad, x0, after(g_mix, tok), dx1, tm=ROWS, tk=IN_PAD, name="b_dh1")

    small = dict(g_mix=dg_mix, b_forget=db, g_xattn=dg_xattn, g_mem=dg_mem, g_mlp=dg_mlp, g_final=dg_final)
    return gx.reshape(B, S, D), small, loss


SMALL_ROWS = ("g_mix", "b_forget", "g_xattn", "g_mem", "g_mlp", "g_final")
COL_SHARDED = ("w_in", "w_up")


def _pack_rows(rows):
    D = D_MODEL
    rows = [jnp.pad(r.reshape(-1), (0, D - r.size)) for r in rows]
    rows += [jnp.zeros((D,), F32)] * (8 - len(rows))
    return jnp.stack(rows)


def _full(name, g):
    if name in COL_SHARDED:
        return g.transpose(1, 0, 2).reshape(g.shape[1], -1)
    return g.reshape(-1, g.shape[2])


def _blocks(name, g, shard_shape):
    if name in COL_SHARDED:
        n = shard_shape[1]
        return g[:, :n * N_DEV].reshape(g.shape[0], N_DEV, n).transpose(1, 0, 2)
    return g.reshape((N_DEV,) + shard_shape)


def kernel(x, mem, g_mix, w_in, b_forget, w_out, g_xattn, g_mem, w_xq, w_xk, w_xv, w_xo, g_mlp, w_up, w_down, g_final, loss_target, m_g_mix, m_w_in, m_b_forget, m_w_out, m_g_xattn, m_g_mem, m_w_xq, m_w_xk, m_w_xv, m_w_xo, m_g_mlp, m_w_up, m_w_down, m_g_final, v_g_mix, v_w_in, v_b_forget, v_w_out, v_g_xattn, v_g_mem, v_w_xq, v_w_xk, v_w_xv, v_w_xo, v_g_mlp, v_w_up, v_w_down, v_g_final):
    W = dict(w_in=w_in, w_out=w_out, w_xq=w_xq, w_xk=w_xk, w_xv=w_xv, w_xo=w_xo, w_up=w_up, w_down=w_down)
    Mo = dict(w_in=m_w_in, w_out=m_w_out, w_xq=m_w_xq, w_xk=m_w_xk, w_xv=m_w_xv, w_xo=m_w_xo, w_up=m_w_up, w_down=m_w_down)
    Vo = dict(w_in=v_w_in, w_out=v_w_out, w_xq=v_w_xq, w_xk=v_w_xk, w_xv=v_w_xv, w_xo=v_w_xo, w_up=v_w_up, w_down=v_w_down)
    later = [n for n in W if n != "w_in"]

    first_handle, first_token = _exchange_start([w_in.astype(BF16)], ["gather"], name="gather_in_start")
    rest_handle, rest_token = _exchange_start([W[n].astype(BF16) + first_token[0, 0].astype(BF16) for n in later],
                                              ["gather"] * len(later), name="gather_rest_start")

    def get_w_in():
        (g,) = _exchange_wait(first_handle, rest_token, name="gather_in_wait")
        return jnp.pad(_full("w_in", g), ((0, 0), (0, IN_PAD - IN_W)))

    def get_rest(after):
        full = {n: _full(n, g) for n, g in zip(later, _exchange_wait(rest_handle, after, name="gather_rest_wait"))}
        full["w_kv"] = jnp.concatenate([full.pop("w_xk"), full.pop("w_xv")], axis=1)
        return full

    sent = []

    def send(grads):
        names = list(grads)
        handle, token = _exchange_start([_blocks(n, grads[n], W[n].shape) for n in names], ["scatter"] * len(names),
                                        name=f"scatter{len(sent)}_start")
        sent.append((names, handle))
        return token

    gx, small, loss = _local_step(x, mem, g_mix, b_forget, g_xattn, g_mem, g_mlp, g_final, loss_target, get_w_in, get_rest, send)

    received = {}
    for i, (names, handle) in enumerate(sent):
        received.update(zip(names, _exchange_wait(handle, gx, name=f"scatter{i}_wait")))
    packed = _pack_rows([small[n] for n in SMALL_ROWS] + [loss[0, :1]])
    (packed_all,) = _exchange([packed], ["gather"], name="gather_small")

    res = {n: _adamw(received[n], W[n], Mo[n], Vo[n], tr=128, name=f"adamw_{n}") for n in W}
    small_w = dict(g_mix=g_mix, b_forget=b_forget, g_xattn=g_xattn, g_mem=g_mem, g_mlp=g_mlp, g_final=g_final)
    small_m = dict(g_mix=m_g_mix, b_forget=m_b_forget, g_xattn=m_g_xattn, g_mem=m_g_mem, g_mlp=m_g_mlp, g_final=m_g_final)
    small_v = dict(g_mix=v_g_mix, b_forget=v_b_forget, g_xattn=v_g_xattn, g_mem=v_g_mem, g_mlp=v_g_mlp, g_final=v_g_final)
    sres = _adamw(packed_all, _pack_rows([small_w[n] for n in SMALL_ROWS]), _pack_rows([small_m[n] for n in SMALL_ROWS]),
                  _pack_rows([small_v[n] for n in SMALL_ROWS]), tr=8, name="adamw_small")
    for i, n in enumerate(SMALL_ROWS):
        res[n] = [r[i, :small_w[n].size] for r in sres]
    loss_total = sres[0][6, 0]

    order = ["g_mix", "w_in", "b_forget", "w_out", "g_xattn", "g_mem", "w_xq", "w_xk", "w_xv", "w_xo", "g_mlp", "w_up", "w_down", "g_final"]
    return (loss_total, gx, *[res[n][0] for n in order], *[res[n][1] for n in order],
            *[res[n][2] for n in order], *[res[n][3] for n in order])
```

```python
import jax
import jax.numpy as jnp
from jax import lax
from jax.experimental import pallas as pl
from jax.experimental.pallas import tpu as pltpu

F32, BF16 = jnp.float32, jnp.bfloat16
SDS = jax.ShapeDtypeStruct

D_MODEL = 1024
HEAD_DIM = 64
WIDTH = 512
QKV_W = 6 * WIDTH
IN_W = QKV_W + 8
IN_PAD = QKV_W + 128
BLOCK = 128
DIL_CONFIGS = ((128, 1), (512, 4), (2048, 16))
N_XH, XHD = 4, 256
D_FF = 4096
EPS = 1e-6
NEG = -1e30
N_DEV = 8
AXES = ("x", "y", "c")

ADAM_LR, ADAM_B1, ADAM_B2, ADAM_EPS, ADAM_WD, ADAM_STEP = 0.001, 0.9, 0.999, 1e-08, 0.01, 10

VMEM_CAP_V7X = 64 * 1024 * 1024
VMEM_LIMIT = VMEM_CAP_V7X * 7 // 8

ROWS = 512
ACC_ROWS = 2048

NT = (((1,), (1,)), ((), ()))
TN = (((0,), (0,)), ((), ()))


def _cp(**kw):
    return pltpu.CompilerParams(vmem_limit_bytes=VMEM_LIMIT, **kw)


def _dot(a, b, dims=None):
    if dims is None:
        return jnp.dot(a, b, preferred_element_type=F32)
    return lax.dot_general(a, b, dims, preferred_element_type=F32)


def _rstd(xv):
    return lax.rsqrt(jnp.mean(xv * xv, axis=-1, keepdims=True) + EPS)


def _rms_bwd(dh, xv, g):
    r = _rstd(xv)
    xhat = xv * r
    dxhat = dh * g
    dx = r * (dxhat - xhat * jnp.mean(dxhat * xhat, axis=-1, keepdims=True))
    return dx, jnp.sum(dh * xhat, axis=0, keepdims=True)


def _rms_matmul(x, g, w, *, tm, tn, out_dtype, relu=False, name):
    T, D = x.shape
    N = w.shape[1]

    def body(x_ref, g_ref, w_ref, h_ref, o_ref, h_s):
        @pl.when(pl.program_id(1) == 0)
        def _():
            xv = x_ref[...]
            h = (xv * _rstd(xv) * g_ref[...]).astype(BF16)
            h_s[...] = h
            h_ref[...] = h

        acc = _dot(h_s[...], w_ref[...])
        if relu:
            acc = jnp.maximum(acc, 0.0)
        o_ref[...] = acc.astype(out_dtype)

    return pl.pallas_call(
        body, grid=(T // tm, N // tn),
        in_specs=[pl.BlockSpec((tm, D), lambda i, j: (i, 0)), pl.BlockSpec((1, D), lambda i, j: (0, 0)),
                  pl.BlockSpec((D, tn), lambda i, j: (0, j))],
        out_specs=[pl.BlockSpec((tm, D), lambda i, j: (i, 0)), pl.BlockSpec((tm, tn), lambda i, j: (i, j))],
        out_shape=[SDS((T, D), BF16), SDS((T, N), out_dtype)],
        scratch_shapes=[pltpu.VMEM((tm, D), BF16)], compiler_params=_cp(), name=name,
    )(x, g.reshape(1, D), w)


def _matmul_nn(a, w, *, res=None, square=False, tm, tn, tk, out_dtype, name):
    T, K = a.shape
    N = w.shape[1]
    nk = K // tk

    def body(*refs):
        a_ref, w_ref = refs[0], refs[1]
        res_ref = refs[2] if res is not None else None
        o_ref = refs[3] if res is not None else refs[2]
        k = pl.program_id(2)
        av = a_ref[...]
        if square:
            af = av.astype(F32)
            av = (af * af).astype(BF16)
        part = _dot(av, w_ref[...])

        def finish(r):
            if res_ref is not None:
                r = res_ref[...] + r
            o_ref[...] = r.astype(out_dtype)

        if nk == 1:
            finish(part)
        else:
            acc = refs[-1]

            @pl.when(k == 0)
            def _():
                acc[...] = part

            @pl.when(k > 0)
            def _():
                acc[...] += part

            @pl.when(k == nk - 1)
            def _():
                finish(acc[...])

    in_specs = [pl.BlockSpec((tm, tk), lambda i, j, k: (i, k)), pl.BlockSpec((tk, tn), lambda i, j, k: (k, j))]
    args = [a, w]
    if res is not None:
        in_specs.append(pl.BlockSpec((tm, tn), lambda i, j, k: (i, j)))
        args.append(res)
    return pl.pallas_call(
        body, grid=(T // tm, N // tn, nk), in_specs=in_specs,
        out_specs=pl.BlockSpec((tm, tn), lambda i, j, k: (i, j)), out_shape=SDS((T, N), out_dtype),
        scratch_shapes=[pltpu.VMEM((tm, tn), F32)] if nk > 1 else [], compiler_params=_cp(), name=name,
    )(*args)


def _matmul_nt(g, w, *, mul2a=None, tm, tn, name):
    T, K = g.shape
    N = w.shape[0]

    def body(*refs):
        g_ref, w_ref = refs[0], refs[1]
        o_ref = refs[-1]
        acc = _dot(g_ref[...].astype(BF16), w_ref[...], NT)
        if mul2a is not None:
            acc = acc * (2.0 * refs[2][...].astype(F32))
        o_ref[...] = acc.astype(BF16)

    in_specs = [pl.BlockSpec((tm, K), lambda i, j: (i, 0)), pl.BlockSpec((tn, K), lambda i, j: (j, 0))]
    args = [g, w]
    if mul2a is not None:
        in_specs.append(pl.BlockSpec((tm, tn), lambda i, j: (i, j)))
        args.append(mul2a)
    return pl.pallas_call(
        body, grid=(T // tm, N // tn), in_specs=in_specs,
        out_specs=pl.BlockSpec((tm, tn), lambda i, j: (i, j)), out_shape=SDS((T, N), BF16),
        compiler_params=_cp(), name=name,
    )(*args)


def _matmul_nt_rms(g, w, x, gain, dres, *, tm, tk, name):
    T, K = g.shape
    D = w.shape[0]
    nk = K // tk
    nt = T // tm

    def body(*refs):
        g_ref, w_ref, x_ref, gain_ref = refs[:4]
        dres_ref = refs[4] if dres is not None else None
        n_in = 5 if dres is not None else 4
        dx_ref, dg_ref = refs[n_in], refs[n_in + 1]
        i, k = pl.program_id(0), pl.program_id(1)
        part = _dot(g_ref[...].astype(BF16), w_ref[...], NT)

        def finish(dh):
            dx, dg = _rms_bwd(dh, x_ref[...], gain_ref[...])
            if dres_ref is not None:
                dx = dres_ref[...] + dx
            dx_ref[...] = dx

            @pl.when(i == 0)
            def _():
                dg_ref[...] = dg

            @pl.when(i > 0)
            def _():
                dg_ref[...] += dg

        if nk == 1:
            finish(part)
        else:
            acc = refs[-1]

            @pl.when(k == 0)
            def _():
                acc[...] = part

            @pl.when(k > 0)
            def _():
                acc[...] += part

            @pl.when(k == nk - 1)
            def _():
                finish(acc[...])

    in_specs = [pl.BlockSpec((tm, tk), lambda i, k: (i, k)), pl.BlockSpec((D, tk), lambda i, k: (0, k)),
                pl.BlockSpec((tm, D), lambda i, k: (i, 0)), pl.BlockSpec((1, D), lambda i, k: (0, 0))]
    args = [g, w, x, gain.reshape(1, D)]
    if dres is not None:
        in_specs.append(pl.BlockSpec((tm, D), lambda i, k: (i, 0)))
        args.append(dres)
    return pl.pallas_call(
        body, grid=(nt, nk), in_specs=in_specs,
        out_specs=[pl.BlockSpec((tm, D), lambda i, k: (i, 0)), pl.BlockSpec((1, D), lambda i, k: (0, 0))],
        out_shape=[SDS((T, D), F32), SDS((1, D), F32)],
        scratch_shapes=[pltpu.VMEM((tm, D), F32)] if nk > 1 else [], compiler_params=_cp(), name=name,
    )(*args)


def _matmul_tn(a, g, *, square=False, bk, bn, tt, out_dtype, name):
    T, K = a.shape
    N = g.shape[1]
    nt = T // tt

    def body(a_ref, g_ref, o_ref, acc):
        t = pl.program_id(2)
        av = a_ref[...]
        if square:
            af = av.astype(F32)
            av = (af * af).astype(BF16)
        part = _dot(av, g_ref[...].astype(BF16), TN)
        if nt == 1:
            o_ref[...] = part.astype(out_dtype)
        else:
            @pl.when(t == 0)
            def _():
                acc[...] = part

            @pl.when((t > 0) & (t < nt - 1))
            def _():
                acc[...] += part

            @pl.when(t == nt - 1)
            def _():
                o_ref[...] = (acc[...] + part).astype(out_dtype)

    return pl.pallas_call(
        body, grid=(K // bk, N // bn, nt),
        in_specs=[pl.BlockSpec((tt, bk), lambda i, j, t: (t, i)), pl.BlockSpec((tt, bn), lambda i, j, t: (t, j))],
        out_specs=pl.BlockSpec((bk, bn), lambda i, j, t: (i, j)), out_shape=SDS((K, N), out_dtype),
        scratch_shapes=[pltpu.VMEM((bk, bn), F32)], compiler_params=_cp(), name=name,
    )(a, g)


def _loss_head(x3, g_final, target, *, tm, name):
    T, D = x3.shape

    def body(x_ref, g_ref, t_ref, dx_ref, dg_ref, loss_ref):
        i = pl.program_id(0)
        xv, g = x_ref[...], g_ref[...]
        r = _rstd(xv)
        xhat = xv * r
        diff = xhat * g - t_ref[...]
        part = 0.5 * jnp.sum(jnp.mean(diff * diff, axis=-1, keepdims=True), axis=0, keepdims=True)
        dy = diff * (1.0 / D)
        dxhat = dy * g
        dx_ref[...] = r * (dxhat - xhat * jnp.mean(dxhat * xhat, axis=-1, keepdims=True))
        dg = jnp.sum(dy * xhat, axis=0, keepdims=True)
        lp = jnp.broadcast_to(part, loss_ref.shape)

        @pl.when(i == 0)
        def _():
            dg_ref[...] = dg
            loss_ref[...] = lp

        @pl.when(i > 0)
        def _():
            dg_ref[...] += dg
            loss_ref[...] += lp

    return pl.pallas_call(
        body, grid=(T // tm,),
        in_specs=[pl.BlockSpec((tm, D), lambda i: (i, 0)), pl.BlockSpec((1, D), lambda i: (0, 0)),
                  pl.BlockSpec((tm, D), lambda i: (i, 0))],
        out_specs=[pl.BlockSpec((tm, D), lambda i: (i, 0)), pl.BlockSpec((1, D), lambda i: (0, 0)),
                   pl.BlockSpec((8, 128), lambda i: (0, 0))],
        out_shape=[SDS((T, D), F32), SDS((1, D), F32), SDS((8, 128), F32)],
        compiler_params=_cp(), name=name,
    )(x3, g_final.reshape(1, D), target)


def _head_lanes(shape, width):
    return lax.broadcasted_iota(jnp.int32, shape, len(shape) - 1) // width


def _gate_fwd(gate, b_pad, *, B, S, name):
    def body(g_ref, b_ref, cc_ref):
        xv = g_ref[...] + b_ref[...]
        lf = jnp.minimum(xv, 0.0) - jnp.log(1.0 + jnp.exp(-jnp.abs(xv)))
        lane = lax.broadcasted_iota(jnp.int32, lf.shape, 1)
        row = lax.broadcasted_iota(jnp.int32, lf.shape, 0)
        c = jnp.where(lane < 8, lf, 0.0)
        sh = 1
        while sh < S:
            c = c + jnp.where(row >= sh, pltpu.roll(c, sh, 0), 0.0)
            sh *= 2
        grp = _head_lanes((S, WIDTH), HEAD_DIM)
        cc = jnp.zeros((S, WIDTH), F32)
        for h in range(8):
            cc = jnp.where(grp == h, c[:, h:h + 1], cc)
        cc_ref[...] = cc

    return pl.pallas_call(
        body, grid=(B,),
        in_specs=[pl.BlockSpec((S, 128), lambda b: (b, 0)), pl.BlockSpec((1, 128), lambda b: (0, 0))],
        out_specs=pl.BlockSpec((S, WIDTH), lambda b: (b, 0)), out_shape=SDS((B * S, WIDTH), F32),
        compiler_params=_cp(), name=name,
    )(gate, b_pad)


def _gate_bwd(dcc, gate, b_pad, *, B, S, name):
    def body(dcc_ref, g_ref, b_ref, dg_ref, db_ref):
        bi = pl.program_id(0)
        dccv = dcc_ref[...]
        lane = lax.broadcasted_iota(jnp.int32, (S, 128), 1)
        row = lax.broadcasted_iota(jnp.int32, (S, 128), 0)
        dc = jnp.zeros((S, 128), F32)
        for h in range(8):
            dc = jnp.where(lane == h, dccv[:, HEAD_DIM * h:HEAD_DIM * h + 1], dc)
        sh = 1
        while sh < S:
            dc = dc + jnp.where(row < S - sh, pltpu.roll(dc, S - sh, 0), 0.0)
            sh *= 2
        xv = g_ref[...] + b_ref[...]
        dgate = jnp.where(lane < 8, dc / (1.0 + jnp.exp(xv)), 0.0)
        dg_ref[...] = dgate.astype(BF16)
        db = jnp.sum(dgate, axis=0, keepdims=True)

        @pl.when(bi == 0)
        def _():
            db_ref[...] = db

        @pl.when(bi > 0)
        def _():
            db_ref[...] += db

    return pl.pallas_call(
        body, grid=(B,),
        in_specs=[pl.BlockSpec((S, WIDTH), lambda b: (b, 0)), pl.BlockSpec((S, 128), lambda b: (b, 0)),
                  pl.BlockSpec((1, 128), lambda b: (0, 0))],
        out_specs=[pl.BlockSpec((S, 128), lambda b: (b, 0)), pl.BlockSpec((1, 128), lambda b: (0, 0))],
        out_shape=[SDS((B * S, 128), BF16), SDS((1, 128), F32)],
        compiler_params=_cp(), name=name,
    )(dcc, gate, b_pad)


_SMEM_SPEC = pl.BlockSpec(memory_space=pltpu.SMEM)


def _alibi_slopes():
    return 2.0 ** (-(jnp.arange(1, 9, dtype=F32) * (8.0 / 8)))


def _pair_masks():
    lane = lax.broadcasted_iota(jnp.int32, (1, 128), 1)
    first = lane < HEAD_DIM
    return (first.astype(BF16), (~first).astype(BF16)), first


BNT =(((2,), (2,)), ((0,), (0,)))
BNN = (((2,), (1,)), ((0,), (0,)))
BTN = (((1,), (1,)), ((0,), (0,)))


def _band_bias(slope, dilation):
    qi = lax.broadcasted_iota(jnp.int32, (BLOCK, BLOCK), 0)
    kj = lax.broadcasted_iota(jnp.int32, (BLOCK, BLOCK), 1)
    cur = jnp.where(kj <= qi, (-slope * dilation) * (qi - kj).astype(F32), NEG)
    prev = jnp.where(kj >= qi, (-slope * dilation) * (qi + BLOCK - kj).astype(F32), NEG)
    return cur, prev


def _to_residue_major(dst, src_f32, dilation, nb, lead=0):
    L = nb * BLOCK
    for r in range(dilation):
        rows = src_f32[pl.ds(r, L, stride=dilation), :] if dilation > 1 else src_f32[...]
        dst[lead + r * nb:lead + (r + 1) * nb] = rows.reshape(nb, BLOCK, 128).astype(dst.dtype)


def _dil_attn_fwd(z, *, B, S, name):
    NB = S // BLOCK

    def body(slope_ref, q_ref, k_ref, v_ref, y_ref, lse_ref, qf, kf, vf, qd, kd, vd, od, ld, acc_o, acc_l):
        (m_first, m_second), first = _pair_masks()
        p = pl.program_id(1)
        qf[...] = q_ref[...].astype(F32)
        kf[...] = k_ref[...].astype(F32)
        vf[...] = v_ref[...].astype(F32)
        kd[0] = jnp.zeros((BLOCK, 128), BF16)
        vd[0] = jnp.zeros((BLOCK, 128), BF16)
        blk = lax.broadcasted_iota(jnp.int32, (NB, 1, 1), 0)

        for idx, (_, dilation) in enumerate(DIL_CONFIGS):
            nb = NB // dilation
            _to_residue_major(qd, qf, dilation, nb)
            _to_residue_major(kd, kf, dilation, nb, lead=1)
            _to_residue_major(vd, vf, dilation, nb, lead=1)
            q4, kc, vc = qd[...], kd[1:NB + 1], vd[1:NB + 1]
            outs, lses = [], []
            for e, hm in enumerate((m_first, m_second)):
                bias_cur, bias_prev = _band_bias(slope_ref[2 * p + e], dilation)
                qm = q4 * hm
                sc = _dot(qm, kc, BNT) * 0.125 + bias_cur
                m = jnp.max(sc, axis=2, keepdims=True)
                if nb > 1:
                    sp = _dot(qm, kd[0:NB], BNT) * 0.125 + jnp.where(blk % nb == 0, NEG, bias_prev)
                    m = jnp.maximum(m, jnp.max(sp, axis=2, keepdims=True))
                pc = jnp.exp(sc - m)
                l = jnp.sum(pc, axis=2, keepdims=True)
                o = _dot(pc.astype(BF16), vc, BNN)
                if nb > 1:
                    pp = jnp.exp(sp - m)
                    l = l + jnp.sum(pp, axis=2, keepdims=True)
                    o = o + _dot(pp.astype(BF16), vd[0:NB], BNN)
                outs.append(o * (1.0 / l))
                lses.append(m + jnp.log(l))
            od[...] = jnp.where(first, outs[0], outs[1])
            ld[...] = jnp.where(first, lses[0], lses[1])

            L = nb * BLOCK
            for r in range(dilation):
                rows = pl.ds(r, L, stride=dilation) if dilation > 1 else slice(None)
                o_new = od[r * nb:(r + 1) * nb].reshape(L, 128)
                l_new = ld[r * nb:(r + 1) * nb].reshape(L, 128)
                if idx == 0:
                    acc_o[rows, :] = o_new
                    acc_l[rows, :] = l_new
                else:
                    l_old = acc_l[rows, :]
                    m2 = jnp.maximum(l_old, l_new)
                    w_old, w_new = jnp.exp(l_old - m2), jnp.exp(l_new - m2)
                    tot = w_old + w_new
                    acc_o[rows, :] = (w_old * acc_o[rows, :] + w_new * o_new) * (1.0 / tot)
                    acc_l[rows, :] = m2 + jnp.log(tot)

        y_ref[...] = acc_o[...].astype(BF16)
        lse_ref[...] = acc_l[...]

    spec = lambda off: pl.BlockSpec((S, 128), lambda b, p: (b, 4 * off + p))
    ospec = pl.BlockSpec((S, 128), lambda b, p: (b, p))
    blocks = lambda n, dt: pltpu.VMEM((n, BLOCK, 128), dt)
    return pl.pallas_call(
        body, grid=(B, 4), in_specs=[_SMEM_SPEC, spec(0), spec(1), spec(2)], out_specs=[ospec, ospec],
        out_shape=[SDS((B * S, WIDTH), BF16), SDS((B * S, WIDTH), F32)],
        scratch_shapes=[pltpu.VMEM((S, 128), F32)] * 3 + [blocks(NB, BF16), blocks(NB + 1, BF16), blocks(NB + 1, BF16),
                                                         blocks(NB, F32), blocks(NB, F32)] + [pltpu.VMEM((S, 128), F32)] * 2,
        compiler_params=_cp(), name=name,
    )(_alibi_slopes(), z, z, z)


def _dil_attn_bwd(z, dy, ya, lse, *, B, S, name):
    NB = S // BLOCK

    def body(slope_ref, q_ref, k_ref, v_ref, do_ref, o_ref, lse_ref, dq_ref, dk_ref, dv_ref,
             qf, kf, vf, dof, ef, qd, dod, kd, vd, lsd, ed, dkd, dvd, dqa, dka, dva):
        (m_first, m_second), first = _pair_masks()
        p = pl.program_id(1)
        qf[...] = q_ref[...].astype(F32)
        kf[...] = k_ref[...].astype(F32)
        vf[...] = v_ref[...].astype(F32)
        dov = do_ref[...].astype(F32)
        dof[...] = dov
        prod = dov * o_ref[...].astype(F32)
        ef[...] = jnp.where(first, jnp.sum(jnp.where(first, prod, 0.0), axis=1, keepdims=True),
                            jnp.sum(jnp.where(first, 0.0, prod), axis=1, keepdims=True))
        kd[0] = jnp.zeros((BLOCK, 128), BF16)
        vd[0] = jnp.zeros((BLOCK, 128), BF16)
        blk = lax.broadcasted_iota(jnp.int32, (NB, 1, 1), 0)

        for idx, (_, dilation) in enumerate(DIL_CONFIGS):
            nb = NB // dilation
            _to_residue_major(qd, qf, dilation, nb)
            _to_residue_major(dod, dof, dilation, nb)
            _to_residue_major(kd, kf, dilation, nb, lead=1)
            _to_residue_major(vd, vf, dilation, nb, lead=1)
            _to_residue_major(lsd, lse_ref, dilation, nb)
            _to_residue_major(ed, ef, dilation, nb)
            q4, do4, kc, vc = qd[...], dod[...], kd[1:NB + 1], vd[1:NB + 1]
            dq4 = None
            dkc = dvc = dkp = dvp = None
            for e, hm in enumerate((m_first, m_second)):
                lane0 = slice(HEAD_DIM * e, HEAD_DIM * e + 1)
                bias_cur, bias_prev = _band_bias(slope_ref[2 * p + e], dilation)
                qm, dom = q4 * hm, do4 * hm
                lse_e, e_e = lsd[...][:, :, lane0], ed[...][:, :, lane0]
                pc = jnp.exp(_dot(qm, kc, BNT) * 0.125 + bias_cur - lse_e)
                dsc = (pc * (_dot(dom, vc, BNT) - e_e)).astype(BF16)
                pcb = pc.astype(BF16)
                dqe = _dot(dsc, kc, BNN)
                dkc = _dot(dsc, qm, BTN) if e == 0 else dkc + _dot(dsc, qm, BTN)
                dvc = _dot(pcb, dom, BTN) if e == 0 else dvc + _dot(pcb, dom, BTN)
                if nb > 1:
                    kp, vp = kd[0:NB], vd[0:NB]
                    pp = jnp.exp(_dot(qm, kp, BNT) * 0.125 + jnp.where(blk % nb == 0, NEG, bias_prev) - lse_e)
                    dsp = (pp * (_dot(dom, vp, BNT) - e_e)).astype(BF16)
                    ppb = pp.astype(BF16)
                    dqe = dqe + _dot(dsp, kp, BNN)
                    dkp = _dot(dsp, qm, BTN) if e == 0 else dkp + _dot(dsp, qm, BTN)
                    dvp = _dot(ppb, dom, BTN) if e == 0 else dvp + _dot(ppb, dom, BTN)
                dq4 = dqe if e == 0 else jnp.where(first, dq4, dqe)

            dkd[1:NB + 1] = dkc
            dvd[1:NB + 1] = dvc
            if nb > 1:
                dkd[1:NB] += dkp[1:NB]
                dvd[1:NB] += dvp[1:NB]
            L = nb * BLOCK
            for r in range(dilation):
                rows = pl.ds(r, L, stride=dilation) if dilation > 1 else slice(None)
                dq_r = dq4[r * nb:(r + 1) * nb].reshape(L, 128) * 0.125
                dk_r = dkd[1 + r * nb:1 + (r + 1) * nb].reshape(L, 128) * 0.125
                dv_r = dvd[1 + r * nb:1 + (r + 1) * nb].reshape(L, 128)
                if idx == 0:
                    dqa[rows, :], dka[rows, :], dva[rows, :] = dq_r, dk_r, dv_r
                else:
                    dqa[rows, :] += dq_r
                    dka[rows, :] += dk_r
                    dva[rows, :] += dv_r

        dq_ref[...] = dqa[...].astype(BF16)
        dk_ref[...] = dka[...].astype(BF16)
        dv_ref[...] = dva[...].astype(BF16)

    spec = lambda off: pl.BlockSpec((S, 128), lambda b, p: (b, 4 * off + p))
    ospec = pl.BlockSpec((S, 128), lambda b, p: (b, p))
    blocks = lambda n, dt: pltpu.VMEM((n, BLOCK, 128), dt)
    return pl.pallas_call(
        body, grid=(B, 4), in_specs=[_SMEM_SPEC, spec(0), spec(1), spec(2), ospec, ospec, ospec],
        out_specs=[ospec] * 3, out_shape=[SDS((B * S, WIDTH), BF16)] * 3,
        scratch_shapes=[pltpu.VMEM((S, 128), F32)] * 5
        + [blocks(NB, BF16), blocks(NB, BF16), blocks(NB + 1, BF16), blocks(NB + 1, BF16), blocks(NB, F32), blocks(NB, F32),
           blocks(NB + 1, F32), blocks(NB + 1, F32)] + [pltpu.VMEM((S, 128), F32)] * 3,
        compiler_params=_cp(), name=name,
    )(_alibi_slopes(), z, z, z, dy, ya, lse)


FOX_TQ = 256


def _split3(v):
    hi = v.astype(BF16).astype(F32)
    mid = (v - hi).astype(BF16).astype(F32)
    lo = (v - hi - mid).astype(BF16).astype(F32)
    return [hi, mid, lo]


def _with_spare_lanes(base, e, cols):
    lane = lax.broadcasted_iota(jnp.int32, (1, 128), 1)
    off = HEAD_DIM * (1 - e)
    extra = jnp.zeros(base.shape, F32)
    for j, c in enumerate(cols):
        extra = jnp.where(lane == off + j, c, extra)
    return base + extra.astype(BF16)


ONES3 = [1.0, 1.0, 1.0]


def _fox_fwd(z, cc, *, B, S, name):
    def body(q_ref, k_ref, v_ref, cc_ref, o_ref, l_ref, qa, ka):
        (m_first, m_second), first = _pair_masks()
        ccv = cc_ref[...]
        eighth = jnp.asarray(0.125, BF16)
        for e, hm in enumerate((m_first, m_second)):
            c_e = jnp.broadcast_to(ccv[:, HEAD_DIM * e:HEAD_DIM * e + 1], (S, 128))
            qa[e] = _with_spare_lanes(q_ref[...] * hm * eighth, e, _split3(c_e) + ONES3)
            ka[e] = _with_spare_lanes(k_ref[...] * hm, e, ONES3 + _split3(-c_e))
        for qi in range(S // FOX_TQ):
            r0, kend = qi * FOX_TQ, (qi + 1) * FOX_TQ
            vv = v_ref[0:kend, :]
            row = lax.broadcasted_iota(jnp.int32, (FOX_TQ, kend), 0) + r0
            col = lax.broadcasted_iota(jnp.int32, (FOX_TQ, kend), 1)
            causal = col <= row
            outs, lses = [], []
            for e in (0, 1):
                s = jnp.where(causal, _dot(qa[e, r0:kend, :], ka[e, 0:kend, :], NT), NEG)
                m = jnp.max(s, axis=1, keepdims=True)
                pe = jnp.exp(s - m)
                l = jnp.sum(pe, axis=1, keepdims=True)
                outs.append(_dot(pe.astype(BF16), vv) * (1.0 / l))
                lses.append(m + jnp.log(l))
            o_ref[r0:kend, :] = jnp.where(first, outs[0], outs[1]).astype(BF16)
            l_ref[r0:kend, :] = jnp.where(first, lses[0], lses[1])

    spec = lambda off: pl.BlockSpec((S, 128), lambda b, p: (b, 4 * off + p))
    pspec = pl.BlockSpec((S, 128), lambda b, p: (b, p))
    return pl.pallas_call(
        body, grid=(B, 4), in_specs=[spec(3), spec(4), spec(5), pspec], out_specs=[pspec, pspec],
        out_shape=[SDS((B * S, WIDTH), BF16), SDS((B * S, WIDTH), F32)],
        scratch_shapes=[pltpu.VMEM((2, S, 128), BF16)] * 2, compiler_params=_cp(), name=name,
    )(z, z, z, cc)


def _fox_bwd(z, dy, lse, cc, *, B, S, name):
    def body(q_ref, k_ref, v_ref, do_ref, lse_ref, cc_ref, dq_ref, dk_ref, dv_ref, dc_ref,
             qa, ka, qp, kp, vp, dp, dk_s, dv_s, dc_s):
        (m_first, m_second), first = _pair_masks()
        ccv, lsev = cc_ref[...], lse_ref[...]
        eighth = jnp.asarray(0.125, BF16)
        for e, hm in enumerate((m_first, m_second)):
            lane0 = slice(HEAD_DIM * e, HEAD_DIM * e + 1)
            c_e = jnp.broadcast_to(ccv[:, lane0], (S, 128))
            lse_e = jnp.broadcast_to(lsev[:, lane0], (S, 128))
            qp[e] = q_ref[...] * hm
            kp[e] = k_ref[...] * hm
            dp[e] = do_ref[...] * hm
            qa[e] = _with_spare_lanes(qp[e] * eighth, e, _split3(c_e - lse_e) + ONES3)
            ka[e] = _with_spare_lanes(kp[e], e, ONES3 + _split3(-c_e))
            vp[e] = v_ref[...] * hm
        dk_s[...] = jnp.zeros_like(dk_s)
        dv_s[...] = jnp.zeros_like(dv_s)
        dc_s[...] = jnp.zeros_like(dc_s)
        for qi in range(S // FOX_TQ):
            r0, kend = qi * FOX_TQ, (qi + 1) * FOX_TQ
            krow = lax.broadcasted_iota(jnp.int32, (kend, FOX_TQ), 0)
            qcol = lax.broadcasted_iota(jnp.int32, (kend, FOX_TQ), 1) + r0
            causal = krow <= qcol
            dq_t = jnp.zeros((FOX_TQ, 128), F32)
            for e in (0, 1):
                sel = first if e == 0 else ~first
                pt = jnp.where(causal, jnp.exp(_dot(ka[e, 0:kend, :], qa[e, r0:kend, :], NT)), 0.0)
                dpt = _dot(vp[e, 0:kend, :], dp[e, r0:kend, :], NT)
                mean = jnp.sum(pt * dpt, axis=0, keepdims=True) / jnp.sum(pt, axis=0, keepdims=True)
                dst = pt * (dpt - mean)
                dsb = dst.astype(BF16)
                dv_s[0:kend, :] += _dot(pt.astype(BF16), dp[e, r0:kend, :])
                dk_s[0:kend, :] += _dot(dsb, qp[e, r0:kend, :]) * 0.125
                dq_t = dq_t + _dot(dsb, kp[e, 0:kend, :], TN)
                dc_s[0:kend, :] += jnp.where(sel, -jnp.sum(dst, axis=1, keepdims=True), 0.0)
            dq_ref[r0:kend, :] = (dq_t * 0.125).astype(BF16)
        dk_ref[...] = dk_s[...].astype(BF16)
        dv_ref[...] = dv_s[...].astype(BF16)
        dc_ref[...] = dc_s[...]

    spec = lambda off: pl.BlockSpec((S, 128), lambda b, p: (b, 4 * off + p))
    pspec = pl.BlockSpec((S, 128), lambda b, p: (b, p))
    return pl.pallas_call(
        body, grid=(B, 4),
        in_specs=[spec(3), spec(4), spec(5), pl.BlockSpec((S, 128), lambda b, p: (b, 4 + p)), pspec, pspec],
        out_specs=[pspec] * 4,
        out_shape=[SDS((B * S, WIDTH), BF16)] * 3 + [SDS((B * S, WIDTH), F32)],
        scratch_shapes=[pltpu.VMEM((2, S, 128), BF16)] * 6 + [pltpu.VMEM((S, 128), F32)] * 3,
        compiler_params=_cp(), name=name,
    )(z, z, z, dy, lse, cc)


def _xattn_fwd(q, kv, *, B, S, M, tq, name):
    D = D_MODEL

    def body(q_ref, kv_ref, o_ref):
        for h in range(N_XH):
            cs = slice(XHD * h, XHD * (h + 1))
            s = _dot(q_ref[:, cs], kv_ref[:, cs], NT) * (1.0 / 16.0)
            pe = jnp.exp(s - jnp.max(s, axis=1, keepdims=True))
            l = jnp.sum(pe, axis=1, keepdims=True)
            o_ref[:, cs] = (_dot(pe.astype(BF16), kv_ref[:, D + XHD * h:D + XHD * (h + 1)]) * (1.0 / l)).astype(BF16)

    nq = S // tq
    return pl.pallas_call(
        body, grid=(B, nq),
        in_specs=[pl.BlockSpec((tq, D), lambda b, t: (b * nq + t, 0)), pl.BlockSpec((M, 2 * D), lambda b, t: (b, 0))],
        out_specs=pl.BlockSpec((tq, D), lambda b, t: (b * nq + t, 0)), out_shape=SDS((B * S, D), BF16),
        compiler_params=_cp(), name=name,
    )(q, kv)


def _xattn_bwd(q, kv, do, *, B, S, M, tq, name):
    D = D_MODEL

    def body(q_ref, kv_ref, do_ref, dq_ref, dkv_ref):
        t = pl.program_id(1)

        @pl.when(t == 0)
        def _():
            dkv_ref[...] = jnp.zeros_like(dkv_ref)

        for h in range(N_XH):
            cs = slice(XHD * h, XHD * (h + 1))
            vs = slice(D + XHD * h, D + XHD * (h + 1))
            qh, kh, vh, doh = q_ref[:, cs], kv_ref[:, cs], kv_ref[:, vs], do_ref[:, cs]
            s = _dot(qh, kh, NT) * (1.0 / 16.0)
            pe = jnp.exp(s - jnp.max(s, axis=1, keepdims=True))
            pe = pe * (1.0 / jnp.sum(pe, axis=1, keepdims=True))
            dp = _dot(doh, vh, NT)
            ds = (pe * (dp - jnp.sum(pe * dp, axis=1, keepdims=True))).astype(BF16)
            dq_ref[:, cs] = (_dot(ds, kh) * (1.0 / 16.0)).astype(BF16)
            dkv_ref[:, cs] += _dot(ds, qh, TN) * (1.0 / 16.0)
            dkv_ref[:, vs] += _dot(pe.astype(BF16), doh, TN)

    nq = S // tq
    qspec = pl.BlockSpec((tq, D), lambda b, t: (b * nq + t, 0))
    kvspec = pl.BlockSpec((M, 2 * D), lambda b, t: (b, 0))
    return pl.pallas_call(
        body, grid=(B, nq), in_specs=[qspec, kvspec, qspec], out_specs=[qspec, kvspec],
        out_shape=[SDS((B * S, D), BF16), SDS((B * M, 2 * D), F32)], compiler_params=_cp(), name=name,
    )(q, kv, do)


def _assemble_dz(parts, dgate, *, tm, name):
    T = dgate.shape[0]

    def body(*refs):
        o_ref = refs[-1]
        for j in range(6):
            o_ref[:, WIDTH * j:WIDTH * (j + 1)] = refs[j][...]
        o_ref[:, QKV_W:IN_PAD] = refs[6][...]

    wspec = pl.BlockSpec((tm, WIDTH), lambda i: (i, 0))
    return pl.pallas_call(
        body, grid=(T // tm,), in_specs=[wspec] * 6 + [pl.BlockSpec((tm, 128), lambda i: (i, 0))],
        out_specs=pl.BlockSpec((tm, IN_PAD), lambda i: (i, 0)), out_shape=SDS((T, IN_PAD), BF16),
        compiler_params=_cp(), name=name,
    )(*parts, dgate)


def _adamw(parts, w, m, v, *, tr, name):
    R, C = w.shape

    def body(p_ref, w_ref, m_ref, v_ref, g_ref, d_ref, nm_ref, nv_ref):
        g = p_ref[0].astype(F32)
        for d in range(1, N_DEV):
            g = g + p_ref[d].astype(F32)
        m2 = ADAM_B1 * m_ref[...] + (1.0 - ADAM_B1) * g
        v2 = ADAM_B2 * v_ref[...] + (1.0 - ADAM_B2) * (g * g)
        m_hat = m2 / (1.0 - ADAM_B1 ** ADAM_STEP)
        v_hat = v2 / (1.0 - ADAM_B2 ** ADAM_STEP)
        g_ref[...] = g
        d_ref[...] = -ADAM_LR * (m_hat / (jnp.sqrt(v_hat) + ADAM_EPS) + ADAM_WD * w_ref[...])
        nm_ref[...] = m2
        nv_ref[...] = v2

    spec = pl.BlockSpec((tr, C), lambda i: (i, 0))
    return pl.pallas_call(
        body, grid=(R // tr,), in_specs=[pl.BlockSpec((N_DEV, tr, C), lambda i: (0, i, 0)), spec, spec, spec],
        out_specs=[spec] * 4, out_shape=[SDS((R, C), F32)] * 4, compiler_params=_cp(), name=name,
    )(parts, w, m, v)


def _peer(k, x, y, c):
    return (1 - x if k & 4 else x, 1 - y if k & 2 else y, 1 - c if k & 1 else c)


_HBM_SPEC = pl.BlockSpec(memory_space=pltpu.HBM)
_SEM_SPEC = pl.BlockSpec(memory_space=pltpu.SEMAPHORE)
_SPLIT_EFFECT = pltpu.SideEffectType.DATAFLOW_SIDE_EFFECTING


def _split_copies(srcs, lands, send_sems, recv_sems, modes):
    x, y, c = (lax.axis_index(a) for a in AXES)
    me = 4 * x + 2 * y + c
    copies = []
    for i, md in enumerate(modes):
        for k in range(1, N_DEV):
            px, py, pc = _peer(k, x, y, c)
            src = srcs[i] if md == "gather" else srcs[i].at[4 * px + 2 * py + pc]
            j = i * (N_DEV - 1) + k - 1
            copies.append(pltpu.make_async_remote_copy(
                src_ref=src, dst_ref=lands[i].at[me], send_sem=send_sems.at[j], recv_sem=recv_sems.at[j],
                device_id=(px, py, pc), device_id_type=pl.DeviceIdType.MESH))
    return copies


def _exchange_start(arrays, modes, *, name):
    n = len(arrays)
    hbm = lambda a: pltpu.with_memory_space_constraint(a, pltpu.HBM)
    srcs = [hbm(a) for a in arrays]
    lands = [hbm(jnp.broadcast_to(a[None], (N_DEV,) + a.shape)) if md == "gather" else hbm(a) for a, md in zip(arrays, modes)]

    def body(*refs):
        for cp in _split_copies(refs[:n], refs[n:2 * n], refs[2 * n], refs[2 * n + 1], modes):
            cp.start()
        token = refs[-1]
        token[...] = jnp.zeros_like(token)

    sems = pltpu.SemaphoreType.DMA((n * (N_DEV - 1),))
    outs = pl.pallas_call(
        body, name=name, in_specs=[_HBM_SPEC] * (2 * n),
        out_shape=(sems, sems, *[pltpu.HBM(a.shape, a.dtype) for a in srcs + lands], SDS((8, 128), F32)),
        out_specs=(_SEM_SPEC, _SEM_SPEC, *[_HBM_SPEC] * (2 * n), pl.BlockSpec(memory_space=pltpu.VMEM)),
        input_output_aliases={i: 2 + i for i in range(2 * n)},
        compiler_params=pltpu.CompilerParams(has_side_effects=_SPLIT_EFFECT),
    )(*srcs, *lands)
    return (outs[0], outs[1], outs[2:2 + n], outs[2 + n:2 + 2 * n], modes), outs[-1]


def _exchange_wait(handle, after, *, name):
    send_sems, recv_sems, srcs, lands, modes = handle
    n = len(srcs)

    def body(*refs):
        for cp in _split_copies(refs[:n], refs[n:2 * n], refs[2 * n], refs[2 * n + 1], modes):
            cp.wait_send()
            cp.wait_recv()

    outs = pl.pallas_call(
        body, name=name, in_specs=[_HBM_SPEC] * (2 * n) + [_SEM_SPEC, _SEM_SPEC, pl.BlockSpec(memory_space=pl.ANY)],
        out_shape=tuple(pltpu.HBM(a.shape, a.dtype) for a in list(srcs) + list(lands)), out_specs=tuple([_HBM_SPEC] * (2 * n)),
        input_output_aliases={i: i for i in range(2 * n)},
        compiler_params=pltpu.CompilerParams(has_side_effects=_SPLIT_EFFECT),
    )(*srcs, *lands, send_sems, recv_sems, after)
    return list(outs[n:])


def _exchange(arrays, modes, *, name):
    n = len(arrays)
    out_shape = [SDS((N_DEV,) + a.shape if md == "gather" else a.shape, a.dtype) for a, md in zip(arrays, modes)]

    def body(*refs):
        ins, outs = refs[:n], refs[n:2 * n]
        send_sems, recv_sems, local_sems = refs[2 * n:]
        x, y, c = (lax.axis_index(a) for a in AXES)
        me = 4 * x + 2 * y + c
        copies = []
        for i, md in enumerate(modes):
            src = ins[i] if md == "gather" else ins[i].at[me]
            cp = pltpu.make_async_copy(src, outs[i].at[me], local_sems.at[i])
            cp.start()
            copies.append(cp)
            for k in range(1, N_DEV):
                px, py, pc = _peer(k, x, y, c)
                src = ins[i] if md == "gather" else ins[i].at[4 * px + 2 * py + pc]
                cp = pltpu.make_async_remote_copy(
                    src_ref=src, dst_ref=outs[i].at[me], send_sem=send_sems.at[i, k - 1], recv_sem=recv_sems.at[i, k - 1],
                    device_id=(px, py, pc), device_id_type=pl.DeviceIdType.MESH)
                cp.start()
                copies.append(cp)
        for cp in copies:
            cp.wait()

    anyspec = pl.BlockSpec(memory_space=pl.ANY)
    return pl.pallas_call(
        body, in_specs=[anyspec] * n, out_specs=[anyspec] * n, out_shape=out_shape,
        scratch_shapes=[pltpu.SemaphoreType.DMA((n, N_DEV - 1)), pltpu.SemaphoreType.DMA((n, N_DEV - 1)),
                        pltpu.SemaphoreType.DMA((n,))],
        name=name,
    )(*arrays)


def _local_step(x, mem, g_mix, b_forget, g_xattn, g_mem, g_mlp, g_final, target, get_w_in, get_rest, send):
    B, S, D = x.shape
    M = mem.shape[1]
    T = B * S
    x0 = x.reshape(T, D)
    mem2 = mem.reshape(B * M, D)
    tgt = target.reshape(T, D)
    b_pad = jnp.pad(b_forget, (0, 120)).reshape(1, 128)
    after = lambda a, tok: a if tok is None else a + tok[0, 0]

    w_in_pad = get_w_in()
    h1, z = _rms_matmul(x0, g_mix, w_in_pad[:, :QKV_W], tm=ROWS, tn=QKV_W, out_dtype=BF16, name="f_in")
    gate = _matmul_nn(h1, w_in_pad[:, QKV_W:], tm=1024, tn=128, tk=D, out_dtype=F32, name="f_gate")
    cc = _gate_fwd(gate, b_pad, B=B, S=S, name="f_gatecum")
    ya, lse = _dil_attn_fwd(z, B=B, S=S, name="f_dil")
    yf, lse_f = _fox_fwd(z, cc, B=B, S=S, name="f_fox")
    ymix = jnp.concatenate([ya, yf], axis=1)
    w = get_rest(ymix)
    x1 = _matmul_nn(ymix, w["w_out"], res=x0, tm=ROWS, tn=D, tk=D, out_dtype=F32, name="f_out")
    h2, q = _rms_matmul(x1, g_xattn, w["w_xq"], tm=ROWS, tn=D, out_dtype=BF16, name="f_xq")
    mn, kv = _rms_matmul(mem2, g_mem, w["w_kv"], tm=B * M, tn=D, out_dtype=BF16, name="f_xkv")
    xo = _xattn_fwd(q, kv, B=B, S=S, M=M, tq=512, name="f_xattn")
    x2 = _matmul_nn(xo, w["w_xo"], res=x1, tm=ROWS, tn=D, tk=D, out_dtype=F32, name="f_xo")
    h3, act = _rms_matmul(x2, g_mlp, w["w_up"], tm=ROWS, tn=D_FF, out_dtype=BF16, relu=True, name="f_up")
    x3 = _matmul_nn(act, w["w_down"], res=x2, square=True, tm=ROWS, tn=D, tk=D_FF, out_dtype=F32, name="f_down")
    dx3, dg_final, loss = _loss_head(x3, g_final, tgt, tm=512, name="f_loss")

    du = _matmul_nt(dx3, w["w_down"], mul2a=act, tm=ROWS, tn=D_FF, name="b_dact")
    dw_down = _matmul_tn(act, dx3, square=True, bk=1024, bn=D, tt=ACC_ROWS, out_dtype=BF16, name="b_wdown")
    dw_up = _matmul_tn(h3, du, bk=D, bn=1024, tt=ACC_ROWS, out_dtype=BF16, name="b_wup")
    tok = send(dict(w_down=dw_down, w_up=dw_up))
    dx2, dg_mlp = _matmul_nt_rms(du, w["w_up"], x2, after(g_mlp, tok), dx3, tm=ROWS, tk=D_FF, name="b_dh3")
    dxo = _matmul_nt(dx2, w["w_xo"], tm=1024, tn=D, name="b_dxo")
    dw_xo = _matmul_tn(xo, dx2, bk=D, bn=D, tt=ACC_ROWS, out_dtype=BF16, name="b_wxo")
    dq, dkv = _xattn_bwd(q, kv, dxo, B=B, S=S, M=M, tq=512, name="b_xattn")
    dw_xq = _matmul_tn(h2, dq, bk=D, bn=D, tt=ACC_ROWS, out_dtype=BF16, name="b_wxq")
    dx1, dg_xattn = _matmul_nt_rms(dq, w["w_xq"], x1, g_xattn, dx2, tm=ROWS, tk=D, name="b_dh2")
    dw_kv = _matmul_tn(mn, dkv, bk=D, bn=D, tt=B * M, out_dtype=BF16, name="b_wkv")
    _, dg_mem = _matmul_nt_rms(dkv, w["w_kv"], mem2, g_mem, None, tm=min(ROWS, B * M), tk=2 * D, name="b_dmem")
    dy = _matmul_nt(dx1, w["w_out"], tm=1024, tn=D, name="b_dy")
    dw_out = _matmul_tn(ymix, dx1, bk=D, bn=D, tt=ACC_ROWS, out_dtype=BF16, name="b_wout")
    tok = send(dict(w_xo=dw_xo, w_xq=dw_xq, w_xk=dw_kv[:, :D], w_xv=dw_kv[:, D:], w_out=dw_out))
    dqf, dkf, dvf, dcc = _fox_bwd(z, dy, lse_f, cc, B=B, S=S, name="b_fox")
    dgate, db = _gate_bwd(dcc, gate, after(b_pad, tok), B=B, S=S, name="b_gate")
    dqa, dka, dva = _dil_attn_bwd(z, dy, ya, lse, B=B, S=S, name="b_dil")
    dz = _assemble_dz([dqa, dka, dva, dqf, dkf, dvf], dgate, tm=512, name="b_dz")
    dw_in = _matmul_tn(h1, dz, bk=D, bn=640, tt=ACC_ROWS, out_dtype=BF16, name="b_win")
    tok = send(dict(w_in=dw_in))
    gx, dg_mix = _matmul_nt_rms(dz, w_in_pad, x0, after(g_mix, tok), dx1, tm=ROWS, tk=IN_PAD, name="b_dh1")

    small = dict(g_mix=dg_mix, b_forget=db, g_xattn=dg_xattn, g_mem=dg_mem, g_mlp=dg_mlp, g_final=dg_final)
    return gx.reshape(B, S, D), small, loss


SMALL_ROWS = ("g_mix", "b_forget", "g_xattn", "g_mem", "g_mlp", "g_final")
COL_SHARDED = ("w_in", "w_up")


def _pack_rows(rows):
    D = D_MODEL
    rows = [jnp.pad(r.reshape(-1), (0, D - r.size)) for r in rows]
    rows += [jnp.zeros((D,), F32)] * (8 - len(rows))
    return jnp.stack(rows)


def _full(name, g):
    if name in COL_SHARDED:
        return g.transpose(1, 0, 2).reshape(g.shape[1], -1)
    return g.reshape(-1, g.shape[2])


def _blocks(name, g, shard_shape):
    if name in COL_SHARDED:
        n = shard_shape[1]
        return g[:, :n * N_DEV].reshape(g.shape[0], N_DEV, n).transpose(1, 0, 2)
    return g.reshape((N_DEV,) + shard_shape)


def kernel(x, mem, g_mix, w_in, b_forget, w_out, g_xattn, g_mem, w_xq, w_xk, w_xv, w_xo, g_mlp, w_up, w_down, g_final, loss_target, m_g_mix, m_w_in, m_b_forget, m_w_out, m_g_xattn, m_g_mem, m_w_xq, m_w_xk, m_w_xv, m_w_xo, m_g_mlp, m_w_up, m_w_down, m_g_final, v_g_mix, v_w_in, v_b_forget, v_w_out, v_g_xattn, v_g_mem, v_w_xq, v_w_xk, v_w_xv, v_w_xo, v_g_mlp, v_w_up, v_w_down, v_g_final):
    W = dict(w_in=w_in, w_out=w_out, w_xq=w_xq, w_xk=w_xk, w_xv=w_xv, w_xo=w_xo, w_up=w_up, w_down=w_down)
    Mo = dict(w_in=m_w_in, w_out=m_w_out, w_xq=m_w_xq, w_xk=m_w_xk, w_xv=m_w_xv, w_xo=m_w_xo, w_up=m_w_up, w_down=m_w_down)
    Vo = dict(w_in=v_w_in, w_out=v_w_out, w_xq=v_w_xq, w_xk=v_w_xk, w_xv=v_w_xv, w_xo=v_w_xo, w_up=v_w_up, w_down=v_w_down)
    later = [n for n in W if n != "w_in"]

    first_handle, first_token = _exchange_start([w_in.astype(BF16)], ["gather"], name="gather_in_start")
    rest_handle, rest_token = _exchange_start([W[n].astype(BF16) + first_token[0, 0].astype(BF16) for n in later],
                                              ["gather"] * len(later), name="gather_rest_start")

    def get_w_in():
        (g,) = _exchange_wait(first_handle, rest_token, name="gather_in_wait")
        return jnp.pad(_full("w_in", g), ((0, 0), (0, IN_PAD - IN_W)))

    def get_rest(after):
        full = {n: _full(n, g) for n, g in zip(later, _exchange_wait(rest_handle, after, name="gather_rest_wait"))}
        full["w_kv"] = jnp.concatenate([full.pop("w_xk"), full.pop("w_xv")], axis=1)
        return full

    sent = []

    def send(grads):
        names = list(grads)
        handle, token = _exchange_start([_blocks(n, grads[n], W[n].shape) for n in names], ["scatter"] * len(names),
                                        name=f"scatter{len(sent)}_start")
        sent.append((names, handle))
        return token

    gx, small, loss = _local_step(x, mem, g_mix, b_forget, g_xattn, g_mem, g_mlp, g_final, loss_target, get_w_in, get_rest, send)

    received = {}
    for i, (names, handle) in enumerate(sent):
        received.update(zip(names, _exchange_wait(handle, gx, name=f"scatter{i}_wait")))
    packed = _pack_rows([small[n] for n in SMALL_ROWS] + [loss[0, :1]])
    (packed_all,) = _exchange([packed], ["gather"], name="gather_small")

    res = {n: _adamw(received[n], W[n], Mo[n], Vo[n], tr=128, name=f"adamw_{n}") for n in W}
    small_w = dict(g_mix=g_mix, b_forget=b_forget, g_xattn=g_xattn, g_mem=g_mem, g_mlp=g_mlp, g_final=g_final)
    small_m = dict(g_mix=m_g_mix, b_forget=m_b_forget, g_xattn=m_g_xattn, g_mem=m_g_mem, g_mlp=m_g_mlp, g_final=m_g_final)
    small_v = dict(g_mix=v_g_mix, b_forget=v_b_forget, g_xattn=v_g_xattn, g_mem=v_g_mem, g_mlp=v_g_mlp, g_final=v_g_final)
    sres = _adamw(packed_all, _pack_rows([small_w[n] for n in SMALL_ROWS]), _pack_rows([small_m[n] for n in SMALL_ROWS]),
                  _pack_rows([small_v[n] for n in SMALL_ROWS]), tr=8, name="adamw_small")
    for i, n in enumerate(SMALL_ROWS):
        res[n] = [r[i, :small_w[n].size] for r in sres]
    loss_total = sres[0][6, 0]

    order = ["g_mix", "w_in", "b_forget", "w_out", "g_xattn", "g_mem", "w_xq", "w_xk", "w_xv", "w_xo", "g_mlp", "w_up", "w_down", "g_final"]
    return (loss_total, gx, *[res[n][0] for n in order], *[res[n][1] for n in order],
            *[res[n][2] for n in order], *[res[n][3] for n in order])
```

```python
import jax
import jax.numpy as jnp
from jax import lax
from jax.experimental import pallas as pl
from jax.experimental.pallas import tpu as pltpu

F32, BF16 = jnp.float32, jnp.bfloat16
SDS = jax.ShapeDtypeStruct

D_MODEL = 1024
HEAD_DIM = 64
WIDTH = 512
QKV_W = 6 * WIDTH
IN_W = QKV_W + 8
IN_PAD = QKV_W + 128
BLOCK = 128
DIL_CONFIGS = ((128, 1), (512, 4), (2048, 16))
N_XH, XHD = 4, 256
D_FF = 4096
EPS = 1e-6
NEG = -1e30
N_DEV = 8
AXES = ("x", "y", "c")

ADAM_LR, ADAM_B1, ADAM_B2, ADAM_EPS, ADAM_WD, ADAM_STEP = 0.001, 0.9, 0.999, 1e-08, 0.01, 10

VMEM_CAP_V7X = 64 * 1024 * 1024
VMEM_LIMIT = VMEM_CAP_V7X * 7 // 8

ROWS = 512
ACC_ROWS = 2048

NT = (((1,), (1,)), ((), ()))
TN = (((0,), (0,)), ((), ()))


def _cp(**kw):
    return pltpu.CompilerParams(vmem_limit_bytes=VMEM_LIMIT, **kw)


def _dot(a, b, dims=None):
    if dims is None:
        return jnp.dot(a, b, preferred_element_type=F32)
    return lax.dot_general(a, b, dims, preferred_element_type=F32)


def _rstd(xv):
    return lax.rsqrt(jnp.mean(xv * xv, axis=-1, keepdims=True) + EPS)


def _rms_bwd(dh, xv, g):
    r = _rstd(xv)
    xhat = xv * r
    dxhat = dh * g
    dx = r * (dxhat - xhat * jnp.mean(dxhat * xhat, axis=-1, keepdims=True))
    return dx, jnp.sum(dh * xhat, axis=0, keepdims=True)


def _rms_matmul(x, g, w, *, tm, tn, out_dtype, relu=False, name):
    T, D = x.shape
    N = w.shape[1]

    def body(x_ref, g_ref, w_ref, h_ref, o_ref, h_s):
        @pl.when(pl.program_id(1) == 0)
        def _():
            xv = x_ref[...]
            h = (xv * _rstd(xv) * g_ref[...]).astype(BF16)
            h_s[...] = h
            h_ref[...] = h

        acc = _dot(h_s[...], w_ref[...])
        if relu:
            acc = jnp.maximum(acc, 0.0)
        o_ref[...] = acc.astype(out_dtype)

    return pl.pallas_call(
        body, grid=(T // tm, N // tn),
        in_specs=[pl.BlockSpec((tm, D), lambda i, j: (i, 0)), pl.BlockSpec((1, D), lambda i, j: (0, 0)),
                  pl.BlockSpec((D, tn), lambda i, j: (0, j))],
        out_specs=[pl.BlockSpec((tm, D), lambda i, j: (i, 0)), pl.BlockSpec((tm, tn), lambda i, j: (i, j))],
        out_shape=[SDS((T, D), BF16), SDS((T, N), out_dtype)],
        scratch_shapes=[pltpu.VMEM((tm, D), BF16)], compiler_params=_cp(), name=name,
    )(x, g.reshape(1, D), w)


def _matmul_nn(a, w, *, res=None, square=False, tm, tn, tk, out_dtype, name):
    T, K = a.shape
    N = w.shape[1]
    nk = K // tk

    def body(*refs):
        a_ref, w_ref = refs[0], refs[1]
        res_ref = refs[2] if res is not None else None
        o_ref = refs[3] if res is not None else refs[2]
        k = pl.program_id(2)
        av = a_ref[...]
        if square:
            af = av.astype(F32)
            av = (af * af).astype(BF16)
        part = _dot(av, w_ref[...])

        def finish(r):
            if res_ref is not None:
                r = res_ref[...] + r
            o_ref[...] = r.astype(out_dtype)

        if nk == 1:
            finish(part)
        else:
            acc = refs[-1]

            @pl.when(k == 0)
            def _():
                acc[...] = part

            @pl.when(k > 0)
            def _():
                acc[...] += part

            @pl.when(k == nk - 1)
            def _():
                finish(acc[...])

    in_specs = [pl.BlockSpec((tm, tk), lambda i, j, k: (i, k)), pl.BlockSpec((tk, tn), lambda i, j, k: (k, j))]
    args = [a, w]
    if res is not None:
        in_specs.append(pl.BlockSpec((tm, tn), lambda i, j, k: (i, j)))
        args.append(res)
    return pl.pallas_call(
        body, grid=(T // tm, N // tn, nk), in_specs=in_specs,
        out_specs=pl.BlockSpec((tm, tn), lambda i, j, k: (i, j)), out_shape=SDS((T, N), out_dtype),
        scratch_shapes=[pltpu.VMEM((tm, tn), F32)] if nk > 1 else [], compiler_params=_cp(), name=name,
    )(*args)


def _matmul_nt(g, w, *, mul2a=None, tm, tn, name):
    T, K = g.shape
    N = w.shape[0]

    def body(*refs):
        g_ref, w_ref = refs[0], refs[1]
        o_ref = refs[-1]
        acc = _dot(g_ref[...].astype(BF16), w_ref[...], NT)
        if mul2a is not None:
            acc = acc * (2.0 * refs[2][...].astype(F32))
        o_ref[...] = acc.astype(BF16)

    in_specs = [pl.BlockSpec((tm, K), lambda i, j: (i, 0)), pl.BlockSpec((tn, K), lambda i, j: (j, 0))]
    args = [g, w]
    if mul2a is not None:
        in_specs.append(pl.BlockSpec((tm, tn), lambda i, j: (i, j)))
        args.append(mul2a)
    return pl.pallas_call(
        body, grid=(T // tm, N // tn), in_specs=in_specs,
        out_specs=pl.BlockSpec((tm, tn), lambda i, j: (i, j)), out_shape=SDS((T, N), BF16),
        compiler_params=_cp(), name=name,
    )(*args)


def _matmul_nt_rms(g, w, x, gain, dres, *, tm, tk, name):
    T, K = g.shape
    D = w.shape[0]
    nk = K // tk
    nt = T // tm

    def body(*refs):
        g_ref, w_ref, x_ref, gain_ref = refs[:4]
        dres_ref = refs[4] if dres is not None else None
        n_in = 5 if dres is not None else 4
        dx_ref, dg_ref = refs[n_in], refs[n_in + 1]
        i, k = pl.program_id(0), pl.program_id(1)
        part = _dot(g_ref[...].astype(BF16), w_ref[...], NT)

        def finish(dh):
            dx, dg = _rms_bwd(dh, x_ref[...], gain_ref[...])
            if dres_ref is not None:
                dx = dres_ref[...] + dx
            dx_ref[...] = dx

            @pl.when(i == 0)
            def _():
                dg_ref[...] = dg

            @pl.when(i > 0)
            def _():
                dg_ref[...] += dg

        if nk == 1:
            finish(part)
        else:
            acc = refs[-1]

            @pl.when(k == 0)
            def _():
                acc[...] = part

            @pl.when(k > 0)
            def _():
                acc[...] += part

            @pl.when(k == nk - 1)
            def _():
                finish(acc[...])

    in_specs = [pl.BlockSpec((tm, tk), lambda i, k: (i, k)), pl.BlockSpec((D, tk), lambda i, k: (0, k)),
                pl.BlockSpec((tm, D), lambda i, k: (i, 0)), pl.BlockSpec((1, D), lambda i, k: (0, 0))]
    args = [g, w, x, gain.reshape(1, D)]
    if dres is not None:
        in_specs.append(pl.BlockSpec((tm, D), lambda i, k: (i, 0)))
        args.append(dres)
    return pl.pallas_call(
        body, grid=(nt, nk), in_specs=in_specs,
        out_specs=[pl.BlockSpec((tm, D), lambda i, k: (i, 0)), pl.BlockSpec((1, D), lambda i, k: (0, 0))],
        out_shape=[SDS((T, D), F32), SDS((1, D), F32)],
        scratch_shapes=[pltpu.VMEM((tm, D), F32)] if nk > 1 else [], compiler_params=_cp(), name=name,
    )(*args)


def _matmul_tn(a, g, *, square=False, bk, bn, tt, out_dtype, name):
    T, K = a.shape
    N = g.shape[1]
    nt = T // tt

    def body(a_ref, g_ref, o_ref, acc):
        t = pl.program_id(2)
        av = a_ref[...]
        if square:
            af = av.astype(F32)
            av = (af * af).astype(BF16)
        part = _dot(av, g_ref[...].astype(BF16), TN)
        if nt == 1:
            o_ref[...] = part.astype(out_dtype)
        else:
            @pl.when(t == 0)
            def _():
                acc[...] = part

            @pl.when((t > 0) & (t < nt - 1))
            def _():
                acc[...] += part

            @pl.when(t == nt - 1)
            def _():
                o_ref[...] = (acc[...] + part).astype(out_dtype)

    return pl.pallas_call(
        body, grid=(K // bk, N // bn, nt),
        in_specs=[pl.BlockSpec((tt, bk), lambda i, j, t: (t, i)), pl.BlockSpec((tt, bn), lambda i, j, t: (t, j))],
        out_specs=pl.BlockSpec((bk, bn), lambda i, j, t: (i, j)), out_shape=SDS((K, N), out_dtype),
        scratch_shapes=[pltpu.VMEM((bk, bn), F32)], compiler_params=_cp(), name=name,
    )(a, g)


def _loss_head(x3, g_final, target, *, tm, name):
    T, D = x3.shape

    def body(x_ref, g_ref, t_ref, dx_ref, dg_ref, loss_ref):
        i = pl.program_id(0)
        xv, g = x_ref[...], g_ref[...]
        r = _rstd(xv)
        xhat = xv * r
        diff = xhat * g - t_ref[...]
        part = 0.5 * jnp.sum(jnp.mean(diff * diff, axis=-1, keepdims=True), axis=0, keepdims=True)
        dy = diff * (1.0 / D)
        dxhat = dy * g
        dx_ref[...] = r * (dxhat - xhat * jnp.mean(dxhat * xhat, axis=-1, keepdims=True))
        dg = jnp.sum(dy * xhat, axis=0, keepdims=True)
        lp = jnp.broadcast_to(part, loss_ref.shape)

        @pl.when(i == 0)
        def _():
            dg_ref[...] = dg
            loss_ref[...] = lp

        @pl.when(i > 0)
        def _():
            dg_ref[...] += dg
            loss_ref[...] += lp

    return pl.pallas_call(
        body, grid=(T // tm,),
        in_specs=[pl.BlockSpec((tm, D), lambda i: (i, 0)), pl.BlockSpec((1, D), lambda i: (0, 0)),
                  pl.BlockSpec((tm, D), lambda i: (i, 0))],
        out_specs=[pl.BlockSpec((tm, D), lambda i: (i, 0)), pl.BlockSpec((1, D), lambda i: (0, 0)),
                   pl.BlockSpec((8, 128), lambda i: (0, 0))],
        out_shape=[SDS((T, D), F32), SDS((1, D), F32), SDS((8, 128), F32)],
        compiler_params=_cp(), name=name,
    )(x3, g_final.reshape(1, D), target)


def _head_lanes(shape, width):
    return lax.broadcasted_iota(jnp.int32, shape, len(shape) - 1) // width


def _gate_fwd(gate, b_pad, *, B, S, name):
    def body(g_ref, b_ref, cc_ref):
        xv = g_ref[...] + b_ref[...]
        lf = jnp.minimum(xv, 0.0) - jnp.log(1.0 + jnp.exp(-jnp.abs(xv)))
        lane = lax.broadcasted_iota(jnp.int32, lf.shape, 1)
        row = lax.broadcasted_iota(jnp.int32, lf.shape, 0)
        c = jnp.where(lane < 8, lf, 0.0)
        sh = 1
        while sh < S:
            c = c + jnp.where(row >= sh, pltpu.roll(c, sh, 0), 0.0)
            sh *= 2
        grp = _head_lanes((S, WIDTH), HEAD_DIM)
        cc = jnp.zeros((S, WIDTH), F32)
        for h in range(8):
            cc = jnp.where(grp == h, c[:, h:h + 1], cc)
        cc_ref[...] = cc

    return pl.pallas_call(
        body, grid=(B,),
        in_specs=[pl.BlockSpec((S, 128), lambda b: (b, 0)), pl.BlockSpec((1, 128), lambda b: (0, 0))],
        out_specs=pl.BlockSpec((S, WIDTH), lambda b: (b, 0)), out_shape=SDS((B * S, WIDTH), F32),
        compiler_params=_cp(), name=name,
    )(gate, b_pad)


def _gate_bwd(dcc, gate, b_pad, *, B, S, name):
    def body(dcc_ref, g_ref, b_ref, dg_ref, db_ref):
        bi = pl.program_id(0)
        dccv = dcc_ref[...]
        lane = lax.broadcasted_iota(jnp.int32, (S, 128), 1)
        row = lax.broadcasted_iota(jnp.int32, (S, 128), 0)
        dc = jnp.zeros((S, 128), F32)
        for h in range(8):
            dc = jnp.where(lane == h, dccv[:, HEAD_DIM * h:HEAD_DIM * h + 1], dc)
        sh = 1
        while sh < S:
            dc = dc + jnp.where(row < S - sh, pltpu.roll(dc, S - sh, 0), 0.0)
            sh *= 2
        xv = g_ref[...] + b_ref[...]
        dgate = jnp.where(lane < 8, dc / (1.0 + jnp.exp(xv)), 0.0)
        dg_ref[...] = dgate.astype(BF16)
        db = jnp.sum(dgate, axis=0, keepdims=True)

        @pl.when(bi == 0)
        def _():
            db_ref[...] = db

        @pl.when(bi > 0)
        def _():
            db_ref[...] += db

    return pl.pallas_call(
        body, grid=(B,),
        in_specs=[pl.BlockSpec((S, WIDTH), lambda b: (b, 0)), pl.BlockSpec((S, 128), lambda b: (b, 0)),
                  pl.BlockSpec((1, 128), lambda b: (0, 0))],
        out_specs=[pl.BlockSpec((S, 128), lambda b: (b, 0)), pl.BlockSpec((1, 128), lambda b: (0, 0))],
        out_shape=[SDS((B * S, 128), BF16), SDS((1, 128), F32)],
        compiler_params=_cp(), name=name,
    )(dcc, gate, b_pad)


_SMEM_SPEC = pl.BlockSpec(memory_space=pltpu.SMEM)


def _alibi_slopes():
    return 2.0 ** (-(jnp.arange(1, 9, dtype=F32) * (8.0 / 8)))


def _pair_masks():
    lane = lax.broadcasted_iota(jnp.int32, (1, 128), 1)
    first = lane < HEAD_DIM
    return (first.astype(BF16), (~first).astype(BF16)), first


BNT =(((2,), (2,)), ((0,), (0,)))
BNN = (((2,), (1,)), ((0,), (0,)))
BTN = (((1,), (1,)), ((0,), (0,)))


def _split3(v):
    hi = v.astype(BF16).astype(F32)
    mid = (v - hi).astype(BF16).astype(F32)
    lo = (v - hi - mid).astype(BF16).astype(F32)
    return [hi, mid, lo]


def _with_spare_lanes(base, e, cols):
    lane = lax.broadcasted_iota(jnp.int32, (1, 128), 1)
    off = HEAD_DIM * (1 - e)
    extra = jnp.zeros(base.shape, F32)
    for j, c in enumerate(cols):
        extra = jnp.where(lane == off + j, c, extra)
    return base + extra.astype(BF16)


ONES3 = [1.0, 1.0, 1.0]


def _band_iotas(nb_total):
    blk = lax.broadcasted_iota(jnp.int32, (nb_total, 1, 1), 0)
    qi = lax.broadcasted_iota(jnp.int32, (1, BLOCK, BLOCK), 1)
    kj = lax.broadcasted_iota(jnp.int32, (1, BLOCK, BLOCK), 2)
    pos = lax.broadcasted_iota(jnp.int32, (1, BLOCK, 128), 1).astype(F32)
    return blk, qi, kj, pos, kj <= qi


def _to_residue_major(dst, src_f32, dilation, nb, lead=0):
    L = nb * BLOCK
    for r in range(dilation):
        rows = src_f32[pl.ds(r, L, stride=dilation), :] if dilation > 1 else src_f32[...]
        dst[lead + r * nb:lead + (r + 1) * nb] = rows.reshape(nb, BLOCK, 128).astype(dst.dtype)


def _dil_attn_fwd(z, *, B, S, name):
    NB = S // BLOCK

    def body(slope_ref, q_ref, k_ref, v_ref, y_ref, lse_ref, qf, kf, vf, qd, kd, vd, od, ld, acc_o, acc_l):
        (m_first, m_second), first = _pair_masks()
        p = pl.program_id(1)
        qf[...] = q_ref[...].astype(F32)
        kf[...] = k_ref[...].astype(F32)
        vf[...] = v_ref[...].astype(F32)
        kd[0] = jnp.zeros((BLOCK, 128), BF16)
        vd[0] = jnp.zeros((BLOCK, 128), BF16)
        blk, qi, kj, pos, see_cur = _band_iotas(NB)
        eighth = jnp.asarray(0.125, BF16)

        for idx, (_, dilation) in enumerate(DIL_CONFIGS):
            nb = NB // dilation
            _to_residue_major(qd, qf, dilation, nb)
            _to_residue_major(kd, kf, dilation, nb, lead=1)
            _to_residue_major(vd, vf, dilation, nb, lead=1)
            q4, vc = qd[...], vd[1:NB + 1]
            outs, lses = [], []
            for e, hm in enumerate((m_first, m_second)):
                step = slope_ref[2 * p + e] * dilation
                qbase = q4 * hm * eighth
                kall = _with_spare_lanes(kd[...] * hm, e, [1.0, step * pos])
                sc = jnp.where(see_cur, _dot(_with_spare_lanes(qbase, e, [-step * pos, 1.0]), kall[1:NB + 1], BNT), NEG)
                m = jnp.max(sc, axis=2, keepdims=True)
                if nb > 1:
                    see_prev = (kj >= qi) & (blk % nb != 0)
                    sp = jnp.where(see_prev, _dot(_with_spare_lanes(qbase, e, [-step * (pos + BLOCK), 1.0]), kall[0:NB], BNT), NEG)
                    m = jnp.maximum(m, jnp.max(sp, axis=2, keepdims=True))
                pc = jnp.exp(sc - m)
                l = jnp.sum(pc, axis=2, keepdims=True)
                o = _dot(pc.astype(BF16), vc, BNN)
                if nb > 1:
                    pp = jnp.exp(sp - m)
                    l = l + jnp.sum(pp, axis=2, keepdims=True)
                    o = o + _dot(pp.astype(BF16), vd[0:NB], BNN)
                outs.append(o * (1.0 / l))
                lses.append(m + jnp.log(l))
            od[...] = jnp.where(first, outs[0], outs[1])
            ld[...] = jnp.where(first, lses[0], lses[1])

            L = nb * BLOCK
            for r in range(dilation):
                rows = pl.ds(r, L, stride=dilation) if dilation > 1 else slice(None)
                o_new = od[r * nb:(r + 1) * nb].reshape(L, 128)
                l_new = ld[r * nb:(r + 1) * nb].reshape(L, 128)
                if idx == 0:
                    acc_o[rows, :] = o_new
                    acc_l[rows, :] = l_new
                else:
                    l_old = acc_l[rows, :]
                    m2 = jnp.maximum(l_old, l_new)
                    w_old, w_new = jnp.exp(l_old - m2), jnp.exp(l_new - m2)
                    tot = w_old + w_new
                    acc_o[rows, :] = (w_old * acc_o[rows, :] + w_new * o_new) * (1.0 / tot)
                    acc_l[rows, :] = m2 + jnp.log(tot)

        y_ref[...] = acc_o[...].astype(BF16)
        lse_ref[...] = acc_l[...]

    spec = lambda off: pl.BlockSpec((S, 128), lambda b, p: (b, 4 * off + p))
    ospec = pl.BlockSpec((S, 128), lambda b, p: (b, p))
    blocks = lambda n, dt: pltpu.VMEM((n, BLOCK, 128), dt)
    return pl.pallas_call(
        body, grid=(B, 4), in_specs=[_SMEM_SPEC, spec(0), spec(1), spec(2)], out_specs=[ospec, ospec],
        out_shape=[SDS((B * S, WIDTH), BF16), SDS((B * S, WIDTH), F32)],
        scratch_shapes=[pltpu.VMEM((S, 128), F32)] * 3 + [blocks(NB, BF16), blocks(NB + 1, BF16), blocks(NB + 1, BF16),
                                                         blocks(NB, F32), blocks(NB, F32)] + [pltpu.VMEM((S, 128), F32)] * 2,
        compiler_params=_cp(), name=name,
    )(_alibi_slopes(), z, z, z)


def _dil_attn_bwd(z, dy, ya, lse, *, B, S, name):
    NB = S // BLOCK

    def body(slope_ref, q_ref, k_ref, v_ref, do_ref, o_ref, lse_ref, dq_ref, dk_ref, dv_ref,
             qf, kf, vf, dof, ef, qd, dod, kd, vd, lsd, ed, dkd, dvd, dqa, dka, dva):
        (m_first, m_second), first = _pair_masks()
        p = pl.program_id(1)
        qf[...] = q_ref[...].astype(F32)
        kf[...] = k_ref[...].astype(F32)
        vf[...] = v_ref[...].astype(F32)
        dov = do_ref[...].astype(F32)
        dof[...] = dov
        prod = dov * o_ref[...].astype(F32)
        ef[...] = jnp.where(first, jnp.sum(jnp.where(first, prod, 0.0), axis=1, keepdims=True),
                            jnp.sum(jnp.where(first, 0.0, prod), axis=1, keepdims=True))
        kd[0] = jnp.zeros((BLOCK, 128), BF16)
        vd[0] = jnp.zeros((BLOCK, 128), BF16)
        blk, qi, kj, pos, see_cur = _band_iotas(NB)
        eighth = jnp.asarray(0.125, BF16)

        for idx, (_, dilation) in enumerate(DIL_CONFIGS):
            nb = NB // dilation
            _to_residue_major(qd, qf, dilation, nb)
            _to_residue_major(dod, dof, dilation, nb)
            _to_residue_major(kd, kf, dilation, nb, lead=1)
            _to_residue_major(vd, vf, dilation, nb, lead=1)
            _to_residue_major(lsd, lse_ref, dilation, nb)
            _to_residue_major(ed, ef, dilation, nb)
            q4, do4, kc, vc = qd[...], dod[...], kd[1:NB + 1], vd[1:NB + 1]
            dq4 = None
            dkc = dvc = dkp = dvp = None
            for e, hm in enumerate((m_first, m_second)):
                lane0 = slice(HEAD_DIM * e, HEAD_DIM * e + 1)
                step = slope_ref[2 * p + e] * dilation
                qm, dom = q4 * hm, do4 * hm
                neg_lse = _split3(-jnp.broadcast_to(lsd[...][:, :, lane0], (NB, BLOCK, 128)))
                neg_dot = _split3(-jnp.broadcast_to(ed[...][:, :, lane0], (NB, BLOCK, 128)))
                qbase = qm * eighth
                kall = _with_spare_lanes(kd[...] * hm, e, [1.0, step * pos] + ONES3)
                vall = _with_spare_lanes(vd[...] * hm, e, ONES3)
                doa = _with_spare_lanes(dom, e, neg_dot)
                qa = _with_spare_lanes(qbase, e, [-step * pos, 1.0] + neg_lse)
                pc = jnp.where(see_cur, jnp.exp(_dot(qa, kall[1:NB + 1], BNT)), 0.0)
                dsc = (pc * _dot(doa, vall[1:NB + 1], BNT)).astype(BF16)
                pcb = pc.astype(BF16)
                dqe = _dot(dsc, kc, BNN)
                dkc = _dot(dsc, qm, BTN) if e == 0 else dkc + _dot(dsc, qm, BTN)
                dvc = _dot(pcb, dom, BTN) if e == 0 else dvc + _dot(pcb, dom, BTN)
                if nb > 1:
                    kp = kd[0:NB]
                    see_prev = (kj >= qi) & (blk % nb != 0)
                    qa = _with_spare_lanes(qbase, e, [-step * (pos + BLOCK), 1.0] + neg_lse)
                    pp = jnp.where(see_prev, jnp.exp(_dot(qa, kall[0:NB], BNT)), 0.0)
                    dsp = (pp * _dot(doa, vall[0:NB], BNT)).astype(BF16)
                    ppb = pp.astype(BF16)
                    dqe = dqe + _dot(dsp, kp, BNN)
                    dkp = _dot(dsp, qm, BTN) if e == 0 else dkp + _dot(dsp, qm, BTN)
                    dvp = _dot(ppb, dom, BTN) if e == 0 else dvp + _dot(ppb, dom, BTN)
                dq4 = dqe if e == 0 else jnp.where(first, dq4, dqe)

            dkd[1:NB + 1] = dkc
            dvd[1:NB + 1] = dvc
            if nb > 1:
                dkd[1:NB] += dkp[1:NB]
                dvd[1:NB] += dvp[1:NB]
            L = nb * BLOCK
            for r in range(dilation):
                rows = pl.ds(r, L, stride=dilation) if dilation > 1 else slice(None)
                dq_r = dq4[r * nb:(r + 1) * nb].reshape(L, 128) * 0.125
                dk_r = dkd[1 + r * nb:1 + (r + 1) * nb].reshape(L, 128) * 0.125
                dv_r = dvd[1 + r * nb:1 + (r + 1) * nb].reshape(L, 128)
                if idx == 0:
                    dqa[rows, :], dka[rows, :], dva[rows, :] = dq_r, dk_r, dv_r
                else:
                    dqa[rows, :] += dq_r
                    dka[rows, :] += dk_r
                    dva[rows, :] += dv_r

        dq_ref[...] = dqa[...].astype(BF16)
        dk_ref[...] = dka[...].astype(BF16)
        dv_ref[...] = dva[...].astype(BF16)

    spec = lambda off: pl.BlockSpec((S, 128), lambda b, p: (b, 4 * off + p))
    ospec = pl.BlockSpec((S, 128), lambda b, p: (b, p))
    blocks = lambda n, dt: pltpu.VMEM((n, BLOCK, 128), dt)
    return pl.pallas_call(
        body, grid=(B, 4), in_specs=[_SMEM_SPEC, spec(0), spec(1), spec(2), ospec, ospec, ospec],
        out_specs=[ospec] * 3, out_shape=[SDS((B * S, WIDTH), BF16)] * 3,
        scratch_shapes=[pltpu.VMEM((S, 128), F32)] * 5
        + [blocks(NB, BF16), blocks(NB, BF16), blocks(NB + 1, BF16), blocks(NB + 1, BF16), blocks(NB, F32), blocks(NB, F32),
           blocks(NB + 1, F32), blocks(NB + 1, F32)] + [pltpu.VMEM((S, 128), F32)] * 3,
        compiler_params=_cp(), name=name,
    )(_alibi_slopes(), z, z, z, dy, ya, lse)


FOX_TQ = 256


def _fox_fwd(z, cc, *, B, S, name):
    def body(q_ref, k_ref, v_ref, cc_ref, o_ref, l_ref, qa, ka):
        (m_first, m_second), first = _pair_masks()
        ccv = cc_ref[...]
        eighth = jnp.asarray(0.125, BF16)
        for e, hm in enumerate((m_first, m_second)):
            c_e = jnp.broadcast_to(ccv[:, HEAD_DIM * e:HEAD_DIM * e + 1], (S, 128))
            qa[e] = _with_spare_lanes(q_ref[...] * hm * eighth, e, _split3(c_e) + ONES3)
            ka[e] = _with_spare_lanes(k_ref[...] * hm, e, ONES3 + _split3(-c_e))
        for qi in range(S // FOX_TQ):
            r0, kend = qi * FOX_TQ, (qi + 1) * FOX_TQ
            vv = v_ref[0:kend, :]
            row = lax.broadcasted_iota(jnp.int32, (FOX_TQ, kend), 0) + r0
            col = lax.broadcasted_iota(jnp.int32, (FOX_TQ, kend), 1)
            causal = col <= row
            outs, lses = [], []
            for e in (0, 1):
                s = jnp.where(causal, _dot(qa[e, r0:kend, :], ka[e, 0:kend, :], NT), NEG)
                m = jnp.max(s, axis=1, keepdims=True)
                pe = jnp.exp(s - m)
                l = jnp.sum(pe, axis=1, keepdims=True)
                outs.append(_dot(pe.astype(BF16), vv) * (1.0 / l))
                lses.append(m + jnp.log(l))
            o_ref[r0:kend, :] = jnp.where(first, outs[0], outs[1]).astype(BF16)
            l_ref[r0:kend, :] = jnp.where(first, lses[0], lses[1])

    spec = lambda off: pl.BlockSpec((S, 128), lambda b, p: (b, 4 * off + p))
    pspec = pl.BlockSpec((S, 128), lambda b, p: (b, p))
    return pl.pallas_call(
        body, grid=(B, 4), in_specs=[spec(3), spec(4), spec(5), pspec], out_specs=[pspec, pspec],
        out_shape=[SDS((B * S, WIDTH), BF16), SDS((B * S, WIDTH), F32)],
        scratch_shapes=[pltpu.VMEM((2, S, 128), BF16)] * 2, compiler_params=_cp(), name=name,
    )(z, z, z, cc)


def _fox_bwd(z, dy, lse, cc, *, B, S, name):
    def body(q_ref, k_ref, v_ref, do_ref, lse_ref, cc_ref, dq_ref, dk_ref, dv_ref, dc_ref,
             qa, ka, qp, kp, vp, dp, dk_s, dv_s, dc_s):
        (m_first, m_second), first = _pair_masks()
        ccv, lsev = cc_ref[...], lse_ref[...]
        eighth = jnp.asarray(0.125, BF16)
        for e, hm in enumerate((m_first, m_second)):
            lane0 = slice(HEAD_DIM * e, HEAD_DIM * e + 1)
            c_e = jnp.broadcast_to(ccv[:, lane0], (S, 128))
            lse_e = jnp.broadcast_to(lsev[:, lane0], (S, 128))
            qp[e] = q_ref[...] * hm
            kp[e] = k_ref[...] * hm
            dp[e] = do_ref[...] * hm
            qa[e] = _with_spare_lanes(qp[e] * eighth, e, _split3(c_e - lse_e) + ONES3)
            ka[e] = _with_spare_lanes(kp[e], e, ONES3 + _split3(-c_e))
            vp[e] = v_ref[...] * hm
        dk_s[...] = jnp.zeros_like(dk_s)
        dv_s[...] = jnp.zeros_like(dv_s)
        dc_s[...] = jnp.zeros_like(dc_s)
        for qi in range(S // FOX_TQ):
            r0, kend = qi * FOX_TQ, (qi + 1) * FOX_TQ
            krow = lax.broadcasted_iota(jnp.int32, (kend, FOX_TQ), 0)
            qcol = lax.broadcasted_iota(jnp.int32, (kend, FOX_TQ), 1) + r0
            causal = krow <= qcol
            dq_t = jnp.zeros((FOX_TQ, 128), F32)
            for e in (0, 1):
                sel = first if e == 0 else ~first
                pt = jnp.where(causal, jnp.exp(_dot(ka[e, 0:kend, :], qa[e, r0:kend, :], NT)), 0.0)
                dpt = _dot(vp[e, 0:kend, :], dp[e, r0:kend, :], NT)
                mean = jnp.sum(pt * dpt, axis=0, keepdims=True) / jnp.sum(pt, axis=0, keepdims=True)
                dst = pt * (dpt - mean)
                dsb = dst.astype(BF16)
                dv_s[0:kend, :] += _dot(pt.astype(BF16), dp[e, r0:kend, :])
                dk_s[0:kend, :] += _dot(dsb, qp[e, r0:kend, :]) * 0.125
                dq_t = dq_t + _dot(dsb, kp[e, 0:kend, :], TN)
                dc_s[0:kend, :] += jnp.where(sel, -jnp.sum(dst, axis=1, keepdims=True), 0.0)
            dq_ref[r0:kend, :] = (dq_t * 0.125).astype(BF16)
        dk_ref[...] = dk_s[...].astype(BF16)
        dv_ref[...] = dv_s[...].astype(BF16)
        dc_ref[...] = dc_s[...]

    spec = lambda off: pl.BlockSpec((S, 128), lambda b, p: (b, 4 * off + p))
    pspec = pl.BlockSpec((S, 128), lambda b, p: (b, p))
    return pl.pallas_call(
        body, grid=(B, 4),
        in_specs=[spec(3), spec(4), spec(5), pl.BlockSpec((S, 128), lambda b, p: (b, 4 + p)), pspec, pspec],
        out_specs=[pspec] * 4,
        out_shape=[SDS((B * S, WIDTH), BF16)] * 3 + [SDS((B * S, WIDTH), F32)],
        scratch_shapes=[pltpu.VMEM((2, S, 128), BF16)] * 6 + [pltpu.VMEM((S, 128), F32)] * 3,
        compiler_params=_cp(), name=name,
    )(z, z, z, dy, lse, cc)


def _xattn_fwd(q, kv, *, B, S, M, tq, name):
    D = D_MODEL

    def body(q_ref, kv_ref, o_ref):
        for h in range(N_XH):
            cs = slice(XHD * h, XHD * (h + 1))
            s = _dot(q_ref[:, cs], kv_ref[:, cs], NT) * (1.0 / 16.0)
            pe = jnp.exp(s - jnp.max(s, axis=1, keepdims=True))
            l = jnp.sum(pe, axis=1, keepdims=True)
            o_ref[:, cs] = (_dot(pe.astype(BF16), kv_ref[:, D + XHD * h:D + XHD * (h + 1)]) * (1.0 / l)).astype(BF16)

    nq = S // tq
    return pl.pallas_call(
        body, grid=(B, nq),
        in_specs=[pl.BlockSpec((tq, D), lambda b, t: (b * nq + t, 0)), pl.BlockSpec((M, 2 * D), lambda b, t: (b, 0))],
        out_specs=pl.BlockSpec((tq, D), lambda b, t: (b * nq + t, 0)), out_shape=SDS((B * S, D), BF16),
        compiler_params=_cp(), name=name,
    )(q, kv)


def _xattn_bwd(q, kv, do, *, B, S, M, tq, name):
    D = D_MODEL

    def body(q_ref, kv_ref, do_ref, dq_ref, dkv_ref):
        t = pl.program_id(1)

        @pl.when(t == 0)
        def _():
            dkv_ref[...] = jnp.zeros_like(dkv_ref)

        for h in range(N_XH):
            cs = slice(XHD * h, XHD * (h + 1))
            vs = slice(D + XHD * h, D + XHD * (h + 1))
            qh, kh, vh, doh = q_ref[:, cs], kv_ref[:, cs], kv_ref[:, vs], do_ref[:, cs]
            s = _dot(qh, kh, NT) * (1.0 / 16.0)
            pe = jnp.exp(s - jnp.max(s, axis=1, keepdims=True))
            pe = pe * (1.0 / jnp.sum(pe, axis=1, keepdims=True))
            dp = _dot(doh, vh, NT)
            ds = (pe * (dp - jnp.sum(pe * dp, axis=1, keepdims=True))).astype(BF16)
            dq_ref[:, cs] = (_dot(ds, kh) * (1.0 / 16.0)).astype(BF16)
            dkv_ref[:, cs] += _dot(ds, qh, TN) * (1.0 / 16.0)
            dkv_ref[:, vs] += _dot(pe.astype(BF16), doh, TN)

    nq = S // tq
    qspec = pl.BlockSpec((tq, D), lambda b, t: (b * nq + t, 0))
    kvspec = pl.BlockSpec((M, 2 * D), lambda b, t: (b, 0))
    return pl.pallas_call(
        body, grid=(B, nq), in_specs=[qspec, kvspec, qspec], out_specs=[qspec, kvspec],
        out_shape=[SDS((B * S, D), BF16), SDS((B * M, 2 * D), F32)], compiler_params=_cp(), name=name,
    )(q, kv, do)


def _assemble_dz(parts, dgate, *, tm, name):
    T = dgate.shape[0]

    def body(*refs):
        o_ref = refs[-1]
        for j in range(6):
            o_ref[:, WIDTH * j:WIDTH * (j + 1)] = refs[j][...]
        o_ref[:, QKV_W:IN_PAD] = refs[6][...]

    wspec = pl.BlockSpec((tm, WIDTH), lambda i: (i, 0))
    return pl.pallas_call(
        body, grid=(T // tm,), in_specs=[wspec] * 6 + [pl.BlockSpec((tm, 128), lambda i: (i, 0))],
        out_specs=pl.BlockSpec((tm, IN_PAD), lambda i: (i, 0)), out_shape=SDS((T, IN_PAD), BF16),
        compiler_params=_cp(), name=name,
    )(*parts, dgate)


def _adamw(parts, w, m, v, *, tr, name):
    R, C = w.shape

    def body(p_ref, w_ref, m_ref, v_ref, g_ref, d_ref, nm_ref, nv_ref):
        g = p_ref[0].astype(F32)
        for d in range(1, N_DEV):
            g = g + p_ref[d].astype(F32)
        m2 = ADAM_B1 * m_ref[...] + (1.0 - ADAM_B1) * g
        v2 = ADAM_B2 * v_ref[...] + (1.0 - ADAM_B2) * (g * g)
        m_hat = m2 / (1.0 - ADAM_B1 ** ADAM_STEP)
        v_hat = v2 / (1.0 - ADAM_B2 ** ADAM_STEP)
        g_ref[...] = g
        d_ref[...] = -ADAM_LR * (m_hat / (jnp.sqrt(v_hat) + ADAM_EPS) + ADAM_WD * w_ref[...])
        nm_ref[...] = m2
        nv_ref[...] = v2

    spec = pl.BlockSpec((tr, C), lambda i: (i, 0))
    return pl.pallas_call(
        body, grid=(R // tr,), in_specs=[pl.BlockSpec((N_DEV, tr, C), lambda i: (0, i, 0)), spec, spec, spec],
        out_specs=[spec] * 4, out_shape=[SDS((R, C), F32)] * 4, compiler_params=_cp(), name=name,
    )(parts, w, m, v)


def _peer(k, x, y, c):
    return (1 - x if k & 4 else x, 1 - y if k & 2 else y, 1 - c if k & 1 else c)


_HBM_SPEC = pl.BlockSpec(memory_space=pltpu.HBM)
_SEM_SPEC = pl.BlockSpec(memory_space=pltpu.SEMAPHORE)
_SPLIT_EFFECT = pltpu.SideEffectType.DATAFLOW_SIDE_EFFECTING


def _split_copies(srcs, lands, send_sems, recv_sems, modes):
    x, y, c = (lax.axis_index(a) for a in AXES)
    me = 4 * x + 2 * y + c
    copies = []
    for i, md in enumerate(modes):
        for k in range(1, N_DEV):
            px, py, pc = _peer(k, x, y, c)
            src = srcs[i] if md == "gather" else srcs[i].at[4 * px + 2 * py + pc]
            j = i * (N_DEV - 1) + k - 1
            copies.append(pltpu.make_async_remote_copy(
                src_ref=src, dst_ref=lands[i].at[me], send_sem=send_sems.at[j], recv_sem=recv_sems.at[j],
                device_id=(px, py, pc), device_id_type=pl.DeviceIdType.MESH))
    return copies


def _exchange_start(arrays, modes, *, name):
    n = len(arrays)
    hbm = lambda a: pltpu.with_memory_space_constraint(a, pltpu.HBM)
    srcs = [hbm(a) for a in arrays]
    lands = [hbm(jnp.broadcast_to(a[None], (N_DEV,) + a.shape)) if md == "gather" else hbm(a) for a, md in zip(arrays, modes)]

    def body(*refs):
        for cp in _split_copies(refs[:n], refs[n:2 * n], refs[2 * n], refs[2 * n + 1], modes):
            cp.start()
        token = refs[-1]
        token[...] = jnp.zeros_like(token)

    sems = pltpu.SemaphoreType.DMA((n * (N_DEV - 1),))
    outs = pl.pallas_call(
        body, name=name, in_specs=[_HBM_SPEC] * (2 * n),
        out_shape=(sems, sems, *[pltpu.HBM(a.shape, a.dtype) for a in srcs + lands], SDS((8, 128), F32)),
        out_specs=(_SEM_SPEC, _SEM_SPEC, *[_HBM_SPEC] * (2 * n), pl.BlockSpec(memory_space=pltpu.VMEM)),
        input_output_aliases={i: 2 + i for i in range(2 * n)},
        compiler_params=pltpu.CompilerParams(has_side_effects=_SPLIT_EFFECT),
    )(*srcs, *lands)
    return (outs[0], outs[1], outs[2:2 + n], outs[2 + n:2 + 2 * n], modes), outs[-1]


def _exchange_wait(handle, after, *, name):
    send_sems, recv_sems, srcs, lands, modes = handle
    n = len(srcs)

    def body(*refs):
        for cp in _split_copies(refs[:n], refs[n:2 * n], refs[2 * n], refs[2 * n + 1], modes):
            cp.wait_send()
            cp.wait_recv()

    outs = pl.pallas_call(
        body, name=name, in_specs=[_HBM_SPEC] * (2 * n) + [_SEM_SPEC, _SEM_SPEC, pl.BlockSpec(memory_space=pl.ANY)],
        out_shape=tuple(pltpu.HBM(a.shape, a.dtype) for a in list(srcs) + list(lands)), out_specs=tuple([_HBM_SPEC] * (2 * n)),
        input_output_aliases={i: i for i in range(2 * n)},
        compiler_params=pltpu.CompilerParams(has_side_effects=_SPLIT_EFFECT),
    )(*srcs, *lands, send_sems, recv_sems, after)
    return list(outs[n:])


def _exchange(arrays, modes, *, name):
    n = len(arrays)
    out_shape = [SDS((N_DEV,) + a.shape if md == "gather" else a.shape, a.dtype) for a, md in zip(arrays, modes)]

    def body(*refs):
        ins, outs = refs[:n], refs[n:2 * n]
        send_sems, recv_sems, local_sems = refs[2 * n:]
        x, y, c = (lax.axis_index(a) for a in AXES)
        me = 4 * x + 2 * y + c
        copies = []
        for i, md in enumerate(modes):
            src = ins[i] if md == "gather" else ins[i].at[me]
            cp = pltpu.make_async_copy(src, outs[i].at[me], local_sems.at[i])
            cp.start()
            copies.append(cp)
            for k in range(1, N_DEV):
                px, py, pc = _peer(k, x, y, c)
                src = ins[i] if md == "gather" else ins[i].at[4 * px + 2 * py + pc]
                cp = pltpu.make_async_remote_copy(
                    src_ref=src, dst_ref=outs[i].at[me], send_sem=send_sems.at[i, k - 1], recv_sem=recv_sems.at[i, k - 1],
                    device_id=(px, py, pc), device_id_type=pl.DeviceIdType.MESH)
                cp.start()
                copies.append(cp)
        for cp in copies:
            cp.wait()

    anyspec = pl.BlockSpec(memory_space=pl.ANY)
    return pl.pallas_call(
        body, in_specs=[anyspec] * n, out_specs=[anyspec] * n, out_shape=out_shape,
        scratch_shapes=[pltpu.SemaphoreType.DMA((n, N_DEV - 1)), pltpu.SemaphoreType.DMA((n, N_DEV - 1)),
                        pltpu.SemaphoreType.DMA((n,))],
        name=name,
    )(*arrays)


def _local_step(x, mem, g_mix, b_forget, g_xattn, g_mem, g_mlp, g_final, target, get_w_in, get_rest, send):
    B, S, D = x.shape
    M = mem.shape[1]
    T = B * S
    x0 = x.reshape(T, D)
    mem2 = mem.reshape(B * M, D)
    tgt = target.reshape(T, D)
    b_pad = jnp.pad(b_forget, (0, 120)).reshape(1, 128)
    after = lambda a, tok: a if tok is None else a + tok[0, 0]

    w_in_pad = get_w_in()
    h1, z = _rms_matmul(x0, g_mix, w_in_pad[:, :QKV_W], tm=ROWS, tn=QKV_W, out_dtype=BF16, name="f_in")
    gate = _matmul_nn(h1, w_in_pad[:, QKV_W:], tm=1024, tn=128, tk=D, out_dtype=F32, name="f_gate")
    cc = _gate_fwd(gate, b_pad, B=B, S=S, name="f_gatecum")
    ya, lse = _dil_attn_fwd(z, B=B, S=S, name="f_dil")
    yf, lse_f = _fox_fwd(z, cc, B=B, S=S, name="f_fox")
    ymix = jnp.concatenate([ya, yf], axis=1)
    w = get_rest(ymix)
    x1 = _matmul_nn(ymix, w["w_out"], res=x0, tm=ROWS, tn=D, tk=D, out_dtype=F32, name="f_out")
    h2, q = _rms_matmul(x1, g_xattn, w["w_xq"], tm=ROWS, tn=D, out_dtype=BF16, name="f_xq")
    mn, kv = _rms_matmul(mem2, g_mem, w["w_kv"], tm=B * M, tn=D, out_dtype=BF16, name="f_xkv")
    xo = _xattn_fwd(q, kv, B=B, S=S, M=M, tq=512, name="f_xattn")
    x2 = _matmul_nn(xo, w["w_xo"], res=x1, tm=ROWS, tn=D, tk=D, out_dtype=F32, name="f_xo")
    h3, act = _rms_matmul(x2, g_mlp, w["w_up"], tm=ROWS, tn=D_FF, out_dtype=BF16, relu=True, name="f_up")
    x3 = _matmul_nn(act, w["w_down"], res=x2, square=True, tm=ROWS, tn=D, tk=D_FF, out_dtype=F32, name="f_down")
    dx3, dg_final, loss = _loss_head(x3, g_final, tgt, tm=512, name="f_loss")

    du = _matmul_nt(dx3, w["w_down"], mul2a=act, tm=ROWS, tn=D_FF, name="b_dact")
    dw_down = _matmul_tn(act, dx3, square=True, bk=1024, bn=D, tt=ACC_ROWS, out_dtype=BF16, name="b_wdown")
    dw_up = _matmul_tn(h3, du, bk=D, bn=1024, tt=ACC_ROWS, out_dtype=BF16, name="b_wup")
    tok = send(dict(w_down=dw_down, w_up=dw_up))
    dx2, dg_mlp = _matmul_nt_rms(du, w["w_up"], x2, after(g_mlp, tok), dx3, tm=ROWS, tk=D_FF, name="b_dh3")
    dxo = _matmul_nt(dx2, w["w_xo"], tm=1024, tn=D, name="b_dxo")
    dw_xo = _matmul_tn(xo, dx2, bk=D, bn=D, tt=ACC_ROWS, out_dtype=BF16, name="b_wxo")
    dq, dkv = _xattn_bwd(q, kv, dxo, B=B, S=S, M=M, tq=512, name="b_xattn")
    dw_xq = _matmul_tn(h2, dq, bk=D, bn=D, tt=ACC_ROWS, out_dtype=BF16, name="b_wxq")
    dx1, dg_xattn = _matmul_nt_rms(dq, w["w_xq"], x1, g_xattn, dx2, tm=ROWS, tk=D, name="b_dh2")
    dw_kv = _matmul_tn(mn, dkv, bk=D, bn=D, tt=B * M, out_dtype=BF16, name="b_wkv")
    _, dg_mem = _matmul_nt_rms(dkv, w["w_kv"], mem2, g_mem, None, tm=min(ROWS, B * M), tk=2 * D, name="b_dmem")
    dy = _matmul_nt(dx1, w["w_out"], tm=1024, tn=D, name="b_dy")
    dw_out = _matmul_tn(ymix, dx1, bk=D, bn=D, tt=ACC_ROWS, out_dtype=BF16, name="b_wout")
    tok = send(dict(w_xo=dw_xo, w_xq=dw_xq, w_xk=dw_kv[:, :D], w_xv=dw_kv[:, D:], w_out=dw_out))
    dqf, dkf, dvf, dcc = _fox_bwd(z, dy, lse_f, cc, B=B, S=S, name="b_fox")
    dgate, db = _gate_bwd(dcc, gate, after(b_pad, tok), B=B, S=S, name="b_gate")
    dqa, dka, dva = _dil_attn_bwd(z, dy, ya, lse, B=B, S=S, name="b_dil")
    dz = _assemble_dz([dqa, dka, dva, dqf, dkf, dvf], dgate, tm=512, name="b_dz")
    dw_in = _matmul_tn(h1, dz, bk=D, bn=640, tt=ACC_ROWS, out_dtype=BF16, name="b_win")
    tok = send(dict(w_in=dw_in))
    gx, dg_mix = _matmul_nt_rms(dz, w_in_pad, x0, after(g_mix, tok), dx1, tm=ROWS, tk=IN_PAD, name="b_dh1")

    small = dict(g_mix=dg_mix, b_forget=db, g_xattn=dg_xattn, g_mem=dg_mem, g_mlp=dg_mlp, g_final=dg_final)
    return gx.reshape(B, S, D), small, loss


SMALL_ROWS = ("g_mix", "b_forget", "g_xattn", "g_mem", "g_mlp", "g_final")
COL_SHARDED = ("w_in", "w_up")


def _pack_rows(rows):
    D = D_MODEL
    rows = [jnp.pad(r.reshape(-1), (0, D - r.size)) for r in rows]
    rows += [jnp.zeros((D,), F32)] * (8 - len(rows))
    return jnp.stack(rows)


def _full(name, g):
    if name in COL_SHARDED:
        return g.transpose(1, 0, 2).reshape(g.shape[1], -1)
    return g.reshape(-1, g.shape[2])


def _blocks(name, g, shard_shape):
    if name in COL_SHARDED:
        n = shard_shape[1]
        return g[:, :n * N_DEV].reshape(g.shape[0], N_DEV, n).transpose(1, 0, 2)
    return g.reshape((N_DEV,) + shard_shape)


def kernel(x, mem, g_mix, w_in, b_forget, w_out, g_xattn, g_mem, w_xq, w_xk, w_xv, w_xo, g_mlp, w_up, w_down, g_final, loss_target, m_g_mix, m_w_in, m_b_forget, m_w_out, m_g_xattn, m_g_mem, m_w_xq, m_w_xk, m_w_xv, m_w_xo, m_g_mlp, m_w_up, m_w_down, m_g_final, v_g_mix, v_w_in, v_b_forget, v_w_out, v_g_xattn, v_g_mem, v_w_xq, v_w_xk, v_w_xv, v_w_xo, v_g_mlp, v_w_up, v_w_down, v_g_final):
    W = dict(w_in=w_in, w_out=w_out, w_xq=w_xq, w_xk=w_xk, w_xv=w_xv, w_xo=w_xo, w_up=w_up, w_down=w_down)
    Mo = dict(w_in=m_w_in, w_out=m_w_out, w_xq=m_w_xq, w_xk=m_w_xk, w_xv=m_w_xv, w_xo=m_w_xo, w_up=m_w_up, w_down=m_w_down)
    Vo = dict(w_in=v_w_in, w_out=v_w_out, w_xq=v_w_xq, w_xk=v_w_xk, w_xv=v_w_xv, w_xo=v_w_xo, w_up=v_w_up, w_down=v_w_down)
    later = [n for n in W if n != "w_in"]

    first_handle, first_token = _exchange_start([w_in.astype(BF16)], ["gather"], name="gather_in_start")
    rest_handle, rest_token = _exchange_start([W[n].astype(BF16) + first_token[0, 0].astype(BF16) for n in later],
                                              ["gather"] * len(later), name="gather_rest_start")

    def get_w_in():
        (g,) = _exchange_wait(first_handle, rest_token, name="gather_in_wait")
        return jnp.pad(_full("w_in", g), ((0, 0), (0, IN_PAD - IN_W)))

    def get_rest(after):
        full = {n: _full(n, g) for n, g in zip(later, _exchange_wait(rest_handle, after, name="gather_rest_wait"))}
        full["w_kv"] = jnp.concatenate([full.pop("w_xk"), full.pop("w_xv")], axis=1)
        return full

    sent = []

    def send(grads):
        names = list(grads)
        handle, token = _exchange_start([_blocks(n, grads[n], W[n].shape) for n in names], ["scatter"] * len(names),
                                        name=f"scatter{len(sent)}_start")
        sent.append((names, handle))
        return token

    gx, small, loss = _local_step(x, mem, g_mix, b_forget, g_xattn, g_mem, g_mlp, g_final, loss_target, get_w_in, get_rest, send)

    received = {}
    for i, (names, handle) in enumerate(sent):
        received.update(zip(names, _exchange_wait(handle, gx, name=f"scatter{i}_wait")))
    packed = _pack_rows([small[n] for n in SMALL_ROWS] + [loss[0, :1]])
    (packed_all,) = _exchange([packed], ["gather"], name="gather_small")

    res = {n: _adamw(received[n], W[n], Mo[n], Vo[n], tr=128, name=f"adamw_{n}") for n in W}
    small_w = dict(g_mix=g_mix, b_forget=b_forget, g_xattn=g_xattn, g_mem=g_mem, g_mlp=g_mlp, g_final=g_final)
    small_m = dict(g_mix=m_g_mix, b_forget=m_b_forget, g_xattn=m_g_xattn, g_mem=m_g_mem, g_mlp=m_g_mlp, g_final=m_g_final)
    small_v = dict(g_mix=v_g_mix, b_forget=v_b_forget, g_xattn=v_g_xattn, g_mem=v_g_mem, g_mlp=v_g_mlp, g_final=v_g_final)
    sres = _adamw(packed_all, _pack_rows([small_w[n] for n in SMALL_ROWS]), _pack_rows([small_m[n] for n in SMALL_ROWS]),
                  _pack_rows([small_v[n] for n in SMALL_ROWS]), tr=8, name="adamw_small")
    for i, n in enumerate(SMALL_ROWS):
        res[n] = [r[i, :small_w[n].size] for r in sres]
    loss_total = sres[0][6, 0]

    order = ["g_mix", "w_in", "b_forget", "w_out", "g_xattn", "g_mem", "w_xq", "w_xk", "w_xv", "w_xo", "g_mlp", "w_up", "w_down", "g_final"]
    return (loss_total, gx, *[res[n][0] for n in order], *[res[n][1] for n in order],
            *[res[n][2] for n in order], *[res[n][3] for n in order])
```

```python
import jax
import jax.numpy as jnp
from jax import lax
from jax.experimental import pallas as pl
from jax.experimental.pallas import tpu as pltpu

F32, BF16 = jnp.float32, jnp.bfloat16
SDS = jax.ShapeDtypeStruct

D_MODEL = 1024
HEAD_DIM = 64
WIDTH = 512
QKV_W = 6 * WIDTH
IN_W = QKV_W + 8
IN_PAD = QKV_W + 128
BLOCK = 128
DIL_CONFIGS = ((128, 1), (512, 4), (2048, 16))
N_XH, XHD = 4, 256
D_FF = 4096
EPS = 1e-6
NEG = -1e30
N_DEV = 8
AXES = ("x", "y", "c")

ADAM_LR, ADAM_B1, ADAM_B2, ADAM_EPS, ADAM_WD, ADAM_STEP = 0.001, 0.9, 0.999, 1e-08, 0.01, 10

VMEM_CAP_V7X = 64 * 1024 * 1024
VMEM_LIMIT = VMEM_CAP_V7X * 7 // 8

ROWS = 512
ACC_ROWS = 2048

NT = (((1,), (1,)), ((), ()))
TN = (((0,), (0,)), ((), ()))


def _cp(**kw):
    return pltpu.CompilerParams(vmem_limit_bytes=VMEM_LIMIT, **kw)


def _dot(a, b, dims=None):
    if dims is None:
        return jnp.dot(a, b, preferred_element_type=F32)
    return lax.dot_general(a, b, dims, preferred_element_type=F32)


def _rstd(xv):
    return lax.rsqrt(jnp.mean(xv * xv, axis=-1, keepdims=True) + EPS)


def _rms_bwd(dh, xv, g):
    r = _rstd(xv)
    xhat = xv * r
    dxhat = dh * g
    dx = r * (dxhat - xhat * jnp.mean(dxhat * xhat, axis=-1, keepdims=True))
    return dx, jnp.sum(dh * xhat, axis=0, keepdims=True)


def _rms_matmul(x, g, w, *, tm, tn, out_dtype, relu=False, name):
    T, D = x.shape
    N = w.shape[1]

    def body(x_ref, g_ref, w_ref, h_ref, o_ref, h_s):
        @pl.when(pl.program_id(1) == 0)
        def _():
            xv = x_ref[...]
            h = (xv * _rstd(xv) * g_ref[...]).astype(BF16)
            h_s[...] = h
            h_ref[...] = h

        acc = _dot(h_s[...], w_ref[...])
        if relu:
            acc = jnp.maximum(acc, 0.0)
        o_ref[...] = acc.astype(out_dtype)

    return pl.pallas_call(
        body, grid=(T // tm, N // tn),
        in_specs=[pl.BlockSpec((tm, D), lambda i, j: (i, 0)), pl.BlockSpec((1, D), lambda i, j: (0, 0)),
                  pl.BlockSpec((D, tn), lambda i, j: (0, j))],
        out_specs=[pl.BlockSpec((tm, D), lambda i, j: (i, 0)), pl.BlockSpec((tm, tn), lambda i, j: (i, j))],
        out_shape=[SDS((T, D), BF16), SDS((T, N), out_dtype)],
        scratch_shapes=[pltpu.VMEM((tm, D), BF16)], compiler_params=_cp(), name=name,
    )(x, g.reshape(1, D), w)


def _matmul_nn(a, w, *, res=None, square=False, tm, tn, tk, out_dtype, name):
    T, K = a.shape
    N = w.shape[1]
    nk = K // tk

    def body(*refs):
        a_ref, w_ref = refs[0], refs[1]
        res_ref = refs[2] if res is not None else None
        o_ref = refs[3] if res is not None else refs[2]
        k = pl.program_id(2)
        av = a_ref[...]
        if square:
            af = av.astype(F32)
            av = (af * af).astype(BF16)
        part = _dot(av, w_ref[...])

        def finish(r):
            if res_ref is not None:
                r = res_ref[...] + r
            o_ref[...] = r.astype(out_dtype)

        if nk == 1:
            finish(part)
        else:
            acc = refs[-1]

            @pl.when(k == 0)
            def _():
                acc[...] = part

            @pl.when(k > 0)
            def _():
                acc[...] += part

            @pl.when(k == nk - 1)
            def _():
                finish(acc[...])

    in_specs = [pl.BlockSpec((tm, tk), lambda i, j, k: (i, k)), pl.BlockSpec((tk, tn), lambda i, j, k: (k, j))]
    args = [a, w]
    if res is not None:
        in_specs.append(pl.BlockSpec((tm, tn), lambda i, j, k: (i, j)))
        args.append(res)
    return pl.pallas_call(
        body, grid=(T // tm, N // tn, nk), in_specs=in_specs,
        out_specs=pl.BlockSpec((tm, tn), lambda i, j, k: (i, j)), out_shape=SDS((T, N), out_dtype),
        scratch_shapes=[pltpu.VMEM((tm, tn), F32)] if nk > 1 else [], compiler_params=_cp(), name=name,
    )(*args)


def _matmul_nt(g, w, *, mul2a=None, tm, tn, name):
    T, K = g.shape
    N = w.shape[0]

    def body(*refs):
        g_ref, w_ref = refs[0], refs[1]
        o_ref = refs[-1]
        acc = _dot(g_ref[...].astype(BF16), w_ref[...], NT)
        if mul2a is not None:
            acc = acc * (2.0 * refs[2][...].astype(F32))
        o_ref[...] = acc.astype(BF16)

    in_specs = [pl.BlockSpec((tm, K), lambda i, j: (i, 0)), pl.BlockSpec((tn, K), lambda i, j: (j, 0))]
    args = [g, w]
    if mul2a is not None:
        in_specs.append(pl.BlockSpec((tm, tn), lambda i, j: (i, j)))
        args.append(mul2a)
    return pl.pallas_call(
        body, grid=(T // tm, N // tn), in_specs=in_specs,
        out_specs=pl.BlockSpec((tm, tn), lambda i, j: (i, j)), out_shape=SDS((T, N), BF16),
        compiler_params=_cp(), name=name,
    )(*args)


def _matmul_nt_rms(g, w, x, gain, dres, *, tm, tk, name):
    T, K = g.shape
    D = w.shape[0]
    nk = K // tk
    nt = T // tm

    def body(*refs):
        g_ref, w_ref, x_ref, gain_ref = refs[:4]
        dres_ref = refs[4] if dres is not None else None
        n_in = 5 if dres is not None else 4
        dx_ref, dg_ref = refs[n_in], refs[n_in + 1]
        i, k = pl.program_id(0), pl.program_id(1)
        part = _dot(g_ref[...].astype(BF16), w_ref[...], NT)

        def finish(dh):
            dx, dg = _rms_bwd(dh, x_ref[...], gain_ref[...])
            if dres_ref is not None:
                dx = dres_ref[...] + dx
            dx_ref[...] = dx

            @pl.when(i == 0)
            def _():
                dg_ref[...] = dg

            @pl.when(i > 0)
            def _():
                dg_ref[...] += dg

        if nk == 1:
            finish(part)
        else:
            acc = refs[-1]

            @pl.when(k == 0)
            def _():
                acc[...] = part

            @pl.when(k > 0)
            def _():
                acc[...] += part

            @pl.when(k == nk - 1)
            def _():
                finish(acc[...])

    in_specs = [pl.BlockSpec((tm, tk), lambda i, k: (i, k)), pl.BlockSpec((D, tk), lambda i, k: (0, k)),
                pl.BlockSpec((tm, D), lambda i, k: (i, 0)), pl.BlockSpec((1, D), lambda i, k: (0, 0))]
    args = [g, w, x, gain.reshape(1, D)]
    if dres is not None:
        in_specs.append(pl.BlockSpec((tm, D), lambda i, k: (i, 0)))
        args.append(dres)
    return pl.pallas_call(
        body, grid=(nt, nk), in_specs=in_specs,
        out_specs=[pl.BlockSpec((tm, D), lambda i, k: (i, 0)), pl.BlockSpec((1, D), lambda i, k: (0, 0))],
        out_shape=[SDS((T, D), F32), SDS((1, D), F32)],
        scratch_shapes=[pltpu.VMEM((tm, D), F32)] if nk > 1 else [], compiler_params=_cp(), name=name,
    )(*args)


def _matmul_tn(a, g, *, square=False, bk, bn, tt, out_dtype, name):
    T, K = a.shape
    N = g.shape[1]
    nt = T // tt

    def body(a_ref, g_ref, o_ref, acc):
        t = pl.program_id(2)
        av = a_ref[...]
        if square:
            af = av.astype(F32)
            av = (af * af).astype(BF16)
        part = _dot(av, g_ref[...].astype(BF16), TN)
        if nt == 1:
            o_ref[...] = part.astype(out_dtype)
        else:
            @pl.when(t == 0)
            def _():
                acc[...] = part

            @pl.when((t > 0) & (t < nt - 1))
            def _():
                acc[...] += part

            @pl.when(t == nt - 1)
            def _():
                o_ref[...] = (acc[...] + part).astype(out_dtype)

    return pl.pallas_call(
        body, grid=(K // bk, N // bn, nt),
        in_specs=[pl.BlockSpec((tt, bk), lambda i, j, t: (t, i)), pl.BlockSpec((tt, bn), lambda i, j, t: (t, j))],
        out_specs=pl.BlockSpec((bk, bn), lambda i, j, t: (i, j)), out_shape=SDS((K, N), out_dtype),
        scratch_shapes=[pltpu.VMEM((bk, bn), F32)], compiler_params=_cp(), name=name,
    )(a, g)


def _loss_head(x3, g_final, target, *, tm, name):
    T, D = x3.shape

    def body(x_ref, g_ref, t_ref, dx_ref, dg_ref, loss_ref):
        i = pl.program_id(0)
        xv, g = x_ref[...], g_ref[...]
        r = _rstd(xv)
        xhat = xv * r
        diff = xhat * g - t_ref[...]
        part = 0.5 * jnp.sum(jnp.mean(diff * diff, axis=-1, keepdims=True), axis=0, keepdims=True)
        dy = diff * (1.0 / D)
        dxhat = dy * g
        dx_ref[...] = r * (dxhat - xhat * jnp.mean(dxhat * xhat, axis=-1, keepdims=True))
        dg = jnp.sum(dy * xhat, axis=0, keepdims=True)
        lp = jnp.broadcast_to(part, loss_ref.shape)

        @pl.when(i == 0)
        def _():
            dg_ref[...] = dg
            loss_ref[...] = lp

        @pl.when(i > 0)
        def _():
            dg_ref[...] += dg
            loss_ref[...] += lp

    return pl.pallas_call(
        body, grid=(T // tm,),
        in_specs=[pl.BlockSpec((tm, D), lambda i: (i, 0)), pl.BlockSpec((1, D), lambda i: (0, 0)),
                  pl.BlockSpec((tm, D), lambda i: (i, 0))],
        out_specs=[pl.BlockSpec((tm, D), lambda i: (i, 0)), pl.BlockSpec((1, D), lambda i: (0, 0)),
                   pl.BlockSpec((8, 128), lambda i: (0, 0))],
        out_shape=[SDS((T, D), F32), SDS((1, D), F32), SDS((8, 128), F32)],
        compiler_params=_cp(), name=name,
    )(x3, g_final.reshape(1, D), target)


def _head_lanes(shape, width):
    return lax.broadcasted_iota(jnp.int32, shape, len(shape) - 1) // width


def _gate_fwd(gate, b_pad, *, B, S, name):
    def body(g_ref, b_ref, cc_ref):
        xv = g_ref[...] + b_ref[...]
        lf = jnp.minimum(xv, 0.0) - jnp.log(1.0 + jnp.exp(-jnp.abs(xv)))
        lane = lax.broadcasted_iota(jnp.int32, lf.shape, 1)
        row = lax.broadcasted_iota(jnp.int32, lf.shape, 0)
        c = jnp.where(lane < 8, lf, 0.0)
        sh = 1
        while sh < S:
            c = c + jnp.where(row >= sh, pltpu.roll(c, sh, 0), 0.0)
            sh *= 2
        grp = _head_lanes((S, WIDTH), HEAD_DIM)
        cc = jnp.zeros((S, WIDTH), F32)
        for h in range(8):
            cc = jnp.where(grp == h, c[:, h:h + 1], cc)
        cc_ref[...] = cc

    return pl.pallas_call(
        body, grid=(B,),
        in_specs=[pl.BlockSpec((S, 128), lambda b: (b, 0)), pl.BlockSpec((1, 128), lambda b: (0, 0))],
        out_specs=pl.BlockSpec((S, WIDTH), lambda b: (b, 0)), out_shape=SDS((B * S, WIDTH), F32),
        compiler_params=_cp(), name=name,
    )(gate, b_pad)


def _gate_bwd(dcc, gate, b_pad, *, B, S, name):
    def body(dcc_ref, g_ref, b_ref, dg_ref, db_ref):
        bi = pl.program_id(0)
        dccv = dcc_ref[...]
        lane = lax.broadcasted_iota(jnp.int32, (S, 128), 1)
        row = lax.broadcasted_iota(jnp.int32, (S, 128), 0)
        dc = jnp.zeros((S, 128), F32)
        for h in range(8):
            dc = jnp.where(lane == h, dccv[:, HEAD_DIM * h:HEAD_DIM * h + 1], dc)
        sh = 1
        while sh < S:
            dc = dc + jnp.where(row < S - sh, pltpu.roll(dc, S - sh, 0), 0.0)
            sh *= 2
        xv = g_ref[...] + b_ref[...]
        dgate = jnp.where(lane < 8, dc / (1.0 + jnp.exp(xv)), 0.0)
        dg_ref[...] = dgate.astype(BF16)
        db = jnp.sum(dgate, axis=0, keepdims=True)

        @pl.when(bi == 0)
        def _():
            db_ref[...] = db

        @pl.when(bi > 0)
        def _():
            db_ref[...] += db

    return pl.pallas_call(
        body, grid=(B,),
        in_specs=[pl.BlockSpec((S, WIDTH), lambda b: (b, 0)), pl.BlockSpec((S, 128), lambda b: (b, 0)),
                  pl.BlockSpec((1, 128), lambda b: (0, 0))],
        out_specs=[pl.BlockSpec((S, 128), lambda b: (b, 0)), pl.BlockSpec((1, 128), lambda b: (0, 0))],
        out_shape=[SDS((B * S, 128), BF16), SDS((1, 128), F32)],
        compiler_params=_cp(), name=name,
    )(dcc, gate, b_pad)


_SMEM_SPEC = pl.BlockSpec(memory_space=pltpu.SMEM)


def _alibi_slopes():
    return 2.0 ** (-(jnp.arange(1, 9, dtype=F32) * (8.0 / 8)))


def _pair_masks():
    lane = lax.broadcasted_iota(jnp.int32, (1, 128), 1)
    first = lane < HEAD_DIM
    return (first.astype(BF16), (~first).astype(BF16)), first


BNT =(((2,), (2,)), ((0,), (0,)))
BNN = (((2,), (1,)), ((0,), (0,)))
BTN = (((1,), (1,)), ((0,), (0,)))


def _split3(v):
    hi = v.astype(BF16).astype(F32)
    mid = (v - hi).astype(BF16).astype(F32)
    lo = (v - hi - mid).astype(BF16).astype(F32)
    return [hi, mid, lo]


def _with_spare_lanes(base, e, cols):
    lane = lax.broadcasted_iota(jnp.int32, (1, 128), 1)
    off = HEAD_DIM * (1 - e)
    extra = jnp.zeros(base.shape, F32)
    for j, c in enumerate(cols):
        extra = jnp.where(lane == off + j, c, extra)
    return base + extra.astype(BF16)


ONES3 = [1.0, 1.0, 1.0]


def _band_bias(slope, dilation):
    qi = lax.broadcasted_iota(jnp.int32, (BLOCK, BLOCK), 0)
    kj = lax.broadcasted_iota(jnp.int32, (BLOCK, BLOCK), 1)
    cur = jnp.where(kj <= qi, (-slope * dilation) * (qi - kj).astype(F32), NEG)
    prev = jnp.where(kj >= qi, (-slope * dilation) * (qi + BLOCK - kj).astype(F32), NEG)
    return cur, prev


def _to_residue_major(dst, src_f32, dilation, nb, lead=0):
    L = nb * BLOCK
    for r in range(dilation):
        rows = src_f32[pl.ds(r, L, stride=dilation), :] if dilation > 1 else src_f32[...]
        dst[lead + r * nb:lead + (r + 1) * nb] = rows.reshape(nb, BLOCK, 128).astype(dst.dtype)


def _dil_attn_fwd(z, *, B, S, name):
    NB = S // BLOCK

    def body(slope_ref, q_ref, k_ref, v_ref, y_ref, lse_ref, qf, kf, vf, qd, kd, vd, od, ld, acc_o, acc_l):
        (m_first, m_second), first = _pair_masks()
        p = pl.program_id(1)
        qf[...] = q_ref[...].astype(F32)
        kf[...] = k_ref[...].astype(F32)
        vf[...] = v_ref[...].astype(F32)
        kd[0] = jnp.zeros((BLOCK, 128), BF16)
        vd[0] = jnp.zeros((BLOCK, 128), BF16)
        blk = lax.broadcasted_iota(jnp.int32, (NB, 1, 1), 0)

        for idx, (_, dilation) in enumerate(DIL_CONFIGS):
            nb = NB // dilation
            _to_residue_major(qd, qf, dilation, nb)
            _to_residue_major(kd, kf, dilation, nb, lead=1)
            _to_residue_major(vd, vf, dilation, nb, lead=1)
            q4, kc, vc = qd[...], kd[1:NB + 1], vd[1:NB + 1]
            outs, lses = [], []
            for e, hm in enumerate((m_first, m_second)):
                bias_cur, bias_prev = _band_bias(slope_ref[2 * p + e], dilation)
                qm = q4 * hm
                sc = _dot(qm, kc, BNT) * 0.125 + bias_cur
                m = jnp.max(sc, axis=2, keepdims=True)
                if nb > 1:
                    sp = _dot(qm, kd[0:NB], BNT) * 0.125 + jnp.where(blk % nb == 0, NEG, bias_prev)
                    m = jnp.maximum(m, jnp.max(sp, axis=2, keepdims=True))
                pc = jnp.exp(sc - m)
                l = jnp.sum(pc, axis=2, keepdims=True)
                o = _dot(pc.astype(BF16), vc, BNN)
                if nb > 1:
                    pp = jnp.exp(sp - m)
                    l = l + jnp.sum(pp, axis=2, keepdims=True)
                    o = o + _dot(pp.astype(BF16), vd[0:NB], BNN)
                outs.append(o * (1.0 / l))
                lses.append(m + jnp.log(l))
            od[...] = jnp.where(first, outs[0], outs[1])
            ld[...] = jnp.where(first, lses[0], lses[1])

            L = nb * BLOCK
            for r in range(dilation):
                rows = pl.ds(r, L, stride=dilation) if dilation > 1 else slice(None)
                o_new = od[r * nb:(r + 1) * nb].reshape(L, 128)
                l_new = ld[r * nb:(r + 1) * nb].reshape(L, 128)
                if idx == 0:
                    acc_o[rows, :] = o_new
                    acc_l[rows, :] = l_new
                else:
                    l_old = acc_l[rows, :]
                    m2 = jnp.maximum(l_old, l_new)
                    w_old, w_new = jnp.exp(l_old - m2), jnp.exp(l_new - m2)
                    tot = w_old + w_new
                    acc_o[rows, :] = (w_old * acc_o[rows, :] + w_new * o_new) * (1.0 / tot)
                    acc_l[rows, :] = m2 + jnp.log(tot)

        y_ref[...] = acc_o[...].astype(BF16)
        lse_ref[...] = acc_l[...]

    spec = lambda off: pl.BlockSpec((S, 128), lambda b, p: (b, 4 * off + p))
    ospec = pl.BlockSpec((S, 128), lambda b, p: (b, p))
    blocks = lambda n, dt: pltpu.VMEM((n, BLOCK, 128), dt)
    return pl.pallas_call(
        body, grid=(B, 4), in_specs=[_SMEM_SPEC, spec(0), spec(1), spec(2)], out_specs=[ospec, ospec],
        out_shape=[SDS((B * S, WIDTH), BF16), SDS((B * S, WIDTH), F32)],
        scratch_shapes=[pltpu.VMEM((S, 128), F32)] * 3 + [blocks(NB, BF16), blocks(NB + 1, BF16), blocks(NB + 1, BF16),
                                                         blocks(NB, F32), blocks(NB, F32)] + [pltpu.VMEM((S, 128), F32)] * 2,
        compiler_params=_cp(), name=name,
    )(_alibi_slopes(), z, z, z)


def _dil_attn_bwd(z, dy, ya, lse, *, B, S, name):
    NB = S // BLOCK

    def body(slope_ref, q_ref, k_ref, v_ref, do_ref, o_ref, lse_ref, dq_ref, dk_ref, dv_ref,
             qf, kf, vf, dof, ef, qd, dod, kd, vd, lsd, ed, dkd, dvd, dqa, dka, dva):
        (m_first, m_second), first = _pair_masks()
        p = pl.program_id(1)
        qf[...] = q_ref[...].astype(F32)
        kf[...] = k_ref[...].astype(F32)
        vf[...] = v_ref[...].astype(F32)
        dov = do_ref[...].astype(F32)
        dof[...] = dov
        prod = dov * o_ref[...].astype(F32)
        ef[...] = jnp.where(first, jnp.sum(jnp.where(first, prod, 0.0), axis=1, keepdims=True),
                            jnp.sum(jnp.where(first, 0.0, prod), axis=1, keepdims=True))
        kd[0] = jnp.zeros((BLOCK, 128), BF16)
        vd[0] = jnp.zeros((BLOCK, 128), BF16)
        blk = lax.broadcasted_iota(jnp.int32, (NB, 1, 1), 0)

        for idx, (_, dilation) in enumerate(DIL_CONFIGS):
            nb = NB // dilation
            _to_residue_major(qd, qf, dilation, nb)
            _to_residue_major(dod, dof, dilation, nb)
            _to_residue_major(kd, kf, dilation, nb, lead=1)
            _to_residue_major(vd, vf, dilation, nb, lead=1)
            _to_residue_major(lsd, lse_ref, dilation, nb)
            _to_residue_major(ed, ef, dilation, nb)
            q4, do4, kc, vc = qd[...], dod[...], kd[1:NB + 1], vd[1:NB + 1]
            dq4 = None
            dkc = dvc = dkp = dvp = None
            for e, hm in enumerate((m_first, m_second)):
                lane0 = slice(HEAD_DIM * e, HEAD_DIM * e + 1)
                bias_cur, bias_prev = _band_bias(slope_ref[2 * p + e], dilation)
                qm, dom = q4 * hm, do4 * hm
                lse_e, e_e = lsd[...][:, :, lane0], ed[...][:, :, lane0]
                pc = jnp.exp(_dot(qm, kc, BNT) * 0.125 + bias_cur - lse_e)
                dsc = (pc * (_dot(dom, vc, BNT) - e_e)).astype(BF16)
                pcb = pc.astype(BF16)
                dqe = _dot(dsc, kc, BNN)
                dkc = _dot(dsc, qm, BTN) if e == 0 else dkc + _dot(dsc, qm, BTN)
                dvc = _dot(pcb, dom, BTN) if e == 0 else dvc + _dot(pcb, dom, BTN)
                if nb > 1:
                    kp, vp = kd[0:NB], vd[0:NB]
                    pp = jnp.exp(_dot(qm, kp, BNT) * 0.125 + jnp.where(blk % nb == 0, NEG, bias_prev) - lse_e)
                    dsp = (pp * (_dot(dom, vp, BNT) - e_e)).astype(BF16)
                    ppb = pp.astype(BF16)
                    dqe = dqe + _dot(dsp, kp, BNN)
                    dkp = _dot(dsp, qm, BTN) if e == 0 else dkp + _dot(dsp, qm, BTN)
                    dvp = _dot(ppb, dom, BTN) if e == 0 else dvp + _dot(ppb, dom, BTN)
                dq4 = dqe if e == 0 else jnp.where(first, dq4, dqe)

            dkd[1:NB + 1] = dkc
            dvd[1:NB + 1] = dvc
            if nb > 1:
                dkd[1:NB] += dkp[1:NB]
                dvd[1:NB] += dvp[1:NB]
            L = nb * BLOCK
            for r in range(dilation):
                rows = pl.ds(r, L, stride=dilation) if dilation > 1 else slice(None)
                dq_r = dq4[r * nb:(r + 1) * nb].reshape(L, 128) * 0.125
                dk_r = dkd[1 + r * nb:1 + (r + 1) * nb].reshape(L, 128) * 0.125
                dv_r = dvd[1 + r * nb:1 + (r + 1) * nb].reshape(L, 128)
                if idx == 0:
                    dqa[rows, :], dka[rows, :], dva[rows, :] = dq_r, dk_r, dv_r
                else:
                    dqa[rows, :] += dq_r
                    dka[rows, :] += dk_r
                    dva[rows, :] += dv_r

        dq_ref[...] = dqa[...].astype(BF16)
        dk_ref[...] = dka[...].astype(BF16)
        dv_ref[...] = dva[...].astype(BF16)

    spec = lambda off: pl.BlockSpec((S, 128), lambda b, p: (b, 4 * off + p))
    ospec = pl.BlockSpec((S, 128), lambda b, p: (b, p))
    blocks = lambda n, dt: pltpu.VMEM((n, BLOCK, 128), dt)
    return pl.pallas_call(
        body, grid=(B, 4), in_specs=[_SMEM_SPEC, spec(0), spec(1), spec(2), ospec, ospec, ospec],
        out_specs=[ospec] * 3, out_shape=[SDS((B * S, WIDTH), BF16)] * 3,
        scratch_shapes=[pltpu.VMEM((S, 128), F32)] * 5
        + [blocks(NB, BF16), blocks(NB, BF16), blocks(NB + 1, BF16), blocks(NB + 1, BF16), blocks(NB, F32), blocks(NB, F32),
           blocks(NB + 1, F32), blocks(NB + 1, F32)] + [pltpu.VMEM((S, 128), F32)] * 3,
        compiler_params=_cp(), name=name,
    )(_alibi_slopes(), z, z, z, dy, ya, lse)


FOX_TQ = 256


def _fox_fwd(z, cc, *, B, S, name):
    def body(q_ref, k_ref, v_ref, cc_ref, o_ref, l_ref, qa, ka):
        (m_first, m_second), first = _pair_masks()
        ccv = cc_ref[...]
        eighth = jnp.asarray(0.125, BF16)
        for e, hm in enumerate((m_first, m_second)):
            c_e = jnp.broadcast_to(ccv[:, HEAD_DIM * e:HEAD_DIM * e + 1], (S, 128))
            qa[e] = _with_spare_lanes(q_ref[...] * hm * eighth, e, _split3(c_e) + ONES3)
            ka[e] = _with_spare_lanes(k_ref[...] * hm, e, ONES3 + _split3(-c_e))
        for qi in range(S // FOX_TQ):
            r0, kend = qi * FOX_TQ, (qi + 1) * FOX_TQ
            vv = v_ref[0:kend, :]
            row = lax.broadcasted_iota(jnp.int32, (FOX_TQ, kend), 0) + r0
            col = lax.broadcasted_iota(jnp.int32, (FOX_TQ, kend), 1)
            causal = col <= row
            outs, lses = [], []
            for e in (0, 1):
                s = jnp.where(causal, _dot(qa[e, r0:kend, :], ka[e, 0:kend, :], NT), NEG)
                m = jnp.max(s, axis=1, keepdims=True)
                pe = jnp.exp(s - m)
                l = jnp.sum(pe, axis=1, keepdims=True)
                outs.append(_dot(pe.astype(BF16), vv) * (1.0 / l))
                lses.append(m + jnp.log(l))
            o_ref[r0:kend, :] = jnp.where(first, outs[0], outs[1]).astype(BF16)
            l_ref[r0:kend, :] = jnp.where(first, lses[0], lses[1])

    spec = lambda off: pl.BlockSpec((S, 128), lambda b, p: (b, 4 * off + p))
    pspec = pl.BlockSpec((S, 128), lambda b, p: (b, p))
    return pl.pallas_call(
        body, grid=(B, 4), in_specs=[spec(3), spec(4), spec(5), pspec], out_specs=[pspec, pspec],
        out_shape=[SDS((B * S, WIDTH), BF16), SDS((B * S, WIDTH), F32)],
        scratch_shapes=[pltpu.VMEM((2, S, 128), BF16)] * 2, compiler_params=_cp(), name=name,
    )(z, z, z, cc)


def _fox_bwd(z, dy, lse, cc, *, B, S, name):
    def body(q_ref, k_ref, v_ref, do_ref, lse_ref, cc_ref, dq_ref, dk_ref, dv_ref, dc_ref,
             qa, ka, qp, kp, vp, dp, dk_s, dv_s, dc_s):
        (m_first, m_second), first = _pair_masks()
        ccv, lsev = cc_ref[...], lse_ref[...]
        eighth = jnp.asarray(0.125, BF16)
        for e, hm in enumerate((m_first, m_second)):
            lane0 = slice(HEAD_DIM * e, HEAD_DIM * e + 1)
            c_e = jnp.broadcast_to(ccv[:, lane0], (S, 128))
            lse_e = jnp.broadcast_to(lsev[:, lane0], (S, 128))
            qp[e] = q_ref[...] * hm
            kp[e] = k_ref[...] * hm
            dp[e] = do_ref[...] * hm
            qa[e] = _with_spare_lanes(qp[e] * eighth, e, _split3(c_e - lse_e) + ONES3)
            ka[e] = _with_spare_lanes(kp[e], e, ONES3 + _split3(-c_e))
            vp[e] = v_ref[...] * hm
        dk_s[...] = jnp.zeros_like(dk_s)
        dv_s[...] = jnp.zeros_like(dv_s)
        dc_s[...] = jnp.zeros_like(dc_s)
        for qi in range(S // FOX_TQ):
            r0, kend = qi * FOX_TQ, (qi + 1) * FOX_TQ
            krow = lax.broadcasted_iota(jnp.int32, (kend, FOX_TQ), 0)
            qcol = lax.broadcasted_iota(jnp.int32, (kend, FOX_TQ), 1) + r0
            causal = krow <= qcol
            dq_t = jnp.zeros((FOX_TQ, 128), F32)
            for e in (0, 1):
                sel = first if e == 0 else ~first
                pt = jnp.where(causal, jnp.exp(_dot(ka[e, 0:kend, :], qa[e, r0:kend, :], NT)), 0.0)
                dpt = _dot(vp[e, 0:kend, :], dp[e, r0:kend, :], NT)
                mean = jnp.sum(pt * dpt, axis=0, keepdims=True) / jnp.sum(pt, axis=0, keepdims=True)
                dst = pt * (dpt - mean)
                dsb = dst.astype(BF16)
                dv_s[0:kend, :] += _dot(pt.astype(BF16), dp[e, r0:kend, :])
                dk_s[0:kend, :] += _dot(dsb, qp[e, r0:kend, :]) * 0.125
                dq_t = dq_t + _dot(dsb, kp[e, 0:kend, :], TN)
                dc_s[0:kend, :] += jnp.where(sel, -jnp.sum(dst, axis=1, keepdims=True), 0.0)
            dq_ref[r0:kend, :] = (dq_t * 0.125).astype(BF16)
        dk_ref[...] = dk_s[...].astype(BF16)
        dv_ref[...] = dv_s[...].astype(BF16)
        dc_ref[...] = dc_s[...]

    spec = lambda off: pl.BlockSpec((S, 128), lambda b, p: (b, 4 * off + p))
    pspec = pl.BlockSpec((S, 128), lambda b, p: (b, p))
    return pl.pallas_call(
        body, grid=(B, 4),
        in_specs=[spec(3), spec(4), spec(5), pl.BlockSpec((S, 128), lambda b, p: (b, 4 + p)), pspec, pspec],
        out_specs=[pspec] * 4,
        out_shape=[SDS((B * S, WIDTH), BF16)] * 3 + [SDS((B * S, WIDTH), F32)],
        scratch_shapes=[pltpu.VMEM((2, S, 128), BF16)] * 6 + [pltpu.VMEM((S, 128), F32)] * 3,
        compiler_params=_cp(), name=name,
    )(z, z, z, dy, lse, cc)


def _xattn_fwd(q, kv, *, B, S, M, tq, name):
    D = D_MODEL

    def body(q_ref, kv_ref, o_ref):
        for h in range(N_XH):
            cs = slice(XHD * h, XHD * (h + 1))
            s = _dot(q_ref[:, cs], kv_ref[:, cs], NT) * (1.0 / 16.0)
            pe = jnp.exp(s - jnp.max(s, axis=1, keepdims=True))
            l = jnp.sum(pe, axis=1, keepdims=True)
            o_ref[:, cs] = (_dot(pe.astype(BF16), kv_ref[:, D + XHD * h:D + XHD * (h + 1)]) * (1.0 / l)).astype(BF16)

    nq = S // tq
    return pl.pallas_call(
        body, grid=(B, nq),
        in_specs=[pl.BlockSpec((tq, D), lambda b, t: (b * nq + t, 0)), pl.BlockSpec((M, 2 * D), lambda b, t: (b, 0))],
        out_specs=pl.BlockSpec((tq, D), lambda b, t: (b * nq + t, 0)), out_shape=SDS((B * S, D), BF16),
        compiler_params=_cp(), name=name,
    )(q, kv)


def _xattn_bwd(q, kv, do, *, B, S, M, tq, name):
    D = D_MODEL

    def body(q_ref, kv_ref, do_ref, dq_ref, dkv_ref):
        t = pl.program_id(1)

        @pl.when(t == 0)
        def _():
            dkv_ref[...] = jnp.zeros_like(dkv_ref)

        for h in range(N_XH):
            cs = slice(XHD * h, XHD * (h + 1))
            vs = slice(D + XHD * h, D + XHD * (h + 1))
            qh, kh, vh, doh = q_ref[:, cs], kv_ref[:, cs], kv_ref[:, vs], do_ref[:, cs]
            s = _dot(qh, kh, NT) * (1.0 / 16.0)
            pe = jnp.exp(s - jnp.max(s, axis=1, keepdims=True))
            pe = pe * (1.0 / jnp.sum(pe, axis=1, keepdims=True))
            dp = _dot(doh, vh, NT)
            ds = (pe * (dp - jnp.sum(pe * dp, axis=1, keepdims=True))).astype(BF16)
            dq_ref[:, cs] = (_dot(ds, kh) * (1.0 / 16.0)).astype(BF16)
            dkv_ref[:, cs] += _dot(ds, qh, TN) * (1.0 / 16.0)
            dkv_ref[:, vs] += _dot(pe.astype(BF16), doh, TN)

    nq = S // tq
    qspec = pl.BlockSpec((tq, D), lambda b, t: (b * nq + t, 0))
    kvspec = pl.BlockSpec((M, 2 * D), lambda b, t: (b, 0))
    return pl.pallas_call(
        body, grid=(B, nq), in_specs=[qspec, kvspec, qspec], out_specs=[qspec, kvspec],
        out_shape=[SDS((B * S, D), BF16), SDS((B * M, 2 * D), F32)], compiler_params=_cp(), name=name,
    )(q, kv, do)


def _assemble_dz(parts, dgate, *, tm, name):
    T = dgate.shape[0]

    def body(*refs):
        o_ref = refs[-1]
        for j in range(6):
            o_ref[:, WIDTH * j:WIDTH * (j + 1)] = refs[j][...]
        o_ref[:, QKV_W:IN_PAD] = refs[6][...]

    wspec = pl.BlockSpec((tm, WIDTH), lambda i: (i, 0))
    return pl.pallas_call(
        body, grid=(T // tm,), in_specs=[wspec] * 6 + [pl.BlockSpec((tm, 128), lambda i: (i, 0))],
        out_specs=pl.BlockSpec((tm, IN_PAD), lambda i: (i, 0)), out_shape=SDS((T, IN_PAD), BF16),
        compiler_params=_cp(), name=name,
    )(*parts, dgate)


def _adamw(parts, w, m, v, *, tr, name):
    R, C = w.shape

    def body(p_ref, w_ref, m_ref, v_ref, g_ref, d_ref, nm_ref, nv_ref):
        g = p_ref[0].astype(F32)
        for d in range(1, N_DEV):
            g = g + p_ref[d].astype(F32)
        m2 = ADAM_B1 * m_ref[...] + (1.0 - ADAM_B1) * g
        v2 = ADAM_B2 * v_ref[...] + (1.0 - ADAM_B2) * (g * g)
        m_hat = m2 / (1.0 - ADAM_B1 ** ADAM_STEP)
        v_hat = v2 / (1.0 - ADAM_B2 ** ADAM_STEP)
        g_ref[...] = g
        d_ref[...] = -ADAM_LR * (m_hat / (jnp.sqrt(v_hat) + ADAM_EPS) + ADAM_WD * w_ref[...])
        nm_ref[...] = m2
        nv_ref[...] = v2

    spec = pl.BlockSpec((tr, C), lambda i: (i, 0))
    return pl.pallas_call(
        body, grid=(R // tr,), in_specs=[pl.BlockSpec((N_DEV, tr, C), lambda i: (0, i, 0)), spec, spec, spec],
        out_specs=[spec] * 4, out_shape=[SDS((R, C), F32)] * 4, compiler_params=_cp(), name=name,
    )(parts, w, m, v)


def _peer(k, x, y, c):
    return (1 - x if k & 4 else x, 1 - y if k & 2 else y, 1 - c if k & 1 else c)


_HBM_SPEC = pl.BlockSpec(memory_space=pltpu.HBM)
_SEM_SPEC = pl.BlockSpec(memory_space=pltpu.SEMAPHORE)
_SPLIT_EFFECT = pltpu.SideEffectType.DATAFLOW_SIDE_EFFECTING


def _split_copies(srcs, lands, send_sems, recv_sems, modes):
    x, y, c = (lax.axis_index(a) for a in AXES)
    me = 4 * x + 2 * y + c
    copies = []
    for i, md in enumerate(modes):
        for k in range(1, N_DEV):
            px, py, pc = _peer(k, x, y, c)
            src = srcs[i] if md == "gather" else srcs[i].at[4 * px + 2 * py + pc]
            j = i * (N_DEV - 1) + k - 1
            copies.append(pltpu.make_async_remote_copy(
                src_ref=src, dst_ref=lands[i].at[me], send_sem=send_sems.at[j], recv_sem=recv_sems.at[j],
                device_id=(px, py, pc), device_id_type=pl.DeviceIdType.MESH))
    return copies


def _exchange_start(arrays, modes, *, name):
    n = len(arrays)
    hbm = lambda a: pltpu.with_memory_space_constraint(a, pltpu.HBM)
    srcs = [hbm(a) for a in arrays]
    me = 4 * lax.axis_index("x") + 2 * lax.axis_index("y") + lax.axis_index("c")

    def landing(a, md):
        own = a[None] if md == "gather" else lax.dynamic_index_in_dim(a, me, 0, keepdims=True)
        return hbm(lax.dynamic_update_index_in_dim(lax.empty((N_DEV,) + own.shape[1:], a.dtype), own, me, 0))

    lands = [landing(a, md) for a, md in zip(arrays, modes)]

    def body(*refs):
        for cp in _split_copies(refs[:n], refs[n:2 * n], refs[2 * n], refs[2 * n + 1], modes):
            cp.start()
        token = refs[-1]
        token[...] = jnp.zeros_like(token)

    sems = pltpu.SemaphoreType.DMA((n * (N_DEV - 1),))
    outs = pl.pallas_call(
        body, name=name, in_specs=[_HBM_SPEC] * (2 * n),
        out_shape=(sems, sems, *[pltpu.HBM(a.shape, a.dtype) for a in srcs + lands], SDS((8, 128), F32)),
        out_specs=(_SEM_SPEC, _SEM_SPEC, *[_HBM_SPEC] * (2 * n), pl.BlockSpec(memory_space=pltpu.VMEM)),
        input_output_aliases={i: 2 + i for i in range(2 * n)},
        compiler_params=pltpu.CompilerParams(has_side_effects=_SPLIT_EFFECT),
    )(*srcs, *lands)
    return (outs[0], outs[1], outs[2:2 + n], outs[2 + n:2 + 2 * n], modes), outs[-1]


def _exchange_wait(handle, after, *, name):
    send_sems, recv_sems, srcs, lands, modes = handle
    n = len(srcs)

    def body(*refs):
        for cp in _split_copies(refs[:n], refs[n:2 * n], refs[2 * n], refs[2 * n + 1], modes):
            cp.wait_send()
            cp.wait_recv()

    outs = pl.pallas_call(
        body, name=name, in_specs=[_HBM_SPEC] * (2 * n) + [_SEM_SPEC, _SEM_SPEC, pl.BlockSpec(memory_space=pl.ANY)],
        out_shape=tuple(pltpu.HBM(a.shape, a.dtype) for a in list(srcs) + list(lands)), out_specs=tuple([_HBM_SPEC] * (2 * n)),
        input_output_aliases={i: i for i in range(2 * n)},
        compiler_params=pltpu.CompilerParams(has_side_effects=_SPLIT_EFFECT),
    )(*srcs, *lands, send_sems, recv_sems, after)
    return list(outs[n:])


def _exchange(arrays, modes, *, name):
    n = len(arrays)
    out_shape = [SDS((N_DEV,) + a.shape if md == "gather" else a.shape, a.dtype) for a, md in zip(arrays, modes)]

    def body(*refs):
        ins, outs = refs[:n], refs[n:2 * n]
        send_sems, recv_sems, local_sems = refs[2 * n:]
        x, y, c = (lax.axis_index(a) for a in AXES)
        me = 4 * x + 2 * y + c
        copies = []
        for i, md in enumerate(modes):
            src = ins[i] if md == "gather" else ins[i].at[me]
            cp = pltpu.make_async_copy(src, outs[i].at[me], local_sems.at[i])
            cp.start()
            copies.append(cp)
            for k in range(1, N_DEV):
                px, py, pc = _peer(k, x, y, c)
                src = ins[i] if md == "gather" else ins[i].at[4 * px + 2 * py + pc]
                cp = pltpu.make_async_remote_copy(
                    src_ref=src, dst_ref=outs[i].at[me], send_sem=send_sems.at[i, k - 1], recv_sem=recv_sems.at[i, k - 1],
                    device_id=(px, py, pc), device_id_type=pl.DeviceIdType.MESH)
                cp.start()
                copies.append(cp)
        for cp in copies:
            cp.wait()

    anyspec = pl.BlockSpec(memory_space=pl.ANY)
    return pl.pallas_call(
        body, in_specs=[anyspec] * n, out_specs=[anyspec] * n, out_shape=out_shape,
        scratch_shapes=[pltpu.SemaphoreType.DMA((n, N_DEV - 1)), pltpu.SemaphoreType.DMA((n, N_DEV - 1)),
                        pltpu.SemaphoreType.DMA((n,))],
        name=name,
    )(*arrays)


def _local_step(x, mem, g_mix, b_forget, g_xattn, g_mem, g_mlp, g_final, target, get_w_in, get_rest, send):
    B, S, D = x.shape
    M = mem.shape[1]
    T = B * S
    x0 = x.reshape(T, D)
    mem2 = mem.reshape(B * M, D)
    tgt = target.reshape(T, D)
    b_pad = jnp.pad(b_forget, (0, 120)).reshape(1, 128)
    after = lambda a, tok: a if tok is None else a + tok[0, 0]

    w_in_pad = get_w_in()
    h1, z = _rms_matmul(x0, g_mix, w_in_pad[:, :QKV_W], tm=ROWS, tn=QKV_W, out_dtype=BF16, name="f_in")
    gate = _matmul_nn(h1, w_in_pad[:, QKV_W:], tm=1024, tn=128, tk=D, out_dtype=F32, name="f_gate")
    cc = _gate_fwd(gate, b_pad, B=B, S=S, name="f_gatecum")
    ya, lse = _dil_attn_fwd(z, B=B, S=S, name="f_dil")
    yf, lse_f = _fox_fwd(z, cc, B=B, S=S, name="f_fox")
    ymix = jnp.concatenate([ya, yf], axis=1)
    w = get_rest(ymix)
    x1 = _matmul_nn(ymix, w["w_out"], res=x0, tm=ROWS, tn=D, tk=D, out_dtype=F32, name="f_out")
    h2, q = _rms_matmul(x1, g_xattn, w["w_xq"], tm=ROWS, tn=D, out_dtype=BF16, name="f_xq")
    mn, kv = _rms_matmul(mem2, g_mem, w["w_kv"], tm=B * M, tn=D, out_dtype=BF16, name="f_xkv")
    xo = _xattn_fwd(q, kv, B=B, S=S, M=M, tq=512, name="f_xattn")
    x2 = _matmul_nn(xo, w["w_xo"], res=x1, tm=ROWS, tn=D, tk=D, out_dtype=F32, name="f_xo")
    h3, act = _rms_matmul(x2, g_mlp, w["w_up"], tm=ROWS, tn=D_FF, out_dtype=BF16, relu=True, name="f_up")
    x3 = _matmul_nn(act, w["w_down"], res=x2, square=True, tm=ROWS, tn=D, tk=D_FF, out_dtype=F32, name="f_down")
    dx3, dg_final, loss = _loss_head(x3, g_final, tgt, tm=512, name="f_loss")

    du = _matmul_nt(dx3, w["w_down"], mul2a=act, tm=ROWS, tn=D_FF, name="b_dact")
    dw_down = _matmul_tn(act, dx3, square=True, bk=1024, bn=D, tt=ACC_ROWS, out_dtype=BF16, name="b_wdown")
    dw_up = _matmul_tn(h3, du, bk=D, bn=1024, tt=ACC_ROWS, out_dtype=BF16, name="b_wup")
    tok = send(dict(w_down=dw_down, w_up=dw_up))
    dx2, dg_mlp = _matmul_nt_rms(du, w["w_up"], x2, after(g_mlp, tok), dx3, tm=ROWS, tk=D_FF, name="b_dh3")
    dxo = _matmul_nt(dx2, w["w_xo"], tm=1024, tn=D, name="b_dxo")
    dw_xo = _matmul_tn(xo, dx2, bk=D, bn=D, tt=ACC_ROWS, out_dtype=BF16, name="b_wxo")
    dq, dkv = _xattn_bwd(q, kv, dxo, B=B, S=S, M=M, tq=512, name="b_xattn")
    dw_xq = _matmul_tn(h2, dq, bk=D, bn=D, tt=ACC_ROWS, out_dtype=BF16, name="b_wxq")
    dx1, dg_xattn = _matmul_nt_rms(dq, w["w_xq"], x1, g_xattn, dx2, tm=ROWS, tk=D, name="b_dh2")
    dw_kv = _matmul_tn(mn, dkv, bk=D, bn=D, tt=B * M, out_dtype=BF16, name="b_wkv")
    _, dg_mem = _matmul_nt_rms(dkv, w["w_kv"], mem2, g_mem, None, tm=min(ROWS, B * M), tk=2 * D, name="b_dmem")
    dy = _matmul_nt(dx1, w["w_out"], tm=1024, tn=D, name="b_dy")
    dw_out = _matmul_tn(ymix, dx1, bk=D, bn=D, tt=ACC_ROWS, out_dtype=BF16, name="b_wout")
    tok = send(dict(w_xo=dw_xo, w_xq=dw_xq, w_xk=dw_kv[:, :D], w_xv=dw_kv[:, D:], w_out=dw_out))
    dqf, dkf, dvf, dcc = _fox_bwd(z, dy, lse_f, cc, B=B, S=S, name="b_fox")
    dgate, db = _gate_bwd(dcc, gate, after(b_pad, tok), B=B, S=S, name="b_gate")
    dqa, dka, dva = _dil_attn_bwd(z, dy, ya, lse, B=B, S=S, name="b_dil")
    dz = _assemble_dz([dqa, dka, dva, dqf, dkf, dvf], dgate, tm=512, name="b_dz")
    dw_in = _matmul_tn(h1, dz, bk=D, bn=640, tt=ACC_ROWS, out_dtype=BF16, name="b_win")
    tok = send(dict(w_in=dw_in))
    gx, dg_mix = _matmul_nt_rms(dz, w_in_pad, x0, after(g_mix, tok), dx1, tm=ROWS, tk=IN_PAD, name="b_dh1")

    small = dict(g_mix=dg_mix, b_forget=db, g_xattn=dg_xattn, g_mem=dg_mem, g_mlp=dg_mlp, g_final=dg_final)
    return gx.reshape(B, S, D), small, loss


SMALL_ROWS = ("g_mix", "b_forget", "g_xattn", "g_mem", "g_mlp", "g_final")
COL_SHARDED = ("w_in", "w_up")


def _pack_rows(rows):
    D = D_MODEL
    rows = [jnp.pad(r.reshape(-1), (0, D - r.size)) for r in rows]
    rows += [jnp.zeros((D,), F32)] * (8 - len(rows))
    return jnp.stack(rows)


def _full(name, g):
    if name in COL_SHARDED:
        return g.transpose(1, 0, 2).reshape(g.shape[1], -1)
    return g.reshape(-1, g.shape[2])


def _blocks(name, g, shard_shape):
    if name in COL_SHARDED:
        n = shard_shape[1]
        return g[:, :n * N_DEV].reshape(g.shape[0], N_DEV, n).transpose(1, 0, 2)
    return g.reshape((N_DEV,) + shard_shape)


def kernel(x, mem, g_mix, w_in, b_forget, w_out, g_xattn, g_mem, w_xq, w_xk, w_xv, w_xo, g_mlp, w_up, w_down, g_final, loss_target, m_g_mix, m_w_in, m_b_forget, m_w_out, m_g_xattn, m_g_mem, m_w_xq, m_w_xk, m_w_xv, m_w_xo, m_g_mlp, m_w_up, m_w_down, m_g_final, v_g_mix, v_w_in, v_b_forget, v_w_out, v_g_xattn, v_g_mem, v_w_xq, v_w_xk, v_w_xv, v_w_xo, v_g_mlp, v_w_up, v_w_down, v_g_final):
    W = dict(w_in=w_in, w_out=w_out, w_xq=w_xq, w_xk=w_xk, w_xv=w_xv, w_xo=w_xo, w_up=w_up, w_down=w_down)
    Mo = dict(w_in=m_w_in, w_out=m_w_out, w_xq=m_w_xq, w_xk=m_w_xk, w_xv=m_w_xv, w_xo=m_w_xo, w_up=m_w_up, w_down=m_w_down)
    Vo = dict(w_in=v_w_in, w_out=v_w_out, w_xq=v_w_xq, w_xk=v_w_xk, w_xv=v_w_xv, w_xo=v_w_xo, w_up=v_w_up, w_down=v_w_down)
    later = [n for n in W if n != "w_in"]

    first_handle, first_token = _exchange_start([w_in.astype(BF16)], ["gather"], name="gather_in_start")
    rest_handle, rest_token = _exchange_start([W[n].astype(BF16) + first_token[0, 0].astype(BF16) for n in later],
                                              ["gather"] * len(later), name="gather_rest_start")

    def get_w_in():
        (g,) = _exchange_wait(first_handle, rest_token, name="gather_in_wait")
        return jnp.pad(_full("w_in", g), ((0, 0), (0, IN_PAD - IN_W)))

    def get_rest(after):
        full = {n: _full(n, g) for n, g in zip(later, _exchange_wait(rest_handle, after, name="gather_rest_wait"))}
        full["w_kv"] = jnp.concatenate([full.pop("w_xk"), full.pop("w_xv")], axis=1)
        return full

    sent = []

    def send(grads):
        names = list(grads)
        handle, token = _exchange_start([_blocks(n, grads[n], W[n].shape) for n in names], ["scatter"] * len(names),
                                        name=f"scatter{len(sent)}_start")
        sent.append((names, handle))
        return token

    gx, small, loss = _local_step(x, mem, g_mix, b_forget, g_xattn, g_mem, g_mlp, g_final, loss_target, get_w_in, get_rest, send)

    received = {}
    for i, (names, handle) in enumerate(sent):
        received.update(zip(names, _exchange_wait(handle, gx, name=f"scatter{i}_wait")))
    packed = _pack_rows([small[n] for n in SMALL_ROWS] + [loss[0, :1]])
    (packed_all,) = _exchange([packed], ["gather"], name="gather_small")

    rows_per_step = lambda shape: max(t for t in (128, 256, 512) if shape[0] % t == 0 and t * shape[1] <= 512 * 512)
    res = {n: _adamw(received[n], W[n], Mo[n], Vo[n], tr=rows_per_step(W[n].shape), name=f"adamw_{n}") for n in W}
    small_w = dict(g_mix=g_mix, b_forget=b_forget, g_xattn=g_xattn, g_mem=g_mem, g_mlp=g_mlp, g_final=g_final)
    small_m = dict(g_mix=m_g_mix, b_forget=m_b_forget, g_xattn=m_g_xattn, g_mem=m_g_mem, g_mlp=m_g_mlp, g_final=m_g_final)
    small_v = dict(g_mix=v_g_mix, b_forget=v_b_forget, g_xattn=v_g_xattn, g_mem=v_g_mem, g_mlp=v_g_mlp, g_final=v_g_final)
    sres = _adamw(packed_all, _pack_rows([small_w[n] for n in SMALL_ROWS]), _pack_rows([small_m[n] for n in SMALL_ROWS]),
                  _pack_rows([small_v[n] for n in SMALL_ROWS]), tr=8, name="adamw_small")
    for i, n in enumerate(SMALL_ROWS):
        res[n] = [r[i, :small_w[n].size] for r in sres]
    loss_total = sres[0][6, 0]

    order = ["g_mix", "w_in", "b_forget", "w_out", "g_xattn", "g_mem", "w_xq", "w_xk", "w_xv", "w_xo", "g_mlp", "w_up", "w_down", "g_final"]
    return (loss_total, gx, *[res[n][0] for n in order], *[res[n][1] for n in order],
            *[res[n][2] for n in order], *[res[n][3] for n in order])
```

```python
import jax
import jax.numpy as jnp
from jax import lax
from jax.experimental import pallas as pl
from jax.experimental.pallas import tpu as pltpu

F32, BF16 = jnp.float32, jnp.bfloat16
SDS = jax.ShapeDtypeStruct

D_MODEL = 1024
HEAD_DIM = 64
WIDTH = 512
QKV_W = 6 * WIDTH
IN_W = QKV_W + 8
IN_PAD = QKV_W + 128
BLOCK = 128
DIL_CONFIGS = ((128, 1), (512, 4), (2048, 16))
N_XH, XHD = 4, 256
D_FF = 4096
EPS = 1e-6
NEG = -1e30
N_DEV = 8
AXES = ("x", "y", "c")

ADAM_LR, ADAM_B1, ADAM_B2, ADAM_EPS, ADAM_WD, ADAM_STEP = 0.001, 0.9, 0.999, 1e-08, 0.01, 10

VMEM_CAP_V7X = 64 * 1024 * 1024
VMEM_LIMIT = VMEM_CAP_V7X * 7 // 8

ROWS = 512
ACC_ROWS = 2048

NT = (((1,), (1,)), ((), ()))
TN = (((0,), (0,)), ((), ()))


def _cp(**kw):
    return pltpu.CompilerParams(vmem_limit_bytes=VMEM_LIMIT, **kw)


def _dot(a, b, dims=None):
    if dims is None:
        return jnp.dot(a, b, preferred_element_type=F32)
    return lax.dot_general(a, b, dims, preferred_element_type=F32)


def _rstd(xv):
    return lax.rsqrt(jnp.mean(xv * xv, axis=-1, keepdims=True) + EPS)


def _rms_bwd(dh, xv, g):
    r = _rstd(xv)
    xhat = xv * r
    dxhat = dh * g
    dx = r * (dxhat - xhat * jnp.mean(dxhat * xhat, axis=-1, keepdims=True))
    return dx, jnp.sum(dh * xhat, axis=0, keepdims=True)


def _rms_matmul(x, g, w, *, tm, tn, out_dtype, relu=False, w_f32=None, name):
    T, D = x.shape
    N = w.shape[1]

    def body(*refs):
        x_ref, g_ref, w_ref = refs[:3]
        h_ref, o_ref, h_s = refs[-3 - (w_f32 is not None)], refs[-2 - (w_f32 is not None)], refs[-1]

        @pl.when(pl.program_id(1) == 0)
        def _():
            xv = x_ref[...]
            h = (xv * _rstd(xv) * g_ref[...]).astype(BF16)
            h_s[...] = h
            h_ref[...] = h
            if w_f32 is not None:
                refs[-2][...] = _dot(h, refs[3][...])

        acc = _dot(h_s[...], w_ref[...])
        if relu:
            acc = jnp.maximum(acc, 0.0)
        o_ref[...] = acc.astype(out_dtype)

    in_specs = [pl.BlockSpec((tm, D), lambda i, j: (i, 0)), pl.BlockSpec((1, D), lambda i, j: (0, 0)),
                pl.BlockSpec((D, tn), lambda i, j: (0, j))]
    out_specs = [pl.BlockSpec((tm, D), lambda i, j: (i, 0)), pl.BlockSpec((tm, tn), lambda i, j: (i, j))]
    out_shape = [SDS((T, D), BF16), SDS((T, N), out_dtype)]
    args = [x, g.reshape(1, D), w]
    if w_f32 is not None:
        n2 = w_f32.shape[1]
        in_specs.append(pl.BlockSpec((D, n2), lambda i, j: (0, 0)))
        out_specs.append(pl.BlockSpec((tm, n2), lambda i, j: (i, 0)))
        out_shape.append(SDS((T, n2), F32))
        args.append(w_f32)
    return pl.pallas_call(
        body, grid=(T // tm, N // tn), in_specs=in_specs, out_specs=out_specs, out_shape=out_shape,
        scratch_shapes=[pltpu.VMEM((tm, D), BF16)], compiler_params=_cp(), name=name,
    )(*args)


def _matmul_nn(a, w, *, res=None, square=False, tm, tn, tk, out_dtype, name):
    T, K = a.shape
    N = w.shape[1]
    nk = K // tk

    def body(*refs):
        a_ref, w_ref = refs[0], refs[1]
        res_ref = refs[2] if res is not None else None
        o_ref = refs[3] if res is not None else refs[2]
        k = pl.program_id(2)
        av = a_ref[...]
        if square:
            af = av.astype(F32)
            av = (af * af).astype(BF16)
        part = _dot(av, w_ref[...])

        def finish(r):
            if res_ref is not None:
                r = res_ref[...] + r
            o_ref[...] = r.astype(out_dtype)

        if nk == 1:
            finish(part)
        else:
            acc = refs[-1]

            @pl.when(k == 0)
            def _():
                acc[...] = part

            @pl.when(k > 0)
            def _():
                acc[...] += part

            @pl.when(k == nk - 1)
            def _():
                finish(acc[...])

    in_specs = [pl.BlockSpec((tm, tk), lambda i, j, k: (i, k)), pl.BlockSpec((tk, tn), lambda i, j, k: (k, j))]
    args = [a, w]
    if res is not None:
        in_specs.append(pl.BlockSpec((tm, tn), lambda i, j, k: (i, j)))
        args.append(res)
    return pl.pallas_call(
        body, grid=(T // tm, N // tn, nk), in_specs=in_specs,
        out_specs=pl.BlockSpec((tm, tn), lambda i, j, k: (i, j)), out_shape=SDS((T, N), out_dtype),
        scratch_shapes=[pltpu.VMEM((tm, tn), F32)] if nk > 1 else [], compiler_params=_cp(), name=name,
    )(*args)


def _matmul_nt(g, w, *, mul2a=None, tm, tn, name):
    T, K = g.shape
    N = w.shape[0]

    def body(*refs):
        g_ref, w_ref = refs[0], refs[1]
        o_ref = refs[-1]
        acc = _dot(g_ref[...].astype(BF16), w_ref[...], NT)
        if mul2a is not None:
            acc = acc * (2.0 * refs[2][...].astype(F32))
        o_ref[...] = acc.astype(BF16)

    in_specs = [pl.BlockSpec((tm, K), lambda i, j: (i, 0)), pl.BlockSpec((tn, K), lambda i, j: (j, 0))]
    args = [g, w]
    if mul2a is not None:
        in_specs.append(pl.BlockSpec((tm, tn), lambda i, j: (i, j)))
        args.append(mul2a)
    return pl.pallas_call(
        body, grid=(T // tm, N // tn), in_specs=in_specs,
        out_specs=pl.BlockSpec((tm, tn), lambda i, j: (i, j)), out_shape=SDS((T, N), BF16),
        compiler_params=_cp(), name=name,
    )(*args)


def _matmul_nt_rms(g, w, x, gain, dres, *, tm, tk, name):
    pieces = list(g) if isinstance(g, (list, tuple)) else [g]
    widths = [p.shape[1] for p in pieces]
    T, K = pieces[0].shape[0], sum(widths)
    D = w.shape[0]
    nk = K // tk
    nt = T // tm
    npc = len(pieces)
    assert npc == 1 or nk == 1

    def body(*refs):
        w_ref, x_ref, gain_ref = refs[npc:npc + 3]
        dres_ref = refs[npc + 3] if dres is not None else None
        n_in = npc + (4 if dres is not None else 3)
        dx_ref, dg_ref = refs[n_in], refs[n_in + 1]
        i, k = pl.program_id(0), pl.program_id(1)
        if npc == 1:
            part = _dot(refs[0][...].astype(BF16), w_ref[...], NT)
        else:
            part, off = None, 0
            for j in range(npc):
                d = _dot(refs[j][...].astype(BF16), w_ref[:, off:off + widths[j]], NT)
                part = d if part is None else part + d
                off += widths[j]

        def finish(dh):
            dx, dg = _rms_bwd(dh, x_ref[...], gain_ref[...])
            if dres_ref is not None:
                dx = dres_ref[...] + dx
            dx_ref[...] = dx

            @pl.when(i == 0)
            def _():
                dg_ref[...] = dg

            @pl.when(i > 0)
            def _():
                dg_ref[...] += dg

        if nk == 1:
            finish(part)
        else:
            acc = refs[-1]

            @pl.when(k == 0)
            def _():
                acc[...] = part

            @pl.when(k > 0)
            def _():
                acc[...] += part

            @pl.when(k == nk - 1)
            def _():
                finish(acc[...])

    g_specs = ([pl.BlockSpec((tm, tk), lambda i, k: (i, k))] if npc == 1 else
               [pl.BlockSpec((tm, wd), lambda i, k: (i, 0)) for wd in widths])
    in_specs = g_specs + [pl.BlockSpec((D, tk), lambda i, k: (0, k)),
                          pl.BlockSpec((tm, D), lambda i, k: (i, 0)), pl.BlockSpec((1, D), lambda i, k: (0, 0))]
    args = pieces + [w, x, gain.reshape(1, D)]
    if dres is not None:
        in_specs.append(pl.BlockSpec((tm, D), lambda i, k: (i, 0)))
        args.append(dres)
    return pl.pallas_call(
        body, grid=(nt, nk), in_specs=in_specs,
        out_specs=[pl.BlockSpec((tm, D), lambda i, k: (i, 0)), pl.BlockSpec((1, D), lambda i, k: (0, 0))],
        out_shape=[SDS((T, D), F32), SDS((1, D), F32)],
        scratch_shapes=[pltpu.VMEM((tm, D), F32)] if nk > 1 else [], compiler_params=_cp(), name=name,
    )(*args)


def _matmul_tn(a, g, *, square=False, bk, bn, tt, out_dtype, name):
    T, K = a.shape
    N = g.shape[1]
    nt = T // tt

    def body(a_ref, g_ref, o_ref, acc):
        t = pl.program_id(2)
        av = a_ref[...]
        if square:
            af = av.astype(F32)
            av = (af * af).astype(BF16)
        part = _dot(av, g_ref[...].astype(BF16), TN)
        if nt == 1:
            o_ref[...] = part.astype(out_dtype)
        else:
            @pl.when(t == 0)
            def _():
                acc[...] = part

            @pl.when((t > 0) & (t < nt - 1))
            def _():
                acc[...] += part

            @pl.when(t == nt - 1)
            def _():
                o_ref[...] = (acc[...] + part).astype(out_dtype)

    return pl.pallas_call(
        body, grid=(K // bk, N // bn, nt),
        in_specs=[pl.BlockSpec((tt, bk), lambda i, j, t: (t, i)), pl.BlockSpec((tt, bn), lambda i, j, t: (t, j))],
        out_specs=pl.BlockSpec((bk, bn), lambda i, j, t: (i, j)), out_shape=SDS((K, N), out_dtype),
        scratch_shapes=[pltpu.VMEM((bk, bn), F32)], compiler_params=_cp(), name=name,
    )(a, g)


def _down_loss(act, w_down, x2, g_final, target, *, tm, name):
    T, D = x2.shape
    F = act.shape[1]

    def body(a_ref, w_ref, x2_ref, g_ref, t_ref, dx_ref, dg_ref, loss_ref):
        i = pl.program_id(0)
        af = a_ref[...].astype(F32)
        xv, g = x2_ref[...] + _dot((af * af).astype(BF16), w_ref[...]), g_ref[...]
        r = _rstd(xv)
        xhat = xv * r
        diff = xhat * g - t_ref[...]
        part = 0.5 * jnp.sum(jnp.mean(diff * diff, axis=-1, keepdims=True), axis=0, keepdims=True)
        dy = diff * (1.0 / D)
        dxhat = dy * g
        dx_ref[...] = r * (dxhat - xhat * jnp.mean(dxhat * xhat, axis=-1, keepdims=True))
        dg = jnp.sum(dy * xhat, axis=0, keepdims=True)
        lp = jnp.broadcast_to(part, loss_ref.shape)

        @pl.when(i == 0)
        def _():
            dg_ref[...] = dg
            loss_ref[...] = lp

        @pl.when(i > 0)
        def _():
            dg_ref[...] += dg
            loss_ref[...] += lp

    rows = pl.BlockSpec((tm, D), lambda i: (i, 0))
    return pl.pallas_call(
        body, grid=(T // tm,),
        in_specs=[pl.BlockSpec((tm, F), lambda i: (i, 0)), pl.BlockSpec((F, D), lambda i: (0, 0)), rows,
                  pl.BlockSpec((1, D), lambda i: (0, 0)), rows],
        out_specs=[rows, pl.BlockSpec((1, D), lambda i: (0, 0)), pl.BlockSpec((8, 128), lambda i: (0, 0))],
        out_shape=[SDS((T, D), F32), SDS((1, D), F32), SDS((8, 128), F32)],
        compiler_params=_cp(), name=name,
    )(act, w_down, x2, g_final.reshape(1, D), target)


def _head_lanes(shape, width):
    return lax.broadcasted_iota(jnp.int32, shape, len(shape) - 1) // width


def _gate_fwd(gate, b_pad, *, B, S, name):
    def body(g_ref, b_ref, cc_ref):
        xv = g_ref[...] + b_ref[...]
        lf = jnp.minimum(xv, 0.0) - jnp.log(1.0 + jnp.exp(-jnp.abs(xv)))
        lane = lax.broadcasted_iota(jnp.int32, lf.shape, 1)
        row = lax.broadcasted_iota(jnp.int32, lf.shape, 0)
        c = jnp.where(lane < 8, lf, 0.0)
        sh = 1
        while sh < S:
            c = c + jnp.where(row >= sh, pltpu.roll(c, sh, 0), 0.0)
            sh *= 2
        grp = _head_lanes((S, WIDTH), HEAD_DIM)
        cc = jnp.zeros((S, WIDTH), F32)
        for h in range(8):
            cc = jnp.where(grp == h, c[:, h:h + 1], cc)
        cc_ref[...] = cc

    return pl.pallas_call(
        body, grid=(B,),
        in_specs=[pl.BlockSpec((S, 128), lambda b: (b, 0)), pl.BlockSpec((1, 128), lambda b: (0, 0))],
        out_specs=pl.BlockSpec((S, WIDTH), lambda b: (b, 0)), out_shape=SDS((B * S, WIDTH), F32),
        compiler_params=_cp(), name=name,
    )(gate, b_pad)


def _gate_bwd(dcc, gate, b_pad, *, B, S, name):
    def body(dcc_ref, g_ref, b_ref, dg_ref, db_ref):
        bi = pl.program_id(0)
        dccv = dcc_ref[...]
        lane = lax.broadcasted_iota(jnp.int32, (S, 128), 1)
        row = lax.broadcasted_iota(jnp.int32, (S, 128), 0)
        dc = jnp.zeros((S, 128), F32)
        for h in range(8):
            dc = jnp.where(lane == h, dccv[:, HEAD_DIM * h:HEAD_DIM * h + 1], dc)
        sh = 1
        while sh < S:
            dc = dc + jnp.where(row < S - sh, pltpu.roll(dc, S - sh, 0), 0.0)
            sh *= 2
        xv = g_ref[...] + b_ref[...]
        dgate = jnp.where(lane < 8, dc / (1.0 + jnp.exp(xv)), 0.0)
        dg_ref[...] = dgate.astype(BF16)
        db = jnp.sum(dgate, axis=0, keepdims=True)

        @pl.when(bi == 0)
        def _():
            db_ref[...] = db

        @pl.when(bi > 0)
        def _():
            db_ref[...] += db

    return pl.pallas_call(
        body, grid=(B,),
        in_specs=[pl.BlockSpec((S, WIDTH), lambda b: (b, 0)), pl.BlockSpec((S, 128), lambda b: (b, 0)),
                  pl.BlockSpec((1, 128), lambda b: (0, 0))],
        out_specs=[pl.BlockSpec((S, 128), lambda b: (b, 0)), pl.BlockSpec((1, 128), lambda b: (0, 0))],
        out_shape=[SDS((B * S, 128), BF16), SDS((1, 128), F32)],
        compiler_params=_cp(), name=name,
    )(dcc, gate, b_pad)


_SMEM_SPEC = pl.BlockSpec(memory_space=pltpu.SMEM)


def _alibi_slopes():
    return 2.0 ** (-(jnp.arange(1, 9, dtype=F32) * (8.0 / 8)))


def _pair_masks():
    lane = lax.broadcasted_iota(jnp.int32, (1, 128), 1)
    first = lane < HEAD_DIM
    return (first.astype(BF16), (~first).astype(BF16)), first


BNT =(((2,), (2,)), ((0,), (0,)))
BNN = (((2,), (1,)), ((0,), (0,)))
BTN = (((1,), (1,)), ((0,), (0,)))


def _split3(v):
    hi = v.astype(BF16).astype(F32)
    mid = (v - hi).astype(BF16).astype(F32)
    lo = (v - hi - mid).astype(BF16).astype(F32)
    return [hi, mid, lo]


def _with_spare_lanes(base, e, cols):
    lane = lax.broadcasted_iota(jnp.int32, (1, 128), 1)
    off = HEAD_DIM * (1 - e)
    extra = jnp.zeros(base.shape, F32)
    for j, c in enumerate(cols):
        extra = jnp.where(lane == off + j, c, extra)
    return base + extra.astype(BF16)


ONES3 = [1.0, 1.0, 1.0]


def _band_bias(slope, dilation):
    qi = lax.broadcasted_iota(jnp.int32, (BLOCK, BLOCK), 0)
    kj = lax.broadcasted_iota(jnp.int32, (BLOCK, BLOCK), 1)
    cur = jnp.where(kj <= qi, (-slope * dilation) * (qi - kj).astype(F32), NEG)
    prev = jnp.where(kj >= qi, (-slope * dilation) * (qi + BLOCK - kj).astype(F32), NEG)
    return cur, prev


def _to_residue_major(dst, src_f32, dilation, nb, lead=0):
    L = nb * BLOCK
    for r in range(dilation):
        rows = src_f32[pl.ds(r, L, stride=dilation), :] if dilation > 1 else src_f32[...]
        dst[lead + r * nb:lead + (r + 1) * nb] = rows.reshape(nb, BLOCK, 128).astype(dst.dtype)


def _dil_attn_fwd(z, *, B, S, name):
    NB = S // BLOCK

    def body(slope_ref, q_ref, k_ref, v_ref, y_ref, lse_ref, qf, kf, vf, qd, kd, vd, od, ld, acc_o, acc_l):
        (m_first, m_second), first = _pair_masks()
        p = pl.program_id(1)
        qf[...] = q_ref[...].astype(F32)
        kf[...] = k_ref[...].astype(F32)
        vf[...] = v_ref[...].astype(F32)
        kd[0] = jnp.zeros((BLOCK, 128), BF16)
        vd[0] = jnp.zeros((BLOCK, 128), BF16)
        blk = lax.broadcasted_iota(jnp.int32, (NB, 1, 1), 0)

        for idx, (_, dilation) in enumerate(DIL_CONFIGS):
            nb = NB // dilation
            _to_residue_major(qd, qf, dilation, nb)
            _to_residue_major(kd, kf, dilation, nb, lead=1)
            _to_residue_major(vd, vf, dilation, nb, lead=1)
            q4, kc, vc = qd[...], kd[1:NB + 1], vd[1:NB + 1]
            outs, lses = [], []
            for e, hm in enumerate((m_first, m_second)):
                bias_cur, bias_prev = _band_bias(slope_ref[2 * p + e], dilation)
                qm = q4 * hm
                sc = _dot(qm, kc, BNT) * 0.125 + bias_cur
                m = jnp.max(sc, axis=2, keepdims=True)
                if nb > 1:
                    sp = _dot(qm, kd[0:NB], BNT) * 0.125 + jnp.where(blk % nb == 0, NEG, bias_prev)
                    m = jnp.maximum(m, jnp.max(sp, axis=2, keepdims=True))
                pc = jnp.exp(sc - m)
                l = jnp.sum(pc, axis=2, keepdims=True)
                o = _dot(pc.astype(BF16), vc, BNN)
                if nb > 1:
                    pp = jnp.exp(sp - m)
                    l = l + jnp.sum(pp, axis=2, keepdims=True)
                    o = o + _dot(pp.astype(BF16), vd[0:NB], BNN)
                outs.append(o * (1.0 / l))
                lses.append(m + jnp.log(l))
            od[...] = jnp.where(first, outs[0], outs[1])
            ld[...] = jnp.where(first, lses[0], lses[1])

            L = nb * BLOCK
            for r in range(dilation):
                rows = pl.ds(r, L, stride=dilation) if dilation > 1 else slice(None)
                o_new = od[r * nb:(r + 1) * nb].reshape(L, 128)
                l_new = ld[r * nb:(r + 1) * nb].reshape(L, 128)
                if idx == 0:
                    acc_o[rows, :] = o_new
                    acc_l[rows, :] = l_new
                else:
                    l_old = acc_l[rows, :]
                    m2 = jnp.maximum(l_old, l_new)
                    w_old, w_new = jnp.exp(l_old - m2), jnp.exp(l_new - m2)
                    tot = w_old + w_new
                    acc_o[rows, :] = (w_old * acc_o[rows, :] + w_new * o_new) * (1.0 / tot)
                    acc_l[rows, :] = m2 + jnp.log(tot)

        y_ref[...] = acc_o[...].astype(BF16)
        lse_ref[...] = acc_l[...]

    spec = lambda off: pl.BlockSpec((S, 128), lambda b, p: (b, 4 * off + p))
    ospec = pl.BlockSpec((S, 128), lambda b, p: (b, p))
    blocks = lambda n, dt: pltpu.VMEM((n, BLOCK, 128), dt)
    return pl.pallas_call(
        body, grid=(B, 4), in_specs=[_SMEM_SPEC, spec(0), spec(1), spec(2)], out_specs=[ospec, ospec],
        out_shape=[SDS((B * S, WIDTH), BF16), SDS((B * S, WIDTH), F32)],
        scratch_shapes=[pltpu.VMEM((S, 128), F32)] * 3 + [blocks(NB, BF16), blocks(NB + 1, BF16), blocks(NB + 1, BF16),
                                                         blocks(NB, F32), blocks(NB, F32)] + [pltpu.VMEM((S, 128), F32)] * 2,
        compiler_params=_cp(), name=name,
    )(_alibi_slopes(), z, z, z)


def _dil_attn_bwd(z, dy, ya, lse, *, B, S, name):
    NB = S // BLOCK

    def body(slope_ref, q_ref, k_ref, v_ref, do_ref, o_ref, lse_ref, dq_ref, dk_ref, dv_ref,
             qf, kf, vf, dof, ef, qd, dod, kd, vd, lsd, ed, dkd, dvd, dqa, dka, dva):
        (m_first, m_second), first = _pair_masks()
        p = pl.program_id(1)
        qf[...] = q_ref[...].astype(F32)
        kf[...] = k_ref[...].astype(F32)
        vf[...] = v_ref[...].astype(F32)
        dov = do_ref[...].astype(F32)
        dof[...] = dov
        prod = dov * o_ref[...].astype(F32)
        ef[...] = jnp.where(first, jnp.sum(jnp.where(first, prod, 0.0), axis=1, keepdims=True),
                            jnp.sum(jnp.where(first, 0.0, prod), axis=1, keepdims=True))
        kd[0] = jnp.zeros((BLOCK, 128), BF16)
        vd[0] = jnp.zeros((BLOCK, 128), BF16)
        blk = lax.broadcasted_iota(jnp.int32, (NB, 1, 1), 0)

        for idx, (_, dilation) in enumerate(DIL_CONFIGS):
            nb = NB // dilation
            _to_residue_major(qd, qf, dilation, nb)
            _to_residue_major(dod, dof, dilation, nb)
            _to_residue_major(kd, kf, dilation, nb, lead=1)
            _to_residue_major(vd, vf, dilation, nb, lead=1)
            _to_residue_major(lsd, lse_ref, dilation, nb)
            _to_residue_major(ed, ef, dilation, nb)
            q4, do4, kc, vc = qd[...], dod[...], kd[1:NB + 1], vd[1:NB + 1]
            dq4 = None
            dkc = dvc = dkp = dvp = None
            for e, hm in enumerate((m_first, m_second)):
                lane0 = slice(HEAD_DIM * e, HEAD_DIM * e + 1)
                bias_cur, bias_prev = _band_bias(slope_ref[2 * p + e], dilation)
                qm, dom = q4 * hm, do4 * hm
                lse_e, e_e = lsd[...][:, :, lane0], ed[...][:, :, lane0]
                pc = jnp.exp(_dot(qm, kc, BNT) * 0.125 + bias_cur - lse_e)
                dsc = (pc * (_dot(dom, vc, BNT) - e_e)).astype(BF16)
                pcb = pc.astype(BF16)
                dqe = _dot(dsc, kc, BNN)
                dkc = _dot(dsc, qm, BTN) if e == 0 else dkc + _dot(dsc, qm, BTN)
                dvc = _dot(pcb, dom, BTN) if e == 0 else dvc + _dot(pcb, dom, BTN)
                if nb > 1:
                    kp, vp = kd[0:NB], vd[0:NB]
                    pp = jnp.exp(_dot(qm, kp, BNT) * 0.125 + jnp.where(blk % nb == 0, NEG, bias_prev) - lse_e)
                    dsp = (pp * (_dot(dom, vp, BNT) - e_e)).astype(BF16)
                    ppb = pp.astype(BF16)
                    dqe = dqe + _dot(dsp, kp, BNN)
                    dkp = _dot(dsp, qm, BTN) if e == 0 else dkp + _dot(dsp, qm, BTN)
                    dvp = _dot(ppb, dom, BTN) if e == 0 else dvp + _dot(ppb, dom, BTN)
                dq4 = dqe if e == 0 else jnp.where(first, dq4, dqe)

            dkd[1:NB + 1] = dkc
            dvd[1:NB + 1] = dvc
            if nb > 1:
                dkd[1:NB] += dkp[1:NB]
                dvd[1:NB] += dvp[1:NB]
            L = nb * BLOCK
            for r in range(dilation):
                rows = pl.ds(r, L, stride=dilation) if dilation > 1 else slice(None)
                dq_r = dq4[r * nb:(r + 1) * nb].reshape(L, 128) * 0.125
                dk_r = dkd[1 + r * nb:1 + (r + 1) * nb].reshape(L, 128) * 0.125
                dv_r = dvd[1 + r * nb:1 + (r + 1) * nb].reshape(L, 128)
                if idx == 0:
                    dqa[rows, :], dka[rows, :], dva[rows, :] = dq_r, dk_r, dv_r
                else:
                    dqa[rows, :] += dq_r
                    dka[rows, :] += dk_r
                    dva[rows, :] += dv_r

        dq_ref[...] = dqa[...].astype(BF16)
        dk_ref[...] = dka[...].astype(BF16)
        dv_ref[...] = dva[...].astype(BF16)

    spec = lambda off: pl.BlockSpec((S, 128), lambda b, p: (b, 4 * off + p))
    ospec = pl.BlockSpec((S, 128), lambda b, p: (b, p))
    blocks = lambda n, dt: pltpu.VMEM((n, BLOCK, 128), dt)
    return pl.pallas_call(
        body, grid=(B, 4), in_specs=[_SMEM_SPEC, spec(0), spec(1), spec(2), ospec, ospec, ospec],
        out_specs=[ospec] * 3, out_shape=[SDS((B * S, WIDTH), BF16)] * 3,
        scratch_shapes=[pltpu.VMEM((S, 128), F32)] * 5
        + [blocks(NB, BF16), blocks(NB, BF16), blocks(NB + 1, BF16), blocks(NB + 1, BF16), blocks(NB, F32), blocks(NB, F32),
           blocks(NB + 1, F32), blocks(NB + 1, F32)] + [pltpu.VMEM((S, 128), F32)] * 3,
        compiler_params=_cp(), name=name,
    )(_alibi_slopes(), z, z, z, dy, ya, lse)


FOX_TQ = 256


def _fox_fwd(z, cc, *, B, S, name):
    def body(q_ref, k_ref, v_ref, cc_ref, o_ref, l_ref, qa, ka):
        (m_first, m_second), first = _pair_masks()
        ccv = cc_ref[...]
        eighth = jnp.asarray(0.125, BF16)
        for e, hm in enumerate((m_first, m_second)):
            c_e = jnp.broadcast_to(ccv[:, HEAD_DIM * e:HEAD_DIM * e + 1], (S, 128))
            qa[e] = _with_spare_lanes(q_ref[...] * hm * eighth, e, _split3(c_e) + ONES3)
            ka[e] = _with_spare_lanes(k_ref[...] * hm, e, ONES3 + _split3(-c_e))
        for qi in range(S // FOX_TQ):
            r0, kend = qi * FOX_TQ, (qi + 1) * FOX_TQ
            vv = v_ref[0:kend, :]
            row = lax.broadcasted_iota(jnp.int32, (FOX_TQ, kend), 0) + r0
            col = lax.broadcasted_iota(jnp.int32, (FOX_TQ, kend), 1)
            causal = col <= row
            outs, lses = [], []
            for e in (0, 1):
                s = jnp.where(causal, _dot(qa[e, r0:kend, :], ka[e, 0:kend, :], NT), NEG)
                m = jnp.max(s, axis=1, keepdims=True)
                pe = jnp.exp(s - m)
                l = jnp.sum(pe, axis=1, keepdims=True)
                outs.append(_dot(pe.astype(BF16), vv) * (1.0 / l))
                lses.append(m + jnp.log(l))
            o_ref[r0:kend, :] = jnp.where(first, outs[0], outs[1]).astype(BF16)
            l_ref[r0:kend, :] = jnp.where(first, lses[0], lses[1])

    spec = lambda off: pl.BlockSpec((S, 128), lambda b, p: (b, 4 * off + p))
    pspec = pl.BlockSpec((S, 128), lambda b, p: (b, p))
    return pl.pallas_call(
        body, grid=(B, 4), in_specs=[spec(3), spec(4), spec(5), pspec], out_specs=[pspec, pspec],
        out_shape=[SDS((B * S, WIDTH), BF16), SDS((B * S, WIDTH), F32)],
        scratch_shapes=[pltpu.VMEM((2, S, 128), BF16)] * 2, compiler_params=_cp(), name=name,
    )(z, z, z, cc)


def _fox_bwd(z, dy, lse, cc, *, B, S, name):
    def body(q_ref, k_ref, v_ref, do_ref, lse_ref, cc_ref, dq_ref, dk_ref, dv_ref, dc_ref,
             qa, ka, qp, kp, vp, dp, dk_s, dv_s, dc_s):
        (m_first, m_second), first = _pair_masks()
        ccv, lsev = cc_ref[...], lse_ref[...]
        eighth = jnp.asarray(0.125, BF16)
        for e, hm in enumerate((m_first, m_second)):
            lane0 = slice(HEAD_DIM * e, HEAD_DIM * e + 1)
            c_e = jnp.broadcast_to(ccv[:, lane0], (S, 128))
            lse_e = jnp.broadcast_to(lsev[:, lane0], (S, 128))
            qp[e] = q_ref[...] * hm
            kp[e] = k_ref[...] * hm
            dp[e] = do_ref[...] * hm
            qa[e] = _with_spare_lanes(qp[e] * eighth, e, _split3(c_e - lse_e) + ONES3)
            ka[e] = _with_spare_lanes(kp[e], e, ONES3 + _split3(-c_e))
            vp[e] = v_ref[...] * hm
        dk_s[...] = jnp.zeros_like(dk_s)
        dv_s[...] = jnp.zeros_like(dv_s)
        dc_s[...] = jnp.zeros_like(dc_s)
        for qi in range(S // FOX_TQ):
            r0, kend = qi * FOX_TQ, (qi + 1) * FOX_TQ
            krow = lax.broadcasted_iota(jnp.int32, (kend, FOX_TQ), 0)
            qcol = lax.broadcasted_iota(jnp.int32, (kend, FOX_TQ), 1) + r0
            causal = krow <= qcol
            dq_t = jnp.zeros((FOX_TQ, 128), F32)
            for e in (0, 1):
                sel = first if e == 0 else ~first
                pt = jnp.where(causal, jnp.exp(_dot(ka[e, 0:kend, :], qa[e, r0:kend, :], NT)), 0.0)
                dpt = _dot(vp[e, 0:kend, :], dp[e, r0:kend, :], NT)
                mean = jnp.sum(pt * dpt, axis=0, keepdims=True) / jnp.sum(pt, axis=0, keepdims=True)
                dst = pt * (dpt - mean)
                dsb = dst.astype(BF16)
                dv_s[0:kend, :] += _dot(pt.astype(BF16), dp[e, r0:kend, :])
                dk_s[0:kend, :] += _dot(dsb, qp[e, r0:kend, :]) * 0.125
                dq_t = dq_t + _dot(dsb, kp[e, 0:kend, :], TN)
                dc_s[0:kend, :] += jnp.where(sel, -jnp.sum(dst, axis=1, keepdims=True), 0.0)
            dq_ref[r0:kend, :] = (dq_t * 0.125).astype(BF16)
        dk_ref[...] = dk_s[...].astype(BF16)
        dv_ref[...] = dv_s[...].astype(BF16)
        dc_ref[...] = dc_s[...]

    spec = lambda off: pl.BlockSpec((S, 128), lambda b, p: (b, 4 * off + p))
    pspec = pl.BlockSpec((S, 128), lambda b, p: (b, p))
    return pl.pallas_call(
        body, grid=(B, 4),
        in_specs=[spec(3), spec(4), spec(5), pl.BlockSpec((S, 128), lambda b, p: (b, 4 + p)), pspec, pspec],
        out_specs=[pspec] * 4,
        out_shape=[SDS((B * S, WIDTH), BF16)] * 3 + [SDS((B * S, WIDTH), F32)],
        scratch_shapes=[pltpu.VMEM((2, S, 128), BF16)] * 6 + [pltpu.VMEM((S, 128), F32)] * 3,
        compiler_params=_cp(), name=name,
    )(z, z, z, dy, lse, cc)


def _xattn_fwd(q, kv, *, B, S, M, tq, name):
    D = D_MODEL

    def body(q_ref, kv_ref, o_ref):
        for h in range(N_XH):
            cs = slice(XHD * h, XHD * (h + 1))
            s = _dot(q_ref[:, cs], kv_ref[:, cs], NT) * (1.0 / 16.0)
            pe = jnp.exp(s - jnp.max(s, axis=1, keepdims=True))
            l = jnp.sum(pe, axis=1, keepdims=True)
            o_ref[:, cs] = (_dot(pe.astype(BF16), kv_ref[:, D + XHD * h:D + XHD * (h + 1)]) * (1.0 / l)).astype(BF16)

    nq = S // tq
    return pl.pallas_call(
        body, grid=(B, nq),
        in_specs=[pl.BlockSpec((tq, D), lambda b, t: (b * nq + t, 0)), pl.BlockSpec((M, 2 * D), lambda b, t: (b, 0))],
        out_specs=pl.BlockSpec((tq, D), lambda b, t: (b * nq + t, 0)), out_shape=SDS((B * S, D), BF16),
        compiler_params=_cp(), name=name,
    )(q, kv)


def _xattn_bwd(q, kv, do, *, B, S, M, tq, name):
    D = D_MODEL

    def body(q_ref, kv_ref, do_ref, dq_ref, dkv_ref):
        t = pl.program_id(1)

        @pl.when(t == 0)
        def _():
            dkv_ref[...] = jnp.zeros_like(dkv_ref)

        for h in range(N_XH):
            cs = slice(XHD * h, XHD * (h + 1))
            vs = slice(D + XHD * h, D + XHD * (h + 1))
            qh, kh, vh, doh = q_ref[:, cs], kv_ref[:, cs], kv_ref[:, vs], do_ref[:, cs]
            s = _dot(qh, kh, NT) * (1.0 / 16.0)
            pe = jnp.exp(s - jnp.max(s, axis=1, keepdims=True))
            pe = pe * (1.0 / jnp.sum(pe, axis=1, keepdims=True))
            dp = _dot(doh, vh, NT)
            ds = (pe * (dp - jnp.sum(pe * dp, axis=1, keepdims=True))).astype(BF16)
            dq_ref[:, cs] = (_dot(ds, kh) * (1.0 / 16.0)).astype(BF16)
            dkv_ref[:, cs] += _dot(ds, qh, TN) * (1.0 / 16.0)
            dkv_ref[:, vs] += _dot(pe.astype(BF16), doh, TN)

    nq = S // tq
    qspec = pl.BlockSpec((tq, D), lambda b, t: (b * nq + t, 0))
    kvspec = pl.BlockSpec((M, 2 * D), lambda b, t: (b, 0))
    return pl.pallas_call(
        body, grid=(B, nq), in_specs=[qspec, kvspec, qspec], out_specs=[qspec, kvspec],
        out_shape=[SDS((B * S, D), BF16), SDS((B * M, 2 * D), F32)], compiler_params=_cp(), name=name,
    )(q, kv, do)


def _adamw(parts, w, m, v, *, tr, name):
    R, C = w.shape

    def body(p_ref, w_ref, m_ref, v_ref, g_ref, d_ref, nm_ref, nv_ref):
        g = p_ref[0].astype(F32)
        for d in range(1, N_DEV):
            g = g + p_ref[d].astype(F32)
        m2 = ADAM_B1 * m_ref[...] + (1.0 - ADAM_B1) * g
        v2 = ADAM_B2 * v_ref[...] + (1.0 - ADAM_B2) * (g * g)
        m_hat = m2 / (1.0 - ADAM_B1 ** ADAM_STEP)
        v_hat = v2 / (1.0 - ADAM_B2 ** ADAM_STEP)
        g_ref[...] = g
        d_ref[...] = -ADAM_LR * (m_hat / (jnp.sqrt(v_hat) + ADAM_EPS) + ADAM_WD * w_ref[...])
        nm_ref[...] = m2
        nv_ref[...] = v2

    spec = pl.BlockSpec((tr, C), lambda i: (i, 0))
    return pl.pallas_call(
        body, grid=(R // tr,), in_specs=[pl.BlockSpec((N_DEV, tr, C), lambda i: (0, i, 0)), spec, spec, spec],
        out_specs=[spec] * 4, out_shape=[SDS((R, C), F32)] * 4, compiler_params=_cp(), name=name,
    )(parts, w, m, v)


def _peer(k, x, y, c):
    return (1 - x if k & 4 else x, 1 - y if k & 2 else y, 1 - c if k & 1 else c)


_HBM_SPEC = pl.BlockSpec(memory_space=pltpu.HBM)
_SEM_SPEC = pl.BlockSpec(memory_space=pltpu.SEMAPHORE)
_SPLIT_EFFECT = pltpu.SideEffectType.DATAFLOW_SIDE_EFFECTING


def _split_copies(srcs, lands, send_sems, recv_sems, modes):
    x, y, c = (lax.axis_index(a) for a in AXES)
    me = 4 * x + 2 * y + c
    copies = []
    for i, md in enumerate(modes):
        for k in range(1, N_DEV):
            px, py, pc = _peer(k, x, y, c)
            src = srcs[i] if md == "gather" else srcs[i].at[4 * px + 2 * py + pc]
            j = i * (N_DEV - 1) + k - 1
            copies.append(pltpu.make_async_remote_copy(
                src_ref=src, dst_ref=lands[i].at[me], send_sem=send_sems.at[j], recv_sem=recv_sems.at[j],
                device_id=(px, py, pc), device_id_type=pl.DeviceIdType.MESH))
    return copies


def _exchange_start(arrays, modes, *, name):
    n = len(arrays)
    hbm = lambda a: pltpu.with_memory_space_constraint(a, pltpu.HBM)
    srcs = [hbm(a) for a in arrays]
    me = 4 * lax.axis_index("x") + 2 * lax.axis_index("y") + lax.axis_index("c")

    def landing(a, md):
        own = a[None] if md == "gather" else lax.dynamic_index_in_dim(a, me, 0, keepdims=True)
        return hbm(lax.dynamic_update_index_in_dim(lax.empty((N_DEV,) + own.shape[1:], a.dtype), own, me, 0))

    lands = [landing(a, md) for a, md in zip(arrays, modes)]

    def body(*refs):
        for cp in _split_copies(refs[:n], refs[n:2 * n], refs[2 * n], refs[2 * n + 1], modes):
            cp.start()
        token = refs[-1]
        token[...] = jnp.zeros_like(token)

    sems = pltpu.SemaphoreType.DMA((n * (N_DEV - 1),))
    outs = pl.pallas_call(
        body, name=name, in_specs=[_HBM_SPEC] * (2 * n),
        out_shape=(sems, sems, *[pltpu.HBM(a.shape, a.dtype) for a in srcs + lands], SDS((8, 128), F32)),
        out_specs=(_SEM_SPEC, _SEM_SPEC, *[_HBM_SPEC] * (2 * n), pl.BlockSpec(memory_space=pltpu.VMEM)),
        input_output_aliases={i: 2 + i for i in range(2 * n)},
        compiler_params=pltpu.CompilerParams(has_side_effects=_SPLIT_EFFECT),
    )(*srcs, *lands)
    return (outs[0], outs[1], outs[2:2 + n], outs[2 + n:2 + 2 * n], modes), outs[-1]


def _exchange_wait(handle, after, *, name):
    send_sems, recv_sems, srcs, lands, modes = handle
    n = len(srcs)

    def body(*refs):
        for cp in _split_copies(refs[:n], refs[n:2 * n], refs[2 * n], refs[2 * n + 1], modes):
            cp.wait_send()
            cp.wait_recv()

    outs = pl.pallas_call(
        body, name=name, in_specs=[_HBM_SPEC] * (2 * n) + [_SEM_SPEC, _SEM_SPEC, pl.BlockSpec(memory_space=pl.ANY)],
        out_shape=tuple(pltpu.HBM(a.shape, a.dtype) for a in list(srcs) + list(lands)), out_specs=tuple([_HBM_SPEC] * (2 * n)),
        input_output_aliases={i: i for i in range(2 * n)},
        compiler_params=pltpu.CompilerParams(has_side_effects=_SPLIT_EFFECT),
    )(*srcs, *lands, send_sems, recv_sems, after)
    return list(outs[n:])


def _exchange(arrays, modes, *, name):
    n = len(arrays)
    out_shape = [SDS((N_DEV,) + a.shape if md == "gather" else a.shape, a.dtype) for a, md in zip(arrays, modes)]

    def body(*refs):
        ins, outs = refs[:n], refs[n:2 * n]
        send_sems, recv_sems, local_sems = refs[2 * n:]
        x, y, c = (lax.axis_index(a) for a in AXES)
        me = 4 * x + 2 * y + c
        copies = []
        for i, md in enumerate(modes):
            src = ins[i] if md == "gather" else ins[i].at[me]
            cp = pltpu.make_async_copy(src, outs[i].at[me], local_sems.at[i])
            cp.start()
            copies.append(cp)
            for k in range(1, N_DEV):
                px, py, pc = _peer(k, x, y, c)
                src = ins[i] if md == "gather" else ins[i].at[4 * px + 2 * py + pc]
                cp = pltpu.make_async_remote_copy(
                    src_ref=src, dst_ref=outs[i].at[me], send_sem=send_sems.at[i, k - 1], recv_sem=recv_sems.at[i, k - 1],
                    device_id=(px, py, pc), device_id_type=pl.DeviceIdType.MESH)
                cp.start()
                copies.append(cp)
        for cp in copies:
            cp.wait()

    anyspec = pl.BlockSpec(memory_space=pl.ANY)
    return pl.pallas_call(
        body, in_specs=[anyspec] * n, out_specs=[anyspec] * n, out_shape=out_shape,
        scratch_shapes=[pltpu.SemaphoreType.DMA((n, N_DEV - 1)), pltpu.SemaphoreType.DMA((n, N_DEV - 1)),
                        pltpu.SemaphoreType.DMA((n,))],
        name=name,
    )(*arrays)


def _local_step(x, mem, g_mix, b_forget, g_xattn, g_mem, g_mlp, g_final, target, get_w_in, get_rest, send):
    B, S, D = x.shape
    M = mem.shape[1]
    T = B * S
    x0 = x.reshape(T, D)
    mem2 = mem.reshape(B * M, D)
    tgt = target.reshape(T, D)
    b_pad = jnp.pad(b_forget, (0, 120)).reshape(1, 128)
    after = lambda a, tok: a if tok is None else a + tok[0, 0]

    w_in_pad = get_w_in()
    h1, z, gate = _rms_matmul(x0, g_mix, w_in_pad[:, :QKV_W], tm=ROWS, tn=QKV_W, out_dtype=BF16,
                              w_f32=w_in_pad[:, QKV_W:], name="f_in")
    cc = _gate_fwd(gate, b_pad, B=B, S=S, name="f_gatecum")
    ya, lse = _dil_attn_fwd(z, B=B, S=S, name="f_dil")
    yf, lse_f = _fox_fwd(z, cc, B=B, S=S, name="f_fox")
    ymix = jnp.concatenate([ya, yf], axis=1)
    w = get_rest(ymix)
    x1 = _matmul_nn(ymix, w["w_out"], res=x0, tm=ROWS, tn=D, tk=D, out_dtype=F32, name="f_out")
    h2, q = _rms_matmul(x1, g_xattn, w["w_xq"], tm=ROWS, tn=D, out_dtype=BF16, name="f_xq")
    mn, kv = _rms_matmul(mem2, g_mem, w["w_kv"], tm=B * M, tn=D, out_dtype=BF16, name="f_xkv")
    xo = _xattn_fwd(q, kv, B=B, S=S, M=M, tq=512, name="f_xattn")
    x2 = _matmul_nn(xo, w["w_xo"], res=x1, tm=ROWS, tn=D, tk=D, out_dtype=F32, name="f_xo")
    h3, act = _rms_matmul(x2, g_mlp, w["w_up"], tm=ROWS, tn=D_FF, out_dtype=BF16, relu=True, name="f_up")
    dx3, dg_final, loss = _down_loss(act, w["w_down"], x2, g_final, tgt, tm=ROWS, name="f_down")

    du = _matmul_nt(dx3, w["w_down"], mul2a=act, tm=ROWS, tn=D_FF, name="b_dact")
    dw_down = _matmul_tn(act, dx3, square=True, bk=1024, bn=D, tt=ACC_ROWS, out_dtype=BF16, name="b_wdown")
    dw_up = _matmul_tn(h3, du, bk=D, bn=1024, tt=ACC_ROWS, out_dtype=BF16, name="b_wup")
    tok = send(dict(w_down=dw_down, w_up=dw_up))
    dx2, dg_mlp = _matmul_nt_rms(du, w["w_up"], x2, after(g_mlp, tok), dx3, tm=ROWS, tk=D_FF, name="b_dh3")
    dxo = _matmul_nt(dx2, w["w_xo"], tm=1024, tn=D, name="b_dxo")
    dw_xo = _matmul_tn(xo, dx2, bk=D, bn=D, tt=ACC_ROWS, out_dtype=BF16, name="b_wxo")
    dq, dkv = _xattn_bwd(q, kv, dxo, B=B, S=S, M=M, tq=512, name="b_xattn")
    dw_xq = _matmul_tn(h2, dq, bk=D, bn=D, tt=ACC_ROWS, out_dtype=BF16, name="b_wxq")
    dx1, dg_xattn = _matmul_nt_rms(dq, w["w_xq"], x1, g_xattn, dx2, tm=ROWS, tk=D, name="b_dh2")
    dw_kv = _matmul_tn(mn, dkv, bk=D, bn=D, tt=B * M, out_dtype=BF16, name="b_wkv")
    _, dg_mem = _matmul_nt_rms(dkv, w["w_kv"], mem2, g_mem, None, tm=min(ROWS, B * M), tk=2 * D, name="b_dmem")
    dy = _matmul_nt(dx1, w["w_out"], tm=1024, tn=D, name="b_dy")
    dw_out = _matmul_tn(ymix, dx1, bk=D, bn=D, tt=ACC_ROWS, out_dtype=BF16, name="b_wout")
    tok = send(dict(w_xo=dw_xo, w_xq=dw_xq, w_xk=dw_kv[:, :D], w_xv=dw_kv[:, D:], w_out=dw_out))
    dqf, dkf, dvf, dcc = _fox_bwd(z, dy, lse_f, cc, B=B, S=S, name="b_fox")
    dgate, db = _gate_bwd(dcc, gate, after(b_pad, tok), B=B, S=S, name="b_gate")
    dqa, dka, dva = _dil_attn_bwd(z, dy, ya, lse, B=B, S=S, name="b_dil")
    dz = [dqa, dka, dva, dqf, dkf, dvf, dgate]
    dw_in = jnp.concatenate([_matmul_tn(h1, piece, bk=D, bn=piece.shape[1], tt=ACC_ROWS, out_dtype=BF16, name=f"b_win{j}")
                             for j, piece in enumerate(dz)], axis=1)
    tok = send(dict(w_in=dw_in))
    gx, dg_mix = _matmul_nt_rms(dz, w_in_pad, x0, after(g_mix, tok), dx1, tm=ROWS, tk=IN_PAD, name="b_dh1")

    small = dict(g_mix=dg_mix, b_forget=db, g_xattn=dg_xattn, g_mem=dg_mem, g_mlp=dg_mlp, g_final=dg_final)
    return gx.reshape(B, S, D), small, loss


SMALL_ROWS = ("g_mix", "b_forget", "g_xattn", "g_mem", "g_mlp", "g_final")
COL_SHARDED = ("w_in", "w_up")


def _pack_rows(rows):
    D = D_MODEL
    rows = [jnp.pad(r.reshape(-1), (0, D - r.size)) for r in rows]
    rows += [jnp.zeros((D,), F32)] * (8 - len(rows))
    return jnp.stack(rows)


def _full(name, g):
    if name in COL_SHARDED:
        return g.transpose(1, 0, 2).reshape(g.shape[1], -1)
    return g.reshape(-1, g.shape[2])


def _blocks(name, g, shard_shape):
    if name in COL_SHARDED:
        n = shard_shape[1]
        return g[:, :n * N_DEV].reshape(g.shape[0], N_DEV, n).transpose(1, 0, 2)
    return g.reshape((N_DEV,) + shard_shape)


def kernel(x, mem, g_mix, w_in, b_forget, w_out, g_xattn, g_mem, w_xq, w_xk, w_xv, w_xo, g_mlp, w_up, w_down, g_final, loss_target, m_g_mix, m_w_in, m_b_forget, m_w_out, m_g_xattn, m_g_mem, m_w_xq, m_w_xk, m_w_xv, m_w_xo, m_g_mlp, m_w_up, m_w_down, m_g_final, v_g_mix, v_w_in, v_b_forget, v_w_out, v_g_xattn, v_g_mem, v_w_xq, v_w_xk, v_w_xv, v_w_xo, v_g_mlp, v_w_up, v_w_down, v_g_final):
    W = dict(w_in=w_in, w_out=w_out, w_xq=w_xq, w_xk=w_xk, w_xv=w_xv, w_xo=w_xo, w_up=w_up, w_down=w_down)
    Mo = dict(w_in=m_w_in, w_out=m_w_out, w_xq=m_w_xq, w_xk=m_w_xk, w_xv=m_w_xv, w_xo=m_w_xo, w_up=m_w_up, w_down=m_w_down)
    Vo = dict(w_in=v_w_in, w_out=v_w_out, w_xq=v_w_xq, w_xk=v_w_xk, w_xv=v_w_xv, w_xo=v_w_xo, w_up=v_w_up, w_down=v_w_down)
    later = [n for n in W if n != "w_in"]

    first_handle, first_token = _exchange_start([w_in.astype(BF16)], ["gather"], name="gather_in_start")
    rest_handle, rest_token = _exchange_start([W[n].astype(BF16) + first_token[0, 0].astype(BF16) for n in later],
                                              ["gather"] * len(later), name="gather_rest_start")

    def get_w_in():
        (g,) = _exchange_wait(first_handle, rest_token, name="gather_in_wait")
        return jnp.pad(_full("w_in", g), ((0, 0), (0, IN_PAD - IN_W)))

    def get_rest(after):
        full = {n: _full(n, g) for n, g in zip(later, _exchange_wait(rest_handle, after, name="gather_rest_wait"))}
        full["w_kv"] = jnp.concatenate([full.pop("w_xk"), full.pop("w_xv")], axis=1)
        return full

    sent = []

    def send(grads):
        names = list(grads)
        handle, token = _exchange_start([_blocks(n, grads[n], W[n].shape) for n in names], ["scatter"] * len(names),
                                        name=f"scatter{len(sent)}_start")
        sent.append((names, handle))
        return token

    gx, small, loss = _local_step(x, mem, g_mix, b_forget, g_xattn, g_mem, g_mlp, g_final, loss_target, get_w_in, get_rest, send)

    received = {}
    for i, (names, handle) in enumerate(sent):
        received.update(zip(names, _exchange_wait(handle, gx, name=f"scatter{i}_wait")))
    packed = _pack_rows([small[n] for n in SMALL_ROWS] + [loss[0, :1]])
    (packed_all,) = _exchange([packed], ["gather"], name="gather_small")

    rows_per_step = lambda shape: max(t for t in (128, 256, 512) if shape[0] % t == 0 and t * shape[1] <= 512 * 512)
    res = {n: _adamw(received[n], W[n], Mo[n], Vo[n], tr=rows_per_step(W[n].shape), name=f"adamw_{n}") for n in W}
    small_w = dict(g_mix=g_mix, b_forget=b_forget, g_xattn=g_xattn, g_mem=g_mem, g_mlp=g_mlp, g_final=g_final)
    small_m = dict(g_mix=m_g_mix, b_forget=m_b_forget, g_xattn=m_g_xattn, g_mem=m_g_mem, g_mlp=m_g_mlp, g_final=m_g_final)
    small_v = dict(g_mix=v_g_mix, b_forget=v_b_forget, g_xattn=v_g_xattn, g_mem=v_g_mem, g_mlp=v_g_mlp, g_final=v_g_final)
    sres = _adamw(packed_all, _pack_rows([small_w[n] for n in SMALL_ROWS]), _pack_rows([small_m[n] for n in SMALL_ROWS]),
                  _pack_rows([small_v[n] for n in SMALL_ROWS]), tr=8, name="adamw_small")
    for i, n in enumerate(SMALL_ROWS):
        res[n] = [r[i, :small_w[n].size] for r in sres]
    loss_total = sres[0][6, 0]

    order = ["g_mix", "w_in", "b_forget", "w_out", "g_xattn", "g_mem", "w_xq", "w_xk", "w_xv", "w_xo", "g_mlp", "w_up", "w_down", "g_final"]
    return (loss_total, gx, *[res[n][0] for n in order], *[res[n][1] for n in order],
            *[res[n][2] for n in order], *[res[n][3] for n in order])
```

```python
import jax
import jax.numpy as jnp
from jax import lax
from jax.experimental import pallas as pl
from jax.experimental.pallas import tpu as pltpu

F32, BF16 = jnp.float32, jnp.bfloat16
SDS = jax.ShapeDtypeStruct

D_MODEL = 1024
HEAD_DIM = 64
WIDTH = 512
QKV_W = 6 * WIDTH
IN_W = QKV_W + 8
IN_PAD = QKV_W + 128
BLOCK = 128
DIL_CONFIGS = ((128, 1), (512, 4), (2048, 16))
N_XH, XHD = 4, 256
D_FF = 4096
EPS = 1e-6
NEG = -1e30
N_DEV = 8
AXES = ("x", "y", "c")

ADAM_LR, ADAM_B1, ADAM_B2, ADAM_EPS, ADAM_WD, ADAM_STEP = 0.001, 0.9, 0.999, 1e-08, 0.01, 10

VMEM_CAP_V7X = 64 * 1024 * 1024
VMEM_LIMIT = VMEM_CAP_V7X * 7 // 8

ROWS = 512
ACC_ROWS = 2048

NT = (((1,), (1,)), ((), ()))
TN = (((0,), (0,)), ((), ()))


def _cp(**kw):
    return pltpu.CompilerParams(vmem_limit_bytes=VMEM_LIMIT, **kw)


def _dot(a, b, dims=None):
    if dims is None:
        return jnp.dot(a, b, preferred_element_type=F32)
    return lax.dot_general(a, b, dims, preferred_element_type=F32)


def _rstd(xv):
    return lax.rsqrt(jnp.mean(xv * xv, axis=-1, keepdims=True) + EPS)


def _rms_bwd(dh, xv, g):
    r = _rstd(xv)
    xhat = xv * r
    dxhat = dh * g
    dx = r * (dxhat - xhat * jnp.mean(dxhat * xhat, axis=-1, keepdims=True))
    return dx, jnp.sum(dh * xhat, axis=0, keepdims=True)


def _rms_matmul(x, g, w, *, tm, tn, out_dtype, relu=False, w_f32=None, name):
    T, D = x.shape
    N = w.shape[1]

    def body(*refs):
        x_ref, g_ref, w_ref = refs[:3]
        h_ref, o_ref, h_s = refs[-3 - (w_f32 is not None)], refs[-2 - (w_f32 is not None)], refs[-1]

        @pl.when(pl.program_id(1) == 0)
        def _():
            xv = x_ref[...]
            h = (xv * _rstd(xv) * g_ref[...]).astype(BF16)
            h_s[...] = h
            h_ref[...] = h
            if w_f32 is not None:
                refs[-2][...] = _dot(h, refs[3][...])

        acc = _dot(h_s[...], w_ref[...])
        if relu:
            acc = jnp.maximum(acc, 0.0)
        o_ref[...] = acc.astype(out_dtype)

    in_specs = [pl.BlockSpec((tm, D), lambda i, j: (i, 0)), pl.BlockSpec((1, D), lambda i, j: (0, 0)),
                pl.BlockSpec((D, tn), lambda i, j: (0, j))]
    out_specs = [pl.BlockSpec((tm, D), lambda i, j: (i, 0)), pl.BlockSpec((tm, tn), lambda i, j: (i, j))]
    out_shape = [SDS((T, D), BF16), SDS((T, N), out_dtype)]
    args = [x, g.reshape(1, D), w]
    if w_f32 is not None:
        n2 = w_f32.shape[1]
        in_specs.append(pl.BlockSpec((D, n2), lambda i, j: (0, 0)))
        out_specs.append(pl.BlockSpec((tm, n2), lambda i, j: (i, 0)))
        out_shape.append(SDS((T, n2), F32))
        args.append(w_f32)
    return pl.pallas_call(
        body, grid=(T // tm, N // tn), in_specs=in_specs, out_specs=out_specs, out_shape=out_shape,
        scratch_shapes=[pltpu.VMEM((tm, D), BF16)], compiler_params=_cp(), name=name,
    )(*args)


def _matmul_nn(a, w, *, res=None, square=False, tm, tn, tk, out_dtype, name):
    T, K = a.shape
    N = w.shape[1]
    nk = K // tk

    def body(*refs):
        a_ref, w_ref = refs[0], refs[1]
        res_ref = refs[2] if res is not None else None
        o_ref = refs[3] if res is not None else refs[2]
        k = pl.program_id(2)
        av = a_ref[...]
        if square:
            af = av.astype(F32)
            av = (af * af).astype(BF16)
        part = _dot(av, w_ref[...])

        def finish(r):
            if res_ref is not None:
                r = res_ref[...] + r
            o_ref[...] = r.astype(out_dtype)

        if nk == 1:
            finish(part)
        else:
            acc = refs[-1]

            @pl.when(k == 0)
            def _():
                acc[...] = part

            @pl.when(k > 0)
            def _():
                acc[...] += part

            @pl.when(k == nk - 1)
            def _():
                finish(acc[...])

    in_specs = [pl.BlockSpec((tm, tk), lambda i, j, k: (i, k)), pl.BlockSpec((tk, tn), lambda i, j, k: (k, j))]
    args = [a, w]
    if res is not None:
        in_specs.append(pl.BlockSpec((tm, tn), lambda i, j, k: (i, j)))
        args.append(res)
    return pl.pallas_call(
        body, grid=(T // tm, N // tn, nk), in_specs=in_specs,
        out_specs=pl.BlockSpec((tm, tn), lambda i, j, k: (i, j)), out_shape=SDS((T, N), out_dtype),
        scratch_shapes=[pltpu.VMEM((tm, tn), F32)] if nk > 1 else [], compiler_params=_cp(), name=name,
    )(*args)


def _matmul_nt(g, w, *, mul2a=None, tm, tn, name):
    T, K = g.shape
    N = w.shape[0]

    def body(*refs):
        g_ref, w_ref = refs[0], refs[1]
        o_ref = refs[-1]
        acc = _dot(g_ref[...].astype(BF16), w_ref[...], NT)
        if mul2a is not None:
            acc = acc * (2.0 * refs[2][...].astype(F32))
        o_ref[...] = acc.astype(BF16)

    in_specs = [pl.BlockSpec((tm, K), lambda i, j: (i, 0)), pl.BlockSpec((tn, K), lambda i, j: (j, 0))]
    args = [g, w]
    if mul2a is not None:
        in_specs.append(pl.BlockSpec((tm, tn), lambda i, j: (i, j)))
        args.append(mul2a)
    return pl.pallas_call(
        body, grid=(T // tm, N // tn), in_specs=in_specs,
        out_specs=pl.BlockSpec((tm, tn), lambda i, j: (i, j)), out_shape=SDS((T, N), BF16),
        compiler_params=_cp(), name=name,
    )(*args)


def _matmul_nt_rms(g, w, x, gain, dres, *, tm, tk, name):
    pieces = list(g) if isinstance(g, (list, tuple)) else [g]
    widths = [p.shape[1] for p in pieces]
    T, K = pieces[0].shape[0], sum(widths)
    D = w.shape[0]
    nk = K // tk
    nt = T // tm
    npc = len(pieces)
    assert npc == 1 or nk == 1

    def body(*refs):
        w_ref, x_ref, gain_ref = refs[npc:npc + 3]
        dres_ref = refs[npc + 3] if dres is not None else None
        n_in = npc + (4 if dres is not None else 3)
        dx_ref, dg_ref = refs[n_in], refs[n_in + 1]
        i, k = pl.program_id(0), pl.program_id(1)
        if npc == 1:
            part = _dot(refs[0][...].astype(BF16), w_ref[...], NT)
        else:
            part, off = None, 0
            for j in range(npc):
                d = _dot(refs[j][...].astype(BF16), w_ref[:, off:off + widths[j]], NT)
                part = d if part is None else part + d
                off += widths[j]

        def finish(dh):
            dx, dg = _rms_bwd(dh, x_ref[...], gain_ref[...])
            if dres_ref is not None:
                dx = dres_ref[...] + dx
            dx_ref[...] = dx

            @pl.when(i == 0)
            def _():
                dg_ref[...] = dg

            @pl.when(i > 0)
            def _():
                dg_ref[...] += dg

        if nk == 1:
            finish(part)
        else:
            acc = refs[-1]

            @pl.when(k == 0)
            def _():
                acc[...] = part

            @pl.when(k > 0)
            def _():
                acc[...] += part

            @pl.when(k == nk - 1)
            def _():
                finish(acc[...])

    g_specs = ([pl.BlockSpec((tm, tk), lambda i, k: (i, k))] if npc == 1 else
               [pl.BlockSpec((tm, wd), lambda i, k: (i, 0)) for wd in widths])
    in_specs = g_specs + [pl.BlockSpec((D, tk), lambda i, k: (0, k)),
                          pl.BlockSpec((tm, D), lambda i, k: (i, 0)), pl.BlockSpec((1, D), lambda i, k: (0, 0))]
    args = pieces + [w, x, gain.reshape(1, D)]
    if dres is not None:
        in_specs.append(pl.BlockSpec((tm, D), lambda i, k: (i, 0)))
        args.append(dres)
    return pl.pallas_call(
        body, grid=(nt, nk), in_specs=in_specs,
        out_specs=[pl.BlockSpec((tm, D), lambda i, k: (i, 0)), pl.BlockSpec((1, D), lambda i, k: (0, 0))],
        out_shape=[SDS((T, D), F32), SDS((1, D), F32)],
        scratch_shapes=[pltpu.VMEM((tm, D), F32)] if nk > 1 else [], compiler_params=_cp(), name=name,
    )(*args)


def _matmul_tn(a, g, *, square=False, bk, bn, tt, out_dtype, name):
    T, K = a.shape
    N = g.shape[1]
    nt = T // tt

    def body(a_ref, g_ref, o_ref, acc):
        t = pl.program_id(2)
        av = a_ref[...]
        if square:
            af = av.astype(F32)
            av = (af * af).astype(BF16)
        part = _dot(av, g_ref[...].astype(BF16), TN)
        if nt == 1:
            o_ref[...] = part.astype(out_dtype)
        else:
            @pl.when(t == 0)
            def _():
                acc[...] = part

            @pl.when((t > 0) & (t < nt - 1))
            def _():
                acc[...] += part

            @pl.when(t == nt - 1)
            def _():
                o_ref[...] = (acc[...] + part).astype(out_dtype)

    return pl.pallas_call(
        body, grid=(K // bk, N // bn, nt),
        in_specs=[pl.BlockSpec((tt, bk), lambda i, j, t: (t, i)), pl.BlockSpec((tt, bn), lambda i, j, t: (t, j))],
        out_specs=pl.BlockSpec((bk, bn), lambda i, j, t: (i, j)), out_shape=SDS((K, N), out_dtype),
        scratch_shapes=[pltpu.VMEM((bk, bn), F32)], compiler_params=_cp(), name=name,
    )(a, g)


def _down_loss(act, w_down, x2, g_final, target, *, tm, name):
    T, D = x2.shape
    F = act.shape[1]

    def body(a_ref, w_ref, x2_ref, g_ref, t_ref, dx_ref, dg_ref, loss_ref):
        i = pl.program_id(0)
        af = a_ref[...].astype(F32)
        xv, g = x2_ref[...] + _dot((af * af).astype(BF16), w_ref[...]), g_ref[...]
        r = _rstd(xv)
        xhat = xv * r
        diff = xhat * g - t_ref[...]
        part = 0.5 * jnp.sum(jnp.mean(diff * diff, axis=-1, keepdims=True), axis=0, keepdims=True)
        dy = diff * (1.0 / D)
        dxhat = dy * g
        dx_ref[...] = r * (dxhat - xhat * jnp.mean(dxhat * xhat, axis=-1, keepdims=True))
        dg = jnp.sum(dy * xhat, axis=0, keepdims=True)
        lp = jnp.broadcast_to(part, loss_ref.shape)

        @pl.when(i == 0)
        def _():
            dg_ref[...] = dg
            loss_ref[...] = lp

        @pl.when(i > 0)
        def _():
            dg_ref[...] += dg
            loss_ref[...] += lp

    rows = pl.BlockSpec((tm, D), lambda i: (i, 0))
    return pl.pallas_call(
        body, grid=(T // tm,),
        in_specs=[pl.BlockSpec((tm, F), lambda i: (i, 0)), pl.BlockSpec((F, D), lambda i: (0, 0)), rows,
                  pl.BlockSpec((1, D), lambda i: (0, 0)), rows],
        out_specs=[rows, pl.BlockSpec((1, D), lambda i: (0, 0)), pl.BlockSpec((8, 128), lambda i: (0, 0))],
        out_shape=[SDS((T, D), F32), SDS((1, D), F32), SDS((8, 128), F32)],
        compiler_params=_cp(), name=name,
    )(act, w_down, x2, g_final.reshape(1, D), target)


def _head_lanes(shape, width):
    return lax.broadcasted_iota(jnp.int32, shape, len(shape) - 1) // width


def _gate_fwd(gate, b_pad, *, B, S, name):
    def body(g_ref, b_ref, cc_ref):
        xv = g_ref[...] + b_ref[...]
        lf = jnp.minimum(xv, 0.0) - jnp.log(1.0 + jnp.exp(-jnp.abs(xv)))
        lane = lax.broadcasted_iota(jnp.int32, lf.shape, 1)
        row = lax.broadcasted_iota(jnp.int32, lf.shape, 0)
        c = jnp.where(lane < 8, lf, 0.0)
        sh = 1
        while sh < S:
            c = c + jnp.where(row >= sh, pltpu.roll(c, sh, 0), 0.0)
            sh *= 2
        grp = _head_lanes((S, WIDTH), HEAD_DIM)
        cc = jnp.zeros((S, WIDTH), F32)
        for h in range(8):
            cc = jnp.where(grp == h, c[:, h:h + 1], cc)
        cc_ref[...] = cc

    return pl.pallas_call(
        body, grid=(B,),
        in_specs=[pl.BlockSpec((S, 128), lambda b: (b, 0)), pl.BlockSpec((1, 128), lambda b: (0, 0))],
        out_specs=pl.BlockSpec((S, WIDTH), lambda b: (b, 0)), out_shape=SDS((B * S, WIDTH), F32),
        compiler_params=_cp(), name=name,
    )(gate, b_pad)


def _gate_bwd(dcc, gate, b_pad, *, B, S, name):
    def body(dcc_ref, g_ref, b_ref, dg_ref, db_ref):
        bi = pl.program_id(0)
        dccv = dcc_ref[...]
        lane = lax.broadcasted_iota(jnp.int32, (S, 128), 1)
        row = lax.broadcasted_iota(jnp.int32, (S, 128), 0)
        dc = jnp.zeros((S, 128), F32)
        for h in range(8):
            dc = jnp.where(lane == h, dccv[:, HEAD_DIM * h:HEAD_DIM * h + 1], dc)
        sh = 1
        while sh < S:
            dc = dc + jnp.where(row < S - sh, pltpu.roll(dc, S - sh, 0), 0.0)
            sh *= 2
        xv = g_ref[...] + b_ref[...]
        dgate = jnp.where(lane < 8, dc / (1.0 + jnp.exp(xv)), 0.0)
        dg_ref[...] = dgate.astype(BF16)
        db = jnp.sum(dgate, axis=0, keepdims=True)

        @pl.when(bi == 0)
        def _():
            db_ref[...] = db

        @pl.when(bi > 0)
        def _():
            db_ref[...] += db

    return pl.pallas_call(
        body, grid=(B,),
        in_specs=[pl.BlockSpec((S, WIDTH), lambda b: (b, 0)), pl.BlockSpec((S, 128), lambda b: (b, 0)),
                  pl.BlockSpec((1, 128), lambda b: (0, 0))],
        out_specs=[pl.BlockSpec((S, 128), lambda b: (b, 0)), pl.BlockSpec((1, 128), lambda b: (0, 0))],
        out_shape=[SDS((B * S, 128), BF16), SDS((1, 128), F32)],
        compiler_params=_cp(), name=name,
    )(dcc, gate, b_pad)


_SMEM_SPEC = pl.BlockSpec(memory_space=pltpu.SMEM)


def _alibi_slopes():
    return 2.0 ** (-(jnp.arange(1, 9, dtype=F32) * (8.0 / 8)))


def _pair_masks():
    lane = lax.broadcasted_iota(jnp.int32, (1, 128), 1)
    first = lane < HEAD_DIM
    return (first.astype(BF16), (~first).astype(BF16)), first


BNT =(((2,), (2,)), ((0,), (0,)))
BNN = (((2,), (1,)), ((0,), (0,)))
BTN = (((1,), (1,)), ((0,), (0,)))


def _split3(v):
    hi = v.astype(BF16).astype(F32)
    mid = (v - hi).astype(BF16).astype(F32)
    lo = (v - hi - mid).astype(BF16).astype(F32)
    return [hi, mid, lo]


def _with_spare_lanes(base, e, cols):
    lane = lax.broadcasted_iota(jnp.int32, (1, 128), 1)
    off = HEAD_DIM * (1 - e)
    extra = jnp.zeros(base.shape, F32)
    for j, c in enumerate(cols):
        extra = jnp.where(lane == off + j, c, extra)
    return base + extra.astype(BF16)


ONES3 = [1.0, 1.0, 1.0]


def _band_bias(slope, dilation):
    qi = lax.broadcasted_iota(jnp.int32, (BLOCK, BLOCK), 0)
    kj = lax.broadcasted_iota(jnp.int32, (BLOCK, BLOCK), 1)
    cur = jnp.where(kj <= qi, (-slope * dilation) * (qi - kj).astype(F32), NEG)
    prev = jnp.where(kj >= qi, (-slope * dilation) * (qi + BLOCK - kj).astype(F32), NEG)
    return cur, prev


def _to_residue_major(dst, src_f32, dilation, nb, lead=0):
    L = nb * BLOCK
    for r in range(dilation):
        rows = src_f32[pl.ds(r, L, stride=dilation), :] if dilation > 1 else src_f32[...]
        dst[lead + r * nb:lead + (r + 1) * nb] = rows.reshape(nb, BLOCK, 128).astype(dst.dtype)


def _dil_attn_fwd(z, *, B, S, name):
    NB = S // BLOCK

    def body(slope_ref, q_ref, k_ref, v_ref, y_ref, lse_ref, qf, kf, vf, qd, kd, vd, od, ld, acc_o, acc_l):
        (m_first, m_second), first = _pair_masks()
        p = pl.program_id(1)
        qf[...] = q_ref[...].astype(F32)
        kf[...] = k_ref[...].astype(F32)
        vf[...] = v_ref[...].astype(F32)
        kd[0] = jnp.zeros((BLOCK, 128), BF16)
        vd[0] = jnp.zeros((BLOCK, 128), BF16)
        blk = lax.broadcasted_iota(jnp.int32, (NB, 1, 1), 0)

        for idx, (_, dilation) in enumerate(DIL_CONFIGS):
            nb = NB // dilation
            _to_residue_major(qd, qf, dilation, nb)
            _to_residue_major(kd, kf, dilation, nb, lead=1)
            _to_residue_major(vd, vf, dilation, nb, lead=1)
            q4, kc, vc = qd[...], kd[1:NB + 1], vd[1:NB + 1]
            outs, lses = [], []
            for e, hm in enumerate((m_first, m_second)):
                bias_cur, bias_prev = _band_bias(slope_ref[2 * p + e], dilation)
                qm = q4 * hm
                sc = _dot(qm, kc, BNT) * 0.125 + bias_cur
                m = jnp.max(sc, axis=2, keepdims=True)
                if nb > 1:
                    sp = _dot(qm, kd[0:NB], BNT) * 0.125 + jnp.where(blk % nb == 0, NEG, bias_prev)
                    m = jnp.maximum(m, jnp.max(sp, axis=2, keepdims=True))
                pc = jnp.exp(sc - m)
                l = jnp.sum(pc, axis=2, keepdims=True)
                o = _dot(pc.astype(BF16), vc, BNN)
                if nb > 1:
                    pp = jnp.exp(sp - m)
                    l = l + jnp.sum(pp, axis=2, keepdims=True)
                    o = o + _dot(pp.astype(BF16), vd[0:NB], BNN)
                outs.append(o * (1.0 / l))
                lses.append(m + jnp.log(l))
            od[...] = jnp.where(first, outs[0], outs[1])
            ld[...] = jnp.where(first, lses[0], lses[1])

            L = nb * BLOCK
            for r in range(dilation):
                rows = pl.ds(r, L, stride=dilation) if dilation > 1 else slice(None)
                o_new = od[r * nb:(r + 1) * nb].reshape(L, 128)
                l_new = ld[r * nb:(r + 1) * nb].reshape(L, 128)
                if idx == 0:
                    acc_o[rows, :] = o_new
                    acc_l[rows, :] = l_new
                else:
                    l_old = acc_l[rows, :]
                    m2 = jnp.maximum(l_old, l_new)
                    w_old, w_new = jnp.exp(l_old - m2), jnp.exp(l_new - m2)
                    tot = w_old + w_new
                    acc_o[rows, :] = (w_old * acc_o[rows, :] + w_new * o_new) * (1.0 / tot)
                    acc_l[rows, :] = m2 + jnp.log(tot)

        y_ref[...] = acc_o[...].astype(BF16)
        lse_ref[...] = acc_l[...]

    spec = lambda off: pl.BlockSpec((S, 128), lambda b, p: (b, 4 * off + p))
    ospec = pl.BlockSpec((S, 128), lambda b, p: (b, p))
    blocks = lambda n, dt: pltpu.VMEM((n, BLOCK, 128), dt)
    return pl.pallas_call(
        body, grid=(B, 4), in_specs=[_SMEM_SPEC, spec(0), spec(1), spec(2)], out_specs=[ospec, ospec],
        out_shape=[SDS((B * S, WIDTH), BF16), SDS((B * S, WIDTH), F32)],
        scratch_shapes=[pltpu.VMEM((S, 128), F32)] * 3 + [blocks(NB, BF16), blocks(NB + 1, BF16), blocks(NB + 1, BF16),
                                                         blocks(NB, F32), blocks(NB, F32)] + [pltpu.VMEM((S, 128), F32)] * 2,
        compiler_params=_cp(), name=name,
    )(_alibi_slopes(), z, z, z)


def _dil_attn_bwd(z, dy, ya, lse, *, B, S, name):
    NB = S // BLOCK

    def body(slope_ref, q_ref, k_ref, v_ref, do_ref, o_ref, lse_ref, dq_ref, dk_ref, dv_ref,
             qf, kf, vf, dof, ef, qd, dod, kd, vd, lsd, ed, dkd, dvd, dqa, dka, dva):
        (m_first, m_second), first = _pair_masks()
        p = pl.program_id(1)
        qf[...] = q_ref[...].astype(F32)
        kf[...] = k_ref[...].astype(F32)
        vf[...] = v_ref[...].astype(F32)
        dov = do_ref[...].astype(F32)
        dof[...] = dov
        prod = dov * o_ref[...].astype(F32)
        ef[...] = jnp.where(first, jnp.sum(jnp.where(first, prod, 0.0), axis=1, keepdims=True),
                            jnp.sum(jnp.where(first, 0.0, prod), axis=1, keepdims=True))
        kd[0] = jnp.zeros((BLOCK, 128), BF16)
        vd[0] = jnp.zeros((BLOCK, 128), BF16)
        blk = lax.broadcasted_iota(jnp.int32, (NB, 1, 1), 0)

        for idx, (_, dilation) in enumerate(DIL_CONFIGS):
            nb = NB // dilation
            _to_residue_major(qd, qf, dilation, nb)
            _to_residue_major(dod, dof, dilation, nb)
            _to_residue_major(kd, kf, dilation, nb, lead=1)
            _to_residue_major(vd, vf, dilation, nb, lead=1)
            _to_residue_major(lsd, lse_ref, dilation, nb)
            _to_residue_major(ed, ef, dilation, nb)
            q4, do4, kc, vc = qd[...], dod[...], kd[1:NB + 1], vd[1:NB + 1]
            dq4 = None
            dkc = dvc = dkp = dvp = None
            for e, hm in enumerate((m_first, m_second)):
                lane0 = slice(HEAD_DIM * e, HEAD_DIM * e + 1)
                bias_cur, bias_prev = _band_bias(slope_ref[2 * p + e], dilation)
                qm, dom = q4 * hm, do4 * hm
                lse_e, e_e = lsd[...][:, :, lane0], ed[...][:, :, lane0]
                pc = jnp.exp(_dot(qm, kc, BNT) * 0.125 + bias_cur - lse_e)
                dsc = (pc * (_dot(dom, vc, BNT) - e_e)).astype(BF16)
                pcb = pc.astype(BF16)
                dqe = _dot(dsc, kc, BNN)
                dkc = _dot(dsc, qm, BTN) if e == 0 else dkc + _dot(dsc, qm, BTN)
                dvc = _dot(pcb, dom, BTN) if e == 0 else dvc + _dot(pcb, dom, BTN)
                if nb > 1:
                    kp, vp = kd[0:NB], vd[0:NB]
                    pp = jnp.exp(_dot(qm, kp, BNT) * 0.125 + jnp.where(blk % nb == 0, NEG, bias_prev) - lse_e)
                    dsp = (pp * (_dot(dom, vp, BNT) - e_e)).astype(BF16)
                    ppb = pp.astype(BF16)
                    dqe = dqe + _dot(dsp, kp, BNN)
                    dkp = _dot(dsp, qm, BTN) if e == 0 else dkp + _dot(dsp, qm, BTN)
                    dvp = _dot(ppb, dom, BTN) if e == 0 else dvp + _dot(ppb, dom, BTN)
                dq4 = dqe if e == 0 else jnp.where(first, dq4, dqe)

            dkd[1:NB + 1] = dkc
            dvd[1:NB + 1] = dvc
            if nb > 1:
                dkd[1:NB] += dkp[1:NB]
                dvd[1:NB] += dvp[1:NB]
            L = nb * BLOCK
            for r in range(dilation):
                rows = pl.ds(r, L, stride=dilation) if dilation > 1 else slice(None)
                dq_r = dq4[r * nb:(r + 1) * nb].reshape(L, 128) * 0.125
                dk_r = dkd[1 + r * nb:1 + (r + 1) * nb].reshape(L, 128) * 0.125
                dv_r = dvd[1 + r * nb:1 + (r + 1) * nb].reshape(L, 128)
                if idx == 0:
                    dqa[rows, :], dka[rows, :], dva[rows, :] = dq_r, dk_r, dv_r
                else:
                    dqa[rows, :] += dq_r
                    dka[rows, :] += dk_r
                    dva[rows, :] += dv_r

        dq_ref[...] = dqa[...].astype(BF16)
        dk_ref[...] = dka[...].astype(BF16)
        dv_ref[...] = dva[...].astype(BF16)

    spec = lambda off: pl.BlockSpec((S, 128), lambda b, p: (b, 4 * off + p))
    ospec = pl.BlockSpec((S, 128), lambda b, p: (b, p))
    blocks = lambda n, dt: pltpu.VMEM((n, BLOCK, 128), dt)
    return pl.pallas_call(
        body, grid=(B, 4), in_specs=[_SMEM_SPEC, spec(0), spec(1), spec(2), ospec, ospec, ospec],
        out_specs=[ospec] * 3, out_shape=[SDS((B * S, WIDTH), BF16)] * 3,
        scratch_shapes=[pltpu.VMEM((S, 128), F32)] * 5
        + [blocks(NB, BF16), blocks(NB, BF16), blocks(NB + 1, BF16), blocks(NB + 1, BF16), blocks(NB, F32), blocks(NB, F32),
           blocks(NB + 1, F32), blocks(NB + 1, F32)] + [pltpu.VMEM((S, 128), F32)] * 3,
        compiler_params=_cp(), name=name,
    )(_alibi_slopes(), z, z, z, dy, ya, lse)


FOX_TQ = 256


def _fox_fwd(z, cc, *, B, S, name):
    def body(q_ref, k_ref, v_ref, cc_ref, o_ref, l_ref, qa, ka, va):
        (m_first, m_second), first = _pair_masks()
        ccv = cc_ref[...]
        eighth = jnp.asarray(0.125, BF16)
        for e, hm in enumerate((m_first, m_second)):
            c_e = jnp.broadcast_to(ccv[:, HEAD_DIM * e:HEAD_DIM * e + 1], (S, 128))
            qa[e] = _with_spare_lanes(q_ref[...] * hm * eighth, e, _split3(c_e) + ONES3)
            ka[e] = _with_spare_lanes(k_ref[...] * hm, e, ONES3 + _split3(-c_e))
            va[e] = _with_spare_lanes(v_ref[...] * hm, e, [1.0])
        in_block = (lax.broadcasted_iota(jnp.int32, (FOX_TQ, FOX_TQ), 1)
                    <= lax.broadcasted_iota(jnp.int32, (FOX_TQ, FOX_TQ), 0))
        for qi in range(S // FOX_TQ):
            r0, kend = qi * FOX_TQ, (qi + 1) * FOX_TQ
            outs, lses = [], []
            for e in (0, 1):
                qe = qa[e, r0:kend, :]
                s_own = jnp.where(in_block, _dot(qe, ka[e, r0:kend, :], NT), NEG)
                m = jnp.max(s_own, axis=1, keepdims=True)
                if qi > 0:
                    s_old = _dot(qe, ka[e, 0:r0, :], NT)
                    m = jnp.maximum(m, jnp.max(s_old, axis=1, keepdims=True))
                o = _dot(jnp.exp(s_own - m).astype(BF16), va[e, r0:kend, :])
                if qi > 0:
                    o = o + _dot(jnp.exp(s_old - m).astype(BF16), va[e, 0:r0, :])
                l = o[:, HEAD_DIM * (1 - e):HEAD_DIM * (1 - e) + 1]
                outs.append(o * (1.0 / l))
                lses.append(m + jnp.log(l))
            o_ref[r0:kend, :] = jnp.where(first, outs[0], outs[1]).astype(BF16)
            l_ref[r0:kend, :] = jnp.where(first, lses[0], lses[1])

    spec = lambda off: pl.BlockSpec((S, 128), lambda b, p: (b, 4 * off + p))
    pspec = pl.BlockSpec((S, 128), lambda b, p: (b, p))
    return pl.pallas_call(
        body, grid=(B, 4), in_specs=[spec(3), spec(4), spec(5), pspec], out_specs=[pspec, pspec],
        out_shape=[SDS((B * S, WIDTH), BF16), SDS((B * S, WIDTH), F32)],
        scratch_shapes=[pltpu.VMEM((2, S, 128), BF16)] * 3, compiler_params=_cp(), name=name,
    )(z, z, z, cc)


def _fox_bwd(z, dy, lse, cc, *, B, S, name):
    def body(q_ref, k_ref, v_ref, do_ref, lse_ref, cc_ref, dq_ref, dk_ref, dv_ref, dc_ref,
             qa, ka, qp, kp, vp, dp, dk_s, dv_s, dc_s):
        (m_first, m_second), first = _pair_masks()
        ccv, lsev = cc_ref[...], lse_ref[...]
        eighth = jnp.asarray(0.125, BF16)
        for e, hm in enumerate((m_first, m_second)):
            lane0 = slice(HEAD_DIM * e, HEAD_DIM * e + 1)
            c_e = jnp.broadcast_to(ccv[:, lane0], (S, 128))
            lse_e = jnp.broadcast_to(lsev[:, lane0], (S, 128))
            qp[e] = q_ref[...] * hm
            kp[e] = k_ref[...] * hm
            dp[e] = do_ref[...] * hm
            qa[e] = _with_spare_lanes(qp[e] * eighth, e, _split3(c_e - lse_e) + ONES3)
            ka[e] = _with_spare_lanes(kp[e], e, ONES3 + _split3(-c_e))
            vp[e] = v_ref[...] * hm
        dk_s[...] = jnp.zeros_like(dk_s)
        dv_s[...] = jnp.zeros_like(dv_s)
        dc_s[...] = jnp.zeros_like(dc_s)
        in_block = (lax.broadcasted_iota(jnp.int32, (FOX_TQ, FOX_TQ), 0)
                    <= lax.broadcasted_iota(jnp.int32, (FOX_TQ, FOX_TQ), 1))
        for qi in range(S // FOX_TQ):
            r0, kend = qi * FOX_TQ, (qi + 1) * FOX_TQ
            dq_t = jnp.zeros((FOX_TQ, 128), F32)
            for e in (0, 1):
                sel = first if e == 0 else ~first
                qe, dpe, qpe = qa[e, r0:kend, :], dp[e, r0:kend, :], qp[e, r0:kend, :]
                spans = [(r0, kend)] + ([(0, r0)] if qi > 0 else [])
                pts = [jnp.exp(_dot(ka[e, a:b, :], qe, NT)) for a, b in spans]
                pts[0] = jnp.where(in_block, pts[0], 0.0)
                dpts = [_dot(vp[e, a:b, :], dpe, NT) for a, b in spans]
                num = sum(jnp.sum(pt * dpt, axis=0, keepdims=True) for pt, dpt in zip(pts, dpts))
                den = sum(jnp.sum(pt, axis=0, keepdims=True) for pt in pts)
                mean = num / den
                for (a, b), pt, dpt in zip(spans, pts, dpts):
                    dst = pt * (dpt - mean)
                    dsb = dst.astype(BF16)
                    dv_s[a:b, :] += _dot(pt.astype(BF16), dpe)
                    dk_s[a:b, :] += _dot(dsb, qpe) * 0.125
                    dq_t = dq_t + _dot(dsb, kp[e, a:b, :], TN)
                    dc_s[a:b, :] += jnp.where(sel, -jnp.sum(dst, axis=1, keepdims=True), 0.0)
            dq_ref[r0:kend, :] = (dq_t * 0.125).astype(BF16)
        dk_ref[...] = dk_s[...].astype(BF16)
        dv_ref[...] = dv_s[...].astype(BF16)
        dc_ref[...] = dc_s[...]

    spec = lambda off: pl.BlockSpec((S, 128), lambda b, p: (b, 4 * off + p))
    pspec = pl.BlockSpec((S, 128), lambda b, p: (b, p))
    return pl.pallas_call(
        body, grid=(B, 4),
        in_specs=[spec(3), spec(4), spec(5), pl.BlockSpec((S, 128), lambda b, p: (b, 4 + p)), pspec, pspec],
        out_specs=[pspec] * 4,
        out_shape=[SDS((B * S, WIDTH), BF16)] * 3 + [SDS((B * S, WIDTH), F32)],
        scratch_shapes=[pltpu.VMEM((2, S, 128), BF16)] * 6 + [pltpu.VMEM((S, 128), F32)] * 3,
        compiler_params=_cp(), name=name,
    )(z, z, z, dy, lse, cc)


def _xattn_fwd(q, kv, *, B, S, M, tq, name):
    D = D_MODEL

    def body(q_ref, kv_ref, o_ref):
        for h in range(N_XH):
            cs = slice(XHD * h, XHD * (h + 1))
            s = _dot(q_ref[:, cs], kv_ref[:, cs], NT) * (1.0 / 16.0)
            pe = jnp.exp(s - jnp.max(s, axis=1, keepdims=True))
            l = jnp.sum(pe, axis=1, keepdims=True)
            o_ref[:, cs] = (_dot(pe.astype(BF16), kv_ref[:, D + XHD * h:D + XHD * (h + 1)]) * (1.0 / l)).astype(BF16)

    nq = S // tq
    return pl.pallas_call(
        body, grid=(B, nq),
        in_specs=[pl.BlockSpec((tq, D), lambda b, t: (b * nq + t, 0)), pl.BlockSpec((M, 2 * D), lambda b, t: (b, 0))],
        out_specs=pl.BlockSpec((tq, D), lambda b, t: (b * nq + t, 0)), out_shape=SDS((B * S, D), BF16),
        compiler_params=_cp(), name=name,
    )(q, kv)


def _xattn_bwd(q, kv, do, *, B, S, M, tq, name):
    D = D_MODEL

    def body(q_ref, kv_ref, do_ref, dq_ref, dkv_ref):
        t = pl.program_id(1)

        @pl.when(t == 0)
        def _():
            dkv_ref[...] = jnp.zeros_like(dkv_ref)

        for h in range(N_XH):
            cs = slice(XHD * h, XHD * (h + 1))
            vs = slice(D + XHD * h, D + XHD * (h + 1))
            qh, kh, vh, doh = q_ref[:, cs], kv_ref[:, cs], kv_ref[:, vs], do_ref[:, cs]
            s = _dot(qh, kh, NT) * (1.0 / 16.0)
            pe = jnp.exp(s - jnp.max(s, axis=1, keepdims=True))
            pe = pe * (1.0 / jnp.sum(pe, axis=1, keepdims=True))
            dp = _dot(doh, vh, NT)
            ds = (pe * (dp - jnp.sum(pe * dp, axis=1, keepdims=True))).astype(BF16)
            dq_ref[:, cs] = (_dot(ds, kh) * (1.0 / 16.0)).astype(BF16)
            dkv_ref[:, cs] += _dot(ds, qh, TN) * (1.0 / 16.0)
            dkv_ref[:, vs] += _dot(pe.astype(BF16), doh, TN)

    nq = S // tq
    qspec = pl.BlockSpec((tq, D), lambda b, t: (b * nq + t, 0))
    kvspec = pl.BlockSpec((M, 2 * D), lambda b, t: (b, 0))
    return pl.pallas_call(
        body, grid=(B, nq), in_specs=[qspec, kvspec, qspec], out_specs=[qspec, kvspec],
        out_shape=[SDS((B * S, D), BF16), SDS((B * M, 2 * D), F32)], compiler_params=_cp(), name=name,
    )(q, kv, do)


def _adamw(parts, w, m, v, *, tr, name):
    R, C = w.shape

    def body(p_ref, w_ref, m_ref, v_ref, g_ref, d_ref, nm_ref, nv_ref):
        g = p_ref[0].astype(F32)
        for d in range(1, N_DEV):
            g = g + p_ref[d].astype(F32)
        m2 = ADAM_B1 * m_ref[...] + (1.0 - ADAM_B1) * g
        v2 = ADAM_B2 * v_ref[...] + (1.0 - ADAM_B2) * (g * g)
        m_hat = m2 / (1.0 - ADAM_B1 ** ADAM_STEP)
        v_hat = v2 / (1.0 - ADAM_B2 ** ADAM_STEP)
        g_ref[...] = g
        d_ref[...] = -ADAM_LR * (m_hat / (jnp.sqrt(v_hat) + ADAM_EPS) + ADAM_WD * w_ref[...])
        nm_ref[...] = m2
        nv_ref[...] = v2

    spec = pl.BlockSpec((tr, C), lambda i: (i, 0))
    return pl.pallas_call(
        body, grid=(R // tr,), in_specs=[pl.BlockSpec((N_DEV, tr, C), lambda i: (0, i, 0)), spec, spec, spec],
        out_specs=[spec] * 4, out_shape=[SDS((R, C), F32)] * 4, compiler_params=_cp(), name=name,
    )(parts, w, m, v)


def _peer(k, x, y, c):
    return (1 - x if k & 4 else x, 1 - y if k & 2 else y, 1 - c if k & 1 else c)


_HBM_SPEC = pl.BlockSpec(memory_space=pltpu.HBM)
_SEM_SPEC = pl.BlockSpec(memory_space=pltpu.SEMAPHORE)
_SPLIT_EFFECT = pltpu.SideEffectType.DATAFLOW_SIDE_EFFECTING


def _split_copies(srcs, lands, send_sems, recv_sems, modes):
    x, y, c = (lax.axis_index(a) for a in AXES)
    me = 4 * x + 2 * y + c
    copies = []
    for i, md in enumerate(modes):
        for k in range(1, N_DEV):
            px, py, pc = _peer(k, x, y, c)
            src = srcs[i] if md == "gather" else srcs[i].at[4 * px + 2 * py + pc]
            j = i * (N_DEV - 1) + k - 1
            copies.append(pltpu.make_async_remote_copy(
                src_ref=src, dst_ref=lands[i].at[me], send_sem=send_sems.at[j], recv_sem=recv_sems.at[j],
                device_id=(px, py, pc), device_id_type=pl.DeviceIdType.MESH))
    return copies


def _exchange_start(arrays, modes, *, name):
    n = len(arrays)
    hbm = lambda a: pltpu.with_memory_space_constraint(a, pltpu.HBM)
    srcs = [hbm(a) for a in arrays]
    me = 4 * lax.axis_index("x") + 2 * lax.axis_index("y") + lax.axis_index("c")

    def landing(a, md):
        own = a[None] if md == "gather" else lax.dynamic_index_in_dim(a, me, 0, keepdims=True)
        return hbm(lax.dynamic_update_index_in_dim(lax.empty((N_DEV,) + own.shape[1:], a.dtype), own, me, 0))

    lands = [landing(a, md) for a, md in zip(arrays, modes)]

    def body(*refs):
        for cp in _split_copies(refs[:n], refs[n:2 * n], refs[2 * n], refs[2 * n + 1], modes):
            cp.start()
        token = refs[-1]
        token[...] = jnp.zeros_like(token)

    sems = pltpu.SemaphoreType.DMA((n * (N_DEV - 1),))
    outs = pl.pallas_call(
        body, name=name, in_specs=[_HBM_SPEC] * (2 * n),
        out_shape=(sems, sems, *[pltpu.HBM(a.shape, a.dtype) for a in srcs + lands], SDS((8, 128), F32)),
        out_specs=(_SEM_SPEC, _SEM_SPEC, *[_HBM_SPEC] * (2 * n), pl.BlockSpec(memory_space=pltpu.VMEM)),
        input_output_aliases={i: 2 + i for i in range(2 * n)},
        compiler_params=pltpu.CompilerParams(has_side_effects=_SPLIT_EFFECT),
    )(*srcs, *lands)
    return (outs[0], outs[1], outs[2:2 + n], outs[2 + n:2 + 2 * n], modes), outs[-1]


def _exchange_wait(handle, after, *, name):
    send_sems, recv_sems, srcs, lands, modes = handle
    n = len(srcs)

    def body(*refs):
        for cp in _split_copies(refs[:n], refs[n:2 * n], refs[2 * n], refs[2 * n + 1], modes):
            cp.wait_send()
            cp.wait_recv()

    outs = pl.pallas_call(
        body, name=name, in_specs=[_HBM_SPEC] * (2 * n) + [_SEM_SPEC, _SEM_SPEC, pl.BlockSpec(memory_space=pl.ANY)],
        out_shape=tuple(pltpu.HBM(a.shape, a.dtype) for a in list(srcs) + list(lands)), out_specs=tuple([_HBM_SPEC] * (2 * n)),
        input_output_aliases={i: i for i in range(2 * n)},
        compiler_params=pltpu.CompilerParams(has_side_effects=_SPLIT_EFFECT),
    )(*srcs, *lands, send_sems, recv_sems, after)
    return list(outs[n:])


def _exchange(arrays, modes, *, name):
    n = len(arrays)
    out_shape = [SDS((N_DEV,) + a.shape if md == "gather" else a.shape, a.dtype) for a, md in zip(arrays, modes)]

    def body(*refs):
        ins, outs = refs[:n], refs[n:2 * n]
        send_sems, recv_sems, local_sems = refs[2 * n:]
        x, y, c = (lax.axis_index(a) for a in AXES)
        me = 4 * x + 2 * y + c
        copies = []
        for i, md in enumerate(modes):
            src = ins[i] if md == "gather" else ins[i].at[me]
            cp = pltpu.make_async_copy(src, outs[i].at[me], local_sems.at[i])
            cp.start()
            copies.append(cp)
            for k in range(1, N_DEV):
                px, py, pc = _peer(k, x, y, c)
                src = ins[i] if md == "gather" else ins[i].at[4 * px + 2 * py + pc]
                cp = pltpu.make_async_remote_copy(
                    src_ref=src, dst_ref=outs[i].at[me], send_sem=send_sems.at[i, k - 1], recv_sem=recv_sems.at[i, k - 1],
                    device_id=(px, py, pc), device_id_type=pl.DeviceIdType.MESH)
                cp.start()
                copies.append(cp)
        for cp in copies:
            cp.wait()

    anyspec = pl.BlockSpec(memory_space=pl.ANY)
    return pl.pallas_call(
        body, in_specs=[anyspec] * n, out_specs=[anyspec] * n, out_shape=out_shape,
        scratch_shapes=[pltpu.SemaphoreType.DMA((n, N_DEV - 1)), pltpu.SemaphoreType.DMA((n, N_DEV - 1)),
                        pltpu.SemaphoreType.DMA((n,))],
        name=name,
    )(*arrays)


def _local_step(x, mem, g_mix, b_forget, g_xattn, g_mem, g_mlp, g_final, target, get_w_in, get_rest, send):
    B, S, D = x.shape
    M = mem.shape[1]
    T = B * S
    x0 = x.reshape(T, D)
    mem2 = mem.reshape(B * M, D)
    tgt = target.reshape(T, D)
    b_pad = jnp.pad(b_forget, (0, 120)).reshape(1, 128)
    after = lambda a, tok: a if tok is None else a + tok[0, 0]

    w_in_pad = get_w_in()
    h1, z, gate = _rms_matmul(x0, g_mix, w_in_pad[:, :QKV_W], tm=ROWS, tn=QKV_W, out_dtype=BF16,
                              w_f32=w_in_pad[:, QKV_W:], name="f_in")
    cc = _gate_fwd(gate, b_pad, B=B, S=S, name="f_gatecum")
    ya, lse = _dil_attn_fwd(z, B=B, S=S, name="f_dil")
    yf, lse_f = _fox_fwd(z, cc, B=B, S=S, name="f_fox")
    ymix = jnp.concatenate([ya, yf], axis=1)
    w = get_rest(ymix)
    x1 = _matmul_nn(ymix, w["w_out"], res=x0, tm=ROWS, tn=D, tk=D, out_dtype=F32, name="f_out")
    h2, q = _rms_matmul(x1, g_xattn, w["w_xq"], tm=ROWS, tn=D, out_dtype=BF16, name="f_xq")
    mn, kv = _rms_matmul(mem2, g_mem, w["w_kv"], tm=B * M, tn=D, out_dtype=BF16, name="f_xkv")
    xo = _xattn_fwd(q, kv, B=B, S=S, M=M, tq=512, name="f_xattn")
    x2 = _matmul_nn(xo, w["w_xo"], res=x1, tm=ROWS, tn=D, tk=D, out_dtype=F32, name="f_xo")
    h3, act = _rms_matmul(x2, g_mlp, w["w_up"], tm=ROWS, tn=D_FF, out_dtype=BF16, relu=True, name="f_up")
    dx3, dg_final, loss = _down_loss(act, w["w_down"], x2, g_final, tgt, tm=ROWS, name="f_down")

    du = _matmul_nt(dx3, w["w_down"], mul2a=act, tm=ROWS, tn=D_FF, name="b_dact")
    dw_down = _matmul_tn(act, dx3, square=True, bk=1024, bn=D, tt=ACC_ROWS, out_dtype=BF16, name="b_wdown")
    dw_up = _matmul_tn(h3, du, bk=D, bn=1024, tt=ACC_ROWS, out_dtype=BF16, name="b_wup")
    tok = send(dict(w_down=dw_down, w_up=dw_up))
    dx2, dg_mlp = _matmul_nt_rms(du, w["w_up"], x2, after(g_mlp, tok), dx3, tm=ROWS, tk=D_FF, name="b_dh3")
    dxo = _matmul_nt(dx2, w["w_xo"], tm=1024, tn=D, name="b_dxo")
    dw_xo = _matmul_tn(xo, dx2, bk=D, bn=D, tt=ACC_ROWS, out_dtype=BF16, name="b_wxo")
    dq, dkv = _xattn_bwd(q, kv, dxo, B=B, S=S, M=M, tq=512, name="b_xattn")
    dw_xq = _matmul_tn(h2, dq, bk=D, bn=D, tt=ACC_ROWS, out_dtype=BF16, name="b_wxq")
    dx1, dg_xattn = _matmul_nt_rms(dq, w["w_xq"], x1, g_xattn, dx2, tm=ROWS, tk=D, name="b_dh2")
    dw_kv = _matmul_tn(mn, dkv, bk=D, bn=D, tt=B * M, out_dtype=BF16, name="b_wkv")
    _, dg_mem = _matmul_nt_rms(dkv, w["w_kv"], mem2, g_mem, None, tm=min(ROWS, B * M), tk=2 * D, name="b_dmem")
    dy = _matmul_nt(dx1, w["w_out"], tm=1024, tn=D, name="b_dy")
    dw_out = _matmul_tn(ymix, dx1, bk=D, bn=D, tt=ACC_ROWS, out_dtype=BF16, name="b_wout")
    tok = send(dict(w_xo=dw_xo, w_xq=dw_xq, w_xk=dw_kv[:, :D], w_xv=dw_kv[:, D:], w_out=dw_out))
    dqf, dkf, dvf, dcc = _fox_bwd(z, dy, lse_f, cc, B=B, S=S, name="b_fox")
    dgate, db = _gate_bwd(dcc, gate, after(b_pad, tok), B=B, S=S, name="b_gate")
    dqa, dka, dva = _dil_attn_bwd(z, dy, ya, lse, B=B, S=S, name="b_dil")
    dz = [dqa, dka, dva, dqf, dkf, dvf, dgate]
    dw_in = jnp.concatenate([_matmul_tn(h1, piece, bk=D, bn=piece.shape[1], tt=ACC_ROWS, out_dtype=BF16, name=f"b_win{j}")
                             for j, piece in enumerate(dz)], axis=1)
    tok = send(dict(w_in=dw_in))
    gx, dg_mix = _matmul_nt_rms(dz, w_in_pad, x0, after(g_mix, tok), dx1, tm=ROWS, tk=IN_PAD, name="b_dh1")

    small = dict(g_mix=dg_mix, b_forget=db, g_xattn=dg_xattn, g_mem=dg_mem, g_mlp=dg_mlp, g_final=dg_final)
    return gx.reshape(B, S, D), small, loss


SMALL_ROWS = ("g_mix", "b_forget", "g_xattn", "g_mem", "g_mlp", "g_final")
COL_SHARDED = ("w_in", "w_up")


def _pack_rows(rows):
    D = D_MODEL
    rows = [jnp.pad(r.reshape(-1), (0, D - r.size)) for r in rows]
    rows += [jnp.zeros((D,), F32)] * (8 - len(rows))
    return jnp.stack(rows)


def _full(name, g):
    if name in COL_SHARDED:
        return g.transpose(1, 0, 2).reshape(g.shape[1], -1)
    return g.reshape(-1, g.shape[2])


def _blocks(name, g, shard_shape):
    if name in COL_SHARDED:
        n = shard_shape[1]
        return g[:, :n * N_DEV].reshape(g.shape[0], N_DEV, n).transpose(1, 0, 2)
    return g.reshape((N_DEV,) + shard_shape)


def kernel(x, mem, g_mix, w_in, b_forget, w_out, g_xattn, g_mem, w_xq, w_xk, w_xv, w_xo, g_mlp, w_up, w_down, g_final, loss_target, m_g_mix, m_w_in, m_b_forget, m_w_out, m_g_xattn, m_g_mem, m_w_xq, m_w_xk, m_w_xv, m_w_xo, m_g_mlp, m_w_up, m_w_down, m_g_final, v_g_mix, v_w_in, v_b_forget, v_w_out, v_g_xattn, v_g_mem, v_w_xq, v_w_xk, v_w_xv, v_w_xo, v_g_mlp, v_w_up, v_w_down, v_g_final):
    W = dict(w_in=w_in, w_out=w_out, w_xq=w_xq, w_xk=w_xk, w_xv=w_xv, w_xo=w_xo, w_up=w_up, w_down=w_down)
    Mo = dict(w_in=m_w_in, w_out=m_w_out, w_xq=m_w_xq, w_xk=m_w_xk, w_xv=m_w_xv, w_xo=m_w_xo, w_up=m_w_up, w_down=m_w_down)
    Vo = dict(w_in=v_w_in, w_out=v_w_out, w_xq=v_w_xq, w_xk=v_w_xk, w_xv=v_w_xv, w_xo=v_w_xo, w_up=v_w_up, w_down=v_w_down)
    later = [n for n in W if n != "w_in"]

    first_handle, first_token = _exchange_start([w_in.astype(BF16)], ["gather"], name="gather_in_start")
    rest_handle, rest_token = _exchange_start([W[n].astype(BF16) + first_token[0, 0].astype(BF16) for n in later],
                                              ["gather"] * len(later), name="gather_rest_start")

    def get_w_in():
        (g,) = _exchange_wait(first_handle, rest_token, name="gather_in_wait")
        return jnp.pad(_full("w_in", g), ((0, 0), (0, IN_PAD - IN_W)))

    def get_rest(after):
        full = {n: _full(n, g) for n, g in zip(later, _exchange_wait(rest_handle, after, name="gather_rest_wait"))}
        full["w_kv"] = jnp.concatenate([full.pop("w_xk"), full.pop("w_xv")], axis=1)
        return full

    sent = []

    def send(grads):
        names = list(grads)
        handle, token = _exchange_start([_blocks(n, grads[n], W[n].shape) for n in names], ["scatter"] * len(names),
                                        name=f"scatter{len(sent)}_start")
        sent.append((names, handle))
        return token

    gx, small, loss = _local_step(x, mem, g_mix, b_forget, g_xattn, g_mem, g_mlp, g_final, loss_target, get_w_in, get_rest, send)

    received = {}
    for i, (names, handle) in enumerate(sent):
        received.update(zip(names, _exchange_wait(handle, gx, name=f"scatter{i}_wait")))
    packed = _pack_rows([small[n] for n in SMALL_ROWS] + [loss[0, :1]])
    (packed_all,) = _exchange([packed], ["gather"], name="gather_small")

    rows_per_step = lambda shape: max(t for t in (128, 256, 512) if shape[0] % t == 0 and t * shape[1] <= 512 * 512)
    res = {n: _adamw(received[n], W[n], Mo[n], Vo[n], tr=rows_per_step(W[n].shape), name=f"adamw_{n}") for n in W}
    small_w = dict(g_mix=g_mix, b_forget=b_forget, g_xattn=g_xattn, g_mem=g_mem, g_mlp=g_mlp, g_final=g_final)
    small_m = dict(g_mix=m_g_mix, b_forget=m_b_forget, g_xattn=m_g_xattn, g_mem=m_g_mem, g_mlp=m_g_mlp, g_final=m_g_final)
    small_v = dict(g_mix=v_g_mix, b_forget=v_b_forget, g_xattn=v_g_xattn, g_mem=v_g_mem, g_mlp=v_g_mlp, g_final=v_g_final)
    sres = _adamw(packed_all, _pack_rows([small_w[n] for n in SMALL_ROWS]), _pack_rows([small_m[n] for n in SMALL_ROWS]),
                  _pack_rows([small_v[n] for n in SMALL_ROWS]), tr=8, name="adamw_small")
    for i, n in enumerate(SMALL_ROWS):
        res[n] = [r[i, :small_w[n].size] for r in sres]
    loss_total = sres[0][6, 0]

    order = ["g_mix", "w_in", "b_forget", "w_out", "g_xattn", "g_mem", "w_xq", "w_xk", "w_xv", "w_xo", "g_mlp", "w_up", "w_down", "g_final"]
    return (loss_total, gx, *[res[n][0] for n in order], *[res[n][1] for n in order],
            *[res[n][2] for n in order], *[res[n][3] for n in order])
```

```python
import jax
import jax.numpy as jnp
from jax import lax
from jax.experimental import pallas as pl
from jax.experimental.pallas import tpu as pltpu

F32, BF16 = jnp.float32, jnp.bfloat16
SDS = jax.ShapeDtypeStruct

D_MODEL = 1024
HEAD_DIM = 64
WIDTH = 512
QKV_W = 6 * WIDTH
IN_W = QKV_W + 8
IN_PAD = QKV_W + 128
BLOCK = 128
DIL_CONFIGS = ((128, 1), (512, 4), (2048, 16))
N_XH, XHD = 4, 256
D_FF = 4096
EPS = 1e-6
NEG = -1e30
N_DEV = 8
AXES = ("x", "y", "c")

ADAM_LR, ADAM_B1, ADAM_B2, ADAM_EPS, ADAM_WD, ADAM_STEP = 0.001, 0.9, 0.999, 1e-08, 0.01, 10

VMEM_CAP_V7X = 64 * 1024 * 1024
VMEM_LIMIT = VMEM_CAP_V7X * 7 // 8

ROWS = 512
ACC_ROWS = 2048

NT = (((1,), (1,)), ((), ()))
TN = (((0,), (0,)), ((), ()))


def _cp(**kw):
    return pltpu.CompilerParams(vmem_limit_bytes=VMEM_LIMIT, **kw)


def _dot(a, b, dims=None):
    if dims is None:
        return jnp.dot(a, b, preferred_element_type=F32)
    return lax.dot_general(a, b, dims, preferred_element_type=F32)


def _rstd(xv):
    return lax.rsqrt(jnp.mean(xv * xv, axis=-1, keepdims=True) + EPS)


def _rms_bwd(dh, xv, g):
    r = _rstd(xv)
    xhat = xv * r
    dxhat = dh * g
    dx = r * (dxhat - xhat * jnp.mean(dxhat * xhat, axis=-1, keepdims=True))
    return dx, jnp.sum(dh * xhat, axis=0, keepdims=True)


def _rms_matmul(x, g, w, *, tm, tn, out_dtype, relu=False, w_f32=None, normed=False, name):
    T, D = x.shape
    N = w.shape[1]

    def body(*refs):
        x_ref, g_ref, w_ref = refs[:3]
        h_ref, o_ref, h_s = refs[-3 - (w_f32 is not None)], refs[-2 - (w_f32 is not None)], refs[-1]

        @pl.when(pl.program_id(1) == 0)
        def _():
            xv = x_ref[...]
            h = xv if normed else (xv * _rstd(xv) * g_ref[...]).astype(BF16)
            h_s[...] = h
            h_ref[...] = h
            if w_f32 is not None:
                refs[-2][...] = _dot(h, refs[3][...])

        acc = _dot(h_s[...], w_ref[...])
        if relu:
            acc = jnp.maximum(acc, 0.0)
        o_ref[...] = acc.astype(out_dtype)

    in_specs = [pl.BlockSpec((tm, D), lambda i, j: (i, 0)), pl.BlockSpec((1, D), lambda i, j: (0, 0)),
                pl.BlockSpec((D, tn), lambda i, j: (0, j))]
    out_specs = [pl.BlockSpec((tm, D), lambda i, j: (i, 0)), pl.BlockSpec((tm, tn), lambda i, j: (i, j))]
    out_shape = [SDS((T, D), BF16), SDS((T, N), out_dtype)]
    args = [x, g.reshape(1, D), w]
    if w_f32 is not None:
        n2 = w_f32.shape[1]
        in_specs.append(pl.BlockSpec((D, n2), lambda i, j: (0, 0)))
        out_specs.append(pl.BlockSpec((tm, n2), lambda i, j: (i, 0)))
        out_shape.append(SDS((T, n2), F32))
        args.append(w_f32)
    return pl.pallas_call(
        body, grid=(T // tm, N // tn), in_specs=in_specs, out_specs=out_specs, out_shape=out_shape,
        scratch_shapes=[pltpu.VMEM((tm, D), BF16)], compiler_params=_cp(), name=name,
    )(*args)


def _rms(x, g, *, tm, name):
    T, D = x.shape

    def body(x_ref, g_ref, h_ref):
        xv = x_ref[...]
        h_ref[...] = (xv * _rstd(xv) * g_ref[...]).astype(BF16)

    rows = pl.BlockSpec((tm, D), lambda i: (i, 0))
    return pl.pallas_call(body, grid=(T // tm,), in_specs=[rows, pl.BlockSpec((1, D), lambda i: (0, 0))], out_specs=rows,
                          out_shape=SDS((T, D), BF16), compiler_params=_cp(), name=name)(x, g.reshape(1, D))


def _res_rms_matmul(a, w1, res, gain, w2, *, tm, relu=False, name):
    T, K = a.shape
    D, N = w2.shape

    def body(a_ref, w1_ref, res_ref, g_ref, w2_ref, x_ref, h_ref, o_ref):
        xv = res_ref[...] + _dot(a_ref[...], w1_ref[...])
        x_ref[...] = xv
        h = (xv * _rstd(xv) * g_ref[...]).astype(BF16)
        h_ref[...] = h
        acc = _dot(h, w2_ref[...])
        if relu:
            acc = jnp.maximum(acc, 0.0)
        o_ref[...] = acc.astype(BF16)

    rows = lambda n: pl.BlockSpec((tm, n), lambda i: (i, 0))
    whole = lambda r, c: pl.BlockSpec((r, c), lambda i: (0, 0))
    return pl.pallas_call(
        body, grid=(T // tm,), in_specs=[rows(K), whole(K, D), rows(D), whole(1, D), whole(D, N)],
        out_specs=[rows(D), rows(D), rows(N)], out_shape=[SDS((T, D), F32), SDS((T, D), BF16), SDS((T, N), BF16)],
        compiler_params=_cp(), name=name,
    )(a, w1, res, gain.reshape(1, D), w2)


def _matmul_nt(g, w, *, mul2a=None, tm, tn, name):
    T, K = g.shape
    N = w.shape[0]

    def body(*refs):
        g_ref, w_ref = refs[0], refs[1]
        o_ref = refs[-1]
        acc = _dot(g_ref[...].astype(BF16), w_ref[...], NT)
        if mul2a is not None:
            acc = acc * (2.0 * refs[2][...].astype(F32))
        o_ref[...] = acc.astype(BF16)

    in_specs = [pl.BlockSpec((tm, K), lambda i, j: (i, 0)), pl.BlockSpec((tn, K), lambda i, j: (j, 0))]
    args = [g, w]
    if mul2a is not None:
        in_specs.append(pl.BlockSpec((tm, tn), lambda i, j: (i, j)))
        args.append(mul2a)
    return pl.pallas_call(
        body, grid=(T // tm, N // tn), in_specs=in_specs,
        out_specs=pl.BlockSpec((tm, tn), lambda i, j: (i, j)), out_shape=SDS((T, N), BF16),
        compiler_params=_cp(), name=name,
    )(*args)


def _matmul_nt_rms(g, w, x, gain, dres, *, then_w=None, tm, tk, name):
    pieces = list(g) if isinstance(g, (list, tuple)) else [g]
    widths = [p.shape[1] for p in pieces]
    T, K = pieces[0].shape[0], sum(widths)
    D = w.shape[0]
    nk = K // tk
    nt = T // tm
    npc = len(pieces)
    assert npc == 1 or nk == 1
    n_in = npc + 3 + (dres is not None) + (then_w is not None)

    def body(*refs):
        w_ref, x_ref, gain_ref = refs[npc:npc + 3]
        dres_ref = refs[npc + 3] if dres is not None else None
        then_ref = refs[n_in - 1] if then_w is not None else None
        dx_ref, dg_ref = refs[n_in], refs[n_in + 1]
        i, k = pl.program_id(0), pl.program_id(1)
        if npc == 1:
            part = _dot(refs[0][...].astype(BF16), w_ref[...], NT)
        else:
            part, off = None, 0
            for j in range(npc):
                d = _dot(refs[j][...].astype(BF16), w_ref[:, off:off + widths[j]], NT)
                part = d if part is None else part + d
                off += widths[j]

        def finish(dh):
            dx, dg = _rms_bwd(dh, x_ref[...], gain_ref[...])
            if dres_ref is not None:
                dx = dres_ref[...] + dx
            dx_ref[...] = dx
            if then_ref is not None:
                refs[n_in + 2][...] = _dot(dx.astype(BF16), then_ref[...], NT).astype(BF16)

            @pl.when(i == 0)
            def _():
                dg_ref[...] = dg

            @pl.when(i > 0)
            def _():
                dg_ref[...] += dg

        if nk == 1:
            finish(part)
        else:
            acc = refs[-1]

            @pl.when(k == 0)
            def _():
                acc[...] = part

            @pl.when(k > 0)
            def _():
                acc[...] += part

            @pl.when(k == nk - 1)
            def _():
                finish(acc[...])

    g_specs = ([pl.BlockSpec((tm, tk), lambda i, k: (i, k))] if npc == 1 else
               [pl.BlockSpec((tm, wd), lambda i, k: (i, 0)) for wd in widths])
    in_specs = g_specs + [pl.BlockSpec((D, tk), lambda i, k: (0, k)),
                          pl.BlockSpec((tm, D), lambda i, k: (i, 0)), pl.BlockSpec((1, D), lambda i, k: (0, 0))]
    args = pieces + [w, x, gain.reshape(1, D)]
    out_specs = [pl.BlockSpec((tm, D), lambda i, k: (i, 0)), pl.BlockSpec((1, D), lambda i, k: (0, 0))]
    out_shape = [SDS((T, D), F32), SDS((1, D), F32)]
    if dres is not None:
        in_specs.append(pl.BlockSpec((tm, D), lambda i, k: (i, 0)))
        args.append(dres)
    if then_w is not None:
        n2 = then_w.shape[0]
        in_specs.append(pl.BlockSpec((n2, D), lambda i, k: (0, 0)))
        args.append(then_w)
        out_specs.append(pl.BlockSpec((tm, n2), lambda i, k: (i, 0)))
        out_shape.append(SDS((T, n2), BF16))
    return pl.pallas_call(
        body, grid=(nt, nk), in_specs=in_specs, out_specs=out_specs, out_shape=out_shape,
        scratch_shapes=[pltpu.VMEM((tm, D), F32)] if nk > 1 else [], compiler_params=_cp(), name=name,
    )(*args)


def _matmul_tn(a, g, *, square=False, bk, bn, tt, out_dtype, name):
    T, K = a.shape
    N = g.shape[1]
    nt = T // tt

    def body(a_ref, g_ref, o_ref, acc):
        t = pl.program_id(2)
        av = a_ref[...]
        if square:
            af = av.astype(F32)
            av = (af * af).astype(BF16)
        part = _dot(av, g_ref[...].astype(BF16), TN)
        if nt == 1:
            o_ref[...] = part.astype(out_dtype)
        else:
            @pl.when(t == 0)
            def _():
                acc[...] = part

            @pl.when((t > 0) & (t < nt - 1))
            def _():
                acc[...] += part

            @pl.when(t == nt - 1)
            def _():
                o_ref[...] = (acc[...] + part).astype(out_dtype)

    return pl.pallas_call(
        body, grid=(K // bk, N // bn, nt),
        in_specs=[pl.BlockSpec((tt, bk), lambda i, j, t: (t, i)), pl.BlockSpec((tt, bn), lambda i, j, t: (t, j))],
        out_specs=pl.BlockSpec((bk, bn), lambda i, j, t: (i, j)), out_shape=SDS((K, N), out_dtype),
        scratch_shapes=[pltpu.VMEM((bk, bn), F32)], compiler_params=_cp(), name=name,
    )(a, g)


def _down_loss(act, w_down, x2, g_final, target, *, tm, name):
    T, D = x2.shape
    F = act.shape[1]

    def body(a_ref, w_ref, x2_ref, g_ref, t_ref, dx_ref, dg_ref, loss_ref):
        i = pl.program_id(0)
        af = a_ref[...].astype(F32)
        xv, g = x2_ref[...] + _dot((af * af).astype(BF16), w_ref[...]), g_ref[...]
        r = _rstd(xv)
        xhat = xv * r
        diff = xhat * g - t_ref[...]
        part = 0.5 * jnp.sum(jnp.mean(diff * diff, axis=-1, keepdims=True), axis=0, keepdims=True)
        dy = diff * (1.0 / D)
        dxhat = dy * g
        dx_ref[...] = r * (dxhat - xhat * jnp.mean(dxhat * xhat, axis=-1, keepdims=True))
        dg = jnp.sum(dy * xhat, axis=0, keepdims=True)
        lp = jnp.broadcast_to(part, loss_ref.shape)

        @pl.when(i == 0)
        def _():
            dg_ref[...] = dg
            loss_ref[...] = lp

        @pl.when(i > 0)
        def _():
            dg_ref[...] += dg
            loss_ref[...] += lp

    rows = pl.BlockSpec((tm, D), lambda i: (i, 0))
    return pl.pallas_call(
        body, grid=(T // tm,),
        in_specs=[pl.BlockSpec((tm, F), lambda i: (i, 0)), pl.BlockSpec((F, D), lambda i: (0, 0)), rows,
                  pl.BlockSpec((1, D), lambda i: (0, 0)), rows],
        out_specs=[rows, pl.BlockSpec((1, D), lambda i: (0, 0)), pl.BlockSpec((8, 128), lambda i: (0, 0))],
        out_shape=[SDS((T, D), F32), SDS((1, D), F32), SDS((8, 128), F32)],
        compiler_params=_cp(), name=name,
    )(act, w_down, x2, g_final.reshape(1, D), target)


def _head_lanes(shape, width):
    return lax.broadcasted_iota(jnp.int32, shape, len(shape) - 1) // width


def _gate_fwd(gate, b_pad, *, B, S, name):
    def body(g_ref, b_ref, cc_ref):
        xv = g_ref[...] + b_ref[...]
        lf = jnp.minimum(xv, 0.0) - jnp.log(1.0 + jnp.exp(-jnp.abs(xv)))
        lane = lax.broadcasted_iota(jnp.int32, lf.shape, 1)
        row = lax.broadcasted_iota(jnp.int32, lf.shape, 0)
        c = jnp.where(lane < 8, lf, 0.0)
        sh = 1
        while sh < S:
            c = c + jnp.where(row >= sh, pltpu.roll(c, sh, 0), 0.0)
            sh *= 2
        grp = _head_lanes((S, WIDTH), HEAD_DIM)
        cc = jnp.zeros((S, WIDTH), F32)
        for h in range(8):
            cc = jnp.where(grp == h, c[:, h:h + 1], cc)
        cc_ref[...] = cc

    return pl.pallas_call(
        body, grid=(B,),
        in_specs=[pl.BlockSpec((S, 128), lambda b: (b, 0)), pl.BlockSpec((1, 128), lambda b: (0, 0))],
        out_specs=pl.BlockSpec((S, WIDTH), lambda b: (b, 0)), out_shape=SDS((B * S, WIDTH), F32),
        compiler_params=_cp(), name=name,
    )(gate, b_pad)


def _gate_bwd(dcc, gate, b_pad, *, B, S, name):
    def body(dcc_ref, g_ref, b_ref, dg_ref, db_ref):
        bi = pl.program_id(0)
        dccv = dcc_ref[...]
        lane = lax.broadcasted_iota(jnp.int32, (S, 128), 1)
        row = lax.broadcasted_iota(jnp.int32, (S, 128), 0)
        dc = jnp.zeros((S, 128), F32)
        for h in range(8):
            dc = jnp.where(lane == h, dccv[:, HEAD_DIM * h:HEAD_DIM * h + 1], dc)
        sh = 1
        while sh < S:
            dc = dc + jnp.where(row < S - sh, pltpu.roll(dc, S - sh, 0), 0.0)
            sh *= 2
        xv = g_ref[...] + b_ref[...]
        dgate = jnp.where(lane < 8, dc / (1.0 + jnp.exp(xv)), 0.0)
        dg_ref[...] = dgate.astype(BF16)
        db = jnp.sum(dgate, axis=0, keepdims=True)

        @pl.when(bi == 0)
        def _():
            db_ref[...] = db

        @pl.when(bi > 0)
        def _():
            db_ref[...] += db

    return pl.pallas_call(
        body, grid=(B,),
        in_specs=[pl.BlockSpec((S, WIDTH), lambda b: (b, 0)), pl.BlockSpec((S, 128), lambda b: (b, 0)),
                  pl.BlockSpec((1, 128), lambda b: (0, 0))],
        out_specs=[pl.BlockSpec((S, 128), lambda b: (b, 0)), pl.BlockSpec((1, 128), lambda b: (0, 0))],
        out_shape=[SDS((B * S, 128), BF16), SDS((1, 128), F32)],
        compiler_params=_cp(), name=name,
    )(dcc, gate, b_pad)


_SMEM_SPEC = pl.BlockSpec(memory_space=pltpu.SMEM)


def _alibi_slopes():
    return 2.0 ** (-(jnp.arange(1, 9, dtype=F32) * (8.0 / 8)))


def _pair_masks():
    lane = lax.broadcasted_iota(jnp.int32, (1, 128), 1)
    first = lane < HEAD_DIM
    return (first.astype(BF16), (~first).astype(BF16)), first


BNT =(((2,), (2,)), ((0,), (0,)))
BNN = (((2,), (1,)), ((0,), (0,)))
BTN = (((1,), (1,)), ((0,), (0,)))


def _split3(v):
    hi = v.astype(BF16).astype(F32)
    mid = (v - hi).astype(BF16).astype(F32)
    lo = (v - hi - mid).astype(BF16).astype(F32)
    return [hi, mid, lo]


def _with_spare_lanes(base, e, cols):
    lane = lax.broadcasted_iota(jnp.int32, (1, 128), 1)
    off = HEAD_DIM * (1 - e)
    extra = jnp.zeros(base.shape, F32)
    for j, c in enumerate(cols):
        extra = jnp.where(lane == off + j, c, extra)
    return base + extra.astype(BF16)


ONES3 = [1.0, 1.0, 1.0]


def _band_bias(slope, dilation):
    qi = lax.broadcasted_iota(jnp.int32, (BLOCK, BLOCK), 0)
    kj = lax.broadcasted_iota(jnp.int32, (BLOCK, BLOCK), 1)
    cur = jnp.where(kj <= qi, (-slope * dilation) * (qi - kj).astype(F32), NEG)
    prev = jnp.where(kj >= qi, (-slope * dilation) * (qi + BLOCK - kj).astype(F32), NEG)
    return cur, prev


def _to_residue_major(dst, src_f32, dilation, nb, lead=0):
    L = nb * BLOCK
    for r in range(dilation):
        rows = src_f32[pl.ds(r, L, stride=dilation), :] if dilation > 1 else src_f32[...]
        dst[lead + r * nb:lead + (r + 1) * nb] = rows.reshape(nb, BLOCK, 128).astype(dst.dtype)


def _dil_attn_fwd(z, *, B, S, name):
    NB = S // BLOCK

    def body(slope_ref, q_ref, k_ref, v_ref, y_ref, lse_ref, qf, kf, vf, qd, kd, vd, od, ld, acc_o, acc_l):
        (m_first, m_second), first = _pair_masks()
        p = pl.program_id(1)
        qf[...] = q_ref[...].astype(F32)
        kf[...] = k_ref[...].astype(F32)
        vf[...] = v_ref[...].astype(F32)
        kd[0] = jnp.zeros((BLOCK, 128), BF16)
        vd[0] = jnp.zeros((BLOCK, 128), BF16)
        blk = lax.broadcasted_iota(jnp.int32, (NB, 1, 1), 0)

        for idx, (_, dilation) in enumerate(DIL_CONFIGS):
            nb = NB // dilation
            _to_residue_major(qd, qf, dilation, nb)
            _to_residue_major(kd, kf, dilation, nb, lead=1)
            _to_residue_major(vd, vf, dilation, nb, lead=1)
            q4, kc, vc = qd[...], kd[1:NB + 1], vd[1:NB + 1]
            outs, lses = [], []
            for e, hm in enumerate((m_first, m_second)):
                bias_cur, bias_prev = _band_bias(slope_ref[2 * p + e], dilation)
                qm = q4 * hm
                sc = _dot(qm, kc, BNT) * 0.125 + bias_cur
                m = jnp.max(sc, axis=2, keepdims=True)
                if nb > 1:
                    sp = _dot(qm, kd[0:NB], BNT) * 0.125 + jnp.where(blk % nb == 0, NEG, bias_prev)
                    m = jnp.maximum(m, jnp.max(sp, axis=2, keepdims=True))
                pc = jnp.exp(sc - m)
                l = jnp.sum(pc, axis=2, keepdims=True)
                o = _dot(pc.astype(BF16), vc, BNN)
                if nb > 1:
                    pp = jnp.exp(sp - m)
                    l = l + jnp.sum(pp, axis=2, keepdims=True)
                    o = o + _dot(pp.astype(BF16), vd[0:NB], BNN)
                outs.append(o * (1.0 / l))
                lses.append(m + jnp.log(l))
            od[...] = jnp.where(first, outs[0], outs[1])
            ld[...] = jnp.where(first, lses[0], lses[1])

            L = nb * BLOCK
            for r in range(dilation):
                rows = pl.ds(r, L, stride=dilation) if dilation > 1 else slice(None)
                o_new = od[r * nb:(r + 1) * nb].reshape(L, 128)
                l_new = ld[r * nb:(r + 1) * nb].reshape(L, 128)
                if idx == 0:
                    acc_o[rows, :] = o_new
                    acc_l[rows, :] = l_new
                else:
                    l_old = acc_l[rows, :]
                    m2 = jnp.maximum(l_old, l_new)
                    w_old, w_new = jnp.exp(l_old - m2), jnp.exp(l_new - m2)
                    tot = w_old + w_new
                    acc_o[rows, :] = (w_old * acc_o[rows, :] + w_new * o_new) * (1.0 / tot)
                    acc_l[rows, :] = m2 + jnp.log(tot)

        y_ref[...] = acc_o[...].astype(BF16)
        lse_ref[...] = acc_l[...]

    spec = lambda off: pl.BlockSpec((S, 128), lambda b, p: (b, 4 * off + p))
    ospec = pl.BlockSpec((S, 128), lambda b, p: (b, p))
    blocks = lambda n, dt: pltpu.VMEM((n, BLOCK, 128), dt)
    return pl.pallas_call(
        body, grid=(B, 4), in_specs=[_SMEM_SPEC, spec(0), spec(1), spec(2)], out_specs=[ospec, ospec],
        out_shape=[SDS((B * S, WIDTH), BF16), SDS((B * S, WIDTH), F32)],
        scratch_shapes=[pltpu.VMEM((S, 128), F32)] * 3 + [blocks(NB, BF16), blocks(NB + 1, BF16), blocks(NB + 1, BF16),
                                                         blocks(NB, F32), blocks(NB, F32)] + [pltpu.VMEM((S, 128), F32)] * 2,
        compiler_params=_cp(), name=name,
    )(_alibi_slopes(), z, z, z)


def _dil_attn_bwd(z, dy, ya, lse, *, B, S, name):
    NB = S // BLOCK

    def body(slope_ref, q_ref, k_ref, v_ref, do_ref, o_ref, lse_ref, dq_ref, dk_ref, dv_ref,
             qf, kf, vf, dof, ef, qd, dod, kd, vd, lsd, ed, dkd, dvd, dqa, dka, dva):
        (m_first, m_second), first = _pair_masks()
        p = pl.program_id(1)
        qf[...] = q_ref[...].astype(F32)
        kf[...] = k_ref[...].astype(F32)
        vf[...] = v_ref[...].astype(F32)
        dov = do_ref[...].astype(F32)
        dof[...] = dov
        prod = dov * o_ref[...].astype(F32)
        ef[...] = jnp.where(first, jnp.sum(jnp.where(first, prod, 0.0), axis=1, keepdims=True),
                            jnp.sum(jnp.where(first, 0.0, prod), axis=1, keepdims=True))
        kd[0] = jnp.zeros((BLOCK, 128), BF16)
        vd[0] = jnp.zeros((BLOCK, 128), BF16)
        blk = lax.broadcasted_iota(jnp.int32, (NB, 1, 1), 0)

        for idx, (_, dilation) in enumerate(DIL_CONFIGS):
            nb = NB // dilation
            _to_residue_major(qd, qf, dilation, nb)
            _to_residue_major(dod, dof, dilation, nb)
            _to_residue_major(kd, kf, dilation, nb, lead=1)
            _to_residue_major(vd, vf, dilation, nb, lead=1)
            _to_residue_major(lsd, lse_ref, dilation, nb)
            _to_residue_major(ed, ef, dilation, nb)
            q4, do4, kc, vc = qd[...], dod[...], kd[1:NB + 1], vd[1:NB + 1]
            dq4 = None
            dkc = dvc = dkp = dvp = None
            for e, hm in enumerate((m_first, m_second)):
                lane0 = slice(HEAD_DIM * e, HEAD_DIM * e + 1)
                bias_cur, bias_prev = _band_bias(slope_ref[2 * p + e], dilation)
                qm, dom = q4 * hm, do4 * hm
                lse_e, e_e = lsd[...][:, :, lane0], ed[...][:, :, lane0]
                pc = jnp.exp(_dot(qm, kc, BNT) * 0.125 + bias_cur - lse_e)
                dsc = (pc * (_dot(dom, vc, BNT) - e_e)).astype(BF16)
                pcb = pc.astype(BF16)
                dqe = _dot(dsc, kc, BNN)
                dkc = _dot(dsc, qm, BTN) if e == 0 else dkc + _dot(dsc, qm, BTN)
                dvc = _dot(pcb, dom, BTN) if e == 0 else dvc + _dot(pcb, dom, BTN)
                if nb > 1:
                    kp, vp = kd[0:NB], vd[0:NB]
                    pp = jnp.exp(_dot(qm, kp, BNT) * 0.125 + jnp.where(blk % nb == 0, NEG, bias_prev) - lse_e)
                    dsp = (pp * (_dot(dom, vp, BNT) - e_e)).astype(BF16)
                    ppb = pp.astype(BF16)
                    dqe = dqe + _dot(dsp, kp, BNN)
                    dkp = _dot(dsp, qm, BTN) if e == 0 else dkp + _dot(dsp, qm, BTN)
                    dvp = _dot(ppb, dom, BTN) if e == 0 else dvp + _dot(ppb, dom, BTN)
                dq4 = dqe if e == 0 else jnp.where(first, dq4, dqe)

            dkd[1:NB + 1] = dkc
            dvd[1:NB + 1] = dvc
            if nb > 1:
                dkd[1:NB] += dkp[1:NB]
                dvd[1:NB] += dvp[1:NB]
            L = nb * BLOCK
            for r in range(dilation):
                rows = pl.ds(r, L, stride=dilation) if dilation > 1 else slice(None)
                dq_r = dq4[r * nb:(r + 1) * nb].reshape(L, 128) * 0.125
                dk_r = dkd[1 + r * nb:1 + (r + 1) * nb].reshape(L, 128) * 0.125
                dv_r = dvd[1 + r * nb:1 + (r + 1) * nb].reshape(L, 128)
                if idx == 0:
                    dqa[rows, :], dka[rows, :], dva[rows, :] = dq_r, dk_r, dv_r
                else:
                    dqa[rows, :] += dq_r
                    dka[rows, :] += dk_r
                    dva[rows, :] += dv_r

        dq_ref[...] = dqa[...].astype(BF16)
        dk_ref[...] = dka[...].astype(BF16)
        dv_ref[...] = dva[...].astype(BF16)

    spec = lambda off: pl.BlockSpec((S, 128), lambda b, p: (b, 4 * off + p))
    ospec = pl.BlockSpec((S, 128), lambda b, p: (b, p))
    blocks = lambda n, dt: pltpu.VMEM((n, BLOCK, 128), dt)
    return pl.pallas_call(
        body, grid=(B, 4), in_specs=[_SMEM_SPEC, spec(0), spec(1), spec(2), ospec, ospec, ospec],
        out_specs=[ospec] * 3, out_shape=[SDS((B * S, WIDTH), BF16)] * 3,
        scratch_shapes=[pltpu.VMEM((S, 128), F32)] * 5
        + [blocks(NB, BF16), blocks(NB, BF16), blocks(NB + 1, BF16), blocks(NB + 1, BF16), blocks(NB, F32), blocks(NB, F32),
           blocks(NB + 1, F32), blocks(NB + 1, F32)] + [pltpu.VMEM((S, 128), F32)] * 3,
        compiler_params=_cp(), name=name,
    )(_alibi_slopes(), z, z, z, dy, ya, lse)


FOX_TQ = 256


def _fox_fwd(z, cc, *, B, S, name):
    def body(q_ref, k_ref, v_ref, cc_ref, o_ref, l_ref, qa, ka):
        (m_first, m_second), first = _pair_masks()
        ccv = cc_ref[...]
        eighth = jnp.asarray(0.125, BF16)
        for e, hm in enumerate((m_first, m_second)):
            c_e = jnp.broadcast_to(ccv[:, HEAD_DIM * e:HEAD_DIM * e + 1], (S, 128))
            qa[e] = _with_spare_lanes(q_ref[...] * hm * eighth, e, _split3(c_e) + ONES3)
            ka[e] = _with_spare_lanes(k_ref[...] * hm, e, ONES3 + _split3(-c_e))
        for qi in range(S // FOX_TQ):
            r0, kend = qi * FOX_TQ, (qi + 1) * FOX_TQ
            vv = v_ref[0:kend, :]
            row = lax.broadcasted_iota(jnp.int32, (FOX_TQ, kend), 0) + r0
            col = lax.broadcasted_iota(jnp.int32, (FOX_TQ, kend), 1)
            causal = col <= row
            outs, lses = [], []
            for e in (0, 1):
                s = jnp.where(causal, _dot(qa[e, r0:kend, :], ka[e, 0:kend, :], NT), NEG)
                m = jnp.max(s, axis=1, keepdims=True)
                pe = jnp.exp(s - m)
                l = jnp.sum(pe, axis=1, keepdims=True)
                outs.append(_dot(pe.astype(BF16), vv) * (1.0 / l))
                lses.append(m + jnp.log(l))
            o_ref[r0:kend, :] = jnp.where(first, outs[0], outs[1]).astype(BF16)
            l_ref[r0:kend, :] = jnp.where(first, lses[0], lses[1])

    spec = lambda off: pl.BlockSpec((S, 128), lambda b, p: (b, 4 * off + p))
    pspec = pl.BlockSpec((S, 128), lambda b, p: (b, p))
    return pl.pallas_call(
        body, grid=(B, 4), in_specs=[spec(3), spec(4), spec(5), pspec], out_specs=[pspec, pspec],
        out_shape=[SDS((B * S, WIDTH), BF16), SDS((B * S, WIDTH), F32)],
        scratch_shapes=[pltpu.VMEM((2, S, 128), BF16)] * 2, compiler_params=_cp(), name=name,
    )(z, z, z, cc)


def _fox_bwd(z, dy, lse, cc, *, B, S, name):
    def body(q_ref, k_ref, v_ref, do_ref, lse_ref, cc_ref, dq_ref, dk_ref, dv_ref, dc_ref,
             qa, ka, qp, kp, vp, dp, dk_s, dv_s, dc_s):
        (m_first, m_second), first = _pair_masks()
        ccv, lsev = cc_ref[...], lse_ref[...]
        eighth = jnp.asarray(0.125, BF16)
        for e, hm in enumerate((m_first, m_second)):
            lane0 = slice(HEAD_DIM * e, HEAD_DIM * e + 1)
            c_e = jnp.broadcast_to(ccv[:, lane0], (S, 128))
            lse_e = jnp.broadcast_to(lsev[:, lane0], (S, 128))
            qp[e] = q_ref[...] * hm
            kp[e] = k_ref[...] * hm
            dp[e] = do_ref[...] * hm
            qa[e] = _with_spare_lanes(qp[e] * eighth, e, _split3(c_e - lse_e) + ONES3)
            ka[e] = _with_spare_lanes(kp[e], e, ONES3 + _split3(-c_e))
            vp[e] = v_ref[...] * hm
        dk_s[...] = jnp.zeros_like(dk_s)
        dv_s[...] = jnp.zeros_like(dv_s)
        dc_s[...] = jnp.zeros_like(dc_s)
        for qi in range(S // FOX_TQ):
            r0, kend = qi * FOX_TQ, (qi + 1) * FOX_TQ
            krow = lax.broadcasted_iota(jnp.int32, (kend, FOX_TQ), 0)
            qcol = lax.broadcasted_iota(jnp.int32, (kend, FOX_TQ), 1) + r0
            causal = krow <= qcol
            dq_t = jnp.zeros((FOX_TQ, 128), F32)
            for e in (0, 1):
                sel = first if e == 0 else ~first
                pt = jnp.where(causal, jnp.exp(_dot(ka[e, 0:kend, :], qa[e, r0:kend, :], NT)), 0.0)
                dpt = _dot(vp[e, 0:kend, :], dp[e, r0:kend, :], NT)
                mean = jnp.sum(pt * dpt, axis=0, keepdims=True) / jnp.sum(pt, axis=0, keepdims=True)
                dst = pt * (dpt - mean)
                dsb = dst.astype(BF16)
                dv_s[0:kend, :] += _dot(pt.astype(BF16), dp[e, r0:kend, :])
                dk_s[0:kend, :] += _dot(dsb, qp[e, r0:kend, :]) * 0.125
                dq_t = dq_t + _dot(dsb, kp[e, 0:kend, :], TN)
                dc_s[0:kend, :] += jnp.where(sel, -jnp.sum(dst, axis=1, keepdims=True), 0.0)
            dq_ref[r0:kend, :] = (dq_t * 0.125).astype(BF16)
        dk_ref[...] = dk_s[...].astype(BF16)
        dv_ref[...] = dv_s[...].astype(BF16)
        dc_ref[...] = dc_s[...]

    spec = lambda off: pl.BlockSpec((S, 128), lambda b, p: (b, 4 * off + p))
    pspec = pl.BlockSpec((S, 128), lambda b, p: (b, p))
    return pl.pallas_call(
        body, grid=(B, 4),
        in_specs=[spec(3), spec(4), spec(5), pl.BlockSpec((S, 128), lambda b, p: (b, 4 + p)), pspec, pspec],
        out_specs=[pspec] * 4,
        out_shape=[SDS((B * S, WIDTH), BF16)] * 3 + [SDS((B * S, WIDTH), F32)],
        scratch_shapes=[pltpu.VMEM((2, S, 128), BF16)] * 6 + [pltpu.VMEM((S, 128), F32)] * 3,
        compiler_params=_cp(), name=name,
    )(z, z, z, dy, lse, cc)


def _xattn_fwd(q, kv, *, B, S, M, tq, name):
    D = D_MODEL

    def body(q_ref, kv_ref, o_ref):
        for h in range(N_XH):
            cs = slice(XHD * h, XHD * (h + 1))
            s = _dot(q_ref[:, cs], kv_ref[:, cs], NT) * (1.0 / 16.0)
            pe = jnp.exp(s - jnp.max(s, axis=1, keepdims=True))
            l = jnp.sum(pe, axis=1, keepdims=True)
            o_ref[:, cs] = (_dot(pe.astype(BF16), kv_ref[:, D + XHD * h:D + XHD * (h + 1)]) * (1.0 / l)).astype(BF16)

    nq = S // tq
    return pl.pallas_call(
        body, grid=(B, nq),
        in_specs=[pl.BlockSpec((tq, D), lambda b, t: (b * nq + t, 0)), pl.BlockSpec((M, 2 * D), lambda b, t: (b, 0))],
        out_specs=pl.BlockSpec((tq, D), lambda b, t: (b * nq + t, 0)), out_shape=SDS((B * S, D), BF16),
        compiler_params=_cp(), name=name,
    )(q, kv)


def _xattn_bwd(q, kv, do, *, B, S, M, tq, name):
    D = D_MODEL

    def body(q_ref, kv_ref, do_ref, dq_ref, dkv_ref):
        t = pl.program_id(1)

        @pl.when(t == 0)
        def _():
            dkv_ref[...] = jnp.zeros_like(dkv_ref)

        for h in range(N_XH):
            cs = slice(XHD * h, XHD * (h + 1))
            vs = slice(D + XHD * h, D + XHD * (h + 1))
            qh, kh, vh, doh = q_ref[:, cs], kv_ref[:, cs], kv_ref[:, vs], do_ref[:, cs]
            s = _dot(qh, kh, NT) * (1.0 / 16.0)
            pe = jnp.exp(s - jnp.max(s, axis=1, keepdims=True))
            pe = pe * (1.0 / jnp.sum(pe, axis=1, keepdims=True))
            dp = _dot(doh, vh, NT)
            ds = (pe * (dp - jnp.sum(pe * dp, axis=1, keepdims=True))).astype(BF16)
            dq_ref[:, cs] = (_dot(ds, kh) * (1.0 / 16.0)).astype(BF16)
            dkv_ref[:, cs] += _dot(ds, qh, TN) * (1.0 / 16.0)
            dkv_ref[:, vs] += _dot(pe.astype(BF16), doh, TN)

    nq = S // tq
    qspec = pl.BlockSpec((tq, D), lambda b, t: (b * nq + t, 0))
    kvspec = pl.BlockSpec((M, 2 * D), lambda b, t: (b, 0))
    return pl.pallas_call(
        body, grid=(B, nq), in_specs=[qspec, kvspec, qspec], out_specs=[qspec, kvspec],
        out_shape=[SDS((B * S, D), BF16), SDS((B * M, 2 * D), F32)], compiler_params=_cp(), name=name,
    )(q, kv, do)


def _adamw(parts, w, m, v, *, tr, name):
    R, C = w.shape

    def body(p_ref, w_ref, m_ref, v_ref, g_ref, d_ref, nm_ref, nv_ref):
        g = p_ref[0].astype(F32)
        for d in range(1, N_DEV):
            g = g + p_ref[d].astype(F32)
        m2 = ADAM_B1 * m_ref[...] + (1.0 - ADAM_B1) * g
        v2 = ADAM_B2 * v_ref[...] + (1.0 - ADAM_B2) * (g * g)
        m_hat = m2 / (1.0 - ADAM_B1 ** ADAM_STEP)
        v_hat = v2 / (1.0 - ADAM_B2 ** ADAM_STEP)
        g_ref[...] = g
        d_ref[...] = -ADAM_LR * (m_hat / (jnp.sqrt(v_hat) + ADAM_EPS) + ADAM_WD * w_ref[...])
        nm_ref[...] = m2
        nv_ref[...] = v2

    spec = pl.BlockSpec((tr, C), lambda i: (i, 0))
    return pl.pallas_call(
        body, grid=(R // tr,), in_specs=[pl.BlockSpec((N_DEV, tr, C), lambda i: (0, i, 0)), spec, spec, spec],
        out_specs=[spec] * 4, out_shape=[SDS((R, C), F32)] * 4, compiler_params=_cp(), name=name,
    )(parts, w, m, v)


def _peer(k, x, y, c):
    return (1 - x if k & 4 else x, 1 - y if k & 2 else y, 1 - c if k & 1 else c)


_HBM_SPEC = pl.BlockSpec(memory_space=pltpu.HBM)
_SEM_SPEC = pl.BlockSpec(memory_space=pltpu.SEMAPHORE)
_SPLIT_EFFECT = pltpu.SideEffectType.DATAFLOW_SIDE_EFFECTING


def _split_copies(srcs, lands, send_sems, recv_sems, modes):
    x, y, c = (lax.axis_index(a) for a in AXES)
    me = 4 * x + 2 * y + c
    copies = []
    for i, md in enumerate(modes):
        for k in range(1, N_DEV):
            px, py, pc = _peer(k, x, y, c)
            src = srcs[i] if md == "gather" else srcs[i].at[4 * px + 2 * py + pc]
            j = i * (N_DEV - 1) + k - 1
            copies.append(pltpu.make_async_remote_copy(
                src_ref=src, dst_ref=lands[i].at[me], send_sem=send_sems.at[j], recv_sem=recv_sems.at[j],
                device_id=(px, py, pc), device_id_type=pl.DeviceIdType.MESH))
    return copies


def _exchange_start(arrays, modes, *, name):
    n = len(arrays)
    hbm = lambda a: pltpu.with_memory_space_constraint(a, pltpu.HBM)
    srcs = [hbm(a) for a in arrays]
    me = 4 * lax.axis_index("x") + 2 * lax.axis_index("y") + lax.axis_index("c")

    def landing(a, md):
        own = a[None] if md == "gather" else lax.dynamic_index_in_dim(a, me, 0, keepdims=True)
        return hbm(lax.dynamic_update_index_in_dim(lax.empty((N_DEV,) + own.shape[1:], a.dtype), own, me, 0))

    lands = [landing(a, md) for a, md in zip(arrays, modes)]

    def body(*refs):
        for cp in _split_copies(refs[:n], refs[n:2 * n], refs[2 * n], refs[2 * n + 1], modes):
            cp.start()
        token = refs[-1]
        token[...] = jnp.zeros_like(token)

    sems = pltpu.SemaphoreType.DMA((n * (N_DEV - 1),))
    outs = pl.pallas_call(
        body, name=name, in_specs=[_HBM_SPEC] * (2 * n),
        out_shape=(sems, sems, *[pltpu.HBM(a.shape, a.dtype) for a in srcs + lands], SDS((8, 128), F32)),
        out_specs=(_SEM_SPEC, _SEM_SPEC, *[_HBM_SPEC] * (2 * n), pl.BlockSpec(memory_space=pltpu.VMEM)),
        input_output_aliases={i: 2 + i for i in range(2 * n)},
        compiler_params=pltpu.CompilerParams(has_side_effects=_SPLIT_EFFECT),
    )(*srcs, *lands)
    return (outs[0], outs[1], outs[2:2 + n], outs[2 + n:2 + 2 * n], modes), outs[-1]


def _exchange_wait(handle, after, *, name):
    send_sems, recv_sems, srcs, lands, modes = handle
    n = len(srcs)

    def body(*refs):
        for cp in _split_copies(refs[:n], refs[n:2 * n], refs[2 * n], refs[2 * n + 1], modes):
            cp.wait_send()
            cp.wait_recv()

    outs = pl.pallas_call(
        body, name=name, in_specs=[_HBM_SPEC] * (2 * n) + [_SEM_SPEC, _SEM_SPEC, pl.BlockSpec(memory_space=pl.ANY)],
        out_shape=tuple(pltpu.HBM(a.shape, a.dtype) for a in list(srcs) + list(lands)), out_specs=tuple([_HBM_SPEC] * (2 * n)),
        input_output_aliases={i: i for i in range(2 * n)},
        compiler_params=pltpu.CompilerParams(has_side_effects=_SPLIT_EFFECT),
    )(*srcs, *lands, send_sems, recv_sems, after)
    return list(outs[n:])


def _exchange(arrays, modes, *, name):
    n = len(arrays)
    out_shape = [SDS((N_DEV,) + a.shape if md == "gather" else a.shape, a.dtype) for a, md in zip(arrays, modes)]

    def body(*refs):
        ins, outs = refs[:n], refs[n:2 * n]
        send_sems, recv_sems, local_sems = refs[2 * n:]
        x, y, c = (lax.axis_index(a) for a in AXES)
        me = 4 * x + 2 * y + c
        copies = []
        for i, md in enumerate(modes):
            src = ins[i] if md == "gather" else ins[i].at[me]
            cp = pltpu.make_async_copy(src, outs[i].at[me], local_sems.at[i])
            cp.start()
            copies.append(cp)
            for k in range(1, N_DEV):
                px, py, pc = _peer(k, x, y, c)
                src = ins[i] if md == "gather" else ins[i].at[4 * px + 2 * py + pc]
                cp = pltpu.make_async_remote_copy(
                    src_ref=src, dst_ref=outs[i].at[me], send_sem=send_sems.at[i, k - 1], recv_sem=recv_sems.at[i, k - 1],
                    device_id=(px, py, pc), device_id_type=pl.DeviceIdType.MESH)
                cp.start()
                copies.append(cp)
        for cp in copies:
            cp.wait()

    anyspec = pl.BlockSpec(memory_space=pl.ANY)
    return pl.pallas_call(
        body, in_specs=[anyspec] * n, out_specs=[anyspec] * n, out_shape=out_shape,
        scratch_shapes=[pltpu.SemaphoreType.DMA((n, N_DEV - 1)), pltpu.SemaphoreType.DMA((n, N_DEV - 1)),
                        pltpu.SemaphoreType.DMA((n,))],
        name=name,
    )(*arrays)


def _local_step(x, mem, g_mix, b_forget, g_xattn, g_mem, g_mlp, g_final, target, get_w_in, get_rest, send):
    B, S, D = x.shape
    M = mem.shape[1]
    T = B * S
    x0 = x.reshape(T, D)
    mem2 = mem.reshape(B * M, D)
    tgt = target.reshape(T, D)
    b_pad = jnp.pad(b_forget, (0, 120)).reshape(1, 128)
    after = lambda a, tok: a if tok is None else a + tok[0, 0]

    h1 = _rms(x0, g_mix, tm=1024, name="f_norm")
    w_in_pad = get_w_in(h1)
    _, z, gate = _rms_matmul(h1, g_mix, w_in_pad[:, :QKV_W], tm=ROWS, tn=QKV_W, out_dtype=BF16,
                             w_f32=w_in_pad[:, QKV_W:], normed=True, name="f_in")
    cc = _gate_fwd(gate, b_pad, B=B, S=S, name="f_gatecum")
    ya, lse = _dil_attn_fwd(z, B=B, S=S, name="f_dil")
    yf, lse_f = _fox_fwd(z, cc, B=B, S=S, name="f_fox")
    ymix = jnp.concatenate([ya, yf], axis=1)
    w = get_rest(ymix)
    x1, h2, q = _res_rms_matmul(ymix, w["w_out"], x0, g_xattn, w["w_xq"], tm=ROWS, name="f_out")
    mn, kv = _rms_matmul(mem2, g_mem, w["w_kv"], tm=B * M, tn=D, out_dtype=BF16, name="f_xkv")
    xo = _xattn_fwd(q, kv, B=B, S=S, M=M, tq=512, name="f_xattn")
    x2, h3, act = _res_rms_matmul(xo, w["w_xo"], x1, g_mlp, w["w_up"], tm=ROWS // 2, relu=True, name="f_xo")
    dx3, dg_final, loss = _down_loss(act, w["w_down"], x2, g_final, tgt, tm=ROWS, name="f_down")

    du = _matmul_nt(dx3, w["w_down"], mul2a=act, tm=ROWS, tn=D_FF, name="b_dact")
    dw_down = _matmul_tn(act, dx3, square=True, bk=1024, bn=D, tt=ACC_ROWS, out_dtype=BF16, name="b_wdown")
    dw_up = _matmul_tn(h3, du, bk=D, bn=1024, tt=ACC_ROWS, out_dtype=BF16, name="b_wup")
    tok = send(dict(w_down=dw_down, w_up=dw_up))
    dx2, dg_mlp, dxo = _matmul_nt_rms(du, w["w_up"], x2, after(g_mlp, tok), dx3, then_w=w["w_xo"], tm=ROWS // 2, tk=D_FF,
                                      name="b_dh3")
    dw_xo = _matmul_tn(xo, dx2, bk=D, bn=D, tt=ACC_ROWS, out_dtype=BF16, name="b_wxo")
    dq, dkv = _xattn_bwd(q, kv, dxo, B=B, S=S, M=M, tq=512, name="b_xattn")
    dw_xq = _matmul_tn(h2, dq, bk=D, bn=D, tt=ACC_ROWS, out_dtype=BF16, name="b_wxq")
    dx1, dg_xattn, dy = _matmul_nt_rms(dq, w["w_xq"], x1, g_xattn, dx2, then_w=w["w_out"], tm=ROWS, tk=D, name="b_dh2")
    dw_kv = _matmul_tn(mn, dkv, bk=D, bn=D, tt=B * M, out_dtype=BF16, name="b_wkv")
    _, dg_mem = _matmul_nt_rms(dkv, w["w_kv"], mem2, g_mem, None, tm=min(ROWS, B * M), tk=2 * D, name="b_dmem")
    dw_out = _matmul_tn(ymix, dx1, bk=D, bn=D, tt=ACC_ROWS, out_dtype=BF16, name="b_wout")
    tok = send(dict(w_xo=dw_xo, w_xq=dw_xq, w_xk=dw_kv[:, :D], w_xv=dw_kv[:, D:], w_out=dw_out))
    dqf, dkf, dvf, dcc = _fox_bwd(z, dy, lse_f, cc, B=B, S=S, name="b_fox")
    dgate, db = _gate_bwd(dcc, gate, after(b_pad, tok), B=B, S=S, name="b_gate")
    dqa, dka, dva = _dil_attn_bwd(z, dy, ya, lse, B=B, S=S, name="b_dil")
    dz = [dqa, dka, dva, dqf, dkf, dvf, dgate]
    dw_in = jnp.concatenate([_matmul_tn(h1, piece, bk=D, bn=piece.shape[1], tt=ACC_ROWS, out_dtype=BF16, name=f"b_win{j}")
                             for j, piece in enumerate(dz)], axis=1)
    tok = send(dict(w_in=dw_in))
    gx, dg_mix = _matmul_nt_rms(dz, w_in_pad, x0, after(g_mix, tok), dx1, tm=ROWS, tk=IN_PAD, name="b_dh1")

    small = dict(g_mix=dg_mix, b_forget=db, g_xattn=dg_xattn, g_mem=dg_mem, g_mlp=dg_mlp, g_final=dg_final)
    return gx.reshape(B, S, D), small, loss


SMALL_ROWS = ("g_mix", "b_forget", "g_xattn", "g_mem", "g_mlp", "g_final")
COL_SHARDED = ("w_in", "w_up")


def _pack_rows(rows):
    D = D_MODEL
    rows = [jnp.pad(r.reshape(-1), (0, D - r.size)) for r in rows]
    rows += [jnp.zeros((D,), F32)] * (8 - len(rows))
    return jnp.stack(rows)


def _full(name, g):
    if name in COL_SHARDED:
        return g.transpose(1, 0, 2).reshape(g.shape[1], -1)
    return g.reshape(-1, g.shape[2])


def _blocks(name, g, shard_shape):
    if name in COL_SHARDED:
        n = shard_shape[1]
        return g[:, :n * N_DEV].reshape(g.shape[0], N_DEV, n).transpose(1, 0, 2)
    return g.reshape((N_DEV,) + shard_shape)


def kernel(x, mem, g_mix, w_in, b_forget, w_out, g_xattn, g_mem, w_xq, w_xk, w_xv, w_xo, g_mlp, w_up, w_down, g_final, loss_target, m_g_mix, m_w_in, m_b_forget, m_w_out, m_g_xattn, m_g_mem, m_w_xq, m_w_xk, m_w_xv, m_w_xo, m_g_mlp, m_w_up, m_w_down, m_g_final, v_g_mix, v_w_in, v_b_forget, v_w_out, v_g_xattn, v_g_mem, v_w_xq, v_w_xk, v_w_xv, v_w_xo, v_g_mlp, v_w_up, v_w_down, v_g_final):
    W = dict(w_in=w_in, w_out=w_out, w_xq=w_xq, w_xk=w_xk, w_xv=w_xv, w_xo=w_xo, w_up=w_up, w_down=w_down)
    Mo = dict(w_in=m_w_in, w_out=m_w_out, w_xq=m_w_xq, w_xk=m_w_xk, w_xv=m_w_xv, w_xo=m_w_xo, w_up=m_w_up, w_down=m_w_down)
    Vo = dict(w_in=v_w_in, w_out=v_w_out, w_xq=v_w_xq, w_xk=v_w_xk, w_xv=v_w_xv, w_xo=v_w_xo, w_up=v_w_up, w_down=v_w_down)
    later = [n for n in W if n != "w_in"]

    first_handle, first_token = _exchange_start([w_in.astype(BF16)], ["gather"], name="gather_in_start")
    rest_handle, rest_token = _exchange_start([W[n].astype(BF16) + first_token[0, 0].astype(BF16) for n in later],
                                              ["gather"] * len(later), name="gather_rest_start")

    def get_w_in(after):
        (g,) = _exchange_wait(first_handle, after, name="gather_in_wait")
        return jnp.pad(_full("w_in", g), ((0, 0), (0, IN_PAD - IN_W)))

    def get_rest(after):
        full = {n: _full(n, g) for n, g in zip(later, _exchange_wait(rest_handle, after, name="gather_rest_wait"))}
        full["w_kv"] = jnp.concatenate([full.pop("w_xk"), full.pop("w_xv")], axis=1)
        return full

    sent = []

    def send(grads):
        names = list(grads)
        handle, token = _exchange_start([_blocks(n, grads[n], W[n].shape) for n in names], ["scatter"] * len(names),
                                        name=f"scatter{len(sent)}_start")
        sent.append((names, handle))
        return token

    gx, small, loss = _local_step(x, mem, g_mix + rest_token[0, 0], b_forget, g_xattn, g_mem, g_mlp, g_final, loss_target,
                                  get_w_in, get_rest, send)

    received = {}
    for i, (names, handle) in enumerate(sent):
        received.update(zip(names, _exchange_wait(handle, gx, name=f"scatter{i}_wait")))
    packed = _pack_rows([small[n] for n in SMALL_ROWS] + [loss[0, :1]])
    (packed_all,) = _exchange([packed], ["gather"], name="gather_small")

    rows_per_step = lambda shape: max(t for t in (128, 256, 512) if shape[0] % t == 0 and t * shape[1] <= 512 * 512)
    res = {n: _adamw(received[n], W[n], Mo[n], Vo[n], tr=rows_per_step(W[n].shape), name=f"adamw_{n}") for n in W}
    small_w = dict(g_mix=g_mix, b_forget=b_forget, g_xattn=g_xattn, g_mem=g_mem, g_mlp=g_mlp, g_final=g_final)
    small_m = dict(g_mix=m_g_mix, b_forget=m_b_forget, g_xattn=m_g_xattn, g_mem=m_g_mem, g_mlp=m_g_mlp, g_final=m_g_final)
    small_v = dict(g_mix=v_g_mix, b_forget=v_b_forget, g_xattn=v_g_xattn, g_mem=v_g_mem, g_mlp=v_g_mlp, g_final=v_g_final)
    sres = _adamw(packed_all, _pack_rows([small_w[n] for n in SMALL_ROWS]), _pack_rows([small_m[n] for n in SMALL_ROWS]),
                  _pack_rows([small_v[n] for n in SMALL_ROWS]), tr=8, name="adamw_small")
    for i, n in enumerate(SMALL_ROWS):
        res[n] = [r[i, :small_w[n].size] for r in sres]
    loss_total = sres[0][6, 0]

    order = ["g_mix", "w_in", "b_forget", "w_out", "g_xattn", "g_mem", "w_xq", "w_xk", "w_xv", "w_xo", "g_mlp", "w_up", "w_down", "g_final"]
    return (loss_total, gx, *[res[n][0] for n in order], *[res[n][1] for n in order],
            *[res[n][2] for n in order], *[res[n][3] for n in order])
```

```python
import jax
import jax.numpy as jnp
from jax import lax
from jax.experimental import pallas as pl
from jax.experimental.pallas import tpu as pltpu

F32, BF16 = jnp.float32, jnp.bfloat16
SDS = jax.ShapeDtypeStruct

D_MODEL = 1024
HEAD_DIM = 64
WIDTH = 512
QKV_W = 6 * WIDTH
IN_W = QKV_W + 8
IN_PAD = QKV_W + 128
BLOCK = 128
DIL_CONFIGS = ((128, 1), (512, 4), (2048, 16))
N_XH, XHD = 4, 256
D_FF = 4096
EPS = 1e-6
NEG = -1e30
N_DEV = 8
AXES = ("x", "y", "c")

ADAM_LR, ADAM_B1, ADAM_B2, ADAM_EPS, ADAM_WD, ADAM_STEP = 0.001, 0.9, 0.999, 1e-08, 0.01, 10

VMEM_CAP_V7X = 64 * 1024 * 1024
VMEM_LIMIT = VMEM_CAP_V7X * 7 // 8

ROWS = 512
ACC_ROWS = 2048

NT = (((1,), (1,)), ((), ()))
TN = (((0,), (0,)), ((), ()))


def _cp(**kw):
    return pltpu.CompilerParams(vmem_limit_bytes=VMEM_LIMIT, **kw)


def _dot(a, b, dims=None):
    if dims is None:
        return jnp.dot(a, b, preferred_element_type=F32)
    return lax.dot_general(a, b, dims, preferred_element_type=F32)


def _rstd(xv):
    return lax.rsqrt(jnp.mean(xv * xv, axis=-1, keepdims=True) + EPS)


def _rms_bwd(dh, xv, g):
    r = _rstd(xv)
    xhat = xv * r
    dxhat = dh * g
    dx = r * (dxhat - xhat * jnp.mean(dxhat * xhat, axis=-1, keepdims=True))
    return dx, jnp.sum(dh * xhat, axis=0, keepdims=True)


def _rms_matmul(x, g, w, *, tm, tn, out_dtype, relu=False, w_f32=None, normed=False, name):
    T, D = x.shape
    N = w.shape[1]

    def body(*refs):
        x_ref, g_ref, w_ref = refs[:3]
        h_ref, o_ref, h_s = refs[-3 - (w_f32 is not None)], refs[-2 - (w_f32 is not None)], refs[-1]

        @pl.when(pl.program_id(1) == 0)
        def _():
            xv = x_ref[...]
            h = xv if normed else (xv * _rstd(xv) * g_ref[...]).astype(BF16)
            h_s[...] = h
            h_ref[...] = h
            if w_f32 is not None:
                refs[-2][...] = _dot(h, refs[3][...])

        acc = _dot(h_s[...], w_ref[...])
        if relu:
            acc = jnp.maximum(acc, 0.0)
        o_ref[...] = acc.astype(out_dtype)

    in_specs = [pl.BlockSpec((tm, D), lambda i, j: (i, 0)), pl.BlockSpec((1, D), lambda i, j: (0, 0)),
                pl.BlockSpec((D, tn), lambda i, j: (0, j))]
    out_specs = [pl.BlockSpec((tm, D), lambda i, j: (i, 0)), pl.BlockSpec((tm, tn), lambda i, j: (i, j))]
    out_shape = [SDS((T, D), BF16), SDS((T, N), out_dtype)]
    args = [x, g.reshape(1, D), w]
    if w_f32 is not None:
        n2 = w_f32.shape[1]
        in_specs.append(pl.BlockSpec((D, n2), lambda i, j: (0, 0)))
        out_specs.append(pl.BlockSpec((tm, n2), lambda i, j: (i, 0)))
        out_shape.append(SDS((T, n2), F32))
        args.append(w_f32)
    return pl.pallas_call(
        body, grid=(T // tm, N // tn), in_specs=in_specs, out_specs=out_specs, out_shape=out_shape,
        scratch_shapes=[pltpu.VMEM((tm, D), BF16)], compiler_params=_cp(), name=name,
    )(*args)


def _rms(x, g, *, tm, name):
    T, D = x.shape

    def body(x_ref, g_ref, h_ref):
        xv = x_ref[...]
        h_ref[...] = (xv * _rstd(xv) * g_ref[...]).astype(BF16)

    rows = pl.BlockSpec((tm, D), lambda i: (i, 0))
    return pl.pallas_call(body, grid=(T // tm,), in_specs=[rows, pl.BlockSpec((1, D), lambda i: (0, 0))], out_specs=rows,
                          out_shape=SDS((T, D), BF16), compiler_params=_cp(), name=name)(x, g.reshape(1, D))


def _res_rms_matmul(a, w1, res, gain, w2, *, tm, relu=False, name):
    T, K = a.shape
    D, N = w2.shape

    def body(a_ref, w1_ref, res_ref, g_ref, w2_ref, x_ref, h_ref, o_ref):
        xv = res_ref[...] + _dot(a_ref[...], w1_ref[...])
        x_ref[...] = xv
        h = (xv * _rstd(xv) * g_ref[...]).astype(BF16)
        h_ref[...] = h
        acc = _dot(h, w2_ref[...])
        if relu:
            acc = jnp.maximum(acc, 0.0)
        o_ref[...] = acc.astype(BF16)

    rows = lambda n: pl.BlockSpec((tm, n), lambda i: (i, 0))
    whole = lambda r, c: pl.BlockSpec((r, c), lambda i: (0, 0))
    return pl.pallas_call(
        body, grid=(T // tm,), in_specs=[rows(K), whole(K, D), rows(D), whole(1, D), whole(D, N)],
        out_specs=[rows(D), rows(D), rows(N)], out_shape=[SDS((T, D), F32), SDS((T, D), BF16), SDS((T, N), BF16)],
        compiler_params=_cp(), name=name,
    )(a, w1, res, gain.reshape(1, D), w2)


def _matmul_nt(g, w, *, mul2a=None, tm, tn, name):
    T, K = g.shape
    N = w.shape[0]

    def body(*refs):
        g_ref, w_ref = refs[0], refs[1]
        o_ref = refs[-1]
        acc = _dot(g_ref[...].astype(BF16), w_ref[...], NT)
        if mul2a is not None:
            acc = acc * (2.0 * refs[2][...].astype(F32))
        o_ref[...] = acc.astype(BF16)

    in_specs = [pl.BlockSpec((tm, K), lambda i, j: (i, 0)), pl.BlockSpec((tn, K), lambda i, j: (j, 0))]
    args = [g, w]
    if mul2a is not None:
        in_specs.append(pl.BlockSpec((tm, tn), lambda i, j: (i, j)))
        args.append(mul2a)
    return pl.pallas_call(
        body, grid=(T // tm, N // tn), in_specs=in_specs,
        out_specs=pl.BlockSpec((tm, tn), lambda i, j: (i, j)), out_shape=SDS((T, N), BF16),
        compiler_params=_cp(), name=name,
    )(*args)


def _matmul_nt_rms(g, w, x, gain, dres, *, then_w=None, tm, tk, name):
    pieces = list(g) if isinstance(g, (list, tuple)) else [g]
    widths = [p.shape[1] for p in pieces]
    T, K = pieces[0].shape[0], sum(widths)
    D = w.shape[0]
    nk = K // tk
    nt = T // tm
    npc = len(pieces)
    assert npc == 1 or nk == 1
    n_in = npc + 3 + (dres is not None) + (then_w is not None)

    def body(*refs):
        w_ref, x_ref, gain_ref = refs[npc:npc + 3]
        dres_ref = refs[npc + 3] if dres is not None else None
        then_ref = refs[n_in - 1] if then_w is not None else None
        dx_ref, dg_ref = refs[n_in], refs[n_in + 1]
        i, k = pl.program_id(0), pl.program_id(1)
        if npc == 1:
            part = _dot(refs[0][...].astype(BF16), w_ref[...], NT)
        else:
            part, off = None, 0
            for j in range(npc):
                d = _dot(refs[j][...].astype(BF16), w_ref[:, off:off + widths[j]], NT)
                part = d if part is None else part + d
                off += widths[j]

        def finish(dh):
            dx, dg = _rms_bwd(dh, x_ref[...], gain_ref[...])
            if dres_ref is not None:
                dx = dres_ref[...] + dx
            dx_ref[...] = dx
            if then_ref is not None:
                refs[n_in + 2][...] = _dot(dx.astype(BF16), then_ref[...], NT).astype(BF16)

            @pl.when(i == 0)
            def _():
                dg_ref[...] = dg

            @pl.when(i > 0)
            def _():
                dg_ref[...] += dg

        if nk == 1:
            finish(part)
        else:
            acc = refs[-1]

            @pl.when(k == 0)
            def _():
                acc[...] = part

            @pl.when(k > 0)
            def _():
                acc[...] += part

            @pl.when(k == nk - 1)
            def _():
                finish(acc[...])

    g_specs = ([pl.BlockSpec((tm, tk), lambda i, k: (i, k))] if npc == 1 else
               [pl.BlockSpec((tm, wd), lambda i, k: (i, 0)) for wd in widths])
    in_specs = g_specs + [pl.BlockSpec((D, tk), lambda i, k: (0, k)),
                          pl.BlockSpec((tm, D), lambda i, k: (i, 0)), pl.BlockSpec((1, D), lambda i, k: (0, 0))]
    args = pieces + [w, x, gain.reshape(1, D)]
    out_specs = [pl.BlockSpec((tm, D), lambda i, k: (i, 0)), pl.BlockSpec((1, D), lambda i, k: (0, 0))]
    out_shape = [SDS((T, D), F32), SDS((1, D), F32)]
    if dres is not None:
        in_specs.append(pl.BlockSpec((tm, D), lambda i, k: (i, 0)))
        args.append(dres)
    if then_w is not None:
        n2 = then_w.shape[0]
        in_specs.append(pl.BlockSpec((n2, D), lambda i, k: (0, 0)))
        args.append(then_w)
        out_specs.append(pl.BlockSpec((tm, n2), lambda i, k: (i, 0)))
        out_shape.append(SDS((T, n2), BF16))
    return pl.pallas_call(
        body, grid=(nt, nk), in_specs=in_specs, out_specs=out_specs, out_shape=out_shape,
        scratch_shapes=[pltpu.VMEM((tm, D), F32)] if nk > 1 else [], compiler_params=_cp(), name=name,
    )(*args)


def _matmul_tn(a, g, *, square=False, bk, bn, tt, out_dtype, name):
    T, K = a.shape
    N = g.shape[1]
    nt = T // tt

    def body(a_ref, g_ref, o_ref, acc):
        t = pl.program_id(2)
        av = a_ref[...]
        if square:
            af = av.astype(F32)
            av = (af * af).astype(BF16)
        part = _dot(av, g_ref[...].astype(BF16), TN)
        if nt == 1:
            o_ref[...] = part.astype(out_dtype)
        else:
            @pl.when(t == 0)
            def _():
                acc[...] = part

            @pl.when((t > 0) & (t < nt - 1))
            def _():
                acc[...] += part

            @pl.when(t == nt - 1)
            def _():
                o_ref[...] = (acc[...] + part).astype(out_dtype)

    return pl.pallas_call(
        body, grid=(K // bk, N // bn, nt),
        in_specs=[pl.BlockSpec((tt, bk), lambda i, j, t: (t, i)), pl.BlockSpec((tt, bn), lambda i, j, t: (t, j))],
        out_specs=pl.BlockSpec((bk, bn), lambda i, j, t: (i, j)), out_shape=SDS((K, N), out_dtype),
        scratch_shapes=[pltpu.VMEM((bk, bn), F32)], compiler_params=_cp(), name=name,
    )(a, g)


def _matmul_tn_pieces(a, pieces, *, tt, name):
    T, K = a.shape
    widths = [p.shape[1] for p in pieces]
    W = sum(widths)
    nt = T // tt
    n = len(pieces)

    def body(*refs):
        a_ref, o_ref, acc = refs[0], refs[n + 1], refs[n + 2]
        t = pl.program_id(0)
        av = a_ref[...]
        off = 0
        for j in range(n):
            cols = slice(off, off + widths[j])
            part = _dot(av, refs[1 + j][...], TN)

            @pl.when(t == 0)
            def _():
                acc[:, cols] = part

            @pl.when(t > 0)
            def _():
                acc[:, cols] += part

            off += widths[j]

        @pl.when(t == nt - 1)
        def _():
            o_ref[...] = acc[...].astype(BF16)

    return pl.pallas_call(
        body, grid=(nt,),
        in_specs=[pl.BlockSpec((tt, K), lambda t: (t, 0))] + [pl.BlockSpec((tt, w), lambda t: (t, 0)) for w in widths],
        out_specs=pl.BlockSpec((K, W), lambda t: (0, 0)), out_shape=SDS((K, W), BF16),
        scratch_shapes=[pltpu.VMEM((K, W), F32)], compiler_params=_cp(), name=name,
    )(a, *pieces)


def _down_loss(act, w_down, x2, g_final, target, *, tm, name):
    T, D = x2.shape
    F = act.shape[1]

    def body(a_ref, w_ref, x2_ref, g_ref, t_ref, dx_ref, dg_ref, loss_ref):
        i = pl.program_id(0)
        af = a_ref[...].astype(F32)
        xv, g = x2_ref[...] + _dot((af * af).astype(BF16), w_ref[...]), g_ref[...]
        r = _rstd(xv)
        xhat = xv * r
        diff = xhat * g - t_ref[...]
        part = 0.5 * jnp.sum(jnp.mean(diff * diff, axis=-1, keepdims=True), axis=0, keepdims=True)
        dy = diff * (1.0 / D)
        dxhat = dy * g
        dx_ref[...] = r * (dxhat - xhat * jnp.mean(dxhat * xhat, axis=-1, keepdims=True))
        dg = jnp.sum(dy * xhat, axis=0, keepdims=True)
        lp = jnp.broadcast_to(part, loss_ref.shape)

        @pl.when(i == 0)
        def _():
            dg_ref[...] = dg
            loss_ref[...] = lp

        @pl.when(i > 0)
        def _():
            dg_ref[...] += dg
            loss_ref[...] += lp

    rows = pl.BlockSpec((tm, D), lambda i: (i, 0))
    return pl.pallas_call(
        body, grid=(T // tm,),
        in_specs=[pl.BlockSpec((tm, F), lambda i: (i, 0)), pl.BlockSpec((F, D), lambda i: (0, 0)), rows,
                  pl.BlockSpec((1, D), lambda i: (0, 0)), rows],
        out_specs=[rows, pl.BlockSpec((1, D), lambda i: (0, 0)), pl.BlockSpec((8, 128), lambda i: (0, 0))],
        out_shape=[SDS((T, D), F32), SDS((1, D), F32), SDS((8, 128), F32)],
        compiler_params=_cp(), name=name,
    )(act, w_down, x2, g_final.reshape(1, D), target)


def _head_lanes(shape, width):
    return lax.broadcasted_iota(jnp.int32, shape, len(shape) - 1) // width


def _gate_fwd(gate, b_pad, *, B, S, name):
    def body(g_ref, b_ref, cc_ref):
        xv = g_ref[...] + b_ref[...]
        lf = jnp.minimum(xv, 0.0) - jnp.log(1.0 + jnp.exp(-jnp.abs(xv)))
        lane = lax.broadcasted_iota(jnp.int32, lf.shape, 1)
        row = lax.broadcasted_iota(jnp.int32, lf.shape, 0)
        c = jnp.where(lane < 8, lf, 0.0)
        sh = 1
        while sh < S:
            c = c + jnp.where(row >= sh, pltpu.roll(c, sh, 0), 0.0)
            sh *= 2
        grp = _head_lanes((S, WIDTH), HEAD_DIM)
        cc = jnp.zeros((S, WIDTH), F32)
        for h in range(8):
            cc = jnp.where(grp == h, c[:, h:h + 1], cc)
        cc_ref[...] = cc

    return pl.pallas_call(
        body, grid=(B,),
        in_specs=[pl.BlockSpec((S, 128), lambda b: (b, 0)), pl.BlockSpec((1, 128), lambda b: (0, 0))],
        out_specs=pl.BlockSpec((S, WIDTH), lambda b: (b, 0)), out_shape=SDS((B * S, WIDTH), F32),
        compiler_params=_cp(), name=name,
    )(gate, b_pad)


def _gate_bwd(dcc, gate, b_pad, *, B, S, name):
    def body(dcc_ref, g_ref, b_ref, dg_ref, db_ref):
        bi = pl.program_id(0)
        dccv = dcc_ref[...]
        lane = lax.broadcasted_iota(jnp.int32, (S, 128), 1)
        row = lax.broadcasted_iota(jnp.int32, (S, 128), 0)
        dc = jnp.zeros((S, 128), F32)
        for h in range(8):
            dc = jnp.where(lane == h, dccv[:, HEAD_DIM * h:HEAD_DIM * h + 1], dc)
        sh = 1
        while sh < S:
            dc = dc + jnp.where(row < S - sh, pltpu.roll(dc, S - sh, 0), 0.0)
            sh *= 2
        xv = g_ref[...] + b_ref[...]
        dgate = jnp.where(lane < 8, dc / (1.0 + jnp.exp(xv)), 0.0)
        dg_ref[...] = dgate.astype(BF16)
        db = jnp.sum(dgate, axis=0, keepdims=True)

        @pl.when(bi == 0)
        def _():
            db_ref[...] = db

        @pl.when(bi > 0)
        def _():
            db_ref[...] += db

    return pl.pallas_call(
        body, grid=(B,),
        in_specs=[pl.BlockSpec((S, WIDTH), lambda b: (b, 0)), pl.BlockSpec((S, 128), lambda b: (b, 0)),
                  pl.BlockSpec((1, 128), lambda b: (0, 0))],
        out_specs=[pl.BlockSpec((S, 128), lambda b: (b, 0)), pl.BlockSpec((1, 128), lambda b: (0, 0))],
        out_shape=[SDS((B * S, 128), BF16), SDS((1, 128), F32)],
        compiler_params=_cp(), name=name,
    )(dcc, gate, b_pad)


_SMEM_SPEC = pl.BlockSpec(memory_space=pltpu.SMEM)


def _alibi_slopes():
    return 2.0 ** (-(jnp.arange(1, 9, dtype=F32) * (8.0 / 8)))


def _pair_masks():
    lane = lax.broadcasted_iota(jnp.int32, (1, 128), 1)
    first = lane < HEAD_DIM
    return (first.astype(BF16), (~first).astype(BF16)), first


BNT =(((2,), (2,)), ((0,), (0,)))
BNN = (((2,), (1,)), ((0,), (0,)))
BTN = (((1,), (1,)), ((0,), (0,)))


def _split3(v):
    hi = v.astype(BF16).astype(F32)
    mid = (v - hi).astype(BF16).astype(F32)
    lo = (v - hi - mid).astype(BF16).astype(F32)
    return [hi, mid, lo]


def _with_spare_lanes(base, e, cols):
    lane = lax.broadcasted_iota(jnp.int32, (1, 128), 1)
    off = HEAD_DIM * (1 - e)
    extra = jnp.zeros(base.shape, F32)
    for j, c in enumerate(cols):
        extra = jnp.where(lane == off + j, c, extra)
    return base + extra.astype(BF16)


ONES3 = [1.0, 1.0, 1.0]


def _band_bias(slope, dilation):
    qi = lax.broadcasted_iota(jnp.int32, (BLOCK, BLOCK), 0)
    kj = lax.broadcasted_iota(jnp.int32, (BLOCK, BLOCK), 1)
    cur = jnp.where(kj <= qi, (-slope * dilation) * (qi - kj).astype(F32), NEG)
    prev = jnp.where(kj >= qi, (-slope * dilation) * (qi + BLOCK - kj).astype(F32), NEG)
    return cur, prev


def _to_residue_major(dst, src_f32, dilation, nb, lead=0):
    L = nb * BLOCK
    for r in range(dilation):
        rows = src_f32[pl.ds(r, L, stride=dilation), :] if dilation > 1 else src_f32[...]
        dst[lead + r * nb:lead + (r + 1) * nb] = rows.reshape(nb, BLOCK, 128).astype(dst.dtype)


def _dil_attn_fwd(z, *, B, S, name):
    NB = S // BLOCK

    def body(slope_ref, q_ref, k_ref, v_ref, y_ref, lse_ref, qf, kf, vf, qd, kd, vd, od, ld, acc_o, acc_l):
        (m_first, m_second), first = _pair_masks()
        p = pl.program_id(1)
        qf[...] = q_ref[...].astype(F32)
        kf[...] = k_ref[...].astype(F32)
        vf[...] = v_ref[...].astype(F32)
        kd[0] = jnp.zeros((BLOCK, 128), BF16)
        vd[0] = jnp.zeros((BLOCK, 128), BF16)
        blk = lax.broadcasted_iota(jnp.int32, (NB, 1, 1), 0)

        for idx, (_, dilation) in enumerate(DIL_CONFIGS):
            nb = NB // dilation
            _to_residue_major(qd, qf, dilation, nb)
            _to_residue_major(kd, kf, dilation, nb, lead=1)
            _to_residue_major(vd, vf, dilation, nb, lead=1)
            q4, kc, vc = qd[...], kd[1:NB + 1], vd[1:NB + 1]
            outs, lses = [], []
            for e, hm in enumerate((m_first, m_second)):
                bias_cur, bias_prev = _band_bias(slope_ref[2 * p + e], dilation)
                qm = q4 * hm
                sc = _dot(qm, kc, BNT) * 0.125 + bias_cur
                m = jnp.max(sc, axis=2, keepdims=True)
                if nb > 1:
                    sp = _dot(qm, kd[0:NB], BNT) * 0.125 + jnp.where(blk % nb == 0, NEG, bias_prev)
                    m = jnp.maximum(m, jnp.max(sp, axis=2, keepdims=True))
                pc = jnp.exp(sc - m)
                l = jnp.sum(pc, axis=2, keepdims=True)
                o = _dot(pc.astype(BF16), vc, BNN)
                if nb > 1:
                    pp = jnp.exp(sp - m)
                    l = l + jnp.sum(pp, axis=2, keepdims=True)
                    o = o + _dot(pp.astype(BF16), vd[0:NB], BNN)
                outs.append(o * (1.0 / l))
                lses.append(m + jnp.log(l))
            od[...] = jnp.where(first, outs[0], outs[1])
            ld[...] = jnp.where(first, lses[0], lses[1])

            L = nb * BLOCK
            for r in range(dilation):
                rows = pl.ds(r, L, stride=dilation) if dilation > 1 else slice(None)
                o_new = od[r * nb:(r + 1) * nb].reshape(L, 128)
                l_new = ld[r * nb:(r + 1) * nb].reshape(L, 128)
                if idx == 0:
                    acc_o[rows, :] = o_new
                    acc_l[rows, :] = l_new
                else:
                    l_old = acc_l[rows, :]
                    m2 = jnp.maximum(l_old, l_new)
                    w_old, w_new = jnp.exp(l_old - m2), jnp.exp(l_new - m2)
                    tot = w_old + w_new
                    acc_o[rows, :] = (w_old * acc_o[rows, :] + w_new * o_new) * (1.0 / tot)
                    acc_l[rows, :] = m2 + jnp.log(tot)

        y_ref[...] = acc_o[...].astype(BF16)
        lse_ref[...] = acc_l[...]

    spec = lambda off: pl.BlockSpec((S, 128), lambda b, p: (b, 4 * off + p))
    ospec = pl.BlockSpec((S, 128), lambda b, p: (b, p))
    blocks = lambda n, dt: pltpu.VMEM((n, BLOCK, 128), dt)
    return pl.pallas_call(
        body, grid=(B, 4), in_specs=[_SMEM_SPEC, spec(0), spec(1), spec(2)], out_specs=[ospec, ospec],
        out_shape=[SDS((B * S, WIDTH), BF16), SDS((B * S, WIDTH), F32)],
        scratch_shapes=[pltpu.VMEM((S, 128), F32)] * 3 + [blocks(NB, BF16), blocks(NB + 1, BF16), blocks(NB + 1, BF16),
                                                         blocks(NB, F32), blocks(NB, F32)] + [pltpu.VMEM((S, 128), F32)] * 2,
        compiler_params=_cp(), name=name,
    )(_alibi_slopes(), z, z, z)


def _dil_attn_bwd(z, dy, ya, lse, *, B, S, name):
    NB = S // BLOCK

    def body(slope_ref, q_ref, k_ref, v_ref, do_ref, o_ref, lse_ref, dq_ref, dk_ref, dv_ref,
             qf, kf, vf, dof, ef, qd, dod, kd, vd, lsd, ed, dkd, dvd, dqa, dka, dva):
        (m_first, m_second), first = _pair_masks()
        p = pl.program_id(1)
        qf[...] = q_ref[...].astype(F32)
        kf[...] = k_ref[...].astype(F32)
        vf[...] = v_ref[...].astype(F32)
        dov = do_ref[...].astype(F32)
        dof[...] = dov
        prod = dov * o_ref[...].astype(F32)
        ef[...] = jnp.where(first, jnp.sum(jnp.where(first, prod, 0.0), axis=1, keepdims=True),
                            jnp.sum(jnp.where(first, 0.0, prod), axis=1, keepdims=True))
        kd[0] = jnp.zeros((BLOCK, 128), BF16)
        vd[0] = jnp.zeros((BLOCK, 128), BF16)
        blk = lax.broadcasted_iota(jnp.int32, (NB, 1, 1), 0)

        for idx, (_, dilation) in enumerate(DIL_CONFIGS):
            nb = NB // dilation
            _to_residue_major(qd, qf, dilation, nb)
            _to_residue_major(dod, dof, dilation, nb)
            _to_residue_major(kd, kf, dilation, nb, lead=1)
            _to_residue_major(vd, vf, dilation, nb, lead=1)
            _to_residue_major(lsd, lse_ref, dilation, nb)
            _to_residue_major(ed, ef, dilation, nb)
            q4, do4, kc, vc = qd[...], dod[...], kd[1:NB + 1], vd[1:NB + 1]
            dq4 = None
            dkc = dvc = dkp = dvp = None
            for e, hm in enumerate((m_first, m_second)):
                lane0 = slice(HEAD_DIM * e, HEAD_DIM * e + 1)
                bias_cur, bias_prev = _band_bias(slope_ref[2 * p + e], dilation)
                qm, dom = q4 * hm, do4 * hm
                lse_e, e_e = lsd[...][:, :, lane0], ed[...][:, :, lane0]
                pc = jnp.exp(_dot(qm, kc, BNT) * 0.125 + bias_cur - lse_e)
                dsc = (pc * (_dot(dom, vc, BNT) - e_e)).astype(BF16)
                pcb = pc.astype(BF16)
                dqe = _dot(dsc, kc, BNN)
                dkc = _dot(dsc, qm, BTN) if e == 0 else dkc + _dot(dsc, qm, BTN)
                dvc = _dot(pcb, dom, BTN) if e == 0 else dvc + _dot(pcb, dom, BTN)
                if nb > 1:
                    kp, vp = kd[0:NB], vd[0:NB]
                    pp = jnp.exp(_dot(qm, kp, BNT) * 0.125 + jnp.where(blk % nb == 0, NEG, bias_prev) - lse_e)
                    dsp = (pp * (_dot(dom, vp, BNT) - e_e)).astype(BF16)
                    ppb = pp.astype(BF16)
                    dqe = dqe + _dot(dsp, kp, BNN)
                    dkp = _dot(dsp, qm, BTN) if e == 0 else dkp + _dot(dsp, qm, BTN)
                    dvp = _dot(ppb, dom, BTN) if e == 0 else dvp + _dot(ppb, dom, BTN)
                dq4 = dqe if e == 0 else jnp.where(first, dq4, dqe)

            dkd[1:NB + 1] = dkc
            dvd[1:NB + 1] = dvc
            if nb > 1:
                dkd[1:NB] += dkp[1:NB]
                dvd[1:NB] += dvp[1:NB]
            L = nb * BLOCK
            for r in range(dilation):
                rows = pl.ds(r, L, stride=dilation) if dilation > 1 else slice(None)
                dq_r = dq4[r * nb:(r + 1) * nb].reshape(L, 128) * 0.125
                dk_r = dkd[1 + r * nb:1 + (r + 1) * nb].reshape(L, 128) * 0.125
                dv_r = dvd[1 + r * nb:1 + (r + 1) * nb].reshape(L, 128)
                if idx == 0:
                    dqa[rows, :], dka[rows, :], dva[rows, :] = dq_r, dk_r, dv_r
                else:
                    dqa[rows, :] += dq_r
                    dka[rows, :] += dk_r
                    dva[rows, :] += dv_r

        dq_ref[...] = dqa[...].astype(BF16)
        dk_ref[...] = dka[...].astype(BF16)
        dv_ref[...] = dva[...].astype(BF16)

    spec = lambda off: pl.BlockSpec((S, 128), lambda b, p: (b, 4 * off + p))
    ospec = pl.BlockSpec((S, 128), lambda b, p: (b, p))
    blocks = lambda n, dt: pltpu.VMEM((n, BLOCK, 128), dt)
    return pl.pallas_call(
        body, grid=(B, 4), in_specs=[_SMEM_SPEC, spec(0), spec(1), spec(2), ospec, ospec, ospec],
        out_specs=[ospec] * 3, out_shape=[SDS((B * S, WIDTH), BF16)] * 3,
        scratch_shapes=[pltpu.VMEM((S, 128), F32)] * 5
        + [blocks(NB, BF16), blocks(NB, BF16), blocks(NB + 1, BF16), blocks(NB + 1, BF16), blocks(NB, F32), blocks(NB, F32),
           blocks(NB + 1, F32), blocks(NB + 1, F32)] + [pltpu.VMEM((S, 128), F32)] * 3,
        compiler_params=_cp(), name=name,
    )(_alibi_slopes(), z, z, z, dy, ya, lse)


FOX_TQ = 256


def _fox_fwd(z, cc, *, B, S, name):
    def body(q_ref, k_ref, v_ref, cc_ref, o_ref, l_ref, qa, ka):
        (m_first, m_second), first = _pair_masks()
        ccv = cc_ref[...]
        eighth = jnp.asarray(0.125, BF16)
        for e, hm in enumerate((m_first, m_second)):
            c_e = jnp.broadcast_to(ccv[:, HEAD_DIM * e:HEAD_DIM * e + 1], (S, 128))
            qa[e] = _with_spare_lanes(q_ref[...] * hm * eighth, e, _split3(c_e) + ONES3)
            ka[e] = _with_spare_lanes(k_ref[...] * hm, e, ONES3 + _split3(-c_e))
        for qi in range(S // FOX_TQ):
            r0, kend = qi * FOX_TQ, (qi + 1) * FOX_TQ
            vv = v_ref[0:kend, :]
            row = lax.broadcasted_iota(jnp.int32, (FOX_TQ, kend), 0) + r0
            col = lax.broadcasted_iota(jnp.int32, (FOX_TQ, kend), 1)
            causal = col <= row
            outs, lses = [], []
            for e in (0, 1):
                s = jnp.where(causal, _dot(qa[e, r0:kend, :], ka[e, 0:kend, :], NT), NEG)
                m = jnp.max(s, axis=1, keepdims=True)
                pe = jnp.exp(s - m)
                l = jnp.sum(pe, axis=1, keepdims=True)
                outs.append(_dot(pe.astype(BF16), vv) * (1.0 / l))
                lses.append(m + jnp.log(l))
            o_ref[r0:kend, :] = jnp.where(first, outs[0], outs[1]).astype(BF16)
            l_ref[r0:kend, :] = jnp.where(first, lses[0], lses[1])

    spec = lambda off: pl.BlockSpec((S, 128), lambda b, p: (b, 4 * off + p))
    pspec = pl.BlockSpec((S, 128), lambda b, p: (b, p))
    return pl.pallas_call(
        body, grid=(B, 4), in_specs=[spec(3), spec(4), spec(5), pspec], out_specs=[pspec, pspec],
        out_shape=[SDS((B * S, WIDTH), BF16), SDS((B * S, WIDTH), F32)],
        scratch_shapes=[pltpu.VMEM((2, S, 128), BF16)] * 2, compiler_params=_cp(), name=name,
    )(z, z, z, cc)


def _fox_bwd(z, dy, lse, cc, *, B, S, name):
    def body(q_ref, k_ref, v_ref, do_ref, lse_ref, cc_ref, dq_ref, dk_ref, dv_ref, dc_ref,
             qa, ka, qp, kp, vp, dp, dk_s, dv_s, dc_s):
        (m_first, m_second), first = _pair_masks()
        ccv, lsev = cc_ref[...], lse_ref[...]
        eighth = jnp.asarray(0.125, BF16)
        for e, hm in enumerate((m_first, m_second)):
            lane0 = slice(HEAD_DIM * e, HEAD_DIM * e + 1)
            c_e = jnp.broadcast_to(ccv[:, lane0], (S, 128))
            lse_e = jnp.broadcast_to(lsev[:, lane0], (S, 128))
            qp[e] = q_ref[...] * hm
            kp[e] = k_ref[...] * hm
            dp[e] = do_ref[...] * hm
            qa[e] = _with_spare_lanes(qp[e] * eighth, e, _split3(c_e - lse_e) + ONES3)
            ka[e] = _with_spare_lanes(kp[e], e, ONES3 + _split3(-c_e))
            vp[e] = v_ref[...] * hm
        dk_s[...] = jnp.zeros_like(dk_s)
        dv_s[...] = jnp.zeros_like(dv_s)
        dc_s[...] = jnp.zeros_like(dc_s)
        for qi in range(S // FOX_TQ):
            r0, kend = qi * FOX_TQ, (qi + 1) * FOX_TQ
            krow = lax.broadcasted_iota(jnp.int32, (kend, FOX_TQ), 0)
            qcol = lax.broadcasted_iota(jnp.int32, (kend, FOX_TQ), 1) + r0
            causal = krow <= qcol
            dq_t = jnp.zeros((FOX_TQ, 128), F32)
            for e in (0, 1):
                sel = first if e == 0 else ~first
                pt = jnp.where(causal, jnp.exp(_dot(ka[e, 0:kend, :], qa[e, r0:kend, :], NT)), 0.0)
                dpt = _dot(vp[e, 0:kend, :], dp[e, r0:kend, :], NT)
                mean = jnp.sum(pt * dpt, axis=0, keepdims=True) / jnp.sum(pt, axis=0, keepdims=True)
                dst = pt * (dpt - mean)
                dsb = dst.astype(BF16)
                dv_s[0:kend, :] += _dot(pt.astype(BF16), dp[e, r0:kend, :])
                dk_s[0:kend, :] += _dot(dsb, qp[e, r0:kend, :]) * 0.125
                dq_t = dq_t + _dot(dsb, kp[e, 0:kend, :], TN)
                dc_s[0:kend, :] += jnp.where(sel, -jnp.sum(dst, axis=1, keepdims=True), 0.0)
            dq_ref[r0:kend, :] = (dq_t * 0.125).astype(BF16)
        dk_ref[...] = dk_s[...].astype(BF16)
        dv_ref[...] = dv_s[...].astype(BF16)
        dc_ref[...] = dc_s[...]

    spec = lambda off: pl.BlockSpec((S, 128), lambda b, p: (b, 4 * off + p))
    pspec = pl.BlockSpec((S, 128), lambda b, p: (b, p))
    return pl.pallas_call(
        body, grid=(B, 4),
        in_specs=[spec(3), spec(4), spec(5), pl.BlockSpec((S, 128), lambda b, p: (b, 4 + p)), pspec, pspec],
        out_specs=[pspec] * 4,
        out_shape=[SDS((B * S, WIDTH), BF16)] * 3 + [SDS((B * S, WIDTH), F32)],
        scratch_shapes=[pltpu.VMEM((2, S, 128), BF16)] * 6 + [pltpu.VMEM((S, 128), F32)] * 3,
        compiler_params=_cp(), name=name,
    )(z, z, z, dy, lse, cc)


def _xattn_fwd(q, kv, *, B, S, M, tq, name):
    D = D_MODEL

    def body(q_ref, kv_ref, o_ref):
        for h in range(N_XH):
            cs = slice(XHD * h, XHD * (h + 1))
            s = _dot(q_ref[:, cs], kv_ref[:, cs], NT) * (1.0 / 16.0)
            pe = jnp.exp(s - jnp.max(s, axis=1, keepdims=True))
            l = jnp.sum(pe, axis=1, keepdims=True)
            o_ref[:, cs] = (_dot(pe.astype(BF16), kv_ref[:, D + XHD * h:D + XHD * (h + 1)]) * (1.0 / l)).astype(BF16)

    nq = S // tq
    return pl.pallas_call(
        body, grid=(B, nq),
        in_specs=[pl.BlockSpec((tq, D), lambda b, t: (b * nq + t, 0)), pl.BlockSpec((M, 2 * D), lambda b, t: (b, 0))],
        out_specs=pl.BlockSpec((tq, D), lambda b, t: (b * nq + t, 0)), out_shape=SDS((B * S, D), BF16),
        compiler_params=_cp(), name=name,
    )(q, kv)


def _xattn_bwd(q, kv, do, *, B, S, M, tq, name):
    D = D_MODEL

    def body(q_ref, kv_ref, do_ref, dq_ref, dkv_ref):
        t = pl.program_id(1)

        @pl.when(t == 0)
        def _():
            dkv_ref[...] = jnp.zeros_like(dkv_ref)

        for h in range(N_XH):
            cs = slice(XHD * h, XHD * (h + 1))
            vs = slice(D + XHD * h, D + XHD * (h + 1))
            qh, kh, vh, doh = q_ref[:, cs], kv_ref[:, cs], kv_ref[:, vs], do_ref[:, cs]
            s = _dot(qh, kh, NT) * (1.0 / 16.0)
            pe = jnp.exp(s - jnp.max(s, axis=1, keepdims=True))
            pe = pe * (1.0 / jnp.sum(pe, axis=1, keepdims=True))
            dp = _dot(doh, vh, NT)
            ds = (pe * (dp - jnp.sum(pe * dp, axis=1, keepdims=True))).astype(BF16)
            dq_ref[:, cs] = (_dot(ds, kh) * (1.0 / 16.0)).astype(BF16)
            dkv_ref[:, cs] += _dot(ds, qh, TN) * (1.0 / 16.0)
            dkv_ref[:, vs] += _dot(pe.astype(BF16), doh, TN)

    nq = S // tq
    qspec = pl.BlockSpec((tq, D), lambda b, t: (b * nq + t, 0))
    kvspec = pl.BlockSpec((M, 2 * D), lambda b, t: (b, 0))
    return pl.pallas_call(
        body, grid=(B, nq), in_specs=[qspec, kvspec, qspec], out_specs=[qspec, kvspec],
        out_shape=[SDS((B * S, D), BF16), SDS((B * M, 2 * D), F32)], compiler_params=_cp(), name=name,
    )(q, kv, do)


def _adamw(parts, w, m, v, *, tr, name):
    R, C = w.shape

    def body(p_ref, w_ref, m_ref, v_ref, g_ref, d_ref, nm_ref, nv_ref):
        g = p_ref[0].astype(F32)
        for d in range(1, N_DEV):
            g = g + p_ref[d].astype(F32)
        m2 = ADAM_B1 * m_ref[...] + (1.0 - ADAM_B1) * g
        v2 = ADAM_B2 * v_ref[...] + (1.0 - ADAM_B2) * (g * g)
        m_hat = m2 / (1.0 - ADAM_B1 ** ADAM_STEP)
        v_hat = v2 / (1.0 - ADAM_B2 ** ADAM_STEP)
        g_ref[...] = g
        d_ref[...] = -ADAM_LR * (m_hat / (jnp.sqrt(v_hat) + ADAM_EPS) + ADAM_WD * w_ref[...])
        nm_ref[...] = m2
        nv_ref[...] = v2

    spec = pl.BlockSpec((tr, C), lambda i: (i, 0))
    return pl.pallas_call(
        body, grid=(R // tr,), in_specs=[pl.BlockSpec((N_DEV, tr, C), lambda i: (0, i, 0)), spec, spec, spec],
        out_specs=[spec] * 4, out_shape=[SDS((R, C), F32)] * 4, compiler_params=_cp(), name=name,
    )(parts, w, m, v)


def _peer(k, x, y, c):
    return (1 - x if k & 4 else x, 1 - y if k & 2 else y, 1 - c if k & 1 else c)


_HBM_SPEC = pl.BlockSpec(memory_space=pltpu.HBM)
_SEM_SPEC = pl.BlockSpec(memory_space=pltpu.SEMAPHORE)
_SPLIT_EFFECT = pltpu.SideEffectType.DATAFLOW_SIDE_EFFECTING


def _split_copies(srcs, lands, send_sems, recv_sems, modes):
    x, y, c = (lax.axis_index(a) for a in AXES)
    me = 4 * x + 2 * y + c
    copies = []
    for i, md in enumerate(modes):
        for k in range(1, N_DEV):
            px, py, pc = _peer(k, x, y, c)
            src = srcs[i] if md == "gather" else srcs[i].at[4 * px + 2 * py + pc]
            j = i * (N_DEV - 1) + k - 1
            copies.append(pltpu.make_async_remote_copy(
                src_ref=src, dst_ref=lands[i].at[me], send_sem=send_sems.at[j], recv_sem=recv_sems.at[j],
                device_id=(px, py, pc), device_id_type=pl.DeviceIdType.MESH))
    return copies


def _exchange_start(arrays, modes, *, name):
    n = len(arrays)
    hbm = lambda a: pltpu.with_memory_space_constraint(a, pltpu.HBM)
    srcs = [hbm(a) for a in arrays]
    me = 4 * lax.axis_index("x") + 2 * lax.axis_index("y") + lax.axis_index("c")

    def landing(a, md):
        own = a[None] if md == "gather" else lax.dynamic_index_in_dim(a, me, 0, keepdims=True)
        return hbm(lax.dynamic_update_index_in_dim(lax.empty((N_DEV,) + own.shape[1:], a.dtype), own, me, 0))

    lands = [landing(a, md) for a, md in zip(arrays, modes)]

    def body(*refs):
        for cp in _split_copies(refs[:n], refs[n:2 * n], refs[2 * n], refs[2 * n + 1], modes):
            cp.start()
        token = refs[-1]
        token[...] = jnp.zeros_like(token)

    sems = pltpu.SemaphoreType.DMA((n * (N_DEV - 1),))
    outs = pl.pallas_call(
        body, name=name, in_specs=[_HBM_SPEC] * (2 * n),
        out_shape=(sems, sems, *[pltpu.HBM(a.shape, a.dtype) for a in srcs + lands], SDS((8, 128), F32)),
        out_specs=(_SEM_SPEC, _SEM_SPEC, *[_HBM_SPEC] * (2 * n), pl.BlockSpec(memory_space=pltpu.VMEM)),
        input_output_aliases={i: 2 + i for i in range(2 * n)},
        compiler_params=pltpu.CompilerParams(has_side_effects=_SPLIT_EFFECT),
    )(*srcs, *lands)
    return (outs[0], outs[1], outs[2:2 + n], outs[2 + n:2 + 2 * n], modes), outs[-1]


def _exchange_wait(handle, after, *, name):
    send_sems, recv_sems, srcs, lands, modes = handle
    n = len(srcs)

    def body(*refs):
        for cp in _split_copies(refs[:n], refs[n:2 * n], refs[2 * n], refs[2 * n + 1], modes):
            cp.wait_send()
            cp.wait_recv()

    outs = pl.pallas_call(
        body, name=name, in_specs=[_HBM_SPEC] * (2 * n) + [_SEM_SPEC, _SEM_SPEC, pl.BlockSpec(memory_space=pl.ANY)],
        out_shape=tuple(pltpu.HBM(a.shape, a.dtype) for a in list(srcs) + list(lands)), out_specs=tuple([_HBM_SPEC] * (2 * n)),
        input_output_aliases={i: i for i in range(2 * n)},
        compiler_params=pltpu.CompilerParams(has_side_effects=_SPLIT_EFFECT),
    )(*srcs, *lands, send_sems, recv_sems, after)
    return list(outs[n:])


def _exchange(arrays, modes, *, name):
    n = len(arrays)
    out_shape = [SDS((N_DEV,) + a.shape if md == "gather" else a.shape, a.dtype) for a, md in zip(arrays, modes)]

    def body(*refs):
        ins, outs = refs[:n], refs[n:2 * n]
        send_sems, recv_sems, local_sems = refs[2 * n:]
        x, y, c = (lax.axis_index(a) for a in AXES)
        me = 4 * x + 2 * y + c
        copies = []
        for i, md in enumerate(modes):
            src = ins[i] if md == "gather" else ins[i].at[me]
            cp = pltpu.make_async_copy(src, outs[i].at[me], local_sems.at[i])
            cp.start()
            copies.append(cp)
            for k in range(1, N_DEV):
                px, py, pc = _peer(k, x, y, c)
                src = ins[i] if md == "gather" else ins[i].at[4 * px + 2 * py + pc]
                cp = pltpu.make_async_remote_copy(
                    src_ref=src, dst_ref=outs[i].at[me], send_sem=send_sems.at[i, k - 1], recv_sem=recv_sems.at[i, k - 1],
                    device_id=(px, py, pc), device_id_type=pl.DeviceIdType.MESH)
                cp.start()
                copies.append(cp)
        for cp in copies:
            cp.wait()

    anyspec = pl.BlockSpec(memory_space=pl.ANY)
    return pl.pallas_call(
        body, in_specs=[anyspec] * n, out_specs=[anyspec] * n, out_shape=out_shape,
        scratch_shapes=[pltpu.SemaphoreType.DMA((n, N_DEV - 1)), pltpu.SemaphoreType.DMA((n, N_DEV - 1)),
                        pltpu.SemaphoreType.DMA((n,))],
        name=name,
    )(*arrays)


def _local_step(x, mem, g_mix, b_forget, g_xattn, g_mem, g_mlp, g_final, target, get_w_in, get_rest, send):
    B, S, D = x.shape
    M = mem.shape[1]
    T = B * S
    x0 = x.reshape(T, D)
    mem2 = mem.reshape(B * M, D)
    tgt = target.reshape(T, D)
    b_pad = jnp.pad(b_forget, (0, 120)).reshape(1, 128)
    after = lambda a, tok: a if tok is None else a + tok[0, 0]

    h1 = _rms(x0, g_mix, tm=1024, name="f_norm")
    w_in_pad = get_w_in(h1)
    _, z, gate = _rms_matmul(h1, g_mix, w_in_pad[:, :QKV_W], tm=ROWS, tn=QKV_W, out_dtype=BF16,
                             w_f32=w_in_pad[:, QKV_W:], normed=True, name="f_in")
    cc = _gate_fwd(gate, b_pad, B=B, S=S, name="f_gatecum")
    ya, lse = _dil_attn_fwd(z, B=B, S=S, name="f_dil")
    yf, lse_f = _fox_fwd(z, cc, B=B, S=S, name="f_fox")
    ymix = jnp.concatenate([ya, yf], axis=1)
    w = get_rest(ymix)
    x1, h2, q = _res_rms_matmul(ymix, w["w_out"], x0, g_xattn, w["w_xq"], tm=ROWS, name="f_out")
    mn, kv = _rms_matmul(mem2, g_mem, w["w_kv"], tm=B * M, tn=D, out_dtype=BF16, name="f_xkv")
    xo = _xattn_fwd(q, kv, B=B, S=S, M=M, tq=512, name="f_xattn")
    x2, h3, act = _res_rms_matmul(xo, w["w_xo"], x1, g_mlp, w["w_up"], tm=ROWS // 2, relu=True, name="f_xo")
    dx3, dg_final, loss = _down_loss(act, w["w_down"], x2, g_final, tgt, tm=ROWS, name="f_down")

    du = _matmul_nt(dx3, w["w_down"], mul2a=act, tm=ROWS, tn=D_FF, name="b_dact")
    dw_down = _matmul_tn(act, dx3, square=True, bk=1024, bn=D, tt=ACC_ROWS, out_dtype=BF16, name="b_wdown")
    dw_up = _matmul_tn(h3, du, bk=D, bn=1024, tt=ACC_ROWS, out_dtype=BF16, name="b_wup")
    tok = send(dict(w_down=dw_down, w_up=dw_up))
    dx2, dg_mlp, dxo = _matmul_nt_rms(du, w["w_up"], x2, after(g_mlp, tok), dx3, then_w=w["w_xo"], tm=ROWS // 2, tk=D_FF,
                                      name="b_dh3")
    dw_xo = _matmul_tn(xo, dx2, bk=D, bn=D, tt=ACC_ROWS, out_dtype=BF16, name="b_wxo")
    dq, dkv = _xattn_bwd(q, kv, dxo, B=B, S=S, M=M, tq=512, name="b_xattn")
    dw_xq = _matmul_tn(h2, dq, bk=D, bn=D, tt=ACC_ROWS, out_dtype=BF16, name="b_wxq")
    dx1, dg_xattn, dy = _matmul_nt_rms(dq, w["w_xq"], x1, g_xattn, dx2, then_w=w["w_out"], tm=ROWS, tk=D, name="b_dh2")
    dw_kv = _matmul_tn(mn, dkv, bk=D, bn=D, tt=B * M, out_dtype=BF16, name="b_wkv")
    _, dg_mem = _matmul_nt_rms(dkv, w["w_kv"], mem2, g_mem, None, tm=min(ROWS, B * M), tk=2 * D, name="b_dmem")
    dw_out = _matmul_tn(ymix, dx1, bk=D, bn=D, tt=ACC_ROWS, out_dtype=BF16, name="b_wout")
    tok = send(dict(w_xo=dw_xo, w_xq=dw_xq, w_xk=dw_kv[:, :D], w_xv=dw_kv[:, D:], w_out=dw_out))
    dqf, dkf, dvf, dcc = _fox_bwd(z, dy, lse_f, cc, B=B, S=S, name="b_fox")
    dgate, db = _gate_bwd(dcc, gate, after(b_pad, tok), B=B, S=S, name="b_gate")
    dqa, dka, dva = _dil_attn_bwd(z, dy, ya, lse, B=B, S=S, name="b_dil")
    dz = [dqa, dka, dva, dqf, dkf, dvf, dgate]
    dw_in = jnp.concatenate([_matmul_tn_pieces(h1, dz[:3], tt=ACC_ROWS, name="b_win_dil"),
                             _matmul_tn_pieces(h1, dz[3:], tt=ACC_ROWS, name="b_win_fox")], axis=1)
    tok = send(dict(w_in=dw_in))
    gx, dg_mix = _matmul_nt_rms(dz, w_in_pad, x0, after(g_mix, tok), dx1, tm=ROWS, tk=IN_PAD, name="b_dh1")

    small = dict(g_mix=dg_mix, b_forget=db, g_xattn=dg_xattn, g_mem=dg_mem, g_mlp=dg_mlp, g_final=dg_final)
    return gx.reshape(B, S, D), small, loss


SMALL_ROWS = ("g_mix", "b_forget", "g_xattn", "g_mem", "g_mlp", "g_final")
COL_SHARDED = ("w_in", "w_up")


def _pack_rows(rows):
    D = D_MODEL
    rows = [jnp.pad(r.reshape(-1), (0, D - r.size)) for r in rows]
    rows += [jnp.zeros((D,), F32)] * (8 - len(rows))
    return jnp.stack(rows)


def _full(name, g):
    if name in COL_SHARDED:
        return g.transpose(1, 0, 2).reshape(g.shape[1], -1)
    return g.reshape(-1, g.shape[2])


def _blocks(name, g, shard_shape):
    if name in COL_SHARDED:
        n = shard_shape[1]
        return g[:, :n * N_DEV].reshape(g.shape[0], N_DEV, n).transpose(1, 0, 2)
    return g.reshape((N_DEV,) + shard_shape)


def kernel(x, mem, g_mix, w_in, b_forget, w_out, g_xattn, g_mem, w_xq, w_xk, w_xv, w_xo, g_mlp, w_up, w_down, g_final, loss_target, m_g_mix, m_w_in, m_b_forget, m_w_out, m_g_xattn, m_g_mem, m_w_xq, m_w_xk, m_w_xv, m_w_xo, m_g_mlp, m_w_up, m_w_down, m_g_final, v_g_mix, v_w_in, v_b_forget, v_w_out, v_g_xattn, v_g_mem, v_w_xq, v_w_xk, v_w_xv, v_w_xo, v_g_mlp, v_w_up, v_w_down, v_g_final):
    W = dict(w_in=w_in, w_out=w_out, w_xq=w_xq, w_xk=w_xk, w_xv=w_xv, w_xo=w_xo, w_up=w_up, w_down=w_down)
    Mo = dict(w_in=m_w_in, w_out=m_w_out, w_xq=m_w_xq, w_xk=m_w_xk, w_xv=m_w_xv, w_xo=m_w_xo, w_up=m_w_up, w_down=m_w_down)
    Vo = dict(w_in=v_w_in, w_out=v_w_out, w_xq=v_w_xq, w_xk=v_w_xk, w_xv=v_w_xv, w_xo=v_w_xo, w_up=v_w_up, w_down=v_w_down)
    later = [n for n in W if n != "w_in"]

    first_handle, first_token = _exchange_start([w_in.astype(BF16)], ["gather"], name="gather_in_start")
    rest_handle, rest_token = _exchange_start([W[n].astype(BF16) + first_token[0, 0].astype(BF16) for n in later],
                                              ["gather"] * len(later), name="gather_rest_start")

    def get_w_in(after):
        (g,) = _exchange_wait(first_handle, after, name="gather_in_wait")
        return jnp.pad(_full("w_in", g), ((0, 0), (0, IN_PAD - IN_W)))

    def get_rest(after):
        full = {n: _full(n, g) for n, g in zip(later, _exchange_wait(rest_handle, after, name="gather_rest_wait"))}
        full["w_kv"] = jnp.concatenate([full.pop("w_xk"), full.pop("w_xv")], axis=1)
        return full

    sent = []

    def send(grads):
        names = list(grads)
        handle, token = _exchange_start([_blocks(n, grads[n], W[n].shape) for n in names], ["scatter"] * len(names),
                                        name=f"scatter{len(sent)}_start")
        sent.append((names, handle))
        return token

    gx, small, loss = _local_step(x, mem, g_mix + rest_token[0, 0], b_forget, g_xattn, g_mem, g_mlp, g_final, loss_target,
                                  get_w_in, get_rest, send)

    received = {}
    for i, (names, handle) in enumerate(sent):
        received.update(zip(names, _exchange_wait(handle, gx, name=f"scatter{i}_wait")))
    packed = _pack_rows([small[n] for n in SMALL_ROWS] + [loss[0, :1]])
    (packed_all,) = _exchange([packed], ["gather"], name="gather_small")

    rows_per_step = lambda shape: max(t for t in (128, 256, 512) if shape[0] % t == 0 and t * shape[1] <= 512 * 512)
    res = {n: _adamw(received[n], W[n], Mo[n], Vo[n], tr=rows_per_step(W[n].shape), name=f"adamw_{n}") for n in W}
    small_w = dict(g_mix=g_mix, b_forget=b_forget, g_xattn=g_xattn, g_mem=g_mem, g_mlp=g_mlp, g_final=g_final)
    small_m = dict(g_mix=m_g_mix, b_forget=m_b_forget, g_xattn=m_g_xattn, g_mem=m_g_mem, g_mlp=m_g_mlp, g_final=m_g_final)
    small_v = dict(g_mix=v_g_mix, b_forget=v_b_forget, g_xattn=v_g_xattn, g_mem=v_g_mem, g_mlp=v_g_mlp, g_final=v_g_final)
    sres = _adamw(packed_all, _pack_rows([small_w[n] for n in SMALL_ROWS]), _pack_rows([small_m[n] for n in SMALL_ROWS]),
                  _pack_rows([small_v[n] for n in SMALL_ROWS]), tr=8, name="adamw_small")
    for i, n in enumerate(SMALL_ROWS):
        res[n] = [r[i, :small_w[n].size] for r in sres]
    loss_total = sres[0][6, 0]

    order = ["g_mix", "w_in", "b_forget", "w_out", "g_xattn", "g_mem", "w_xq", "w_xk", "w_xv", "w_xo", "g_mlp", "w_up", "w_down", "g_final"]
    return (loss_total, gx, *[res[n][0] for n in order], *[res[n][1] for n in order],
            *[res[n][2] for n in order], *[res[n][3] for n in order])
```

```python
import jax
import jax.numpy as jnp
from jax import lax
from jax.experimental import pallas as pl
from jax.experimental.pallas import tpu as pltpu

F32, BF16 = jnp.float32, jnp.bfloat16
SDS = jax.ShapeDtypeStruct

D_MODEL = 1024
HEAD_DIM = 64
WIDTH = 512
QKV_W = 6 * WIDTH
IN_W = QKV_W + 8
IN_PAD = QKV_W + 128
BLOCK = 128
DIL_CONFIGS = ((128, 1), (512, 4), (2048, 16))
N_XH, XHD = 4, 256
D_FF = 4096
EPS = 1e-6
NEG = -1e30
N_DEV = 8
AXES = ("x", "y", "c")

ADAM_LR, ADAM_B1, ADAM_B2, ADAM_EPS, ADAM_WD, ADAM_STEP = 0.001, 0.9, 0.999, 1e-08, 0.01, 10

VMEM_CAP_V7X = 64 * 1024 * 1024
VMEM_LIMIT = VMEM_CAP_V7X * 7 // 8

ROWS = 512
ACC_ROWS = 2048

NT = (((1,), (1,)), ((), ()))
TN = (((0,), (0,)), ((), ()))


def _cp(**kw):
    return pltpu.CompilerParams(vmem_limit_bytes=VMEM_LIMIT, **kw)


def _dot(a, b, dims=None):
    if dims is None:
        return jnp.dot(a, b, preferred_element_type=F32)
    return lax.dot_general(a, b, dims, preferred_element_type=F32)


def _rstd(xv):
    return lax.rsqrt(jnp.mean(xv * xv, axis=-1, keepdims=True) + EPS)


def _rms_bwd(dh, xv, g):
    r = _rstd(xv)
    xhat = xv * r
    dxhat = dh * g
    dx = r * (dxhat - xhat * jnp.mean(dxhat * xhat, axis=-1, keepdims=True))
    return dx, jnp.sum(dh * xhat, axis=0, keepdims=True)


def _rms_matmul(x, g, w, *, tm, tn, out_dtype, relu=False, w_f32=None, normed=False, name):
    T, D = x.shape
    N = w.shape[1]

    def body(*refs):
        x_ref, g_ref, w_ref = refs[:3]
        h_ref, o_ref, h_s = refs[-3 - (w_f32 is not None)], refs[-2 - (w_f32 is not None)], refs[-1]

        @pl.when(pl.program_id(1) == 0)
        def _():
            xv = x_ref[...]
            h = xv if normed else (xv * _rstd(xv) * g_ref[...]).astype(BF16)
            h_s[...] = h
            h_ref[...] = h
            if w_f32 is not None:
                refs[-2][...] = _dot(h, refs[3][...])

        acc = _dot(h_s[...], w_ref[...])
        if relu:
            acc = jnp.maximum(acc, 0.0)
        o_ref[...] = acc.astype(out_dtype)

    in_specs = [pl.BlockSpec((tm, D), lambda i, j: (i, 0)), pl.BlockSpec((1, D), lambda i, j: (0, 0)),
                pl.BlockSpec((D, tn), lambda i, j: (0, j))]
    out_specs = [pl.BlockSpec((tm, D), lambda i, j: (i, 0)), pl.BlockSpec((tm, tn), lambda i, j: (i, j))]
    out_shape = [SDS((T, D), BF16), SDS((T, N), out_dtype)]
    args = [x, g.reshape(1, D), w]
    if w_f32 is not None:
        n2 = w_f32.shape[1]
        in_specs.append(pl.BlockSpec((D, n2), lambda i, j: (0, 0)))
        out_specs.append(pl.BlockSpec((tm, n2), lambda i, j: (i, 0)))
        out_shape.append(SDS((T, n2), F32))
        args.append(w_f32)
    return pl.pallas_call(
        body, grid=(T // tm, N // tn), in_specs=in_specs, out_specs=out_specs, out_shape=out_shape,
        scratch_shapes=[pltpu.VMEM((tm, D), BF16)], compiler_params=_cp(), name=name,
    )(*args)


def _rms(x, g, *, tm, name):
    T, D = x.shape

    def body(x_ref, g_ref, h_ref):
        xv = x_ref[...]
        h_ref[...] = (xv * _rstd(xv) * g_ref[...]).astype(BF16)

    rows = pl.BlockSpec((tm, D), lambda i: (i, 0))
    return pl.pallas_call(body, grid=(T // tm,), in_specs=[rows, pl.BlockSpec((1, D), lambda i: (0, 0))], out_specs=rows,
                          out_shape=SDS((T, D), BF16), compiler_params=_cp(), name=name)(x, g.reshape(1, D))


def _res_rms_matmul(a, w1, res, gain, w2, *, tm, relu=False, name):
    T, K = a.shape
    D, N = w2.shape

    def body(a_ref, w1_ref, res_ref, g_ref, w2_ref, x_ref, h_ref, o_ref):
        xv = res_ref[...] + _dot(a_ref[...], w1_ref[...])
        x_ref[...] = xv
        h = (xv * _rstd(xv) * g_ref[...]).astype(BF16)
        h_ref[...] = h
        acc = _dot(h, w2_ref[...])
        if relu:
            acc = jnp.maximum(acc, 0.0)
        o_ref[...] = acc.astype(BF16)

    rows = lambda n: pl.BlockSpec((tm, n), lambda i: (i, 0))
    whole = lambda r, c: pl.BlockSpec((r, c), lambda i: (0, 0))
    return pl.pallas_call(
        body, grid=(T // tm,), in_specs=[rows(K), whole(K, D), rows(D), whole(1, D), whole(D, N)],
        out_specs=[rows(D), rows(D), rows(N)], out_shape=[SDS((T, D), F32), SDS((T, D), BF16), SDS((T, N), BF16)],
        compiler_params=_cp(), name=name,
    )(a, w1, res, gain.reshape(1, D), w2)


def _matmul_nt(g, w, *, mul2a=None, tm, tn, name):
    T, K = g.shape
    N = w.shape[0]

    def body(*refs):
        g_ref, w_ref = refs[0], refs[1]
        o_ref = refs[-1]
        acc = _dot(g_ref[...].astype(BF16), w_ref[...], NT)
        if mul2a is not None:
            acc = acc * (2.0 * refs[2][...].astype(F32))
        o_ref[...] = acc.astype(BF16)

    in_specs = [pl.BlockSpec((tm, K), lambda i, j: (i, 0)), pl.BlockSpec((tn, K), lambda i, j: (j, 0))]
    args = [g, w]
    if mul2a is not None:
        in_specs.append(pl.BlockSpec((tm, tn), lambda i, j: (i, j)))
        args.append(mul2a)
    return pl.pallas_call(
        body, grid=(T // tm, N // tn), in_specs=in_specs,
        out_specs=pl.BlockSpec((tm, tn), lambda i, j: (i, j)), out_shape=SDS((T, N), BF16),
        compiler_params=_cp(), name=name,
    )(*args)


def _matmul_nt_rms(g, w, x, gain, dres, *, then_w=None, tm, tk, name):
    pieces = list(g) if isinstance(g, (list, tuple)) else [g]
    widths = [p.shape[1] for p in pieces]
    T, K = pieces[0].shape[0], sum(widths)
    D = w.shape[0]
    nk = K // tk
    nt = T // tm
    npc = len(pieces)
    assert npc == 1 or nk == 1
    n_in = npc + 3 + (dres is not None) + (then_w is not None)

    def body(*refs):
        w_ref, x_ref, gain_ref = refs[npc:npc + 3]
        dres_ref = refs[npc + 3] if dres is not None else None
        then_ref = refs[n_in - 1] if then_w is not None else None
        dx_ref, dg_ref = refs[n_in], refs[n_in + 1]
        i, k = pl.program_id(0), pl.program_id(1)
        if npc == 1:
            part = _dot(refs[0][...].astype(BF16), w_ref[...], NT)
        else:
            part, off = None, 0
            for j in range(npc):
                d = _dot(refs[j][...].astype(BF16), w_ref[:, off:off + widths[j]], NT)
                part = d if part is None else part + d
                off += widths[j]

        def finish(dh):
            dx, dg = _rms_bwd(dh, x_ref[...], gain_ref[...])
            if dres_ref is not None:
                dx = dres_ref[...] + dx
            dx_ref[...] = dx
            if then_ref is not None:
                refs[n_in + 2][...] = _dot(dx.astype(BF16), then_ref[...], NT).astype(BF16)

            @pl.when(i == 0)
            def _():
                dg_ref[...] = dg

            @pl.when(i > 0)
            def _():
                dg_ref[...] += dg

        if nk == 1:
            finish(part)
        else:
            acc = refs[-1]

            @pl.when(k == 0)
            def _():
                acc[...] = part

            @pl.when(k > 0)
            def _():
                acc[...] += part

            @pl.when(k == nk - 1)
            def _():
                finish(acc[...])

    g_specs = ([pl.BlockSpec((tm, tk), lambda i, k: (i, k))] if npc == 1 else
               [pl.BlockSpec((tm, wd), lambda i, k: (i, 0)) for wd in widths])
    in_specs = g_specs + [pl.BlockSpec((D, tk), lambda i, k: (0, k)),
                          pl.BlockSpec((tm, D), lambda i, k: (i, 0)), pl.BlockSpec((1, D), lambda i, k: (0, 0))]
    args = pieces + [w, x, gain.reshape(1, D)]
    out_specs = [pl.BlockSpec((tm, D), lambda i, k: (i, 0)), pl.BlockSpec((1, D), lambda i, k: (0, 0))]
    out_shape = [SDS((T, D), F32), SDS((1, D), F32)]
    if dres is not None:
        in_specs.append(pl.BlockSpec((tm, D), lambda i, k: (i, 0)))
        args.append(dres)
    if then_w is not None:
        n2 = then_w.shape[0]
        in_specs.append(pl.BlockSpec((n2, D), lambda i, k: (0, 0)))
        args.append(then_w)
        out_specs.append(pl.BlockSpec((tm, n2), lambda i, k: (i, 0)))
        out_shape.append(SDS((T, n2), BF16))
    return pl.pallas_call(
        body, grid=(nt, nk), in_specs=in_specs, out_specs=out_specs, out_shape=out_shape,
        scratch_shapes=[pltpu.VMEM((tm, D), F32)] if nk > 1 else [], compiler_params=_cp(), name=name,
    )(*args)


def _matmul_tn(a, g, *, square=False, bk, bn, tt, out_dtype, name):
    T, K = a.shape
    N = g.shape[1]
    nt = T // tt

    def body(a_ref, g_ref, o_ref, acc):
        t = pl.program_id(2)
        av = a_ref[...]
        if square:
            af = av.astype(F32)
            av = (af * af).astype(BF16)
        part = _dot(av, g_ref[...].astype(BF16), TN)
        if nt == 1:
            o_ref[...] = part.astype(out_dtype)
        else:
            @pl.when(t == 0)
            def _():
                acc[...] = part

            @pl.when((t > 0) & (t < nt - 1))
            def _():
                acc[...] += part

            @pl.when(t == nt - 1)
            def _():
                o_ref[...] = (acc[...] + part).astype(out_dtype)

    return pl.pallas_call(
        body, grid=(K // bk, N // bn, nt),
        in_specs=[pl.BlockSpec((tt, bk), lambda i, j, t: (t, i)), pl.BlockSpec((tt, bn), lambda i, j, t: (t, j))],
        out_specs=pl.BlockSpec((bk, bn), lambda i, j, t: (i, j)), out_shape=SDS((K, N), out_dtype),
        scratch_shapes=[pltpu.VMEM((bk, bn), F32)], compiler_params=_cp(), name=name,
    )(a, g)


def _matmul_tn_pieces(a, pieces, *, tt, name):
    T, K = a.shape
    widths = [p.shape[1] for p in pieces]
    W = sum(widths)
    nt = T // tt
    n = len(pieces)

    def body(*refs):
        a_ref, o_ref, acc = refs[0], refs[n + 1], refs[n + 2]
        t = pl.program_id(0)
        av = a_ref[...]
        off = 0
        for j in range(n):
            cols = slice(off, off + widths[j])
            part = _dot(av, refs[1 + j][...], TN)

            @pl.when(t == 0)
            def _():
                acc[:, cols] = part

            @pl.when(t > 0)
            def _():
                acc[:, cols] += part

            off += widths[j]

        @pl.when(t == nt - 1)
        def _():
            o_ref[...] = acc[...].astype(BF16)

    return pl.pallas_call(
        body, grid=(nt,),
        in_specs=[pl.BlockSpec((tt, K), lambda t: (t, 0))] + [pl.BlockSpec((tt, w), lambda t: (t, 0)) for w in widths],
        out_specs=pl.BlockSpec((K, W), lambda t: (0, 0)), out_shape=SDS((K, W), BF16),
        scratch_shapes=[pltpu.VMEM((K, W), F32)], compiler_params=_cp(), name=name,
    )(a, *pieces)


def _down_loss(act, w_down, x2, g_final, target, *, tm, name):
    T, D = x2.shape
    F = act.shape[1]

    def body(a_ref, w_ref, x2_ref, g_ref, t_ref, dx_ref, dg_ref, loss_ref):
        i = pl.program_id(0)
        af = a_ref[...].astype(F32)
        xv, g = x2_ref[...] + _dot((af * af).astype(BF16), w_ref[...]), g_ref[...]
        r = _rstd(xv)
        xhat = xv * r
        diff = xhat * g - t_ref[...]
        part = 0.5 * jnp.sum(jnp.mean(diff * diff, axis=-1, keepdims=True), axis=0, keepdims=True)
        dy = diff * (1.0 / D)
        dxhat = dy * g
        dx_ref[...] = r * (dxhat - xhat * jnp.mean(dxhat * xhat, axis=-1, keepdims=True))
        dg = jnp.sum(dy * xhat, axis=0, keepdims=True)
        lp = jnp.broadcast_to(part, loss_ref.shape)

        @pl.when(i == 0)
        def _():
            dg_ref[...] = dg
            loss_ref[...] = lp

        @pl.when(i > 0)
        def _():
            dg_ref[...] += dg
            loss_ref[...] += lp

    rows = pl.BlockSpec((tm, D), lambda i: (i, 0))
    return pl.pallas_call(
        body, grid=(T // tm,),
        in_specs=[pl.BlockSpec((tm, F), lambda i: (i, 0)), pl.BlockSpec((F, D), lambda i: (0, 0)), rows,
                  pl.BlockSpec((1, D), lambda i: (0, 0)), rows],
        out_specs=[rows, pl.BlockSpec((1, D), lambda i: (0, 0)), pl.BlockSpec((8, 128), lambda i: (0, 0))],
        out_shape=[SDS((T, D), F32), SDS((1, D), F32), SDS((8, 128), F32)],
        compiler_params=_cp(), name=name,
    )(act, w_down, x2, g_final.reshape(1, D), target)


def _head_lanes(shape, width):
    return lax.broadcasted_iota(jnp.int32, shape, len(shape) - 1) // width


def _gate_fwd(gate, b_pad, *, B, S, name):
    def body(g_ref, b_ref, cc_ref):
        xv = g_ref[...] + b_ref[...]
        lf = jnp.minimum(xv, 0.0) - jnp.log(1.0 + jnp.exp(-jnp.abs(xv)))
        lane = lax.broadcasted_iota(jnp.int32, lf.shape, 1)
        row = lax.broadcasted_iota(jnp.int32, lf.shape, 0)
        c = jnp.where(lane < 8, lf, 0.0)
        sh = 1
        while sh < S:
            c = c + jnp.where(row >= sh, pltpu.roll(c, sh, 0), 0.0)
            sh *= 2
        grp = _head_lanes((S, WIDTH), HEAD_DIM)
        cc = jnp.zeros((S, WIDTH), F32)
        for h in range(8):
            cc = jnp.where(grp == h, c[:, h:h + 1], cc)
        cc_ref[...] = cc

    return pl.pallas_call(
        body, grid=(B,),
        in_specs=[pl.BlockSpec((S, 128), lambda b: (b, 0)), pl.BlockSpec((1, 128), lambda b: (0, 0))],
        out_specs=pl.BlockSpec((S, WIDTH), lambda b: (b, 0)), out_shape=SDS((B * S, WIDTH), F32),
        compiler_params=_cp(), name=name,
    )(gate, b_pad)


def _gate_bwd(dcc, gate, b_pad, *, B, S, name):
    def body(dcc_ref, g_ref, b_ref, dg_ref, db_ref):
        bi = pl.program_id(0)
        dccv = dcc_ref[...]
        lane = lax.broadcasted_iota(jnp.int32, (S, 128), 1)
        row = lax.broadcasted_iota(jnp.int32, (S, 128), 0)
        dc = jnp.zeros((S, 128), F32)
        for h in range(8):
            dc = jnp.where(lane == h, dccv[:, HEAD_DIM * h:HEAD_DIM * h + 1], dc)
        sh = 1
        while sh < S:
            dc = dc + jnp.where(row < S - sh, pltpu.roll(dc, S - sh, 0), 0.0)
            sh *= 2
        xv = g_ref[...] + b_ref[...]
        dgate = jnp.where(lane < 8, dc / (1.0 + jnp.exp(xv)), 0.0)
        dg_ref[...] = dgate.astype(BF16)
        db = jnp.sum(dgate, axis=0, keepdims=True)

        @pl.when(bi == 0)
        def _():
            db_ref[...] = db

        @pl.when(bi > 0)
        def _():
            db_ref[...] += db

    return pl.pallas_call(
        body, grid=(B,),
        in_specs=[pl.BlockSpec((S, WIDTH), lambda b: (b, 0)), pl.BlockSpec((S, 128), lambda b: (b, 0)),
                  pl.BlockSpec((1, 128), lambda b: (0, 0))],
        out_specs=[pl.BlockSpec((S, 128), lambda b: (b, 0)), pl.BlockSpec((1, 128), lambda b: (0, 0))],
        out_shape=[SDS((B * S, 128), BF16), SDS((1, 128), F32)],
        compiler_params=_cp(), name=name,
    )(dcc, gate, b_pad)


_SMEM_SPEC = pl.BlockSpec(memory_space=pltpu.SMEM)


def _alibi_slopes():
    return 2.0 ** (-(jnp.arange(1, 9, dtype=F32) * (8.0 / 8)))


def _pair_masks():
    lane = lax.broadcasted_iota(jnp.int32, (1, 128), 1)
    first = lane < HEAD_DIM
    return (first.astype(BF16), (~first).astype(BF16)), first


BNT =(((2,), (2,)), ((0,), (0,)))
BNN = (((2,), (1,)), ((0,), (0,)))
BTN = (((1,), (1,)), ((0,), (0,)))


def _split3(v):
    hi = v.astype(BF16).astype(F32)
    mid = (v - hi).astype(BF16).astype(F32)
    lo = (v - hi - mid).astype(BF16).astype(F32)
    return [hi, mid, lo]


def _with_spare_lanes(base, e, cols):
    lane = lax.broadcasted_iota(jnp.int32, (1, 128), 1)
    off = HEAD_DIM * (1 - e)
    extra = jnp.zeros(base.shape, F32)
    for j, c in enumerate(cols):
        extra = jnp.where(lane == off + j, c, extra)
    return base + extra.astype(BF16)


ONES3 = [1.0, 1.0, 1.0]


def _band_bias(slope, dilation):
    qi = lax.broadcasted_iota(jnp.int32, (BLOCK, BLOCK), 0)
    kj = lax.broadcasted_iota(jnp.int32, (BLOCK, BLOCK), 1)
    cur = jnp.where(kj <= qi, (-slope * dilation) * (qi - kj).astype(F32), NEG)
    prev = jnp.where(kj >= qi, (-slope * dilation) * (qi + BLOCK - kj).astype(F32), NEG)
    return cur, prev


def _to_residue_major(dst, src_f32, dilation, nb, lead=0):
    L = nb * BLOCK
    for r in range(dilation):
        rows = src_f32[pl.ds(r, L, stride=dilation), :] if dilation > 1 else src_f32[...]
        dst[lead + r * nb:lead + (r + 1) * nb] = rows.reshape(nb, BLOCK, 128).astype(dst.dtype)


def _dil_attn_fwd(z, *, B, S, name):
    NB = S // BLOCK

    def body(slope_ref, q_ref, k_ref, v_ref, y_ref, lse_ref, qf, kf, vf, qd, kd, vd, od, ld, acc_o, acc_l):
        (m_first, m_second), first = _pair_masks()
        p = pl.program_id(1)
        qf[...] = q_ref[...].astype(F32)
        kf[...] = k_ref[...].astype(F32)
        vf[...] = v_ref[...].astype(F32)
        kd[0] = jnp.zeros((BLOCK, 128), BF16)
        vd[0] = jnp.zeros((BLOCK, 128), BF16)
        blk = lax.broadcasted_iota(jnp.int32, (NB, 1, 1), 0)

        for idx, (_, dilation) in enumerate(DIL_CONFIGS):
            nb = NB // dilation
            _to_residue_major(qd, qf, dilation, nb)
            _to_residue_major(kd, kf, dilation, nb, lead=1)
            _to_residue_major(vd, vf, dilation, nb, lead=1)
            q4, kc, vc = qd[...], kd[1:NB + 1], vd[1:NB + 1]
            outs, lses = [], []
            for e, hm in enumerate((m_first, m_second)):
                bias_cur, bias_prev = _band_bias(slope_ref[2 * p + e], dilation)
                qm = q4 * hm
                sc = _dot(qm, kc, BNT) * 0.125 + bias_cur
                m = jnp.max(sc, axis=2, keepdims=True)
                if nb > 1:
                    sp = _dot(qm, kd[0:NB], BNT) * 0.125 + jnp.where(blk % nb == 0, NEG, bias_prev)
                    m = jnp.maximum(m, jnp.max(sp, axis=2, keepdims=True))
                pc = jnp.exp(sc - m)
                l = jnp.sum(pc, axis=2, keepdims=True)
                o = _dot(pc.astype(BF16), vc, BNN)
                if nb > 1:
                    pp = jnp.exp(sp - m)
                    l = l + jnp.sum(pp, axis=2, keepdims=True)
                    o = o + _dot(pp.astype(BF16), vd[0:NB], BNN)
                outs.append(o * (1.0 / l))
                lses.append(m + jnp.log(l))
            od[...] = jnp.where(first, outs[0], outs[1])
            ld[...] = jnp.where(first, lses[0], lses[1])

            L = nb * BLOCK
            for r in range(dilation):
                rows = pl.ds(r, L, stride=dilation) if dilation > 1 else slice(None)
                o_new = od[r * nb:(r + 1) * nb].reshape(L, 128)
                l_new = ld[r * nb:(r + 1) * nb].reshape(L, 128)
                if idx == 0:
                    acc_o[rows, :] = o_new
                    acc_l[rows, :] = l_new
                else:
                    l_old = acc_l[rows, :]
                    m2 = jnp.maximum(l_old, l_new)
                    w_old, w_new = jnp.exp(l_old - m2), jnp.exp(l_new - m2)
                    tot = w_old + w_new
                    acc_o[rows, :] = (w_old * acc_o[rows, :] + w_new * o_new) * (1.0 / tot)
                    acc_l[rows, :] = m2 + jnp.log(tot)

        y_ref[...] = acc_o[...].astype(BF16)
        lse_ref[...] = acc_l[...]

    spec = lambda off: pl.BlockSpec((S, 128), lambda b, p: (b, 4 * off + p))
    ospec = pl.BlockSpec((S, 128), lambda b, p: (b, p))
    blocks = lambda n, dt: pltpu.VMEM((n, BLOCK, 128), dt)
    return pl.pallas_call(
        body, grid=(B, 4), in_specs=[_SMEM_SPEC, spec(0), spec(1), spec(2)], out_specs=[ospec, ospec],
        out_shape=[SDS((B * S, WIDTH), BF16), SDS((B * S, WIDTH), F32)],
        scratch_shapes=[pltpu.VMEM((S, 128), F32)] * 3 + [blocks(NB, BF16), blocks(NB + 1, BF16), blocks(NB + 1, BF16),
                                                         blocks(NB, F32), blocks(NB, F32)] + [pltpu.VMEM((S, 128), F32)] * 2,
        compiler_params=_cp(), name=name,
    )(_alibi_slopes(), z, z, z)


def _dil_attn_bwd(z, dy, ya, lse, *, B, S, name):
    NB = S // BLOCK

    def body(slope_ref, q_ref, k_ref, v_ref, do_ref, o_ref, lse_ref, dq_ref, dk_ref, dv_ref,
             qf, kf, vf, dof, ef, qd, dod, kd, vd, lsd, ed, dkd, dvd, dqa, dka, dva):
        (m_first, m_second), first = _pair_masks()
        p = pl.program_id(1)
        qf[...] = q_ref[...].astype(F32)
        kf[...] = k_ref[...].astype(F32)
        vf[...] = v_ref[...].astype(F32)
        dov = do_ref[...].astype(F32)
        dof[...] = dov
        prod = dov * o_ref[...].astype(F32)
        ef[...] = jnp.where(first, jnp.sum(jnp.where(first, prod, 0.0), axis=1, keepdims=True),
                            jnp.sum(jnp.where(first, 0.0, prod), axis=1, keepdims=True))
        kd[0] = jnp.zeros((BLOCK, 128), BF16)
        vd[0] = jnp.zeros((BLOCK, 128), BF16)
        blk = lax.broadcasted_iota(jnp.int32, (NB, 1, 1), 0)

        for idx, (_, dilation) in enumerate(DIL_CONFIGS):
            nb = NB // dilation
            _to_residue_major(qd, qf, dilation, nb)
            _to_residue_major(dod, dof, dilation, nb)
            _to_residue_major(kd, kf, dilation, nb, lead=1)
            _to_residue_major(vd, vf, dilation, nb, lead=1)
            _to_residue_major(lsd, lse_ref, dilation, nb)
            _to_residue_major(ed, ef, dilation, nb)
            q4, do4, kc, vc = qd[...], dod[...], kd[1:NB + 1], vd[1:NB + 1]
            dq4 = None
            dkc = dvc = dkp = dvp = None
            for e, hm in enumerate((m_first, m_second)):
                lane0 = slice(HEAD_DIM * e, HEAD_DIM * e + 1)
                bias_cur, bias_prev = _band_bias(slope_ref[2 * p + e], dilation)
                qm, dom = q4 * hm, do4 * hm
                lse_e, e_e = lsd[...][:, :, lane0], ed[...][:, :, lane0]
                pc = jnp.exp(_dot(qm, kc, BNT) * 0.125 + bias_cur - lse_e)
                dsc = (pc * (_dot(dom, vc, BNT) - e_e)).astype(BF16)
                pcb = pc.astype(BF16)
                dqe = _dot(dsc, kc, BNN)
                dkc = _dot(dsc, qm, BTN) if e == 0 else dkc + _dot(dsc, qm, BTN)
                dvc = _dot(pcb, dom, BTN) if e == 0 else dvc + _dot(pcb, dom, BTN)
                if nb > 1:
                    kp, vp = kd[0:NB], vd[0:NB]
                    pp = jnp.exp(_dot(qm, kp, BNT) * 0.125 + jnp.where(blk % nb == 0, NEG, bias_prev) - lse_e)
                    dsp = (pp * (_dot(dom, vp, BNT) - e_e)).astype(BF16)
                    ppb = pp.astype(BF16)
                    dqe = dqe + _dot(dsp, kp, BNN)
                    dkp = _dot(dsp, qm, BTN) if e == 0 else dkp + _dot(dsp, qm, BTN)
                    dvp = _dot(ppb, dom, BTN) if e == 0 else dvp + _dot(ppb, dom, BTN)
                dq4 = dqe if e == 0 else jnp.where(first, dq4, dqe)

            dkd[1:NB + 1] = dkc
            dvd[1:NB + 1] = dvc
            if nb > 1:
                dkd[1:NB] += dkp[1:NB]
                dvd[1:NB] += dvp[1:NB]
            L = nb * BLOCK
            for r in range(dilation):
                rows = pl.ds(r, L, stride=dilation) if dilation > 1 else slice(None)
                dq_r = dq4[r * nb:(r + 1) * nb].reshape(L, 128) * 0.125
                dk_r = dkd[1 + r * nb:1 + (r + 1) * nb].reshape(L, 128) * 0.125
                dv_r = dvd[1 + r * nb:1 + (r + 1) * nb].reshape(L, 128)
                if idx == 0:
                    dqa[rows, :], dka[rows, :], dva[rows, :] = dq_r, dk_r, dv_r
                else:
                    dqa[rows, :] += dq_r
                    dka[rows, :] += dk_r
                    dva[rows, :] += dv_r

        dq_ref[...] = dqa[...].astype(BF16)
        dk_ref[...] = dka[...].astype(BF16)
        dv_ref[...] = dva[...].astype(BF16)

    spec = lambda off: pl.BlockSpec((S, 128), lambda b, p: (b, 4 * off + p))
    ospec = pl.BlockSpec((S, 128), lambda b, p: (b, p))
    blocks = lambda n, dt: pltpu.VMEM((n, BLOCK, 128), dt)
    return pl.pallas_call(
        body, grid=(B, 4), in_specs=[_SMEM_SPEC, spec(0), spec(1), spec(2), ospec, ospec, ospec],
        out_specs=[ospec] * 3, out_shape=[SDS((B * S, WIDTH), BF16)] * 3,
        scratch_shapes=[pltpu.VMEM((S, 128), F32)] * 5
        + [blocks(NB, BF16), blocks(NB, BF16), blocks(NB + 1, BF16), blocks(NB + 1, BF16), blocks(NB, F32), blocks(NB, F32),
           blocks(NB + 1, F32), blocks(NB + 1, F32)] + [pltpu.VMEM((S, 128), F32)] * 3,
        compiler_params=_cp(), name=name,
    )(_alibi_slopes(), z, z, z, dy, ya, lse)


FOX_TQ = 256


def _fox_fwd(z, cc, *, B, S, name):
    def body(q_ref, k_ref, v_ref, cc_ref, o_ref, l_ref, qa, ka):
        (m_first, m_second), first = _pair_masks()
        ccv = cc_ref[...]
        eighth = jnp.asarray(0.125, BF16)
        for e, hm in enumerate((m_first, m_second)):
            c_e = jnp.broadcast_to(ccv[:, HEAD_DIM * e:HEAD_DIM * e + 1], (S, 128))
            qa[e] = _with_spare_lanes(q_ref[...] * hm * eighth, e, _split3(c_e) + ONES3)
            ka[e] = _with_spare_lanes(k_ref[...] * hm, e, ONES3 + _split3(-c_e))
        for qi in range(S // FOX_TQ):
            r0, kend = qi * FOX_TQ, (qi + 1) * FOX_TQ
            vv = v_ref[0:kend, :]
            row = lax.broadcasted_iota(jnp.int32, (FOX_TQ, kend), 0) + r0
            col = lax.broadcasted_iota(jnp.int32, (FOX_TQ, kend), 1)
            causal = col <= row
            outs, lses = [], []
            for e in (0, 1):
                s = jnp.where(causal, _dot(qa[e, r0:kend, :], ka[e, 0:kend, :], NT), NEG)
                m = jnp.max(s, axis=1, keepdims=True)
                pe = jnp.exp(s - m)
                l = jnp.sum(pe, axis=1, keepdims=True)
                outs.append(_dot(pe.astype(BF16), vv) * (1.0 / l))
                lses.append(m + jnp.log(l))
            o_ref[r0:kend, :] = jnp.where(first, outs[0], outs[1]).astype(BF16)
            l_ref[r0:kend, :] = jnp.where(first, lses[0], lses[1])

    spec = lambda off: pl.BlockSpec((S, 128), lambda b, p: (b, 4 * off + p))
    pspec = pl.BlockSpec((S, 128), lambda b, p: (b, p))
    return pl.pallas_call(
        body, grid=(B, 4), in_specs=[spec(3), spec(4), spec(5), pspec], out_specs=[pspec, pspec],
        out_shape=[SDS((B * S, WIDTH), BF16), SDS((B * S, WIDTH), F32)],
        scratch_shapes=[pltpu.VMEM((2, S, 128), BF16)] * 2, compiler_params=_cp(), name=name,
    )(z, z, z, cc)


def _fox_bwd(z, dy, lse, cc, *, B, S, name):
    def body(q_ref, k_ref, v_ref, do_ref, lse_ref, cc_ref, dq_ref, dk_ref, dv_ref, dc_ref,
             qa, ka, qp, kp, vp, dp, dk_s, dv_s, dc_s):
        (m_first, m_second), first = _pair_masks()
        ccv, lsev = cc_ref[...], lse_ref[...]
        eighth = jnp.asarray(0.125, BF16)
        for e, hm in enumerate((m_first, m_second)):
            lane0 = slice(HEAD_DIM * e, HEAD_DIM * e + 1)
            c_e = jnp.broadcast_to(ccv[:, lane0], (S, 128))
            lse_e = jnp.broadcast_to(lsev[:, lane0], (S, 128))
            qp[e] = q_ref[...] * hm
            kp[e] = k_ref[...] * hm
            dp[e] = do_ref[...] * hm
            qa[e] = _with_spare_lanes(qp[e] * eighth, e, _split3(c_e - lse_e) + ONES3)
            ka[e] = _with_spare_lanes(kp[e], e, ONES3 + _split3(-c_e))
            vp[e] = v_ref[...] * hm
        dk_s[...] = jnp.zeros_like(dk_s)
        dv_s[...] = jnp.zeros_like(dv_s)
        dc_s[...] = jnp.zeros_like(dc_s)
        for qi in range(S // FOX_TQ):
            r0, kend = qi * FOX_TQ, (qi + 1) * FOX_TQ
            krow = lax.broadcasted_iota(jnp.int32, (kend, FOX_TQ), 0)
            qcol = lax.broadcasted_iota(jnp.int32, (kend, FOX_TQ), 1) + r0
            causal = krow <= qcol
            dq_t = jnp.zeros((FOX_TQ, 128), F32)
            for e in (0, 1):
                sel = first if e == 0 else ~first
                pt = jnp.where(causal, jnp.exp(_dot(ka[e, 0:kend, :], qa[e, r0:kend, :], NT)), 0.0)
                dpt = _dot(vp[e, 0:kend, :], dp[e, r0:kend, :], NT)
                mean = jnp.sum(pt * dpt, axis=0, keepdims=True) / jnp.sum(pt, axis=0, keepdims=True)
                dst = pt * (dpt - mean)
                dsb = dst.astype(BF16)
                dv_s[0:kend, :] += _dot(pt.astype(BF16), dp[e, r0:kend, :])
                dk_s[0:kend, :] += _dot(dsb, qp[e, r0:kend, :]) * 0.125
                dq_t = dq_t + _dot(dsb, kp[e, 0:kend, :], TN)
                dc_s[0:kend, :] += jnp.where(sel, -jnp.sum(dst, axis=1, keepdims=True), 0.0)
            dq_ref[r0:kend, :] = (dq_t * 0.125).astype(BF16)
        dk_ref[...] = dk_s[...].astype(BF16)
        dv_ref[...] = dv_s[...].astype(BF16)
        dc_ref[...] = dc_s[...]

    spec = lambda off: pl.BlockSpec((S, 128), lambda b, p: (b, 4 * off + p))
    pspec = pl.BlockSpec((S, 128), lambda b, p: (b, p))
    return pl.pallas_call(
        body, grid=(B, 4),
        in_specs=[spec(3), spec(4), spec(5), pl.BlockSpec((S, 128), lambda b, p: (b, 4 + p)), pspec, pspec],
        out_specs=[pspec] * 4,
        out_shape=[SDS((B * S, WIDTH), BF16)] * 3 + [SDS((B * S, WIDTH), F32)],
        scratch_shapes=[pltpu.VMEM((2, S, 128), BF16)] * 6 + [pltpu.VMEM((S, 128), F32)] * 3,
        compiler_params=_cp(), name=name,
    )(z, z, z, dy, lse, cc)


def _xattn_fwd(q, kv, *, B, S, M, tq, name):
    D = D_MODEL

    def body(q_ref, kv_ref, o_ref):
        for h in range(N_XH):
            cs = slice(XHD * h, XHD * (h + 1))
            s = _dot(q_ref[:, cs], kv_ref[:, cs], NT) * (1.0 / 16.0)
            pe = jnp.exp(s - jnp.max(s, axis=1, keepdims=True))
            l = jnp.sum(pe, axis=1, keepdims=True)
            o_ref[:, cs] = (_dot(pe.astype(BF16), kv_ref[:, D + XHD * h:D + XHD * (h + 1)]) * (1.0 / l)).astype(BF16)

    nq = S // tq
    return pl.pallas_call(
        body, grid=(B, nq),
        in_specs=[pl.BlockSpec((tq, D), lambda b, t: (b * nq + t, 0)), pl.BlockSpec((M, 2 * D), lambda b, t: (b, 0))],
        out_specs=pl.BlockSpec((tq, D), lambda b, t: (b * nq + t, 0)), out_shape=SDS((B * S, D), BF16),
        compiler_params=_cp(), name=name,
    )(q, kv)


def _xattn_bwd(q, kv, do, *, B, S, M, tq, name):
    D = D_MODEL

    def body(q_ref, kv_ref, do_ref, dq_ref, dkv_ref):
        t = pl.program_id(1)

        @pl.when(t == 0)
        def _():
            dkv_ref[...] = jnp.zeros_like(dkv_ref)

        for h in range(N_XH):
            cs = slice(XHD * h, XHD * (h + 1))
            vs = slice(D + XHD * h, D + XHD * (h + 1))
            qh, kh, vh, doh = q_ref[:, cs], kv_ref[:, cs], kv_ref[:, vs], do_ref[:, cs]
            s = _dot(qh, kh, NT) * (1.0 / 16.0)
            pe = jnp.exp(s - jnp.max(s, axis=1, keepdims=True))
            pe = pe * (1.0 / jnp.sum(pe, axis=1, keepdims=True))
            dp = _dot(doh, vh, NT)
            ds = (pe * (dp - jnp.sum(pe * dp, axis=1, keepdims=True))).astype(BF16)
            dq_ref[:, cs] = (_dot(ds, kh) * (1.0 / 16.0)).astype(BF16)
            dkv_ref[:, cs] += _dot(ds, qh, TN) * (1.0 / 16.0)
            dkv_ref[:, vs] += _dot(pe.astype(BF16), doh, TN)

    nq = S // tq
    qspec = pl.BlockSpec((tq, D), lambda b, t: (b * nq + t, 0))
    kvspec = pl.BlockSpec((M, 2 * D), lambda b, t: (b, 0))
    return pl.pallas_call(
        body, grid=(B, nq), in_specs=[qspec, kvspec, qspec], out_specs=[qspec, kvspec],
        out_shape=[SDS((B * S, D), BF16), SDS((B * M, 2 * D), F32)], compiler_params=_cp(), name=name,
    )(q, kv, do)


def _adamw(parts, w, m, v, *, tr, name):
    R, C = w.shape

    def body(p_ref, w_ref, m_ref, v_ref, g_ref, d_ref, nm_ref, nv_ref):
        g = p_ref[0].astype(F32)
        for d in range(1, N_DEV):
            g = g + p_ref[d].astype(F32)
        m2 = ADAM_B1 * m_ref[...] + (1.0 - ADAM_B1) * g
        v2 = ADAM_B2 * v_ref[...] + (1.0 - ADAM_B2) * (g * g)
        m_hat = m2 / (1.0 - ADAM_B1 ** ADAM_STEP)
        v_hat = v2 / (1.0 - ADAM_B2 ** ADAM_STEP)
        g_ref[...] = g
        d_ref[...] = -ADAM_LR * (m_hat / (jnp.sqrt(v_hat) + ADAM_EPS) + ADAM_WD * w_ref[...])
        nm_ref[...] = m2
        nv_ref[...] = v2

    spec = pl.BlockSpec((tr, C), lambda i: (i, 0))
    return pl.pallas_call(
        body, grid=(R // tr,), in_specs=[pl.BlockSpec((N_DEV, tr, C), lambda i: (0, i, 0)), spec, spec, spec],
        out_specs=[spec] * 4, out_shape=[SDS((R, C), F32)] * 4, compiler_params=_cp(), name=name,
    )(parts, w, m, v)


def _peer(k, x, y, c):
    return (1 - x if k & 4 else x, 1 - y if k & 2 else y, 1 - c if k & 1 else c)


_HBM_SPEC = pl.BlockSpec(memory_space=pltpu.HBM)
_SEM_SPEC = pl.BlockSpec(memory_space=pltpu.SEMAPHORE)
_SPLIT_EFFECT = pltpu.SideEffectType.DATAFLOW_SIDE_EFFECTING


def _split_copies(srcs, lands, send_sems, recv_sems, modes):
    x, y, c = (lax.axis_index(a) for a in AXES)
    me = 4 * x + 2 * y + c
    copies = []
    for i, md in enumerate(modes):
        for k in range(1, N_DEV):
            px, py, pc = _peer(k, x, y, c)
            src = srcs[i] if md == "gather" else srcs[i].at[4 * px + 2 * py + pc]
            j = i * (N_DEV - 1) + k - 1
            copies.append(pltpu.make_async_remote_copy(
                src_ref=src, dst_ref=lands[i].at[me], send_sem=send_sems.at[j], recv_sem=recv_sems.at[j],
                device_id=(px, py, pc), device_id_type=pl.DeviceIdType.MESH))
    return copies


def _exchange_start(arrays, modes, *, name):
    n = len(arrays)
    hbm = lambda a: pltpu.with_memory_space_constraint(a, pltpu.HBM)
    srcs = [hbm(a) for a in arrays]
    me = 4 * lax.axis_index("x") + 2 * lax.axis_index("y") + lax.axis_index("c")

    def landing(a, md):
        own = a[None] if md == "gather" else lax.dynamic_index_in_dim(a, me, 0, keepdims=True)
        return hbm(lax.dynamic_update_index_in_dim(lax.empty((N_DEV,) + own.shape[1:], a.dtype), own, me, 0))

    lands = [landing(a, md) for a, md in zip(arrays, modes)]

    def body(*refs):
        for cp in _split_copies(refs[:n], refs[n:2 * n], refs[2 * n], refs[2 * n + 1], modes):
            cp.start()
        token = refs[-1]
        token[...] = jnp.zeros_like(token)

    sems = pltpu.SemaphoreType.DMA((n * (N_DEV - 1),))
    outs = pl.pallas_call(
        body, name=name, in_specs=[_HBM_SPEC] * (2 * n),
        out_shape=(sems, sems, *[pltpu.HBM(a.shape, a.dtype) for a in srcs + lands], SDS((8, 128), F32)),
        out_specs=(_SEM_SPEC, _SEM_SPEC, *[_HBM_SPEC] * (2 * n), pl.BlockSpec(memory_space=pltpu.VMEM)),
        input_output_aliases={i: 2 + i for i in range(2 * n)},
        compiler_params=pltpu.CompilerParams(has_side_effects=_SPLIT_EFFECT),
    )(*srcs, *lands)
    return (outs[0], outs[1], outs[2:2 + n], outs[2 + n:2 + 2 * n], modes), outs[-1]


def _exchange_wait(handle, after, *, name):
    send_sems, recv_sems, srcs, lands, modes = handle
    n = len(srcs)

    def body(*refs):
        for cp in _split_copies(refs[:n], refs[n:2 * n], refs[2 * n], refs[2 * n + 1], modes):
            cp.wait_send()
            cp.wait_recv()

    outs = pl.pallas_call(
        body, name=name, in_specs=[_HBM_SPEC] * (2 * n) + [_SEM_SPEC, _SEM_SPEC, pl.BlockSpec(memory_space=pl.ANY)],
        out_shape=tuple(pltpu.HBM(a.shape, a.dtype) for a in list(srcs) + list(lands)), out_specs=tuple([_HBM_SPEC] * (2 * n)),
        input_output_aliases={i: i for i in range(2 * n)},
        compiler_params=pltpu.CompilerParams(has_side_effects=_SPLIT_EFFECT),
    )(*srcs, *lands, send_sems, recv_sems, after)
    return list(outs[n:])


def _exchange(arrays, modes, *, name):
    n = len(arrays)
    out_shape = [SDS((N_DEV,) + a.shape if md == "gather" else a.shape, a.dtype) for a, md in zip(arrays, modes)]

    def body(*refs):
        ins, outs = refs[:n], refs[n:2 * n]
        send_sems, recv_sems, local_sems = refs[2 * n:]
        x, y, c = (lax.axis_index(a) for a in AXES)
        me = 4 * x + 2 * y + c
        copies = []
        for i, md in enumerate(modes):
            src = ins[i] if md == "gather" else ins[i].at[me]
            cp = pltpu.make_async_copy(src, outs[i].at[me], local_sems.at[i])
            cp.start()
            copies.append(cp)
            for k in range(1, N_DEV):
                px, py, pc = _peer(k, x, y, c)
                src = ins[i] if md == "gather" else ins[i].at[4 * px + 2 * py + pc]
                cp = pltpu.make_async_remote_copy(
                    src_ref=src, dst_ref=outs[i].at[me], send_sem=send_sems.at[i, k - 1], recv_sem=recv_sems.at[i, k - 1],
                    device_id=(px, py, pc), device_id_type=pl.DeviceIdType.MESH)
                cp.start()
                copies.append(cp)
        for cp in copies:
            cp.wait()

    anyspec = pl.BlockSpec(memory_space=pl.ANY)
    return pl.pallas_call(
        body, in_specs=[anyspec] * n, out_specs=[anyspec] * n, out_shape=out_shape,
        scratch_shapes=[pltpu.SemaphoreType.DMA((n, N_DEV - 1)), pltpu.SemaphoreType.DMA((n, N_DEV - 1)),
                        pltpu.SemaphoreType.DMA((n,))],
        name=name,
    )(*arrays)


def _local_step(x, mem, g_mix, b_forget, g_xattn, g_mem, g_mlp, g_final, target, get_w_in, get_rest, send):
    B, S, D = x.shape
    M = mem.shape[1]
    T = B * S
    x0 = x.reshape(T, D)
    mem2 = mem.reshape(B * M, D)
    tgt = target.reshape(T, D)
    b_pad = jnp.pad(b_forget, (0, 120)).reshape(1, 128)
    after = lambda a, tok: a if tok is None else a + tok[0, 0]

    h1 = _rms(x0, g_mix, tm=1024, name="f_norm")
    w_in_pad = get_w_in(h1)
    _, z, gate = _rms_matmul(h1, g_mix, w_in_pad[:, :QKV_W], tm=ROWS, tn=QKV_W, out_dtype=BF16,
                             w_f32=w_in_pad[:, QKV_W:], normed=True, name="f_in")
    cc = _gate_fwd(gate, b_pad, B=B, S=S, name="f_gatecum")
    ya, lse = _dil_attn_fwd(z, B=B, S=S, name="f_dil")
    yf, lse_f = _fox_fwd(z, cc, B=B, S=S, name="f_fox")
    ymix = jnp.concatenate([ya, yf], axis=1)
    w = get_rest(ymix)
    x1, h2, q = _res_rms_matmul(ymix, w["w_out"], x0, g_xattn, w["w_xq"], tm=ROWS, name="f_out")
    mn, kv = _rms_matmul(mem2, g_mem, w["w_kv"], tm=B * M, tn=D, out_dtype=BF16, name="f_xkv")
    xo = _xattn_fwd(q, kv, B=B, S=S, M=M, tq=1024, name="f_xattn")
    x2, h3, act = _res_rms_matmul(xo, w["w_xo"], x1, g_mlp, w["w_up"], tm=ROWS, relu=True, name="f_xo")
    dx3, dg_final, loss = _down_loss(act, w["w_down"], x2, g_final, tgt, tm=ROWS, name="f_down")

    du = _matmul_nt(dx3, w["w_down"], mul2a=act, tm=ROWS, tn=D_FF, name="b_dact")
    dw_down = _matmul_tn(act, dx3, square=True, bk=1024, bn=D, tt=ACC_ROWS, out_dtype=BF16, name="b_wdown")
    dw_up = _matmul_tn(h3, du, bk=D, bn=1024, tt=ACC_ROWS, out_dtype=BF16, name="b_wup")
    tok = send(dict(w_down=dw_down, w_up=dw_up))
    dx2, dg_mlp, dxo = _matmul_nt_rms(du, w["w_up"], x2, after(g_mlp, tok), dx3, then_w=w["w_xo"], tm=ROWS, tk=D_FF,
                                      name="b_dh3")
    dw_xo = _matmul_tn(xo, dx2, bk=D, bn=D, tt=ACC_ROWS, out_dtype=BF16, name="b_wxo")
    dq, dkv = _xattn_bwd(q, kv, dxo, B=B, S=S, M=M, tq=1024, name="b_xattn")
    dw_xq = _matmul_tn(h2, dq, bk=D, bn=D, tt=ACC_ROWS, out_dtype=BF16, name="b_wxq")
    dx1, dg_xattn, dy = _matmul_nt_rms(dq, w["w_xq"], x1, g_xattn, dx2, then_w=w["w_out"], tm=ROWS, tk=D, name="b_dh2")
    dw_kv = _matmul_tn(mn, dkv, bk=D, bn=D, tt=B * M, out_dtype=BF16, name="b_wkv")
    _, dg_mem = _matmul_nt_rms(dkv, w["w_kv"], mem2, g_mem, None, tm=min(ROWS, B * M), tk=2 * D, name="b_dmem")
    dw_out = _matmul_tn(ymix, dx1, bk=D, bn=D, tt=ACC_ROWS, out_dtype=BF16, name="b_wout")
    tok = send(dict(w_xo=dw_xo, w_xq=dw_xq, w_xk=dw_kv[:, :D], w_xv=dw_kv[:, D:], w_out=dw_out))
    dqf, dkf, dvf, dcc = _fox_bwd(z, dy, lse_f, cc, B=B, S=S, name="b_fox")
    dgate, db = _gate_bwd(dcc, gate, after(b_pad, tok), B=B, S=S, name="b_gate")
    dqa, dka, dva = _dil_attn_bwd(z, dy, ya, lse, B=B, S=S, name="b_dil")
    dz = [dqa, dka, dva, dqf, dkf, dvf, dgate]
    dw_in = jnp.concatenate([_matmul_tn_pieces(h1, dz[:3], tt=ACC_ROWS, name="b_win_dil"),
                             _matmul_tn_pieces(h1, dz[3:], tt=ACC_ROWS, name="b_win_fox")], axis=1)
    tok = send(dict(w_in=dw_in))
    gx, dg_mix = _matmul_nt_rms(dz, w_in_pad, x0, after(g_mix, tok), dx1, tm=ROWS, tk=IN_PAD, name="b_dh1")

    small = dict(g_mix=dg_mix, b_forget=db, g_xattn=dg_xattn, g_mem=dg_mem, g_mlp=dg_mlp, g_final=dg_final)
    return gx.reshape(B, S, D), small, loss


SMALL_ROWS = ("g_mix", "b_forget", "g_xattn", "g_mem", "g_mlp", "g_final")
COL_SHARDED = ("w_in", "w_up")


def _pack_rows(rows):
    D = D_MODEL
    rows = [jnp.pad(r.reshape(-1), (0, D - r.size)) for r in rows]
    rows += [jnp.zeros((D,), F32)] * (8 - len(rows))
    return jnp.stack(rows)


def _full(name, g):
    if name in COL_SHARDED:
        return g.transpose(1, 0, 2).reshape(g.shape[1], -1)
    return g.reshape(-1, g.shape[2])


def _blocks(name, g, shard_shape):
    if name in COL_SHARDED:
        n = shard_shape[1]
        return g[:, :n * N_DEV].reshape(g.shape[0], N_DEV, n).transpose(1, 0, 2)
    return g.reshape((N_DEV,) + shard_shape)


def kernel(x, mem, g_mix, w_in, b_forget, w_out, g_xattn, g_mem, w_xq, w_xk, w_xv, w_xo, g_mlp, w_up, w_down, g_final, loss_target, m_g_mix, m_w_in, m_b_forget, m_w_out, m_g_xattn, m_g_mem, m_w_xq, m_w_xk, m_w_xv, m_w_xo, m_g_mlp, m_w_up, m_w_down, m_g_final, v_g_mix, v_w_in, v_b_forget, v_w_out, v_g_xattn, v_g_mem, v_w_xq, v_w_xk, v_w_xv, v_w_xo, v_g_mlp, v_w_up, v_w_down, v_g_final):
    W = dict(w_in=w_in, w_out=w_out, w_xq=w_xq, w_xk=w_xk, w_xv=w_xv, w_xo=w_xo, w_up=w_up, w_down=w_down)
    Mo = dict(w_in=m_w_in, w_out=m_w_out, w_xq=m_w_xq, w_xk=m_w_xk, w_xv=m_w_xv, w_xo=m_w_xo, w_up=m_w_up, w_down=m_w_down)
    Vo = dict(w_in=v_w_in, w_out=v_w_out, w_xq=v_w_xq, w_xk=v_w_xk, w_xv=v_w_xv, w_xo=v_w_xo, w_up=v_w_up, w_down=v_w_down)
    later = [n for n in W if n != "w_in"]

    first_handle, first_token = _exchange_start([w_in.astype(BF16)], ["gather"], name="gather_in_start")
    rest_handle, rest_token = _exchange_start([W[n].astype(BF16) + first_token[0, 0].astype(BF16) for n in later],
                                              ["gather"] * len(later), name="gather_rest_start")

    def get_w_in(after):
        (g,) = _exchange_wait(first_handle, after, name="gather_in_wait")
        return jnp.pad(_full("w_in", g), ((0, 0), (0, IN_PAD - IN_W)))

    def get_rest(after):
        full = {n: _full(n, g) for n, g in zip(later, _exchange_wait(rest_handle, after, name="gather_rest_wait"))}
        full["w_kv"] = jnp.concatenate([full.pop("w_xk"), full.pop("w_xv")], axis=1)
        return full

    sent = []

    def send(grads):
        names = list(grads)
        handle, token = _exchange_start([_blocks(n, grads[n], W[n].shape) for n in names], ["scatter"] * len(names),
                                        name=f"scatter{len(sent)}_start")
        sent.append((names, handle))
        return token

    gx, small, loss = _local_step(x, mem, g_mix + rest_token[0, 0], b_forget, g_xattn, g_mem, g_mlp, g_final, loss_target,
                                  get_w_in, get_rest, send)

    received = {}
    for i, (names, handle) in enumerate(sent):
        received.update(zip(names, _exchange_wait(handle, gx, name=f"scatter{i}_wait")))
    packed = _pack_rows([small[n] for n in SMALL_ROWS] + [loss[0, :1]])
    (packed_all,) = _exchange([packed], ["gather"], name="gather_small")

    rows_per_step = lambda shape: max(t for t in (128, 256, 512) if shape[0] % t == 0 and t * shape[1] <= 512 * 512)
    res = {n: _adamw(received[n], W[n], Mo[n], Vo[n], tr=rows_per_step(W[n].shape), name=f"adamw_{n}") for n in W}
    small_w = dict(g_mix=g_mix, b_forget=b_forget, g_xattn=g_xattn, g_mem=g_mem, g_mlp=g_mlp, g_final=g_final)
    small_m = dict(g_mix=m_g_mix, b_forget=m_b_forget, g_xattn=m_g_xattn, g_mem=m_g_mem, g_mlp=m_g_mlp, g_final=m_g_final)
    small_v = dict(g_mix=v_g_mix, b_forget=v_b_forget, g_xattn=v_g_xattn, g_mem=v_g_mem, g_mlp=v_g_mlp, g_final=v_g_final)
    sres = _adamw(packed_all, _pack_rows([small_w[n] for n in SMALL_ROWS]), _pack_rows([small_m[n] for n in SMALL_ROWS]),
                  _pack_rows([small_v[n] for n in SMALL_ROWS]), tr=8, name="adamw_small")
    for i, n in enumerate(SMALL_ROWS):
        res[n] = [r[i, :small_w[n].size] for r in sres]
    loss_total = sres[0][6, 0]

    order = ["g_mix", "w_in", "b_forget", "w_out", "g_xattn", "g_mem", "w_xq", "w_xk", "w_xv", "w_xo", "g_mlp", "w_up", "w_down", "g_final"]
    return (loss_total, gx, *[res[n][0] for n in order], *[res[n][1] for n in order],
            *[res[n][2] for n in order], *[res[n][3] for n in order])
```

```python
import jax
import jax.numpy as jnp
from jax import lax
from jax.experimental import pallas as pl
from jax.experimental.pallas import tpu as pltpu

F32, BF16 = jnp.float32, jnp.bfloat16
SDS = jax.ShapeDtypeStruct

D_MODEL = 1024
HEAD_DIM = 64
WIDTH = 512
QKV_W = 6 * WIDTH
IN_W = QKV_W + 8
IN_PAD = QKV_W + 128
BLOCK = 128
DIL_CONFIGS = ((128, 1), (512, 4), (2048, 16))
N_XH, XHD = 4, 256
D_FF = 4096
EPS = 1e-6
NEG = -1e30
N_DEV = 8
AXES = ("x", "y", "c")

ADAM_LR, ADAM_B1, ADAM_B2, ADAM_EPS, ADAM_WD, ADAM_STEP = 0.001, 0.9, 0.999, 1e-08, 0.01, 10

VMEM_CAP_V7X = 64 * 1024 * 1024
VMEM_LIMIT = VMEM_CAP_V7X * 7 // 8

ROWS = 512
ACC_ROWS = 2048

NT = (((1,), (1,)), ((), ()))
TN = (((0,), (0,)), ((), ()))


def _cp(**kw):
    return pltpu.CompilerParams(vmem_limit_bytes=VMEM_LIMIT, **kw)


def _dot(a, b, dims=None):
    if dims is None:
        return jnp.dot(a, b, preferred_element_type=F32)
    return lax.dot_general(a, b, dims, preferred_element_type=F32)


def _rstd(xv):
    return lax.rsqrt(jnp.mean(xv * xv, axis=-1, keepdims=True) + EPS)


def _rms_bwd(dh, xv, g):
    r = _rstd(xv)
    xhat = xv * r
    dxhat = dh * g
    dx = r * (dxhat - xhat * jnp.mean(dxhat * xhat, axis=-1, keepdims=True))
    return dx, jnp.sum(dh * xhat, axis=0, keepdims=True)


def _rms_matmul(x, g, w, *, tm, tn, out_dtype, relu=False, w_f32=None, normed=False, name):
    T, D = x.shape
    N = w.shape[1]

    def body(*refs):
        x_ref, g_ref, w_ref = refs[:3]
        h_ref, o_ref, h_s = refs[-3 - (w_f32 is not None)], refs[-2 - (w_f32 is not None)], refs[-1]

        @pl.when(pl.program_id(1) == 0)
        def _():
            xv = x_ref[...]
            h = xv if normed else (xv * _rstd(xv) * g_ref[...]).astype(BF16)
            h_s[...] = h
            h_ref[...] = h
            if w_f32 is not None:
                refs[-2][...] = _dot(h, refs[3][...])

        acc = _dot(h_s[...], w_ref[...])
        if relu:
            acc = jnp.maximum(acc, 0.0)
        o_ref[...] = acc.astype(out_dtype)

    in_specs = [pl.BlockSpec((tm, D), lambda i, j: (i, 0)), pl.BlockSpec((1, D), lambda i, j: (0, 0)),
                pl.BlockSpec((D, tn), lambda i, j: (0, j))]
    out_specs = [pl.BlockSpec((tm, D), lambda i, j: (i, 0)), pl.BlockSpec((tm, tn), lambda i, j: (i, j))]
    out_shape = [SDS((T, D), BF16), SDS((T, N), out_dtype)]
    args = [x, g.reshape(1, D), w]
    if w_f32 is not None:
        n2 = w_f32.shape[1]
        in_specs.append(pl.BlockSpec((D, n2), lambda i, j: (0, 0)))
        out_specs.append(pl.BlockSpec((tm, n2), lambda i, j: (i, 0)))
        out_shape.append(SDS((T, n2), F32))
        args.append(w_f32)
    return pl.pallas_call(
        body, grid=(T // tm, N // tn), in_specs=in_specs, out_specs=out_specs, out_shape=out_shape,
        scratch_shapes=[pltpu.VMEM((tm, D), BF16)], compiler_params=_cp(), name=name,
    )(*args)


def _rms(x, g, *, tm, name):
    T, D = x.shape

    def body(x_ref, g_ref, h_ref):
        xv = x_ref[...]
        h_ref[...] = (xv * _rstd(xv) * g_ref[...]).astype(BF16)

    rows = pl.BlockSpec((tm, D), lambda i: (i, 0))
    return pl.pallas_call(body, grid=(T // tm,), in_specs=[rows, pl.BlockSpec((1, D), lambda i: (0, 0))], out_specs=rows,
                          out_shape=SDS((T, D), BF16), compiler_params=_cp(), name=name)(x, g.reshape(1, D))


def _res_rms_matmul(a, w1, res, gain, w2, *, tm, relu=False, name):
    T, K = a.shape
    D, N = w2.shape

    def body(a_ref, w1_ref, res_ref, g_ref, w2_ref, x_ref, h_ref, o_ref):
        xv = res_ref[...] + _dot(a_ref[...], w1_ref[...])
        x_ref[...] = xv
        h = (xv * _rstd(xv) * g_ref[...]).astype(BF16)
        h_ref[...] = h
        acc = _dot(h, w2_ref[...])
        if relu:
            acc = jnp.maximum(acc, 0.0)
        o_ref[...] = acc.astype(BF16)

    rows = lambda n: pl.BlockSpec((tm, n), lambda i: (i, 0))
    whole = lambda r, c: pl.BlockSpec((r, c), lambda i: (0, 0))
    return pl.pallas_call(
        body, grid=(T // tm,), in_specs=[rows(K), whole(K, D), rows(D), whole(1, D), whole(D, N)],
        out_specs=[rows(D), rows(D), rows(N)], out_shape=[SDS((T, D), F32), SDS((T, D), BF16), SDS((T, N), BF16)],
        compiler_params=_cp(), name=name,
    )(a, w1, res, gain.reshape(1, D), w2)


def _matmul_nt(g, w, *, mul2a=None, tm, tn, name):
    T, K = g.shape
    N = w.shape[0]

    def body(*refs):
        g_ref, w_ref = refs[0], refs[1]
        o_ref = refs[-1]
        acc = _dot(g_ref[...].astype(BF16), w_ref[...], NT)
        if mul2a is not None:
            acc = acc * (2.0 * refs[2][...].astype(F32))
        o_ref[...] = acc.astype(BF16)

    in_specs = [pl.BlockSpec((tm, K), lambda i, j: (i, 0)), pl.BlockSpec((tn, K), lambda i, j: (j, 0))]
    args = [g, w]
    if mul2a is not None:
        in_specs.append(pl.BlockSpec((tm, tn), lambda i, j: (i, j)))
        args.append(mul2a)
    return pl.pallas_call(
        body, grid=(T // tm, N // tn), in_specs=in_specs,
        out_specs=pl.BlockSpec((tm, tn), lambda i, j: (i, j)), out_shape=SDS((T, N), BF16),
        compiler_params=_cp(), name=name,
    )(*args)


def _matmul_nt_rms(g, w, x, gain, dres, *, then_w=None, tm, tk, name):
    pieces = list(g) if isinstance(g, (list, tuple)) else [g]
    widths = [p.shape[1] for p in pieces]
    T, K = pieces[0].shape[0], sum(widths)
    D = w.shape[0]
    nk = K // tk
    nt = T // tm
    npc = len(pieces)
    assert npc == 1 or nk == 1
    n_in = npc + 3 + (dres is not None) + (then_w is not None)

    def body(*refs):
        w_ref, x_ref, gain_ref = refs[npc:npc + 3]
        dres_ref = refs[npc + 3] if dres is not None else None
        then_ref = refs[n_in - 1] if then_w is not None else None
        dx_ref, dg_ref = refs[n_in], refs[n_in + 1]
        i, k = pl.program_id(0), pl.program_id(1)
        if npc == 1:
            part = _dot(refs[0][...].astype(BF16), w_ref[...], NT)
        else:
            part, off = None, 0
            for j in range(npc):
                d = _dot(refs[j][...].astype(BF16), w_ref[:, off:off + widths[j]], NT)
                part = d if part is None else part + d
                off += widths[j]

        def finish(dh):
            dx, dg = _rms_bwd(dh, x_ref[...], gain_ref[...])
            if dres_ref is not None:
                dx = dres_ref[...] + dx
            dx_ref[...] = dx
            if then_ref is not None:
                refs[n_in + 2][...] = _dot(dx.astype(BF16), then_ref[...], NT).astype(BF16)

            @pl.when(i == 0)
            def _():
                dg_ref[...] = dg

            @pl.when(i > 0)
            def _():
                dg_ref[...] += dg

        if nk == 1:
            finish(part)
        else:
            acc = refs[-1]

            @pl.when(k == 0)
            def _():
                acc[...] = part

            @pl.when(k > 0)
            def _():
                acc[...] += part

            @pl.when(k == nk - 1)
            def _():
                finish(acc[...])

    g_specs = ([pl.BlockSpec((tm, tk), lambda i, k: (i, k))] if npc == 1 else
               [pl.BlockSpec((tm, wd), lambda i, k: (i, 0)) for wd in widths])
    in_specs = g_specs + [pl.BlockSpec((D, tk), lambda i, k: (0, k)),
                          pl.BlockSpec((tm, D), lambda i, k: (i, 0)), pl.BlockSpec((1, D), lambda i, k: (0, 0))]
    args = pieces + [w, x, gain.reshape(1, D)]
    out_specs = [pl.BlockSpec((tm, D), lambda i, k: (i, 0)), pl.BlockSpec((1, D), lambda i, k: (0, 0))]
    out_shape = [SDS((T, D), F32), SDS((1, D), F32)]
    if dres is not None:
        in_specs.append(pl.BlockSpec((tm, D), lambda i, k: (i, 0)))
        args.append(dres)
    if then_w is not None:
        n2 = then_w.shape[0]
        in_specs.append(pl.BlockSpec((n2, D), lambda i, k: (0, 0)))
        args.append(then_w)
        out_specs.append(pl.BlockSpec((tm, n2), lambda i, k: (i, 0)))
        out_shape.append(SDS((T, n2), BF16))
    return pl.pallas_call(
        body, grid=(nt, nk), in_specs=in_specs, out_specs=out_specs, out_shape=out_shape,
        scratch_shapes=[pltpu.VMEM((tm, D), F32)] if nk > 1 else [], compiler_params=_cp(), name=name,
    )(*args)


def _matmul_tn(a, g, *, square=False, bk, bn, tt, out_dtype, name):
    T, K = a.shape
    N = g.shape[1]
    nt = T // tt

    def body(a_ref, g_ref, o_ref, acc):
        t = pl.program_id(2)
        av = a_ref[...]
        if square:
            af = av.astype(F32)
            av = (af * af).astype(BF16)
        part = _dot(av, g_ref[...].astype(BF16), TN)
        if nt == 1:
            o_ref[...] = part.astype(out_dtype)
        else:
            @pl.when(t == 0)
            def _():
                acc[...] = part

            @pl.when((t > 0) & (t < nt - 1))
            def _():
                acc[...] += part

            @pl.when(t == nt - 1)
            def _():
                o_ref[...] = (acc[...] + part).astype(out_dtype)

    return pl.pallas_call(
        body, grid=(K // bk, N // bn, nt),
        in_specs=[pl.BlockSpec((tt, bk), lambda i, j, t: (t, i)), pl.BlockSpec((tt, bn), lambda i, j, t: (t, j))],
        out_specs=pl.BlockSpec((bk, bn), lambda i, j, t: (i, j)), out_shape=SDS((K, N), out_dtype),
        scratch_shapes=[pltpu.VMEM((bk, bn), F32)], compiler_params=_cp(), name=name,
    )(a, g)


def _matmul_tn_pieces(a, pieces, *, tt, name):
    T, K = a.shape
    widths = [p.shape[1] for p in pieces]
    W = sum(widths)
    nt = T // tt
    n = len(pieces)

    def body(*refs):
        a_ref, o_ref, acc = refs[0], refs[n + 1], refs[n + 2]
        t = pl.program_id(0)
        av = a_ref[...]
        off = 0
        for j in range(n):
            cols = slice(off, off + widths[j])
            part = _dot(av, refs[1 + j][...], TN)

            @pl.when(t == 0)
            def _():
                acc[:, cols] = part

            @pl.when(t > 0)
            def _():
                acc[:, cols] += part

            off += widths[j]

        @pl.when(t == nt - 1)
        def _():
            o_ref[...] = acc[...].astype(BF16)

    return pl.pallas_call(
        body, grid=(nt,),
        in_specs=[pl.BlockSpec((tt, K), lambda t: (t, 0))] + [pl.BlockSpec((tt, w), lambda t: (t, 0)) for w in widths],
        out_specs=pl.BlockSpec((K, W), lambda t: (0, 0)), out_shape=SDS((K, W), BF16),
        scratch_shapes=[pltpu.VMEM((K, W), F32)], compiler_params=_cp(), name=name,
    )(a, *pieces)


def _down_loss(act, w_down, x2, g_final, target, *, tm, name):
    T, D = x2.shape
    F = act.shape[1]

    def body(a_ref, w_ref, x2_ref, g_ref, t_ref, dx_ref, dg_ref, loss_ref):
        i = pl.program_id(0)
        af = a_ref[...].astype(F32)
        xv, g = x2_ref[...] + _dot((af * af).astype(BF16), w_ref[...]), g_ref[...]
        r = _rstd(xv)
        xhat = xv * r
        diff = xhat * g - t_ref[...]
        part = 0.5 * jnp.sum(jnp.mean(diff * diff, axis=-1, keepdims=True), axis=0, keepdims=True)
        dy = diff * (1.0 / D)
        dxhat = dy * g
        dx_ref[...] = r * (dxhat - xhat * jnp.mean(dxhat * xhat, axis=-1, keepdims=True))
        dg = jnp.sum(dy * xhat, axis=0, keepdims=True)
        lp = jnp.broadcast_to(part, loss_ref.shape)

        @pl.when(i == 0)
        def _():
            dg_ref[...] = dg
            loss_ref[...] = lp

        @pl.when(i > 0)
        def _():
            dg_ref[...] += dg
            loss_ref[...] += lp

    rows = pl.BlockSpec((tm, D), lambda i: (i, 0))
    return pl.pallas_call(
        body, grid=(T // tm,),
        in_specs=[pl.BlockSpec((tm, F), lambda i: (i, 0)), pl.BlockSpec((F, D), lambda i: (0, 0)), rows,
                  pl.BlockSpec((1, D), lambda i: (0, 0)), rows],
        out_specs=[rows, pl.BlockSpec((1, D), lambda i: (0, 0)), pl.BlockSpec((8, 128), lambda i: (0, 0))],
        out_shape=[SDS((T, D), F32), SDS((1, D), F32), SDS((8, 128), F32)],
        compiler_params=_cp(), name=name,
    )(act, w_down, x2, g_final.reshape(1, D), target)


def _head_lanes(shape, width):
    return lax.broadcasted_iota(jnp.int32, shape, len(shape) - 1) // width


def _gate_fwd(gate, b_pad, *, B, S, name):
    def body(g_ref, b_ref, cc_ref):
        xv = g_ref[...] + b_ref[...]
        lf = jnp.minimum(xv, 0.0) - jnp.log(1.0 + jnp.exp(-jnp.abs(xv)))
        lane = lax.broadcasted_iota(jnp.int32, lf.shape, 1)
        row = lax.broadcasted_iota(jnp.int32, lf.shape, 0)
        c = jnp.where(lane < 8, lf, 0.0)
        sh = 1
        while sh < S:
            c = c + jnp.where(row >= sh, pltpu.roll(c, sh, 0), 0.0)
            sh *= 2
        grp = _head_lanes((S, WIDTH), HEAD_DIM)
        cc = jnp.zeros((S, WIDTH), F32)
        for h in range(8):
            cc = jnp.where(grp == h, c[:, h:h + 1], cc)
        cc_ref[...] = cc

    return pl.pallas_call(
        body, grid=(B,),
        in_specs=[pl.BlockSpec((S, 128), lambda b: (b, 0)), pl.BlockSpec((1, 128), lambda b: (0, 0))],
        out_specs=pl.BlockSpec((S, WIDTH), lambda b: (b, 0)), out_shape=SDS((B * S, WIDTH), F32),
        compiler_params=_cp(), name=name,
    )(gate, b_pad)


def _gate_bwd(dcc, gate, b_pad, *, B, S, name):
    def body(dcc_ref, g_ref, b_ref, dg_ref, db_ref):
        bi = pl.program_id(0)
        dccv = dcc_ref[...]
        lane = lax.broadcasted_iota(jnp.int32, (S, 128), 1)
        row = lax.broadcasted_iota(jnp.int32, (S, 128), 0)
        dc = jnp.zeros((S, 128), F32)
        for h in range(8):
            dc = jnp.where(lane == h, dccv[:, HEAD_DIM * h:HEAD_DIM * h + 1], dc)
        sh = 1
        while sh < S:
            dc = dc + jnp.where(row < S - sh, pltpu.roll(dc, S - sh, 0), 0.0)
            sh *= 2
        xv = g_ref[...] + b_ref[...]
        dgate = jnp.where(lane < 8, dc / (1.0 + jnp.exp(xv)), 0.0)
        dg_ref[...] = dgate.astype(BF16)
        db = jnp.sum(dgate, axis=0, keepdims=True)

        @pl.when(bi == 0)
        def _():
            db_ref[...] = db

        @pl.when(bi > 0)
        def _():
            db_ref[...] += db

    return pl.pallas_call(
        body, grid=(B,),
        in_specs=[pl.BlockSpec((S, WIDTH), lambda b: (b, 0)), pl.BlockSpec((S, 128), lambda b: (b, 0)),
                  pl.BlockSpec((1, 128), lambda b: (0, 0))],
        out_specs=[pl.BlockSpec((S, 128), lambda b: (b, 0)), pl.BlockSpec((1, 128), lambda b: (0, 0))],
        out_shape=[SDS((B * S, 128), BF16), SDS((1, 128), F32)],
        compiler_params=_cp(), name=name,
    )(dcc, gate, b_pad)


_SMEM_SPEC = pl.BlockSpec(memory_space=pltpu.SMEM)


def _alibi_slopes():
    return 2.0 ** (-(jnp.arange(1, 9, dtype=F32) * (8.0 / 8)))


def _pair_masks():
    lane = lax.broadcasted_iota(jnp.int32, (1, 128), 1)
    first = lane < HEAD_DIM
    return (first.astype(BF16), (~first).astype(BF16)), first


BNT =(((2,), (2,)), ((0,), (0,)))
BNN = (((2,), (1,)), ((0,), (0,)))
BTN = (((1,), (1,)), ((0,), (0,)))


def _split3(v):
    hi = v.astype(BF16).astype(F32)
    mid = (v - hi).astype(BF16).astype(F32)
    lo = (v - hi - mid).astype(BF16).astype(F32)
    return [hi, mid, lo]


def _with_spare_lanes(base, e, cols):
    lane = lax.broadcasted_iota(jnp.int32, (1, 128), 1)
    off = HEAD_DIM * (1 - e)
    extra = jnp.zeros(base.shape, F32)
    for j, c in enumerate(cols):
        extra = jnp.where(lane == off + j, c, extra)
    return base + extra.astype(BF16)


ONES3 = [1.0, 1.0, 1.0]


def _band_bias(slope, dilation):
    qi = lax.broadcasted_iota(jnp.int32, (BLOCK, BLOCK), 0)
    kj = lax.broadcasted_iota(jnp.int32, (BLOCK, BLOCK), 1)
    cur = jnp.where(kj <= qi, (-slope * dilation) * (qi - kj).astype(F32), NEG)
    prev = jnp.where(kj >= qi, (-slope * dilation) * (qi + BLOCK - kj).astype(F32), NEG)
    return cur, prev


def _to_residue_major(dst, src_f32, dilation, nb, lead=0):
    L = nb * BLOCK
    for r in range(dilation):
        rows = src_f32[pl.ds(r, L, stride=dilation), :] if dilation > 1 else src_f32[...]
        dst[lead + r * nb:lead + (r + 1) * nb] = rows.reshape(nb, BLOCK, 128).astype(dst.dtype)


def _dil_attn_fwd(z, *, B, S, name):
    NB = S // BLOCK

    def body(slope_ref, q_ref, k_ref, v_ref, y_ref, lse_ref, qf, kf, vf, qd, kd, vd, od, ld, acc_o, acc_l):
        (m_first, m_second), first = _pair_masks()
        p = pl.program_id(1)
        qf[...] = q_ref[...].astype(F32)
        kf[...] = k_ref[...].astype(F32)
        vf[...] = v_ref[...].astype(F32)
        kd[0] = jnp.zeros((BLOCK, 128), BF16)
        vd[0] = jnp.zeros((BLOCK, 128), BF16)
        blk = lax.broadcasted_iota(jnp.int32, (NB, 1, 1), 0)

        for idx, (_, dilation) in enumerate(DIL_CONFIGS):
            nb = NB // dilation
            _to_residue_major(qd, qf, dilation, nb)
            _to_residue_major(kd, kf, dilation, nb, lead=1)
            _to_residue_major(vd, vf, dilation, nb, lead=1)
            q4, kc, vc = qd[...], kd[1:NB + 1], vd[1:NB + 1]
            outs, lses = [], []
            for e, hm in enumerate((m_first, m_second)):
                bias_cur, bias_prev = _band_bias(slope_ref[2 * p + e], dilation)
                qm = q4 * hm
                sc = _dot(qm, kc, BNT) * 0.125 + bias_cur
                m = jnp.max(sc, axis=2, keepdims=True)
                if nb > 1:
                    sp = _dot(qm, kd[0:NB], BNT) * 0.125 + jnp.where(blk % nb == 0, NEG, bias_prev)
                    m = jnp.maximum(m, jnp.max(sp, axis=2, keepdims=True))
                pc = jnp.exp(sc - m)
                l = jnp.sum(pc, axis=2, keepdims=True)
                o = _dot(pc.astype(BF16), vc, BNN)
                if nb > 1:
                    pp = jnp.exp(sp - m)
                    l = l + jnp.sum(pp, axis=2, keepdims=True)
                    o = o + _dot(pp.astype(BF16), vd[0:NB], BNN)
                outs.append(o * (1.0 / l))
                lses.append(m + jnp.log(l))
            od[...] = jnp.where(first, outs[0], outs[1])
            ld[...] = jnp.where(first, lses[0], lses[1])

            L = nb * BLOCK
            for r in range(dilation):
                rows = pl.ds(r, L, stride=dilation) if dilation > 1 else slice(None)
                o_new = od[r * nb:(r + 1) * nb].reshape(L, 128)
                l_new = ld[r * nb:(r + 1) * nb].reshape(L, 128)
                if idx == 0:
                    acc_o[rows, :] = o_new
                    acc_l[rows, :] = l_new
                else:
                    l_old = acc_l[rows, :]
                    m2 = jnp.maximum(l_old, l_new)
                    w_old, w_new = jnp.exp(l_old - m2), jnp.exp(l_new - m2)
                    tot = w_old + w_new
                    acc_o[rows, :] = (w_old * acc_o[rows, :] + w_new * o_new) * (1.0 / tot)
                    acc_l[rows, :] = m2 + jnp.log(tot)

        y_ref[...] = acc_o[...].astype(BF16)
        lse_ref[...] = acc_l[...]

    spec = lambda off: pl.BlockSpec((S, 128), lambda b, p: (b, 4 * off + p))
    ospec = pl.BlockSpec((S, 128), lambda b, p: (b, p))
    blocks = lambda n, dt: pltpu.VMEM((n, BLOCK, 128), dt)
    return pl.pallas_call(
        body, grid=(B, 4), in_specs=[_SMEM_SPEC, spec(0), spec(1), spec(2)], out_specs=[ospec, ospec],
        out_shape=[SDS((B * S, WIDTH), BF16), SDS((B * S, WIDTH), F32)],
        scratch_shapes=[pltpu.VMEM((S, 128), F32)] * 3 + [blocks(NB, BF16), blocks(NB + 1, BF16), blocks(NB + 1, BF16),
                                                         blocks(NB, F32), blocks(NB, F32)] + [pltpu.VMEM((S, 128), F32)] * 2,
        compiler_params=_cp(), name=name,
    )(_alibi_slopes(), z, z, z)


def _dil_attn_bwd(z, dy, ya, lse, *, B, S, name):
    NB = S // BLOCK

    def body(slope_ref, q_ref, k_ref, v_ref, do_ref, o_ref, lse_ref, dq_ref, dk_ref, dv_ref,
             qf, kf, vf, dof, ef, qd, dod, kd, vd, lsd, ed, dkd, dvd, dqa, dka, dva):
        (m_first, m_second), first = _pair_masks()
        p = pl.program_id(1)
        qf[...] = q_ref[...].astype(F32)
        kf[...] = k_ref[...].astype(F32)
        vf[...] = v_ref[...].astype(F32)
        dov = do_ref[...].astype(F32)
        dof[...] = dov
        prod = dov * o_ref[...].astype(F32)
        ef[...] = jnp.where(first, jnp.sum(jnp.where(first, prod, 0.0), axis=1, keepdims=True),
                            jnp.sum(jnp.where(first, 0.0, prod), axis=1, keepdims=True))
        kd[0] = jnp.zeros((BLOCK, 128), BF16)
        vd[0] = jnp.zeros((BLOCK, 128), BF16)
        blk = lax.broadcasted_iota(jnp.int32, (NB, 1, 1), 0)

        for idx, (_, dilation) in enumerate(DIL_CONFIGS):
            nb = NB // dilation
            _to_residue_major(qd, qf, dilation, nb)
            _to_residue_major(dod, dof, dilation, nb)
            _to_residue_major(kd, kf, dilation, nb, lead=1)
            _to_residue_major(vd, vf, dilation, nb, lead=1)
            _to_residue_major(lsd, lse_ref, dilation, nb)
            _to_residue_major(ed, ef, dilation, nb)
            q4, do4, kc, vc = qd[...], dod[...], kd[1:NB + 1], vd[1:NB + 1]
            dq4 = None
            dkc = dvc = dkp = dvp = None
            for e, hm in enumerate((m_first, m_second)):
                lane0 = slice(HEAD_DIM * e, HEAD_DIM * e + 1)
                bias_cur, bias_prev = _band_bias(slope_ref[2 * p + e], dilation)
                qm, dom = q4 * hm, do4 * hm
                lse_e, e_e = lsd[...][:, :, lane0], ed[...][:, :, lane0]
                pc = jnp.exp(_dot(qm, kc, BNT) * 0.125 + bias_cur - lse_e)
                dsc = (pc * (_dot(dom, vc, BNT) - e_e)).astype(BF16)
                pcb = pc.astype(BF16)
                dqe = _dot(dsc, kc, BNN)
                dkc = _dot(dsc, qm, BTN) if e == 0 else dkc + _dot(dsc, qm, BTN)
                dvc = _dot(pcb, dom, BTN) if e == 0 else dvc + _dot(pcb, dom, BTN)
                if nb > 1:
                    kp, vp = kd[0:NB], vd[0:NB]
                    pp = jnp.exp(_dot(qm, kp, BNT) * 0.125 + jnp.where(blk % nb == 0, NEG, bias_prev) - lse_e)
                    dsp = (pp * (_dot(dom, vp, BNT) - e_e)).astype(BF16)
                    ppb = pp.astype(BF16)
                    dqe = dqe + _dot(dsp, kp, BNN)
                    dkp = _dot(dsp, qm, BTN) if e == 0 else dkp + _dot(dsp, qm, BTN)
                    dvp = _dot(ppb, dom, BTN) if e == 0 else dvp + _dot(ppb, dom, BTN)
                dq4 = dqe if e == 0 else jnp.where(first, dq4, dqe)

            dkd[1:NB + 1] = dkc
            dvd[1:NB + 1] = dvc
            if nb > 1:
                dkd[1:NB] += dkp[1:NB]
                dvd[1:NB] += dvp[1:NB]
            L = nb * BLOCK
            for r in range(dilation):
                rows = pl.ds(r, L, stride=dilation) if dilation > 1 else slice(None)
                dq_r = dq4[r * nb:(r + 1) * nb].reshape(L, 128) * 0.125
                dk_r = dkd[1 + r * nb:1 + (r + 1) * nb].reshape(L, 128) * 0.125
                dv_r = dvd[1 + r * nb:1 + (r + 1) * nb].reshape(L, 128)
                if idx == 0:
                    dqa[rows, :], dka[rows, :], dva[rows, :] = dq_r, dk_r, dv_r
                else:
                    dqa[rows, :] += dq_r
                    dka[rows, :] += dk_r
                    dva[rows, :] += dv_r

        dq_ref[...] = dqa[...].astype(BF16)
        dk_ref[...] = dka[...].astype(BF16)
        dv_ref[...] = dva[...].astype(BF16)

    spec = lambda off: pl.BlockSpec((S, 128), lambda b, p: (b, 4 * off + p))
    ospec = pl.BlockSpec((S, 128), lambda b, p: (b, p))
    blocks = lambda n, dt: pltpu.VMEM((n, BLOCK, 128), dt)
    return pl.pallas_call(
        body, grid=(B, 4), in_specs=[_SMEM_SPEC, spec(0), spec(1), spec(2), ospec, ospec, ospec],
        out_specs=[ospec] * 3, out_shape=[SDS((B * S, WIDTH), BF16)] * 3,
        scratch_shapes=[pltpu.VMEM((S, 128), F32)] * 5
        + [blocks(NB, BF16), blocks(NB, BF16), blocks(NB + 1, BF16), blocks(NB + 1, BF16), blocks(NB, F32), blocks(NB, F32),
           blocks(NB + 1, F32), blocks(NB + 1, F32)] + [pltpu.VMEM((S, 128), F32)] * 3,
        compiler_params=_cp(), name=name,
    )(_alibi_slopes(), z, z, z, dy, ya, lse)


FOX_TQ = 256
FOX_TQ_FWD = 512


def _fox_fwd(z, cc, *, B, S, name):
    def body(q_ref, k_ref, v_ref, cc_ref, o_ref, l_ref, qa, ka):
        (m_first, m_second), first = _pair_masks()
        ccv = cc_ref[...]
        eighth = jnp.asarray(0.125, BF16)
        for e, hm in enumerate((m_first, m_second)):
            c_e = jnp.broadcast_to(ccv[:, HEAD_DIM * e:HEAD_DIM * e + 1], (S, 128))
            qa[e] = _with_spare_lanes(q_ref[...] * hm * eighth, e, _split3(c_e) + ONES3)
            ka[e] = _with_spare_lanes(k_ref[...] * hm, e, ONES3 + _split3(-c_e))
        for qi in range(S // FOX_TQ_FWD):
            r0, kend = qi * FOX_TQ_FWD, (qi + 1) * FOX_TQ_FWD
            vv = v_ref[0:kend, :]
            row = lax.broadcasted_iota(jnp.int32, (FOX_TQ_FWD, kend), 0) + r0
            col = lax.broadcasted_iota(jnp.int32, (FOX_TQ_FWD, kend), 1)
            causal = col <= row
            outs, lses = [], []
            for e in (0, 1):
                s = jnp.where(causal, _dot(qa[e, r0:kend, :], ka[e, 0:kend, :], NT), NEG)
                m = jnp.max(s, axis=1, keepdims=True)
                pe = jnp.exp(s - m)
                l = jnp.sum(pe, axis=1, keepdims=True)
                outs.append(_dot(pe.astype(BF16), vv) * (1.0 / l))
                lses.append(m + jnp.log(l))
            o_ref[r0:kend, :] = jnp.where(first, outs[0], outs[1]).astype(BF16)
            l_ref[r0:kend, :] = jnp.where(first, lses[0], lses[1])

    spec = lambda off: pl.BlockSpec((S, 128), lambda b, p: (b, 4 * off + p))
    pspec = pl.BlockSpec((S, 128), lambda b, p: (b, p))
    return pl.pallas_call(
        body, grid=(B, 4), in_specs=[spec(3), spec(4), spec(5), pspec], out_specs=[pspec, pspec],
        out_shape=[SDS((B * S, WIDTH), BF16), SDS((B * S, WIDTH), F32)],
        scratch_shapes=[pltpu.VMEM((2, S, 128), BF16)] * 2, compiler_params=_cp(), name=name,
    )(z, z, z, cc)


def _fox_bwd(z, dy, lse, cc, *, B, S, name):
    def body(q_ref, k_ref, v_ref, do_ref, lse_ref, cc_ref, dq_ref, dk_ref, dv_ref, dc_ref,
             qa, ka, qp, kp, vp, dp, dk_s, dv_s, dc_s):
        (m_first, m_second), first = _pair_masks()
        ccv, lsev = cc_ref[...], lse_ref[...]
        eighth = jnp.asarray(0.125, BF16)
        for e, hm in enumerate((m_first, m_second)):
            lane0 = slice(HEAD_DIM * e, HEAD_DIM * e + 1)
            c_e = jnp.broadcast_to(ccv[:, lane0], (S, 128))
            lse_e = jnp.broadcast_to(lsev[:, lane0], (S, 128))
            qp[e] = q_ref[...] * hm
            kp[e] = k_ref[...] * hm
            dp[e] = do_ref[...] * hm
            qa[e] = _with_spare_lanes(qp[e] * eighth, e, _split3(c_e - lse_e) + ONES3)
            ka[e] = _with_spare_lanes(kp[e], e, ONES3 + _split3(-c_e))
            vp[e] = v_ref[...] * hm
        dk_s[...] = jnp.zeros_like(dk_s)
        dv_s[...] = jnp.zeros_like(dv_s)
        dc_s[...] = jnp.zeros_like(dc_s)
        for qi in range(S // FOX_TQ):
            r0, kend = qi * FOX_TQ, (qi + 1) * FOX_TQ
            krow = lax.broadcasted_iota(jnp.int32, (kend, FOX_TQ), 0)
            qcol = lax.broadcasted_iota(jnp.int32, (kend, FOX_TQ), 1) + r0
            causal = krow <= qcol
            dq_t = jnp.zeros((FOX_TQ, 128), F32)
            for e in (0, 1):
                sel = first if e == 0 else ~first
                pt = jnp.where(causal, jnp.exp(_dot(ka[e, 0:kend, :], qa[e, r0:kend, :], NT)), 0.0)
                dpt = _dot(vp[e, 0:kend, :], dp[e, r0:kend, :], NT)
                mean = jnp.sum(pt * dpt, axis=0, keepdims=True) / jnp.sum(pt, axis=0, keepdims=True)
                dst = pt * (dpt - mean)
                dsb = dst.astype(BF16)
                dv_s[0:kend, :] += _dot(pt.astype(BF16), dp[e, r0:kend, :])
                dk_s[0:kend, :] += _dot(dsb, qp[e, r0:kend, :]) * 0.125
                dq_t = dq_t + _dot(dsb, kp[e, 0:kend, :], TN)
                dc_s[0:kend, :] += jnp.where(sel, -jnp.sum(dst, axis=1, keepdims=True), 0.0)
            dq_ref[r0:kend, :] = (dq_t * 0.125).astype(BF16)
        dk_ref[...] = dk_s[...].astype(BF16)
        dv_ref[...] = dv_s[...].astype(BF16)
        dc_ref[...] = dc_s[...]

    spec = lambda off: pl.BlockSpec((S, 128), lambda b, p: (b, 4 * off + p))
    pspec = pl.BlockSpec((S, 128), lambda b, p: (b, p))
    return pl.pallas_call(
        body, grid=(B, 4),
        in_specs=[spec(3), spec(4), spec(5), pl.BlockSpec((S, 128), lambda b, p: (b, 4 + p)), pspec, pspec],
        out_specs=[pspec] * 4,
        out_shape=[SDS((B * S, WIDTH), BF16)] * 3 + [SDS((B * S, WIDTH), F32)],
        scratch_shapes=[pltpu.VMEM((2, S, 128), BF16)] * 6 + [pltpu.VMEM((S, 128), F32)] * 3,
        compiler_params=_cp(), name=name,
    )(z, z, z, dy, lse, cc)


def _xattn_fwd(q, kv, *, B, S, M, tq, name):
    D = D_MODEL

    def body(q_ref, kv_ref, o_ref):
        for h in range(N_XH):
            cs = slice(XHD * h, XHD * (h + 1))
            s = _dot(q_ref[:, cs], kv_ref[:, cs], NT) * (1.0 / 16.0)
            pe = jnp.exp(s - jnp.max(s, axis=1, keepdims=True))
            l = jnp.sum(pe, axis=1, keepdims=True)
            o_ref[:, cs] = (_dot(pe.astype(BF16), kv_ref[:, D + XHD * h:D + XHD * (h + 1)]) * (1.0 / l)).astype(BF16)

    nq = S // tq
    return pl.pallas_call(
        body, grid=(B, nq),
        in_specs=[pl.BlockSpec((tq, D), lambda b, t: (b * nq + t, 0)), pl.BlockSpec((M, 2 * D), lambda b, t: (b, 0))],
        out_specs=pl.BlockSpec((tq, D), lambda b, t: (b * nq + t, 0)), out_shape=SDS((B * S, D), BF16),
        compiler_params=_cp(), name=name,
    )(q, kv)


def _xattn_bwd(q, kv, do, *, B, S, M, tq, name):
    D = D_MODEL

    def body(q_ref, kv_ref, do_ref, dq_ref, dkv_ref):
        t = pl.program_id(1)

        @pl.when(t == 0)
        def _():
            dkv_ref[...] = jnp.zeros_like(dkv_ref)

        for h in range(N_XH):
            cs = slice(XHD * h, XHD * (h + 1))
            vs = slice(D + XHD * h, D + XHD * (h + 1))
            qh, kh, vh, doh = q_ref[:, cs], kv_ref[:, cs], kv_ref[:, vs], do_ref[:, cs]
            s = _dot(qh, kh, NT) * (1.0 / 16.0)
            pe = jnp.exp(s - jnp.max(s, axis=1, keepdims=True))
            pe = pe * (1.0 / jnp.sum(pe, axis=1, keepdims=True))
            dp = _dot(doh, vh, NT)
            ds = (pe * (dp - jnp.sum(pe * dp, axis=1, keepdims=True))).astype(BF16)
            dq_ref[:, cs] = (_dot(ds, kh) * (1.0 / 16.0)).astype(BF16)
            dkv_ref[:, cs] += _dot(ds, qh, TN) * (1.0 / 16.0)
            dkv_ref[:, vs] += _dot(pe.astype(BF16), doh, TN)

    nq = S // tq
    qspec = pl.BlockSpec((tq, D), lambda b, t: (b * nq + t, 0))
    kvspec = pl.BlockSpec((M, 2 * D), lambda b, t: (b, 0))
    return pl.pallas_call(
        body, grid=(B, nq), in_specs=[qspec, kvspec, qspec], out_specs=[qspec, kvspec],
        out_shape=[SDS((B * S, D), BF16), SDS((B * M, 2 * D), F32)], compiler_params=_cp(), name=name,
    )(q, kv, do)


def _adamw(parts, w, m, v, *, tr, name):
    R, C = w.shape

    def body(p_ref, w_ref, m_ref, v_ref, g_ref, d_ref, nm_ref, nv_ref):
        g = p_ref[0].astype(F32)
        for d in range(1, N_DEV):
            g = g + p_ref[d].astype(F32)
        m2 = ADAM_B1 * m_ref[...] + (1.0 - ADAM_B1) * g
        v2 = ADAM_B2 * v_ref[...] + (1.0 - ADAM_B2) * (g * g)
        m_hat = m2 / (1.0 - ADAM_B1 ** ADAM_STEP)
        v_hat = v2 / (1.0 - ADAM_B2 ** ADAM_STEP)
        g_ref[...] = g
        d_ref[...] = -ADAM_LR * (m_hat / (jnp.sqrt(v_hat) + ADAM_EPS) + ADAM_WD * w_ref[...])
        nm_ref[...] = m2
        nv_ref[...] = v2

    spec = pl.BlockSpec((tr, C), lambda i: (i, 0))
    return pl.pallas_call(
        body, grid=(R // tr,), in_specs=[pl.BlockSpec((N_DEV, tr, C), lambda i: (0, i, 0)), spec, spec, spec],
        out_specs=[spec] * 4, out_shape=[SDS((R, C), F32)] * 4, compiler_params=_cp(), name=name,
    )(parts, w, m, v)


def _peer(k, x, y, c):
    return (1 - x if k & 4 else x, 1 - y if k & 2 else y, 1 - c if k & 1 else c)


_HBM_SPEC = pl.BlockSpec(memory_space=pltpu.HBM)
_SEM_SPEC = pl.BlockSpec(memory_space=pltpu.SEMAPHORE)
_SPLIT_EFFECT = pltpu.SideEffectType.DATAFLOW_SIDE_EFFECTING


def _split_copies(srcs, lands, send_sems, recv_sems, modes):
    x, y, c = (lax.axis_index(a) for a in AXES)
    me = 4 * x + 2 * y + c
    copies = []
    for i, md in enumerate(modes):
        for k in range(1, N_DEV):
            px, py, pc = _peer(k, x, y, c)
            src = srcs[i] if md == "gather" else srcs[i].at[4 * px + 2 * py + pc]
            j = i * (N_DEV - 1) + k - 1
            copies.append(pltpu.make_async_remote_copy(
                src_ref=src, dst_ref=lands[i].at[me], send_sem=send_sems.at[j], recv_sem=recv_sems.at[j],
                device_id=(px, py, pc), device_id_type=pl.DeviceIdType.MESH))
    return copies


def _exchange_start(arrays, modes, *, name):
    n = len(arrays)
    hbm = lambda a: pltpu.with_memory_space_constraint(a, pltpu.HBM)
    srcs = [hbm(a) for a in arrays]
    me = 4 * lax.axis_index("x") + 2 * lax.axis_index("y") + lax.axis_index("c")

    def landing(a, md):
        own = a[None] if md == "gather" else lax.dynamic_index_in_dim(a, me, 0, keepdims=True)
        return hbm(lax.dynamic_update_index_in_dim(lax.empty((N_DEV,) + own.shape[1:], a.dtype), own, me, 0))

    lands = [landing(a, md) for a, md in zip(arrays, modes)]

    def body(*refs):
        for cp in _split_copies(refs[:n], refs[n:2 * n], refs[2 * n], refs[2 * n + 1], modes):
            cp.start()
        token = refs[-1]
        token[...] = jnp.zeros_like(token)

    sems = pltpu.SemaphoreType.DMA((n * (N_DEV - 1),))
    outs = pl.pallas_call(
        body, name=name, in_specs=[_HBM_SPEC] * (2 * n),
        out_shape=(sems, sems, *[pltpu.HBM(a.shape, a.dtype) for a in srcs + lands], SDS((8, 128), F32)),
        out_specs=(_SEM_SPEC, _SEM_SPEC, *[_HBM_SPEC] * (2 * n), pl.BlockSpec(memory_space=pltpu.VMEM)),
        input_output_aliases={i: 2 + i for i in range(2 * n)},
        compiler_params=pltpu.CompilerParams(has_side_effects=_SPLIT_EFFECT),
    )(*srcs, *lands)
    return (outs[0], outs[1], outs[2:2 + n], outs[2 + n:2 + 2 * n], modes), outs[-1]


def _exchange_wait(handle, after, *, name):
    send_sems, recv_sems, srcs, lands, modes = handle
    n = len(srcs)

    def body(*refs):
        for cp in _split_copies(refs[:n], refs[n:2 * n], refs[2 * n], refs[2 * n + 1], modes):
            cp.wait_send()
            cp.wait_recv()

    outs = pl.pallas_call(
        body, name=name, in_specs=[_HBM_SPEC] * (2 * n) + [_SEM_SPEC, _SEM_SPEC, pl.BlockSpec(memory_space=pl.ANY)],
        out_shape=tuple(pltpu.HBM(a.shape, a.dtype) for a in list(srcs) + list(lands)), out_specs=tuple([_HBM_SPEC] * (2 * n)),
        input_output_aliases={i: i for i in range(2 * n)},
        compiler_params=pltpu.CompilerParams(has_side_effects=_SPLIT_EFFECT),
    )(*srcs, *lands, send_sems, recv_sems, after)
    return list(outs[n:])


def _exchange(arrays, modes, *, name):
    n = len(arrays)
    out_shape = [SDS((N_DEV,) + a.shape if md == "gather" else a.shape, a.dtype) for a, md in zip(arrays, modes)]

    def body(*refs):
        ins, outs = refs[:n], refs[n:2 * n]
        send_sems, recv_sems, local_sems = refs[2 * n:]
        x, y, c = (lax.axis_index(a) for a in AXES)
        me = 4 * x + 2 * y + c
        copies = []
        for i, md in enumerate(modes):
            src = ins[i] if md == "gather" else ins[i].at[me]
            cp = pltpu.make_async_copy(src, outs[i].at[me], local_sems.at[i])
            cp.start()
            copies.append(cp)
            for k in range(1, N_DEV):
                px, py, pc = _peer(k, x, y, c)
                src = ins[i] if md == "gather" else ins[i].at[4 * px + 2 * py + pc]
                cp = pltpu.make_async_remote_copy(
                    src_ref=src, dst_ref=outs[i].at[me], send_sem=send_sems.at[i, k - 1], recv_sem=recv_sems.at[i, k - 1],
                    device_id=(px, py, pc), device_id_type=pl.DeviceIdType.MESH)
                cp.start()
                copies.append(cp)
        for cp in copies:
            cp.wait()

    anyspec = pl.BlockSpec(memory_space=pl.ANY)
    return pl.pallas_call(
        body, in_specs=[anyspec] * n, out_specs=[anyspec] * n, out_shape=out_shape,
        scratch_shapes=[pltpu.SemaphoreType.DMA((n, N_DEV - 1)), pltpu.SemaphoreType.DMA((n, N_DEV - 1)),
                        pltpu.SemaphoreType.DMA((n,))],
        name=name,
    )(*arrays)


def _local_step(x, mem, g_mix, b_forget, g_xattn, g_mem, g_mlp, g_final, target, get_w_in, get_rest, send):
    B, S, D = x.shape
    M = mem.shape[1]
    T = B * S
    x0 = x.reshape(T, D)
    mem2 = mem.reshape(B * M, D)
    tgt = target.reshape(T, D)
    b_pad = jnp.pad(b_forget, (0, 120)).reshape(1, 128)
    after = lambda a, tok: a if tok is None else a + tok[0, 0]

    h1 = _rms(x0, g_mix, tm=1024, name="f_norm")
    w_in_pad = get_w_in(h1)
    _, z, gate = _rms_matmul(h1, g_mix, w_in_pad[:, :QKV_W], tm=ROWS, tn=QKV_W, out_dtype=BF16,
                             w_f32=w_in_pad[:, QKV_W:], normed=True, name="f_in")
    cc = _gate_fwd(gate, b_pad, B=B, S=S, name="f_gatecum")
    ya, lse = _dil_attn_fwd(z, B=B, S=S, name="f_dil")
    yf, lse_f = _fox_fwd(z, cc, B=B, S=S, name="f_fox")
    ymix = jnp.concatenate([ya, yf], axis=1)
    w = get_rest(ymix)
    x1, h2, q = _res_rms_matmul(ymix, w["w_out"], x0, g_xattn, w["w_xq"], tm=ROWS, name="f_out")
    mn, kv = _rms_matmul(mem2, g_mem, w["w_kv"], tm=B * M, tn=D, out_dtype=BF16, name="f_xkv")
    xo = _xattn_fwd(q, kv, B=B, S=S, M=M, tq=1024, name="f_xattn")
    x2, h3, act = _res_rms_matmul(xo, w["w_xo"], x1, g_mlp, w["w_up"], tm=ROWS, relu=True, name="f_xo")
    dx3, dg_final, loss = _down_loss(act, w["w_down"], x2, g_final, tgt, tm=ROWS, name="f_down")

    du = _matmul_nt(dx3, w["w_down"], mul2a=act, tm=ROWS, tn=D_FF, name="b_dact")
    dw_down = _matmul_tn(act, dx3, square=True, bk=1024, bn=D, tt=ACC_ROWS, out_dtype=BF16, name="b_wdown")
    dw_up = _matmul_tn(h3, du, bk=D, bn=1024, tt=min(T, 2 * ACC_ROWS), out_dtype=BF16, name="b_wup")
    tok = send(dict(w_down=dw_down, w_up=dw_up))
    dx2, dg_mlp, dxo = _matmul_nt_rms(du, w["w_up"], x2, after(g_mlp, tok), dx3, then_w=w["w_xo"], tm=ROWS, tk=D_FF,
                                      name="b_dh3")
    dw_xo = _matmul_tn(xo, dx2, bk=D, bn=D, tt=ACC_ROWS, out_dtype=BF16, name="b_wxo")
    dq, dkv = _xattn_bwd(q, kv, dxo, B=B, S=S, M=M, tq=1024, name="b_xattn")
    dw_xq = _matmul_tn(h2, dq, bk=D, bn=D, tt=min(T, 2 * ACC_ROWS), out_dtype=BF16, name="b_wxq")
    dx1, dg_xattn, dy = _matmul_nt_rms(dq, w["w_xq"], x1, g_xattn, dx2, then_w=w["w_out"], tm=ROWS, tk=D, name="b_dh2")
    dw_kv = _matmul_tn(mn, dkv, bk=D, bn=D, tt=B * M, out_dtype=BF16, name="b_wkv")
    _, dg_mem = _matmul_nt_rms(dkv, w["w_kv"], mem2, g_mem, None, tm=min(ROWS, B * M), tk=2 * D, name="b_dmem")
    dw_out = _matmul_tn(ymix, dx1, bk=D, bn=D, tt=ACC_ROWS, out_dtype=BF16, name="b_wout")
    tok = send(dict(w_xo=dw_xo, w_xq=dw_xq, w_xk=dw_kv[:, :D], w_xv=dw_kv[:, D:], w_out=dw_out))
    dqf, dkf, dvf, dcc = _fox_bwd(z, dy, lse_f, cc, B=B, S=S, name="b_fox")
    dgate, db = _gate_bwd(dcc, gate, after(b_pad, tok), B=B, S=S, name="b_gate")
    dqa, dka, dva = _dil_attn_bwd(z, dy, ya, lse, B=B, S=S, name="b_dil")
    dz = [dqa, dka, dva, dqf, dkf, dvf, dgate]
    dw_in = jnp.concatenate([_matmul_tn_pieces(h1, dz[:3], tt=ACC_ROWS, name="b_win_dil"),
                             _matmul_tn_pieces(h1, dz[3:], tt=ACC_ROWS, name="b_win_fox")], axis=1)
    tok = send(dict(w_in=dw_in))
    gx, dg_mix = _matmul_nt_rms(dz, w_in_pad, x0, after(g_mix, tok), dx1, tm=ROWS, tk=IN_PAD, name="b_dh1")

    small = dict(g_mix=dg_mix, b_forget=db, g_xattn=dg_xattn, g_mem=dg_mem, g_mlp=dg_mlp, g_final=dg_final)
    return gx.reshape(B, S, D), small, loss


SMALL_ROWS = ("g_mix", "b_forget", "g_xattn", "g_mem", "g_mlp", "g_final")
COL_SHARDED = ("w_in", "w_up")


def _pack_rows(rows):
    D = D_MODEL
    rows = [jnp.pad(r.reshape(-1), (0, D - r.size)) for r in rows]
    rows += [jnp.zeros((D,), F32)] * (8 - len(rows))
    return jnp.stack(rows)


def _full(name, g):
    if name in COL_SHARDED:
        return g.transpose(1, 0, 2).reshape(g.shape[1], -1)
    return g.reshape(-1, g.shape[2])


def _blocks(name, g, shard_shape):
    if name in COL_SHARDED:
        n = shard_shape[1]
        return g[:, :n * N_DEV].reshape(g.shape[0], N_DEV, n).transpose(1, 0, 2)
    return g.reshape((N_DEV,) + shard_shape)


def kernel(x, mem, g_mix, w_in, b_forget, w_out, g_xattn, g_mem, w_xq, w_xk, w_xv, w_xo, g_mlp, w_up, w_down, g_final, loss_target, m_g_mix, m_w_in, m_b_forget, m_w_out, m_g_xattn, m_g_mem, m_w_xq, m_w_xk, m_w_xv, m_w_xo, m_g_mlp, m_w_up, m_w_down, m_g_final, v_g_mix, v_w_in, v_b_forget, v_w_out, v_g_xattn, v_g_mem, v_w_xq, v_w_xk, v_w_xv, v_w_xo, v_g_mlp, v_w_up, v_w_down, v_g_final):
    W = dict(w_in=w_in, w_out=w_out, w_xq=w_xq, w_xk=w_xk, w_xv=w_xv, w_xo=w_xo, w_up=w_up, w_down=w_down)
    Mo = dict(w_in=m_w_in, w_out=m_w_out, w_xq=m_w_xq, w_xk=m_w_xk, w_xv=m_w_xv, w_xo=m_w_xo, w_up=m_w_up, w_down=m_w_down)
    Vo = dict(w_in=v_w_in, w_out=v_w_out, w_xq=v_w_xq, w_xk=v_w_xk, w_xv=v_w_xv, w_xo=v_w_xo, w_up=v_w_up, w_down=v_w_down)
    later = [n for n in W if n != "w_in"]

    first_handle, first_token = _exchange_start([w_in.astype(BF16)], ["gather"], name="gather_in_start")
    rest_handle, rest_token = _exchange_start([W[n].astype(BF16) + first_token[0, 0].astype(BF16) for n in later],
                                              ["gather"] * len(later), name="gather_rest_start")

    def get_w_in(after):
        (g,) = _exchange_wait(first_handle, after, name="gather_in_wait")
        return jnp.pad(_full("w_in", g), ((0, 0), (0, IN_PAD - IN_W)))

    def get_rest(after):
        full = {n: _full(n, g) for n, g in zip(later, _exchange_wait(rest_handle, after, name="gather_rest_wait"))}
        full["w_kv"] = jnp.concatenate([full.pop("w_xk"), full.pop("w_xv")], axis=1)
        return full

    sent = []

    def send(grads):
        names = list(grads)
        handle, token = _exchange_start([_blocks(n, grads[n], W[n].shape) for n in names], ["scatter"] * len(names),
                                        name=f"scatter{len(sent)}_start")
        sent.append((names, handle))
        return token

    gx, small, loss = _local_step(x, mem, g_mix + rest_token[0, 0], b_forget, g_xattn, g_mem, g_mlp, g_final, loss_target,
                                  get_w_in, get_rest, send)

    received = {}
    for i, (names, handle) in enumerate(sent):
        received.update(zip(names, _exchange_wait(handle, gx, name=f"scatter{i}_wait")))
    packed = _pack_rows([small[n] for n in SMALL_ROWS] + [loss[0, :1]])
    (packed_all,) = _exchange([packed], ["gather"], name="gather_small")

    rows_per_step = lambda shape: max(t for t in (128, 256, 512) if shape[0] % t == 0 and t * shape[1] <= 512 * 512)
    res = {n: _adamw(received[n], W[n], Mo[n], Vo[n], tr=rows_per_step(W[n].shape), name=f"adamw_{n}") for n in W}
    small_w = dict(g_mix=g_mix, b_forget=b_forget, g_xattn=g_xattn, g_mem=g_mem, g_mlp=g_mlp, g_final=g_final)
    small_m = dict(g_mix=m_g_mix, b_forget=m_b_forget, g_xattn=m_g_xattn, g_mem=m_g_mem, g_mlp=m_g_mlp, g_final=m_g_final)
    small_v = dict(g_mix=v_g_mix, b_forget=v_b_forget, g_xattn=v_g_xattn, g_mem=v_g_mem, g_mlp=v_g_mlp, g_final=v_g_final)
    sres = _adamw(packed_all, _pack_rows([small_w[n] for n in SMALL_ROWS]), _pack_rows([small_m[n] for n in SMALL_ROWS]),
                  _pack_rows([small_v[n] for n in SMALL_ROWS]), tr=8, name="adamw_small")
    for i, n in enumerate(SMALL_ROWS):
        res[n] = [r[i, :small_w[n].size] for r in sres]
    loss_total = sres[0][6, 0]

    order = ["g_mix", "w_in", "b_forget", "w_out", "g_xattn", "g_mem", "w_xq", "w_xk", "w_xv", "w_xo", "g_mlp", "w_up", "w_down", "g_final"]
    return (loss_total, gx, *[res[n][0] for n in order], *[res[n][1] for n in order],
            *[res[n][2] for n in order], *[res[n][3] for n in order])
```

```python
import jax
import jax.numpy as jnp
from jax import lax
from jax.experimental import pallas as pl
from jax.experimental.pallas import tpu as pltpu

F32, BF16 = jnp.float32, jnp.bfloat16
SDS = jax.ShapeDtypeStruct

D_MODEL = 1024
HEAD_DIM = 64
WIDTH = 512
QKV_W = 6 * WIDTH
IN_W = QKV_W + 8
IN_PAD = QKV_W + 128
BLOCK = 128
DIL_CONFIGS = ((128, 1), (512, 4), (2048, 16))
N_XH, XHD = 4, 256
D_FF = 4096
EPS = 1e-6
NEG = -1e30
N_DEV = 8
AXES = ("x", "y", "c")

ADAM_LR, ADAM_B1, ADAM_B2, ADAM_EPS, ADAM_WD, ADAM_STEP = 0.001, 0.9, 0.999, 1e-08, 0.01, 10

VMEM_CAP_V7X = 64 * 1024 * 1024
VMEM_LIMIT = VMEM_CAP_V7X * 7 // 8

ROWS = 512
ACC_ROWS = 2048

NT = (((1,), (1,)), ((), ()))
TN = (((0,), (0,)), ((), ()))


def _cp(**kw):
    return pltpu.CompilerParams(vmem_limit_bytes=VMEM_LIMIT, **kw)


def _dot(a, b, dims=None):
    if dims is None:
        return jnp.dot(a, b, preferred_element_type=F32)
    return lax.dot_general(a, b, dims, preferred_element_type=F32)


def _rstd(xv):
    return lax.rsqrt(jnp.mean(xv * xv, axis=-1, keepdims=True) + EPS)


def _rms_bwd(dh, xv, g):
    r = _rstd(xv)
    xhat = xv * r
    dxhat = dh * g
    dx = r * (dxhat - xhat * jnp.mean(dxhat * xhat, axis=-1, keepdims=True))
    return dx, jnp.sum(dh * xhat, axis=0, keepdims=True)


def _rms_matmul(x, g, w, *, tm, tn, out_dtype, relu=False, w_f32=None, normed=False, name):
    T, D = x.shape
    N = w.shape[1]

    def body(*refs):
        x_ref, g_ref, w_ref = refs[:3]
        h_ref, o_ref, h_s = refs[-3 - (w_f32 is not None)], refs[-2 - (w_f32 is not None)], refs[-1]

        @pl.when(pl.program_id(1) == 0)
        def _():
            xv = x_ref[...]
            h = xv if normed else (xv * _rstd(xv) * g_ref[...]).astype(BF16)
            h_s[...] = h
            h_ref[...] = h
            if w_f32 is not None:
                refs[-2][...] = _dot(h, refs[3][...])

        acc = _dot(h_s[...], w_ref[...])
        if relu:
            acc = jnp.maximum(acc, 0.0)
        o_ref[...] = acc.astype(out_dtype)

    in_specs = [pl.BlockSpec((tm, D), lambda i, j: (i, 0)), pl.BlockSpec((1, D), lambda i, j: (0, 0)),
                pl.BlockSpec((D, tn), lambda i, j: (0, j))]
    out_specs = [pl.BlockSpec((tm, D), lambda i, j: (i, 0)), pl.BlockSpec((tm, tn), lambda i, j: (i, j))]
    out_shape = [SDS((T, D), BF16), SDS((T, N), out_dtype)]
    args = [x, g.reshape(1, D), w]
    if w_f32 is not None:
        n2 = w_f32.shape[1]
        in_specs.append(pl.BlockSpec((D, n2), lambda i, j: (0, 0)))
        out_specs.append(pl.BlockSpec((tm, n2), lambda i, j: (i, 0)))
        out_shape.append(SDS((T, n2), F32))
        args.append(w_f32)
    return pl.pallas_call(
        body, grid=(T // tm, N // tn), in_specs=in_specs, out_specs=out_specs, out_shape=out_shape,
        scratch_shapes=[pltpu.VMEM((tm, D), BF16)], compiler_params=_cp(), name=name,
    )(*args)


def _rms(x, g, *, tm, name):
    T, D = x.shape

    def body(x_ref, g_ref, h_ref):
        xv = x_ref[...]
        h_ref[...] = (xv * _rstd(xv) * g_ref[...]).astype(BF16)

    rows = pl.BlockSpec((tm, D), lambda i: (i, 0))
    return pl.pallas_call(body, grid=(T // tm,), in_specs=[rows, pl.BlockSpec((1, D), lambda i: (0, 0))], out_specs=rows,
                          out_shape=SDS((T, D), BF16), compiler_params=_cp(), name=name)(x, g.reshape(1, D))


def _res_rms_matmul(a, w1, res, gain, w2, *, tm, relu=False, name):
    T, K = a.shape
    D, N = w2.shape

    def body(a_ref, w1_ref, res_ref, g_ref, w2_ref, x_ref, h_ref, o_ref):
        xv = res_ref[...] + _dot(a_ref[...], w1_ref[...])
        x_ref[...] = xv
        h = (xv * _rstd(xv) * g_ref[...]).astype(BF16)
        h_ref[...] = h
        acc = _dot(h, w2_ref[...])
        if relu:
            acc = jnp.maximum(acc, 0.0)
        o_ref[...] = acc.astype(BF16)

    rows = lambda n: pl.BlockSpec((tm, n), lambda i: (i, 0))
    whole = lambda r, c: pl.BlockSpec((r, c), lambda i: (0, 0))
    return pl.pallas_call(
        body, grid=(T // tm,), in_specs=[rows(K), whole(K, D), rows(D), whole(1, D), whole(D, N)],
        out_specs=[rows(D), rows(D), rows(N)], out_shape=[SDS((T, D), F32), SDS((T, D), BF16), SDS((T, N), BF16)],
        compiler_params=_cp(), name=name,
    )(a, w1, res, gain.reshape(1, D), w2)


def _matmul_nt(g, w, *, mul2a=None, tm, tn, name):
    T, K = g.shape
    N = w.shape[0]

    def body(*refs):
        g_ref, w_ref = refs[0], refs[1]
        o_ref = refs[-1]
        acc = _dot(g_ref[...].astype(BF16), w_ref[...], NT)
        if mul2a is not None:
            acc = acc * (2.0 * refs[2][...].astype(F32))
        o_ref[...] = acc.astype(BF16)

    in_specs = [pl.BlockSpec((tm, K), lambda i, j: (i, 0)), pl.BlockSpec((tn, K), lambda i, j: (j, 0))]
    args = [g, w]
    if mul2a is not None:
        in_specs.append(pl.BlockSpec((tm, tn), lambda i, j: (i, j)))
        args.append(mul2a)
    return pl.pallas_call(
        body, grid=(T // tm, N // tn), in_specs=in_specs,
        out_specs=pl.BlockSpec((tm, tn), lambda i, j: (i, j)), out_shape=SDS((T, N), BF16),
        compiler_params=_cp(), name=name,
    )(*args)


def _matmul_nt_rms(g, w, x, gain, dres, *, then_w=None, tm, tk, name):
    pieces = list(g) if isinstance(g, (list, tuple)) else [g]
    widths = [p.shape[1] for p in pieces]
    T, K = pieces[0].shape[0], sum(widths)
    D = w.shape[0]
    nk = K // tk
    nt = T // tm
    npc = len(pieces)
    assert npc == 1 or nk == 1
    n_in = npc + 3 + (dres is not None) + (then_w is not None)

    def body(*refs):
        w_ref, x_ref, gain_ref = refs[npc:npc + 3]
        dres_ref = refs[npc + 3] if dres is not None else None
        then_ref = refs[n_in - 1] if then_w is not None else None
        dx_ref, dg_ref = refs[n_in], refs[n_in + 1]
        i, k = pl.program_id(0), pl.program_id(1)
        if npc == 1:
            part = _dot(refs[0][...].astype(BF16), w_ref[...], NT)
        else:
            part, off = None, 0
            for j in range(npc):
                d = _dot(refs[j][...].astype(BF16), w_ref[:, off:off + widths[j]], NT)
                part = d if part is None else part + d
                off += widths[j]

        def finish(dh):
            dx, dg = _rms_bwd(dh, x_ref[...], gain_ref[...])
            if dres_ref is not None:
                dx = dres_ref[...] + dx
            dx_ref[...] = dx
            if then_ref is not None:
                refs[n_in + 2][...] = _dot(dx.astype(BF16), then_ref[...], NT).astype(BF16)

            @pl.when(i == 0)
            def _():
                dg_ref[...] = dg

            @pl.when(i > 0)
            def _():
                dg_ref[...] += dg

        if nk == 1:
            finish(part)
        else:
            acc = refs[-1]

            @pl.when(k == 0)
            def _():
                acc[...] = part

            @pl.when(k > 0)
            def _():
                acc[...] += part

            @pl.when(k == nk - 1)
            def _():
                finish(acc[...])

    g_specs = ([pl.BlockSpec((tm, tk), lambda i, k: (i, k))] if npc == 1 else
               [pl.BlockSpec((tm, wd), lambda i, k: (i, 0)) for wd in widths])
    in_specs = g_specs + [pl.BlockSpec((D, tk), lambda i, k: (0, k)),
                          pl.BlockSpec((tm, D), lambda i, k: (i, 0)), pl.BlockSpec((1, D), lambda i, k: (0, 0))]
    args = pieces + [w, x, gain.reshape(1, D)]
    out_specs = [pl.BlockSpec((tm, D), lambda i, k: (i, 0)), pl.BlockSpec((1, D), lambda i, k: (0, 0))]
    out_shape = [SDS((T, D), F32), SDS((1, D), F32)]
    if dres is not None:
        in_specs.append(pl.BlockSpec((tm, D), lambda i, k: (i, 0)))
        args.append(dres)
    if then_w is not None:
        n2 = then_w.shape[0]
        in_specs.append(pl.BlockSpec((n2, D), lambda i, k: (0, 0)))
        args.append(then_w)
        out_specs.append(pl.BlockSpec((tm, n2), lambda i, k: (i, 0)))
        out_shape.append(SDS((T, n2), BF16))
    return pl.pallas_call(
        body, grid=(nt, nk), in_specs=in_specs, out_specs=out_specs, out_shape=out_shape,
        scratch_shapes=[pltpu.VMEM((tm, D), F32)] if nk > 1 else [], compiler_params=_cp(), name=name,
    )(*args)


def _matmul_tn(a, g, *, square=False, bk, bn, tt, out_dtype, name):
    T, K = a.shape
    N = g.shape[1]
    nt = T // tt

    def body(a_ref, g_ref, o_ref, acc):
        t = pl.program_id(2)
        av = a_ref[...]
        if square:
            af = av.astype(F32)
            av = (af * af).astype(BF16)
        part = _dot(av, g_ref[...].astype(BF16), TN)
        if nt == 1:
            o_ref[...] = part.astype(out_dtype)
        else:
            @pl.when(t == 0)
            def _():
                acc[...] = part

            @pl.when((t > 0) & (t < nt - 1))
            def _():
                acc[...] += part

            @pl.when(t == nt - 1)
            def _():
                o_ref[...] = (acc[...] + part).astype(out_dtype)

    return pl.pallas_call(
        body, grid=(K // bk, N // bn, nt),
        in_specs=[pl.BlockSpec((tt, bk), lambda i, j, t: (t, i)), pl.BlockSpec((tt, bn), lambda i, j, t: (t, j))],
        out_specs=pl.BlockSpec((bk, bn), lambda i, j, t: (i, j)), out_shape=SDS((K, N), out_dtype),
        scratch_shapes=[pltpu.VMEM((bk, bn), F32)], compiler_params=_cp(), name=name,
    )(a, g)


def _matmul_tn_pieces(a, pieces, *, tt, name):
    T, K = a.shape
    widths = [p.shape[1] for p in pieces]
    W = sum(widths)
    nt = T // tt
    n = len(pieces)

    def body(*refs):
        a_ref, o_ref, acc = refs[0], refs[n + 1], refs[n + 2]
        t = pl.program_id(0)
        av = a_ref[...]
        off = 0
        for j in range(n):
            cols = slice(off, off + widths[j])
            part = _dot(av, refs[1 + j][...], TN)

            @pl.when(t == 0)
            def _():
                acc[:, cols] = part

            @pl.when(t > 0)
            def _():
                acc[:, cols] += part

            off += widths[j]

        @pl.when(t == nt - 1)
        def _():
            o_ref[...] = acc[...].astype(BF16)

    return pl.pallas_call(
        body, grid=(nt,),
        in_specs=[pl.BlockSpec((tt, K), lambda t: (t, 0))] + [pl.BlockSpec((tt, w), lambda t: (t, 0)) for w in widths],
        out_specs=pl.BlockSpec((K, W), lambda t: (0, 0)), out_shape=SDS((K, W), BF16),
        scratch_shapes=[pltpu.VMEM((K, W), F32)], compiler_params=_cp(), name=name,
    )(a, *pieces)


def _down_loss(act, w_down, x2, g_final, target, *, tm, name):
    T, D = x2.shape
    F = act.shape[1]

    def body(a_ref, w_ref, x2_ref, g_ref, t_ref, dx_ref, dg_ref, loss_ref):
        i = pl.program_id(0)
        af = a_ref[...].astype(F32)
        xv, g = x2_ref[...] + _dot((af * af).astype(BF16), w_ref[...]), g_ref[...]
        r = _rstd(xv)
        xhat = xv * r
        diff = xhat * g - t_ref[...]
        part = 0.5 * jnp.sum(jnp.mean(diff * diff, axis=-1, keepdims=True), axis=0, keepdims=True)
        dy = diff * (1.0 / D)
        dxhat = dy * g
        dx_ref[...] = r * (dxhat - xhat * jnp.mean(dxhat * xhat, axis=-1, keepdims=True))
        dg = jnp.sum(dy * xhat, axis=0, keepdims=True)
        lp = jnp.broadcast_to(part, loss_ref.shape)

        @pl.when(i == 0)
        def _():
            dg_ref[...] = dg
            loss_ref[...] = lp

        @pl.when(i > 0)
        def _():
            dg_ref[...] += dg
            loss_ref[...] += lp

    rows = pl.BlockSpec((tm, D), lambda i: (i, 0))
    return pl.pallas_call(
        body, grid=(T // tm,),
        in_specs=[pl.BlockSpec((tm, F), lambda i: (i, 0)), pl.BlockSpec((F, D), lambda i: (0, 0)), rows,
                  pl.BlockSpec((1, D), lambda i: (0, 0)), rows],
        out_specs=[rows, pl.BlockSpec((1, D), lambda i: (0, 0)), pl.BlockSpec((8, 128), lambda i: (0, 0))],
        out_shape=[SDS((T, D), F32), SDS((1, D), F32), SDS((8, 128), F32)],
        compiler_params=_cp(), name=name,
    )(act, w_down, x2, g_final.reshape(1, D), target)


def _head_lanes(shape, width):
    return lax.broadcasted_iota(jnp.int32, shape, len(shape) - 1) // width


def _gate_fwd(gate, b_pad, *, B, S, name):
    def body(g_ref, b_ref, cc_ref):
        xv = g_ref[...] + b_ref[...]
        lf = jnp.minimum(xv, 0.0) - jnp.log(1.0 + jnp.exp(-jnp.abs(xv)))
        lane = lax.broadcasted_iota(jnp.int32, lf.shape, 1)
        row = lax.broadcasted_iota(jnp.int32, lf.shape, 0)
        c = jnp.where(lane < 8, lf, 0.0)
        sh = 1
        while sh < S:
            c = c + jnp.where(row >= sh, pltpu.roll(c, sh, 0), 0.0)
            sh *= 2
        grp = _head_lanes((S, WIDTH), HEAD_DIM)
        cc = jnp.zeros((S, WIDTH), F32)
        for h in range(8):
            cc = jnp.where(grp == h, c[:, h:h + 1], cc)
        cc_ref[...] = cc

    return pl.pallas_call(
        body, grid=(B,),
        in_specs=[pl.BlockSpec((S, 128), lambda b: (b, 0)), pl.BlockSpec((1, 128), lambda b: (0, 0))],
        out_specs=pl.BlockSpec((S, WIDTH), lambda b: (b, 0)), out_shape=SDS((B * S, WIDTH), F32),
        compiler_params=_cp(), name=name,
    )(gate, b_pad)


def _gate_bwd(dcc, gate, b_pad, *, B, S, name):
    def body(dcc_ref, g_ref, b_ref, dg_ref, db_ref):
        bi = pl.program_id(0)
        dccv = dcc_ref[...]
        lane = lax.broadcasted_iota(jnp.int32, (S, 128), 1)
        row = lax.broadcasted_iota(jnp.int32, (S, 128), 0)
        dc = jnp.zeros((S, 128), F32)
        for h in range(8):
            dc = jnp.where(lane == h, dccv[:, HEAD_DIM * h:HEAD_DIM * h + 1], dc)
        sh = 1
        while sh < S:
            dc = dc + jnp.where(row < S - sh, pltpu.roll(dc, S - sh, 0), 0.0)
            sh *= 2
        xv = g_ref[...] + b_ref[...]
        dgate = jnp.where(lane < 8, dc / (1.0 + jnp.exp(xv)), 0.0)
        dg_ref[...] = dgate.astype(BF16)
        db = jnp.sum(dgate, axis=0, keepdims=True)

        @pl.when(bi == 0)
        def _():
            db_ref[...] = db

        @pl.when(bi > 0)
        def _():
            db_ref[...] += db

    return pl.pallas_call(
        body, grid=(B,),
        in_specs=[pl.BlockSpec((S, WIDTH), lambda b: (b, 0)), pl.BlockSpec((S, 128), lambda b: (b, 0)),
                  pl.BlockSpec((1, 128), lambda b: (0, 0))],
        out_specs=[pl.BlockSpec((S, 128), lambda b: (b, 0)), pl.BlockSpec((1, 128), lambda b: (0, 0))],
        out_shape=[SDS((B * S, 128), BF16), SDS((1, 128), F32)],
        compiler_params=_cp(), name=name,
    )(dcc, gate, b_pad)


_SMEM_SPEC = pl.BlockSpec(memory_space=pltpu.SMEM)


def _alibi_slopes():
    return 2.0 ** (-(jnp.arange(1, 9, dtype=F32) * (8.0 / 8)))


def _pair_masks():
    lane = lax.broadcasted_iota(jnp.int32, (1, 128), 1)
    first = lane < HEAD_DIM
    return (first.astype(BF16), (~first).astype(BF16)), first


BNT =(((2,), (2,)), ((0,), (0,)))
BNN = (((2,), (1,)), ((0,), (0,)))
BTN = (((1,), (1,)), ((0,), (0,)))


def _split3(v):
    hi = v.astype(BF16).astype(F32)
    mid = (v - hi).astype(BF16).astype(F32)
    lo = (v - hi - mid).astype(BF16).astype(F32)
    return [hi, mid, lo]


def _with_spare_lanes(base, e, cols):
    lane = lax.broadcasted_iota(jnp.int32, (1, 128), 1)
    off = HEAD_DIM * (1 - e)
    extra = jnp.zeros(base.shape, F32)
    for j, c in enumerate(cols):
        extra = jnp.where(lane == off + j, c, extra)
    return base + extra.astype(BF16)


ONES3 = [1.0, 1.0, 1.0]


def _band_bias(slope, dilation):
    qi = lax.broadcasted_iota(jnp.int32, (BLOCK, BLOCK), 0)
    kj = lax.broadcasted_iota(jnp.int32, (BLOCK, BLOCK), 1)
    cur = jnp.where(kj <= qi, (-slope * dilation) * (qi - kj).astype(F32), NEG)
    prev = jnp.where(kj >= qi, (-slope * dilation) * (qi + BLOCK - kj).astype(F32), NEG)
    return cur, prev


def _to_residue_major(dst, src_f32, dilation, nb, lead=0):
    L = nb * BLOCK
    for r in range(dilation):
        rows = src_f32[pl.ds(r, L, stride=dilation), :] if dilation > 1 else src_f32[...]
        dst[lead + r * nb:lead + (r + 1) * nb] = rows.reshape(nb, BLOCK, 128).astype(dst.dtype)


def _dil_attn_fwd(z, *, B, S, name):
    NB = S // BLOCK

    def body(slope_ref, q_ref, k_ref, v_ref, y_ref, lse_ref, qf, kf, vf, qd, kd, vd, od, ld, acc_o, acc_l):
        (m_first, m_second), first = _pair_masks()
        p = pl.program_id(1)
        qf[...] = q_ref[...].astype(F32)
        kf[...] = k_ref[...].astype(F32)
        vf[...] = v_ref[...].astype(F32)
        kd[0] = jnp.zeros((BLOCK, 128), BF16)
        vd[0] = jnp.zeros((BLOCK, 128), BF16)
        blk = lax.broadcasted_iota(jnp.int32, (NB, 1, 1), 0)

        for idx, (_, dilation) in enumerate(DIL_CONFIGS):
            nb = NB // dilation
            _to_residue_major(qd, qf, dilation, nb)
            _to_residue_major(kd, kf, dilation, nb, lead=1)
            _to_residue_major(vd, vf, dilation, nb, lead=1)
            q4, kc, vc = qd[...], kd[1:NB + 1], vd[1:NB + 1]
            outs, lses = [], []
            for e, hm in enumerate((m_first, m_second)):
                bias_cur, bias_prev = _band_bias(slope_ref[2 * p + e], dilation)
                qm = q4 * hm
                sc = _dot(qm, kc, BNT) * 0.125 + bias_cur
                m = jnp.max(sc, axis=2, keepdims=True)
                if nb > 1:
                    sp = _dot(qm, kd[0:NB], BNT) * 0.125 + jnp.where(blk % nb == 0, NEG, bias_prev)
                    m = jnp.maximum(m, jnp.max(sp, axis=2, keepdims=True))
                pc = jnp.exp(sc - m)
                l = jnp.sum(pc, axis=2, keepdims=True)
                o = _dot(pc.astype(BF16), vc, BNN)
                if nb > 1:
                    pp = jnp.exp(sp - m)
                    l = l + jnp.sum(pp, axis=2, keepdims=True)
                    o = o + _dot(pp.astype(BF16), vd[0:NB], BNN)
                outs.append(o * (1.0 / l))
                lses.append(m + jnp.log(l))
            od[...] = jnp.where(first, outs[0], outs[1])
            ld[...] = jnp.where(first, lses[0], lses[1])

            L = nb * BLOCK
            for r in range(dilation):
                rows = pl.ds(r, L, stride=dilation) if dilation > 1 else slice(None)
                o_new = od[r * nb:(r + 1) * nb].reshape(L, 128)
                l_new = ld[r * nb:(r + 1) * nb].reshape(L, 128)
                if idx == 0:
                    acc_o[rows, :] = o_new
                    acc_l[rows, :] = l_new
                else:
                    l_old = acc_l[rows, :]
                    m2 = jnp.maximum(l_old, l_new)
                    w_old, w_new = jnp.exp(l_old - m2), jnp.exp(l_new - m2)
                    tot = w_old + w_new
                    acc_o[rows, :] = (w_old * acc_o[rows, :] + w_new * o_new) * (1.0 / tot)
                    acc_l[rows, :] = m2 + jnp.log(tot)

        y_ref[...] = acc_o[...].astype(BF16)
        lse_ref[...] = acc_l[...]

    spec = lambda off: pl.BlockSpec((S, 128), lambda b, p: (b, 4 * off + p))
    ospec = pl.BlockSpec((S, 128), lambda b, p: (b, p))
    blocks = lambda n, dt: pltpu.VMEM((n, BLOCK, 128), dt)
    return pl.pallas_call(
        body, grid=(B, 4), in_specs=[_SMEM_SPEC, spec(0), spec(1), spec(2)], out_specs=[ospec, ospec],
        out_shape=[SDS((B * S, WIDTH), BF16), SDS((B * S, WIDTH), F32)],
        scratch_shapes=[pltpu.VMEM((S, 128), F32)] * 3 + [blocks(NB, BF16), blocks(NB + 1, BF16), blocks(NB + 1, BF16),
                                                         blocks(NB, F32), blocks(NB, F32)] + [pltpu.VMEM((S, 128), F32)] * 2,
        compiler_params=_cp(), name=name,
    )(_alibi_slopes(), z, z, z)


def _dil_attn_bwd(z, dy, ya, lse, *, B, S, name):
    NB = S // BLOCK

    def body(slope_ref, q_ref, k_ref, v_ref, do_ref, o_ref, lse_ref, dq_ref, dk_ref, dv_ref,
             qf, kf, vf, dof, ef, qd, dod, kd, vd, lsd, dkd, dvd, dqa, dka, dva):
        (m_first, m_second), first = _pair_masks()
        p = pl.program_id(1)
        qf[...] = q_ref[...].astype(F32)
        kf[...] = k_ref[...].astype(F32)
        vf[...] = v_ref[...].astype(F32)
        dov = do_ref[...].astype(F32)
        dof[...] = dov
        prod = dov * o_ref[...].astype(F32)
        rowdot = jnp.where(first, jnp.sum(jnp.where(first, prod, 0.0), axis=1, keepdims=True),
                           jnp.sum(jnp.where(first, 0.0, prod), axis=1, keepdims=True))
        lane = lax.broadcasted_iota(jnp.int32, (1, 128), 1)
        ef[...] = jnp.where(lane % HEAD_DIM < HEAD_DIM // 2, lse_ref[...], rowdot)
        kd[0] = jnp.zeros((BLOCK, 128), BF16)
        vd[0] = jnp.zeros((BLOCK, 128), BF16)
        blk = lax.broadcasted_iota(jnp.int32, (NB, 1, 1), 0)

        for idx, (_, dilation) in enumerate(DIL_CONFIGS):
            nb = NB // dilation
            _to_residue_major(qd, qf, dilation, nb)
            _to_residue_major(dod, dof, dilation, nb)
            _to_residue_major(kd, kf, dilation, nb, lead=1)
            _to_residue_major(vd, vf, dilation, nb, lead=1)
            _to_residue_major(lsd, ef, dilation, nb)
            stats = lsd[...]
            q4, do4, kc, vc = qd[...], dod[...], kd[1:NB + 1], vd[1:NB + 1]
            dq4 = None
            dkc = dvc = dkp = dvp = None
            for e, hm in enumerate((m_first, m_second)):
                lane0 = slice(HEAD_DIM * e, HEAD_DIM * e + 1)
                bias_cur, bias_prev = _band_bias(slope_ref[2 * p + e], dilation)
                qm, dom = q4 * hm, do4 * hm
                lse_e = stats[:, :, lane0]
                e_e = stats[:, :, HEAD_DIM * e + HEAD_DIM // 2:HEAD_DIM * e + HEAD_DIM // 2 + 1]
                pc = jnp.exp(_dot(qm, kc, BNT) * 0.125 + bias_cur - lse_e)
                dsc = (pc * (_dot(dom, vc, BNT) - e_e)).astype(BF16)
                pcb = pc.astype(BF16)
                dqe = _dot(dsc, kc, BNN)
                dkc = _dot(dsc, qm, BTN) if e == 0 else dkc + _dot(dsc, qm, BTN)
                dvc = _dot(pcb, dom, BTN) if e == 0 else dvc + _dot(pcb, dom, BTN)
                if nb > 1:
                    kp, vp = kd[0:NB], vd[0:NB]
                    pp = jnp.exp(_dot(qm, kp, BNT) * 0.125 + jnp.where(blk % nb == 0, NEG, bias_prev) - lse_e)
                    dsp = (pp * (_dot(dom, vp, BNT) - e_e)).astype(BF16)
                    ppb = pp.astype(BF16)
                    dqe = dqe + _dot(dsp, kp, BNN)
                    dkp = _dot(dsp, qm, BTN) if e == 0 else dkp + _dot(dsp, qm, BTN)
                    dvp = _dot(ppb, dom, BTN) if e == 0 else dvp + _dot(ppb, dom, BTN)
                dq4 = dqe if e == 0 else jnp.where(first, dq4, dqe)

            dkd[1:NB + 1] = dkc
            dvd[1:NB + 1] = dvc
            if nb > 1:
                dkd[1:NB] += dkp[1:NB]
                dvd[1:NB] += dvp[1:NB]
            L = nb * BLOCK
            for r in range(dilation):
                rows = pl.ds(r, L, stride=dilation) if dilation > 1 else slice(None)
                dq_r = dq4[r * nb:(r + 1) * nb].reshape(L, 128) * 0.125
                dk_r = dkd[1 + r * nb:1 + (r + 1) * nb].reshape(L, 128) * 0.125
                dv_r = dvd[1 + r * nb:1 + (r + 1) * nb].reshape(L, 128)
                if idx == 0:
                    dqa[rows, :], dka[rows, :], dva[rows, :] = dq_r, dk_r, dv_r
                else:
                    dqa[rows, :] += dq_r
                    dka[rows, :] += dk_r
                    dva[rows, :] += dv_r

        dq_ref[...] = dqa[...].astype(BF16)
        dk_ref[...] = dka[...].astype(BF16)
        dv_ref[...] = dva[...].astype(BF16)

    spec = lambda off: pl.BlockSpec((S, 128), lambda b, p: (b, 4 * off + p))
    ospec = pl.BlockSpec((S, 128), lambda b, p: (b, p))
    blocks = lambda n, dt: pltpu.VMEM((n, BLOCK, 128), dt)
    return pl.pallas_call(
        body, grid=(B, 4), in_specs=[_SMEM_SPEC, spec(0), spec(1), spec(2), ospec, ospec, ospec],
        out_specs=[ospec] * 3, out_shape=[SDS((B * S, WIDTH), BF16)] * 3,
        scratch_shapes=[pltpu.VMEM((S, 128), F32)] * 5
        + [blocks(NB, BF16), blocks(NB, BF16), blocks(NB + 1, BF16), blocks(NB + 1, BF16), blocks(NB, F32),
           blocks(NB + 1, F32), blocks(NB + 1, F32)] + [pltpu.VMEM((S, 128), F32)] * 3,
        compiler_params=_cp(), name=name,
    )(_alibi_slopes(), z, z, z, dy, ya, lse)


FOX_TQ = 256
FOX_TQ_FWD = 512


def _fox_fwd(z, cc, *, B, S, name):
    def body(q_ref, k_ref, v_ref, cc_ref, o_ref, l_ref, qa, ka):
        (m_first, m_second), first = _pair_masks()
        ccv = cc_ref[...]
        eighth = jnp.asarray(0.125, BF16)
        for e, hm in enumerate((m_first, m_second)):
            c_e = jnp.broadcast_to(ccv[:, HEAD_DIM * e:HEAD_DIM * e + 1], (S, 128))
            qa[e] = _with_spare_lanes(q_ref[...] * hm * eighth, e, _split3(c_e) + ONES3)
            ka[e] = _with_spare_lanes(k_ref[...] * hm, e, ONES3 + _split3(-c_e))
        for qi in range(S // FOX_TQ_FWD):
            r0, kend = qi * FOX_TQ_FWD, (qi + 1) * FOX_TQ_FWD
            vv = v_ref[0:kend, :]
            row = lax.broadcasted_iota(jnp.int32, (FOX_TQ_FWD, kend), 0) + r0
            col = lax.broadcasted_iota(jnp.int32, (FOX_TQ_FWD, kend), 1)
            causal = col <= row
            outs, lses = [], []
            for e in (0, 1):
                s = jnp.where(causal, _dot(qa[e, r0:kend, :], ka[e, 0:kend, :], NT), NEG)
                m = jnp.max(s, axis=1, keepdims=True)
                pe = jnp.exp(s - m)
                l = jnp.sum(pe, axis=1, keepdims=True)
                outs.append(_dot(pe.astype(BF16), vv) * (1.0 / l))
                lses.append(m + jnp.log(l))
            o_ref[r0:kend, :] = jnp.where(first, outs[0], outs[1]).astype(BF16)
            l_ref[r0:kend, :] = jnp.where(first, lses[0], lses[1])

    spec = lambda off: pl.BlockSpec((S, 128), lambda b, p: (b, 4 * off + p))
    pspec = pl.BlockSpec((S, 128), lambda b, p: (b, p))
    return pl.pallas_call(
        body, grid=(B, 4), in_specs=[spec(3), spec(4), spec(5), pspec], out_specs=[pspec, pspec],
        out_shape=[SDS((B * S, WIDTH), BF16), SDS((B * S, WIDTH), F32)],
        scratch_shapes=[pltpu.VMEM((2, S, 128), BF16)] * 2, compiler_params=_cp(), name=name,
    )(z, z, z, cc)


def _fox_bwd(z, dy, lse, cc, *, B, S, name):
    def body(q_ref, k_ref, v_ref, do_ref, lse_ref, cc_ref, dq_ref, dk_ref, dv_ref, dc_ref,
             qa, ka, qp, kp, vp, dp, dk_s, dv_s, dc_s):
        (m_first, m_second), first = _pair_masks()
        ccv, lsev = cc_ref[...], lse_ref[...]
        eighth = jnp.asarray(0.125, BF16)
        for e, hm in enumerate((m_first, m_second)):
            lane0 = slice(HEAD_DIM * e, HEAD_DIM * e + 1)
            c_e = jnp.broadcast_to(ccv[:, lane0], (S, 128))
            lse_e = jnp.broadcast_to(lsev[:, lane0], (S, 128))
            qp[e] = q_ref[...] * hm
            kp[e] = k_ref[...] * hm
            dp[e] = do_ref[...] * hm
            qa[e] = _with_spare_lanes(qp[e] * eighth, e, _split3(c_e - lse_e) + ONES3)
            ka[e] = _with_spare_lanes(kp[e], e, ONES3 + _split3(-c_e))
            vp[e] = v_ref[...] * hm
        dk_s[...] = jnp.zeros_like(dk_s)
        dv_s[...] = jnp.zeros_like(dv_s)
        dc_s[...] = jnp.zeros_like(dc_s)
        for qi in range(S // FOX_TQ):
            r0, kend = qi * FOX_TQ, (qi + 1) * FOX_TQ
            krow = lax.broadcasted_iota(jnp.int32, (kend, FOX_TQ), 0)
            qcol = lax.broadcasted_iota(jnp.int32, (kend, FOX_TQ), 1) + r0
            causal = krow <= qcol
            dq_t = jnp.zeros((FOX_TQ, 128), F32)
            for e in (0, 1):
                sel = first if e == 0 else ~first
                pt = jnp.where(causal, jnp.exp(_dot(ka[e, 0:kend, :], qa[e, r0:kend, :], NT)), 0.0)
                dpt = _dot(vp[e, 0:kend, :], dp[e, r0:kend, :], NT)
                mean = jnp.sum(pt * dpt, axis=0, keepdims=True) / jnp.sum(pt, axis=0, keepdims=True)
                dst = pt * (dpt - mean)
                dsb = dst.astype(BF16)
                dv_s[0:kend, :] += _dot(pt.astype(BF16), dp[e, r0:kend, :])
                dk_s[0:kend, :] += _dot(dsb, qp[e, r0:kend, :]) * 0.125
                dq_t = dq_t + _dot(dsb, kp[e, 0:kend, :], TN)
                dc_s[0:kend, :] += jnp.where(sel, -jnp.sum(dst, axis=1, keepdims=True), 0.0)
            dq_ref[r0:kend, :] = (dq_t * 0.125).astype(BF16)
        dk_ref[...] = dk_s[...].astype(BF16)
        dv_ref[...] = dv_s[...].astype(BF16)
        dc_ref[...] = dc_s[...]

    spec = lambda off: pl.BlockSpec((S, 128), lambda b, p: (b, 4 * off + p))
    pspec = pl.BlockSpec((S, 128), lambda b, p: (b, p))
    return pl.pallas_call(
        body, grid=(B, 4),
        in_specs=[spec(3), spec(4), spec(5), pl.BlockSpec((S, 128), lambda b, p: (b, 4 + p)), pspec, pspec],
        out_specs=[pspec] * 4,
        out_shape=[SDS((B * S, WIDTH), BF16)] * 3 + [SDS((B * S, WIDTH), F32)],
        scratch_shapes=[pltpu.VMEM((2, S, 128), BF16)] * 6 + [pltpu.VMEM((S, 128), F32)] * 3,
        compiler_params=_cp(), name=name,
    )(z, z, z, dy, lse, cc)


def _xattn_fwd(q, kv, *, B, S, M, tq, name):
    D = D_MODEL

    def body(q_ref, kv_ref, o_ref):
        for h in range(N_XH):
            cs = slice(XHD * h, XHD * (h + 1))
            s = _dot(q_ref[:, cs], kv_ref[:, cs], NT) * (1.0 / 16.0)
            pe = jnp.exp(s - jnp.max(s, axis=1, keepdims=True))
            l = jnp.sum(pe, axis=1, keepdims=True)
            o_ref[:, cs] = (_dot(pe.astype(BF16), kv_ref[:, D + XHD * h:D + XHD * (h + 1)]) * (1.0 / l)).astype(BF16)

    nq = S // tq
    return pl.pallas_call(
        body, grid=(B, nq),
        in_specs=[pl.BlockSpec((tq, D), lambda b, t: (b * nq + t, 0)), pl.BlockSpec((M, 2 * D), lambda b, t: (b, 0))],
        out_specs=pl.BlockSpec((tq, D), lambda b, t: (b * nq + t, 0)), out_shape=SDS((B * S, D), BF16),
        compiler_params=_cp(), name=name,
    )(q, kv)


def _xattn_bwd(q, kv, do, *, B, S, M, tq, name):
    D = D_MODEL

    def body(q_ref, kv_ref, do_ref, dq_ref, dkv_ref):
        t = pl.program_id(1)

        @pl.when(t == 0)
        def _():
            dkv_ref[...] = jnp.zeros_like(dkv_ref)

        for h in range(N_XH):
            cs = slice(XHD * h, XHD * (h + 1))
            vs = slice(D + XHD * h, D + XHD * (h + 1))
            qh, kh, vh, doh = q_ref[:, cs], kv_ref[:, cs], kv_ref[:, vs], do_ref[:, cs]
            s = _dot(qh, kh, NT) * (1.0 / 16.0)
            pe = jnp.exp(s - jnp.max(s, axis=1, keepdims=True))
            pe = pe * (1.0 / jnp.sum(pe, axis=1, keepdims=True))
            dp = _dot(doh, vh, NT)
            ds = (pe * (dp - jnp.sum(pe * dp, axis=1, keepdims=True))).astype(BF16)
            dq_ref[:, cs] = (_dot(ds, kh) * (1.0 / 16.0)).astype(BF16)
            dkv_ref[:, cs] += _dot(ds, qh, TN) * (1.0 / 16.0)
            dkv_ref[:, vs] += _dot(pe.astype(BF16), doh, TN)

    nq = S // tq
    qspec = pl.BlockSpec((tq, D), lambda b, t: (b * nq + t, 0))
    kvspec = pl.BlockSpec((M, 2 * D), lambda b, t: (b, 0))
    return pl.pallas_call(
        body, grid=(B, nq), in_specs=[qspec, kvspec, qspec], out_specs=[qspec, kvspec],
        out_shape=[SDS((B * S, D), BF16), SDS((B * M, 2 * D), F32)], compiler_params=_cp(), name=name,
    )(q, kv, do)


def _adamw(parts, w, m, v, *, tr, name):
    R, C = w.shape

    def body(p_ref, w_ref, m_ref, v_ref, g_ref, d_ref, nm_ref, nv_ref):
        g = p_ref[0].astype(F32)
        for d in range(1, N_DEV):
            g = g + p_ref[d].astype(F32)
        m2 = ADAM_B1 * m_ref[...] + (1.0 - ADAM_B1) * g
        v2 = ADAM_B2 * v_ref[...] + (1.0 - ADAM_B2) * (g * g)
        m_hat = m2 / (1.0 - ADAM_B1 ** ADAM_STEP)
        v_hat = v2 / (1.0 - ADAM_B2 ** ADAM_STEP)
        g_ref[...] = g
        d_ref[...] = -ADAM_LR * (m_hat / (jnp.sqrt(v_hat) + ADAM_EPS) + ADAM_WD * w_ref[...])
        nm_ref[...] = m2
        nv_ref[...] = v2

    spec = pl.BlockSpec((tr, C), lambda i: (i, 0))
    return pl.pallas_call(
        body, grid=(R // tr,), in_specs=[pl.BlockSpec((N_DEV, tr, C), lambda i: (0, i, 0)), spec, spec, spec],
        out_specs=[spec] * 4, out_shape=[SDS((R, C), F32)] * 4, compiler_params=_cp(), name=name,
    )(parts, w, m, v)


def _peer(k, x, y, c):
    return (1 - x if k & 4 else x, 1 - y if k & 2 else y, 1 - c if k & 1 else c)


_HBM_SPEC = pl.BlockSpec(memory_space=pltpu.HBM)
_SEM_SPEC = pl.BlockSpec(memory_space=pltpu.SEMAPHORE)
_SPLIT_EFFECT = pltpu.SideEffectType.DATAFLOW_SIDE_EFFECTING


def _split_copies(srcs, lands, send_sems, recv_sems, modes):
    x, y, c = (lax.axis_index(a) for a in AXES)
    me = 4 * x + 2 * y + c
    copies = []
    for i, md in enumerate(modes):
        for k in range(1, N_DEV):
            px, py, pc = _peer(k, x, y, c)
            src = srcs[i] if md == "gather" else srcs[i].at[4 * px + 2 * py + pc]
            j = i * (N_DEV - 1) + k - 1
            copies.append(pltpu.make_async_remote_copy(
                src_ref=src, dst_ref=lands[i].at[me], send_sem=send_sems.at[j], recv_sem=recv_sems.at[j],
                device_id=(px, py, pc), device_id_type=pl.DeviceIdType.MESH))
    return copies


def _exchange_start(arrays, modes, *, name):
    n = len(arrays)
    hbm = lambda a: pltpu.with_memory_space_constraint(a, pltpu.HBM)
    srcs = [hbm(a) for a in arrays]
    me = 4 * lax.axis_index("x") + 2 * lax.axis_index("y") + lax.axis_index("c")

    def landing(a, md):
        own = a[None] if md == "gather" else lax.dynamic_index_in_dim(a, me, 0, keepdims=True)
        return hbm(lax.dynamic_update_index_in_dim(lax.empty((N_DEV,) + own.shape[1:], a.dtype), own, me, 0))

    lands = [landing(a, md) for a, md in zip(arrays, modes)]

    def body(*refs):
        for cp in _split_copies(refs[:n], refs[n:2 * n], refs[2 * n], refs[2 * n + 1], modes):
            cp.start()
        token = refs[-1]
        token[...] = jnp.zeros_like(token)

    sems = pltpu.SemaphoreType.DMA((n * (N_DEV - 1),))
    outs = pl.pallas_call(
        body, name=name, in_specs=[_HBM_SPEC] * (2 * n),
        out_shape=(sems, sems, *[pltpu.HBM(a.shape, a.dtype) for a in srcs + lands], SDS((8, 128), F32)),
        out_specs=(_SEM_SPEC, _SEM_SPEC, *[_HBM_SPEC] * (2 * n), pl.BlockSpec(memory_space=pltpu.VMEM)),
        input_output_aliases={i: 2 + i for i in range(2 * n)},
        compiler_params=pltpu.CompilerParams(has_side_effects=_SPLIT_EFFECT),
    )(*srcs, *lands)
    return (outs[0], outs[1], outs[2:2 + n], outs[2 + n:2 + 2 * n], modes), outs[-1]


def _exchange_wait(handle, after, *, name):
    send_sems, recv_sems, srcs, lands, modes = handle
    n = len(srcs)

    def body(*refs):
        for cp in _split_copies(refs[:n], refs[n:2 * n], refs[2 * n], refs[2 * n + 1], modes):
            cp.wait_send()
            cp.wait_recv()

    outs = pl.pallas_call(
        body, name=name, in_specs=[_HBM_SPEC] * (2 * n) + [_SEM_SPEC, _SEM_SPEC, pl.BlockSpec(memory_space=pl.ANY)],
        out_shape=tuple(pltpu.HBM(a.shape, a.dtype) for a in list(srcs) + list(lands)), out_specs=tuple([_HBM_SPEC] * (2 * n)),
        input_output_aliases={i: i for i in range(2 * n)},
        compiler_params=pltpu.CompilerParams(has_side_effects=_SPLIT_EFFECT),
    )(*srcs, *lands, send_sems, recv_sems, after)
    return list(outs[n:])


def _exchange(arrays, modes, *, name):
    n = len(arrays)
    out_shape = [SDS((N_DEV,) + a.shape if md == "gather" else a.shape, a.dtype) for a, md in zip(arrays, modes)]

    def body(*refs):
        ins, outs = refs[:n], refs[n:2 * n]
        send_sems, recv_sems, local_sems = refs[2 * n:]
        x, y, c = (lax.axis_index(a) for a in AXES)
        me = 4 * x + 2 * y + c
        copies = []
        for i, md in enumerate(modes):
            src = ins[i] if md == "gather" else ins[i].at[me]
            cp = pltpu.make_async_copy(src, outs[i].at[me], local_sems.at[i])
            cp.start()
            copies.append(cp)
            for k in range(1, N_DEV):
                px, py, pc = _peer(k, x, y, c)
                src = ins[i] if md == "gather" else ins[i].at[4 * px + 2 * py + pc]
                cp = pltpu.make_async_remote_copy(
                    src_ref=src, dst_ref=outs[i].at[me], send_sem=send_sems.at[i, k - 1], recv_sem=recv_sems.at[i, k - 1],
                    device_id=(px, py, pc), device_id_type=pl.DeviceIdType.MESH)
                cp.start()
                copies.append(cp)
        for cp in copies:
            cp.wait()

    anyspec = pl.BlockSpec(memory_space=pl.ANY)
    return pl.pallas_call(
        body, in_specs=[anyspec] * n, out_specs=[anyspec] * n, out_shape=out_shape,
        scratch_shapes=[pltpu.SemaphoreType.DMA((n, N_DEV - 1)), pltpu.SemaphoreType.DMA((n, N_DEV - 1)),
                        pltpu.SemaphoreType.DMA((n,))],
        name=name,
    )(*arrays)


def _local_step(x, mem, g_mix, b_forget, g_xattn, g_mem, g_mlp, g_final, target, get_w_in, get_rest, send):
    B, S, D = x.shape
    M = mem.shape[1]
    T = B * S
    x0 = x.reshape(T, D)
    mem2 = mem.reshape(B * M, D)
    tgt = target.reshape(T, D)
    b_pad = jnp.pad(b_forget, (0, 120)).reshape(1, 128)
    after = lambda a, tok: a if tok is None else a + tok[0, 0]

    h1 = _rms(x0, g_mix, tm=1024, name="f_norm")
    w_in_pad = get_w_in(h1)
    _, z, gate = _rms_matmul(h1, g_mix, w_in_pad[:, :QKV_W], tm=ROWS, tn=QKV_W, out_dtype=BF16,
                             w_f32=w_in_pad[:, QKV_W:], normed=True, name="f_in")
    cc = _gate_fwd(gate, b_pad, B=B, S=S, name="f_gatecum")
    ya, lse = _dil_attn_fwd(z, B=B, S=S, name="f_dil")
    yf, lse_f = _fox_fwd(z, cc, B=B, S=S, name="f_fox")
    ymix = jnp.concatenate([ya, yf], axis=1)
    w = get_rest(ymix)
    x1, h2, q = _res_rms_matmul(ymix, w["w_out"], x0, g_xattn, w["w_xq"], tm=ROWS, name="f_out")
    mn, kv = _rms_matmul(mem2, g_mem, w["w_kv"], tm=B * M, tn=D, out_dtype=BF16, name="f_xkv")
    xo = _xattn_fwd(q, kv, B=B, S=S, M=M, tq=1024, name="f_xattn")
    x2, h3, act = _res_rms_matmul(xo, w["w_xo"], x1, g_mlp, w["w_up"], tm=ROWS, relu=True, name="f_xo")
    dx3, dg_final, loss = _down_loss(act, w["w_down"], x2, g_final, tgt, tm=ROWS, name="f_down")

    du = _matmul_nt(dx3, w["w_down"], mul2a=act, tm=ROWS, tn=D_FF, name="b_dact")
    dw_down = _matmul_tn(act, dx3, square=True, bk=1024, bn=D, tt=ACC_ROWS, out_dtype=BF16, name="b_wdown")
    dw_up = _matmul_tn(h3, du, bk=D, bn=1024, tt=min(T, 2 * ACC_ROWS), out_dtype=BF16, name="b_wup")
    tok = send(dict(w_down=dw_down, w_up=dw_up))
    dx2, dg_mlp, dxo = _matmul_nt_rms(du, w["w_up"], x2, after(g_mlp, tok), dx3, then_w=w["w_xo"], tm=ROWS, tk=D_FF,
                                      name="b_dh3")
    dw_xo = _matmul_tn(xo, dx2, bk=D, bn=D, tt=ACC_ROWS, out_dtype=BF16, name="b_wxo")
    dq, dkv = _xattn_bwd(q, kv, dxo, B=B, S=S, M=M, tq=1024, name="b_xattn")
    dw_xq = _matmul_tn(h2, dq, bk=D, bn=D, tt=min(T, 2 * ACC_ROWS), out_dtype=BF16, name="b_wxq")
    dx1, dg_xattn, dy = _matmul_nt_rms(dq, w["w_xq"], x1, g_xattn, dx2, then_w=w["w_out"], tm=ROWS, tk=D, name="b_dh2")
    dw_kv = _matmul_tn(mn, dkv, bk=D, bn=D, tt=B * M, out_dtype=BF16, name="b_wkv")
    _, dg_mem = _matmul_nt_rms(dkv, w["w_kv"], mem2, g_mem, None, tm=min(ROWS, B * M), tk=2 * D, name="b_dmem")
    dw_out = _matmul_tn(ymix, dx1, bk=D, bn=D, tt=ACC_ROWS, out_dtype=BF16, name="b_wout")
    tok = send(dict(w_xo=dw_xo, w_xq=dw_xq, w_xk=dw_kv[:, :D], w_xv=dw_kv[:, D:], w_out=dw_out))
    dqf, dkf, dvf, dcc = _fox_bwd(z, dy, lse_f, cc, B=B, S=S, name="b_fox")
    dgate, db = _gate_bwd(dcc, gate, after(b_pad, tok), B=B, S=S, name="b_gate")
    dqa, dka, dva = _dil_attn_bwd(z, dy, ya, lse, B=B, S=S, name="b_dil")
    dz = [dqa, dka, dva, dqf, dkf, dvf, dgate]
    dw_in = jnp.concatenate([_matmul_tn_pieces(h1, dz[:3], tt=ACC_ROWS, name="b_win_dil"),
                             _matmul_tn_pieces(h1, dz[3:], tt=ACC_ROWS, name="b_win_fox")], axis=1)
    tok = send(dict(w_in=dw_in))
    gx, dg_mix = _matmul_nt_rms(dz, w_in_pad, x0, after(g_mix, tok), dx1, tm=ROWS, tk=IN_PAD, name="b_dh1")

    small = dict(g_mix=dg_mix, b_forget=db, g_xattn=dg_xattn, g_mem=dg_mem, g_mlp=dg_mlp, g_final=dg_final)
    return gx.reshape(B, S, D), small, loss


SMALL_ROWS = ("g_mix", "b_forget", "g_xattn", "g_mem", "g_mlp", "g_final")
COL_SHARDED = ("w_in", "w_up")


def _pack_rows(rows):
    D = D_MODEL
    rows = [jnp.pad(r.reshape(-1), (0, D - r.size)) for r in rows]
    rows += [jnp.zeros((D,), F32)] * (8 - len(rows))
    return jnp.stack(rows)


def _full(name, g):
    if name in COL_SHARDED:
        return g.transpose(1, 0, 2).reshape(g.shape[1], -1)
    return g.reshape(-1, g.shape[2])


def _blocks(name, g, shard_shape):
    if name in COL_SHARDED:
        n = shard_shape[1]
        return g[:, :n * N_DEV].reshape(g.shape[0], N_DEV, n).transpose(1, 0, 2)
    return g.reshape((N_DEV,) + shard_shape)


def kernel(x, mem, g_mix, w_in, b_forget, w_out, g_xattn, g_mem, w_xq, w_xk, w_xv, w_xo, g_mlp, w_up, w_down, g_final, loss_target, m_g_mix, m_w_in, m_b_forget, m_w_out, m_g_xattn, m_g_mem, m_w_xq, m_w_xk, m_w_xv, m_w_xo, m_g_mlp, m_w_up, m_w_down, m_g_final, v_g_mix, v_w_in, v_b_forget, v_w_out, v_g_xattn, v_g_mem, v_w_xq, v_w_xk, v_w_xv, v_w_xo, v_g_mlp, v_w_up, v_w_down, v_g_final):
    W = dict(w_in=w_in, w_out=w_out, w_xq=w_xq, w_xk=w_xk, w_xv=w_xv, w_xo=w_xo, w_up=w_up, w_down=w_down)
    Mo = dict(w_in=m_w_in, w_out=m_w_out, w_xq=m_w_xq, w_xk=m_w_xk, w_xv=m_w_xv, w_xo=m_w_xo, w_up=m_w_up, w_down=m_w_down)
    Vo = dict(w_in=v_w_in, w_out=v_w_out, w_xq=v_w_xq, w_xk=v_w_xk, w_xv=v_w_xv, w_xo=v_w_xo, w_up=v_w_up, w_down=v_w_down)
    later = [n for n in W if n != "w_in"]

    first_handle, first_token = _exchange_start([w_in.astype(BF16)], ["gather"], name="gather_in_start")
    rest_handle, rest_token = _exchange_start([W[n].astype(BF16) + first_token[0, 0].astype(BF16) for n in later],
                                              ["gather"] * len(later), name="gather_rest_start")

    def get_w_in(after):
        (g,) = _exchange_wait(first_handle, after, name="gather_in_wait")
        return jnp.pad(_full("w_in", g), ((0, 0), (0, IN_PAD - IN_W)))

    def get_rest(after):
        full = {n: _full(n, g) for n, g in zip(later, _exchange_wait(rest_handle, after, name="gather_rest_wait"))}
        full["w_kv"] = jnp.concatenate([full.pop("w_xk"), full.pop("w_xv")], axis=1)
        return full

    sent = []

    def send(grads):
        names = list(grads)
        handle, token = _exchange_start([_blocks(n, grads[n], W[n].shape) for n in names], ["scatter"] * len(names),
                                        name=f"scatter{len(sent)}_start")
        sent.append((names, handle))
        return token

    gx, small, loss = _local_step(x, mem, g_mix + rest_token[0, 0], b_forget, g_xattn, g_mem, g_mlp, g_final, loss_target,
                                  get_w_in, get_rest, send)

    received = {}
    for i, (names, handle) in enumerate(sent):
        received.update(zip(names, _exchange_wait(handle, gx, name=f"scatter{i}_wait")))
    packed = _pack_rows([small[n] for n in SMALL_ROWS] + [loss[0, :1]])
    (packed_all,) = _exchange([packed], ["gather"], name="gather_small")

    rows_per_step = lambda shape: max(t for t in (128, 256, 512) if shape[0] % t == 0 and t * shape[1] <= 512 * 512)
    res = {n: _adamw(received[n], W[n], Mo[n], Vo[n], tr=rows_per_step(W[n].shape), name=f"adamw_{n}") for n in W}
    small_w = dict(g_mix=g_mix, b_forget=b_forget, g_xattn=g_xattn, g_mem=g_mem, g_mlp=g_mlp, g_final=g_final)
    small_m = dict(g_mix=m_g_mix, b_forget=m_b_forget, g_xattn=m_g_xattn, g_mem=m_g_mem, g_mlp=m_g_mlp, g_final=m_g_final)
    small_v = dict(g_mix=v_g_mix, b_forget=v_b_forget, g_xattn=v_g_xattn, g_mem=v_g_mem, g_mlp=v_g_mlp, g_final=v_g_final)
    sres = _adamw(packed_all, _pack_rows([small_w[n] for n in SMALL_ROWS]), _pack_rows([small_m[n] for n in SMALL_ROWS]),
                  _pack_rows([small_v[n] for n in SMALL_ROWS]), tr=8, name="adamw_small")
    for i, n in enumerate(SMALL_ROWS):
        res[n] = [r[i, :small_w[n].size] for r in sres]
    loss_total = sres[0][6, 0]

    order = ["g_mix", "w_in", "b_forget", "w_out", "g_xattn", "g_mem", "w_xq", "w_xk", "w_xv", "w_xo", "g_mlp", "w_up", "w_down", "g_final"]
    return (loss_total, gx, *[res[n][0] for n in order], *[res[n][1] for n in order],
            *[res[n][2] for n in order], *[res[n][3] for n in order])
```

```python
import jax
import jax.numpy as jnp
from jax import lax
from jax.experimental import pallas as pl
from jax.experimental.pallas import tpu as pltpu

F32, BF16 = jnp.float32, jnp.bfloat16
SDS = jax.ShapeDtypeStruct

D_MODEL = 1024
HEAD_DIM = 64
WIDTH = 512
QKV_W = 6 * WIDTH
IN_W = QKV_W + 8
IN_PAD = QKV_W + 128
BLOCK = 128
DIL_CONFIGS = ((128, 1), (512, 4), (2048, 16))
N_XH, XHD = 4, 256
D_FF = 4096
EPS = 1e-6
NEG = -1e30
N_DEV = 8
AXES = ("x", "y", "c")

ADAM_LR, ADAM_B1, ADAM_B2, ADAM_EPS, ADAM_WD, ADAM_STEP = 0.001, 0.9, 0.999, 1e-08, 0.01, 10

VMEM_CAP_V7X = 64 * 1024 * 1024
VMEM_LIMIT = VMEM_CAP_V7X * 7 // 8

ROWS = 512
ACC_ROWS = 2048

NT = (((1,), (1,)), ((), ()))
TN = (((0,), (0,)), ((), ()))


def _cp(**kw):
    return pltpu.CompilerParams(vmem_limit_bytes=VMEM_LIMIT, **kw)


def _dot(a, b, dims=None):
    if dims is None:
        return jnp.dot(a, b, preferred_element_type=F32)
    return lax.dot_general(a, b, dims, preferred_element_type=F32)


def _rstd(xv):
    return lax.rsqrt(jnp.mean(xv * xv, axis=-1, keepdims=True) + EPS)


def _rms_bwd(dh, xv, g):
    r = _rstd(xv)
    xhat = xv * r
    dxhat = dh * g
    dx = r * (dxhat - xhat * jnp.mean(dxhat * xhat, axis=-1, keepdims=True))
    return dx, jnp.sum(dh * xhat, axis=0, keepdims=True)


def _rms_matmul(x, g, w, *, tm, tn, out_dtype, relu=False, w_f32=None, normed=False, name):
    T, D = x.shape
    N = w.shape[1]

    def body(*refs):
        x_ref, g_ref, w_ref = refs[:3]
        outs, h_s = refs[3 + (w_f32 is not None):-1], refs[-1]
        o_ref = outs[0 if normed else 1]

        @pl.when(pl.program_id(1) == 0)
        def _():
            xv = x_ref[...]
            h = xv if normed else (xv * _rstd(xv) * g_ref[...]).astype(BF16)
            h_s[...] = h
            if not normed:
                outs[0][...] = h
            if w_f32 is not None:
                outs[-1][...] = _dot(h, refs[3][...])

        acc = _dot(h_s[...], w_ref[...])
        if relu:
            acc = jnp.maximum(acc, 0.0)
        o_ref[...] = acc.astype(out_dtype)

    in_specs = [pl.BlockSpec((tm, D), lambda i, j: (i, 0)), pl.BlockSpec((1, D), lambda i, j: (0, 0)),
                pl.BlockSpec((D, tn), lambda i, j: (0, j))]
    out_specs = [pl.BlockSpec((tm, D), lambda i, j: (i, 0)), pl.BlockSpec((tm, tn), lambda i, j: (i, j))]
    out_shape = [SDS((T, D), BF16), SDS((T, N), out_dtype)]
    args = [x, g.reshape(1, D), w]
    if normed:
        out_specs, out_shape = out_specs[1:], out_shape[1:]
    if w_f32 is not None:
        n2 = w_f32.shape[1]
        in_specs.append(pl.BlockSpec((D, n2), lambda i, j: (0, 0)))
        out_specs.append(pl.BlockSpec((tm, n2), lambda i, j: (i, 0)))
        out_shape.append(SDS((T, n2), F32))
        args.append(w_f32)
    res = pl.pallas_call(
        body, grid=(T // tm, N // tn), in_specs=in_specs, out_specs=out_specs, out_shape=out_shape,
        scratch_shapes=[pltpu.VMEM((tm, D), BF16)], compiler_params=_cp(), name=name,
    )(*args)
    return [x] + list(res) if normed else res


def _rms(x, g, *, tm, name):
    T, D = x.shape

    def body(x_ref, g_ref, h_ref):
        xv = x_ref[...]
        h_ref[...] = (xv * _rstd(xv) * g_ref[...]).astype(BF16)

    rows = pl.BlockSpec((tm, D), lambda i: (i, 0))
    return pl.pallas_call(body, grid=(T // tm,), in_specs=[rows, pl.BlockSpec((1, D), lambda i: (0, 0))], out_specs=rows,
                          out_shape=SDS((T, D), BF16), compiler_params=_cp(), name=name)(x, g.reshape(1, D))


def _res_rms_matmul(a, w1, res, gain, w2, *, tm, relu=False, name):
    T, K = a.shape
    D, N = w2.shape

    def body(a_ref, w1_ref, res_ref, g_ref, w2_ref, x_ref, h_ref, o_ref):
        xv = res_ref[...] + _dot(a_ref[...], w1_ref[...])
        x_ref[...] = xv
        h = (xv * _rstd(xv) * g_ref[...]).astype(BF16)
        h_ref[...] = h
        acc = _dot(h, w2_ref[...])
        if relu:
            acc = jnp.maximum(acc, 0.0)
        o_ref[...] = acc.astype(BF16)

    rows = lambda n: pl.BlockSpec((tm, n), lambda i: (i, 0))
    whole = lambda r, c: pl.BlockSpec((r, c), lambda i: (0, 0))
    return pl.pallas_call(
        body, grid=(T // tm,), in_specs=[rows(K), whole(K, D), rows(D), whole(1, D), whole(D, N)],
        out_specs=[rows(D), rows(D), rows(N)], out_shape=[SDS((T, D), F32), SDS((T, D), BF16), SDS((T, N), BF16)],
        compiler_params=_cp(), name=name,
    )(a, w1, res, gain.reshape(1, D), w2)


def _matmul_nt(g, w, *, mul2a=None, tm, tn, name):
    T, K = g.shape
    N = w.shape[0]

    def body(*refs):
        g_ref, w_ref = refs[0], refs[1]
        o_ref = refs[-1]
        acc = _dot(g_ref[...].astype(BF16), w_ref[...], NT)
        if mul2a is not None:
            acc = acc * (2.0 * refs[2][...].astype(F32))
        o_ref[...] = acc.astype(BF16)

    in_specs = [pl.BlockSpec((tm, K), lambda i, j: (i, 0)), pl.BlockSpec((tn, K), lambda i, j: (j, 0))]
    args = [g, w]
    if mul2a is not None:
        in_specs.append(pl.BlockSpec((tm, tn), lambda i, j: (i, j)))
        args.append(mul2a)
    return pl.pallas_call(
        body, grid=(T // tm, N // tn), in_specs=in_specs,
        out_specs=pl.BlockSpec((tm, tn), lambda i, j: (i, j)), out_shape=SDS((T, N), BF16),
        compiler_params=_cp(), name=name,
    )(*args)


def _matmul_nt_rms(g, w, x, gain, dres, *, then_w=None, tm, tk, name):
    pieces = list(g) if isinstance(g, (list, tuple)) else [g]
    widths = [p.shape[1] for p in pieces]
    T, K = pieces[0].shape[0], sum(widths)
    D = w.shape[0]
    nk = K // tk
    nt = T // tm
    npc = len(pieces)
    assert npc == 1 or nk == 1
    n_in = npc + 3 + (dres is not None) + (then_w is not None)

    def body(*refs):
        w_ref, x_ref, gain_ref = refs[npc:npc + 3]
        dres_ref = refs[npc + 3] if dres is not None else None
        then_ref = refs[n_in - 1] if then_w is not None else None
        dx_ref, dg_ref = refs[n_in], refs[n_in + 1]
        i, k = pl.program_id(0), pl.program_id(1)
        if npc == 1:
            part = _dot(refs[0][...].astype(BF16), w_ref[...], NT)
        else:
            part, off = None, 0
            for j in range(npc):
                d = _dot(refs[j][...].astype(BF16), w_ref[:, off:off + widths[j]], NT)
                part = d if part is None else part + d
                off += widths[j]

        def finish(dh):
            dx, dg = _rms_bwd(dh, x_ref[...], gain_ref[...])
            if dres_ref is not None:
                dx = dres_ref[...] + dx
            dx_ref[...] = dx
            if then_ref is not None:
                refs[n_in + 2][...] = _dot(dx.astype(BF16), then_ref[...], NT).astype(BF16)

            @pl.when(i == 0)
            def _():
                dg_ref[...] = dg

            @pl.when(i > 0)
            def _():
                dg_ref[...] += dg

        if nk == 1:
            finish(part)
        else:
            acc = refs[-1]

            @pl.when(k == 0)
            def _():
                acc[...] = part

            @pl.when(k > 0)
            def _():
                acc[...] += part

            @pl.when(k == nk - 1)
            def _():
                finish(acc[...])

    g_specs = ([pl.BlockSpec((tm, tk), lambda i, k: (i, k))] if npc == 1 else
               [pl.BlockSpec((tm, wd), lambda i, k: (i, 0)) for wd in widths])
    in_specs = g_specs + [pl.BlockSpec((D, tk), lambda i, k: (0, k)),
                          pl.BlockSpec((tm, D), lambda i, k: (i, 0)), pl.BlockSpec((1, D), lambda i, k: (0, 0))]
    args = pieces + [w, x, gain.reshape(1, D)]
    out_specs = [pl.BlockSpec((tm, D), lambda i, k: (i, 0)), pl.BlockSpec((1, D), lambda i, k: (0, 0))]
    out_shape = [SDS((T, D), F32), SDS((1, D), F32)]
    if dres is not None:
        in_specs.append(pl.BlockSpec((tm, D), lambda i, k: (i, 0)))
        args.append(dres)
    if then_w is not None:
        n2 = then_w.shape[0]
        in_specs.append(pl.BlockSpec((n2, D), lambda i, k: (0, 0)))
        args.append(then_w)
        out_specs.append(pl.BlockSpec((tm, n2), lambda i, k: (i, 0)))
        out_shape.append(SDS((T, n2), BF16))
    return pl.pallas_call(
        body, grid=(nt, nk), in_specs=in_specs, out_specs=out_specs, out_shape=out_shape,
        scratch_shapes=[pltpu.VMEM((tm, D), F32)] if nk > 1 else [], compiler_params=_cp(), name=name,
    )(*args)


def _matmul_tn(a, g, *, square=False, bk, bn, tt, out_dtype, name):
    T, K = a.shape
    N = g.shape[1]
    nt = T // tt

    def body(a_ref, g_ref, o_ref, acc):
        t = pl.program_id(2)
        av = a_ref[...]
        if square:
            af = av.astype(F32)
            av = (af * af).astype(BF16)
        part = _dot(av, g_ref[...].astype(BF16), TN)
        if nt == 1:
            o_ref[...] = part.astype(out_dtype)
        else:
            @pl.when(t == 0)
            def _():
                acc[...] = part

            @pl.when((t > 0) & (t < nt - 1))
            def _():
                acc[...] += part

            @pl.when(t == nt - 1)
            def _():
                o_ref[...] = (acc[...] + part).astype(out_dtype)

    return pl.pallas_call(
        body, grid=(K // bk, N // bn, nt),
        in_specs=[pl.BlockSpec((tt, bk), lambda i, j, t: (t, i)), pl.BlockSpec((tt, bn), lambda i, j, t: (t, j))],
        out_specs=pl.BlockSpec((bk, bn), lambda i, j, t: (i, j)), out_shape=SDS((K, N), out_dtype),
        scratch_shapes=[pltpu.VMEM((bk, bn), F32)], compiler_params=_cp(), name=name,
    )(a, g)


def _matmul_tn_pieces(a, pieces, *, tt, name):
    T, K = a.shape
    widths = [p.shape[1] for p in pieces]
    W = sum(widths)
    nt = T // tt
    n = len(pieces)

    def body(*refs):
        a_ref, o_ref, acc = refs[0], refs[n + 1], refs[n + 2]
        t = pl.program_id(0)
        av = a_ref[...]
        off = 0
        for j in range(n):
            cols = slice(off, off + widths[j])
            part = _dot(av, refs[1 + j][...], TN)

            @pl.when(t == 0)
            def _():
                acc[:, cols] = part

            @pl.when(t > 0)
            def _():
                acc[:, cols] += part

            off += widths[j]

        @pl.when(t == nt - 1)
        def _():
            o_ref[...] = acc[...].astype(BF16)

    return pl.pallas_call(
        body, grid=(nt,),
        in_specs=[pl.BlockSpec((tt, K), lambda t: (t, 0))] + [pl.BlockSpec((tt, w), lambda t: (t, 0)) for w in widths],
        out_specs=pl.BlockSpec((K, W), lambda t: (0, 0)), out_shape=SDS((K, W), BF16),
        scratch_shapes=[pltpu.VMEM((K, W), F32)], compiler_params=_cp(), name=name,
    )(a, *pieces)


def _down_loss(act, w_down, x2, g_final, target, *, tm, name):
    T, D = x2.shape
    F = act.shape[1]

    def body(a_ref, w_ref, x2_ref, g_ref, t_ref, dx_ref, dg_ref, loss_ref):
        i = pl.program_id(0)
        af = a_ref[...].astype(F32)
        xv, g = x2_ref[...] + _dot((af * af).astype(BF16), w_ref[...]), g_ref[...]
        r = _rstd(xv)
        xhat = xv * r
        diff = xhat * g - t_ref[...]
        part = 0.5 * jnp.sum(jnp.mean(diff * diff, axis=-1, keepdims=True), axis=0, keepdims=True)
        dy = diff * (1.0 / D)
        dxhat = dy * g
        dx_ref[...] = r * (dxhat - xhat * jnp.mean(dxhat * xhat, axis=-1, keepdims=True))
        dg = jnp.sum(dy * xhat, axis=0, keepdims=True)
        lp = jnp.broadcast_to(part, loss_ref.shape)

        @pl.when(i == 0)
        def _():
            dg_ref[...] = dg
            loss_ref[...] = lp

        @pl.when(i > 0)
        def _():
            dg_ref[...] += dg
            loss_ref[...] += lp

    rows = pl.BlockSpec((tm, D), lambda i: (i, 0))
    return pl.pallas_call(
        body, grid=(T // tm,),
        in_specs=[pl.BlockSpec((tm, F), lambda i: (i, 0)), pl.BlockSpec((F, D), lambda i: (0, 0)), rows,
                  pl.BlockSpec((1, D), lambda i: (0, 0)), rows],
        out_specs=[rows, pl.BlockSpec((1, D), lambda i: (0, 0)), pl.BlockSpec((8, 128), lambda i: (0, 0))],
        out_shape=[SDS((T, D), F32), SDS((1, D), F32), SDS((8, 128), F32)],
        compiler_params=_cp(), name=name,
    )(act, w_down, x2, g_final.reshape(1, D), target)


def _head_lanes(shape, width):
    return lax.broadcasted_iota(jnp.int32, shape, len(shape) - 1) // width


def _gate_fwd(gate, b_pad, *, B, S, name):
    def body(g_ref, b_ref, cc_ref):
        xv = g_ref[...] + b_ref[...]
        lf = jnp.minimum(xv, 0.0) - jnp.log(1.0 + jnp.exp(-jnp.abs(xv)))
        lane = lax.broadcasted_iota(jnp.int32, lf.shape, 1)
        row = lax.broadcasted_iota(jnp.int32, lf.shape, 0)
        c = jnp.where(lane < 8, lf, 0.0)
        sh = 1
        while sh < S:
            c = c + jnp.where(row >= sh, pltpu.roll(c, sh, 0), 0.0)
            sh *= 2
        grp = _head_lanes((S, WIDTH), HEAD_DIM)
        cc = jnp.zeros((S, WIDTH), F32)
        for h in range(8):
            cc = jnp.where(grp == h, c[:, h:h + 1], cc)
        cc_ref[...] = cc

    return pl.pallas_call(
        body, grid=(B,),
        in_specs=[pl.BlockSpec((S, 128), lambda b: (b, 0)), pl.BlockSpec((1, 128), lambda b: (0, 0))],
        out_specs=pl.BlockSpec((S, WIDTH), lambda b: (b, 0)), out_shape=SDS((B * S, WIDTH), F32),
        compiler_params=_cp(), name=name,
    )(gate, b_pad)


def _gate_bwd(dcc, gate, b_pad, *, B, S, name):
    def body(dcc_ref, g_ref, b_ref, dg_ref, db_ref):
        bi = pl.program_id(0)
        dccv = dcc_ref[...]
        lane = lax.broadcasted_iota(jnp.int32, (S, 128), 1)
        row = lax.broadcasted_iota(jnp.int32, (S, 128), 0)
        dc = jnp.zeros((S, 128), F32)
        for h in range(8):
            dc = jnp.where(lane == h, dccv[:, HEAD_DIM * h:HEAD_DIM * h + 1], dc)
        sh = 1
        while sh < S:
            dc = dc + jnp.where(row < S - sh, pltpu.roll(dc, S - sh, 0), 0.0)
            sh *= 2
        xv = g_ref[...] + b_ref[...]
        dgate = jnp.where(lane < 8, dc / (1.0 + jnp.exp(xv)), 0.0)
        dg_ref[...] = dgate.astype(BF16)
        db = jnp.sum(dgate, axis=0, keepdims=True)

        @pl.when(bi == 0)
        def _():
            db_ref[...] = db

        @pl.when(bi > 0)
        def _():
            db_ref[...] += db

    return pl.pallas_call(
        body, grid=(B,),
        in_specs=[pl.BlockSpec((S, WIDTH), lambda b: (b, 0)), pl.BlockSpec((S, 128), lambda b: (b, 0)),
                  pl.BlockSpec((1, 128), lambda b: (0, 0))],
        out_specs=[pl.BlockSpec((S, 128), lambda b: (b, 0)), pl.BlockSpec((1, 128), lambda b: (0, 0))],
        out_shape=[SDS((B * S, 128), BF16), SDS((1, 128), F32)],
        compiler_params=_cp(), name=name,
    )(dcc, gate, b_pad)


_SMEM_SPEC = pl.BlockSpec(memory_space=pltpu.SMEM)


def _alibi_slopes():
    return 2.0 ** (-(jnp.arange(1, 9, dtype=F32) * (8.0 / 8)))


def _pair_masks():
    lane = lax.broadcasted_iota(jnp.int32, (1, 128), 1)
    first = lane < HEAD_DIM
    return (first.astype(BF16), (~first).astype(BF16)), first


BNT =(((2,), (2,)), ((0,), (0,)))
BNN = (((2,), (1,)), ((0,), (0,)))
BTN = (((1,), (1,)), ((0,), (0,)))


def _split3(v):
    hi = v.astype(BF16).astype(F32)
    mid = (v - hi).astype(BF16).astype(F32)
    lo = (v - hi - mid).astype(BF16).astype(F32)
    return [hi, mid, lo]


def _with_spare_lanes(base, e, cols):
    lane = lax.broadcasted_iota(jnp.int32, (1, 128), 1)
    off = HEAD_DIM * (1 - e)
    extra = jnp.zeros(base.shape, F32)
    for j, c in enumerate(cols):
        extra = jnp.where(lane == off + j, c, extra)
    return base + extra.astype(BF16)


ONES3 = [1.0, 1.0, 1.0]


def _band_bias(slope, dilation):
    qi = lax.broadcasted_iota(jnp.int32, (BLOCK, BLOCK), 0)
    kj = lax.broadcasted_iota(jnp.int32, (BLOCK, BLOCK), 1)
    cur = jnp.where(kj <= qi, (-slope * dilation) * (qi - kj).astype(F32), NEG)
    prev = jnp.where(kj >= qi, (-slope * dilation) * (qi + BLOCK - kj).astype(F32), NEG)
    return cur, prev


def _to_residue_major(dst, src_f32, dilation, nb, lead=0):
    L = nb * BLOCK
    for r in range(dilation):
        rows = src_f32[pl.ds(r, L, stride=dilation), :] if dilation > 1 else src_f32[...]
        dst[lead + r * nb:lead + (r + 1) * nb] = rows.reshape(nb, BLOCK, 128).astype(dst.dtype)


def _dil_attn_fwd(z, *, B, S, name):
    NB = S // BLOCK

    def body(slope_ref, q_ref, k_ref, v_ref, y_ref, lse_ref, qf, kf, vf, qd, kd, vd, od, ld, acc_o, acc_l):
        (m_first, m_second), first = _pair_masks()
        p = pl.program_id(1)
        qf[...] = q_ref[...].astype(F32)
        kf[...] = k_ref[...].astype(F32)
        vf[...] = v_ref[...].astype(F32)
        kd[0] = jnp.zeros((BLOCK, 128), BF16)
        vd[0] = jnp.zeros((BLOCK, 128), BF16)
        blk = lax.broadcasted_iota(jnp.int32, (NB, 1, 1), 0)

        for idx, (_, dilation) in enumerate(DIL_CONFIGS):
            nb = NB // dilation
            _to_residue_major(qd, qf, dilation, nb)
            _to_residue_major(kd, kf, dilation, nb, lead=1)
            _to_residue_major(vd, vf, dilation, nb, lead=1)
            q4, kc, vc = qd[...], kd[1:NB + 1], vd[1:NB + 1]
            outs, lses = [], []
            for e, hm in enumerate((m_first, m_second)):
                bias_cur, bias_prev = _band_bias(slope_ref[2 * p + e], dilation)
                qm = q4 * hm
                sc = _dot(qm, kc, BNT) * 0.125 + bias_cur
                m = jnp.max(sc, axis=2, keepdims=True)
                if nb > 1:
                    sp = _dot(qm, kd[0:NB], BNT) * 0.125 + jnp.where(blk % nb == 0, NEG, bias_prev)
                    m = jnp.maximum(m, jnp.max(sp, axis=2, keepdims=True))
                pc = jnp.exp(sc - m)
                l = jnp.sum(pc, axis=2, keepdims=True)
                o = _dot(pc.astype(BF16), vc, BNN)
                if nb > 1:
                    pp = jnp.exp(sp - m)
                    l = l + jnp.sum(pp, axis=2, keepdims=True)
                    o = o + _dot(pp.astype(BF16), vd[0:NB], BNN)
                outs.append(o * (1.0 / l))
                lses.append(m + jnp.log(l))
            od[...] = jnp.where(first, outs[0], outs[1])
            ld[...] = jnp.where(first, lses[0], lses[1])

            L = nb * BLOCK
            for r in range(dilation):
                rows = pl.ds(r, L, stride=dilation) if dilation > 1 else slice(None)
                o_new = od[r * nb:(r + 1) * nb].reshape(L, 128)
                l_new = ld[r * nb:(r + 1) * nb].reshape(L, 128)
                if idx == 0:
                    acc_o[rows, :] = o_new
                    acc_l[rows, :] = l_new
                else:
                    l_old = acc_l[rows, :]
                    m2 = jnp.maximum(l_old, l_new)
                    w_old, w_new = jnp.exp(l_old - m2), jnp.exp(l_new - m2)
                    tot = w_old + w_new
                    acc_o[rows, :] = (w_old * acc_o[rows, :] + w_new * o_new) * (1.0 / tot)
                    acc_l[rows, :] = m2 + jnp.log(tot)

        y_ref[...] = acc_o[...].astype(BF16)
        lse_ref[...] = acc_l[...]

    spec = lambda off: pl.BlockSpec((S, 128), lambda b, p: (b, 4 * off + p))
    ospec = pl.BlockSpec((S, 128), lambda b, p: (b, p))
    blocks = lambda n, dt: pltpu.VMEM((n, BLOCK, 128), dt)
    return pl.pallas_call(
        body, grid=(B, 4), in_specs=[_SMEM_SPEC, spec(0), spec(1), spec(2)], out_specs=[ospec, ospec],
        out_shape=[SDS((B * S, WIDTH), BF16), SDS((B * S, WIDTH), F32)],
        scratch_shapes=[pltpu.VMEM((S, 128), F32)] * 3 + [blocks(NB, BF16), blocks(NB + 1, BF16), blocks(NB + 1, BF16),
                                                         blocks(NB, F32), blocks(NB, F32)] + [pltpu.VMEM((S, 128), F32)] * 2,
        compiler_params=_cp(), name=name,
    )(_alibi_slopes(), z, z, z)


def _dil_attn_bwd(z, dy, ya, lse, *, B, S, name):
    NB = S // BLOCK

    def body(slope_ref, q_ref, k_ref, v_ref, do_ref, o_ref, lse_ref, dq_ref, dk_ref, dv_ref,
             qf, kf, vf, dof, ef, qd, dod, kd, vd, lsd, dkd, dvd, dqa, dka, dva):
        (m_first, m_second), first = _pair_masks()
        p = pl.program_id(1)
        qf[...] = q_ref[...].astype(F32)
        kf[...] = k_ref[...].astype(F32)
        vf[...] = v_ref[...].astype(F32)
        dov = do_ref[...].astype(F32)
        dof[...] = dov
        prod = dov * o_ref[...].astype(F32)
        rowdot = jnp.where(first, jnp.sum(jnp.where(first, prod, 0.0), axis=1, keepdims=True),
                           jnp.sum(jnp.where(first, 0.0, prod), axis=1, keepdims=True))
        lane = lax.broadcasted_iota(jnp.int32, (1, 128), 1)
        ef[...] = jnp.where(lane % HEAD_DIM < HEAD_DIM // 2, lse_ref[...], rowdot)
        kd[0] = jnp.zeros((BLOCK, 128), BF16)
        vd[0] = jnp.zeros((BLOCK, 128), BF16)
        blk = lax.broadcasted_iota(jnp.int32, (NB, 1, 1), 0)

        for idx, (_, dilation) in enumerate(DIL_CONFIGS):
            nb = NB // dilation
            _to_residue_major(qd, qf, dilation, nb)
            _to_residue_major(dod, dof, dilation, nb)
            _to_residue_major(kd, kf, dilation, nb, lead=1)
            _to_residue_major(vd, vf, dilation, nb, lead=1)
            _to_residue_major(lsd, ef, dilation, nb)
            stats = lsd[...]
            q4, do4, kc, vc = qd[...], dod[...], kd[1:NB + 1], vd[1:NB + 1]
            dq4 = None
            dkc = dvc = dkp = dvp = None
            for e, hm in enumerate((m_first, m_second)):
                lane0 = slice(HEAD_DIM * e, HEAD_DIM * e + 1)
                bias_cur, bias_prev = _band_bias(slope_ref[2 * p + e], dilation)
                qm, dom = q4 * hm, do4 * hm
                lse_e = stats[:, :, lane0]
                e_e = stats[:, :, HEAD_DIM * e + HEAD_DIM // 2:HEAD_DIM * e + HEAD_DIM // 2 + 1]
                pc = jnp.exp(_dot(qm, kc, BNT) * 0.125 + bias_cur - lse_e)
                dsc = (pc * (_dot(dom, vc, BNT) - e_e)).astype(BF16)
                pcb = pc.astype(BF16)
                dqe = _dot(dsc, kc, BNN)
                dkc = _dot(dsc, qm, BTN) if e == 0 else dkc + _dot(dsc, qm, BTN)
                dvc = _dot(pcb, dom, BTN) if e == 0 else dvc + _dot(pcb, dom, BTN)
                if nb > 1:
                    kp, vp = kd[0:NB], vd[0:NB]
                    pp = jnp.exp(_dot(qm, kp, BNT) * 0.125 + jnp.where(blk % nb == 0, NEG, bias_prev) - lse_e)
                    dsp = (pp * (_dot(dom, vp, BNT) - e_e)).astype(BF16)
                    ppb = pp.astype(BF16)
                    dqe = dqe + _dot(dsp, kp, BNN)
                    dkp = _dot(dsp, qm, BTN) if e == 0 else dkp + _dot(dsp, qm, BTN)
                    dvp = _dot(ppb, dom, BTN) if e == 0 else dvp + _dot(ppb, dom, BTN)
                dq4 = dqe if e == 0 else jnp.where(first, dq4, dqe)

            dkd[1:NB + 1] = dkc
            dvd[1:NB + 1] = dvc
            if nb > 1:
                dkd[1:NB] += dkp[1:NB]
                dvd[1:NB] += dvp[1:NB]
            L = nb * BLOCK
            for r in range(dilation):
                rows = pl.ds(r, L, stride=dilation) if dilation > 1 else slice(None)
                dq_r = dq4[r * nb:(r + 1) * nb].reshape(L, 128) * 0.125
                dk_r = dkd[1 + r * nb:1 + (r + 1) * nb].reshape(L, 128) * 0.125
                dv_r = dvd[1 + r * nb:1 + (r + 1) * nb].reshape(L, 128)
                if idx == 0:
                    dqa[rows, :], dka[rows, :], dva[rows, :] = dq_r, dk_r, dv_r
                else:
                    dqa[rows, :] += dq_r
                    dka[rows, :] += dk_r
                    dva[rows, :] += dv_r

        dq_ref[...] = dqa[...].astype(BF16)
        dk_ref[...] = dka[...].astype(BF16)
        dv_ref[...] = dva[...].astype(BF16)

    spec = lambda off: pl.BlockSpec((S, 128), lambda b, p: (b, 4 * off + p))
    ospec = pl.BlockSpec((S, 128), lambda b, p: (b, p))
    blocks = lambda n, dt: pltpu.VMEM((n, BLOCK, 128), dt)
    return pl.pallas_call(
        body, grid=(B, 4), in_specs=[_SMEM_SPEC, spec(0), spec(1), spec(2), ospec, ospec, ospec],
        out_specs=[ospec] * 3, out_shape=[SDS((B * S, WIDTH), BF16)] * 3,
        scratch_shapes=[pltpu.VMEM((S, 128), F32)] * 5
        + [blocks(NB, BF16), blocks(NB, BF16), blocks(NB + 1, BF16), blocks(NB + 1, BF16), blocks(NB, F32),
           blocks(NB + 1, F32), blocks(NB + 1, F32)] + [pltpu.VMEM((S, 128), F32)] * 3,
        compiler_params=_cp(), name=name,
    )(_alibi_slopes(), z, z, z, dy, ya, lse)


FOX_TQ = 256
FOX_TQ_FWD = 512


def _fox_fwd(z, cc, *, B, S, name):
    def body(q_ref, k_ref, v_ref, cc_ref, o_ref, l_ref, qa, ka):
        (m_first, m_second), first = _pair_masks()
        ccv = cc_ref[...]
        eighth = jnp.asarray(0.125, BF16)
        for e, hm in enumerate((m_first, m_second)):
            c_e = jnp.broadcast_to(ccv[:, HEAD_DIM * e:HEAD_DIM * e + 1], (S, 128))
            qa[e] = _with_spare_lanes(q_ref[...] * hm * eighth, e, _split3(c_e) + ONES3)
            ka[e] = _with_spare_lanes(k_ref[...] * hm, e, ONES3 + _split3(-c_e))
        for qi in range(S // FOX_TQ_FWD):
            r0, kend = qi * FOX_TQ_FWD, (qi + 1) * FOX_TQ_FWD
            vv = v_ref[0:kend, :]
            row = lax.broadcasted_iota(jnp.int32, (FOX_TQ_FWD, kend), 0) + r0
            col = lax.broadcasted_iota(jnp.int32, (FOX_TQ_FWD, kend), 1)
            causal = col <= row
            outs, lses = [], []
            for e in (0, 1):
                s = jnp.where(causal, _dot(qa[e, r0:kend, :], ka[e, 0:kend, :], NT), NEG)
                m = jnp.max(s, axis=1, keepdims=True)
                pe = jnp.exp(s - m)
                l = jnp.sum(pe, axis=1, keepdims=True)
                outs.append(_dot(pe.astype(BF16), vv) * (1.0 / l))
                lses.append(m + jnp.log(l))
            o_ref[r0:kend, :] = jnp.where(first, outs[0], outs[1]).astype(BF16)
            l_ref[r0:kend, :] = jnp.where(first, lses[0], lses[1])

    spec = lambda off: pl.BlockSpec((S, 128), lambda b, p: (b, 4 * off + p))
    pspec = pl.BlockSpec((S, 128), lambda b, p: (b, p))
    return pl.pallas_call(
        body, grid=(B, 4), in_specs=[spec(3), spec(4), spec(5), pspec], out_specs=[pspec, pspec],
        out_shape=[SDS((B * S, WIDTH), BF16), SDS((B * S, WIDTH), F32)],
        scratch_shapes=[pltpu.VMEM((2, S, 128), BF16)] * 2, compiler_params=_cp(), name=name,
    )(z, z, z, cc)


def _fox_bwd(z, dy, lse, cc, *, B, S, name):
    def body(q_ref, k_ref, v_ref, do_ref, lse_ref, cc_ref, dq_ref, dk_ref, dv_ref, dc_ref,
             qa, ka, qp, kp, vp, dp, dk_s, dv_s, dc_s):
        (m_first, m_second), first = _pair_masks()
        ccv, lsev = cc_ref[...], lse_ref[...]
        eighth = jnp.asarray(0.125, BF16)
        for e, hm in enumerate((m_first, m_second)):
            lane0 = slice(HEAD_DIM * e, HEAD_DIM * e + 1)
            c_e = jnp.broadcast_to(ccv[:, lane0], (S, 128))
            lse_e = jnp.broadcast_to(lsev[:, lane0], (S, 128))
            qp[e] = q_ref[...] * hm
            kp[e] = k_ref[...] * hm
            dp[e] = do_ref[...] * hm
            qa[e] = _with_spare_lanes(qp[e] * eighth, e, _split3(c_e - lse_e) + ONES3)
            ka[e] = _with_spare_lanes(kp[e], e, ONES3 + _split3(-c_e))
            vp[e] = v_ref[...] * hm
        dk_s[...] = jnp.zeros_like(dk_s)
        dv_s[...] = jnp.zeros_like(dv_s)
        dc_s[...] = jnp.zeros_like(dc_s)
        for qi in range(S // FOX_TQ):
            r0, kend = qi * FOX_TQ, (qi + 1) * FOX_TQ
            krow = lax.broadcasted_iota(jnp.int32, (kend, FOX_TQ), 0)
            qcol = lax.broadcasted_iota(jnp.int32, (kend, FOX_TQ), 1) + r0
            causal = krow <= qcol
            dq_t = jnp.zeros((FOX_TQ, 128), F32)
            for e in (0, 1):
                sel = first if e == 0 else ~first
                pt = jnp.where(causal, jnp.exp(_dot(ka[e, 0:kend, :], qa[e, r0:kend, :], NT)), 0.0)
                dpt = _dot(vp[e, 0:kend, :], dp[e, r0:kend, :], NT)
                mean = jnp.sum(pt * dpt, axis=0, keepdims=True) / jnp.sum(pt, axis=0, keepdims=True)
                dst = pt * (dpt - mean)
                dsb = dst.astype(BF16)
                dv_s[0:kend, :] += _dot(pt.astype(BF16), dp[e, r0:kend, :])
                dk_s[0:kend, :] += _dot(dsb, qp[e, r0:kend, :]) * 0.125
                dq_t = dq_t + _dot(dsb, kp[e, 0:kend, :], TN)
                dc_s[0:kend, :] += jnp.where(sel, -jnp.sum(dst, axis=1, keepdims=True), 0.0)
            dq_ref[r0:kend, :] = (dq_t * 0.125).astype(BF16)
        dk_ref[...] = dk_s[...].astype(BF16)
        dv_ref[...] = dv_s[...].astype(BF16)
        dc_ref[...] = dc_s[...]

    spec = lambda off: pl.BlockSpec((S, 128), lambda b, p: (b, 4 * off + p))
    pspec = pl.BlockSpec((S, 128), lambda b, p: (b, p))
    return pl.pallas_call(
        body, grid=(B, 4),
        in_specs=[spec(3), spec(4), spec(5), pl.BlockSpec((S, 128), lambda b, p: (b, 4 + p)), pspec, pspec],
        out_specs=[pspec] * 4,
        out_shape=[SDS((B * S, WIDTH), BF16)] * 3 + [SDS((B * S, WIDTH), F32)],
        scratch_shapes=[pltpu.VMEM((2, S, 128), BF16)] * 6 + [pltpu.VMEM((S, 128), F32)] * 3,
        compiler_params=_cp(), name=name,
    )(z, z, z, dy, lse, cc)


def _xattn_fwd(q, kv, *, B, S, M, tq, name):
    D = D_MODEL

    def body(q_ref, kv_ref, o_ref):
        for h in range(N_XH):
            cs = slice(XHD * h, XHD * (h + 1))
            s = _dot(q_ref[:, cs], kv_ref[:, cs], NT) * (1.0 / 16.0)
            pe = jnp.exp(s - jnp.max(s, axis=1, keepdims=True))
            l = jnp.sum(pe, axis=1, keepdims=True)
            o_ref[:, cs] = (_dot(pe.astype(BF16), kv_ref[:, D + XHD * h:D + XHD * (h + 1)]) * (1.0 / l)).astype(BF16)

    nq = S // tq
    return pl.pallas_call(
        body, grid=(B, nq),
        in_specs=[pl.BlockSpec((tq, D), lambda b, t: (b * nq + t, 0)), pl.BlockSpec((M, 2 * D), lambda b, t: (b, 0))],
        out_specs=pl.BlockSpec((tq, D), lambda b, t: (b * nq + t, 0)), out_shape=SDS((B * S, D), BF16),
        compiler_params=_cp(), name=name,
    )(q, kv)


def _xattn_bwd(q, kv, do, *, B, S, M, tq, name):
    D = D_MODEL

    def body(q_ref, kv_ref, do_ref, dq_ref, dkv_ref):
        t = pl.program_id(1)

        @pl.when(t == 0)
        def _():
            dkv_ref[...] = jnp.zeros_like(dkv_ref)

        for h in range(N_XH):
            cs = slice(XHD * h, XHD * (h + 1))
            vs = slice(D + XHD * h, D + XHD * (h + 1))
            qh, kh, vh, doh = q_ref[:, cs], kv_ref[:, cs], kv_ref[:, vs], do_ref[:, cs]
            s = _dot(qh, kh, NT) * (1.0 / 16.0)
            pe = jnp.exp(s - jnp.max(s, axis=1, keepdims=True))
            pe = pe * (1.0 / jnp.sum(pe, axis=1, keepdims=True))
            dp = _dot(doh, vh, NT)
            ds = (pe * (dp - jnp.sum(pe * dp, axis=1, keepdims=True))).astype(BF16)
            dq_ref[:, cs] = (_dot(ds, kh) * (1.0 / 16.0)).astype(BF16)
            dkv_ref[:, cs] += _dot(ds, qh, TN) * (1.0 / 16.0)
            dkv_ref[:, vs] += _dot(pe.astype(BF16), doh, TN)

    nq = S // tq
    qspec = pl.BlockSpec((tq, D), lambda b, t: (b * nq + t, 0))
    kvspec = pl.BlockSpec((M, 2 * D), lambda b, t: (b, 0))
    return pl.pallas_call(
        body, grid=(B, nq), in_specs=[qspec, kvspec, qspec], out_specs=[qspec, kvspec],
        out_shape=[SDS((B * S, D), BF16), SDS((B * M, 2 * D), F32)], compiler_params=_cp(), name=name,
    )(q, kv, do)


def _adamw(parts, w, m, v, *, tr, name):
    R, C = w.shape

    def body(p_ref, w_ref, m_ref, v_ref, g_ref, d_ref, nm_ref, nv_ref):
        g = p_ref[0].astype(F32)
        for d in range(1, N_DEV):
            g = g + p_ref[d].astype(F32)
        m2 = ADAM_B1 * m_ref[...] + (1.0 - ADAM_B1) * g
        v2 = ADAM_B2 * v_ref[...] + (1.0 - ADAM_B2) * (g * g)
        m_hat = m2 / (1.0 - ADAM_B1 ** ADAM_STEP)
        v_hat = v2 / (1.0 - ADAM_B2 ** ADAM_STEP)
        g_ref[...] = g
        d_ref[...] = -ADAM_LR * (m_hat / (jnp.sqrt(v_hat) + ADAM_EPS) + ADAM_WD * w_ref[...])
        nm_ref[...] = m2
        nv_ref[...] = v2

    spec = pl.BlockSpec((tr, C), lambda i: (i, 0))
    return pl.pallas_call(
        body, grid=(R // tr,), in_specs=[pl.BlockSpec((N_DEV, tr, C), lambda i: (0, i, 0)), spec, spec, spec],
        out_specs=[spec] * 4, out_shape=[SDS((R, C), F32)] * 4, compiler_params=_cp(), name=name,
    )(parts, w, m, v)


def _peer(k, x, y, c):
    return (1 - x if k & 4 else x, 1 - y if k & 2 else y, 1 - c if k & 1 else c)


_HBM_SPEC = pl.BlockSpec(memory_space=pltpu.HBM)
_SEM_SPEC = pl.BlockSpec(memory_space=pltpu.SEMAPHORE)
_SPLIT_EFFECT = pltpu.SideEffectType.DATAFLOW_SIDE_EFFECTING


def _split_copies(srcs, lands, send_sems, recv_sems, modes):
    x, y, c = (lax.axis_index(a) for a in AXES)
    me = 4 * x + 2 * y + c
    copies = []
    for i, md in enumerate(modes):
        for k in range(1, N_DEV):
            px, py, pc = _peer(k, x, y, c)
            src = srcs[i] if md == "gather" else srcs[i].at[4 * px + 2 * py + pc]
            j = i * (N_DEV - 1) + k - 1
            copies.append(pltpu.make_async_remote_copy(
                src_ref=src, dst_ref=lands[i].at[me], send_sem=send_sems.at[j], recv_sem=recv_sems.at[j],
                device_id=(px, py, pc), device_id_type=pl.DeviceIdType.MESH))
    return copies


def _exchange_start(arrays, modes, *, name):
    n = len(arrays)
    hbm = lambda a: pltpu.with_memory_space_constraint(a, pltpu.HBM)
    srcs = [hbm(a) for a in arrays]
    me = 4 * lax.axis_index("x") + 2 * lax.axis_index("y") + lax.axis_index("c")

    def landing(a, md):
        own = a[None] if md == "gather" else lax.dynamic_index_in_dim(a, me, 0, keepdims=True)
        return hbm(lax.dynamic_update_index_in_dim(lax.empty((N_DEV,) + own.shape[1:], a.dtype), own, me, 0))

    lands = [landing(a, md) for a, md in zip(arrays, modes)]

    def body(*refs):
        for cp in _split_copies(refs[:n], refs[n:2 * n], refs[2 * n], refs[2 * n + 1], modes):
            cp.start()
        token = refs[-1]
        token[...] = jnp.zeros_like(token)

    sems = pltpu.SemaphoreType.DMA((n * (N_DEV - 1),))
    outs = pl.pallas_call(
        body, name=name, in_specs=[_HBM_SPEC] * (2 * n),
        out_shape=(sems, sems, *[pltpu.HBM(a.shape, a.dtype) for a in srcs + lands], SDS((8, 128), F32)),
        out_specs=(_SEM_SPEC, _SEM_SPEC, *[_HBM_SPEC] * (2 * n), pl.BlockSpec(memory_space=pltpu.VMEM)),
        input_output_aliases={i: 2 + i for i in range(2 * n)},
        compiler_params=pltpu.CompilerParams(has_side_effects=_SPLIT_EFFECT),
    )(*srcs, *lands)
    return (outs[0], outs[1], outs[2:2 + n], outs[2 + n:2 + 2 * n], modes), outs[-1]


def _exchange_wait(handle, after, *, name):
    send_sems, recv_sems, srcs, lands, modes = handle
    n = len(srcs)

    def body(*refs):
        for cp in _split_copies(refs[:n], refs[n:2 * n], refs[2 * n], refs[2 * n + 1], modes):
            cp.wait_send()
            cp.wait_recv()

    outs = pl.pallas_call(
        body, name=name, in_specs=[_HBM_SPEC] * (2 * n) + [_SEM_SPEC, _SEM_SPEC, pl.BlockSpec(memory_space=pl.ANY)],
        out_shape=tuple(pltpu.HBM(a.shape, a.dtype) for a in list(srcs) + list(lands)), out_specs=tuple([_HBM_SPEC] * (2 * n)),
        input_output_aliases={i: i for i in range(2 * n)},
        compiler_params=pltpu.CompilerParams(has_side_effects=_SPLIT_EFFECT),
    )(*srcs, *lands, send_sems, recv_sems, after)
    return list(outs[n:])


def _exchange(arrays, modes, *, name):
    n = len(arrays)
    out_shape = [SDS((N_DEV,) + a.shape if md == "gather" else a.shape, a.dtype) for a, md in zip(arrays, modes)]

    def body(*refs):
        ins, outs = refs[:n], refs[n:2 * n]
        send_sems, recv_sems, local_sems = refs[2 * n:]
        x, y, c = (lax.axis_index(a) for a in AXES)
        me = 4 * x + 2 * y + c
        copies = []
        for i, md in enumerate(modes):
            src = ins[i] if md == "gather" else ins[i].at[me]
            cp = pltpu.make_async_copy(src, outs[i].at[me], local_sems.at[i])
            cp.start()
            copies.append(cp)
            for k in range(1, N_DEV):
                px, py, pc = _peer(k, x, y, c)
                src = ins[i] if md == "gather" else ins[i].at[4 * px + 2 * py + pc]
                cp = pltpu.make_async_remote_copy(
                    src_ref=src, dst_ref=outs[i].at[me], send_sem=send_sems.at[i, k - 1], recv_sem=recv_sems.at[i, k - 1],
                    device_id=(px, py, pc), device_id_type=pl.DeviceIdType.MESH)
                cp.start()
                copies.append(cp)
        for cp in copies:
            cp.wait()

    anyspec = pl.BlockSpec(memory_space=pl.ANY)
    return pl.pallas_call(
        body, in_specs=[anyspec] * n, out_specs=[anyspec] * n, out_shape=out_shape,
        scratch_shapes=[pltpu.SemaphoreType.DMA((n, N_DEV - 1)), pltpu.SemaphoreType.DMA((n, N_DEV - 1)),
                        pltpu.SemaphoreType.DMA((n,))],
        name=name,
    )(*arrays)


def _local_step(x, mem, g_mix, b_forget, g_xattn, g_mem, g_mlp, g_final, target, get_w_in, get_rest, send):
    B, S, D = x.shape
    M = mem.shape[1]
    T = B * S
    x0 = x.reshape(T, D)
    mem2 = mem.reshape(B * M, D)
    tgt = target.reshape(T, D)
    b_pad = jnp.pad(b_forget, (0, 120)).reshape(1, 128)
    after = lambda a, tok: a if tok is None else a + tok[0, 0]

    h1 = _rms(x0, g_mix, tm=1024, name="f_norm")
    w_in_pad = get_w_in(h1)
    _, z, gate = _rms_matmul(h1, g_mix, w_in_pad[:, :QKV_W], tm=ROWS, tn=QKV_W, out_dtype=BF16,
                             w_f32=w_in_pad[:, QKV_W:], normed=True, name="f_in")
    cc = _gate_fwd(gate, b_pad, B=B, S=S, name="f_gatecum")
    ya, lse = _dil_attn_fwd(z, B=B, S=S, name="f_dil")
    yf, lse_f = _fox_fwd(z, cc, B=B, S=S, name="f_fox")
    ymix = jnp.concatenate([ya, yf], axis=1)
    w = get_rest(ymix)
    x1, h2, q = _res_rms_matmul(ymix, w["w_out"], x0, g_xattn, w["w_xq"], tm=ROWS, name="f_out")
    mn, kv = _rms_matmul(mem2, g_mem, w["w_kv"], tm=B * M, tn=D, out_dtype=BF16, name="f_xkv")
    xo = _xattn_fwd(q, kv, B=B, S=S, M=M, tq=1024, name="f_xattn")
    x2, h3, act = _res_rms_matmul(xo, w["w_xo"], x1, g_mlp, w["w_up"], tm=ROWS, relu=True, name="f_xo")
    dx3, dg_final, loss = _down_loss(act, w["w_down"], x2, g_final, tgt, tm=ROWS, name="f_down")

    du = _matmul_nt(dx3, w["w_down"], mul2a=act, tm=ROWS, tn=D_FF, name="b_dact")
    dw_down = _matmul_tn(act, dx3, square=True, bk=1024, bn=D, tt=ACC_ROWS, out_dtype=BF16, name="b_wdown")
    dw_up = _matmul_tn(h3, du, bk=D, bn=1024, tt=min(T, 2 * ACC_ROWS), out_dtype=BF16, name="b_wup")
    tok = send(dict(w_down=dw_down, w_up=dw_up))
    dx2, dg_mlp, dxo = _matmul_nt_rms(du, w["w_up"], x2, after(g_mlp, tok), dx3, then_w=w["w_xo"], tm=ROWS, tk=D_FF,
                                      name="b_dh3")
    dw_xo = _matmul_tn(xo, dx2, bk=D, bn=D, tt=ACC_ROWS, out_dtype=BF16, name="b_wxo")
    dq, dkv = _xattn_bwd(q, kv, dxo, B=B, S=S, M=M, tq=1024, name="b_xattn")
    dw_xq = _matmul_tn(h2, dq, bk=D, bn=D, tt=min(T, 2 * ACC_ROWS), out_dtype=BF16, name="b_wxq")
    dx1, dg_xattn, dy = _matmul_nt_rms(dq, w["w_xq"], x1, g_xattn, dx2, then_w=w["w_out"], tm=ROWS, tk=D, name="b_dh2")
    dw_kv = _matmul_tn(mn, dkv, bk=D, bn=D, tt=B * M, out_dtype=BF16, name="b_wkv")
    _, dg_mem = _matmul_nt_rms(dkv, w["w_kv"], mem2, g_mem, None, tm=min(ROWS, B * M), tk=2 * D, name="b_dmem")
    dw_out = _matmul_tn(ymix, dx1, bk=D, bn=D, tt=ACC_ROWS, out_dtype=BF16, name="b_wout")
    tok = send(dict(w_xo=dw_xo, w_xq=dw_xq, w_xk=dw_kv[:, :D], w_xv=dw_kv[:, D:], w_out=dw_out))
    dqf, dkf, dvf, dcc = _fox_bwd(z, dy, lse_f, cc, B=B, S=S, name="b_fox")
    dgate, db = _gate_bwd(dcc, gate, after(b_pad, tok), B=B, S=S, name="b_gate")
    dqa, dka, dva = _dil_attn_bwd(z, dy, ya, lse, B=B, S=S, name="b_dil")
    dz = [dqa, dka, dva, dqf, dkf, dvf, dgate]
    dw_in = jnp.concatenate([_matmul_tn_pieces(h1, dz[:3], tt=ACC_ROWS, name="b_win_dil"),
                             _matmul_tn_pieces(h1, dz[3:], tt=ACC_ROWS, name="b_win_fox")], axis=1)
    tok = send(dict(w_in=dw_in))
    gx, dg_mix = _matmul_nt_rms(dz, w_in_pad, x0, after(g_mix, tok), dx1, tm=ROWS, tk=IN_PAD, name="b_dh1")

    small = dict(g_mix=dg_mix, b_forget=db, g_xattn=dg_xattn, g_mem=dg_mem, g_mlp=dg_mlp, g_final=dg_final)
    return gx.reshape(B, S, D), small, loss


SMALL_ROWS = ("g_mix", "b_forget", "g_xattn", "g_mem", "g_mlp", "g_final")
COL_SHARDED = ("w_in", "w_up")


def _pack_rows(rows):
    D = D_MODEL
    rows = [jnp.pad(r.reshape(-1), (0, D - r.size)) for r in rows]
    rows += [jnp.zeros((D,), F32)] * (8 - len(rows))
    return jnp.stack(rows)


def _full(name, g):
    if name in COL_SHARDED:
        return g.transpose(1, 0, 2).reshape(g.shape[1], -1)
    return g.reshape(-1, g.shape[2])


def _blocks(name, g, shard_shape):
    if name in COL_SHARDED:
        n = shard_shape[1]
        return g[:, :n * N_DEV].reshape(g.shape[0], N_DEV, n).transpose(1, 0, 2)
    return g.reshape((N_DEV,) + shard_shape)


def kernel(x, mem, g_mix, w_in, b_forget, w_out, g_xattn, g_mem, w_xq, w_xk, w_xv, w_xo, g_mlp, w_up, w_down, g_final, loss_target, m_g_mix, m_w_in, m_b_forget, m_w_out, m_g_xattn, m_g_mem, m_w_xq, m_w_xk, m_w_xv, m_w_xo, m_g_mlp, m_w_up, m_w_down, m_g_final, v_g_mix, v_w_in, v_b_forget, v_w_out, v_g_xattn, v_g_mem, v_w_xq, v_w_xk, v_w_xv, v_w_xo, v_g_mlp, v_w_up, v_w_down, v_g_final):
    W = dict(w_in=w_in, w_out=w_out, w_xq=w_xq, w_xk=w_xk, w_xv=w_xv, w_xo=w_xo, w_up=w_up, w_down=w_down)
    Mo = dict(w_in=m_w_in, w_out=m_w_out, w_xq=m_w_xq, w_xk=m_w_xk, w_xv=m_w_xv, w_xo=m_w_xo, w_up=m_w_up, w_down=m_w_down)
    Vo = dict(w_in=v_w_in, w_out=v_w_out, w_xq=v_w_xq, w_xk=v_w_xk, w_xv=v_w_xv, w_xo=v_w_xo, w_up=v_w_up, w_down=v_w_down)
    later = [n for n in W if n != "w_in"]

    first_handle, first_token = _exchange_start([w_in.astype(BF16)], ["gather"], name="gather_in_start")
    rest_handle, rest_token = _exchange_start([W[n].astype(BF16) + first_token[0, 0].astype(BF16) for n in later],
                                              ["gather"] * len(later), name="gather_rest_start")

    def get_w_in(after):
        (g,) = _exchange_wait(first_handle, after, name="gather_in_wait")
        return jnp.pad(_full("w_in", g), ((0, 0), (0, IN_PAD - IN_W)))

    def get_rest(after):
        full = {n: _full(n, g) for n, g in zip(later, _exchange_wait(rest_handle, after, name="gather_rest_wait"))}
        full["w_kv"] = jnp.concatenate([full.pop("w_xk"), full.pop("w_xv")], axis=1)
        return full

    sent = []

    def send(grads):
        names = list(grads)
        handle, token = _exchange_start([_blocks(n, grads[n], W[n].shape) for n in names], ["scatter"] * len(names),
                                        name=f"scatter{len(sent)}_start")
        sent.append((names, handle))
        return token

    gx, small, loss = _local_step(x, mem, g_mix + rest_token[0, 0], b_forget, g_xattn, g_mem, g_mlp, g_final, loss_target,
                                  get_w_in, get_rest, send)

    received = {}
    for i, (names, handle) in enumerate(sent):
        received.update(zip(names, _exchange_wait(handle, gx, name=f"scatter{i}_wait")))
    packed = _pack_rows([small[n] for n in SMALL_ROWS] + [loss[0, :1]])
    (packed_all,) = _exchange([packed], ["gather"], name="gather_small")

    rows_per_step = lambda shape: max(t for t in (128, 256, 512) if shape[0] % t == 0 and t * shape[1] <= 512 * 512)
    res = {n: _adamw(received[n], W[n], Mo[n], Vo[n], tr=rows_per_step(W[n].shape), name=f"adamw_{n}") for n in W}
    small_w = dict(g_mix=g_mix, b_forget=b_forget, g_xattn=g_xattn, g_mem=g_mem, g_mlp=g_mlp, g_final=g_final)
    small_m = dict(g_mix=m_g_mix, b_forget=m_b_forget, g_xattn=m_g_xattn, g_mem=m_g_mem, g_mlp=m_g_mlp, g_final=m_g_final)
    small_v = dict(g_mix=v_g_mix, b_forget=v_b_forget, g_xattn=v_g_xattn, g_mem=v_g_mem, g_mlp=v_g_mlp, g_final=v_g_final)
    sres = _adamw(packed_all, _pack_rows([small_w[n] for n in SMALL_ROWS]), _pack_rows([small_m[n] for n in SMALL_ROWS]),
                  _pack_rows([small_v[n] for n in SMALL_ROWS]), tr=8, name="adamw_small")
    for i, n in enumerate(SMALL_ROWS):
        res[n] = [r[i, :small_w[n].size] for r in sres]
    loss_total = sres[0][6, 0]

    order = ["g_mix", "w_in", "b_forget", "w_out", "g_xattn", "g_mem", "w_xq", "w_xk", "w_xv", "w_xo", "g_mlp", "w_up", "w_down", "g_final"]
    return (loss_total, gx, *[res[n][0] for n in order], *[res[n][1] for n in order],
            *[res[n][2] for n in order], *[res[n][3] for n in order])
```

```python
import jax
import jax.numpy as jnp
from jax import lax
from jax.experimental import pallas as pl
from jax.experimental.pallas import tpu as pltpu

F32, BF16 = jnp.float32, jnp.bfloat16
SDS = jax.ShapeDtypeStruct

D_MODEL = 1024
HEAD_DIM = 64
WIDTH = 512
QKV_W = 6 * WIDTH
IN_W = QKV_W + 8
IN_PAD = QKV_W + 128
BLOCK = 128
DIL_CONFIGS = ((128, 1), (512, 4), (2048, 16))
N_XH, XHD = 4, 256
D_FF = 4096
EPS = 1e-6
NEG = -1e30
N_DEV = 8
AXES = ("x", "y", "c")

ADAM_LR, ADAM_B1, ADAM_B2, ADAM_EPS, ADAM_WD, ADAM_STEP = 0.001, 0.9, 0.999, 1e-08, 0.01, 10

VMEM_CAP_V7X = 64 * 1024 * 1024
VMEM_LIMIT = VMEM_CAP_V7X * 7 // 8

ROWS = 512
ACC_ROWS = 2048

NT = (((1,), (1,)), ((), ()))
TN = (((0,), (0,)), ((), ()))


def _cp(**kw):
    return pltpu.CompilerParams(vmem_limit_bytes=VMEM_LIMIT, **kw)


def _dot(a, b, dims=None):
    if dims is None:
        return jnp.dot(a, b, preferred_element_type=F32)
    return lax.dot_general(a, b, dims, preferred_element_type=F32)


def _rstd(xv):
    return lax.rsqrt(jnp.mean(xv * xv, axis=-1, keepdims=True) + EPS)


def _rms_bwd(dh, xv, g):
    r = _rstd(xv)
    xhat = xv * r
    dxhat = dh * g
    dx = r * (dxhat - xhat * jnp.mean(dxhat * xhat, axis=-1, keepdims=True))
    return dx, jnp.sum(dh * xhat, axis=0, keepdims=True)


def _rms_matmul(x, g, w, *, tm, tn, out_dtype, relu=False, w_f32=None, normed=False, name):
    T, D = x.shape
    N = w.shape[1]

    def body(*refs):
        x_ref, g_ref, w_ref = refs[:3]
        outs, h_s = refs[3 + (w_f32 is not None):-1], refs[-1]
        o_ref = outs[0 if normed else 1]

        @pl.when(pl.program_id(1) == 0)
        def _():
            xv = x_ref[...]
            h = xv if normed else (xv * _rstd(xv) * g_ref[...]).astype(BF16)
            h_s[...] = h
            if not normed:
                outs[0][...] = h
            if w_f32 is not None:
                outs[-1][...] = _dot(h, refs[3][...])

        acc = _dot(h_s[...], w_ref[...])
        if relu:
            acc = jnp.maximum(acc, 0.0)
        o_ref[...] = acc.astype(out_dtype)

    in_specs = [pl.BlockSpec((tm, D), lambda i, j: (i, 0)), pl.BlockSpec((1, D), lambda i, j: (0, 0)),
                pl.BlockSpec((D, tn), lambda i, j: (0, j))]
    out_specs = [pl.BlockSpec((tm, D), lambda i, j: (i, 0)), pl.BlockSpec((tm, tn), lambda i, j: (i, j))]
    out_shape = [SDS((T, D), BF16), SDS((T, N), out_dtype)]
    args = [x, g.reshape(1, D), w]
    if normed:
        out_specs, out_shape = out_specs[1:], out_shape[1:]
    if w_f32 is not None:
        n2 = w_f32.shape[1]
        in_specs.append(pl.BlockSpec((D, n2), lambda i, j: (0, 0)))
        out_specs.append(pl.BlockSpec((tm, n2), lambda i, j: (i, 0)))
        out_shape.append(SDS((T, n2), F32))
        args.append(w_f32)
    res = pl.pallas_call(
        body, grid=(T // tm, N // tn), in_specs=in_specs, out_specs=out_specs, out_shape=out_shape,
        scratch_shapes=[pltpu.VMEM((tm, D), BF16)], compiler_params=_cp(), name=name,
    )(*args)
    return [x] + list(res) if normed else res


def _rms(x, g, *, tm, name):
    T, D = x.shape

    def body(x_ref, g_ref, h_ref):
        xv = x_ref[...]
        h_ref[...] = (xv * _rstd(xv) * g_ref[...]).astype(BF16)

    rows = pl.BlockSpec((tm, D), lambda i: (i, 0))
    return pl.pallas_call(body, grid=(T // tm,), in_specs=[rows, pl.BlockSpec((1, D), lambda i: (0, 0))], out_specs=rows,
                          out_shape=SDS((T, D), BF16), compiler_params=_cp(), name=name)(x, g.reshape(1, D))


def _res_rms_matmul(a, w1, res, gain, w2, *, tm, relu=False, name):
    T, K = a.shape
    D, N = w2.shape

    def body(a_ref, w1_ref, res_ref, g_ref, w2_ref, x_ref, h_ref, o_ref):
        xv = res_ref[...] + _dot(a_ref[...], w1_ref[...])
        x_ref[...] = xv
        h = (xv * _rstd(xv) * g_ref[...]).astype(BF16)
        h_ref[...] = h
        acc = _dot(h, w2_ref[...])
        if relu:
            acc = jnp.maximum(acc, 0.0)
        o_ref[...] = acc.astype(BF16)

    rows = lambda n: pl.BlockSpec((tm, n), lambda i: (i, 0))
    whole = lambda r, c: pl.BlockSpec((r, c), lambda i: (0, 0))
    return pl.pallas_call(
        body, grid=(T // tm,), in_specs=[rows(K), whole(K, D), rows(D), whole(1, D), whole(D, N)],
        out_specs=[rows(D), rows(D), rows(N)], out_shape=[SDS((T, D), F32), SDS((T, D), BF16), SDS((T, N), BF16)],
        compiler_params=_cp(), name=name,
    )(a, w1, res, gain.reshape(1, D), w2)


def _matmul_nt(g, w, *, mul2a=None, tm, tn, name):
    T, K = g.shape
    N = w.shape[0]

    def body(*refs):
        g_ref, w_ref = refs[0], refs[1]
        o_ref = refs[-1]
        acc = _dot(g_ref[...].astype(BF16), w_ref[...], NT)
        if mul2a is not None:
            acc = acc * (2.0 * refs[2][...].astype(F32))
        o_ref[...] = acc.astype(BF16)

    in_specs = [pl.BlockSpec((tm, K), lambda i, j: (i, 0)), pl.BlockSpec((tn, K), lambda i, j: (j, 0))]
    args = [g, w]
    if mul2a is not None:
        in_specs.append(pl.BlockSpec((tm, tn), lambda i, j: (i, j)))
        args.append(mul2a)
    return pl.pallas_call(
        body, grid=(T // tm, N // tn), in_specs=in_specs,
        out_specs=pl.BlockSpec((tm, tn), lambda i, j: (i, j)), out_shape=SDS((T, N), BF16),
        compiler_params=_cp(), name=name,
    )(*args)


def _matmul_nt_rms(g, w, x, gain, dres, *, then_w=None, tm, tk, name):
    pieces = list(g) if isinstance(g, (list, tuple)) else [g]
    widths = [p.shape[1] for p in pieces]
    T, K = pieces[0].shape[0], sum(widths)
    D = w.shape[0]
    nk = K // tk
    nt = T // tm
    npc = len(pieces)
    assert npc == 1 or nk == 1
    n_in = npc + 3 + (dres is not None) + (then_w is not None)

    def body(*refs):
        w_ref, x_ref, gain_ref = refs[npc:npc + 3]
        dres_ref = refs[npc + 3] if dres is not None else None
        then_ref = refs[n_in - 1] if then_w is not None else None
        dx_ref, dg_ref = refs[n_in], refs[n_in + 1]
        i, k = pl.program_id(0), pl.program_id(1)
        if npc == 1:
            part = _dot(refs[0][...].astype(BF16), w_ref[...], NT)
        else:
            part, off = None, 0
            for j in range(npc):
                d = _dot(refs[j][...].astype(BF16), w_ref[:, off:off + widths[j]], NT)
                part = d if part is None else part + d
                off += widths[j]

        def finish(dh):
            dx, dg = _rms_bwd(dh, x_ref[...], gain_ref[...])
            if dres_ref is not None:
                dx = dres_ref[...] + dx
            dx_ref[...] = dx
            if then_ref is not None:
                refs[n_in + 2][...] = _dot(dx.astype(BF16), then_ref[...], NT).astype(BF16)

            @pl.when(i == 0)
            def _():
                dg_ref[...] = dg

            @pl.when(i > 0)
            def _():
                dg_ref[...] += dg

        if nk == 1:
            finish(part)
        else:
            acc = refs[-1]

            @pl.when(k == 0)
            def _():
                acc[...] = part

            @pl.when(k > 0)
            def _():
                acc[...] += part

            @pl.when(k == nk - 1)
            def _():
                finish(acc[...])

    g_specs = ([pl.BlockSpec((tm, tk), lambda i, k: (i, k))] if npc == 1 else
               [pl.BlockSpec((tm, wd), lambda i, k: (i, 0)) for wd in widths])
    in_specs = g_specs + [pl.BlockSpec((D, tk), lambda i, k: (0, k)),
                          pl.BlockSpec((tm, D), lambda i, k: (i, 0)), pl.BlockSpec((1, D), lambda i, k: (0, 0))]
    args = pieces + [w, x, gain.reshape(1, D)]
    out_specs = [pl.BlockSpec((tm, D), lambda i, k: (i, 0)), pl.BlockSpec((1, D), lambda i, k: (0, 0))]
    out_shape = [SDS((T, D), F32), SDS((1, D), F32)]
    if dres is not None:
        in_specs.append(pl.BlockSpec((tm, D), lambda i, k: (i, 0)))
        args.append(dres)
    if then_w is not None:
        n2 = then_w.shape[0]
        in_specs.append(pl.BlockSpec((n2, D), lambda i, k: (0, 0)))
        args.append(then_w)
        out_specs.append(pl.BlockSpec((tm, n2), lambda i, k: (i, 0)))
        out_shape.append(SDS((T, n2), BF16))
    return pl.pallas_call(
        body, grid=(nt, nk), in_specs=in_specs, out_specs=out_specs, out_shape=out_shape,
        scratch_shapes=[pltpu.VMEM((tm, D), F32)] if nk > 1 else [], compiler_params=_cp(), name=name,
    )(*args)


def _matmul_tn(a, g, *, square=False, bk, bn, tt, out_dtype, name):
    T, K = a.shape
    N = g.shape[1]
    nt = T // tt

    def body(a_ref, g_ref, o_ref, acc):
        t = pl.program_id(2)
        av = a_ref[...]
        if square:
            af = av.astype(F32)
            av = (af * af).astype(BF16)
        part = _dot(av, g_ref[...].astype(BF16), TN)
        if nt == 1:
            o_ref[...] = part.astype(out_dtype)
        else:
            @pl.when(t == 0)
            def _():
                acc[...] = part

            @pl.when((t > 0) & (t < nt - 1))
            def _():
                acc[...] += part

            @pl.when(t == nt - 1)
            def _():
                o_ref[...] = (acc[...] + part).astype(out_dtype)

    return pl.pallas_call(
        body, grid=(K // bk, N // bn, nt),
        in_specs=[pl.BlockSpec((tt, bk), lambda i, j, t: (t, i)), pl.BlockSpec((tt, bn), lambda i, j, t: (t, j))],
        out_specs=pl.BlockSpec((bk, bn), lambda i, j, t: (i, j)), out_shape=SDS((K, N), out_dtype),
        scratch_shapes=[pltpu.VMEM((bk, bn), F32)], compiler_params=_cp(), name=name,
    )(a, g)


def _matmul_tn_pieces(a, pieces, *, tt, name):
    T, K = a.shape
    widths = [p.shape[1] for p in pieces]
    W = sum(widths)
    nt = T // tt
    n = len(pieces)

    def body(*refs):
        a_ref, o_ref, acc = refs[0], refs[n + 1], refs[n + 2]
        t = pl.program_id(0)
        av = a_ref[...]
        off = 0
        for j in range(n):
            cols = slice(off, off + widths[j])
            part = _dot(av, refs[1 + j][...], TN)

            @pl.when(t == 0)
            def _():
                acc[:, cols] = part

            @pl.when(t > 0)
            def _():
                acc[:, cols] += part

            off += widths[j]

        @pl.when(t == nt - 1)
        def _():
            o_ref[...] = acc[...].astype(BF16)

    return pl.pallas_call(
        body, grid=(nt,),
        in_specs=[pl.BlockSpec((tt, K), lambda t: (t, 0))] + [pl.BlockSpec((tt, w), lambda t: (t, 0)) for w in widths],
        out_specs=pl.BlockSpec((K, W), lambda t: (0, 0)), out_shape=SDS((K, W), BF16),
        scratch_shapes=[pltpu.VMEM((K, W), F32)], compiler_params=_cp(), name=name,
    )(a, *pieces)


def _down_loss(act, w_down, x2, g_final, target, *, tm, name):
    T, D = x2.shape
    F = act.shape[1]

    def body(a_ref, w_ref, x2_ref, g_ref, t_ref, dx_ref, dg_ref, loss_ref):
        i = pl.program_id(0)
        af = a_ref[...].astype(F32)
        xv, g = x2_ref[...] + _dot((af * af).astype(BF16), w_ref[...]), g_ref[...]
        r = _rstd(xv)
        xhat = xv * r
        diff = xhat * g - t_ref[...]
        part = 0.5 * jnp.sum(jnp.mean(diff * diff, axis=-1, keepdims=True), axis=0, keepdims=True)
        dy = diff * (1.0 / D)
        dxhat = dy * g
        dx_ref[...] = r * (dxhat - xhat * jnp.mean(dxhat * xhat, axis=-1, keepdims=True))
        dg = jnp.sum(dy * xhat, axis=0, keepdims=True)
        lp = jnp.broadcast_to(part, loss_ref.shape)

        @pl.when(i == 0)
        def _():
            dg_ref[...] = dg
            loss_ref[...] = lp

        @pl.when(i > 0)
        def _():
            dg_ref[...] += dg
            loss_ref[...] += lp

    rows = pl.BlockSpec((tm, D), lambda i: (i, 0))
    return pl.pallas_call(
        body, grid=(T // tm,),
        in_specs=[pl.BlockSpec((tm, F), lambda i: (i, 0)), pl.BlockSpec((F, D), lambda i: (0, 0)), rows,
                  pl.BlockSpec((1, D), lambda i: (0, 0)), rows],
        out_specs=[rows, pl.BlockSpec((1, D), lambda i: (0, 0)), pl.BlockSpec((8, 128), lambda i: (0, 0))],
        out_shape=[SDS((T, D), F32), SDS((1, D), F32), SDS((8, 128), F32)],
        compiler_params=_cp(), name=name,
    )(act, w_down, x2, g_final.reshape(1, D), target)


def _head_lanes(shape, width):
    return lax.broadcasted_iota(jnp.int32, shape, len(shape) - 1) // width


def _gate_fwd(gate, b_pad, *, B, S, name):
    def body(g_ref, b_ref, cc_ref):
        xv = g_ref[...] + b_ref[...]
        lf = jnp.minimum(xv, 0.0) - jnp.log(1.0 + jnp.exp(-jnp.abs(xv)))
        lane = lax.broadcasted_iota(jnp.int32, lf.shape, 1)
        row = lax.broadcasted_iota(jnp.int32, lf.shape, 0)
        c = jnp.where(lane < 8, lf, 0.0)
        sh = 1
        while sh < S:
            c = c + jnp.where(row >= sh, pltpu.roll(c, sh, 0), 0.0)
            sh *= 2
        grp = _head_lanes((S, WIDTH), HEAD_DIM)
        cc = jnp.zeros((S, WIDTH), F32)
        for h in range(8):
            cc = jnp.where(grp == h, c[:, h:h + 1], cc)
        cc_ref[...] = cc

    return pl.pallas_call(
        body, grid=(B,),
        in_specs=[pl.BlockSpec((S, 128), lambda b: (b, 0)), pl.BlockSpec((1, 128), lambda b: (0, 0))],
        out_specs=pl.BlockSpec((S, WIDTH), lambda b: (b, 0)), out_shape=SDS((B * S, WIDTH), F32),
        compiler_params=_cp(), name=name,
    )(gate, b_pad)


def _gate_bwd(dcc, gate, b_pad, *, B, S, name):
    def body(dcc_ref, g_ref, b_ref, dg_ref, db_ref):
        bi = pl.program_id(0)
        dccv = dcc_ref[...]
        lane = lax.broadcasted_iota(jnp.int32, (S, 128), 1)
        row = lax.broadcasted_iota(jnp.int32, (S, 128), 0)
        dc = jnp.zeros((S, 128), F32)
        for h in range(8):
            dc = jnp.where(lane == h, dccv[:, HEAD_DIM * h:HEAD_DIM * h + 1], dc)
        sh = 1
        while sh < S:
            dc = dc + jnp.where(row < S - sh, pltpu.roll(dc, S - sh, 0), 0.0)
            sh *= 2
        xv = g_ref[...] + b_ref[...]
        dgate = jnp.where(lane < 8, dc / (1.0 + jnp.exp(xv)), 0.0)
        dg_ref[...] = dgate.astype(BF16)
        db = jnp.sum(dgate, axis=0, keepdims=True)

        @pl.when(bi == 0)
        def _():
            db_ref[...] = db

        @pl.when(bi > 0)
        def _():
            db_ref[...] += db

    return pl.pallas_call(
        body, grid=(B,),
        in_specs=[pl.BlockSpec((S, WIDTH), lambda b: (b, 0)), pl.BlockSpec((S, 128), lambda b: (b, 0)),
                  pl.BlockSpec((1, 128), lambda b: (0, 0))],
        out_specs=[pl.BlockSpec((S, 128), lambda b: (b, 0)), pl.BlockSpec((1, 128), lambda b: (0, 0))],
        out_shape=[SDS((B * S, 128), BF16), SDS((1, 128), F32)],
        compiler_params=_cp(), name=name,
    )(dcc, gate, b_pad)


_SMEM_SPEC = pl.BlockSpec(memory_space=pltpu.SMEM)


def _alibi_slopes():
    return 2.0 ** (-(jnp.arange(1, 9, dtype=F32) * (8.0 / 8)))


def _pair_masks():
    lane = lax.broadcasted_iota(jnp.int32, (1, 128), 1)
    first = lane < HEAD_DIM
    return (first.astype(BF16), (~first).astype(BF16)), first


BNT =(((2,), (2,)), ((0,), (0,)))
BNN = (((2,), (1,)), ((0,), (0,)))
BTN = (((1,), (1,)), ((0,), (0,)))


def _split3(v):
    hi = v.astype(BF16).astype(F32)
    mid = (v - hi).astype(BF16).astype(F32)
    lo = (v - hi - mid).astype(BF16).astype(F32)
    return [hi, mid, lo]


def _with_spare_lanes(base, e, cols):
    lane = lax.broadcasted_iota(jnp.int32, (1, 128), 1)
    off = HEAD_DIM * (1 - e)
    extra = jnp.zeros(base.shape, F32)
    for j, c in enumerate(cols):
        extra = jnp.where(lane == off + j, c, extra)
    return base + extra.astype(BF16)


ONES3 = [1.0, 1.0, 1.0]


def _band_bias(slope, dilation):
    qi = lax.broadcasted_iota(jnp.int32, (BLOCK, BLOCK), 0)
    kj = lax.broadcasted_iota(jnp.int32, (BLOCK, BLOCK), 1)
    cur = jnp.where(kj <= qi, (-slope * dilation) * (qi - kj).astype(F32), NEG)
    prev = jnp.where(kj >= qi, (-slope * dilation) * (qi + BLOCK - kj).astype(F32), NEG)
    return cur, prev


def _to_residue_major(dst, src_f32, dilation, nb, lead=0, src=None):
    L = nb * BLOCK
    if dilation == 1 and src is not None:
        dst[lead:lead + nb] = src[...].reshape(nb, BLOCK, 128)
        return
    for r in range(dilation):
        rows = src_f32[pl.ds(r, L, stride=dilation), :] if dilation > 1 else src_f32[...]
        dst[lead + r * nb:lead + (r + 1) * nb] = rows.reshape(nb, BLOCK, 128).astype(dst.dtype)


def _dil_attn_fwd(z, *, B, S, name):
    NB = S // BLOCK

    def body(slope_ref, q_ref, k_ref, v_ref, y_ref, lse_ref, qf, kf, vf, qd, kd, vd, od, ld, acc_o, acc_l):
        (m_first, m_second), first = _pair_masks()
        p = pl.program_id(1)
        qf[...] = q_ref[...].astype(F32)
        kf[...] = k_ref[...].astype(F32)
        vf[...] = v_ref[...].astype(F32)
        kd[0] = jnp.zeros((BLOCK, 128), BF16)
        vd[0] = jnp.zeros((BLOCK, 128), BF16)
        blk = lax.broadcasted_iota(jnp.int32, (NB, 1, 1), 0)

        for idx, (_, dilation) in enumerate(DIL_CONFIGS):
            nb = NB // dilation
            _to_residue_major(qd, qf, dilation, nb, src=q_ref)
            _to_residue_major(kd, kf, dilation, nb, lead=1, src=k_ref)
            _to_residue_major(vd, vf, dilation, nb, lead=1, src=v_ref)
            q4, kc, vc = qd[...], kd[1:NB + 1], vd[1:NB + 1]
            outs, lses = [], []
            for e, hm in enumerate((m_first, m_second)):
                bias_cur, bias_prev = _band_bias(slope_ref[2 * p + e], dilation)
                qm = q4 * hm
                sc = _dot(qm, kc, BNT) * 0.125 + bias_cur
                m = jnp.max(sc, axis=2, keepdims=True)
                if nb > 1:
                    sp = _dot(qm, kd[0:NB], BNT) * 0.125 + jnp.where(blk % nb == 0, NEG, bias_prev)
                    m = jnp.maximum(m, jnp.max(sp, axis=2, keepdims=True))
                pc = jnp.exp(sc - m)
                l = jnp.sum(pc, axis=2, keepdims=True)
                o = _dot(pc.astype(BF16), vc, BNN)
                if nb > 1:
                    pp = jnp.exp(sp - m)
                    l = l + jnp.sum(pp, axis=2, keepdims=True)
                    o = o + _dot(pp.astype(BF16), vd[0:NB], BNN)
                outs.append(o * (1.0 / l))
                lses.append(m + jnp.log(l))
            od[...] = jnp.where(first, outs[0], outs[1])
            ld[...] = jnp.where(first, lses[0], lses[1])

            L = nb * BLOCK
            for r in range(dilation):
                rows = pl.ds(r, L, stride=dilation) if dilation > 1 else slice(None)
                o_new = od[r * nb:(r + 1) * nb].reshape(L, 128)
                l_new = ld[r * nb:(r + 1) * nb].reshape(L, 128)
                if idx == 0:
                    acc_o[rows, :] = o_new
                    acc_l[rows, :] = l_new
                else:
                    l_old = acc_l[rows, :]
                    m2 = jnp.maximum(l_old, l_new)
                    w_old, w_new = jnp.exp(l_old - m2), jnp.exp(l_new - m2)
                    tot = w_old + w_new
                    acc_o[rows, :] = (w_old * acc_o[rows, :] + w_new * o_new) * (1.0 / tot)
                    acc_l[rows, :] = m2 + jnp.log(tot)

        y_ref[...] = acc_o[...].astype(BF16)
        lse_ref[...] = acc_l[...]

    spec = lambda off: pl.BlockSpec((S, 128), lambda b, p: (b, 4 * off + p))
    ospec = pl.BlockSpec((S, 128), lambda b, p: (b, p))
    blocks = lambda n, dt: pltpu.VMEM((n, BLOCK, 128), dt)
    return pl.pallas_call(
        body, grid=(B, 4), in_specs=[_SMEM_SPEC, spec(0), spec(1), spec(2)], out_specs=[ospec, ospec],
        out_shape=[SDS((B * S, WIDTH), BF16), SDS((B * S, WIDTH), F32)],
        scratch_shapes=[pltpu.VMEM((S, 128), F32)] * 3 + [blocks(NB, BF16), blocks(NB + 1, BF16), blocks(NB + 1, BF16),
                                                         blocks(NB, F32), blocks(NB, F32)] + [pltpu.VMEM((S, 128), F32)] * 2,
        compiler_params=_cp(), name=name,
    )(_alibi_slopes(), z, z, z)


def _dil_attn_bwd(z, dy, ya, lse, *, B, S, name):
    NB = S // BLOCK

    def body(slope_ref, q_ref, k_ref, v_ref, do_ref, o_ref, lse_ref, dq_ref, dk_ref, dv_ref,
             qf, kf, vf, dof, ef, qd, dod, kd, vd, lsd, dkd, dvd, dqa, dka, dva):
        (m_first, m_second), first = _pair_masks()
        p = pl.program_id(1)
        qf[...] = q_ref[...].astype(F32)
        kf[...] = k_ref[...].astype(F32)
        vf[...] = v_ref[...].astype(F32)
        dov = do_ref[...].astype(F32)
        dof[...] = dov
        prod = dov * o_ref[...].astype(F32)
        rowdot = jnp.where(first, jnp.sum(jnp.where(first, prod, 0.0), axis=1, keepdims=True),
                           jnp.sum(jnp.where(first, 0.0, prod), axis=1, keepdims=True))
        lane = lax.broadcasted_iota(jnp.int32, (1, 128), 1)
        ef[...] = jnp.where(lane % HEAD_DIM < HEAD_DIM // 2, lse_ref[...], rowdot)
        kd[0] = jnp.zeros((BLOCK, 128), BF16)
        vd[0] = jnp.zeros((BLOCK, 128), BF16)
        blk = lax.broadcasted_iota(jnp.int32, (NB, 1, 1), 0)

        for idx, (_, dilation) in enumerate(DIL_CONFIGS):
            nb = NB // dilation
            _to_residue_major(qd, qf, dilation, nb, src=q_ref)
            _to_residue_major(dod, dof, dilation, nb, src=do_ref)
            _to_residue_major(kd, kf, dilation, nb, lead=1, src=k_ref)
            _to_residue_major(vd, vf, dilation, nb, lead=1, src=v_ref)
            _to_residue_major(lsd, ef, dilation, nb)
            stats = lsd[...]
            q4, do4, kc, vc = qd[...], dod[...], kd[1:NB + 1], vd[1:NB + 1]
            dq4 = None
            dkc = dvc = dkp = dvp = None
            for e, hm in enumerate((m_first, m_second)):
                lane0 = slice(HEAD_DIM * e, HEAD_DIM * e + 1)
                bias_cur, bias_prev = _band_bias(slope_ref[2 * p + e], dilation)
                qm, dom = q4 * hm, do4 * hm
                lse_e = stats[:, :, lane0]
                e_e = stats[:, :, HEAD_DIM * e + HEAD_DIM // 2:HEAD_DIM * e + HEAD_DIM // 2 + 1]
                pc = jnp.exp(_dot(qm, kc, BNT) * 0.125 + bias_cur - lse_e)
                dsc = (pc * (_dot(dom, vc, BNT) - e_e)).astype(BF16)
                pcb = pc.astype(BF16)
                dqe = _dot(dsc, kc, BNN)
                dkc = _dot(dsc, qm, BTN) if e == 0 else dkc + _dot(dsc, qm, BTN)
                dvc = _dot(pcb, dom, BTN) if e == 0 else dvc + _dot(pcb, dom, BTN)
                if nb > 1:
                    kp, vp = kd[0:NB], vd[0:NB]
                    pp = jnp.exp(_dot(qm, kp, BNT) * 0.125 + jnp.where(blk % nb == 0, NEG, bias_prev) - lse_e)
                    dsp = (pp * (_dot(dom, vp, BNT) - e_e)).astype(BF16)
                    ppb = pp.astype(BF16)
                    dqe = dqe + _dot(dsp, kp, BNN)
                    dkp = _dot(dsp, qm, BTN) if e == 0 else dkp + _dot(dsp, qm, BTN)
                    dvp = _dot(ppb, dom, BTN) if e == 0 else dvp + _dot(ppb, dom, BTN)
                dq4 = dqe if e == 0 else jnp.where(first, dq4, dqe)

            dkd[1:NB + 1] = dkc
            dvd[1:NB + 1] = dvc
            if nb > 1:
                dkd[1:NB] += dkp[1:NB]
                dvd[1:NB] += dvp[1:NB]
            L = nb * BLOCK
            for r in range(dilation):
                rows = pl.ds(r, L, stride=dilation) if dilation > 1 else slice(None)
                dq_r = dq4[r * nb:(r + 1) * nb].reshape(L, 128) * 0.125
                dk_r = dkd[1 + r * nb:1 + (r + 1) * nb].reshape(L, 128) * 0.125
                dv_r = dvd[1 + r * nb:1 + (r + 1) * nb].reshape(L, 128)
                if idx == 0:
                    dqa[rows, :], dka[rows, :], dva[rows, :] = dq_r, dk_r, dv_r
                else:
                    dqa[rows, :] += dq_r
                    dka[rows, :] += dk_r
                    dva[rows, :] += dv_r

        dq_ref[...] = dqa[...].astype(BF16)
        dk_ref[...] = dka[...].astype(BF16)
        dv_ref[...] = dva[...].astype(BF16)

    spec = lambda off: pl.BlockSpec((S, 128), lambda b, p: (b, 4 * off + p))
    ospec = pl.BlockSpec((S, 128), lambda b, p: (b, p))
    blocks = lambda n, dt: pltpu.VMEM((n, BLOCK, 128), dt)
    return pl.pallas_call(
        body, grid=(B, 4), in_specs=[_SMEM_SPEC, spec(0), spec(1), spec(2), ospec, ospec, ospec],
        out_specs=[ospec] * 3, out_shape=[SDS((B * S, WIDTH), BF16)] * 3,
        scratch_shapes=[pltpu.VMEM((S, 128), F32)] * 5
        + [blocks(NB, BF16), blocks(NB, BF16), blocks(NB + 1, BF16), blocks(NB + 1, BF16), blocks(NB, F32),
           blocks(NB + 1, F32), blocks(NB + 1, F32)] + [pltpu.VMEM((S, 128), F32)] * 3,
        compiler_params=_cp(), name=name,
    )(_alibi_slopes(), z, z, z, dy, ya, lse)


FOX_TQ = 256
FOX_TQ_FWD = 512


def _fox_fwd(z, cc, *, B, S, name):
    def body(q_ref, k_ref, v_ref, cc_ref, o_ref, l_ref, qa, ka):
        (m_first, m_second), first = _pair_masks()
        ccv = cc_ref[...]
        eighth = jnp.asarray(0.125, BF16)
        for e, hm in enumerate((m_first, m_second)):
            c_e = jnp.broadcast_to(ccv[:, HEAD_DIM * e:HEAD_DIM * e + 1], (S, 128))
            qa[e] = _with_spare_lanes(q_ref[...] * hm * eighth, e, _split3(c_e) + ONES3)
            ka[e] = _with_spare_lanes(k_ref[...] * hm, e, ONES3 + _split3(-c_e))
        for qi in range(S // FOX_TQ_FWD):
            r0, kend = qi * FOX_TQ_FWD, (qi + 1) * FOX_TQ_FWD
            vv = v_ref[0:kend, :]
            row = lax.broadcasted_iota(jnp.int32, (FOX_TQ_FWD, kend), 0) + r0
            col = lax.broadcasted_iota(jnp.int32, (FOX_TQ_FWD, kend), 1)
            causal = col <= row
            outs, lses = [], []
            for e in (0, 1):
                s = jnp.where(causal, _dot(qa[e, r0:kend, :], ka[e, 0:kend, :], NT), NEG)
                m = jnp.max(s, axis=1, keepdims=True)
                pe = jnp.exp(s - m)
                l = jnp.sum(pe, axis=1, keepdims=True)
                outs.append(_dot(pe.astype(BF16), vv) * (1.0 / l))
                lses.append(m + jnp.log(l))
            o_ref[r0:kend, :] = jnp.where(first, outs[0], outs[1]).astype(BF16)
            l_ref[r0:kend, :] = jnp.where(first, lses[0], lses[1])

    spec = lambda off: pl.BlockSpec((S, 128), lambda b, p: (b, 4 * off + p))
    pspec = pl.BlockSpec((S, 128), lambda b, p: (b, p))
    return pl.pallas_call(
        body, grid=(B, 4), in_specs=[spec(3), spec(4), spec(5), pspec], out_specs=[pspec, pspec],
        out_shape=[SDS((B * S, WIDTH), BF16), SDS((B * S, WIDTH), F32)],
        scratch_shapes=[pltpu.VMEM((2, S, 128), BF16)] * 2, compiler_params=_cp(), name=name,
    )(z, z, z, cc)


def _fox_bwd(z, dy, lse, cc, *, B, S, name):
    def body(q_ref, k_ref, v_ref, do_ref, lse_ref, cc_ref, dq_ref, dk_ref, dv_ref, dc_ref,
             qa, ka, qp, kp, vp, dp, dk_s, dv_s, dc_s):
        (m_first, m_second), first = _pair_masks()
        ccv, lsev = cc_ref[...], lse_ref[...]
        eighth = jnp.asarray(0.125, BF16)
        for e, hm in enumerate((m_first, m_second)):
            lane0 = slice(HEAD_DIM * e, HEAD_DIM * e + 1)
            c_e = jnp.broadcast_to(ccv[:, lane0], (S, 128))
            lse_e = jnp.broadcast_to(lsev[:, lane0], (S, 128))
            qp[e] = q_ref[...] * hm
            kp[e] = k_ref[...] * hm
            dp[e] = do_ref[...] * hm
            qa[e] = _with_spare_lanes(qp[e] * eighth, e, _split3(c_e - lse_e) + ONES3)
            ka[e] = _with_spare_lanes(kp[e], e, ONES3 + _split3(-c_e))
            vp[e] = v_ref[...] * hm
        dk_s[...] = jnp.zeros_like(dk_s)
        dv_s[...] = jnp.zeros_like(dv_s)
        dc_s[...] = jnp.zeros_like(dc_s)
        for qi in range(S // FOX_TQ):
            r0, kend = qi * FOX_TQ, (qi + 1) * FOX_TQ
            krow = lax.broadcasted_iota(jnp.int32, (kend, FOX_TQ), 0)
            qcol = lax.broadcasted_iota(jnp.int32, (kend, FOX_TQ), 1) + r0
            causal = krow <= qcol
            dq_t = jnp.zeros((FOX_TQ, 128), F32)
            for e in (0, 1):
                sel = first if e == 0 else ~first
                pt = jnp.where(causal, jnp.exp(_dot(ka[e, 0:kend, :], qa[e, r0:kend, :], NT)), 0.0)
                dpt = _dot(vp[e, 0:kend, :], dp[e, r0:kend, :], NT)
                mean = jnp.sum(pt * dpt, axis=0, keepdims=True) / jnp.sum(pt, axis=0, keepdims=True)
                dst = pt * (dpt - mean)
                dsb = dst.astype(BF16)
                dv_s[0:kend, :] += _dot(pt.astype(BF16), dp[e, r0:kend, :])
                dk_s[0:kend, :] += _dot(dsb, qp[e, r0:kend, :]) * 0.125
                dq_t = dq_t + _dot(dsb, kp[e, 0:kend, :], TN)
                dc_s[0:kend, :] += jnp.where(sel, -jnp.sum(dst, axis=1, keepdims=True), 0.0)
            dq_ref[r0:kend, :] = (dq_t * 0.125).astype(BF16)
        dk_ref[...] = dk_s[...].astype(BF16)
        dv_ref[...] = dv_s[...].astype(BF16)
        dc_ref[...] = dc_s[...]

    spec = lambda off: pl.BlockSpec((S, 128), lambda b, p: (b, 4 * off + p))
    pspec = pl.BlockSpec((S, 128), lambda b, p: (b, p))
    return pl.pallas_call(
        body, grid=(B, 4),
        in_specs=[spec(3), spec(4), spec(5), pl.BlockSpec((S, 128), lambda b, p: (b, 4 + p)), pspec, pspec],
        out_specs=[pspec] * 4,
        out_shape=[SDS((B * S, WIDTH), BF16)] * 3 + [SDS((B * S, WIDTH), F32)],
        scratch_shapes=[pltpu.VMEM((2, S, 128), BF16)] * 6 + [pltpu.VMEM((S, 128), F32)] * 3,
        compiler_params=_cp(), name=name,
    )(z, z, z, dy, lse, cc)


def _xattn_fwd(q, kv, *, B, S, M, tq, name):
    D = D_MODEL

    def body(q_ref, kv_ref, o_ref):
        for h in range(N_XH):
            cs = slice(XHD * h, XHD * (h + 1))
            s = _dot(q_ref[:, cs], kv_ref[:, cs], NT) * (1.0 / 16.0)
            pe = jnp.exp(s - jnp.max(s, axis=1, keepdims=True))
            l = jnp.sum(pe, axis=1, keepdims=True)
            o_ref[:, cs] = (_dot(pe.astype(BF16), kv_ref[:, D + XHD * h:D + XHD * (h + 1)]) * (1.0 / l)).astype(BF16)

    nq = S // tq
    return pl.pallas_call(
        body, grid=(B, nq),
        in_specs=[pl.BlockSpec((tq, D), lambda b, t: (b * nq + t, 0)), pl.BlockSpec((M, 2 * D), lambda b, t: (b, 0))],
        out_specs=pl.BlockSpec((tq, D), lambda b, t: (b * nq + t, 0)), out_shape=SDS((B * S, D), BF16),
        compiler_params=_cp(), name=name,
    )(q, kv)


def _xattn_bwd(q, kv, do, *, B, S, M, tq, name):
    D = D_MODEL

    def body(q_ref, kv_ref, do_ref, dq_ref, dkv_ref):
        t = pl.program_id(1)

        @pl.when(t == 0)
        def _():
            dkv_ref[...] = jnp.zeros_like(dkv_ref)

        for h in range(N_XH):
            cs = slice(XHD * h, XHD * (h + 1))
            vs = slice(D + XHD * h, D + XHD * (h + 1))
            qh, kh, vh, doh = q_ref[:, cs], kv_ref[:, cs], kv_ref[:, vs], do_ref[:, cs]
            s = _dot(qh, kh, NT) * (1.0 / 16.0)
            pe = jnp.exp(s - jnp.max(s, axis=1, keepdims=True))
            pe = pe * (1.0 / jnp.sum(pe, axis=1, keepdims=True))
            dp = _dot(doh, vh, NT)
            ds = (pe * (dp - jnp.sum(pe * dp, axis=1, keepdims=True))).astype(BF16)
            dq_ref[:, cs] = (_dot(ds, kh) * (1.0 / 16.0)).astype(BF16)
            dkv_ref[:, cs] += _dot(ds, qh, TN) * (1.0 / 16.0)
            dkv_ref[:, vs] += _dot(pe.astype(BF16), doh, TN)

    nq = S // tq
    qspec = pl.BlockSpec((tq, D), lambda b, t: (b * nq + t, 0))
    kvspec = pl.BlockSpec((M, 2 * D), lambda b, t: (b, 0))
    return pl.pallas_call(
        body, grid=(B, nq), in_specs=[qspec, kvspec, qspec], out_specs=[qspec, kvspec],
        out_shape=[SDS((B * S, D), BF16), SDS((B * M, 2 * D), F32)], compiler_params=_cp(), name=name,
    )(q, kv, do)


def _adamw(parts, w, m, v, *, tr, name):
    R, C = w.shape

    def body(p_ref, w_ref, m_ref, v_ref, g_ref, d_ref, nm_ref, nv_ref):
        g = p_ref[0].astype(F32)
        for d in range(1, N_DEV):
            g = g + p_ref[d].astype(F32)
        m2 = ADAM_B1 * m_ref[...] + (1.0 - ADAM_B1) * g
        v2 = ADAM_B2 * v_ref[...] + (1.0 - ADAM_B2) * (g * g)
        m_hat = m2 / (1.0 - ADAM_B1 ** ADAM_STEP)
        v_hat = v2 / (1.0 - ADAM_B2 ** ADAM_STEP)
        g_ref[...] = g
        d_ref[...] = -ADAM_LR * (m_hat / (jnp.sqrt(v_hat) + ADAM_EPS) + ADAM_WD * w_ref[...])
        nm_ref[...] = m2
        nv_ref[...] = v2

    spec = pl.BlockSpec((tr, C), lambda i: (i, 0))
    return pl.pallas_call(
        body, grid=(R // tr,), in_specs=[pl.BlockSpec((N_DEV, tr, C), lambda i: (0, i, 0)), spec, spec, spec],
        out_specs=[spec] * 4, out_shape=[SDS((R, C), F32)] * 4, compiler_params=_cp(), name=name,
    )(parts, w, m, v)


def _peer(k, x, y, c):
    return (1 - x if k & 4 else x, 1 - y if k & 2 else y, 1 - c if k & 1 else c)


_HBM_SPEC = pl.BlockSpec(memory_space=pltpu.HBM)
_SEM_SPEC = pl.BlockSpec(memory_space=pltpu.SEMAPHORE)
_SPLIT_EFFECT = pltpu.SideEffectType.DATAFLOW_SIDE_EFFECTING


def _split_copies(srcs, lands, send_sems, recv_sems, modes):
    x, y, c = (lax.axis_index(a) for a in AXES)
    me = 4 * x + 2 * y + c
    copies = []
    for i, md in enumerate(modes):
        for k in range(1, N_DEV):
            px, py, pc = _peer(k, x, y, c)
            src = srcs[i] if md == "gather" else srcs[i].at[4 * px + 2 * py + pc]
            j = i * (N_DEV - 1) + k - 1
            copies.append(pltpu.make_async_remote_copy(
                src_ref=src, dst_ref=lands[i].at[me], send_sem=send_sems.at[j], recv_sem=recv_sems.at[j],
                device_id=(px, py, pc), device_id_type=pl.DeviceIdType.MESH))
    return copies


def _exchange_start(arrays, modes, *, name):
    n = len(arrays)
    hbm = lambda a: pltpu.with_memory_space_constraint(a, pltpu.HBM)
    srcs = [hbm(a) for a in arrays]
    me = 4 * lax.axis_index("x") + 2 * lax.axis_index("y") + lax.axis_index("c")

    def landing(a, md):
        own = a[None] if md == "gather" else lax.dynamic_index_in_dim(a, me, 0, keepdims=True)
        return hbm(lax.dynamic_update_index_in_dim(lax.empty((N_DEV,) + own.shape[1:], a.dtype), own, me, 0))

    lands = [landing(a, md) for a, md in zip(arrays, modes)]

    def body(*refs):
        for cp in _split_copies(refs[:n], refs[n:2 * n], refs[2 * n], refs[2 * n + 1], modes):
            cp.start()
        token = refs[-1]
        token[...] = jnp.zeros_like(token)

    sems = pltpu.SemaphoreType.DMA((n * (N_DEV - 1),))
    outs = pl.pallas_call(
        body, name=name, in_specs=[_HBM_SPEC] * (2 * n),
        out_shape=(sems, sems, *[pltpu.HBM(a.shape, a.dtype) for a in srcs + lands], SDS((8, 128), F32)),
        out_specs=(_SEM_SPEC, _SEM_SPEC, *[_HBM_SPEC] * (2 * n), pl.BlockSpec(memory_space=pltpu.VMEM)),
        input_output_aliases={i: 2 + i for i in range(2 * n)},
        compiler_params=pltpu.CompilerParams(has_side_effects=_SPLIT_EFFECT),
    )(*srcs, *lands)
    return (outs[0], outs[1], outs[2:2 + n], outs[2 + n:2 + 2 * n], modes), outs[-1]


def _exchange_wait(handle, after, *, name):
    send_sems, recv_sems, srcs, lands, modes = handle
    n = len(srcs)

    def body(*refs):
        for cp in _split_copies(refs[:n], refs[n:2 * n], refs[2 * n], refs[2 * n + 1], modes):
            cp.wait_send()
            cp.wait_recv()

    outs = pl.pallas_call(
        body, name=name, in_specs=[_HBM_SPEC] * (2 * n) + [_SEM_SPEC, _SEM_SPEC, pl.BlockSpec(memory_space=pl.ANY)],
        out_shape=tuple(pltpu.HBM(a.shape, a.dtype) for a in list(srcs) + list(lands)), out_specs=tuple([_HBM_SPEC] * (2 * n)),
        input_output_aliases={i: i for i in range(2 * n)},
        compiler_params=pltpu.CompilerParams(has_side_effects=_SPLIT_EFFECT),
    )(*srcs, *lands, send_sems, recv_sems, after)
    return list(outs[n:])


def _exchange(arrays, modes, *, name):
    n = len(arrays)
    out_shape = [SDS((N_DEV,) + a.shape if md == "gather" else a.shape, a.dtype) for a, md in zip(arrays, modes)]

    def body(*refs):
        ins, outs = refs[:n], refs[n:2 * n]
        send_sems, recv_sems, local_sems = refs[2 * n:]
        x, y, c = (lax.axis_index(a) for a in AXES)
        me = 4 * x + 2 * y + c
        copies = []
        for i, md in enumerate(modes):
            src = ins[i] if md == "gather" else ins[i].at[me]
            cp = pltpu.make_async_copy(src, outs[i].at[me], local_sems.at[i])
            cp.start()
            copies.append(cp)
            for k in range(1, N_DEV):
                px, py, pc = _peer(k, x, y, c)
                src = ins[i] if md == "gather" else ins[i].at[4 * px + 2 * py + pc]
                cp = pltpu.make_async_remote_copy(
                    src_ref=src, dst_ref=outs[i].at[me], send_sem=send_sems.at[i, k - 1], recv_sem=recv_sems.at[i, k - 1],
                    device_id=(px, py, pc), device_id_type=pl.DeviceIdType.MESH)
                cp.start()
                copies.append(cp)
        for cp in copies:
            cp.wait()

    anyspec = pl.BlockSpec(memory_space=pl.ANY)
    return pl.pallas_call(
        body, in_specs=[anyspec] * n, out_specs=[anyspec] * n, out_shape=out_shape,
        scratch_shapes=[pltpu.SemaphoreType.DMA((n, N_DEV - 1)), pltpu.SemaphoreType.DMA((n, N_DEV - 1)),
                        pltpu.SemaphoreType.DMA((n,))],
        name=name,
    )(*arrays)


def _local_step(x, mem, g_mix, b_forget, g_xattn, g_mem, g_mlp, g_final, target, get_w_in, get_rest, send):
    B, S, D = x.shape
    M = mem.shape[1]
    T = B * S
    x0 = x.reshape(T, D)
    mem2 = mem.reshape(B * M, D)
    tgt = target.reshape(T, D)
    b_pad = jnp.pad(b_forget, (0, 120)).reshape(1, 128)
    after = lambda a, tok: a if tok is None else a + tok[0, 0]

    h1 = _rms(x0, g_mix, tm=1024, name="f_norm")
    w_in_pad = get_w_in(h1)
    _, z, gate = _rms_matmul(h1, g_mix, w_in_pad[:, :QKV_W], tm=ROWS, tn=QKV_W, out_dtype=BF16,
                             w_f32=w_in_pad[:, QKV_W:], normed=True, name="f_in")
    cc = _gate_fwd(gate, b_pad, B=B, S=S, name="f_gatecum")
    ya, lse = _dil_attn_fwd(z, B=B, S=S, name="f_dil")
    yf, lse_f = _fox_fwd(z, cc, B=B, S=S, name="f_fox")
    ymix = jnp.concatenate([ya, yf], axis=1)
    w = get_rest(ymix)
    x1, h2, q = _res_rms_matmul(ymix, w["w_out"], x0, g_xattn, w["w_xq"], tm=ROWS, name="f_out")
    mn, kv = _rms_matmul(mem2, g_mem, w["w_kv"], tm=B * M, tn=D, out_dtype=BF16, name="f_xkv")
    xo = _xattn_fwd(q, kv, B=B, S=S, M=M, tq=1024, name="f_xattn")
    x2, h3, act = _res_rms_matmul(xo, w["w_xo"], x1, g_mlp, w["w_up"], tm=ROWS, relu=True, name="f_xo")
    dx3, dg_final, loss = _down_loss(act, w["w_down"], x2, g_final, tgt, tm=ROWS, name="f_down")

    du = _matmul_nt(dx3, w["w_down"], mul2a=act, tm=ROWS, tn=D_FF, name="b_dact")
    dw_down = _matmul_tn(act, dx3, square=True, bk=1024, bn=D, tt=ACC_ROWS, out_dtype=BF16, name="b_wdown")
    dw_up = _matmul_tn(h3, du, bk=D, bn=1024, tt=min(T, 2 * ACC_ROWS), out_dtype=BF16, name="b_wup")
    tok = send(dict(w_down=dw_down, w_up=dw_up))
    dx2, dg_mlp, dxo = _matmul_nt_rms(du, w["w_up"], x2, after(g_mlp, tok), dx3, then_w=w["w_xo"], tm=ROWS, tk=D_FF,
                                      name="b_dh3")
    dw_xo = _matmul_tn(xo, dx2, bk=D, bn=D, tt=ACC_ROWS, out_dtype=BF16, name="b_wxo")
    dq, dkv = _xattn_bwd(q, kv, dxo, B=B, S=S, M=M, tq=1024, name="b_xattn")
    dw_xq = _matmul_tn(h2, dq, bk=D, bn=D, tt=min(T, 2 * ACC_ROWS), out_dtype=BF16, name="b_wxq")
    dx1, dg_xattn, dy = _matmul_nt_rms(dq, w["w_xq"], x1, g_xattn, dx2, then_w=w["w_out"], tm=ROWS, tk=D, name="b_dh2")
    dw_kv = _matmul_tn(mn, dkv, bk=D, bn=D, tt=B * M, out_dtype=BF16, name="b_wkv")
    _, dg_mem = _matmul_nt_rms(dkv, w["w_kv"], mem2, g_mem, None, tm=min(ROWS, B * M), tk=2 * D, name="b_dmem")
    dw_out = _matmul_tn(ymix, dx1, bk=D, bn=D, tt=ACC_ROWS, out_dtype=BF16, name="b_wout")
    tok = send(dict(w_xo=dw_xo, w_xq=dw_xq, w_xk=dw_kv[:, :D], w_xv=dw_kv[:, D:], w_out=dw_out))
    dqf, dkf, dvf, dcc = _fox_bwd(z, dy, lse_f, cc, B=B, S=S, name="b_fox")
    dgate, db = _gate_bwd(dcc, gate, after(b_pad, tok), B=B, S=S, name="b_gate")
    dqa, dka, dva = _dil_attn_bwd(z, dy, ya, lse, B=B, S=S, name="b_dil")
    dz = [dqa, dka, dva, dqf, dkf, dvf, dgate]
    dw_in = jnp.concatenate([_matmul_tn_pieces(h1, dz[:3], tt=ACC_ROWS, name="b_win_dil"),
                             _matmul_tn_pieces(h1, dz[3:], tt=ACC_ROWS, name="b_win_fox")], axis=1)
    tok = send(dict(w_in=dw_in))
    gx, dg_mix = _matmul_nt_rms(dz, w_in_pad, x0, after(g_mix, tok), dx1, tm=ROWS, tk=IN_PAD, name="b_dh1")

    small = dict(g_mix=dg_mix, b_forget=db, g_xattn=dg_xattn, g_mem=dg_mem, g_mlp=dg_mlp, g_final=dg_final)
    return gx.reshape(B, S, D), small, loss


SMALL_ROWS = ("g_mix", "b_forget", "g_xattn", "g_mem", "g_mlp", "g_final")
COL_SHARDED = ("w_in", "w_up")


def _pack_rows(rows):
    D = D_MODEL
    rows = [jnp.pad(r.reshape(-1), (0, D - r.size)) for r in rows]
    rows += [jnp.zeros((D,), F32)] * (8 - len(rows))
    return jnp.stack(rows)


def _full(name, g):
    if name in COL_SHARDED:
        return g.transpose(1, 0, 2).reshape(g.shape[1], -1)
    return g.reshape(-1, g.shape[2])


def _blocks(name, g, shard_shape):
    if name in COL_SHARDED:
        n = shard_shape[1]
        return g[:, :n * N_DEV].reshape(g.shape[0], N_DEV, n).transpose(1, 0, 2)
    return g.reshape((N_DEV,) + shard_shape)


def kernel(x, mem, g_mix, w_in, b_forget, w_out, g_xattn, g_mem, w_xq, w_xk, w_xv, w_xo, g_mlp, w_up, w_down, g_final, loss_target, m_g_mix, m_w_in, m_b_forget, m_w_out, m_g_xattn, m_g_mem, m_w_xq, m_w_xk, m_w_xv, m_w_xo, m_g_mlp, m_w_up, m_w_down, m_g_final, v_g_mix, v_w_in, v_b_forget, v_w_out, v_g_xattn, v_g_mem, v_w_xq, v_w_xk, v_w_xv, v_w_xo, v_g_mlp, v_w_up, v_w_down, v_g_final):
    W = dict(w_in=w_in, w_out=w_out, w_xq=w_xq, w_xk=w_xk, w_xv=w_xv, w_xo=w_xo, w_up=w_up, w_down=w_down)
    Mo = dict(w_in=m_w_in, w_out=m_w_out, w_xq=m_w_xq, w_xk=m_w_xk, w_xv=m_w_xv, w_xo=m_w_xo, w_up=m_w_up, w_down=m_w_down)
    Vo = dict(w_in=v_w_in, w_out=v_w_out, w_xq=v_w_xq, w_xk=v_w_xk, w_xv=v_w_xv, w_xo=v_w_xo, w_up=v_w_up, w_down=v_w_down)
    later = [n for n in W if n != "w_in"]

    first_handle, first_token = _exchange_start([w_in.astype(BF16)], ["gather"], name="gather_in_start")
    rest_handle, rest_token = _exchange_start([W[n].astype(BF16) + first_token[0, 0].astype(BF16) for n in later],
                                              ["gather"] * len(later), name="gather_rest_start")

    def get_w_in(after):
        (g,) = _exchange_wait(first_handle, after, name="gather_in_wait")
        return jnp.pad(_full("w_in", g), ((0, 0), (0, IN_PAD - IN_W)))

    def get_rest(after):
        full = {n: _full(n, g) for n, g in zip(later, _exchange_wait(rest_handle, after, name="gather_rest_wait"))}
        full["w_kv"] = jnp.concatenate([full.pop("w_xk"), full.pop("w_xv")], axis=1)
        return full

    sent = []

    def send(grads):
        names = list(grads)
        handle, token = _exchange_start([_blocks(n, grads[n], W[n].shape) for n in names], ["scatter"] * len(names),
                                        name=f"scatter{len(sent)}_start")
        sent.append((names, handle))
        return token

    gx, small, loss = _local_step(x, mem, g_mix + rest_token[0, 0], b_forget, g_xattn, g_mem, g_mlp, g_final, loss_target,
                                  get_w_in, get_rest, send)

    received = {}
    for i, (names, handle) in enumerate(sent):
        received.update(zip(names, _exchange_wait(handle, gx, name=f"scatter{i}_wait")))
    packed = _pack_rows([small[n] for n in SMALL_ROWS] + [loss[0, :1]])
    (packed_all,) = _exchange([packed], ["gather"], name="gather_small")

    rows_per_step = lambda shape: max(t for t in (128, 256, 512) if shape[0] % t == 0 and t * shape[1] <= 512 * 512)
    res = {n: _adamw(received[n], W[n], Mo[n], Vo[n], tr=rows_per_step(W[n].shape), name=f"adamw_{n}") for n in W}
    small_w = dict(g_mix=g_mix, b_forget=b_forget, g_xattn=g_xattn, g_mem=g_mem, g_mlp=g_mlp, g_final=g_final)
    small_m = dict(g_mix=m_g_mix, b_forget=m_b_forget, g_xattn=m_g_xattn, g_mem=m_g_mem, g_mlp=m_g_mlp, g_final=m_g_final)
    small_v = dict(g_mix=v_g_mix, b_forget=v_b_forget, g_xattn=v_g_xattn, g_mem=v_g_mem, g_mlp=v_g_mlp, g_final=v_g_final)
    sres = _adamw(packed_all, _pack_rows([small_w[n] for n in SMALL_ROWS]), _pack_rows([small_m[n] for n in SMALL_ROWS]),
                  _pack_rows([small_v[n] for n in SMALL_ROWS]), tr=8, name="adamw_small")
    for i, n in enumerate(SMALL_ROWS):
        res[n] = [r[i, :small_w[n].size] for r in sres]
    loss_total = sres[0][6, 0]

    order = ["g_mix", "w_in", "b_forget", "w_out", "g_xattn", "g_mem", "w_xq", "w_xk", "w_xv", "w_xo", "g_mlp", "w_up", "w_down", "g_final"]
    return (loss_total, gx, *[res[n][0] for n in order], *[res[n][1] for n in order],
            *[res[n][2] for n in order], *[res[n][3] for n in order])
```

```python
import jax
import jax.numpy as jnp
from jax import lax
from jax.experimental import pallas as pl
from jax.experimental.pallas import tpu as pltpu

F32, BF16 = jnp.float32, jnp.bfloat16
SDS = jax.ShapeDtypeStruct

D_MODEL = 1024
HEAD_DIM = 64
WIDTH = 512
QKV_W = 6 * WIDTH
IN_W = QKV_W + 8
IN_PAD = QKV_W + 128
BLOCK = 128
DIL_CONFIGS = ((128, 1), (512, 4), (2048, 16))
N_XH, XHD = 4, 256
D_FF = 4096
EPS = 1e-6
NEG = -1e30
N_DEV = 8
AXES = ("x", "y", "c")

ADAM_LR, ADAM_B1, ADAM_B2, ADAM_EPS, ADAM_WD, ADAM_STEP = 0.001, 0.9, 0.999, 1e-08, 0.01, 10

VMEM_CAP_V7X = 64 * 1024 * 1024
VMEM_LIMIT = VMEM_CAP_V7X * 7 // 8

ROWS = 512
ACC_ROWS = 2048

NT = (((1,), (1,)), ((), ()))
TN = (((0,), (0,)), ((), ()))


def _cp(**kw):
    return pltpu.CompilerParams(vmem_limit_bytes=VMEM_LIMIT, **kw)


def _dot(a, b, dims=None):
    if dims is None:
        return jnp.dot(a, b, preferred_element_type=F32)
    return lax.dot_general(a, b, dims, preferred_element_type=F32)


def _rstd(xv):
    return lax.rsqrt(jnp.mean(xv * xv, axis=-1, keepdims=True) + EPS)


def _rms_bwd(dh, xv, g):
    r = _rstd(xv)
    xhat = xv * r
    dxhat = dh * g
    dx = r * (dxhat - xhat * jnp.mean(dxhat * xhat, axis=-1, keepdims=True))
    return dx, jnp.sum(dh * xhat, axis=0, keepdims=True)


def _rms_matmul(x, g, w, *, tm, tn, out_dtype, relu=False, w_f32=None, normed=False, name):
    T, D = x.shape
    N = w.shape[1]

    def body(*refs):
        x_ref, g_ref, w_ref = refs[:3]
        outs, h_s = refs[3 + (w_f32 is not None):-1], refs[-1]
        o_ref = outs[0 if normed else 1]

        @pl.when(pl.program_id(1) == 0)
        def _():
            xv = x_ref[...]
            h = xv if normed else (xv * _rstd(xv) * g_ref[...]).astype(BF16)
            h_s[...] = h
            if not normed:
                outs[0][...] = h
            if w_f32 is not None:
                outs[-1][...] = _dot(h, refs[3][...])

        acc = _dot(h_s[...], w_ref[...])
        if relu:
            acc = jnp.maximum(acc, 0.0)
        o_ref[...] = acc.astype(out_dtype)

    in_specs = [pl.BlockSpec((tm, D), lambda i, j: (i, 0)), pl.BlockSpec((1, D), lambda i, j: (0, 0)),
                pl.BlockSpec((D, tn), lambda i, j: (0, j))]
    out_specs = [pl.BlockSpec((tm, D), lambda i, j: (i, 0)), pl.BlockSpec((tm, tn), lambda i, j: (i, j))]
    out_shape = [SDS((T, D), BF16), SDS((T, N), out_dtype)]
    args = [x, g.reshape(1, D), w]
    if normed:
        out_specs, out_shape = out_specs[1:], out_shape[1:]
    if w_f32 is not None:
        n2 = w_f32.shape[1]
        in_specs.append(pl.BlockSpec((D, n2), lambda i, j: (0, 0)))
        out_specs.append(pl.BlockSpec((tm, n2), lambda i, j: (i, 0)))
        out_shape.append(SDS((T, n2), F32))
        args.append(w_f32)
    res = pl.pallas_call(
        body, grid=(T // tm, N // tn), in_specs=in_specs, out_specs=out_specs, out_shape=out_shape,
        scratch_shapes=[pltpu.VMEM((tm, D), BF16)], compiler_params=_cp(), name=name,
    )(*args)
    return [x] + list(res) if normed else res


def _rms(x, g, *, tm, name):
    T, D = x.shape

    def body(x_ref, g_ref, h_ref):
        xv = x_ref[...]
        h_ref[...] = (xv * _rstd(xv) * g_ref[...]).astype(BF16)

    rows = pl.BlockSpec((tm, D), lambda i: (i, 0))
    return pl.pallas_call(body, grid=(T // tm,), in_specs=[rows, pl.BlockSpec((1, D), lambda i: (0, 0))], out_specs=rows,
                          out_shape=SDS((T, D), BF16), compiler_params=_cp(), name=name)(x, g.reshape(1, D))


def _res_rms_matmul(a, w1, res, gain, w2, *, tm, relu=False, name):
    T, K = a.shape
    D, N = w2.shape

    def body(a_ref, w1_ref, res_ref, g_ref, w2_ref, x_ref, h_ref, o_ref):
        xv = res_ref[...] + _dot(a_ref[...], w1_ref[...])
        x_ref[...] = xv
        h = (xv * _rstd(xv) * g_ref[...]).astype(BF16)
        h_ref[...] = h
        acc = _dot(h, w2_ref[...])
        if relu:
            acc = jnp.maximum(acc, 0.0)
        o_ref[...] = acc.astype(BF16)

    rows = lambda n: pl.BlockSpec((tm, n), lambda i: (i, 0))
    whole = lambda r, c: pl.BlockSpec((r, c), lambda i: (0, 0))
    return pl.pallas_call(
        body, grid=(T // tm,), in_specs=[rows(K), whole(K, D), rows(D), whole(1, D), whole(D, N)],
        out_specs=[rows(D), rows(D), rows(N)], out_shape=[SDS((T, D), F32), SDS((T, D), BF16), SDS((T, N), BF16)],
        compiler_params=_cp(), name=name,
    )(a, w1, res, gain.reshape(1, D), w2)


def _matmul_nt(g, w, *, mul2a=None, tm, tn, name):
    T, K = g.shape
    N = w.shape[0]

    def body(*refs):
        g_ref, w_ref = refs[0], refs[1]
        o_ref = refs[-1]
        acc = _dot(g_ref[...].astype(BF16), w_ref[...], NT)
        if mul2a is not None:
            acc = acc * (2.0 * refs[2][...].astype(F32))
        o_ref[...] = acc.astype(BF16)

    in_specs = [pl.BlockSpec((tm, K), lambda i, j: (i, 0)), pl.BlockSpec((tn, K), lambda i, j: (j, 0))]
    args = [g, w]
    if mul2a is not None:
        in_specs.append(pl.BlockSpec((tm, tn), lambda i, j: (i, j)))
        args.append(mul2a)
    return pl.pallas_call(
        body, grid=(T // tm, N // tn), in_specs=in_specs,
        out_specs=pl.BlockSpec((tm, tn), lambda i, j: (i, j)), out_shape=SDS((T, N), BF16),
        compiler_params=_cp(), name=name,
    )(*args)


def _matmul_nt_rms(g, w, x, gain, dres, *, then_w=None, tm, tk, name):
    pieces = list(g) if isinstance(g, (list, tuple)) else [g]
    widths = [p.shape[1] for p in pieces]
    T, K = pieces[0].shape[0], sum(widths)
    D = w.shape[0]
    nk = K // tk
    nt = T // tm
    npc = len(pieces)
    assert npc == 1 or nk == 1
    n_in = npc + 3 + (dres is not None) + (then_w is not None)

    def body(*refs):
        w_ref, x_ref, gain_ref = refs[npc:npc + 3]
        dres_ref = refs[npc + 3] if dres is not None else None
        then_ref = refs[n_in - 1] if then_w is not None else None
        dx_ref, dg_ref = refs[n_in], refs[n_in + 1]
        i, k = pl.program_id(0), pl.program_id(1)
        if npc == 1:
            part = _dot(refs[0][...].astype(BF16), w_ref[...], NT)
        else:
            part, off = None, 0
            for j in range(npc):
                d = _dot(refs[j][...].astype(BF16), w_ref[:, off:off + widths[j]], NT)
                part = d if part is None else part + d
                off += widths[j]

        def finish(dh):
            dx, dg = _rms_bwd(dh, x_ref[...], gain_ref[...])
            if dres_ref is not None:
                dx = dres_ref[...] + dx
            dx_ref[...] = dx
            if then_ref is not None:
                refs[n_in + 2][...] = _dot(dx.astype(BF16), then_ref[...], NT).astype(BF16)

            @pl.when(i == 0)
            def _():
                dg_ref[...] = dg

            @pl.when(i > 0)
            def _():
                dg_ref[...] += dg

        if nk == 1:
            finish(part)
        else:
            acc = refs[-1]

            @pl.when(k == 0)
            def _():
                acc[...] = part

            @pl.when(k > 0)
            def _():
                acc[...] += part

            @pl.when(k == nk - 1)
            def _():
                finish(acc[...])

    g_specs = ([pl.BlockSpec((tm, tk), lambda i, k: (i, k))] if npc == 1 else
               [pl.BlockSpec((tm, wd), lambda i, k: (i, 0)) for wd in widths])
    in_specs = g_specs + [pl.BlockSpec((D, tk), lambda i, k: (0, k)),
                          pl.BlockSpec((tm, D), lambda i, k: (i, 0)), pl.BlockSpec((1, D), lambda i, k: (0, 0))]
    args = pieces + [w, x, gain.reshape(1, D)]
    out_specs = [pl.BlockSpec((tm, D), lambda i, k: (i, 0)), pl.BlockSpec((1, D), lambda i, k: (0, 0))]
    out_shape = [SDS((T, D), F32), SDS((1, D), F32)]
    if dres is not None:
        in_specs.append(pl.BlockSpec((tm, D), lambda i, k: (i, 0)))
        args.append(dres)
    if then_w is not None:
        n2 = then_w.shape[0]
        in_specs.append(pl.BlockSpec((n2, D), lambda i, k: (0, 0)))
        args.append(then_w)
        out_specs.append(pl.BlockSpec((tm, n2), lambda i, k: (i, 0)))
        out_shape.append(SDS((T, n2), BF16))
    return pl.pallas_call(
        body, grid=(nt, nk), in_specs=in_specs, out_specs=out_specs, out_shape=out_shape,
        scratch_shapes=[pltpu.VMEM((tm, D), F32)] if nk > 1 else [], compiler_params=_cp(), name=name,
    )(*args)


def _matmul_tn(a, g, *, square=False, bk, bn, tt, out_dtype, name):
    T, K = a.shape
    N = g.shape[1]
    nt = T // tt

    def body(a_ref, g_ref, o_ref, acc):
        t = pl.program_id(2)
        av = a_ref[...]
        if square:
            af = av.astype(F32)
            av = (af * af).astype(BF16)
        part = _dot(av, g_ref[...].astype(BF16), TN)
        if nt == 1:
            o_ref[...] = part.astype(out_dtype)
        else:
            @pl.when(t == 0)
            def _():
                acc[...] = part

            @pl.when((t > 0) & (t < nt - 1))
            def _():
                acc[...] += part

            @pl.when(t == nt - 1)
            def _():
                o_ref[...] = (acc[...] + part).astype(out_dtype)

    return pl.pallas_call(
        body, grid=(K // bk, N // bn, nt),
        in_specs=[pl.BlockSpec((tt, bk), lambda i, j, t: (t, i)), pl.BlockSpec((tt, bn), lambda i, j, t: (t, j))],
        out_specs=pl.BlockSpec((bk, bn), lambda i, j, t: (i, j)), out_shape=SDS((K, N), out_dtype),
        scratch_shapes=[pltpu.VMEM((bk, bn), F32)], compiler_params=_cp(), name=name,
    )(a, g)


def _matmul_tn_pieces(a, pieces, *, tt, name):
    T, K = a.shape
    widths = [p.shape[1] for p in pieces]
    W = sum(widths)
    nt = T // tt
    n = len(pieces)

    def body(*refs):
        a_ref, o_ref, acc = refs[0], refs[n + 1], refs[n + 2]
        t = pl.program_id(0)
        av = a_ref[...]
        off = 0
        for j in range(n):
            cols = slice(off, off + widths[j])
            part = _dot(av, refs[1 + j][...], TN)

            @pl.when(t == 0)
            def _():
                acc[:, cols] = part

            @pl.when(t > 0)
            def _():
                acc[:, cols] += part

            off += widths[j]

        @pl.when(t == nt - 1)
        def _():
            o_ref[...] = acc[...].astype(BF16)

    return pl.pallas_call(
        body, grid=(nt,),
        in_specs=[pl.BlockSpec((tt, K), lambda t: (t, 0))] + [pl.BlockSpec((tt, w), lambda t: (t, 0)) for w in widths],
        out_specs=pl.BlockSpec((K, W), lambda t: (0, 0)), out_shape=SDS((K, W), BF16),
        scratch_shapes=[pltpu.VMEM((K, W), F32)], compiler_params=_cp(), name=name,
    )(a, *pieces)


def _down_loss(act, w_down, x2, g_final, target, *, tm, name):
    T, D = x2.shape
    F = act.shape[1]

    def body(a_ref, w_ref, x2_ref, g_ref, t_ref, dx_ref, dxb_ref, dg_ref, loss_ref):
        i = pl.program_id(0)
        af = a_ref[...].astype(F32)
        xv, g = x2_ref[...] + _dot((af * af).astype(BF16), w_ref[...]), g_ref[...]
        r = _rstd(xv)
        xhat = xv * r
        diff = xhat * g - t_ref[...]
        part = 0.5 * jnp.sum(jnp.mean(diff * diff, axis=-1, keepdims=True), axis=0, keepdims=True)
        dy = diff * (1.0 / D)
        dxhat = dy * g
        dx = r * (dxhat - xhat * jnp.mean(dxhat * xhat, axis=-1, keepdims=True))
        dx_ref[...] = dx
        dxb_ref[...] = dx.astype(BF16)
        dg = jnp.sum(dy * xhat, axis=0, keepdims=True)
        lp = jnp.broadcast_to(part, loss_ref.shape)

        @pl.when(i == 0)
        def _():
            dg_ref[...] = dg
            loss_ref[...] = lp

        @pl.when(i > 0)
        def _():
            dg_ref[...] += dg
            loss_ref[...] += lp

    rows = pl.BlockSpec((tm, D), lambda i: (i, 0))
    return pl.pallas_call(
        body, grid=(T // tm,),
        in_specs=[pl.BlockSpec((tm, F), lambda i: (i, 0)), pl.BlockSpec((F, D), lambda i: (0, 0)), rows,
                  pl.BlockSpec((1, D), lambda i: (0, 0)), rows],
        out_specs=[rows, rows, pl.BlockSpec((1, D), lambda i: (0, 0)), pl.BlockSpec((8, 128), lambda i: (0, 0))],
        out_shape=[SDS((T, D), F32), SDS((T, D), BF16), SDS((1, D), F32), SDS((8, 128), F32)],
        compiler_params=_cp(), name=name,
    )(act, w_down, x2, g_final.reshape(1, D), target)


def _head_lanes(shape, width):
    return lax.broadcasted_iota(jnp.int32, shape, len(shape) - 1) // width


def _gate_fwd(gate, b_pad, *, B, S, name):
    def body(g_ref, b_ref, cc_ref):
        xv = g_ref[...] + b_ref[...]
        lf = jnp.minimum(xv, 0.0) - jnp.log(1.0 + jnp.exp(-jnp.abs(xv)))
        lane = lax.broadcasted_iota(jnp.int32, lf.shape, 1)
        row = lax.broadcasted_iota(jnp.int32, lf.shape, 0)
        c = jnp.where(lane < 8, lf, 0.0)
        sh = 1
        while sh < S:
            c = c + jnp.where(row >= sh, pltpu.roll(c, sh, 0), 0.0)
            sh *= 2
        grp = _head_lanes((S, WIDTH), HEAD_DIM)
        cc = jnp.zeros((S, WIDTH), F32)
        for h in range(8):
            cc = jnp.where(grp == h, c[:, h:h + 1], cc)
        cc_ref[...] = cc

    return pl.pallas_call(
        body, grid=(B,),
        in_specs=[pl.BlockSpec((S, 128), lambda b: (b, 0)), pl.BlockSpec((1, 128), lambda b: (0, 0))],
        out_specs=pl.BlockSpec((S, WIDTH), lambda b: (b, 0)), out_shape=SDS((B * S, WIDTH), F32),
        compiler_params=_cp(), name=name,
    )(gate, b_pad)


def _gate_bwd(dcc, gate, b_pad, *, B, S, name):
    def body(dcc_ref, g_ref, b_ref, dg_ref, db_ref):
        bi = pl.program_id(0)
        dccv = dcc_ref[...]
        lane = lax.broadcasted_iota(jnp.int32, (S, 128), 1)
        row = lax.broadcasted_iota(jnp.int32, (S, 128), 0)
        dc = jnp.zeros((S, 128), F32)
        for h in range(8):
            dc = jnp.where(lane == h, dccv[:, HEAD_DIM * h:HEAD_DIM * h + 1], dc)
        sh = 1
        while sh < S:
            dc = dc + jnp.where(row < S - sh, pltpu.roll(dc, S - sh, 0), 0.0)
            sh *= 2
        xv = g_ref[...] + b_ref[...]
        dgate = jnp.where(lane < 8, dc / (1.0 + jnp.exp(xv)), 0.0)
        dg_ref[...] = dgate.astype(BF16)
        db = jnp.sum(dgate, axis=0, keepdims=True)

        @pl.when(bi == 0)
        def _():
            db_ref[...] = db

        @pl.when(bi > 0)
        def _():
            db_ref[...] += db

    return pl.pallas_call(
        body, grid=(B,),
        in_specs=[pl.BlockSpec((S, WIDTH), lambda b: (b, 0)), pl.BlockSpec((S, 128), lambda b: (b, 0)),
                  pl.BlockSpec((1, 128), lambda b: (0, 0))],
        out_specs=[pl.BlockSpec((S, 128), lambda b: (b, 0)), pl.BlockSpec((1, 128), lambda b: (0, 0))],
        out_shape=[SDS((B * S, 128), BF16), SDS((1, 128), F32)],
        compiler_params=_cp(), name=name,
    )(dcc, gate, b_pad)


_SMEM_SPEC = pl.BlockSpec(memory_space=pltpu.SMEM)


def _alibi_slopes():
    return 2.0 ** (-(jnp.arange(1, 9, dtype=F32) * (8.0 / 8)))


def _pair_masks():
    lane = lax.broadcasted_iota(jnp.int32, (1, 128), 1)
    first = lane < HEAD_DIM
    return (first.astype(BF16), (~first).astype(BF16)), first


BNT =(((2,), (2,)), ((0,), (0,)))
BNN = (((2,), (1,)), ((0,), (0,)))
BTN = (((1,), (1,)), ((0,), (0,)))


def _split3(v):
    hi = v.astype(BF16).astype(F32)
    mid = (v - hi).astype(BF16).astype(F32)
    lo = (v - hi - mid).astype(BF16).astype(F32)
    return [hi, mid, lo]


def _with_spare_lanes(base, e, cols):
    lane = lax.broadcasted_iota(jnp.int32, (1, 128), 1)
    off = HEAD_DIM * (1 - e)
    extra = jnp.zeros(base.shape, F32)
    for j, c in enumerate(cols):
        extra = jnp.where(lane == off + j, c, extra)
    return base + extra.astype(BF16)


ONES3 = [1.0, 1.0, 1.0]


def _band_bias(slope, dilation):
    qi = lax.broadcasted_iota(jnp.int32, (BLOCK, BLOCK), 0)
    kj = lax.broadcasted_iota(jnp.int32, (BLOCK, BLOCK), 1)
    cur = jnp.where(kj <= qi, (-slope * dilation) * (qi - kj).astype(F32), NEG)
    prev = jnp.where(kj >= qi, (-slope * dilation) * (qi + BLOCK - kj).astype(F32), NEG)
    return cur, prev


def _to_residue_major(dst, src_f32, dilation, nb, lead=0, src=None):
    L = nb * BLOCK
    if dilation == 1 and src is not None:
        dst[lead:lead + nb] = src[...].reshape(nb, BLOCK, 128)
        return
    for r in range(dilation):
        rows = src_f32[pl.ds(r, L, stride=dilation), :] if dilation > 1 else src_f32[...]
        dst[lead + r * nb:lead + (r + 1) * nb] = rows.reshape(nb, BLOCK, 128).astype(dst.dtype)


def _dil_attn_fwd(z, *, B, S, name):
    NB = S // BLOCK

    def body(slope_ref, q_ref, k_ref, v_ref, y_ref, lse_ref, qf, kf, vf, qd, kd, vd, od, ld, acc_o, acc_l):
        (m_first, m_second), first = _pair_masks()
        p = pl.program_id(1)
        qf[...] = q_ref[...].astype(F32)
        kf[...] = k_ref[...].astype(F32)
        vf[...] = v_ref[...].astype(F32)
        kd[0] = jnp.zeros((BLOCK, 128), BF16)
        vd[0] = jnp.zeros((BLOCK, 128), BF16)
        blk = lax.broadcasted_iota(jnp.int32, (NB, 1, 1), 0)

        for idx, (_, dilation) in enumerate(DIL_CONFIGS):
            nb = NB // dilation
            _to_residue_major(qd, qf, dilation, nb, src=q_ref)
            _to_residue_major(kd, kf, dilation, nb, lead=1, src=k_ref)
            _to_residue_major(vd, vf, dilation, nb, lead=1, src=v_ref)
            q4, kc, vc = qd[...], kd[1:NB + 1], vd[1:NB + 1]
            outs, lses = [], []
            for e, hm in enumerate((m_first, m_second)):
                bias_cur, bias_prev = _band_bias(slope_ref[2 * p + e], dilation)
                qm = q4 * hm
                sc = _dot(qm, kc, BNT) * 0.125 + bias_cur
                m = jnp.max(sc, axis=2, keepdims=True)
                if nb > 1:
                    sp = _dot(qm, kd[0:NB], BNT) * 0.125 + jnp.where(blk % nb == 0, NEG, bias_prev)
                    m = jnp.maximum(m, jnp.max(sp, axis=2, keepdims=True))
                pc = jnp.exp(sc - m)
                l = jnp.sum(pc, axis=2, keepdims=True)
                o = _dot(pc.astype(BF16), vc, BNN)
                if nb > 1:
                    pp = jnp.exp(sp - m)
                    l = l + jnp.sum(pp, axis=2, keepdims=True)
                    o = o + _dot(pp.astype(BF16), vd[0:NB], BNN)
                outs.append(o * (1.0 / l))
                lses.append(m + jnp.log(l))
            od[...] = jnp.where(first, outs[0], outs[1])
            ld[...] = jnp.where(first, lses[0], lses[1])

            L = nb * BLOCK
            for r in range(dilation):
                rows = pl.ds(r, L, stride=dilation) if dilation > 1 else slice(None)
                o_new = od[r * nb:(r + 1) * nb].reshape(L, 128)
                l_new = ld[r * nb:(r + 1) * nb].reshape(L, 128)
                if idx == 0:
                    acc_o[rows, :] = o_new
                    acc_l[rows, :] = l_new
                else:
                    l_old = acc_l[rows, :]
                    m2 = jnp.maximum(l_old, l_new)
                    w_old, w_new = jnp.exp(l_old - m2), jnp.exp(l_new - m2)
                    tot = w_old + w_new
                    acc_o[rows, :] = (w_old * acc_o[rows, :] + w_new * o_new) * (1.0 / tot)
                    acc_l[rows, :] = m2 + jnp.log(tot)

        y_ref[...] = acc_o[...].astype(BF16)
        lse_ref[...] = acc_l[...]

    spec = lambda off: pl.BlockSpec((S, 128), lambda b, p: (b, 4 * off + p))
    ospec = pl.BlockSpec((S, 128), lambda b, p: (b, p))
    blocks = lambda n, dt: pltpu.VMEM((n, BLOCK, 128), dt)
    return pl.pallas_call(
        body, grid=(B, 4), in_specs=[_SMEM_SPEC, spec(0), spec(1), spec(2)], out_specs=[ospec, ospec],
        out_shape=[SDS((B * S, WIDTH), BF16), SDS((B * S, WIDTH), F32)],
        scratch_shapes=[pltpu.VMEM((S, 128), F32)] * 3 + [blocks(NB, BF16), blocks(NB + 1, BF16), blocks(NB + 1, BF16),
                                                         blocks(NB, F32), blocks(NB, F32)] + [pltpu.VMEM((S, 128), F32)] * 2,
        compiler_params=_cp(), name=name,
    )(_alibi_slopes(), z, z, z)


def _dil_attn_bwd(z, dy, ya, lse, *, B, S, name):
    NB = S // BLOCK

    def body(slope_ref, q_ref, k_ref, v_ref, do_ref, o_ref, lse_ref, dq_ref, dk_ref, dv_ref,
             qf, kf, vf, dof, ef, qd, dod, kd, vd, lsd, dkd, dvd, dqa, dka, dva):
        (m_first, m_second), first = _pair_masks()
        p = pl.program_id(1)
        qf[...] = q_ref[...].astype(F32)
        kf[...] = k_ref[...].astype(F32)
        vf[...] = v_ref[...].astype(F32)
        dov = do_ref[...].astype(F32)
        dof[...] = dov
        prod = dov * o_ref[...].astype(F32)
        rowdot = jnp.where(first, jnp.sum(jnp.where(first, prod, 0.0), axis=1, keepdims=True),
                           jnp.sum(jnp.where(first, 0.0, prod), axis=1, keepdims=True))
        lane = lax.broadcasted_iota(jnp.int32, (1, 128), 1)
        ef[...] = jnp.where(lane % HEAD_DIM < HEAD_DIM // 2, lse_ref[...], rowdot)
        kd[0] = jnp.zeros((BLOCK, 128), BF16)
        vd[0] = jnp.zeros((BLOCK, 128), BF16)
        blk = lax.broadcasted_iota(jnp.int32, (NB, 1, 1), 0)

        for idx, (_, dilation) in enumerate(DIL_CONFIGS):
            nb = NB // dilation
            _to_residue_major(qd, qf, dilation, nb, src=q_ref)
            _to_residue_major(dod, dof, dilation, nb, src=do_ref)
            _to_residue_major(kd, kf, dilation, nb, lead=1, src=k_ref)
            _to_residue_major(vd, vf, dilation, nb, lead=1, src=v_ref)
            _to_residue_major(lsd, ef, dilation, nb)
            stats = lsd[...]
            q4, do4, kc, vc = qd[...], dod[...], kd[1:NB + 1], vd[1:NB + 1]
            dq4 = None
            dkc = dvc = dkp = dvp = None
            for e, hm in enumerate((m_first, m_second)):
                lane0 = slice(HEAD_DIM * e, HEAD_DIM * e + 1)
                bias_cur, bias_prev = _band_bias(slope_ref[2 * p + e], dilation)
                qm, dom = q4 * hm, do4 * hm
                lse_e = stats[:, :, lane0]
                e_e = stats[:, :, HEAD_DIM * e + HEAD_DIM // 2:HEAD_DIM * e + HEAD_DIM // 2 + 1]
                pc = jnp.exp(_dot(qm, kc, BNT) * 0.125 + bias_cur - lse_e)
                dsc = (pc * (_dot(dom, vc, BNT) - e_e)).astype(BF16)
                pcb = pc.astype(BF16)
                dqe = _dot(dsc, kc, BNN)
                dkc = _dot(dsc, qm, BTN) if e == 0 else dkc + _dot(dsc, qm, BTN)
                dvc = _dot(pcb, dom, BTN) if e == 0 else dvc + _dot(pcb, dom, BTN)
                if nb > 1:
                    kp, vp = kd[0:NB], vd[0:NB]
                    pp = jnp.exp(_dot(qm, kp, BNT) * 0.125 + jnp.where(blk % nb == 0, NEG, bias_prev) - lse_e)
                    dsp = (pp * (_dot(dom, vp, BNT) - e_e)).astype(BF16)
                    ppb = pp.astype(BF16)
                    dqe = dqe + _dot(dsp, kp, BNN)
                    dkp = _dot(dsp, qm, BTN) if e == 0 else dkp + _dot(dsp, qm, BTN)
                    dvp = _dot(ppb, dom, BTN) if e == 0 else dvp + _dot(ppb, dom, BTN)
                dq4 = dqe if e == 0 else jnp.where(first, dq4, dqe)

            dkd[1:NB + 1] = dkc
            dvd[1:NB + 1] = dvc
            if nb > 1:
                dkd[1:NB] += dkp[1:NB]
                dvd[1:NB] += dvp[1:NB]
            L = nb * BLOCK
            for r in range(dilation):
                rows = pl.ds(r, L, stride=dilation) if dilation > 1 else slice(None)
                dq_r = dq4[r * nb:(r + 1) * nb].reshape(L, 128) * 0.125
                dk_r = dkd[1 + r * nb:1 + (r + 1) * nb].reshape(L, 128) * 0.125
                dv_r = dvd[1 + r * nb:1 + (r + 1) * nb].reshape(L, 128)
                if idx == 0:
                    dqa[rows, :], dka[rows, :], dva[rows, :] = dq_r, dk_r, dv_r
                else:
                    dqa[rows, :] += dq_r
                    dka[rows, :] += dk_r
                    dva[rows, :] += dv_r

        dq_ref[...] = dqa[...].astype(BF16)
        dk_ref[...] = dka[...].astype(BF16)
        dv_ref[...] = dva[...].astype(BF16)

    spec = lambda off: pl.BlockSpec((S, 128), lambda b, p: (b, 4 * off + p))
    ospec = pl.BlockSpec((S, 128), lambda b, p: (b, p))
    blocks = lambda n, dt: pltpu.VMEM((n, BLOCK, 128), dt)
    return pl.pallas_call(
        body, grid=(B, 4), in_specs=[_SMEM_SPEC, spec(0), spec(1), spec(2), ospec, ospec, ospec],
        out_specs=[ospec] * 3, out_shape=[SDS((B * S, WIDTH), BF16)] * 3,
        scratch_shapes=[pltpu.VMEM((S, 128), F32)] * 5
        + [blocks(NB, BF16), blocks(NB, BF16), blocks(NB + 1, BF16), blocks(NB + 1, BF16), blocks(NB, F32),
           blocks(NB + 1, F32), blocks(NB + 1, F32)] + [pltpu.VMEM((S, 128), F32)] * 3,
        compiler_params=_cp(), name=name,
    )(_alibi_slopes(), z, z, z, dy, ya, lse)


FOX_TQ = 256
FOX_TQ_FWD = 512


def _fox_fwd(z, cc, *, B, S, name):
    def body(q_ref, k_ref, v_ref, cc_ref, o_ref, l_ref, qa, ka):
        (m_first, m_second), first = _pair_masks()
        ccv = cc_ref[...]
        eighth = jnp.asarray(0.125, BF16)
        for e, hm in enumerate((m_first, m_second)):
            c_e = jnp.broadcast_to(ccv[:, HEAD_DIM * e:HEAD_DIM * e + 1], (S, 128))
            qa[e] = _with_spare_lanes(q_ref[...] * hm * eighth, e, _split3(c_e) + ONES3)
            ka[e] = _with_spare_lanes(k_ref[...] * hm, e, ONES3 + _split3(-c_e))
        for qi in range(S // FOX_TQ_FWD):
            r0, kend = qi * FOX_TQ_FWD, (qi + 1) * FOX_TQ_FWD
            vv = v_ref[0:kend, :]
            row = lax.broadcasted_iota(jnp.int32, (FOX_TQ_FWD, kend), 0) + r0
            col = lax.broadcasted_iota(jnp.int32, (FOX_TQ_FWD, kend), 1)
            causal = col <= row
            outs, lses = [], []
            for e in (0, 1):
                s = jnp.where(causal, _dot(qa[e, r0:kend, :], ka[e, 0:kend, :], NT), NEG)
                m = jnp.max(s, axis=1, keepdims=True)
                pe = jnp.exp(s - m)
                l = jnp.sum(pe, axis=1, keepdims=True)
                outs.append(_dot(pe.astype(BF16), vv) * (1.0 / l))
                lses.append(m + jnp.log(l))
            o_ref[r0:kend, :] = jnp.where(first, outs[0], outs[1]).astype(BF16)
            l_ref[r0:kend, :] = jnp.where(first, lses[0], lses[1])

    spec = lambda off: pl.BlockSpec((S, 128), lambda b, p: (b, 4 * off + p))
    pspec = pl.BlockSpec((S, 128), lambda b, p: (b, p))
    return pl.pallas_call(
        body, grid=(B, 4), in_specs=[spec(3), spec(4), spec(5), pspec], out_specs=[pspec, pspec],
        out_shape=[SDS((B * S, WIDTH), BF16), SDS((B * S, WIDTH), F32)],
        scratch_shapes=[pltpu.VMEM((2, S, 128), BF16)] * 2, compiler_params=_cp(), name=name,
    )(z, z, z, cc)


def _fox_bwd(z, dy, lse, cc, *, B, S, name):
    def body(q_ref, k_ref, v_ref, do_ref, lse_ref, cc_ref, dq_ref, dk_ref, dv_ref, dc_ref,
             qa, ka, qp, kp, vp, dp, dk_s, dv_s, dc_s):
        (m_first, m_second), first = _pair_masks()
        ccv, lsev = cc_ref[...], lse_ref[...]
        eighth = jnp.asarray(0.125, BF16)
        for e, hm in enumerate((m_first, m_second)):
            lane0 = slice(HEAD_DIM * e, HEAD_DIM * e + 1)
            c_e = jnp.broadcast_to(ccv[:, lane0], (S, 128))
            lse_e = jnp.broadcast_to(lsev[:, lane0], (S, 128))
            qp[e] = q_ref[...] * hm
            kp[e] = k_ref[...] * hm
            dp[e] = do_ref[...] * hm
            qa[e] = _with_spare_lanes(qp[e] * eighth, e, _split3(c_e - lse_e) + ONES3)
            ka[e] = _with_spare_lanes(kp[e], e, ONES3 + _split3(-c_e))
            vp[e] = v_ref[...] * hm
        dk_s[...] = jnp.zeros_like(dk_s)
        dv_s[...] = jnp.zeros_like(dv_s)
        dc_s[...] = jnp.zeros_like(dc_s)
        for qi in range(S // FOX_TQ):
            r0, kend = qi * FOX_TQ, (qi + 1) * FOX_TQ
            krow = lax.broadcasted_iota(jnp.int32, (kend, FOX_TQ), 0)
            qcol = lax.broadcasted_iota(jnp.int32, (kend, FOX_TQ), 1) + r0
            causal = krow <= qcol
            dq_t = jnp.zeros((FOX_TQ, 128), F32)
            for e in (0, 1):
                sel = first if e == 0 else ~first
                pt = jnp.where(causal, jnp.exp(_dot(ka[e, 0:kend, :], qa[e, r0:kend, :], NT)), 0.0)
                dpt = _dot(vp[e, 0:kend, :], dp[e, r0:kend, :], NT)
                mean = jnp.sum(pt * dpt, axis=0, keepdims=True) / jnp.sum(pt, axis=0, keepdims=True)
                dst = pt * (dpt - mean)
                dsb = dst.astype(BF16)
                dv_s[0:kend, :] += _dot(pt.astype(BF16), dp[e, r0:kend, :])
                dk_s[0:kend, :] += _dot(dsb, qp[e, r0:kend, :]) * 0.125
                dq_t = dq_t + _dot(dsb, kp[e, 0:kend, :], TN)
                dc_s[0:kend, :] += jnp.where(sel, -jnp.sum(dst, axis=1, keepdims=True), 0.0)
            dq_ref[r0:kend, :] = (dq_t * 0.125).astype(BF16)
        dk_ref[...] = dk_s[...].astype(BF16)
        dv_ref[...] = dv_s[...].astype(BF16)
        dc_ref[...] = dc_s[...]

    spec = lambda off: pl.BlockSpec((S, 128), lambda b, p: (b, 4 * off + p))
    pspec = pl.BlockSpec((S, 128), lambda b, p: (b, p))
    return pl.pallas_call(
        body, grid=(B, 4),
        in_specs=[spec(3), spec(4), spec(5), pl.BlockSpec((S, 128), lambda b, p: (b, 4 + p)), pspec, pspec],
        out_specs=[pspec] * 4,
        out_shape=[SDS((B * S, WIDTH), BF16)] * 3 + [SDS((B * S, WIDTH), F32)],
        scratch_shapes=[pltpu.VMEM((2, S, 128), BF16)] * 6 + [pltpu.VMEM((S, 128), F32)] * 3,
        compiler_params=_cp(), name=name,
    )(z, z, z, dy, lse, cc)


def _xattn_fwd(q, kv, *, B, S, M, tq, name):
    D = D_MODEL

    def body(q_ref, kv_ref, o_ref):
        for h in range(N_XH):
            cs = slice(XHD * h, XHD * (h + 1))
            s = _dot(q_ref[:, cs], kv_ref[:, cs], NT) * (1.0 / 16.0)
            pe = jnp.exp(s - jnp.max(s, axis=1, keepdims=True))
            l = jnp.sum(pe, axis=1, keepdims=True)
            o_ref[:, cs] = (_dot(pe.astype(BF16), kv_ref[:, D + XHD * h:D + XHD * (h + 1)]) * (1.0 / l)).astype(BF16)

    nq = S // tq
    return pl.pallas_call(
        body, grid=(B, nq),
        in_specs=[pl.BlockSpec((tq, D), lambda b, t: (b * nq + t, 0)), pl.BlockSpec((M, 2 * D), lambda b, t: (b, 0))],
        out_specs=pl.BlockSpec((tq, D), lambda b, t: (b * nq + t, 0)), out_shape=SDS((B * S, D), BF16),
        compiler_params=_cp(), name=name,
    )(q, kv)


def _xattn_bwd(q, kv, do, *, B, S, M, tq, name):
    D = D_MODEL

    def body(q_ref, kv_ref, do_ref, dq_ref, dkv_ref):
        t = pl.program_id(1)

        @pl.when(t == 0)
        def _():
            dkv_ref[...] = jnp.zeros_like(dkv_ref)

        for h in range(N_XH):
            cs = slice(XHD * h, XHD * (h + 1))
            vs = slice(D + XHD * h, D + XHD * (h + 1))
            qh, kh, vh, doh = q_ref[:, cs], kv_ref[:, cs], kv_ref[:, vs], do_ref[:, cs]
            s = _dot(qh, kh, NT) * (1.0 / 16.0)
            pe = jnp.exp(s - jnp.max(s, axis=1, keepdims=True))
            pe = pe * (1.0 / jnp.sum(pe, axis=1, keepdims=True))
            dp = _dot(doh, vh, NT)
            ds = (pe * (dp - jnp.sum(pe * dp, axis=1, keepdims=True))).astype(BF16)
            dq_ref[:, cs] = (_dot(ds, kh) * (1.0 / 16.0)).astype(BF16)
            dkv_ref[:, cs] += _dot(ds, qh, TN) * (1.0 / 16.0)
            dkv_ref[:, vs] += _dot(pe.astype(BF16), doh, TN)

    nq = S // tq
    qspec = pl.BlockSpec((tq, D), lambda b, t: (b * nq + t, 0))
    kvspec = pl.BlockSpec((M, 2 * D), lambda b, t: (b, 0))
    return pl.pallas_call(
        body, grid=(B, nq), in_specs=[qspec, kvspec, qspec], out_specs=[qspec, kvspec],
        out_shape=[SDS((B * S, D), BF16), SDS((B * M, 2 * D), F32)], compiler_params=_cp(), name=name,
    )(q, kv, do)


def _adamw(parts, w, m, v, *, tr, name):
    R, C = w.shape

    def body(p_ref, w_ref, m_ref, v_ref, g_ref, d_ref, nm_ref, nv_ref):
        g = p_ref[0].astype(F32)
        for d in range(1, N_DEV):
            g = g + p_ref[d].astype(F32)
        m2 = ADAM_B1 * m_ref[...] + (1.0 - ADAM_B1) * g
        v2 = ADAM_B2 * v_ref[...] + (1.0 - ADAM_B2) * (g * g)
        m_hat = m2 / (1.0 - ADAM_B1 ** ADAM_STEP)
        v_hat = v2 / (1.0 - ADAM_B2 ** ADAM_STEP)
        g_ref[...] = g
        d_ref[...] = -ADAM_LR * (m_hat / (jnp.sqrt(v_hat) + ADAM_EPS) + ADAM_WD * w_ref[...])
        nm_ref[...] = m2
        nv_ref[...] = v2

    spec = pl.BlockSpec((tr, C), lambda i: (i, 0))
    return pl.pallas_call(
        body, grid=(R // tr,), in_specs=[pl.BlockSpec((N_DEV, tr, C), lambda i: (0, i, 0)), spec, spec, spec],
        out_specs=[spec] * 4, out_shape=[SDS((R, C), F32)] * 4, compiler_params=_cp(), name=name,
    )(parts, w, m, v)


def _peer(k, x, y, c):
    return (1 - x if k & 4 else x, 1 - y if k & 2 else y, 1 - c if k & 1 else c)


_HBM_SPEC = pl.BlockSpec(memory_space=pltpu.HBM)
_SEM_SPEC = pl.BlockSpec(memory_space=pltpu.SEMAPHORE)
_SPLIT_EFFECT = pltpu.SideEffectType.DATAFLOW_SIDE_EFFECTING


def _split_copies(srcs, lands, send_sems, recv_sems, modes):
    x, y, c = (lax.axis_index(a) for a in AXES)
    me = 4 * x + 2 * y + c
    copies = []
    for i, md in enumerate(modes):
        for k in range(1, N_DEV):
            px, py, pc = _peer(k, x, y, c)
            src = srcs[i] if md == "gather" else srcs[i].at[4 * px + 2 * py + pc]
            j = i * (N_DEV - 1) + k - 1
            copies.append(pltpu.make_async_remote_copy(
                src_ref=src, dst_ref=lands[i].at[me], send_sem=send_sems.at[j], recv_sem=recv_sems.at[j],
                device_id=(px, py, pc), device_id_type=pl.DeviceIdType.MESH))
    return copies


def _exchange_start(arrays, modes, *, name):
    n = len(arrays)
    hbm = lambda a: pltpu.with_memory_space_constraint(a, pltpu.HBM)
    srcs = [hbm(a) for a in arrays]
    me = 4 * lax.axis_index("x") + 2 * lax.axis_index("y") + lax.axis_index("c")

    def landing(a, md):
        own = a[None] if md == "gather" else lax.dynamic_index_in_dim(a, me, 0, keepdims=True)
        return hbm(lax.dynamic_update_index_in_dim(lax.empty((N_DEV,) + own.shape[1:], a.dtype), own, me, 0))

    lands = [landing(a, md) for a, md in zip(arrays, modes)]

    def body(*refs):
        for cp in _split_copies(refs[:n], refs[n:2 * n], refs[2 * n], refs[2 * n + 1], modes):
            cp.start()
        token = refs[-1]
        token[...] = jnp.zeros_like(token)

    sems = pltpu.SemaphoreType.DMA((n * (N_DEV - 1),))
    outs = pl.pallas_call(
        body, name=name, in_specs=[_HBM_SPEC] * (2 * n),
        out_shape=(sems, sems, *[pltpu.HBM(a.shape, a.dtype) for a in srcs + lands], SDS((8, 128), F32)),
        out_specs=(_SEM_SPEC, _SEM_SPEC, *[_HBM_SPEC] * (2 * n), pl.BlockSpec(memory_space=pltpu.VMEM)),
        input_output_aliases={i: 2 + i for i in range(2 * n)},
        compiler_params=pltpu.CompilerParams(has_side_effects=_SPLIT_EFFECT),
    )(*srcs, *lands)
    return (outs[0], outs[1], outs[2:2 + n], outs[2 + n:2 + 2 * n], modes), outs[-1]


def _exchange_wait(handle, after, *, name):
    send_sems, recv_sems, srcs, lands, modes = handle
    n = len(srcs)

    def body(*refs):
        for cp in _split_copies(refs[:n], refs[n:2 * n], refs[2 * n], refs[2 * n + 1], modes):
            cp.wait_send()
            cp.wait_recv()

    outs = pl.pallas_call(
        body, name=name, in_specs=[_HBM_SPEC] * (2 * n) + [_SEM_SPEC, _SEM_SPEC, pl.BlockSpec(memory_space=pl.ANY)],
        out_shape=tuple(pltpu.HBM(a.shape, a.dtype) for a in list(srcs) + list(lands)), out_specs=tuple([_HBM_SPEC] * (2 * n)),
        input_output_aliases={i: i for i in range(2 * n)},
        compiler_params=pltpu.CompilerParams(has_side_effects=_SPLIT_EFFECT),
    )(*srcs, *lands, send_sems, recv_sems, after)
    return list(outs[n:])


def _exchange(arrays, modes, *, name):
    n = len(arrays)
    out_shape = [SDS((N_DEV,) + a.shape if md == "gather" else a.shape, a.dtype) for a, md in zip(arrays, modes)]

    def body(*refs):
        ins, outs = refs[:n], refs[n:2 * n]
        send_sems, recv_sems, local_sems = refs[2 * n:]
        x, y, c = (lax.axis_index(a) for a in AXES)
        me = 4 * x + 2 * y + c
        copies = []
        for i, md in enumerate(modes):
            src = ins[i] if md == "gather" else ins[i].at[me]
            cp = pltpu.make_async_copy(src, outs[i].at[me], local_sems.at[i])
            cp.start()
            copies.append(cp)
            for k in range(1, N_DEV):
                px, py, pc = _peer(k, x, y, c)
                src = ins[i] if md == "gather" else ins[i].at[4 * px + 2 * py + pc]
                cp = pltpu.make_async_remote_copy(
                    src_ref=src, dst_ref=outs[i].at[me], send_sem=send_sems.at[i, k - 1], recv_sem=recv_sems.at[i, k - 1],
                    device_id=(px, py, pc), device_id_type=pl.DeviceIdType.MESH)
                cp.start()
                copies.append(cp)
        for cp in copies:
            cp.wait()

    anyspec = pl.BlockSpec(memory_space=pl.ANY)
    return pl.pallas_call(
        body, in_specs=[anyspec] * n, out_specs=[anyspec] * n, out_shape=out_shape,
        scratch_shapes=[pltpu.SemaphoreType.DMA((n, N_DEV - 1)), pltpu.SemaphoreType.DMA((n, N_DEV - 1)),
                        pltpu.SemaphoreType.DMA((n,))],
        name=name,
    )(*arrays)


def _local_step(x, mem, g_mix, b_forget, g_xattn, g_mem, g_mlp, g_final, target, get_w_in, get_rest, send):
    B, S, D = x.shape
    M = mem.shape[1]
    T = B * S
    x0 = x.reshape(T, D)
    mem2 = mem.reshape(B * M, D)
    tgt = target.reshape(T, D)
    b_pad = jnp.pad(b_forget, (0, 120)).reshape(1, 128)
    after = lambda a, tok: a if tok is None else a + tok[0, 0]

    h1 = _rms(x0, g_mix, tm=1024, name="f_norm")
    w_in_pad = get_w_in(h1)
    _, z, gate = _rms_matmul(h1, g_mix, w_in_pad[:, :QKV_W], tm=ROWS, tn=QKV_W, out_dtype=BF16,
                             w_f32=w_in_pad[:, QKV_W:], normed=True, name="f_in")
    cc = _gate_fwd(gate, b_pad, B=B, S=S, name="f_gatecum")
    ya, lse = _dil_attn_fwd(z, B=B, S=S, name="f_dil")
    yf, lse_f = _fox_fwd(z, cc, B=B, S=S, name="f_fox")
    ymix = jnp.concatenate([ya, yf], axis=1)
    w = get_rest(ymix)
    x1, h2, q = _res_rms_matmul(ymix, w["w_out"], x0, g_xattn, w["w_xq"], tm=ROWS, name="f_out")
    mn, kv = _rms_matmul(mem2, g_mem, w["w_kv"], tm=B * M, tn=D, out_dtype=BF16, name="f_xkv")
    xo = _xattn_fwd(q, kv, B=B, S=S, M=M, tq=1024, name="f_xattn")
    x2, h3, act = _res_rms_matmul(xo, w["w_xo"], x1, g_mlp, w["w_up"], tm=ROWS, relu=True, name="f_xo")
    dx3, dx3_bf, dg_final, loss = _down_loss(act, w["w_down"], x2, g_final, tgt, tm=ROWS, name="f_down")

    du = _matmul_nt(dx3_bf, w["w_down"], mul2a=act, tm=ROWS, tn=D_FF, name="b_dact")
    dw_down = _matmul_tn(act, dx3_bf, square=True, bk=1024, bn=D, tt=min(T, 2 * ACC_ROWS), out_dtype=BF16, name="b_wdown")
    dw_up = _matmul_tn(h3, du, bk=D, bn=1024, tt=min(T, 2 * ACC_ROWS), out_dtype=BF16, name="b_wup")
    tok = send(dict(w_down=dw_down, w_up=dw_up))
    dx2, dg_mlp, dxo = _matmul_nt_rms(du, w["w_up"], x2, after(g_mlp, tok), dx3, then_w=w["w_xo"], tm=ROWS, tk=D_FF,
                                      name="b_dh3")
    dw_xo = _matmul_tn(xo, dx2, bk=D, bn=D, tt=ACC_ROWS, out_dtype=BF16, name="b_wxo")
    dq, dkv = _xattn_bwd(q, kv, dxo, B=B, S=S, M=M, tq=1024, name="b_xattn")
    dw_xq = _matmul_tn(h2, dq, bk=D, bn=D, tt=min(T, 2 * ACC_ROWS), out_dtype=BF16, name="b_wxq")
    dx1, dg_xattn, dy = _matmul_nt_rms(dq, w["w_xq"], x1, g_xattn, dx2, then_w=w["w_out"], tm=ROWS, tk=D, name="b_dh2")
    dw_kv = _matmul_tn(mn, dkv, bk=D, bn=D, tt=B * M, out_dtype=BF16, name="b_wkv")
    _, dg_mem = _matmul_nt_rms(dkv, w["w_kv"], mem2, g_mem, None, tm=min(ROWS, B * M), tk=2 * D, name="b_dmem")
    dw_out = _matmul_tn(ymix, dx1, bk=D, bn=D, tt=ACC_ROWS, out_dtype=BF16, name="b_wout")
    tok = send(dict(w_xo=dw_xo, w_xq=dw_xq, w_xk=dw_kv[:, :D], w_xv=dw_kv[:, D:], w_out=dw_out))
    dqf, dkf, dvf, dcc = _fox_bwd(z, dy, lse_f, cc, B=B, S=S, name="b_fox")
    dgate, db = _gate_bwd(dcc, gate, after(b_pad, tok), B=B, S=S, name="b_gate")
    dqa, dka, dva = _dil_attn_bwd(z, dy, ya, lse, B=B, S=S, name="b_dil")
    dz = [dqa, dka, dva, dqf, dkf, dvf, dgate]
    dw_in = jnp.concatenate([_matmul_tn_pieces(h1, dz[:3], tt=ACC_ROWS, name="b_win_dil"),
                             _matmul_tn_pieces(h1, dz[3:], tt=ACC_ROWS, name="b_win_fox")], axis=1)
    tok = send(dict(w_in=dw_in))
    gx, dg_mix = _matmul_nt_rms(dz, w_in_pad, x0, after(g_mix, tok), dx1, tm=ROWS, tk=IN_PAD, name="b_dh1")

    small = dict(g_mix=dg_mix, b_forget=db, g_xattn=dg_xattn, g_mem=dg_mem, g_mlp=dg_mlp, g_final=dg_final)
    return gx.reshape(B, S, D), small, loss


SMALL_ROWS = ("g_mix", "b_forget", "g_xattn", "g_mem", "g_mlp", "g_final")
COL_SHARDED = ("w_in", "w_up")


def _pack_rows(rows):
    D = D_MODEL
    rows = [jnp.pad(r.reshape(-1), (0, D - r.size)) for r in rows]
    rows += [jnp.zeros((D,), F32)] * (8 - len(rows))
    return jnp.stack(rows)


def _full(name, g):
    if name in COL_SHARDED:
        return g.transpose(1, 0, 2).reshape(g.shape[1], -1)
    return g.reshape(-1, g.shape[2])


def _blocks(name, g, shard_shape):
    if name in COL_SHARDED:
        n = shard_shape[1]
        return g[:, :n * N_DEV].reshape(g.shape[0], N_DEV, n).transpose(1, 0, 2)
    return g.reshape((N_DEV,) + shard_shape)


def kernel(x, mem, g_mix, w_in, b_forget, w_out, g_xattn, g_mem, w_xq, w_xk, w_xv, w_xo, g_mlp, w_up, w_down, g_final, loss_target, m_g_mix, m_w_in, m_b_forget, m_w_out, m_g_xattn, m_g_mem, m_w_xq, m_w_xk, m_w_xv, m_w_xo, m_g_mlp, m_w_up, m_w_down, m_g_final, v_g_mix, v_w_in, v_b_forget, v_w_out, v_g_xattn, v_g_mem, v_w_xq, v_w_xk, v_w_xv, v_w_xo, v_g_mlp, v_w_up, v_w_down, v_g_final):
    W = dict(w_in=w_in, w_out=w_out, w_xq=w_xq, w_xk=w_xk, w_xv=w_xv, w_xo=w_xo, w_up=w_up, w_down=w_down)
    Mo = dict(w_in=m_w_in, w_out=m_w_out, w_xq=m_w_xq, w_xk=m_w_xk, w_xv=m_w_xv, w_xo=m_w_xo, w_up=m_w_up, w_down=m_w_down)
    Vo = dict(w_in=v_w_in, w_out=v_w_out, w_xq=v_w_xq, w_xk=v_w_xk, w_xv=v_w_xv, w_xo=v_w_xo, w_up=v_w_up, w_down=v_w_down)
    later = [n for n in W if n != "w_in"]

    first_handle, first_token = _exchange_start([w_in.astype(BF16)], ["gather"], name="gather_in_start")
    rest_handle, rest_token = _exchange_start([W[n].astype(BF16) + first_token[0, 0].astype(BF16) for n in later],
                                              ["gather"] * len(later), name="gather_rest_start")

    def get_w_in(after):
        (g,) = _exchange_wait(first_handle, after, name="gather_in_wait")
        return jnp.pad(_full("w_in", g), ((0, 0), (0, IN_PAD - IN_W)))

    def get_rest(after):
        full = {n: _full(n, g) for n, g in zip(later, _exchange_wait(rest_handle, after, name="gather_rest_wait"))}
        full["w_kv"] = jnp.concatenate([full.pop("w_xk"), full.pop("w_xv")], axis=1)
        return full

    sent = []

    def send(grads):
        names = list(grads)
        handle, token = _exchange_start([_blocks(n, grads[n], W[n].shape) for n in names], ["scatter"] * len(names),
                                        name=f"scatter{len(sent)}_start")
        sent.append((names, handle))
        return token

    gx, small, loss = _local_step(x, mem, g_mix + rest_token[0, 0], b_forget, g_xattn, g_mem, g_mlp, g_final, loss_target,
                                  get_w_in, get_rest, send)

    received = {}
    for i, (names, handle) in enumerate(sent):
        received.update(zip(names, _exchange_wait(handle, gx, name=f"scatter{i}_wait")))
    packed = _pack_rows([small[n] for n in SMALL_ROWS] + [loss[0, :1]])
    (packed_all,) = _exchange([packed], ["gather"], name="gather_small")

    rows_per_step = lambda shape: max(t for t in (128, 256, 512) if shape[0] % t == 0 and t * shape[1] <= 512 * 512)
    res = {n: _adamw(received[n], W[n], Mo[n], Vo[n], tr=rows_per_step(W[n].shape), name=f"adamw_{n}") for n in W}
    small_w = dict(g_mix=g_mix, b_forget=b_forget, g_xattn=g_xattn, g_mem=g_mem, g_mlp=g_mlp, g_final=g_final)
    small_m = dict(g_mix=m_g_mix, b_forget=m_b_forget, g_xattn=m_g_xattn, g_mem=m_g_mem, g_mlp=m_g_mlp, g_final=m_g_final)
    small_v = dict(g_mix=v_g_mix, b_forget=v_b_forget, g_xattn=v_g_xattn, g_mem=v_g_mem, g_mlp=v_g_mlp, g_final=v_g_final)
    sres = _adamw(packed_all, _pack_rows([small_w[n] for n in SMALL_ROWS]), _pack_rows([small_m[n] for n in SMALL_ROWS]),
                  _pack_rows([small_v[n] for n in SMALL_ROWS]), tr=8, name="adamw_small")
    for i, n in enumerate(SMALL_ROWS):
        res[n] = [r[i, :small_w[n].size] for r in sres]
    loss_total = sres[0][6, 0]

    order = ["g_mix", "w_in", "b_forget", "w_out", "g_xattn", "g_mem", "w_xq", "w_xk", "w_xv", "w_xo", "g_mlp", "w_up", "w_down", "g_final"]
    return (loss_total, gx, *[res[n][0] for n in order], *[res[n][1] for n in order],
            *[res[n][2] for n in order], *[res[n][3] for n in order])
```
